```python
import math
import jax, jax.numpy as jnp
from jax import lax
import numpy as np

D_MODEL = 1024
BATCH = 8
SEQ = 8192
DEPTH = 2

CHUNK = 64
QBLOCK = 128
EPS = 1e-6
NEG_INF = -1e30

LRU_WIDTH = 512
LRU_HEADS = 8
LRU_HEAD_DIM = LRU_WIDTH // LRU_HEADS
CONV_WIDTH = 4
LRU_C = 8.0

MLA_HEADS = 8
MLA_Q_LORA = 384
MLA_KV_LORA = 256
MLA_NOPE = 64
MLA_ROPE = 32
MLA_V = 64
ROPE_BASE = 10000.0

FOX_HEADS = 8
FOX_HEAD_DIM = 64
FOX_WIDTH = FOX_HEADS * FOX_HEAD_DIM

N_BRANCH = 3
D_FF = ((8 * D_MODEL // 3 + 255) // 256) * 256
PLE_DIM = 256

SPLIT_SIZES = (
    LRU_WIDTH,
    LRU_WIDTH,
    MLA_Q_LORA,
    MLA_KV_LORA + MLA_ROPE,
    FOX_WIDTH,
    FOX_WIDTH,
    FOX_WIDTH,
    FOX_HEADS,
    N_BRANCH * D_MODEL,
)
D_IN = 2 * LRU_WIDTH + MLA_Q_LORA + MLA_KV_LORA + MLA_ROPE + 3 * FOX_WIDTH + FOX_HEADS + N_BRANCH * D_MODEL

kernel_name = "hybrid_gated_rglru_mla_fox_encoder"


def rmsnorm(x, g):
    xf = x.astype(jnp.float32)
    y = xf * lax.rsqrt(jnp.mean(xf * xf, axis=-1, keepdims=True) + EPS)
    return (y * g.astype(jnp.float32)).astype(x.dtype)


def split_columns(z):
    idx = []
    acc = 0
    for s in SPLIT_SIZES[:-1]:
        acc += s
        idx.append(acc)
    return jnp.split(z, idx, axis=-1)


def rope(x, cos, sin):
    half = x.shape[-1] // 2
    x1, x2 = x[..., :half], x[..., half:]
    c = cos[None, :, None, :].astype(x.dtype)
    s = sin[None, :, None, :].astype(x.dtype)
    return jnp.concatenate([x1 * c - x2 * s, x2 * c + x1 * s], axis=-1)


def block_attention(q, k, v, scale, unit, decay=None):
    B, S, H, Dk = q.shape
    nb = S // QBLOCK
    q_blocks = q.reshape(B, nb, QBLOCK, H, Dk).transpose(1, 0, 2, 3, 4)
    key_unit = jnp.arange(S) // unit
    decay_t = None if decay is None else decay.transpose(0, 2, 1)

    def one_block(args):
        ib, q_blk = args
        s = jnp.einsum('bqhd,bkhd->bhqk', q_blk, k, preferred_element_type=jnp.float32) * scale
        if decay_t is not None:
            dq = lax.dynamic_slice_in_dim(decay_t, ib * QBLOCK, QBLOCK, axis=2)
            s = s + dq[:, :, :, None] - decay_t[:, :, None, :]
        q_unit = (ib * QBLOCK + jnp.arange(QBLOCK)) // unit
        mask = q_unit[:, None] >= key_unit[None, :]
        s = jnp.where(mask[None, None], s, NEG_INF)
        pr = jax.nn.softmax(s, axis=-1)
        return jnp.einsum('bhqk,bkhd->bqhd', pr.astype(v.dtype), v)

    out = lax.map(one_block, (jnp.arange(nb), q_blocks))
    return out.transpose(1, 0, 2, 3, 4).reshape(B, S, H, v.shape[-1])


def _lru_combine(left, right):
    a1, b1 = left
    a2, b2 = right
    return a1 * a2, a2 * b1 + b2


def rglru_branch(u, u_gate, conv_w, conv_b, wa, ba, wx, bx, lam):
    B, S, W = u.shape
    up = jnp.pad(u, ((0, 0), (CONV_WIDTH - 1, 0), (0, 0)))
    xc = conv_b + up[:, 0:S] * conv_w[0]
    for kk in range(1, CONV_WIDTH):
        xc = xc + up[:, kk:kk + S] * conv_w[kk]
    xh = xc.reshape(B, S, LRU_HEADS, LRU_HEAD_DIM)
    r = jax.nn.sigmoid(jnp.einsum('bshi,hij->bshj', xh, wa).reshape(B, S, W) + ba)
    ig = jax.nn.sigmoid(jnp.einsum('bshi,hij->bshj', xh, wx).reshape(B, S, W) + bx)
    log_a = -LRU_C * r.astype(jnp.float32) * jax.nn.softplus(-lam.astype(jnp.float32))
    a = jnp.exp(log_a)
    b = jnp.sqrt(-jnp.expm1(2.0 * log_a)) * (ig * xc).astype(jnp.float32)
    _, h = lax.associative_scan(_lru_combine, (a, b), axis=1)
    return h.astype(u.dtype) * jax.nn.gelu(u_gate)


def mla_branch(c_q, ckv_rope, q_norm, wuq, kv_norm, wukv, cos, sin):
    B, S, _ = c_q.shape
    q = (rmsnorm(c_q, q_norm) @ wuq).reshape(B, S, MLA_HEADS, MLA_NOPE + MLA_ROPE)
    q_nope, q_rope = q[..., :MLA_NOPE], q[..., MLA_NOPE:]
    c_kv, k_rope = ckv_rope[..., :MLA_KV_LORA], ckv_rope[..., MLA_KV_LORA:]
    kv = (rmsnorm(c_kv, kv_norm) @ wukv).reshape(B, S, MLA_HEADS, MLA_NOPE + MLA_V)
    k_nope, v = kv[..., :MLA_NOPE], kv[..., MLA_NOPE:]
    q_rope = rope(q_rope, cos, sin)
    k_rope = rope(k_rope[:, :, None, :], cos, sin)
    q_full = jnp.concatenate([q_nope, q_rope], axis=-1)
    k_full = jnp.concatenate([k_nope, jnp.broadcast_to(k_rope, (B, S, MLA_HEADS, MLA_ROPE))], axis=-1)
    o = block_attention(q_full, k_full, v, (MLA_NOPE + MLA_ROPE) ** -0.5, CHUNK)
    return o.reshape(B, S, MLA_HEADS * MLA_V)


def fox_branch(fq, fk, fv, f_logit, bf):
    B, S, _ = fq.shape
    q = fq.reshape(B, S, FOX_HEADS, FOX_HEAD_DIM)
    k = fk.reshape(B, S, FOX_HEADS, FOX_HEAD_DIM)
    v = fv.reshape(B, S, FOX_HEADS, FOX_HEAD_DIM)
    log_f = jax.nn.log_sigmoid((f_logit + bf).astype(jnp.float32))
    cum = jnp.cumsum(log_f, axis=1)
    o = block_attention(q, k, v, FOX_HEAD_DIM ** -0.5, 1, decay=cum)
    return o.reshape(B, S, FOX_WIDTH)


def _fwd_setup_inputs(seed: int = 0) -> dict:
    key = jax.random.key(seed)
    ks = jax.random.split(key, 32)

    def nrm(k, shape, scale):
        return jax.random.normal(k, shape, jnp.float32) * scale

    def gain(k, shape):
        return 1.0 + 0.05 * jax.random.normal(k, shape, jnp.float32)

    u = jax.random.uniform(ks[10], (DEPTH, LRU_WIDTH), jnp.float32, 0.9, 0.999)
    a = u ** (1.0 / LRU_C)
    lru_lambda = jnp.log(a) - jnp.log1p(-a)

    return {
        "x": nrm(ks[0], (BATCH, SEQ, D_MODEL), 1.0),
        "p": nrm(ks[1], (DEPTH, BATCH, SEQ, PLE_DIM), 1.0),
        "mix_norm": gain(ks[2], (DEPTH, D_MODEL)),
        "w_in": nrm(ks[3], (DEPTH, D_MODEL, D_IN), D_MODEL ** -0.5),
        "gate_b": nrm(ks[4], (DEPTH, N_BRANCH * D_MODEL), 0.1),
        "conv_w": nrm(ks[5], (DEPTH, CONV_WIDTH, LRU_WIDTH), CONV_WIDTH ** -0.5),
        "conv_b": nrm(ks[6], (DEPTH, LRU_WIDTH), 0.1),
        "lru_wa": nrm(ks[7], (DEPTH, LRU_HEADS, LRU_HEAD_DIM, LRU_HEAD_DIM), LRU_HEAD_DIM ** -0.5),
        "lru_ba": nrm(ks[8], (DEPTH, LRU_WIDTH), 0.1),
        "lru_wx": nrm(ks[9], (DEPTH, LRU_HEADS, LRU_HEAD_DIM, LRU_HEAD_DIM), LRU_HEAD_DIM ** -0.5),
        "lru_bx": nrm(ks[11], (DEPTH, LRU_WIDTH), 0.1),
        "lru_lambda": lru_lambda,
        "mla_q_norm": gain(ks[12], (DEPTH, MLA_Q_LORA)),
        "mla_wuq": nrm(ks[13], (DEPTH, MLA_Q_LORA, MLA_HEADS * (MLA_NOPE + MLA_ROPE)), MLA_Q_LORA ** -0.5),
        "mla_kv_norm": gain(ks[14], (DEPTH, MLA_KV_LORA)),
        "mla_wukv": nrm(ks[15], (DEPTH, MLA_KV_LORA, MLA_HEADS * (MLA_NOPE + MLA_V)), MLA_KV_LORA ** -0.5),
        "fox_bf": jax.random.uniform(ks[16], (DEPTH, FOX_HEADS), jnp.float32, 1.0, 5.0),
        "w_br_a": nrm(ks[17], (DEPTH, LRU_WIDTH, D_MODEL), LRU_WIDTH ** -0.5),
        "w_br_b": nrm(ks[18], (DEPTH, MLA_HEADS * MLA_V, D_MODEL), (MLA_HEADS * MLA_V) ** -0.5),
        "w_br_c": nrm(ks[19], (DEPTH, FOX_WIDTH, D_MODEL), FOX_WIDTH ** -0.5),
        "w_o": nrm(ks[20], (DEPTH, D_MODEL, D_MODEL), D_MODEL ** -0.5),
        "ffn_norm": gain(ks[21], (DEPTH, D_MODEL)),
        "w_gate_up": nrm(ks[22], (DEPTH, D_MODEL, 2 * D_FF), D_MODEL ** -0.5),
        "w_down": nrm(ks[23], (DEPTH, D_FF, D_MODEL), D_FF ** -0.5),
        "ple_norm": gain(ks[24], (DEPTH, D_MODEL)),
        "w_ple_gate": nrm(ks[25], (DEPTH, D_MODEL, D_MODEL), D_MODEL ** -0.5),
        "w_ple": nrm(ks[26], (DEPTH, PLE_DIM, D_MODEL), PLE_DIM ** -0.5),
        "final_norm": gain(ks[27], (D_MODEL,)),
    }


def _fwd_reference(x, p, mix_norm, w_in, gate_b, conv_w, conv_b, lru_wa, lru_ba, lru_wx, lru_bx, lru_lambda,
              mla_q_norm, mla_wuq, mla_kv_norm, mla_wukv, fox_bf, w_br_a, w_br_b, w_br_c, w_o,
              ffn_norm, w_gate_up, w_down, ple_norm, w_ple_gate, w_ple, final_norm):
    B, S, D = x.shape
    pos = jnp.arange(S, dtype=jnp.float32)
    inv_freq = ROPE_BASE ** (-jnp.arange(0, MLA_ROPE, 2, dtype=jnp.float32) / MLA_ROPE)
    ang = pos[:, None] * inv_freq[None, :]
    cos, sin = jnp.cos(ang), jnp.sin(ang)

    for i in range(DEPTH):
        h = rmsnorm(x, mix_norm[i])
        z = h @ w_in[i]
        u_rnn, u_gelu, c_q, ckv_rope, fq, fk, fv, f_logit, gate_logit = split_columns(z)
        y_a = rglru_branch(u_rnn, u_gelu, conv_w[i], conv_b[i], lru_wa[i], lru_ba[i],
                           lru_wx[i], lru_bx[i], lru_lambda[i]) @ w_br_a[i]
        y_b = mla_branch(c_q, ckv_rope, mla_q_norm[i], mla_wuq[i], mla_kv_norm[i],
                         mla_wukv[i], cos, sin) @ w_br_b[i]
        y_c = fox_branch(fq, fk, fv, f_logit, fox_bf[i]) @ w_br_c[i]
        g = jax.nn.sigmoid(gate_logit + gate_b[i]).reshape(B, S, N_BRANCH, D)
        merged = g[:, :, 0] * y_a + g[:, :, 1] * y_b + g[:, :, 2] * y_c
        x = x + merged @ w_o[i]
        hf = rmsnorm(x, ffn_norm[i]) @ w_gate_up[i]
        x = x + (jax.nn.silu(hf[..., :D_FF]) * hf[..., D_FF:]) @ w_down[i]
        pg = jax.nn.sigmoid(rmsnorm(x, ple_norm[i]) @ w_ple_gate[i])
        x = x + pg * (p[i] @ w_ple[i])
    return rmsnorm(x, final_norm)


import jax as _jax
import jax.numpy as _jnp

TWIN_FORMAT = 'train_step'
FWD_PARAMS = ['x', 'p', 'mix_norm', 'w_in', 'gate_b', 'conv_w', 'conv_b', 'lru_wa', 'lru_ba', 'lru_wx', 'lru_bx', 'lru_lambda', 'mla_q_norm', 'mla_wuq', 'mla_kv_norm', 'mla_wukv', 'fox_bf', 'w_br_a', 'w_br_b', 'w_br_c', 'w_o', 'ffn_norm', 'w_gate_up', 'w_down', 'ple_norm', 'w_ple_gate', 'w_ple', 'final_norm']
TWIN_WEIGHTS = ['mix_norm', 'w_in', 'gate_b', 'conv_w', 'conv_b', 'lru_wa', 'lru_ba', 'lru_wx', 'lru_bx', 'lru_lambda', 'mla_q_norm', 'mla_wuq', 'mla_kv_norm', 'mla_wukv', 'fox_bf', 'w_br_a', 'w_br_b', 'w_br_c', 'w_o', 'ffn_norm', 'w_gate_up', 'w_down', 'ple_norm', 'w_ple_gate', 'w_ple', 'final_norm']
TWIN_DIFF_INPUT = 'x'
TWIN_INPUTS = ['x', 'p', 'mix_norm', 'w_in', 'gate_b', 'conv_w', 'conv_b', 'lru_wa', 'lru_ba', 'lru_wx', 'lru_bx', 'lru_lambda', 'mla_q_norm', 'mla_wuq', 'mla_kv_norm', 'mla_wukv', 'fox_bf', 'w_br_a', 'w_br_b', 'w_br_c', 'w_o', 'ffn_norm', 'w_gate_up', 'w_down', 'ple_norm', 'w_ple_gate', 'w_ple', 'final_norm', 'loss_target', 'm_mix_norm', 'm_w_in', 'm_gate_b', 'm_conv_w', 'm_conv_b', 'm_lru_wa', 'm_lru_ba', 'm_lru_wx', 'm_lru_bx', 'm_lru_lambda', 'm_mla_q_norm', 'm_mla_wuq', 'm_mla_kv_norm', 'm_mla_wukv', 'm_fox_bf', 'm_w_br_a', 'm_w_br_b', 'm_w_br_c', 'm_w_o', 'm_ffn_norm', 'm_w_gate_up', 'm_w_down', 'm_ple_norm', 'm_w_ple_gate', 'm_w_ple', 'm_final_norm', 'v_mix_norm', 'v_w_in', 'v_gate_b', 'v_conv_w', 'v_conv_b', 'v_lru_wa', 'v_lru_ba', 'v_lru_wx', 'v_lru_bx', 'v_lru_lambda', 'v_mla_q_norm', 'v_mla_wuq', 'v_mla_kv_norm', 'v_mla_wukv', 'v_fox_bf', 'v_w_br_a', 'v_w_br_b', 'v_w_br_c', 'v_w_o', 'v_ffn_norm', 'v_w_gate_up', 'v_w_down', 'v_ple_norm', 'v_w_ple_gate', 'v_w_ple', 'v_final_norm']
TWIN_OUTPUTS = ['loss', 'grad_x', 'grad_mix_norm', 'grad_w_in', 'grad_gate_b', 'grad_conv_w', 'grad_conv_b', 'grad_lru_wa', 'grad_lru_ba', 'grad_lru_wx', 'grad_lru_bx', 'grad_lru_lambda', 'grad_mla_q_norm', 'grad_mla_wuq', 'grad_mla_kv_norm', 'grad_mla_wukv', 'grad_fox_bf', 'grad_w_br_a', 'grad_w_br_b', 'grad_w_br_c', 'grad_w_o', 'grad_ffn_norm', 'grad_w_gate_up', 'grad_w_down', 'grad_ple_norm', 'grad_w_ple_gate', 'grad_w_ple', 'grad_final_norm', 'delta_mix_norm', 'delta_w_in', 'delta_gate_b', 'delta_conv_w', 'delta_conv_b', 'delta_lru_wa', 'delta_lru_ba', 'delta_lru_wx', 'delta_lru_bx', 'delta_lru_lambda', 'delta_mla_q_norm', 'delta_mla_wuq', 'delta_mla_kv_norm', 'delta_mla_wukv', 'delta_fox_bf', 'delta_w_br_a', 'delta_w_br_b', 'delta_w_br_c', 'delta_w_o', 'delta_ffn_norm', 'delta_w_gate_up', 'delta_w_down', 'delta_ple_norm', 'delta_w_ple_gate', 'delta_w_ple', 'delta_final_norm', 'new_m_mix_norm', 'new_m_w_in', 'new_m_gate_b', 'new_m_conv_w', 'new_m_conv_b', 'new_m_lru_wa', 'new_m_lru_ba', 'new_m_lru_wx', 'new_m_lru_bx', 'new_m_lru_lambda', 'new_m_mla_q_norm', 'new_m_mla_wuq', 'new_m_mla_kv_norm', 'new_m_mla_wukv', 'new_m_fox_bf', 'new_m_w_br_a', 'new_m_w_br_b', 'new_m_w_br_c', 'new_m_w_o', 'new_m_ffn_norm', 'new_m_w_gate_up', 'new_m_w_down', 'new_m_ple_norm', 'new_m_w_ple_gate', 'new_m_w_ple', 'new_m_final_norm', 'new_v_mix_norm', 'new_v_w_in', 'new_v_gate_b', 'new_v_conv_w', 'new_v_conv_b', 'new_v_lru_wa', 'new_v_lru_ba', 'new_v_lru_wx', 'new_v_lru_bx', 'new_v_lru_lambda', 'new_v_mla_q_norm', 'new_v_mla_wuq', 'new_v_mla_kv_norm', 'new_v_mla_wukv', 'new_v_fox_bf', 'new_v_w_br_a', 'new_v_w_br_b', 'new_v_w_br_c', 'new_v_w_o', 'new_v_ffn_norm', 'new_v_w_gate_up', 'new_v_w_down', 'new_v_ple_norm', 'new_v_w_ple_gate', 'new_v_w_ple', 'new_v_final_norm']
TWIN_LEAF_KINDS = {'loss': 'loss', 'grad_x': 'grad_x', 'grad_mix_norm': 'grad_w', 'grad_w_in': 'grad_w', 'grad_gate_b': 'grad_w', 'grad_conv_w': 'grad_w', 'grad_conv_b': 'grad_w', 'grad_lru_wa': 'grad_w', 'grad_lru_ba': 'grad_w', 'grad_lru_wx': 'grad_w', 'grad_lru_bx': 'grad_w', 'grad_lru_lambda': 'grad_w', 'grad_mla_q_norm': 'grad_w', 'grad_mla_wuq': 'grad_w', 'grad_mla_kv_norm': 'grad_w', 'grad_mla_wukv': 'grad_w', 'grad_fox_bf': 'grad_w', 'grad_w_br_a': 'grad_w', 'grad_w_br_b': 'grad_w', 'grad_w_br_c': 'grad_w', 'grad_w_o': 'grad_w', 'grad_ffn_norm': 'grad_w', 'grad_w_gate_up': 'grad_w', 'grad_w_down': 'grad_w', 'grad_ple_norm': 'grad_w', 'grad_w_ple_gate': 'grad_w', 'grad_w_ple': 'grad_w', 'grad_final_norm': 'grad_w', 'delta_mix_norm': 'delta_w', 'delta_w_in': 'delta_w', 'delta_gate_b': 'delta_w', 'delta_conv_w': 'delta_w', 'delta_conv_b': 'delta_w', 'delta_lru_wa': 'delta_w', 'delta_lru_ba': 'delta_w', 'delta_lru_wx': 'delta_w', 'delta_lru_bx': 'delta_w', 'delta_lru_lambda': 'delta_w', 'delta_mla_q_norm': 'delta_w', 'delta_mla_wuq': 'delta_w', 'delta_mla_kv_norm': 'delta_w', 'delta_mla_wukv': 'delta_w', 'delta_fox_bf': 'delta_w', 'delta_w_br_a': 'delta_w', 'delta_w_br_b': 'delta_w', 'delta_w_br_c': 'delta_w', 'delta_w_o': 'delta_w', 'delta_ffn_norm': 'delta_w', 'delta_w_gate_up': 'delta_w', 'delta_w_down': 'delta_w', 'delta_ple_norm': 'delta_w', 'delta_w_ple_gate': 'delta_w', 'delta_w_ple': 'delta_w', 'delta_final_norm': 'delta_w', 'new_m_mix_norm': 'new_m', 'new_m_w_in': 'new_m', 'new_m_gate_b': 'new_m', 'new_m_conv_w': 'new_m', 'new_m_conv_b': 'new_m', 'new_m_lru_wa': 'new_m', 'new_m_lru_ba': 'new_m', 'new_m_lru_wx': 'new_m', 'new_m_lru_bx': 'new_m', 'new_m_lru_lambda': 'new_m', 'new_m_mla_q_norm': 'new_m', 'new_m_mla_wuq': 'new_m', 'new_m_mla_kv_norm': 'new_m', 'new_m_mla_wukv': 'new_m', 'new_m_fox_bf': 'new_m', 'new_m_w_br_a': 'new_m', 'new_m_w_br_b': 'new_m', 'new_m_w_br_c': 'new_m', 'new_m_w_o': 'new_m', 'new_m_ffn_norm': 'new_m', 'new_m_w_gate_up': 'new_m', 'new_m_w_down': 'new_m', 'new_m_ple_norm': 'new_m', 'new_m_w_ple_gate': 'new_m', 'new_m_w_ple': 'new_m', 'new_m_final_norm': 'new_m', 'new_v_mix_norm': 'new_v', 'new_v_w_in': 'new_v', 'new_v_gate_b': 'new_v', 'new_v_conv_w': 'new_v', 'new_v_conv_b': 'new_v', 'new_v_lru_wa': 'new_v', 'new_v_lru_ba': 'new_v', 'new_v_lru_wx': 'new_v', 'new_v_lru_bx': 'new_v', 'new_v_lru_lambda': 'new_v', 'new_v_mla_q_norm': 'new_v', 'new_v_mla_wuq': 'new_v', 'new_v_mla_kv_norm': 'new_v', 'new_v_mla_wukv': 'new_v', 'new_v_fox_bf': 'new_v', 'new_v_w_br_a': 'new_v', 'new_v_w_br_b': 'new_v', 'new_v_w_br_c': 'new_v', 'new_v_w_o': 'new_v', 'new_v_ffn_norm': 'new_v', 'new_v_w_gate_up': 'new_v', 'new_v_w_down': 'new_v', 'new_v_ple_norm': 'new_v', 'new_v_w_ple_gate': 'new_v', 'new_v_w_ple': 'new_v', 'new_v_final_norm': 'new_v'}


def _forward(args):
    return _fwd_reference(*[args[k] for k in FWD_PARAMS])


def _output_shape():
    def fwd():
        inp = _fwd_setup_inputs(0)
        return _fwd_reference(*[inp[k] for k in FWD_PARAMS])
    out = _jax.eval_shape(fwd)
    return out.shape, out.dtype

N_MICROBATCH = 1
ADAM_LR = 0.001
ADAM_B1 = 0.9
ADAM_B2 = 0.999
ADAM_EPS = 1e-08
ADAM_WD = 0.01
ADAM_STEP = 10
PER_EXAMPLE_BATCH_AXIS = {'x': 0, 'p': 1, 'loss_target': 0}
SHARED_INPUTS = []
_WEIGHT_DTYPES = {'mix_norm': _jnp.float32, 'w_in': _jnp.float32, 'gate_b': _jnp.float32, 'conv_w': _jnp.float32, 'conv_b': _jnp.float32, 'lru_wa': _jnp.float32, 'lru_ba': _jnp.float32, 'lru_wx': _jnp.float32, 'lru_bx': _jnp.float32, 'lru_lambda': _jnp.float32, 'mla_q_norm': _jnp.float32, 'mla_wuq': _jnp.float32, 'mla_kv_norm': _jnp.float32, 'mla_wukv': _jnp.float32, 'fox_bf': _jnp.float32, 'w_br_a': _jnp.float32, 'w_br_b': _jnp.float32, 'w_br_c': _jnp.float32, 'w_o': _jnp.float32, 'ffn_norm': _jnp.float32, 'w_gate_up': _jnp.float32, 'w_down': _jnp.float32, 'ple_norm': _jnp.float32, 'w_ple_gate': _jnp.float32, 'w_ple': _jnp.float32, 'final_norm': _jnp.float32}
MOMENT_SCALE = {'mix_norm': 1.291078e-01, 'w_in': 5.093073e-02, 'gate_b': 2.571739e-02, 'conv_w': 9.668646e-02, 'conv_b': 1.099176e+00, 'lru_wa': 3.452392e-02, 'lru_ba': 2.791904e-02, 'lru_wx': 6.240022e-02, 'lru_bx': 6.175005e-02, 'lru_lambda': 6.127938e-02, 'mla_q_norm': 3.058780e-02, 'mla_wuq': 2.169012e-02, 'mla_kv_norm': 5.405519e-02, 'mla_wukv': 2.656635e-02, 'fox_bf': 1.822922e-01, 'w_br_a': 1.059234e-01, 'w_br_b': 2.135840e-02, 'w_br_c': 4.622765e-02, 'w_o': 1.040973e-01, 'ffn_norm': 1.569784e-01, 'w_gate_up': 6.705235e-02, 'w_down': 1.094367e-01, 'ple_norm': 3.908345e-02, 'w_ple_gate': 3.893597e-02, 'w_ple': 9.786942e-02, 'final_norm': 6.405652e+01}


def _to_microbatches(a, axis):
    t = _jnp.moveaxis(a, axis, 0)
    t = t.reshape((N_MICROBATCH, t.shape[0] // N_MICROBATCH) + t.shape[1:])
    return _jnp.moveaxis(t, 1, axis + 1)


def setup_inputs(seed: int = 0) -> dict:
    inp = _fwd_setup_inputs(seed)
    key = _jax.random.fold_in(_jax.random.key(seed), 7919)
    shape, _ = _output_shape()
    out = dict(inp)
    out["loss_target"] = _jax.random.normal(_jax.random.fold_in(key, 0), shape, _jnp.float32)
    for i, name in enumerate(TWIN_WEIGHTS):
        w = inp[name].astype(_jnp.float32)
        if MOMENT_SCALE is None:
            s = _jnp.sqrt(_jnp.mean(_jnp.square(w)) + 1e-30)
        else:
            s = MOMENT_SCALE[name]
        km, kv = _jax.random.split(_jax.random.fold_in(key, i + 1))
        out[name] = w
        out["m_" + name] = s * _jax.random.normal(km, w.shape, _jnp.float32)
        out["v_" + name] = (s * s) * _jax.random.uniform(kv, w.shape, _jnp.float32, 0.5, 1.5)
    if N_MICROBATCH > 1:
        for name, axis in PER_EXAMPLE_BATCH_AXIS.items():
            out[name] = _to_microbatches(out[name], axis)
    return {'x': out['x'], 'p': out['p'], 'mix_norm': out['mix_norm'], 'w_in': out['w_in'], 'gate_b': out['gate_b'], 'conv_w': out['conv_w'], 'conv_b': out['conv_b'], 'lru_wa': out['lru_wa'], 'lru_ba': out['lru_ba'], 'lru_wx': out['lru_wx'], 'lru_bx': out['lru_bx'], 'lru_lambda': out['lru_lambda'], 'mla_q_norm': out['mla_q_norm'], 'mla_wuq': out['mla_wuq'], 'mla_kv_norm': out['mla_kv_norm'], 'mla_wukv': out['mla_wukv'], 'fox_bf': out['fox_bf'], 'w_br_a': out['w_br_a'], 'w_br_b': out['w_br_b'], 'w_br_c': out['w_br_c'], 'w_o': out['w_o'], 'ffn_norm': out['ffn_norm'], 'w_gate_up': out['w_gate_up'], 'w_down': out['w_down'], 'ple_norm': out['ple_norm'], 'w_ple_gate': out['w_ple_gate'], 'w_ple': out['w_ple'], 'final_norm': out['final_norm'], 'loss_target': out['loss_target'], 'm_mix_norm': out['m_mix_norm'], 'm_w_in': out['m_w_in'], 'm_gate_b': out['m_gate_b'], 'm_conv_w': out['m_conv_w'], 'm_conv_b': out['m_conv_b'], 'm_lru_wa': out['m_lru_wa'], 'm_lru_ba': out['m_lru_ba'], 'm_lru_wx': out['m_lru_wx'], 'm_lru_bx': out['m_lru_bx'], 'm_lru_lambda': out['m_lru_lambda'], 'm_mla_q_norm': out['m_mla_q_norm'], 'm_mla_wuq': out['m_mla_wuq'], 'm_mla_kv_norm': out['m_mla_kv_norm'], 'm_mla_wukv': out['m_mla_wukv'], 'm_fox_bf': out['m_fox_bf'], 'm_w_br_a': out['m_w_br_a'], 'm_w_br_b': out['m_w_br_b'], 'm_w_br_c': out['m_w_br_c'], 'm_w_o': out['m_w_o'], 'm_ffn_norm': out['m_ffn_norm'], 'm_w_gate_up': out['m_w_gate_up'], 'm_w_down': out['m_w_down'], 'm_ple_norm': out['m_ple_norm'], 'm_w_ple_gate': out['m_w_ple_gate'], 'm_w_ple': out['m_w_ple'], 'm_final_norm': out['m_final_norm'], 'v_mix_norm': out['v_mix_norm'], 'v_w_in': out['v_w_in'], 'v_gate_b': out['v_gate_b'], 'v_conv_w': out['v_conv_w'], 'v_conv_b': out['v_conv_b'], 'v_lru_wa': out['v_lru_wa'], 'v_lru_ba': out['v_lru_ba'], 'v_lru_wx': out['v_lru_wx'], 'v_lru_bx': out['v_lru_bx'], 'v_lru_lambda': out['v_lru_lambda'], 'v_mla_q_norm': out['v_mla_q_norm'], 'v_mla_wuq': out['v_mla_wuq'], 'v_mla_kv_norm': out['v_mla_kv_norm'], 'v_mla_wukv': out['v_mla_wukv'], 'v_fox_bf': out['v_fox_bf'], 'v_w_br_a': out['v_w_br_a'], 'v_w_br_b': out['v_w_br_b'], 'v_w_br_c': out['v_w_br_c'], 'v_w_o': out['v_w_o'], 'v_ffn_norm': out['v_ffn_norm'], 'v_w_gate_up': out['v_w_gate_up'], 'v_w_down': out['v_w_down'], 'v_ple_norm': out['v_ple_norm'], 'v_w_ple_gate': out['v_w_ple_gate'], 'v_w_ple': out['v_w_ple'], 'v_final_norm': out['v_final_norm']}


def _loss(weights, diff, rest, loss_target):
    with _jax.named_scope("forward"):
        args = {**rest, TWIN_DIFF_INPUT: diff, **{k: w.astype(_WEIGHT_DTYPES[k]) for k, w in weights.items()}}
        y = _forward(args)
    with _jax.named_scope("loss_head"):
        err = _jnp.square(y.astype(_jnp.float32) - loss_target)
        return 0.5 * _jnp.sum(_jnp.mean(err, axis=-1)) if err.ndim else 0.5 * err


def _adamw(w, g, m, v):
    m = ADAM_B1 * m + (1.0 - ADAM_B1) * g
    v = ADAM_B2 * v + (1.0 - ADAM_B2) * _jnp.square(g)
    m_hat = m / (1.0 - ADAM_B1 ** ADAM_STEP)
    v_hat = v / (1.0 - ADAM_B2 ** ADAM_STEP)
    delta = -ADAM_LR * (m_hat / (_jnp.sqrt(v_hat) + ADAM_EPS) + ADAM_WD * w)
    return delta, m, v


def reference(x, p, mix_norm, w_in, gate_b, conv_w, conv_b, lru_wa, lru_ba, lru_wx, lru_bx, lru_lambda, mla_q_norm, mla_wuq, mla_kv_norm, mla_wukv, fox_bf, w_br_a, w_br_b, w_br_c, w_o, ffn_norm, w_gate_up, w_down, ple_norm, w_ple_gate, w_ple, final_norm, loss_target, m_mix_norm, m_w_in, m_gate_b, m_conv_w, m_conv_b, m_lru_wa, m_lru_ba, m_lru_wx, m_lru_bx, m_lru_lambda, m_mla_q_norm, m_mla_wuq, m_mla_kv_norm, m_mla_wukv, m_fox_bf, m_w_br_a, m_w_br_b, m_w_br_c, m_w_o, m_ffn_norm, m_w_gate_up, m_w_down, m_ple_norm, m_w_ple_gate, m_w_ple, m_final_norm, v_mix_norm, v_w_in, v_gate_b, v_conv_w, v_conv_b, v_lru_wa, v_lru_ba, v_lru_wx, v_lru_bx, v_lru_lambda, v_mla_q_norm, v_mla_wuq, v_mla_kv_norm, v_mla_wukv, v_fox_bf, v_w_br_a, v_w_br_b, v_w_br_c, v_w_o, v_ffn_norm, v_w_gate_up, v_w_down, v_ple_norm, v_w_ple_gate, v_w_ple, v_final_norm):
    given = dict(x=x, p=p, mix_norm=mix_norm, w_in=w_in, gate_b=gate_b, conv_w=conv_w, conv_b=conv_b, lru_wa=lru_wa, lru_ba=lru_ba, lru_wx=lru_wx, lru_bx=lru_bx, lru_lambda=lru_lambda, mla_q_norm=mla_q_norm, mla_wuq=mla_wuq, mla_kv_norm=mla_kv_norm, mla_wukv=mla_wukv, fox_bf=fox_bf, w_br_a=w_br_a, w_br_b=w_br_b, w_br_c=w_br_c, w_o=w_o, ffn_norm=ffn_norm, w_gate_up=w_gate_up, w_down=w_down, ple_norm=ple_norm, w_ple_gate=w_ple_gate, w_ple=w_ple, final_norm=final_norm, loss_target=loss_target, m_mix_norm=m_mix_norm, m_w_in=m_w_in, m_gate_b=m_gate_b, m_conv_w=m_conv_w, m_conv_b=m_conv_b, m_lru_wa=m_lru_wa, m_lru_ba=m_lru_ba, m_lru_wx=m_lru_wx, m_lru_bx=m_lru_bx, m_lru_lambda=m_lru_lambda, m_mla_q_norm=m_mla_q_norm, m_mla_wuq=m_mla_wuq, m_mla_kv_norm=m_mla_kv_norm, m_mla_wukv=m_mla_wukv, m_fox_bf=m_fox_bf, m_w_br_a=m_w_br_a, m_w_br_b=m_w_br_b, m_w_br_c=m_w_br_c, m_w_o=m_w_o, m_ffn_norm=m_ffn_norm, m_w_gate_up=m_w_gate_up, m_w_down=m_w_down, m_ple_norm=m_ple_norm, m_w_ple_gate=m_w_ple_gate, m_w_ple=m_w_ple, m_final_norm=m_final_norm, v_mix_norm=v_mix_norm, v_w_in=v_w_in, v_gate_b=v_gate_b, v_conv_w=v_conv_w, v_conv_b=v_conv_b, v_lru_wa=v_lru_wa, v_lru_ba=v_lru_ba, v_lru_wx=v_lru_wx, v_lru_bx=v_lru_bx, v_lru_lambda=v_lru_lambda, v_mla_q_norm=v_mla_q_norm, v_mla_wuq=v_mla_wuq, v_mla_kv_norm=v_mla_kv_norm, v_mla_wukv=v_mla_wukv, v_fox_bf=v_fox_bf, v_w_br_a=v_w_br_a, v_w_br_b=v_w_br_b, v_w_br_c=v_w_br_c, v_w_o=v_w_o, v_ffn_norm=v_ffn_norm, v_w_gate_up=v_w_gate_up, v_w_down=v_w_down, v_ple_norm=v_ple_norm, v_w_ple_gate=v_w_ple_gate, v_w_ple=v_w_ple, v_final_norm=v_final_norm)
    weights = {n: given[n] for n in TWIN_WEIGHTS}
    shared = {n: given[n] for n in SHARED_INPUTS}
    per_example = {n: given[n] for n in ['x', 'p']}
    grad_fn = _jax.value_and_grad(_loss, argnums=(0, 1))

    def one_microbatch(ex, loss_target):
        ex = dict(ex)
        diff = ex.pop(TWIN_DIFF_INPUT)
        return grad_fn(weights, diff, {**shared, **ex}, loss_target)

    if N_MICROBATCH == 1:
        loss, (grad_w, grad_x) = one_microbatch(per_example, given["loss_target"])
    else:
        def body(carry, xs):
            loss_sum, grad_sum = carry
            l_k, (gw_k, gx_k) = one_microbatch(xs[0], xs[1])
            with _jax.named_scope("update"):
                return (loss_sum + l_k, _jax.tree.map(_jnp.add, grad_sum, gw_k)), gx_k

        init = (_jnp.zeros((), _jnp.float32), _jax.tree.map(_jnp.zeros_like, weights))
        (loss, grad_w), grad_x = _jax.lax.scan(body, init, (per_example, given["loss_target"]))
    with _jax.named_scope("update"):
        delta_w, new_m, new_v = {}, {}, {}
        for n in TWIN_WEIGHTS:
            delta_w[n], new_m[n], new_v[n] = _adamw(weights[n], grad_w[n], given["m_" + n], given["v_" + n])
    return (loss, grad_x, *[grad_w[n] for n in TWIN_WEIGHTS], *[delta_w[n] for n in TWIN_WEIGHTS],
            *[new_m[n] for n in TWIN_WEIGHTS], *[new_v[n] for n in TWIN_WEIGHTS])
```

```python
import functools
import math

import numpy as np
import jax
import jax.numpy as jnp
from jax import lax
from jax.experimental import pallas as pl
from jax.experimental.pallas import tpu as pltpu

F32, BF16 = jnp.float32, jnp.bfloat16
MESH = pl.DeviceIdType.MESH

D_MODEL = 1024
DEPTH = 2
EPS = 1e-6
NEG_INF = -1e30
LRU_WIDTH = 512
LRU_HEADS = 8
LRU_C = 8.0
CONV_WIDTH = 4
HEADS = 8
MLA_Q_LORA = 384
MLA_KV_LORA = 256
MLA_NOPE = 64
MLA_ROPE = 32
MLA_V = 64
ROPE_BASE = 10000.0
FOX_HEAD_DIM = 64
D_FF = 2816
PLE_DIM = 256
HEAD_PAD = 128
MLA_SCALE = (MLA_NOPE + MLA_ROPE) ** -0.5
FOX_SCALE = FOX_HEAD_DIM ** -0.5

ADAM_LR, ADAM_B1, ADAM_B2, ADAM_EPS, ADAM_WD, ADAM_STEP = 0.001, 0.9, 0.999, 1e-08, 0.01, 10

VMEM_LIMIT_BYTES = 48 * 1024 * 1024
LANES = 128
PACK_W = 1024

ROW_TILE = 512
ATTN_TILE = 512
LRU_CHUNK = 512


def _cparams(dims):
    return pltpu.CompilerParams(dimension_semantics=dims, vmem_limit_bytes=VMEM_LIMIT_BYTES)


def _tile(n, cap):
    if n <= cap:
        return n
    t = (cap // LANES) * LANES
    while t >= LANES:
        if n % t == 0:
            return t
        t -= LANES
    raise ValueError(f"no tile for {n} under {cap}")


def _rows(n):
    return min(ROW_TILE, n)


def _mm(a, b, *, ta=False, tb=False, out_dtype=F32, res=None, name):
    K, M = a.shape if ta else a.shape[::-1]
    N, K2 = b.shape if tb else b.shape[::-1]
    assert K == K2, (name, a.shape, b.shape)
    tm, tn, tk = _tile(M, 512), _tile(N, 1792), _tile(K, 1408)
    nk = K // tk
    a_spec = pl.BlockSpec((tk, tm), lambda i, j, k: (k, i)) if ta else pl.BlockSpec((tm, tk), lambda i, j, k: (i, k))
    b_spec = pl.BlockSpec((tn, tk), lambda i, j, k: (j, k)) if tb else pl.BlockSpec((tk, tn), lambda i, j, k: (k, j))
    o_spec = pl.BlockSpec((tm, tn), lambda i, j, k: (i, j))
    dn = (((0,) if ta else (1,), (1,) if tb else (0,)), ((), ()))
    has_res = res is not None

    def body(*refs):
        a_ref, b_ref = refs[0], refs[1]
        r_ref = refs[2] if has_res else None
        o_ref = refs[3] if has_res else refs[2]
        av, bv = a_ref[...], b_ref[...]
        if av.dtype != BF16:
            av = av.astype(BF16)
        if bv.dtype != BF16:
            bv = bv.astype(BF16)
        part = lax.dot_general(av, bv, dn, preferred_element_type=F32)

        def finish(total):
            if has_res:
                total = total + r_ref[...]
            o_ref[...] = total.astype(out_dtype)

        if nk == 1:
            finish(part)
        else:
            acc = refs[-1]
            k = pl.program_id(2)

            @pl.when(k == 0)
            def _():
                acc[...] = part

            @pl.when(k > 0)
            def _():
                acc[...] += part

            @pl.when(k == nk - 1)
            def _():
                finish(acc[...])

    in_specs = [a_spec, b_spec] + ([o_spec] if has_res else [])
    args = (a, b) + ((res,) if has_res else ())
    return pl.pallas_call(
        body, name=name, grid=(M // tm, N // tn, nk), in_specs=in_specs, out_specs=o_spec,
        out_shape=jax.ShapeDtypeStruct((M, N), out_dtype),
        scratch_shapes=[pltpu.VMEM((tm, tn), F32)] if nk > 1 else [],
        compiler_params=_cparams(("parallel", "parallel", "arbitrary")),
    )(*args)


def _rmsnorm_fwd(x, g, name):
    S, W = x.shape
    tm = _rows(S)

    def body(x_ref, g_ref, o_ref):
        xf = x_ref[...]
        rstd = lax.rsqrt(jnp.mean(xf * xf, axis=1, keepdims=True) + EPS)
        o_ref[...] = (xf * rstd * g_ref[...]).astype(BF16)

    return pl.pallas_call(
        body, name=name, grid=(S // tm,),
        in_specs=[pl.BlockSpec((tm, W), lambda i: (i, 0)), pl.BlockSpec((1, W), lambda i: (0, 0))],
        out_specs=pl.BlockSpec((tm, W), lambda i: (i, 0)),
        out_shape=jax.ShapeDtypeStruct((S, W), BF16), compiler_params=_cparams(("parallel",)),
    )(x, g.reshape(1, W))


def _rmsnorm_bwd(x, g, dy, *, add=None, out_dtype=F32, name):
    S, W = x.shape
    tm = _rows(S)
    has_add = add is not None

    def body(*refs):
        x_ref, g_ref, dy_ref = refs[:3]
        add_ref = refs[3] if has_add else None
        dx_ref, dg_ref = refs[-2], refs[-1]
        xf = x_ref[...]
        rstd = lax.rsqrt(jnp.mean(xf * xf, axis=1, keepdims=True) + EPS)
        xhat = xf * rstd
        dyv = dy_ref[...]
        dxh = dyv * g_ref[...]
        dx = rstd * (dxh - xhat * jnp.mean(dxh * xhat, axis=1, keepdims=True))
        if has_add:
            dx = dx + add_ref[...]
        dx_ref[...] = dx.astype(out_dtype)

        @pl.when(pl.program_id(0) == 0)
        def _():
            dg_ref[...] = jnp.zeros_like(dg_ref)

        dg_ref[...] += jnp.sum(dyv * xhat, axis=0, keepdims=True)

    row = pl.BlockSpec((tm, W), lambda i: (i, 0))
    vec = pl.BlockSpec((1, W), lambda i: (0, 0))
    dx, dg = pl.pallas_call(
        body, name=name, grid=(S // tm,),
        in_specs=[row, vec, row] + ([row] if has_add else []),
        out_specs=(row, vec),
        out_shape=(jax.ShapeDtypeStruct((S, W), out_dtype), jax.ShapeDtypeStruct((1, W), F32)),
        compiler_params=_cparams(("arbitrary",)),
    )(x, g.reshape(1, W), dy, *((add,) if has_add else ()))
    return dx, dg.reshape(W)


def _loss_head(x, g, target):
    S, W = x.shape
    tm = _rows(S)

    def body(x_ref, g_ref, t_ref, loss_ref, dx_ref, dg_ref):
        xf = x_ref[...]
        gv = g_ref[...]
        rstd = lax.rsqrt(jnp.mean(xf * xf, axis=1, keepdims=True) + EPS)
        xhat = xf * rstd
        err = xhat * gv - t_ref[...]
        part = 0.5 * jnp.sum(jnp.mean(err * err, axis=1, keepdims=True), axis=0, keepdims=True)
        dyv = err * (1.0 / W)
        dxh = dyv * gv
        dx_ref[...] = rstd * (dxh - xhat * jnp.mean(dxh * xhat, axis=1, keepdims=True))

        @pl.when(pl.program_id(0) == 0)
        def _():
            dg_ref[...] = jnp.zeros_like(dg_ref)
            loss_ref[...] = jnp.zeros_like(loss_ref)

        dg_ref[...] += jnp.sum(dyv * xhat, axis=0, keepdims=True)
        loss_ref[...] += part

    row = pl.BlockSpec((tm, W), lambda i: (i, 0))
    vec = pl.BlockSpec((1, W), lambda i: (0, 0))
    loss, dx, dg = pl.pallas_call(
        body, name="loss_head", grid=(S // tm,), in_specs=[row, vec, row],
        out_specs=(pl.BlockSpec((1, 1), lambda i: (0, 0)), row, vec),
        out_shape=(jax.ShapeDtypeStruct((1, 1), F32), jax.ShapeDtypeStruct((S, W), F32), jax.ShapeDtypeStruct((1, W), F32)),
        compiler_params=_cparams(("arbitrary",)),
    )(x, g.reshape(1, W), target)
    return loss[0, 0], dx, dg.reshape(W)


def _scan_fwd(a, b, row):
    T = a.shape[0]
    d = 1
    while d < T:
        keep = row >= d
        b = jnp.where(keep, a * pltpu.roll(b, d, axis=0) + b, b)
        a = jnp.where(keep, a * pltpu.roll(a, d, axis=0), a)
        d *= 2
    return a, b


def _scan_bwd(a, b, row):
    T = a.shape[0]
    d = 1
    while d < T:
        keep = row < T - d
        b = jnp.where(keep, a * pltpu.roll(b, T - d, axis=0) + b, b)
        a = jnp.where(keep, a * pltpu.roll(a, T - d, axis=0), a)
        d *= 2
    return a, b


def _expm1(x):
    small = x * (1.0 + x * (0.5 + x * (1.0 / 6 + x * (1.0 / 24 + x * (1.0 / 120 + x * (1.0 / 720 + x * (1.0 / 5040)))))))
    return jnp.where(jnp.abs(x) < 0.25, small, jnp.exp(x) - 1.0)


_GELU_C = math.sqrt(2.0 / math.pi)


def _gelu_and_grad(x):
    inner = _GELU_C * (x + 0.044715 * x * x * x)
    th = jnp.tanh(inner)
    val = 0.5 * x * (1.0 + th)
    grad = 0.5 * (1.0 + th) + 0.5 * x * (1.0 - th * th) * _GELU_C * (1.0 + 3 * 0.044715 * x * x)
    return val, grad


def _lru_gates(xc, wa, wx, ba, bx, lam):
    xcb = xc.astype(BF16)
    r = jax.nn.sigmoid(jnp.dot(xcb, wa, preferred_element_type=F32) + ba)
    ig = jax.nn.sigmoid(jnp.dot(xcb, wx, preferred_element_type=F32) + bx)
    sp = jax.nn.softplus(-lam)
    log_a = -LRU_C * r * sp
    a = jnp.exp(log_a)
    mult = jnp.sqrt(-_expm1(2.0 * log_a))
    return xcb, r, ig, sp, a, mult


def _lru_fwd(u, ug, conv_w, conv_b, wa_bd, wx_bd, ba, bx, lam):
    S, W = u.shape
    T = min(LRU_CHUNK, S)
    nl, nc = W // LANES, S // T

    def body(u_ref, ug_ref, cw_ref, cb_ref, wa_ref, wx_ref, ba_ref, bx_ref, lam_ref, ya_ref, xc_ref, h_ref, prev_u, h_carry):
        c = pl.program_id(1)

        @pl.when(c == 0)
        def _():
            prev_u[...] = jnp.zeros_like(prev_u)
            h_carry[...] = jnp.zeros_like(h_carry)

        uv = u_ref[...]
        row = lax.broadcasted_iota(jnp.int32, (T, LANES), 0)
        row8 = lax.broadcasted_iota(jnp.int32, (8, LANES), 0)
        cw = cw_ref[...]
        xc = cb_ref[...] + uv * cw[3:4, :]
        pv = prev_u[...]
        for k in range(1, CONV_WIDTH):
            us = pltpu.roll(uv, k, axis=0)
            top = jnp.where(row8 < k, pltpu.roll(pv, k, axis=0), us[0:8])
            us = jnp.concatenate([top, us[8:]], axis=0)
            xc = xc + us * cw[3 - k:4 - k, :]
        prev_u[...] = uv[T - 8:T]
        _, r, ig, sp, a, mult = _lru_gates(xc, wa_ref[...], wx_ref[...], ba_ref[...], bx_ref[...], lam_ref[...])
        bb = mult * (ig * xc)
        aa, hh = _scan_fwd(a, bb, row)
        h = hh + aa * h_carry[7:8, :]
        h_carry[...] = h[T - 8:T]
        gl, _ = _gelu_and_grad(ug_ref[...])
        ya_ref[...] = (h * gl).astype(BF16)
        xc_ref[...] = xc
        h_ref[...] = h

    seq = pl.BlockSpec((T, LANES), lambda l, c: (c, l))
    vec = pl.BlockSpec((1, LANES), lambda l, c: (0, l))
    mat = pl.BlockSpec((None, LANES, LANES), lambda l, c: (l, 0, 0))
    return pl.pallas_call(
        body, name="lru_fwd", grid=(nl, nc),
        in_specs=[seq, seq, pl.BlockSpec((CONV_WIDTH, LANES), lambda l, c: (0, l)), vec, mat, mat, vec, vec, vec],
        out_specs=(seq, seq, seq),
        out_shape=(jax.ShapeDtypeStruct((S, W), BF16), jax.ShapeDtypeStruct((S, W), F32), jax.ShapeDtypeStruct((S, W), F32)),
        scratch_shapes=[pltpu.VMEM((8, LANES), F32), pltpu.VMEM((8, LANES), F32)],
        compiler_params=_cparams(("parallel", "arbitrary")),
    )(u, ug, conv_w, conv_b.reshape(1, W), wa_bd, wx_bd, ba.reshape(1, W), bx.reshape(1, W), lam.reshape(1, W))


def _lru_bwd(dya, u, ug, xc, h, conv_w, wa_bd, wx_bd, ba, bx, lam):
    S, W = u.shape
    T = min(LRU_CHUNK, S)
    nl, nc = W // LANES, S // T
    tb8 = T // 8

    def body(dya_ref, u_ref, ug_ref, xc_ref, h_ref, hp_ref, cw_ref, wa_ref, wx_ref, ba_ref, bx_ref, lam_ref,
             du_ref, dug_ref, dcw_ref, dcb_ref, dba_ref, dbx_ref, dlam_ref, dwa_ref, dwx_ref,
             g_next, a_next, dxc_next):
        c = pl.program_id(1)

        @pl.when(c == 0)
        def _():
            g_next[...] = jnp.zeros_like(g_next)
            a_next[...] = jnp.zeros_like(a_next)
            dxc_next[...] = jnp.zeros_like(dxc_next)
            for ref in (dcw_ref, dcb_ref, dba_ref, dbx_ref, dlam_ref, dwa_ref, dwx_ref):
                ref[...] = jnp.zeros_like(ref)

        row = lax.broadcasted_iota(jnp.int32, (T, LANES), 0)
        row8 = lax.broadcasted_iota(jnp.int32, (8, LANES), 0)
        xcv = xc_ref[...]
        wa, wx = wa_ref[...], wx_ref[...]
        xcb, r, ig, sp, a, mult = _lru_gates(xcv, wa, wx, ba_ref[...], bx_ref[...], lam_ref[...])
        gl, dgl = _gelu_and_grad(ug_ref[...])
        dyav = dya_ref[...]
        hv = h_ref[...]
        dug_ref[...] = (dyav * hv * dgl).astype(BF16)
        dh = dyav * gl
        a_up = pltpu.roll(a, T - 1, axis=0)
        a_up = jnp.where(row == T - 1, a_next[0:1, :], a_up)
        prod, gg = _scan_bwd(a_up, dh, row)
        g = gg + prod * g_next[0:1, :]
        h_prev = pltpu.roll(hv, 1, axis=0)
        first_chunk = c == nc - 1
        h_before = jnp.where(first_chunk, 0.0, hp_ref[7:8, :])
        h_prev = jnp.where(row == 0, h_before, h_prev)
        da = g * h_prev
        d_mult = g * (ig * xcv)
        d_ig = g * mult * xcv
        dxc = g * mult * ig
        d_log_a = da * a - d_mult * (a * a) / mult
        d_r = d_log_a * (-LRU_C * sp)
        d_pa = d_r * r * (1.0 - r)
        d_px = d_ig * ig * (1.0 - ig)
        d_pab, d_pxb = d_pa.astype(BF16), d_px.astype(BF16)
        nt = (((1,), (1,)), ((), ()))
        tn = (((0,), (0,)), ((), ()))
        dxc = dxc + lax.dot_general(d_pab, wa, nt, preferred_element_type=F32) + lax.dot_general(d_pxb, wx, nt, preferred_element_type=F32)
        dwa_ref[...] += lax.dot_general(xcb, d_pab, tn, preferred_element_type=F32)
        dwx_ref[...] += lax.dot_general(xcb, d_pxb, tn, preferred_element_type=F32)
        dlam_ref[...] += jnp.sum(d_log_a * r, axis=0, keepdims=True)
        dba_ref[...] += jnp.sum(d_pa, axis=0, keepdims=True)
        dbx_ref[...] += jnp.sum(d_px, axis=0, keepdims=True)
        dcb_ref[...] += jnp.sum(dxc, axis=0, keepdims=True)
        uv = u_ref[...]
        cw = cw_ref[...]
        nxt = dxc_next[...]
        du = dxc * cw[3:4, :]
        dcw_ref[3:4, :] += jnp.sum(uv * dxc, axis=0, keepdims=True)
        for k in range(1, CONV_WIDTH):
            ds = pltpu.roll(dxc, T - k, axis=0)
            bot = jnp.where(row8 >= 8 - k, pltpu.roll(nxt, 8 - k, axis=0), ds[T - 8:T])
            ds = jnp.concatenate([ds[:T - 8], bot], axis=0)
            du = du + ds * cw[3 - k:4 - k, :]
            dcw_ref[3 - k:4 - k, :] += jnp.sum(uv * ds, axis=0, keepdims=True)
        du_ref[...] = du.astype(BF16)
        g_next[...] = g[0:8]
        a_next[...] = a[0:8]
        dxc_next[...] = dxc[0:8]

    seq = pl.BlockSpec((T, LANES), lambda l, c: (nc - 1 - c, l))
    before = pl.BlockSpec((8, LANES), lambda l, c: (jnp.maximum((nc - 1 - c) * tb8 - 1, 0), l))
    vec = pl.BlockSpec((1, LANES), lambda l, c: (0, l))
    cwb = pl.BlockSpec((CONV_WIDTH, LANES), lambda l, c: (0, l))
    mat = pl.BlockSpec((None, LANES, LANES), lambda l, c: (l, 0, 0))
    vshape = jax.ShapeDtypeStruct((1, W), F32)
    mshape = jax.ShapeDtypeStruct((nl, LANES, LANES), F32)
    return pl.pallas_call(
        body, name="lru_bwd", grid=(nl, nc),
        in_specs=[seq, seq, seq, seq, seq, before, cwb, mat, mat, vec, vec, vec],
        out_specs=(seq, seq, cwb, vec, vec, vec, vec, mat, mat),
        out_shape=(jax.ShapeDtypeStruct((S, W), BF16), jax.ShapeDtypeStruct((S, W), BF16),
                   jax.ShapeDtypeStruct((CONV_WIDTH, W), F32), vshape, vshape, vshape, vshape, mshape, mshape),
        scratch_shapes=[pltpu.VMEM((8, LANES), F32)] * 3,
        compiler_params=_cparams(("parallel", "arbitrary")),
    )(dya, u, ug, xc, h, h, conv_w, wa_bd, wx_bd, ba.reshape(1, W), bx.reshape(1, W), lam.reshape(1, W))


def _decay_fwd(f_logit, bf):
    S = f_logit.shape[0]
    T = min(LRU_CHUNK, S)

    def body(f_ref, b_ref, o_ref, carry):
        @pl.when(pl.program_id(0) == 0)
        def _():
            carry[...] = jnp.zeros_like(carry)

        row = lax.broadcasted_iota(jnp.int32, (T, LANES), 0)
        v = jax.nn.log_sigmoid(f_ref[...] + b_ref[...])
        d = 1
        while d < T:
            v = jnp.where(row >= d, v + pltpu.roll(v, d, axis=0), v)
            d *= 2
        v = v + carry[7:8, :]
        carry[...] = v[T - 8:T]
        o_ref[...] = v

    return pl.pallas_call(
        body, name="decay_fwd", grid=(S // T,),
        in_specs=[pl.BlockSpec((T, LANES), lambda c: (c, 0)), pl.BlockSpec((1, LANES), lambda c: (0, 0))],
        out_specs=pl.BlockSpec((T, LANES), lambda c: (c, 0)),
        out_shape=jax.ShapeDtypeStruct((S, LANES), F32), scratch_shapes=[pltpu.VMEM((8, LANES), F32)],
        compiler_params=_cparams(("arbitrary",)),
    )(f_logit, bf)


def _decay_bwd(d_dec, f_logit, bf):
    S = f_logit.shape[0]
    T = min(LRU_CHUNK, S)
    nc = S // T

    def body(dd_ref, f_ref, b_ref, df_ref, db_ref, carry):
        @pl.when(pl.program_id(0) == 0)
        def _():
            carry[...] = jnp.zeros_like(carry)
            db_ref[...] = jnp.zeros_like(db_ref)

        row = lax.broadcasted_iota(jnp.int32, (T, LANES), 0)
        v = dd_ref[...]
        d = 1
        while d < T:
            v = jnp.where(row < T - d, v + pltpu.roll(v, T - d, axis=0), v)
            d *= 2
        v = v + carry[0:1, :]
        carry[...] = v[0:8]
        df = v * jax.nn.sigmoid(-(f_ref[...] + b_ref[...]))
        df_ref[...] = df.astype(BF16)
        db_ref[...] += jnp.sum(df, axis=0, keepdims=True)

    seq = pl.BlockSpec((T, LANES), lambda c: (nc - 1 - c, 0))
    vec = pl.BlockSpec((1, LANES), lambda c: (0, 0))
    return pl.pallas_call(
        body, name="decay_bwd", grid=(nc,), in_specs=[seq, seq, vec], out_specs=(seq, vec),
        out_shape=(jax.ShapeDtypeStruct((S, LANES), BF16), jax.ShapeDtypeStruct((1, LANES), F32)),
        scratch_shapes=[pltpu.VMEM((8, LANES), F32)], compiler_params=_cparams(("arbitrary",)),
    )(d_dec, f_logit, bf)


def _rope_tables(S):
    pos = jnp.arange(S, dtype=F32)
    inv_freq = ROPE_BASE ** (-jnp.arange(0, MLA_ROPE, 2, dtype=F32) / MLA_ROPE)
    ang = pos[:, None] * inv_freq[None, :]
    cos, sin = jnp.cos(ang), jnp.sin(ang)
    half = MLA_ROPE // 2
    z = lambda n: jnp.zeros((S, n), F32)
    c_q = jnp.concatenate([jnp.ones((S, MLA_NOPE), F32), cos, cos, z(HEAD_PAD - MLA_NOPE - MLA_ROPE)], axis=1)
    c_k = jnp.concatenate([z(MLA_NOPE), cos, cos, z(HEAD_PAD - MLA_NOPE - MLA_ROPE)], axis=1)
    s_lo = jnp.concatenate([z(MLA_NOPE), -sin, z(HEAD_PAD - MLA_NOPE - half)], axis=1)
    s_hi = jnp.concatenate([z(MLA_NOPE + half), sin, z(HEAD_PAD - MLA_NOPE - MLA_ROPE)], axis=1)
    return c_q, c_k, s_lo, s_hi


def _rot(v, c, s_lo, s_hi):
    half = MLA_ROPE // 2
    return v * c + pltpu.roll(v, LANES - half, axis=1) * s_lo + pltpu.roll(v, half, axis=1) * s_hi


def _rot_t(dv, c, s_lo, s_hi):
    half = MLA_ROPE // 2
    return dv * c + pltpu.roll(dv * s_lo, half, axis=1) + pltpu.roll(dv * s_hi, LANES - half, axis=1)


def _rope_q(q_pre, c_q, s_lo, s_hi, *, transpose, out_dtype, name):
    S, W = q_pre.shape
    tm = _rows(S)
    fn = _rot_t if transpose else _rot

    def body(q_ref, c_ref, lo_ref, hi_ref, o_ref):
        o_ref[...] = (fn(q_ref[...] * MLA_SCALE, c_ref[...], lo_ref[...], hi_ref[...])).astype(out_dtype)

    blk = pl.BlockSpec((tm, LANES), lambda i, h: (i, h))
    tab = pl.BlockSpec((tm, LANES), lambda i, h: (i, 0))
    return pl.pallas_call(
        body, name=name, grid=(S // tm, W // LANES), in_specs=[blk, tab, tab, tab], out_specs=blk,
        out_shape=jax.ShapeDtypeStruct((S, W), out_dtype), compiler_params=_cparams(("parallel", "parallel")),
    )(q_pre, c_q, s_lo, s_hi)


def _rope_k(k_pre, k_rope, c_k, s_lo, s_hi):
    S, W = k_pre.shape
    tm = _rows(S)

    def body(k_ref, r_ref, c_ref, lo_ref, hi_ref, o_ref):
        o_ref[...] = (k_ref[...] + _rot(r_ref[...], c_ref[...], lo_ref[...], hi_ref[...])).astype(BF16)

    blk = pl.BlockSpec((tm, LANES), lambda i, h: (i, h))
    tab = pl.BlockSpec((tm, LANES), lambda i, h: (i, 0))
    return pl.pallas_call(
        body, name="rope_k", grid=(S // tm, W // LANES), in_specs=[blk, tab, tab, tab, tab], out_specs=blk,
        out_shape=jax.ShapeDtypeStruct((S, W), BF16), compiler_params=_cparams(("parallel", "parallel")),
    )(k_pre, k_rope, c_k, s_lo, s_hi)


def _rope_k_bwd(dk, c_k, s_lo, s_hi):
    S, W = dk.shape
    tm = _rows(S)

    def body(dk_ref, c_ref, lo_ref, hi_ref, o_ref):
        tot = dk_ref[:, 0:LANES]
        for hd in range(1, W // LANES):
            tot = tot + dk_ref[:, hd * LANES:(hd + 1) * LANES]
        o_ref[...] = _rot_t(tot, c_ref[...], lo_ref[...], hi_ref[...]).astype(BF16)

    tab = pl.BlockSpec((tm, LANES), lambda i: (i, 0))
    return pl.pallas_call(
        body, name="rope_k_bwd", grid=(S // tm,), in_specs=[pl.BlockSpec((tm, W), lambda i: (i, 0)), tab, tab, tab],
        out_specs=tab, out_shape=jax.ShapeDtypeStruct((S, LANES), BF16), compiler_params=_cparams(("parallel",)),
    )(dk, c_k, s_lo, s_hi)


def _pairs(n, by_key):
    if by_key:
        pr = [(i, j) for j in range(n) for i in range(j, n)]
    else:
        pr = [(i, j) for i in range(n) for j in range(i + 1)]
    return (jnp.asarray(np.array([p[0] for p in pr], np.int32)), jnp.asarray(np.array([p[1] for p in pr], np.int32)), len(pr))


def _unit_mask(shape, unit, key_axis):
    q = lax.broadcasted_iota(jnp.int32, shape, 1 - key_axis)
    k = lax.broadcasted_iota(jnp.int32, shape, key_axis)
    if unit > 1:
        q, k = q // unit, k // unit
    return q >= k


_NT = (((1,), (1,)), ((), ()))


def _attn_fwd(q, k, v, dec_col, dec_row, *, unit, name):
    S, W = q.shape
    H = W // LANES
    T = min(ATTN_TILE, S)
    n = S // T
    qi, kj, npairs = _pairs(n, by_key=False)
    has_dec = dec_col is not None

    def body(qi_ref, kj_ref, *refs):
        if has_dec:
            q_ref, k_ref, v_ref, dc_ref, dr_ref, o_ref, lse_ref, m_s, l_s, acc = refs
        else:
            q_ref, k_ref, v_ref, o_ref, lse_ref, m_s, l_s, acc = refs
        t = pl.program_id(1)
        i, j = qi_ref[t], kj_ref[t]

        @pl.when(j == 0)
        def _():
            m_s[...] = jnp.full_like(m_s, NEG_INF)
            l_s[...] = jnp.zeros_like(l_s)
            acc[...] = jnp.zeros_like(acc)

        def step(diag):
            s = lax.dot_general(q_ref[...], k_ref[...], _NT, preferred_element_type=F32)
            if has_dec:
                s = s + (dc_ref[...] - dr_ref[...])
            if diag:
                s = jnp.where(_unit_mask((T, T), unit, 1), s, NEG_INF)
            m_prev = m_s[...]
            m_new = jnp.maximum(m_prev, jnp.max(s, axis=1, keepdims=True))
            alpha = jnp.exp(m_prev - m_new)
            p = jnp.exp(s - m_new)
            l_s[...] = alpha * l_s[...] + jnp.sum(p, axis=1, keepdims=True)
            acc[...] = alpha * acc[...] + jnp.dot(p.astype(BF16), v_ref[...], preferred_element_type=F32)
            m_s[...] = m_new

        @pl.when(j < i)
        def _():
            step(False)

        @pl.when(j == i)
        def _():
            step(True)
            o_ref[...] = (acc[...] / l_s[...]).astype(BF16)
            lse_ref[...] = m_s[...] + jnp.log(l_s[...])

    qb = pl.BlockSpec((T, LANES), lambda h, t, qi, kj: (qi[t], h))
    kb = pl.BlockSpec((T, LANES), lambda h, t, qi, kj: (kj[t], h))
    colq = pl.BlockSpec((None, T, 1), lambda h, t, qi, kj: (h, qi[t], 0))
    rowk = pl.BlockSpec((None, 1, T), lambda h, t, qi, kj: (h, 0, kj[t]))
    in_specs = [qb, kb, kb] + ([colq, rowk] if has_dec else [])
    args = (q, k, v) + ((dec_col, dec_row) if has_dec else ())
    return pl.pallas_call(
        body, name=name,
        grid_spec=pltpu.PrefetchScalarGridSpec(
            num_scalar_prefetch=2, grid=(H, npairs), in_specs=in_specs, out_specs=(qb, colq),
            scratch_shapes=[pltpu.VMEM((T, 1), F32), pltpu.VMEM((T, 1), F32), pltpu.VMEM((T, LANES), F32)]),
        out_shape=(jax.ShapeDtypeStruct((S, W), BF16), jax.ShapeDtypeStruct((H, S, 1), F32)),
        compiler_params=_cparams(("parallel", "arbitrary")),
    )(qi, kj, *args)


def _attn_delta(do, o):
    S, W = o.shape
    H = W // LANES
    tm = _rows(S)

    def body(do_ref, o_ref, d_ref):
        d_ref[...] = jnp.sum(do_ref[...].astype(F32) * o_ref[...].astype(F32), axis=1, keepdims=True)

    blk = pl.BlockSpec((tm, LANES), lambda h, i: (i, h))
    return pl.pallas_call(
        body, name="attn_delta", grid=(H, S // tm), in_specs=[blk, blk],
        out_specs=pl.BlockSpec((None, tm, 1), lambda h, i: (h, i, 0)),
        out_shape=jax.ShapeDtypeStruct((H, S, 1), F32), compiler_params=_cparams(("parallel", "parallel")),
    )(do, o)


def _attn_bwd_dq(q, k, v, do, lse, delta, dec_col, dec_row, *, unit, out_dtype, name):
    S, W = q.shape
    H = W // LANES
    T = min(ATTN_TILE, S)
    n = S // T
    qi, kj, npairs = _pairs(n, by_key=False)
    has_dec = dec_col is not None

    def body(qi_ref, kj_ref, *refs):
        if has_dec:
            q_ref, k_ref, v_ref, do_ref, lse_ref, dl_ref, dc_ref, dr_ref, dq_ref, dd_ref, acc, dacc = refs
        else:
            q_ref, k_ref, v_ref, do_ref, lse_ref, dl_ref, dq_ref, acc = refs
        t = pl.program_id(1)
        i, j = qi_ref[t], kj_ref[t]

        @pl.when(j == 0)
        def _():
            acc[...] = jnp.zeros_like(acc)
            if has_dec:
                dacc[...] = jnp.zeros_like(dacc)

        def step(diag):
            kv = k_ref[...]
            s = lax.dot_general(q_ref[...], kv, _NT, preferred_element_type=F32)
            if has_dec:
                s = s + (dc_ref[...] - dr_ref[...])
            if diag:
                s = jnp.where(_unit_mask((T, T), unit, 1), s, NEG_INF)
            p = jnp.exp(s - lse_ref[...])
            dp = lax.dot_general(do_ref[...], v_ref[...], _NT, preferred_element_type=F32)
            ds = p * (dp - dl_ref[...])
            acc[...] += jnp.dot(ds.astype(BF16), kv, preferred_element_type=F32)
            if has_dec:
                dacc[...] += jnp.sum(ds, axis=1, keepdims=True)

        @pl.when(j < i)
        def _():
            step(False)

        @pl.when(j == i)
        def _():
            step(True)
            dq_ref[...] = acc[...].astype(out_dtype)
            if has_dec:
                dd_ref[...] = dacc[...]

    qb = pl.BlockSpec((T, LANES), lambda h, t, qi, kj: (qi[t], h))
    kb = pl.BlockSpec((T, LANES), lambda h, t, qi, kj: (kj[t], h))
    colq = pl.BlockSpec((None, T, 1), lambda h, t, qi, kj: (h, qi[t], 0))
    rowk = pl.BlockSpec((None, 1, T), lambda h, t, qi, kj: (h, 0, kj[t]))
    in_specs = [qb, kb, kb, qb, colq, colq] + ([colq, rowk] if has_dec else [])
    args = (q, k, v, do, lse, delta) + ((dec_col, dec_row) if has_dec else ())
    scratch = [pltpu.VMEM((T, LANES), F32)] + ([pltpu.VMEM((T, 1), F32)] if has_dec else [])
    out_specs = (qb, colq) if has_dec else qb
    out_shape = jax.ShapeDtypeStruct((S, W), out_dtype)
    if has_dec:
        out_shape = (out_shape, jax.ShapeDtypeStruct((H, S, 1), F32))
    res = pl.pallas_call(
        body, name=name,
        grid_spec=pltpu.PrefetchScalarGridSpec(num_scalar_prefetch=2, grid=(H, npairs), in_specs=in_specs,
                                               out_specs=out_specs, scratch_shapes=scratch),
        out_shape=out_shape, compiler_params=_cparams(("parallel", "arbitrary")),
    )(qi, kj, *args)
    return res if has_dec else (res, None)


def _attn_bwd_dkv(q, k, v, do, lse_row, delta_row, dec_col, dec_row, *, unit, dk_dtype, name):
    S, W = q.shape
    H = W // LANES
    T = min(ATTN_TILE, S)
    n = S // T
    qi, kj, npairs = _pairs(n, by_key=True)
    has_dec = dec_col is not None

    def body(qi_ref, kj_ref, *refs):
        if has_dec:
            q_ref, k_ref, v_ref, do_ref, lse_ref, dl_ref, dc_ref, dr_ref, dk_ref, dv_ref, dd_ref, kacc, vacc, dacc = refs
        else:
            q_ref, k_ref, v_ref, do_ref, lse_ref, dl_ref, dk_ref, dv_ref, kacc, vacc = refs
        t = pl.program_id(1)
        i, j = qi_ref[t], kj_ref[t]

        @pl.when(i == j)
        def _():
            kacc[...] = jnp.zeros_like(kacc)
            vacc[...] = jnp.zeros_like(vacc)
            if has_dec:
                dacc[...] = jnp.zeros_like(dacc)

        def step(diag):
            qv, dov = q_ref[...], do_ref[...]
            st = lax.dot_general(k_ref[...], qv, _NT, preferred_element_type=F32)
            if has_dec:
                st = st + (dr_ref[...] - dc_ref[...])
            if diag:
                st = jnp.where(_unit_mask((T, T), unit, 0), st, NEG_INF)
            pt = jnp.exp(st - lse_ref[...])
            dpt = lax.dot_general(v_ref[...], dov, _NT, preferred_element_type=F32)
            dst = pt * (dpt - dl_ref[...])
            vacc[...] += jnp.dot(pt.astype(BF16), dov, preferred_element_type=F32)
            kacc[...] += jnp.dot(dst.astype(BF16), qv, preferred_element_type=F32)
            if has_dec:
                dacc[...] -= jnp.sum(dst, axis=1, keepdims=True)

        @pl.when(i == j)
        def _():
            step(True)

        @pl.when(i > j)
        def _():
            step(False)

        @pl.when(i == n - 1)
        def _():
            dk_ref[...] = kacc[...].astype(dk_dtype)
            dv_ref[...] = vacc[...].astype(BF16)
            if has_dec:
                dd_ref[...] = dacc[...]

    qb = pl.BlockSpec((T, LANES), lambda h, t, qi, kj: (qi[t], h))
    kb = pl.BlockSpec((T, LANES), lambda h, t, qi, kj: (kj[t], h))
    rowq = pl.BlockSpec((None, 1, T), lambda h, t, qi, kj: (h, 0, qi[t]))
    colk = pl.BlockSpec((None, T, 1), lambda h, t, qi, kj: (h, kj[t], 0))
    in_specs = [qb, kb, kb, qb, rowq, rowq] + ([colk, rowq] if has_dec else [])
    args = (q, k, v, do, lse_row, delta_row) + ((dec_col, dec_row) if has_dec else ())
    scratch = [pltpu.VMEM((T, LANES), F32)] * 2 + ([pltpu.VMEM((T, 1), F32)] if has_dec else [])
    out_specs = (kb, kb) + ((colk,) if has_dec else ())
    out_shape = (jax.ShapeDtypeStruct((S, W), dk_dtype), jax.ShapeDtypeStruct((S, W), BF16))
    if has_dec:
        out_shape = out_shape + (jax.ShapeDtypeStruct((H, S, 1), F32),)
    res = pl.pallas_call(
        body, name=name,
        grid_spec=pltpu.PrefetchScalarGridSpec(num_scalar_prefetch=2, grid=(H, npairs), in_specs=in_specs,
                                               out_specs=out_specs, scratch_shapes=scratch),
        out_shape=out_shape, compiler_params=_cparams(("parallel", "arbitrary")),
    )(qi, kj, *args)
    return res if has_dec else (res[0], res[1], None)


def _attn_bwd(q, k, v, o, lse, do, dec_col, dec_row, *, unit, dq_dtype, dk_dtype, name):
    H, S = lse.shape[0], lse.shape[1]
    delta = _attn_delta(do, o)
    dq, dd_q = _attn_bwd_dq(q, k, v, do, lse, delta, dec_col, dec_row, unit=unit, out_dtype=dq_dtype, name=name + "_dq")
    dk, dv, dd_k = _attn_bwd_dkv(q, k, v, do, lse.reshape(H, 1, S), delta.reshape(H, 1, S), dec_col, dec_row,
                                 unit=unit, dk_dtype=dk_dtype, name=name + "_dkv")
    d_dec = None if dec_col is None else dd_q + dd_k
    return dq, dk, dv, d_dec


def _merge_fwd(ya, yb, yc, gate_logit, gate_b):
    S, D = ya.shape
    tm = min(256, S)

    def body(a_ref, b_ref, c_ref, gl_ref, gb_ref, o_ref):
        g = jax.nn.sigmoid(gl_ref[...] + gb_ref[...])
        o_ref[...] = (g[:, 0:D] * a_ref[...] + g[:, D:2 * D] * b_ref[...] + g[:, 2 * D:3 * D] * c_ref[...]).astype(BF16)

    row = pl.BlockSpec((tm, D), lambda i: (i, 0))
    return pl.pallas_call(
        body, name="merge_fwd", grid=(S // tm,),
        in_specs=[row, row, row, pl.BlockSpec((tm, 3 * D), lambda i: (i, 0)), pl.BlockSpec((1, 3 * D), lambda i: (0, 0))],
        out_specs=row, out_shape=jax.ShapeDtypeStruct((S, D), BF16), compiler_params=_cparams(("parallel",)),
    )(ya, yb, yc, gate_logit, gate_b.reshape(1, 3 * D))


def _merge_bwd(dm, ya, yb, yc, gate_logit, gate_b):
    S, D = ya.shape
    tm = min(256, S)

    def body(dm_ref, a_ref, b_ref, c_ref, gl_ref, gb_ref, da_ref, db_ref, dc_ref, dgl_ref, dgb_ref):
        g = jax.nn.sigmoid(gl_ref[...] + gb_ref[...])
        dmv = dm_ref[...]
        parts = []
        for n, (y_ref, dy_ref) in enumerate(((a_ref, da_ref), (b_ref, db_ref), (c_ref, dc_ref))):
            gn = g[:, n * D:(n + 1) * D]
            dy_ref[...] = (dmv * gn).astype(BF16)
            parts.append(dmv * y_ref[...] * gn * (1.0 - gn))
        dgl = jnp.concatenate(parts, axis=1)
        dgl_ref[...] = dgl.astype(BF16)

        @pl.when(pl.program_id(0) == 0)
        def _():
            dgb_ref[...] = jnp.zeros_like(dgb_ref)

        dgb_ref[...] += jnp.sum(dgl, axis=0, keepdims=True)

    row = pl.BlockSpec((tm, D), lambda i: (i, 0))
    wide = pl.BlockSpec((tm, 3 * D), lambda i: (i, 0))
    vec = pl.BlockSpec((1, 3 * D), lambda i: (0, 0))
    act = jax.ShapeDtypeStruct((S, D), BF16)
    da, db, dc, dgl, dgb = pl.pallas_call(
        body, name="merge_bwd", grid=(S // tm,), in_specs=[row, row, row, row, wide, vec],
        out_specs=(row, row, row, wide, vec),
        out_shape=(act, act, act, jax.ShapeDtypeStruct((S, 3 * D), BF16), jax.ShapeDtypeStruct((1, 3 * D), F32)),
        compiler_params=_cparams(("arbitrary",)),
    )(dm, ya, yb, yc, gate_logit, gate_b.reshape(1, 3 * D))
    return da, db, dc, dgl, dgb.reshape(3 * D)


def _swiglu_fwd(hf):
    S, W2 = hf.shape
    F = W2 // 2
    tm = min(128, S)

    def body(h_ref, o_ref):
        gt, up = h_ref[:, 0:F], h_ref[:, F:W2]
        o_ref[...] = (gt * jax.nn.sigmoid(gt) * up).astype(BF16)

    return pl.pallas_call(
        body, name="swiglu_fwd", grid=(S // tm,), in_specs=[pl.BlockSpec((tm, W2), lambda i: (i, 0))],
        out_specs=pl.BlockSpec((tm, F), lambda i: (i, 0)), out_shape=jax.ShapeDtypeStruct((S, F), BF16),
        compiler_params=_cparams(("parallel",)),
    )(hf)


def _swiglu_bwd(dact, hf):
    S, W2 = hf.shape
    F = W2 // 2
    tm = min(128, S)

    def body(d_ref, h_ref, o_ref):
        gt, up = h_ref[:, 0:F], h_ref[:, F:W2]
        sg = jax.nn.sigmoid(gt)
        dv = d_ref[...]
        o_ref[:, 0:F] = (dv * up * sg * (1.0 + gt * (1.0 - sg))).astype(BF16)
        o_ref[:, F:W2] = (dv * gt * sg).astype(BF16)

    return pl.pallas_call(
        body, name="swiglu_bwd", grid=(S // tm,),
        in_specs=[pl.BlockSpec((tm, F), lambda i: (i, 0)), pl.BlockSpec((tm, W2), lambda i: (i, 0))],
        out_specs=pl.BlockSpec((tm, W2), lambda i: (i, 0)), out_shape=jax.ShapeDtypeStruct((S, W2), BF16),
        compiler_params=_cparams(("parallel",)),
    )(dact, hf)


def _ple_fwd(x, pre, e):
    S, D = x.shape
    tm = _rows(S)

    def body(x_ref, p_ref, e_ref, o_ref):
        o_ref[...] = x_ref[...] + jax.nn.sigmoid(p_ref[...]) * e_ref[...]

    row = pl.BlockSpec((tm, D), lambda i: (i, 0))
    return pl.pallas_call(body, name="ple_fwd", grid=(S // tm,), in_specs=[row, row, row], out_specs=row,
                          out_shape=jax.ShapeDtypeStruct((S, D), F32), compiler_params=_cparams(("parallel",)))(x, pre, e)


def _ple_bwd(dx, pre, e):
    S, D = dx.shape
    tm = _rows(S)

    def body(dx_ref, p_ref, e_ref, dp_ref, de_ref):
        pg = jax.nn.sigmoid(p_ref[...])
        dxv = dx_ref[...]
        dp_ref[...] = (dxv * e_ref[...] * pg * (1.0 - pg)).astype(BF16)
        de_ref[...] = (dxv * pg).astype(BF16)

    row = pl.BlockSpec((tm, D), lambda i: (i, 0))
    act = jax.ShapeDtypeStruct((S, D), BF16)
    return pl.pallas_call(body, name="ple_bwd", grid=(S // tm,), in_specs=[row, row, row], out_specs=(row, row),
                          out_shape=(act, act), compiler_params=_cparams(("parallel",)))(dx, pre, e)


def _pad_heads(w, real):
    K = w.shape[0]
    w = w.reshape(K, HEADS, real)
    return jnp.pad(w, ((0, 0), (0, 0), (0, HEAD_PAD - real))).reshape(K, HEADS * HEAD_PAD)


def _unpad_heads(w, real):
    K = w.shape[0]
    return w.reshape(K, HEADS, HEAD_PAD)[:, :, :real].reshape(K, HEADS * real)


def _pad_head_rows(w, real):
    N = w.shape[1]
    w = w.reshape(HEADS, real, N)
    return jnp.pad(w, ((0, 0), (0, HEAD_PAD - real), (0, 0))).reshape(HEADS * HEAD_PAD, N)


def _unpad_head_rows(w, real):
    N = w.shape[1]
    return w.reshape(HEADS, HEAD_PAD, N)[:, :real].reshape(HEADS * real, N)


def _block_diag(w):
    w = w.reshape(4, 2, 64, 64)
    z = jnp.zeros((4, 64, 64), w.dtype)
    top = jnp.concatenate([w[:, 0], z], axis=2)
    bot = jnp.concatenate([z, w[:, 1]], axis=2)
    return jnp.concatenate([top, bot], axis=1)


def _block_diag_t(w):
    return jnp.stack([w[:, :64, :64], w[:, 64:, 64:]], axis=1).reshape(8, 64, 64)


_IN_SPLITS = (512, 512, 384, 288, 512, 512, 512, 8, 3072)
_IN_OFF = np.concatenate([[0], np.cumsum(_IN_SPLITS)])
_KR_OFF = 64
_SEG_NAMES = ("u", "ug", "cq", "ckv", "kr", "fq", "fk", "fv", "fl", "gate")


def _in_segments(w_in):
    c = lambda n: w_in[:, int(_IN_OFF[n]):int(_IN_OFF[n + 1])]
    kv = c(3)
    kr = jnp.pad(kv[:, MLA_KV_LORA:], ((0, 0), (_KR_OFF, LANES - _KR_OFF - MLA_ROPE)))
    fl = jnp.pad(c(7), ((0, 0), (0, LANES - HEADS)))
    fq = _pad_heads(c(4), FOX_HEAD_DIM) * jnp.asarray(FOX_SCALE, w_in.dtype)
    return [c(0), c(1), c(2), kv[:, :MLA_KV_LORA], kr, fq, _pad_heads(c(5), FOX_HEAD_DIM), _pad_heads(c(6), FOX_HEAD_DIM), fl, c(8)]


def _in_unsegment(dw_p, widths):
    offs = np.concatenate([[0], np.cumsum(widths)])
    seg = [dw_p[:, int(offs[n]):int(offs[n + 1])] for n in range(len(widths))]
    u, ug, cq, ckv, kr, fq, fk, fv, fl, gate = seg
    return jnp.concatenate([
        u, ug, cq, ckv, kr[:, _KR_OFF:_KR_OFF + MLA_ROPE], _unpad_heads(fq, FOX_HEAD_DIM) * FOX_SCALE,
        _unpad_heads(fk, FOX_HEAD_DIM), _unpad_heads(fv, FOX_HEAD_DIM), fl[:, :HEADS], gate], axis=1)


def _split_wuq(wuq):
    return _pad_heads(wuq, MLA_NOPE + MLA_ROPE)


def _split_wukv(wukv):
    w = wukv.reshape(MLA_KV_LORA, HEADS, MLA_NOPE + MLA_V)
    pad = lambda t: jnp.pad(t, ((0, 0), (0, 0), (0, HEAD_PAD - t.shape[2]))).reshape(MLA_KV_LORA, HEADS * HEAD_PAD)
    return pad(w[:, :, :MLA_NOPE]), pad(w[:, :, MLA_NOPE:])


def _merge_wukv(dk_p, dv_p):
    k = dk_p.reshape(MLA_KV_LORA, HEADS, HEAD_PAD)[:, :, :MLA_NOPE]
    v = dv_p.reshape(MLA_KV_LORA, HEADS, HEAD_PAD)[:, :, :MLA_V]
    return jnp.concatenate([k, v], axis=2).reshape(MLA_KV_LORA, HEADS * (MLA_NOPE + MLA_V))


def _heads_layout(d):
    S = d.shape[0]
    t = d[:, :HEADS].T
    return t.reshape(HEADS, S, 1), t.reshape(HEADS, 1, S)


def _layer_fwd(x, p_i, w, tabs):
    c_q, c_k, s_lo, s_hi = tabs
    sv = {"x0": x}
    segs = _in_segments(w["w_in"])
    h = _rmsnorm_fwd(x, w["mix_norm"], "mix_norm_fwd")
    z = {}
    for nm, ws in zip(_SEG_NAMES, segs):
        z[nm] = _mm(h, ws, out_dtype=BF16 if nm in ("fq", "fk", "fv") else F32, name="in_" + nm)
    sv.update(h=h, z=z)
    wa_bd, wx_bd = _block_diag(w["lru_wa"]).astype(BF16), _block_diag(w["lru_wx"]).astype(BF16)
    oa, xc, hs = _lru_fwd(z["u"], z["ug"], w["conv_w"], w["conv_b"], wa_bd, wx_bd, w["lru_ba"], w["lru_bx"], w["lru_lambda"])
    sv.update(oa=oa, xc=xc, hs=hs)
    qn = _rmsnorm_fwd(z["cq"], w["mla_q_norm"], "q_norm_fwd")
    kvn = _rmsnorm_fwd(z["ckv"], w["mla_kv_norm"], "kv_norm_fwd")
    wuq_p = _split_wuq(w["mla_wuq"])
    wk_p, wv_p = _split_wukv(w["mla_wukv"])
    qb = _rope_q(_mm(qn, wuq_p, name="mla_q"), c_q, s_lo, s_hi, transpose=False, out_dtype=BF16, name="rope_q")
    kb = _rope_k(_mm(kvn, wk_p, name="mla_k"), z["kr"], c_k, s_lo, s_hi)
    vb = _mm(kvn, wv_p, out_dtype=BF16, name="mla_v")
    ob, lse_b = _attn_fwd(qb, kb, vb, None, None, unit=64, name="mla_attn")
    sv.update(qn=qn, kvn=kvn, qb=qb, kb=kb, vb=vb, ob=ob, lse_b=lse_b)
    bf = jnp.pad(w["fox_bf"], (0, LANES - HEADS)).reshape(1, LANES)
    dec = _decay_fwd(z["fl"], bf)
    dcol, drow = _heads_layout(dec)
    oc, lse_c = _attn_fwd(z["fq"], z["fk"], z["fv"], dcol, drow, unit=1, name="fox_attn")
    sv.update(dcol=dcol, drow=drow, oc=oc, lse_c=lse_c)
    ya = _mm(oa, w["w_br_a"], name="br_a")
    yb = _mm(ob, _pad_head_rows(w["w_br_b"], MLA_V), name="br_b")
    yc = _mm(oc, _pad_head_rows(w["w_br_c"], FOX_HEAD_DIM), name="br_c")
    merged = _merge_fwd(ya, yb, yc, z["gate"], w["gate_b"])
    x1 = _mm(merged, w["w_o"], res=x, name="w_o")
    sv.update(ya=ya, yb=yb, yc=yc, merged=merged, x1=x1)
    hn = _rmsnorm_fwd(x1, w["ffn_norm"], "ffn_norm_fwd")
    hf = _mm(hn, w["w_gate_up"], name="ffn_up")
    act = _swiglu_fwd(hf)
    x2 = _mm(act, w["w_down"], res=x1, name="ffn_down")
    sv.update(hn=hn, hf=hf, act=act, x2=x2)
    pn = _rmsnorm_fwd(x2, w["ple_norm"], "ple_norm_fwd")
    pre = _mm(pn, w["w_ple_gate"], name="ple_gate")
    e = _mm(p_i, w["w_ple"], name="ple_embed")
    x3 = _ple_fwd(x2, pre, e)
    sv.update(pn=pn, pre=pre, e=e, p_i=p_i)
    return x3, sv


def _layer_bwd(dx3, w, sv, tabs):
    c_q, c_k, s_lo, s_hi = tabs
    g = {}
    z = sv["z"]
    dpre, de = _ple_bwd(dx3, sv["pre"], sv["e"])
    g["w_ple"] = _mm(sv["p_i"], de, ta=True, name="d_w_ple")
    g["w_ple_gate"] = _mm(sv["pn"], dpre, ta=True, name="d_w_ple_gate")
    dpn = _mm(dpre, w["w_ple_gate"], tb=True, name="d_pn")
    dx2, g["ple_norm"] = _rmsnorm_bwd(sv["x2"], w["ple_norm"], dpn, add=dx3, name="ple_norm_bwd")
    dact = _mm(dx2, w["w_down"], tb=True, name="d_act")
    g["w_down"] = _mm(sv["act"], dx2, ta=True, name="d_w_down")
    dhf = _swiglu_bwd(dact, sv["hf"])
    g["w_gate_up"] = _mm(sv["hn"], dhf, ta=True, name="d_w_gate_up")
    dhn = _mm(dhf, w["w_gate_up"], tb=True, name="d_hn")
    dx1, g["ffn_norm"] = _rmsnorm_bwd(sv["x1"], w["ffn_norm"], dhn, add=dx2, name="ffn_norm_bwd")
    g["w_o"] = _mm(sv["merged"], dx1, ta=True, name="d_w_o")
    dm = _mm(dx1, w["w_o"], tb=True, name="d_merged")
    dya, dyb, dyc, dgate, g["gate_b"] = _merge_bwd(dm, sv["ya"], sv["yb"], sv["yc"], z["gate"], w["gate_b"])
    wbb_p, wbc_p = _pad_head_rows(w["w_br_b"], MLA_V), _pad_head_rows(w["w_br_c"], FOX_HEAD_DIM)
    g["w_br_a"] = _mm(sv["oa"], dya, ta=True, name="d_w_br_a")
    g["w_br_b"] = _unpad_head_rows(_mm(sv["ob"], dyb, ta=True, name="d_w_br_b"), MLA_V)
    g["w_br_c"] = _unpad_head_rows(_mm(sv["oc"], dyc, ta=True, name="d_w_br_c"), FOX_HEAD_DIM)
    doa = _mm(dya, w["w_br_a"], tb=True, name="d_oa")
    dob = _mm(dyb, wbb_p, tb=True, out_dtype=BF16, name="d_ob")
    doc = _mm(dyc, wbc_p, tb=True, out_dtype=BF16, name="d_oc")
    dfq, dfk, dfv, d_dec = _attn_bwd(z["fq"], z["fk"], z["fv"], sv["oc"], sv["lse_c"], doc, sv["dcol"], sv["drow"],
                                     unit=1, dq_dtype=BF16, dk_dtype=BF16, name="fox_attn_bwd")
    S = dx3.shape[0]
    d_dec = jnp.pad(d_dec.reshape(HEADS, S).T, ((0, 0), (0, LANES - HEADS)))
    bf = jnp.pad(w["fox_bf"], (0, LANES - HEADS)).reshape(1, LANES)
    dfl, dbf = _decay_bwd(d_dec, z["fl"], bf)
    g["fox_bf"] = dbf[0, :HEADS]
    dqb, dkb, dvb, _ = _attn_bwd(sv["qb"], sv["kb"], sv["vb"], sv["ob"], sv["lse_b"], dob, None, None,
                                 unit=64, dq_dtype=F32, dk_dtype=F32, name="mla_attn_bwd")
    wuq_p = _split_wuq(w["mla_wuq"])
    wk_p, wv_p = _split_wukv(w["mla_wukv"])
    dq_pre = _rope_q(dqb, c_q, s_lo, s_hi, transpose=True, out_dtype=BF16, name="rope_q_bwd")
    dkr = _rope_k_bwd(dkb, c_k, s_lo, s_hi)
    g["mla_wuq"] = _unpad_heads(_mm(sv["qn"], dq_pre, ta=True, name="d_wuq"), MLA_NOPE + MLA_ROPE)
    g["mla_wukv"] = _merge_wukv(_mm(sv["kvn"], dkb, ta=True, name="d_wuk"), _mm(sv["kvn"], dvb, ta=True, name="d_wuv"))
    dqn = _mm(dq_pre, wuq_p, tb=True, name="d_qn")
    dkvn = _mm(dvb, wv_p, tb=True, res=_mm(dkb, wk_p, tb=True, name="d_kvn_k"), name="d_kvn")
    dcq, g["mla_q_norm"] = _rmsnorm_bwd(z["cq"], w["mla_q_norm"], dqn, out_dtype=BF16, name="q_norm_bwd")
    dckv, g["mla_kv_norm"] = _rmsnorm_bwd(z["ckv"], w["mla_kv_norm"], dkvn, out_dtype=BF16, name="kv_norm_bwd")
    wa_bd, wx_bd = _block_diag(w["lru_wa"]).astype(BF16), _block_diag(w["lru_wx"]).astype(BF16)
    du, dug, dcw, dcb, dba, dbx, dlam, dwa, dwx = _lru_bwd(
        doa, z["u"], z["ug"], sv["xc"], sv["hs"], w["conv_w"], wa_bd, wx_bd, w["lru_ba"], w["lru_bx"], w["lru_lambda"])
    g["conv_w"], g["conv_b"], g["lru_ba"], g["lru_bx"] = dcw, dcb[0], dba[0], dbx[0]
    g["lru_lambda"] = dlam[0] * LRU_C * jax.nn.sigmoid(-w["lru_lambda"])
    g["lru_wa"], g["lru_wx"] = _block_diag_t(dwa), _block_diag_t(dwx)
    dsegs = [du, dug, dcq, dckv, dkr, dfq, dfk, dfv, dfl, dgate]
    dz = jnp.concatenate(dsegs, axis=1)
    w_in_p = jnp.concatenate(_in_segments(w["w_in"]), axis=1)
    g["w_in"] = _in_unsegment(_mm(sv["h"], dz, ta=True, name="d_w_in"), [d.shape[1] for d in dsegs])
    dh = _mm(dz, w_in_p, tb=True, name="d_h")
    dx0, g["mix_norm"] = _rmsnorm_bwd(sv["x0"], w["mix_norm"], dh, add=dx1, name="mix_norm_bwd")
    return dx0, g


_LAYER_WEIGHTS = ("mix_norm", "w_in", "gate_b", "conv_w", "conv_b", "lru_wa", "lru_ba", "lru_wx", "lru_bx", "lru_lambda",
                  "mla_q_norm", "mla_wuq", "mla_kv_norm", "mla_wukv", "fox_bf", "w_br_a", "w_br_b", "w_br_c", "w_o",
                  "ffn_norm", "w_gate_up", "w_down", "ple_norm", "w_ple_gate", "w_ple")
_BIG = ("w_in", "mla_wuq", "mla_wukv", "w_br_a", "w_br_b", "w_br_c", "w_o", "w_gate_up", "w_down", "w_ple_gate", "w_ple")
_ROW_SHARDED = ("w_o", "w_down", "w_ple_gate")
_SMALL = ("mix_norm", "gate_b", "conv_b", "lru_wa", "lru_ba", "lru_wx", "lru_bx", "lru_lambda", "mla_q_norm", "mla_kv_norm",
          "fox_bf", "ffn_norm", "ple_norm")


def _local_step(x, p, layers, final_norm, target):
    tabs = _rope_tables(x.shape[0])
    saved = []
    for i in range(DEPTH):
        x, sv = _layer_fwd(x, p[i], layers[i], tabs)
        saved.append(sv)
    loss, dx, d_final = _loss_head(x, final_norm, target)
    grads = [None] * DEPTH
    for i in reversed(range(DEPTH)):
        dx, grads[i] = _layer_bwd(dx, layers[i], saved[i], tabs)
    return loss, dx, grads, d_final


def _hbm():
    return pl.BlockSpec(memory_space=pltpu.HBM)


def _peers(x, y):
    return [(1 - x, y), (x, 1 - y), (1 - x, 1 - y)]


def _gather_chips(shard, name):
    R, W = shard.shape

    def body(src_ref, out_ref, send_sems, recv_sems, local_sem):
        x, y, c = lax.axis_index("x"), lax.axis_index("y"), lax.axis_index("c")
        me = 2 * x + y
        mine = pltpu.make_async_copy(src_ref, out_ref.at[me], local_sem)
        mine.start()

        def copy(j, slot, to):
            return pltpu.make_async_remote_copy(src_ref=src_ref, dst_ref=out_ref.at[slot], send_sem=send_sems.at[j],
                                                recv_sem=recv_sems.at[j], device_id=(to[0], to[1], c), device_id_type=MESH)

        sends = [copy(j, me, peer) for j, peer in enumerate(_peers(x, y))]
        for cp in sends:
            cp.start()
        for j, peer in enumerate(_peers(x, y)):
            copy(j, 2 * peer[0] + peer[1], peer).wait_recv()
        for cp in sends:
            cp.wait_send()
        mine.wait()

    return pl.pallas_call(
        body, name=name, in_specs=[_hbm()], out_specs=_hbm(), out_shape=jax.ShapeDtypeStruct((4, R, W), shard.dtype),
        scratch_shapes=[pltpu.SemaphoreType.DMA((3,)), pltpu.SemaphoreType.DMA((3,)), pltpu.SemaphoreType.DMA],
    )(shard)


def _pair_swap_halves(g4):
    n, R, W = g4.shape
    Rh = R // 2

    def body(src_ref, out_ref, send_sem, recv_sem):
        x, y, c = lax.axis_index("x"), lax.axis_index("y"), lax.axis_index("c")
        cp = pltpu.make_async_remote_copy(src_ref=src_ref.at[:, pl.ds((1 - c) * Rh, Rh), :], dst_ref=out_ref, send_sem=send_sem,
                                          recv_sem=recv_sem, device_id=(x, y, 1 - c), device_id_type=MESH)
        cp.start()
        cp.wait()

    return pl.pallas_call(
        body, name="grad_pair_swap", in_specs=[_hbm()], out_specs=_hbm(), out_shape=jax.ShapeDtypeStruct((n, Rh, W), g4.dtype),
        scratch_shapes=[pltpu.SemaphoreType.DMA, pltpu.SemaphoreType.DMA],
    )(g4)


def _pair_add(g4, sib, c_arr):
    n, R, W = g4.shape
    Rh = R // 2
    tr = _tile_rows(Rh)
    nb = Rh // tr

    def body(c_ref, a_ref, b_ref, o_ref):
        o_ref[...] = a_ref[...] + b_ref[...]

    return pl.pallas_call(
        body, name="grad_pair_add",
        grid_spec=pltpu.PrefetchScalarGridSpec(
            num_scalar_prefetch=1, grid=(n, nb),
            in_specs=[pl.BlockSpec((None, tr, W), lambda s, i, c: (s, c[0] * nb + i, 0)), pl.BlockSpec((None, tr, W), lambda s, i, c: (s, i, 0))],
            out_specs=pl.BlockSpec((None, tr, W), lambda s, i, c: (s, i, 0))),
        out_shape=jax.ShapeDtypeStruct((n, Rh, W), F32), compiler_params=_cparams(("parallel", "parallel")),
    )(c_arr, g4, sib)


def _tile_rows(n):
    for t in (512, 480, 400, 320, 256, 240, 160, 128, 80, 64, 40, 32, 16, 8):
        if n % t == 0:
            return t
    return n


def _chips_exchange(part):
    n, Rh, W = part.shape

    def body(src_ref, out_ref, send_sems, recv_sems):
        x, y, c = lax.axis_index("x"), lax.axis_index("y"), lax.axis_index("c")

        def copy(j, to):
            return pltpu.make_async_remote_copy(src_ref=src_ref.at[2 * to[0] + to[1]], dst_ref=out_ref.at[j], send_sem=send_sems.at[j],
                                                recv_sem=recv_sems.at[j], device_id=(to[0], to[1], c), device_id_type=MESH)

        cps = [copy(j, peer) for j, peer in enumerate(_peers(x, y))]
        for cp in cps:
            cp.start()
        for cp in cps:
            cp.wait()

    return pl.pallas_call(
        body, name="grad_chips_exchange", in_specs=[_hbm()], out_specs=_hbm(), out_shape=jax.ShapeDtypeStruct((3, Rh, W), part.dtype),
        scratch_shapes=[pltpu.SemaphoreType.DMA((3,)), pltpu.SemaphoreType.DMA((3,))],
    )(part)


def _chips_add(part, got, k_arr):
    n, Rh, W = part.shape
    tr = _tile_rows(Rh)

    def body(k_ref, a_ref, b_ref, o_ref):
        o_ref[...] = ((a_ref[...] + b_ref[0]) + b_ref[1]) + b_ref[2]

    return pl.pallas_call(
        body, name="grad_chips_add",
        grid_spec=pltpu.PrefetchScalarGridSpec(
            num_scalar_prefetch=1, grid=(Rh // tr,),
            in_specs=[pl.BlockSpec((None, tr, W), lambda i, k: (k[0], i, 0)), pl.BlockSpec((3, tr, W), lambda i, k: (0, i, 0))],
            out_specs=pl.BlockSpec((tr, W), lambda i, k: (i, 0))),
        out_shape=jax.ShapeDtypeStruct((Rh, W), F32), compiler_params=_cparams(("parallel",)),
    )(k_arr, part, got)


def _pair_gather(half):
    Rh, W = half.shape

    def body(src_ref, out_ref, send_sem, recv_sem, local_sem):
        x, y, c = lax.axis_index("x"), lax.axis_index("y"), lax.axis_index("c")
        mine = pltpu.make_async_copy(src_ref, out_ref.at[c], local_sem)
        mine.start()
        pltpu.make_async_remote_copy(src_ref=src_ref, dst_ref=out_ref.at[c], send_sem=send_sem, recv_sem=recv_sem,
                                     device_id=(x, y, 1 - c), device_id_type=MESH).start()
        pltpu.make_async_remote_copy(src_ref=src_ref, dst_ref=out_ref.at[1 - c], send_sem=send_sem, recv_sem=recv_sem,
                                     device_id=(x, y, 1 - c), device_id_type=MESH).wait()
        mine.wait()

    return pl.pallas_call(
        body, name="grad_pair_gather", in_specs=[_hbm()], out_specs=_hbm(), out_shape=jax.ShapeDtypeStruct((2, Rh, W), half.dtype),
        scratch_shapes=[pltpu.SemaphoreType.DMA, pltpu.SemaphoreType.DMA, pltpu.SemaphoreType.DMA],
    )(half)


def _gather_all(buf):
    R, W = buf.shape

    def body(src_ref, out_ref, send_sems, recv_sems, local_sem):
        x, y, c = lax.axis_index("x"), lax.axis_index("y"), lax.axis_index("c")
        me = 4 * x + 2 * y + c
        mine = pltpu.make_async_copy(src_ref, out_ref.at[me], local_sem)
        mine.start()
        rel = [((x + (r >> 2 & 1)) % 2, (y + (r >> 1 & 1)) % 2, (c + (r & 1)) % 2) for r in range(1, 8)]

        def copy(j, slot, to):
            return pltpu.make_async_remote_copy(src_ref=src_ref, dst_ref=out_ref.at[slot], send_sem=send_sems.at[j],
                                                recv_sem=recv_sems.at[j], device_id=to, device_id_type=MESH)

        sends = [copy(j, me, to) for j, to in enumerate(rel)]
        for cp in sends:
            cp.start()
        for j, to in enumerate(rel):
            copy(j, 4 * to[0] + 2 * to[1] + to[2], to).wait_recv()
        for cp in sends:
            cp.wait_send()
        mine.wait()

    return pl.pallas_call(
        body, name="small_gather", in_specs=[_hbm()], out_specs=_hbm(), out_shape=jax.ShapeDtypeStruct((8, R, W), buf.dtype),
        scratch_shapes=[pltpu.SemaphoreType.DMA((7,)), pltpu.SemaphoreType.DMA((7,)), pltpu.SemaphoreType.DMA],
    )(buf)


def _sum_slots(stack):
    n, R, W = stack.shape
    tr = _tile_rows(R)

    def body(s_ref, o_ref):
        tot = s_ref[0]
        for j in range(1, n):
            tot = tot + s_ref[j]
        o_ref[...] = tot

    return pl.pallas_call(
        body, name="small_sum", grid=(R // tr,), in_specs=[pl.BlockSpec((n, tr, W), lambda i: (0, i, 0))],
        out_specs=pl.BlockSpec((tr, W), lambda i: (i, 0)), out_shape=jax.ShapeDtypeStruct((R, W), F32),
        compiler_params=_cparams(("parallel",)),
    )(stack)


def _adamw(wp, gp, mp, vp, name):
    R, W = wp.shape
    tr = _tile_rows(R)
    c1 = 1.0 - ADAM_B1 ** ADAM_STEP
    c2 = 1.0 - ADAM_B2 ** ADAM_STEP

    def body(w_ref, g_ref, m_ref, v_ref, d_ref, mo_ref, vo_ref):
        gv = g_ref[...]
        m = ADAM_B1 * m_ref[...] + (1.0 - ADAM_B1) * gv
        v = ADAM_B2 * v_ref[...] + (1.0 - ADAM_B2) * (gv * gv)
        m_hat = m / c1
        v_hat = v / c2
        d_ref[...] = -ADAM_LR * (m_hat / (jnp.sqrt(v_hat) + ADAM_EPS) + ADAM_WD * w_ref[...])
        mo_ref[...] = m
        vo_ref[...] = v

    blk = pl.BlockSpec((tr, W), lambda i: (i, 0))
    shp = jax.ShapeDtypeStruct((R, W), F32)
    return pl.pallas_call(body, name=name, grid=(R // tr,), in_specs=[blk] * 4, out_specs=(blk,) * 3, out_shape=(shp,) * 3,
                          compiler_params=_cparams(("parallel",)))(wp, gp, mp, vp)


def _pack(arrs, rows):
    flat = jnp.concatenate([a.reshape(-1) for a in arrs])
    return jnp.pad(flat, (0, rows * PACK_W - flat.shape[0])).reshape(rows, PACK_W)


def _unpack(buf, shapes):
    flat = buf.reshape(-1)
    out, off = [], 0
    for shp in shapes:
        n = int(np.prod(shp))
        out.append(flat[off:off + n].reshape(shp))
        off += n
    return out


def _rows_for(shapes, mult):
    n = sum(int(np.prod(s)) for s in shapes)
    rows = -(-n // PACK_W)
    return -(-rows // mult) * mult


def _shard_of(full, name, k):
    ax = 0 if name in _ROW_SHARDED else full.ndim - 1
    n = full.shape[ax] // 4
    return lax.slice_in_dim(full, k * n, (k + 1) * n, axis=ax)


def _join_shards(blocks, name):
    ax = 0 if name in _ROW_SHARDED else blocks[0].ndim - 1
    return jnp.concatenate(blocks, axis=ax)


def kernel(x, p, mix_norm, w_in, gate_b, conv_w, conv_b, lru_wa, lru_ba, lru_wx, lru_bx, lru_lambda, mla_q_norm, mla_wuq, mla_kv_norm, mla_wukv, fox_bf, w_br_a, w_br_b, w_br_c, w_o, ffn_norm, w_gate_up, w_down, ple_norm, w_ple_gate, w_ple, final_norm, loss_target, m_mix_norm, m_w_in, m_gate_b, m_conv_w, m_conv_b, m_lru_wa, m_lru_ba, m_lru_wx, m_lru_bx, m_lru_lambda, m_mla_q_norm, m_mla_wuq, m_mla_kv_norm, m_mla_wukv, m_fox_bf, m_w_br_a, m_w_br_b, m_w_br_c, m_w_o, m_ffn_norm, m_w_gate_up, m_w_down, m_ple_norm, m_w_ple_gate, m_w_ple, m_final_norm, v_mix_norm, v_w_in, v_gate_b, v_conv_w, v_conv_b, v_lru_wa, v_lru_ba, v_lru_wx, v_lru_bx, v_lru_lambda, v_mla_q_norm, v_mla_wuq, v_mla_kv_norm, v_mla_wukv, v_fox_bf, v_w_br_a, v_w_br_b, v_w_br_c, v_w_o, v_ffn_norm, v_w_gate_up, v_w_down, v_ple_norm, v_w_ple_gate, v_w_ple, v_final_norm):
    a = dict(locals())
    names = list(_LAYER_WEIGHTS) + ["final_norm"]
    W = {n: a[n] for n in names}
    M = {n: a["m_" + n] for n in names}
    V = {n: a["v_" + n] for n in names}
    ix, iy, ic = lax.axis_index("x"), lax.axis_index("y"), lax.axis_index("c")

    sharded = [(n, i) for i in range(DEPTH) for n in _BIG] + [("conv_w", None)]
    pick = lambda src, n, i: src[n] if i is None else src[n][i]
    shard_shapes = [pick(W, n, i).shape for n, i in sharded]
    R = _rows_for(shard_shapes, 64)
    w_pack = _pack([pick(W, n, i) for n, i in sharded], R)
    gathered = _gather_chips(w_pack.astype(BF16), "weight_gather")
    per_chip = [_unpack(gathered[k], shard_shapes) for k in range(4)]
    full = {}
    for j, (n, i) in enumerate(sharded):
        blocks = [per_chip[k][j] for k in range(4)]
        if i is None:
            full[(n, None)] = jnp.concatenate(blocks, axis=-1)
        else:
            full[(n, i)] = _join_shards(blocks, n)
    conv_blocks = _gather_chips(conv_w.reshape(DEPTH * CONV_WIDTH, LANES), "conv_w_gather")
    conv_w_full = jnp.concatenate([conv_blocks[k].reshape(DEPTH, CONV_WIDTH, LANES) for k in range(4)], axis=-1)
    layers = []
    for i in range(DEPTH):
        lw = {n: W[n][i] for n in _SMALL}
        for n in _BIG:
            lw[n] = full[(n, i)]
        lw["conv_w"] = conv_w_full[i]
        layers.append(lw)

    loss_sum, dx, grads, d_final = _local_step(x[0], p[:, 0], layers, final_norm, loss_target[0])
    loss = lax.psum(loss_sum, ("x", "y", "c"))

    def grad_of(n, i):
        return jnp.stack([grads[l]["conv_w"] for l in range(DEPTH)]) if i is None else grads[i][n]

    g4 = jnp.stack([_pack([(_shard_of(grad_of(n, i), n, k)) for n, i in sharded], R) for k in range(4)])
    c_arr = jnp.reshape(ic, (1,)).astype(jnp.int32)
    k_arr = jnp.reshape(2 * ix + iy, (1,)).astype(jnp.int32)
    pair = _pair_add(g4, _pair_swap_halves(g4), c_arr)
    mine = _chips_add(pair, _chips_exchange(pair), k_arr)
    g_pack = _pair_gather(mine).reshape(R, PACK_W)
    m_pack = _pack([pick(M, n, i) for n, i in sharded], R)
    v_pack = _pack([pick(V, n, i) for n, i in sharded], R)
    d_pack, nm_pack, nv_pack = _adamw(w_pack, g_pack, m_pack, v_pack, "adamw_sharded")
    big_out = {}
    for key, buf in (("g", g_pack), ("d", d_pack), ("m", nm_pack), ("v", nv_pack)):
        for (n, i), arr in zip(sharded, _unpack(buf, shard_shapes)):
            big_out[(key, n, i)] = arr

    small = [(n, i) for i in range(DEPTH) for n in _SMALL] + [("final_norm", None)]
    small_shapes = [pick(W, n, i).shape for n, i in small]
    Rs = _rows_for(small_shapes, 8)
    sg = _pack([d_final if i is None else grads[i][n] for n, i in small], Rs)
    sg = _sum_slots(_gather_all(sg))
    sw = _pack([pick(W, n, i) for n, i in small], Rs)
    sm = _pack([pick(M, n, i) for n, i in small], Rs)
    sv_ = _pack([pick(V, n, i) for n, i in small], Rs)
    sd, snm, snv = _adamw(sw, sg, sm, sv_, "adamw_replicated")
    small_out = {}
    for key, buf in (("g", sg), ("d", sd), ("m", snm), ("v", snv)):
        for (n, i), arr in zip(small, _unpack(buf, small_shapes)):
            small_out[(key, n, i)] = arr

    def assemble(key, n):
        if n == "final_norm":
            return small_out[(key, n, None)]
        if n == "conv_w":
            return big_out[(key, n, None)]
        src = big_out if n in _BIG else small_out
        return jnp.stack([src[(key, n, i)] for i in range(DEPTH)])

    outs = [loss, dx[None]]
    for key in ("g", "d", "m", "v"):
        outs += [assemble(key, n) for n in names]
    return tuple(outs)
```

```python
import functools
import math

import numpy as np
import jax
import jax.numpy as jnp
from jax import lax
from jax.experimental import pallas as pl
from jax.experimental.pallas import tpu as pltpu

F32, BF16 = jnp.float32, jnp.bfloat16
MESH = pl.DeviceIdType.MESH

D_MODEL = 1024
DEPTH = 2
EPS = 1e-6
NEG_INF = -1e30
LRU_WIDTH = 512
LRU_HEADS = 8
LRU_C = 8.0
CONV_WIDTH = 4
HEADS = 8
MLA_Q_LORA = 384
MLA_KV_LORA = 256
MLA_NOPE = 64
MLA_ROPE = 32
MLA_V = 64
ROPE_BASE = 10000.0
FOX_HEAD_DIM = 64
D_FF = 2816
PLE_DIM = 256
HEAD_PAD = 128
MLA_SCALE = (MLA_NOPE + MLA_ROPE) ** -0.5
FOX_SCALE = FOX_HEAD_DIM ** -0.5

ADAM_LR, ADAM_B1, ADAM_B2, ADAM_EPS, ADAM_WD, ADAM_STEP = 0.001, 0.9, 0.999, 1e-08, 0.01, 10

VMEM_LIMIT_BYTES = 48 * 1024 * 1024
LANES = 128
PACK_W = 1024

ROW_TILE = 512
ATTN_TILE = 1024
LRU_CHUNK = 512


def _cparams(dims):
    return pltpu.CompilerParams(dimension_semantics=dims, vmem_limit_bytes=VMEM_LIMIT_BYTES)


def _tile(n, cap):
    if n <= cap:
        return n
    t = (cap // LANES) * LANES
    while t >= LANES:
        if n % t == 0:
            return t
        t -= LANES
    raise ValueError(f"no tile for {n} under {cap}")


def _rows(n):
    return min(ROW_TILE, n)


def _mm(a, b, *, ta=False, tb=False, out_dtype=F32, res=None, bias=None, name):
    K, M = a.shape if ta else a.shape[::-1]
    N, K2 = b.shape if tb else b.shape[::-1]
    assert K == K2, (name, a.shape, b.shape)
    assert res is None or bias is None
    tm, tn, tk = _tile(M, 512), _tile(N, 1792), _tile(K, 1408)
    nk = K // tk
    a_spec = pl.BlockSpec((tk, tm), lambda i, j, k: (k, i)) if ta else pl.BlockSpec((tm, tk), lambda i, j, k: (i, k))
    b_spec = pl.BlockSpec((tn, tk), lambda i, j, k: (j, k)) if tb else pl.BlockSpec((tk, tn), lambda i, j, k: (k, j))
    o_spec = pl.BlockSpec((tm, tn), lambda i, j, k: (i, j))
    dn = (((0,) if ta else (1,), (1,) if tb else (0,)), ((), ()))
    if bias is not None:
        res, r_spec = bias, pl.BlockSpec((1, tn), lambda i, j, k: (0, j))
    else:
        r_spec = o_spec
    has_res = res is not None

    def body(*refs):
        a_ref, b_ref = refs[0], refs[1]
        r_ref = refs[2] if has_res else None
        o_ref = refs[3] if has_res else refs[2]
        av, bv = a_ref[...], b_ref[...]
        if av.dtype != BF16:
            av = av.astype(BF16)
        if bv.dtype != BF16:
            bv = bv.astype(BF16)
        part = lax.dot_general(av, bv, dn, preferred_element_type=F32)

        def finish(total):
            if has_res:
                total = total + r_ref[...]
            o_ref[...] = total.astype(out_dtype)

        if nk == 1:
            finish(part)
        else:
            acc = refs[-1]
            k = pl.program_id(2)

            @pl.when(k == 0)
            def _():
                acc[...] = part

            @pl.when(k > 0)
            def _():
                acc[...] += part

            @pl.when(k == nk - 1)
            def _():
                finish(acc[...])

    in_specs = [a_spec, b_spec] + ([r_spec] if has_res else [])
    args = (a, b) + ((res,) if has_res else ())
    return pl.pallas_call(
        body, name=name, grid=(M // tm, N // tn, nk), in_specs=in_specs, out_specs=o_spec,
        out_shape=jax.ShapeDtypeStruct((M, N), out_dtype),
        scratch_shapes=[pltpu.VMEM((tm, tn), F32)] if nk > 1 else [],
        compiler_params=_cparams(("parallel", "parallel", "arbitrary")),
    )(*args)


def _rmsnorm_fwd(x, g, name):
    S, W = x.shape
    tm = _rows(S)

    def body(x_ref, g_ref, o_ref):
        xf = x_ref[...]
        rstd = lax.rsqrt(jnp.mean(xf * xf, axis=1, keepdims=True) + EPS)
        o_ref[...] = (xf * rstd * g_ref[...]).astype(BF16)

    return pl.pallas_call(
        body, name=name, grid=(S // tm,),
        in_specs=[pl.BlockSpec((tm, W), lambda i: (i, 0)), pl.BlockSpec((1, W), lambda i: (0, 0))],
        out_specs=pl.BlockSpec((tm, W), lambda i: (i, 0)),
        out_shape=jax.ShapeDtypeStruct((S, W), BF16), compiler_params=_cparams(("parallel",)),
    )(x, g.reshape(1, W))


def _rmsnorm_bwd(x, g, dy, *, add=None, out_dtype=F32, name):
    S, W = x.shape
    tm = _rows(S)
    has_add = add is not None

    def body(*refs):
        x_ref, g_ref, dy_ref = refs[:3]
        add_ref = refs[3] if has_add else None
        dx_ref, dg_ref = refs[-2], refs[-1]
        xf = x_ref[...]
        rstd = lax.rsqrt(jnp.mean(xf * xf, axis=1, keepdims=True) + EPS)
        xhat = xf * rstd
        dyv = dy_ref[...]
        dxh = dyv * g_ref[...]
        dx = rstd * (dxh - xhat * jnp.mean(dxh * xhat, axis=1, keepdims=True))
        if has_add:
            dx = dx + add_ref[...]
        dx_ref[...] = dx.astype(out_dtype)

        @pl.when(pl.program_id(0) == 0)
        def _():
            dg_ref[...] = jnp.zeros_like(dg_ref)

        dg_ref[...] += jnp.sum(dyv * xhat, axis=0, keepdims=True)

    row = pl.BlockSpec((tm, W), lambda i: (i, 0))
    vec = pl.BlockSpec((1, W), lambda i: (0, 0))
    dx, dg = pl.pallas_call(
        body, name=name, grid=(S // tm,),
        in_specs=[row, vec, row] + ([row] if has_add else []),
        out_specs=(row, vec),
        out_shape=(jax.ShapeDtypeStruct((S, W), out_dtype), jax.ShapeDtypeStruct((1, W), F32)),
        compiler_params=_cparams(("arbitrary",)),
    )(x, g.reshape(1, W), dy, *((add,) if has_add else ()))
    return dx, dg.reshape(W)


def _loss_head(x, g, target):
    S, W = x.shape
    tm = _rows(S)

    def body(x_ref, g_ref, t_ref, loss_ref, dx_ref, dg_ref):
        xf = x_ref[...]
        gv = g_ref[...]
        rstd = lax.rsqrt(jnp.mean(xf * xf, axis=1, keepdims=True) + EPS)
        xhat = xf * rstd
        err = xhat * gv - t_ref[...]
        part = 0.5 * jnp.sum(jnp.mean(err * err, axis=1, keepdims=True), axis=0, keepdims=True)
        dyv = err * (1.0 / W)
        dxh = dyv * gv
        dx_ref[...] = rstd * (dxh - xhat * jnp.mean(dxh * xhat, axis=1, keepdims=True))

        @pl.when(pl.program_id(0) == 0)
        def _():
            dg_ref[...] = jnp.zeros_like(dg_ref)
            loss_ref[...] = jnp.zeros_like(loss_ref)

        dg_ref[...] += jnp.sum(dyv * xhat, axis=0, keepdims=True)
        loss_ref[...] += part

    row = pl.BlockSpec((tm, W), lambda i: (i, 0))
    vec = pl.BlockSpec((1, W), lambda i: (0, 0))
    loss, dx, dg = pl.pallas_call(
        body, name="loss_head", grid=(S // tm,), in_specs=[row, vec, row],
        out_specs=(pl.BlockSpec((1, 1), lambda i: (0, 0)), row, vec),
        out_shape=(jax.ShapeDtypeStruct((1, 1), F32), jax.ShapeDtypeStruct((S, W), F32), jax.ShapeDtypeStruct((1, W), F32)),
        compiler_params=_cparams(("arbitrary",)),
    )(x, g.reshape(1, W), target)
    return loss[0, 0], dx, dg.reshape(W)


def _scan_fwd(a, b, row):
    T = a.shape[0]
    d = 1
    while d < T:
        keep = row >= d
        b = jnp.where(keep, a * pltpu.roll(b, d, axis=0) + b, b)
        a = jnp.where(keep, a * pltpu.roll(a, d, axis=0), a)
        d *= 2
    return a, b


def _scan_bwd(a, b, row):
    T = a.shape[0]
    d = 1
    while d < T:
        keep = row < T - d
        b = jnp.where(keep, a * pltpu.roll(b, T - d, axis=0) + b, b)
        a = jnp.where(keep, a * pltpu.roll(a, T - d, axis=0), a)
        d *= 2
    return a, b


def _expm1(x):
    small = x * (1.0 + x * (0.5 + x * (1.0 / 6 + x * (1.0 / 24 + x * (1.0 / 120 + x * (1.0 / 720 + x * (1.0 / 5040)))))))
    return jnp.where(jnp.abs(x) < 0.25, small, jnp.exp(x) - 1.0)


_GELU_C = math.sqrt(2.0 / math.pi)


def _gelu_and_grad(x):
    inner = _GELU_C * (x + 0.044715 * x * x * x)
    th = jnp.tanh(inner)
    val = 0.5 * x * (1.0 + th)
    grad = 0.5 * (1.0 + th) + 0.5 * x * (1.0 - th * th) * _GELU_C * (1.0 + 3 * 0.044715 * x * x)
    return val, grad


def _lru_gates(xc, wa, wx, ba, bx, lam):
    xcb = xc.astype(BF16)
    r = jax.nn.sigmoid(jnp.dot(xcb, wa, preferred_element_type=F32) + ba)
    ig = jax.nn.sigmoid(jnp.dot(xcb, wx, preferred_element_type=F32) + bx)
    sp = jax.nn.softplus(-lam)
    log_a = -LRU_C * r * sp
    a = jnp.exp(log_a)
    mult = jnp.sqrt(-_expm1(2.0 * log_a))
    return xcb, r, ig, sp, a, mult


def _lru_fwd(u, ug, conv_w, conv_b, wa_bd, wx_bd, ba, bx, lam):
    S, W = u.shape
    T = min(LRU_CHUNK, S)
    nl, nc = W // LANES, S // T

    def body(u_ref, ug_ref, cw_ref, cb_ref, wa_ref, wx_ref, ba_ref, bx_ref, lam_ref, ya_ref, xc_ref, h_ref, prev_u, h_carry):
        c = pl.program_id(1)

        @pl.when(c == 0)
        def _():
            prev_u[...] = jnp.zeros_like(prev_u)
            h_carry[...] = jnp.zeros_like(h_carry)

        uv = u_ref[...]
        row = lax.broadcasted_iota(jnp.int32, (T, LANES), 0)
        row8 = lax.broadcasted_iota(jnp.int32, (8, LANES), 0)
        cw = cw_ref[...]
        xc = cb_ref[...] + uv * cw[3:4, :]
        pv = prev_u[...]
        for k in range(1, CONV_WIDTH):
            us = pltpu.roll(uv, k, axis=0)
            top = jnp.where(row8 < k, pltpu.roll(pv, k, axis=0), us[0:8])
            us = jnp.concatenate([top, us[8:]], axis=0)
            xc = xc + us * cw[3 - k:4 - k, :]
        prev_u[...] = uv[T - 8:T]
        _, r, ig, sp, a, mult = _lru_gates(xc, wa_ref[...], wx_ref[...], ba_ref[...], bx_ref[...], lam_ref[...])
        bb = mult * (ig * xc)
        aa, hh = _scan_fwd(a, bb, row)
        h = hh + aa * h_carry[7:8, :]
        h_carry[...] = h[T - 8:T]
        gl, _ = _gelu_and_grad(ug_ref[...])
        ya_ref[...] = (h * gl).astype(BF16)
        xc_ref[...] = xc
        h_ref[...] = h

    seq = pl.BlockSpec((T, LANES), lambda l, c: (c, l))
    vec = pl.BlockSpec((1, LANES), lambda l, c: (0, l))
    mat = pl.BlockSpec((None, LANES, LANES), lambda l, c: (l, 0, 0))
    return pl.pallas_call(
        body, name="lru_fwd", grid=(nl, nc),
        in_specs=[seq, seq, pl.BlockSpec((CONV_WIDTH, LANES), lambda l, c: (0, l)), vec, mat, mat, vec, vec, vec],
        out_specs=(seq, seq, seq),
        out_shape=(jax.ShapeDtypeStruct((S, W), BF16), jax.ShapeDtypeStruct((S, W), F32), jax.ShapeDtypeStruct((S, W), F32)),
        scratch_shapes=[pltpu.VMEM((8, LANES), F32), pltpu.VMEM((8, LANES), F32)],
        compiler_params=_cparams(("parallel", "arbitrary")),
    )(u, ug, conv_w, conv_b.reshape(1, W), wa_bd, wx_bd, ba.reshape(1, W), bx.reshape(1, W), lam.reshape(1, W))


def _lru_bwd(dya, u, ug, xc, h, conv_w, wa_bd, wx_bd, ba, bx, lam):
    S, W = u.shape
    T = min(LRU_CHUNK, S)
    nl, nc = W // LANES, S // T
    tb8 = T // 8

    def body(dya_ref, u_ref, ug_ref, xc_ref, h_ref, hp_ref, cw_ref, wa_ref, wx_ref, ba_ref, bx_ref, lam_ref,
             du_ref, dug_ref, dcw_ref, dcb_ref, dba_ref, dbx_ref, dlam_ref, dwa_ref, dwx_ref,
             g_next, a_next, dxc_next):
        c = pl.program_id(1)

        @pl.when(c == 0)
        def _():
            g_next[...] = jnp.zeros_like(g_next)
            a_next[...] = jnp.zeros_like(a_next)
            dxc_next[...] = jnp.zeros_like(dxc_next)
            for ref in (dcw_ref, dcb_ref, dba_ref, dbx_ref, dlam_ref, dwa_ref, dwx_ref):
                ref[...] = jnp.zeros_like(ref)

        row = lax.broadcasted_iota(jnp.int32, (T, LANES), 0)
        row8 = lax.broadcasted_iota(jnp.int32, (8, LANES), 0)
        xcv = xc_ref[...]
        wa, wx = wa_ref[...], wx_ref[...]
        xcb, r, ig, sp, a, mult = _lru_gates(xcv, wa, wx, ba_ref[...], bx_ref[...], lam_ref[...])
        gl, dgl = _gelu_and_grad(ug_ref[...])
        dyav = dya_ref[...]
        hv = h_ref[...]
        dug_ref[...] = (dyav * hv * dgl).astype(BF16)
        dh = dyav * gl
        a_up = pltpu.roll(a, T - 1, axis=0)
        a_up = jnp.where(row == T - 1, a_next[0:1, :], a_up)
        prod, gg = _scan_bwd(a_up, dh, row)
        g = gg + prod * g_next[0:1, :]
        h_prev = pltpu.roll(hv, 1, axis=0)
        first_chunk = c == nc - 1
        h_before = jnp.where(first_chunk, 0.0, hp_ref[7:8, :])
        h_prev = jnp.where(row == 0, h_before, h_prev)
        da = g * h_prev
        d_mult = g * (ig * xcv)
        d_ig = g * mult * xcv
        dxc = g * mult * ig
        d_log_a = da * a - d_mult * (a * a) / mult
        d_r = d_log_a * (-LRU_C * sp)
        d_pa = d_r * r * (1.0 - r)
        d_px = d_ig * ig * (1.0 - ig)
        d_pab, d_pxb = d_pa.astype(BF16), d_px.astype(BF16)
        nt = (((1,), (1,)), ((), ()))
        tn = (((0,), (0,)), ((), ()))
        dxc = dxc + lax.dot_general(d_pab, wa, nt, preferred_element_type=F32) + lax.dot_general(d_pxb, wx, nt, preferred_element_type=F32)
        dwa_ref[...] += lax.dot_general(xcb, d_pab, tn, preferred_element_type=F32)
        dwx_ref[...] += lax.dot_general(xcb, d_pxb, tn, preferred_element_type=F32)
        dlam_ref[...] += jnp.sum(d_log_a * r, axis=0, keepdims=True)
        dba_ref[...] += jnp.sum(d_pa, axis=0, keepdims=True)
        dbx_ref[...] += jnp.sum(d_px, axis=0, keepdims=True)
        dcb_ref[...] += jnp.sum(dxc, axis=0, keepdims=True)
        uv = u_ref[...]
        cw = cw_ref[...]
        nxt = dxc_next[...]
        du = dxc * cw[3:4, :]
        dcw_ref[3:4, :] += jnp.sum(uv * dxc, axis=0, keepdims=True)
        for k in range(1, CONV_WIDTH):
            ds = pltpu.roll(dxc, T - k, axis=0)
            bot = jnp.where(row8 >= 8 - k, pltpu.roll(nxt, 8 - k, axis=0), ds[T - 8:T])
            ds = jnp.concatenate([ds[:T - 8], bot], axis=0)
            du = du + ds * cw[3 - k:4 - k, :]
            dcw_ref[3 - k:4 - k, :] += jnp.sum(uv * ds, axis=0, keepdims=True)
        du_ref[...] = du.astype(BF16)
        g_next[...] = g[0:8]
        a_next[...] = a[0:8]
        dxc_next[...] = dxc[0:8]

    seq = pl.BlockSpec((T, LANES), lambda l, c: (nc - 1 - c, l))
    before = pl.BlockSpec((8, LANES), lambda l, c: (jnp.maximum((nc - 1 - c) * tb8 - 1, 0), l))
    vec = pl.BlockSpec((1, LANES), lambda l, c: (0, l))
    cwb = pl.BlockSpec((CONV_WIDTH, LANES), lambda l, c: (0, l))
    mat = pl.BlockSpec((None, LANES, LANES), lambda l, c: (l, 0, 0))
    vshape = jax.ShapeDtypeStruct((1, W), F32)
    mshape = jax.ShapeDtypeStruct((nl, LANES, LANES), F32)
    return pl.pallas_call(
        body, name="lru_bwd", grid=(nl, nc),
        in_specs=[seq, seq, seq, seq, seq, before, cwb, mat, mat, vec, vec, vec],
        out_specs=(seq, seq, cwb, vec, vec, vec, vec, mat, mat),
        out_shape=(jax.ShapeDtypeStruct((S, W), BF16), jax.ShapeDtypeStruct((S, W), BF16),
                   jax.ShapeDtypeStruct((CONV_WIDTH, W), F32), vshape, vshape, vshape, vshape, mshape, mshape),
        scratch_shapes=[pltpu.VMEM((8, LANES), F32)] * 3,
        compiler_params=_cparams(("parallel", "arbitrary")),
    )(dya, u, ug, xc, h, h, conv_w, wa_bd, wx_bd, ba.reshape(1, W), bx.reshape(1, W), lam.reshape(1, W))


def _decay_fwd(f_logit, bf):
    S = f_logit.shape[0]
    T = min(LRU_CHUNK, S)

    def body(f_ref, b_ref, o_ref, carry):
        @pl.when(pl.program_id(0) == 0)
        def _():
            carry[...] = jnp.zeros_like(carry)

        row = lax.broadcasted_iota(jnp.int32, (T, LANES), 0)
        v = jax.nn.log_sigmoid(f_ref[...] + b_ref[...])
        d = 1
        while d < T:
            v = jnp.where(row >= d, v + pltpu.roll(v, d, axis=0), v)
            d *= 2
        v = v + carry[7:8, :]
        carry[...] = v[T - 8:T]
        o_ref[...] = v

    return pl.pallas_call(
        body, name="decay_fwd", grid=(S // T,),
        in_specs=[pl.BlockSpec((T, LANES), lambda c: (c, 0)), pl.BlockSpec((1, LANES), lambda c: (0, 0))],
        out_specs=pl.BlockSpec((T, LANES), lambda c: (c, 0)),
        out_shape=jax.ShapeDtypeStruct((S, LANES), F32), scratch_shapes=[pltpu.VMEM((8, LANES), F32)],
        compiler_params=_cparams(("arbitrary",)),
    )(f_logit, bf)


def _decay_bwd(d_dec, f_logit, bf):
    S = f_logit.shape[0]
    T = min(LRU_CHUNK, S)
    nc = S // T

    def body(dd_ref, f_ref, b_ref, df_ref, db_ref, carry):
        @pl.when(pl.program_id(0) == 0)
        def _():
            carry[...] = jnp.zeros_like(carry)
            db_ref[...] = jnp.zeros_like(db_ref)

        row = lax.broadcasted_iota(jnp.int32, (T, LANES), 0)
        v = dd_ref[...]
        d = 1
        while d < T:
            v = jnp.where(row < T - d, v + pltpu.roll(v, T - d, axis=0), v)
            d *= 2
        v = v + carry[0:1, :]
        carry[...] = v[0:8]
        df = v * jax.nn.sigmoid(-(f_ref[...] + b_ref[...]))
        df_ref[...] = df.astype(BF16)
        db_ref[...] += jnp.sum(df, axis=0, keepdims=True)

    seq = pl.BlockSpec((T, LANES), lambda c: (nc - 1 - c, 0))
    vec = pl.BlockSpec((1, LANES), lambda c: (0, 0))
    return pl.pallas_call(
        body, name="decay_bwd", grid=(nc,), in_specs=[seq, seq, vec], out_specs=(seq, vec),
        out_shape=(jax.ShapeDtypeStruct((S, LANES), BF16), jax.ShapeDtypeStruct((1, LANES), F32)),
        scratch_shapes=[pltpu.VMEM((8, LANES), F32)], compiler_params=_cparams(("arbitrary",)),
    )(d_dec, f_logit, bf)


def _rope_tables(S):
    pos = jnp.arange(S, dtype=F32)
    inv_freq = ROPE_BASE ** (-jnp.arange(0, MLA_ROPE, 2, dtype=F32) / MLA_ROPE)
    ang = pos[:, None] * inv_freq[None, :]
    cos, sin = jnp.cos(ang), jnp.sin(ang)
    half = MLA_ROPE // 2
    z = lambda n: jnp.zeros((S, n), F32)
    c_q = jnp.concatenate([jnp.ones((S, MLA_NOPE), F32), cos, cos, z(HEAD_PAD - MLA_NOPE - MLA_ROPE)], axis=1)
    c_k = jnp.concatenate([z(MLA_NOPE), cos, cos, z(HEAD_PAD - MLA_NOPE - MLA_ROPE)], axis=1)
    s_lo = jnp.concatenate([z(MLA_NOPE), -sin, z(HEAD_PAD - MLA_NOPE - half)], axis=1)
    s_hi = jnp.concatenate([z(MLA_NOPE + half), sin, z(HEAD_PAD - MLA_NOPE - MLA_ROPE)], axis=1)
    return c_q, c_k, s_lo, s_hi


def _rot(v, c, s_lo, s_hi):
    half = MLA_ROPE // 2
    return v * c + pltpu.roll(v, LANES - half, axis=1) * s_lo + pltpu.roll(v, half, axis=1) * s_hi


def _rot_t(dv, c, s_lo, s_hi):
    half = MLA_ROPE // 2
    return dv * c + pltpu.roll(dv * s_lo, half, axis=1) + pltpu.roll(dv * s_hi, LANES - half, axis=1)


def _rope_q(q_pre, c_q, s_lo, s_hi, *, transpose, out_dtype, name):
    S, W = q_pre.shape
    tm = _rows(S)
    fn = _rot_t if transpose else _rot

    def body(q_ref, c_ref, lo_ref, hi_ref, o_ref):
        o_ref[...] = (fn(q_ref[...] * MLA_SCALE, c_ref[...], lo_ref[...], hi_ref[...])).astype(out_dtype)

    blk = pl.BlockSpec((tm, LANES), lambda i, h: (i, h))
    tab = pl.BlockSpec((tm, LANES), lambda i, h: (i, 0))
    return pl.pallas_call(
        body, name=name, grid=(S // tm, W // LANES), in_specs=[blk, tab, tab, tab], out_specs=blk,
        out_shape=jax.ShapeDtypeStruct((S, W), out_dtype), compiler_params=_cparams(("parallel", "parallel")),
    )(q_pre, c_q, s_lo, s_hi)


def _rope_k(k_pre, k_rope, c_k, s_lo, s_hi):
    S, W = k_pre.shape
    tm = _rows(S)

    def body(k_ref, r_ref, c_ref, lo_ref, hi_ref, o_ref):
        o_ref[...] = (k_ref[...] + _rot(r_ref[...], c_ref[...], lo_ref[...], hi_ref[...])).astype(BF16)

    blk = pl.BlockSpec((tm, LANES), lambda i, h: (i, h))
    tab = pl.BlockSpec((tm, LANES), lambda i, h: (i, 0))
    return pl.pallas_call(
        body, name="rope_k", grid=(S // tm, W // LANES), in_specs=[blk, tab, tab, tab, tab], out_specs=blk,
        out_shape=jax.ShapeDtypeStruct((S, W), BF16), compiler_params=_cparams(("parallel", "parallel")),
    )(k_pre, k_rope, c_k, s_lo, s_hi)


def _rope_k_bwd(dk, c_k, s_lo, s_hi):
    S, W = dk.shape
    tm = _rows(S)

    def body(dk_ref, c_ref, lo_ref, hi_ref, o_ref):
        tot = dk_ref[:, 0:LANES]
        for hd in range(1, W // LANES):
            tot = tot + dk_ref[:, hd * LANES:(hd + 1) * LANES]
        o_ref[...] = _rot_t(tot, c_ref[...], lo_ref[...], hi_ref[...]).astype(BF16)

    tab = pl.BlockSpec((tm, LANES), lambda i: (i, 0))
    return pl.pallas_call(
        body, name="rope_k_bwd", grid=(S // tm,), in_specs=[pl.BlockSpec((tm, W), lambda i: (i, 0)), tab, tab, tab],
        out_specs=tab, out_shape=jax.ShapeDtypeStruct((S, LANES), BF16), compiler_params=_cparams(("parallel",)),
    )(dk, c_k, s_lo, s_hi)


def _pairs(n, by_key):
    if by_key:
        pr = [(i, j) for j in range(n) for i in range(j, n)]
    else:
        pr = [(i, j) for i in range(n) for j in range(i + 1)]
    return (jnp.asarray(np.array([p[0] for p in pr], np.int32)), jnp.asarray(np.array([p[1] for p in pr], np.int32)), len(pr))


def _unit_mask(shape, unit, key_axis):
    q = lax.broadcasted_iota(jnp.int32, shape, 1 - key_axis)
    k = lax.broadcasted_iota(jnp.int32, shape, key_axis)
    if unit > 1:
        q, k = q // unit, k // unit
    return q >= k


_NT = (((1,), (1,)), ((), ()))


def _attn_fwd(q, k, v, dec_col, dec_row, *, unit, name):
    S, W = q.shape
    H = W // LANES
    T = min(ATTN_TILE, S)
    n = S // T
    qi, kj, npairs = _pairs(n, by_key=False)
    has_dec = dec_col is not None

    def body(qi_ref, kj_ref, *refs):
        if has_dec:
            q_ref, k_ref, v_ref, dc_ref, dr_ref, o_ref, lse_ref, m_s, l_s, acc = refs
        else:
            q_ref, k_ref, v_ref, o_ref, lse_ref, m_s, l_s, acc = refs
        t = pl.program_id(1)
        i, j = qi_ref[t], kj_ref[t]

        @pl.when(j == 0)
        def _():
            m_s[...] = jnp.full_like(m_s, NEG_INF)
            l_s[...] = jnp.zeros_like(l_s)
            acc[...] = jnp.zeros_like(acc)

        def step(diag):
            s = lax.dot_general(q_ref[...], k_ref[...], _NT, preferred_element_type=F32)
            if has_dec:
                s = s + (dc_ref[...] - dr_ref[...])
            if diag:
                s = jnp.where(_unit_mask((T, T), unit, 1), s, NEG_INF)
            m_prev = m_s[...]
            m_new = jnp.maximum(m_prev, jnp.max(s, axis=1, keepdims=True))
            alpha = jnp.exp(m_prev - m_new)
            p = jnp.exp(s - m_new)
            l_s[...] = alpha * l_s[...] + jnp.sum(p, axis=1, keepdims=True)
            acc[...] = alpha * acc[...] + jnp.dot(p.astype(BF16), v_ref[...], preferred_element_type=F32)
            m_s[...] = m_new

        @pl.when(j < i)
        def _():
            step(False)

        @pl.when(j == i)
        def _():
            step(True)
            o_ref[...] = (acc[...] / l_s[...]).astype(BF16)
            lse_ref[...] = m_s[...] + jnp.log(l_s[...])

    qb = pl.BlockSpec((T, LANES), lambda h, t, qi, kj: (qi[t], h))
    kb = pl.BlockSpec((T, LANES), lambda h, t, qi, kj: (kj[t], h))
    colq = pl.BlockSpec((None, T, 1), lambda h, t, qi, kj: (h, qi[t], 0))
    rowk = pl.BlockSpec((None, 1, T), lambda h, t, qi, kj: (h, 0, kj[t]))
    in_specs = [qb, kb, kb] + ([colq, rowk] if has_dec else [])
    args = (q, k, v) + ((dec_col, dec_row) if has_dec else ())
    return pl.pallas_call(
        body, name=name,
        grid_spec=pltpu.PrefetchScalarGridSpec(
            num_scalar_prefetch=2, grid=(H, npairs), in_specs=in_specs, out_specs=(qb, colq),
            scratch_shapes=[pltpu.VMEM((T, 1), F32), pltpu.VMEM((T, 1), F32), pltpu.VMEM((T, LANES), F32)]),
        out_shape=(jax.ShapeDtypeStruct((S, W), BF16), jax.ShapeDtypeStruct((H, S, 1), F32)),
        compiler_params=_cparams(("parallel", "arbitrary")),
    )(qi, kj, *args)


def _attn_delta(do, o):
    S, W = o.shape
    H = W // LANES
    tm = _rows(S)

    def body(do_ref, o_ref, d_ref):
        d_ref[...] = jnp.sum(do_ref[...].astype(F32) * o_ref[...].astype(F32), axis=1, keepdims=True)

    blk = pl.BlockSpec((tm, LANES), lambda h, i: (i, h))
    return pl.pallas_call(
        body, name="attn_delta", grid=(H, S // tm), in_specs=[blk, blk],
        out_specs=pl.BlockSpec((None, tm, 1), lambda h, i: (h, i, 0)),
        out_shape=jax.ShapeDtypeStruct((H, S, 1), F32), compiler_params=_cparams(("parallel", "parallel")),
    )(do, o)


def _attn_bwd_dq(q, k, v, do, lse, delta, dec_col, dec_row, *, unit, out_dtype, name):
    S, W = q.shape
    H = W // LANES
    T = min(ATTN_TILE, S)
    n = S // T
    qi, kj, npairs = _pairs(n, by_key=False)
    has_dec = dec_col is not None

    def body(qi_ref, kj_ref, *refs):
        if has_dec:
            q_ref, k_ref, v_ref, do_ref, lse_ref, dl_ref, dc_ref, dr_ref, dq_ref, dd_ref, acc, dacc = refs
        else:
            q_ref, k_ref, v_ref, do_ref, lse_ref, dl_ref, dq_ref, acc = refs
        t = pl.program_id(1)
        i, j = qi_ref[t], kj_ref[t]

        @pl.when(j == 0)
        def _():
            acc[...] = jnp.zeros_like(acc)
            if has_dec:
                dacc[...] = jnp.zeros_like(dacc)

        def step(diag):
            kv = k_ref[...]
            s = lax.dot_general(q_ref[...], kv, _NT, preferred_element_type=F32)
            if has_dec:
                s = s + (dc_ref[...] - dr_ref[...])
            if diag:
                s = jnp.where(_unit_mask((T, T), unit, 1), s, NEG_INF)
            p = jnp.exp(s - lse_ref[...])
            dp = lax.dot_general(do_ref[...], v_ref[...], _NT, preferred_element_type=F32)
            ds = p * (dp - dl_ref[...])
            acc[...] += jnp.dot(ds.astype(BF16), kv, preferred_element_type=F32)
            if has_dec:
                dacc[...] += jnp.sum(ds, axis=1, keepdims=True)

        @pl.when(j < i)
        def _():
            step(False)

        @pl.when(j == i)
        def _():
            step(True)
            dq_ref[...] = acc[...].astype(out_dtype)
            if has_dec:
                dd_ref[...] = dacc[...]

    qb = pl.BlockSpec((T, LANES), lambda h, t, qi, kj: (qi[t], h))
    kb = pl.BlockSpec((T, LANES), lambda h, t, qi, kj: (kj[t], h))
    colq = pl.BlockSpec((None, T, 1), lambda h, t, qi, kj: (h, qi[t], 0))
    rowk = pl.BlockSpec((None, 1, T), lambda h, t, qi, kj: (h, 0, kj[t]))
    in_specs = [qb, kb, kb, qb, colq, colq] + ([colq, rowk] if has_dec else [])
    args = (q, k, v, do, lse, delta) + ((dec_col, dec_row) if has_dec else ())
    scratch = [pltpu.VMEM((T, LANES), F32)] + ([pltpu.VMEM((T, 1), F32)] if has_dec else [])
    out_specs = (qb, colq) if has_dec else qb
    out_shape = jax.ShapeDtypeStruct((S, W), out_dtype)
    if has_dec:
        out_shape = (out_shape, jax.ShapeDtypeStruct((H, S, 1), F32))
    res = pl.pallas_call(
        body, name=name,
        grid_spec=pltpu.PrefetchScalarGridSpec(num_scalar_prefetch=2, grid=(H, npairs), in_specs=in_specs,
                                               out_specs=out_specs, scratch_shapes=scratch),
        out_shape=out_shape, compiler_params=_cparams(("parallel", "arbitrary")),
    )(qi, kj, *args)
    return res if has_dec else (res, None)


def _attn_bwd_dkv(q, k, v, do, lse_row, delta_row, dec_col, dec_row, *, unit, dk_dtype, name):
    S, W = q.shape
    H = W // LANES
    T = min(ATTN_TILE, S)
    n = S // T
    qi, kj, npairs = _pairs(n, by_key=True)
    has_dec = dec_col is not None

    def body(qi_ref, kj_ref, *refs):
        if has_dec:
            q_ref, k_ref, v_ref, do_ref, lse_ref, dl_ref, dc_ref, dr_ref, dk_ref, dv_ref, dd_ref, kacc, vacc, dacc = refs
        else:
            q_ref, k_ref, v_ref, do_ref, lse_ref, dl_ref, dk_ref, dv_ref, kacc, vacc = refs
        t = pl.program_id(1)
        i, j = qi_ref[t], kj_ref[t]

        @pl.when(i == j)
        def _():
            kacc[...] = jnp.zeros_like(kacc)
            vacc[...] = jnp.zeros_like(vacc)
            if has_dec:
                dacc[...] = jnp.zeros_like(dacc)

        def step(diag):
            qv, dov = q_ref[...], do_ref[...]
            st = lax.dot_general(k_ref[...], qv, _NT, preferred_element_type=F32)
            if has_dec:
                st = st + (dr_ref[...] - dc_ref[...])
            if diag:
                st = jnp.where(_unit_mask((T, T), unit, 0), st, NEG_INF)
            pt = jnp.exp(st - lse_ref[...])
            dpt = lax.dot_general(v_ref[...], dov, _NT, preferred_element_type=F32)
            dst = pt * (dpt - dl_ref[...])
            vacc[...] += jnp.dot(pt.astype(BF16), dov, preferred_element_type=F32)
            kacc[...] += jnp.dot(dst.astype(BF16), qv, preferred_element_type=F32)
            if has_dec:
                dacc[...] -= jnp.sum(dst, axis=1, keepdims=True)

        @pl.when(i == j)
        def _():
            step(True)

        @pl.when(i > j)
        def _():
            step(False)

        @pl.when(i == n - 1)
        def _():
            dk_ref[...] = kacc[...].astype(dk_dtype)
            dv_ref[...] = vacc[...].astype(BF16)
            if has_dec:
                dd_ref[...] = dacc[...]

    qb = pl.BlockSpec((T, LANES), lambda h, t, qi, kj: (qi[t], h))
    kb = pl.BlockSpec((T, LANES), lambda h, t, qi, kj: (kj[t], h))
    rowq = pl.BlockSpec((None, 1, T), lambda h, t, qi, kj: (h, 0, qi[t]))
    colk = pl.BlockSpec((None, T, 1), lambda h, t, qi, kj: (h, kj[t], 0))
    in_specs = [qb, kb, kb, qb, rowq, rowq] + ([colk, rowq] if has_dec else [])
    args = (q, k, v, do, lse_row, delta_row) + ((dec_col, dec_row) if has_dec else ())
    scratch = [pltpu.VMEM((T, LANES), F32)] * 2 + ([pltpu.VMEM((T, 1), F32)] if has_dec else [])
    out_specs = (kb, kb) + ((colk,) if has_dec else ())
    out_shape = (jax.ShapeDtypeStruct((S, W), dk_dtype), jax.ShapeDtypeStruct((S, W), BF16))
    if has_dec:
        out_shape = out_shape + (jax.ShapeDtypeStruct((H, S, 1), F32),)
    res = pl.pallas_call(
        body, name=name,
        grid_spec=pltpu.PrefetchScalarGridSpec(num_scalar_prefetch=2, grid=(H, npairs), in_specs=in_specs,
                                               out_specs=out_specs, scratch_shapes=scratch),
        out_shape=out_shape, compiler_params=_cparams(("parallel", "arbitrary")),
    )(qi, kj, *args)
    return res if has_dec else (res[0], res[1], None)


def _attn_bwd(q, k, v, o, lse, do, dec_col, dec_row, *, unit, dq_dtype, dk_dtype, name):
    H, S = lse.shape[0], lse.shape[1]
    delta = _attn_delta(do, o)
    dq, dd_q = _attn_bwd_dq(q, k, v, do, lse, delta, dec_col, dec_row, unit=unit, out_dtype=dq_dtype, name=name + "_dq")
    dk, dv, dd_k = _attn_bwd_dkv(q, k, v, do, lse.reshape(H, 1, S), delta.reshape(H, 1, S), dec_col, dec_row,
                                 unit=unit, dk_dtype=dk_dtype, name=name + "_dkv")
    d_dec = None if dec_col is None else dd_q + dd_k
    return dq, dk, dv, d_dec


ONES_LANE = 64


def _ones_lane_bias():
    one = np.zeros((HEADS, HEAD_PAD), np.float32)
    one[:, ONES_LANE] = 1.0
    return jnp.asarray(one.reshape(1, HEADS * HEAD_PAD))


def _lane_sum(t):
    tot = t[:, 0:LANES]
    for c in range(1, t.shape[1] // LANES):
        tot = tot + t[:, c * LANES:(c + 1) * LANES]
    return tot


def _fa_fwd(q, k, v, dec_row, *, unit, name):
    S, W = q.shape
    H = W // LANES
    T = min(ATTN_TILE, S)
    n, reps = S // T, T // LANES
    qi, kj, npairs = _pairs(n, by_key=False)
    has_dec = dec_row is not None

    def body(qi_ref, kj_ref, *refs):
        if has_dec:
            q_ref, k_ref, v_ref, dr_ref, o_ref, lse_ref, m_s, acc = refs
        else:
            q_ref, k_ref, v_ref, o_ref, lse_ref, m_s, acc = refs
        t = pl.program_id(1)
        i, j = qi_ref[t], kj_ref[t]

        @pl.when(j == 0)
        def _():
            m_s[...] = jnp.full_like(m_s, NEG_INF)
            acc[...] = jnp.zeros_like(acc)

        def step(diag):
            s = lax.dot_general(q_ref[...], k_ref[...], _NT, preferred_element_type=F32)
            if has_dec:
                s = s - dr_ref[...]
            if diag:
                s = jnp.where(_unit_mask((T, T), unit, 1), s, NEG_INF)
            m_prev = m_s[...]
            m_new = jnp.maximum(m_prev, jnp.max(s, axis=1, keepdims=True))
            alpha = jnp.exp(m_prev - m_new)
            p = jnp.exp(s - jnp.tile(m_new, (1, reps)))
            acc[...] = alpha * acc[...] + jnp.dot(p.astype(BF16), v_ref[...], preferred_element_type=F32)
            m_s[...] = m_new

        @pl.when(j < i)
        def _():
            step(False)

        @pl.when(j == i)
        def _():
            step(True)
            av = acc[...]
            l = av[:, ONES_LANE:ONES_LANE + 1]
            lane = lax.broadcasted_iota(jnp.int32, (T, LANES), 1)
            o_ref[...] = jnp.where(lane < ONES_LANE, av / l, 0.0).astype(BF16)
            lse_ref[...] = m_s[...] + jnp.log(l)

    qb = pl.BlockSpec((T, LANES), lambda h, t, qi, kj: (qi[t], h))
    kb = pl.BlockSpec((T, LANES), lambda h, t, qi, kj: (kj[t], h))
    repq = pl.BlockSpec((None, T, LANES), lambda h, t, qi, kj: (h, qi[t], 0))
    rowk = pl.BlockSpec((None, 1, T), lambda h, t, qi, kj: (h, 0, kj[t]))
    in_specs = [qb, kb, kb] + ([rowk] if has_dec else [])
    args = (q, k, v) + ((dec_row,) if has_dec else ())
    return pl.pallas_call(
        body, name=name,
        grid_spec=pltpu.PrefetchScalarGridSpec(
            num_scalar_prefetch=2, grid=(H, npairs), in_specs=in_specs, out_specs=(qb, repq),
            scratch_shapes=[pltpu.VMEM((T, LANES), F32), pltpu.VMEM((T, LANES), F32)]),
        out_shape=(jax.ShapeDtypeStruct((S, W), BF16), jax.ShapeDtypeStruct((H, S, LANES), F32)),
        compiler_params=_cparams(("parallel", "arbitrary")),
    )(qi, kj, *args)


def _fa_delta(do, o):
    S, W = o.shape
    H = W // LANES
    tm = _rows(S)

    def body(do_ref, o_ref, d_ref):
        d = jnp.sum(do_ref[...].astype(F32) * o_ref[...].astype(F32), axis=1, keepdims=True)
        d_ref[...] = jnp.broadcast_to(d, (tm, LANES))

    blk = pl.BlockSpec((tm, LANES), lambda h, i: (i, h))
    return pl.pallas_call(
        body, name="attn_delta", grid=(H, S // tm), in_specs=[blk, blk],
        out_specs=pl.BlockSpec((None, tm, LANES), lambda h, i: (h, i, 0)),
        out_shape=jax.ShapeDtypeStruct((H, S, LANES), F32), compiler_params=_cparams(("parallel", "parallel")),
    )(do, o)


def _fa_bwd_dq(q, k, v, do, lse, delta, dec_row, *, unit, out_dtype, name):
    S, W = q.shape
    H = W // LANES
    T = min(ATTN_TILE, S)
    n, reps = S // T, T // LANES
    qi, kj, npairs = _pairs(n, by_key=False)
    has_dec = dec_row is not None

    def body(qi_ref, kj_ref, *refs):
        if has_dec:
            q_ref, k_ref, v_ref, do_ref, lse_ref, dl_ref, dr_ref, dq_ref, dd_ref, acc, dacc = refs
        else:
            q_ref, k_ref, v_ref, do_ref, lse_ref, dl_ref, dq_ref, acc = refs
        t = pl.program_id(1)
        i, j = qi_ref[t], kj_ref[t]

        @pl.when(j == 0)
        def _():
            acc[...] = jnp.zeros_like(acc)
            if has_dec:
                dacc[...] = jnp.zeros_like(dacc)

        def step(diag):
            kv = k_ref[...]
            s = lax.dot_general(q_ref[...], kv, _NT, preferred_element_type=F32)
            if has_dec:
                s = s - dr_ref[...]
            if diag:
                s = jnp.where(_unit_mask((T, T), unit, 1), s, NEG_INF)
            p = jnp.exp(s - jnp.tile(lse_ref[...], (1, reps)))
            dp = lax.dot_general(do_ref[...], v_ref[...], _NT, preferred_element_type=F32)
            ds = p * (dp - jnp.tile(dl_ref[...], (1, reps)))
            acc[...] += jnp.dot(ds.astype(BF16), kv, preferred_element_type=F32)
            if has_dec:
                dacc[...] += _lane_sum(ds)

        @pl.when(j < i)
        def _():
            step(False)

        @pl.when(j == i)
        def _():
            step(True)
            dq_ref[...] = acc[...].astype(out_dtype)
            if has_dec:
                dd_ref[...] = jnp.broadcast_to(jnp.sum(dacc[...], axis=1, keepdims=True), (T, LANES))

    qb = pl.BlockSpec((T, LANES), lambda h, t, qi, kj: (qi[t], h))
    kb = pl.BlockSpec((T, LANES), lambda h, t, qi, kj: (kj[t], h))
    repq = pl.BlockSpec((None, T, LANES), lambda h, t, qi, kj: (h, qi[t], 0))
    rowk = pl.BlockSpec((None, 1, T), lambda h, t, qi, kj: (h, 0, kj[t]))
    in_specs = [qb, kb, kb, qb, repq, repq] + ([rowk] if has_dec else [])
    args = (q, k, v, do, lse, delta) + ((dec_row,) if has_dec else ())
    out_shape = jax.ShapeDtypeStruct((S, W), out_dtype)
    res = pl.pallas_call(
        body, name=name,
        grid_spec=pltpu.PrefetchScalarGridSpec(num_scalar_prefetch=2, grid=(H, npairs), in_specs=in_specs,
                                               out_specs=(qb, repq) if has_dec else qb,
                                               scratch_shapes=[pltpu.VMEM((T, LANES), F32)] * (2 if has_dec else 1)),
        out_shape=(out_shape, jax.ShapeDtypeStruct((H, S, LANES), F32)) if has_dec else out_shape,
        compiler_params=_cparams(("parallel", "arbitrary")),
    )(qi, kj, *args)
    return res if has_dec else (res, None)


def _fa_bwd_dkv(q, k, v, do, lse_row, delta_row, dec_rep, *, unit, dk_dtype, name):
    S, W = q.shape
    H = W // LANES
    T = min(ATTN_TILE, S)
    n, reps = S // T, T // LANES
    qi, kj, npairs = _pairs(n, by_key=True)
    has_dec = dec_rep is not None

    def body(qi_ref, kj_ref, *refs):
        if has_dec:
            q_ref, k_ref, v_ref, do_ref, lse_ref, dl_ref, dc_ref, dk_ref, dv_ref, dd_ref, kacc, vacc, dacc = refs
        else:
            q_ref, k_ref, v_ref, do_ref, lse_ref, dl_ref, dk_ref, dv_ref, kacc, vacc = refs
        t = pl.program_id(1)
        i, j = qi_ref[t], kj_ref[t]

        @pl.when(i == j)
        def _():
            kacc[...] = jnp.zeros_like(kacc)
            vacc[...] = jnp.zeros_like(vacc)
            if has_dec:
                dacc[...] = jnp.zeros_like(dacc)

        def step(diag):
            qv, dov = q_ref[...], do_ref[...]
            st = lax.dot_general(k_ref[...], qv, _NT, preferred_element_type=F32)
            if has_dec:
                st = st - jnp.tile(dc_ref[...], (1, reps))
            if diag:
                st = jnp.where(_unit_mask((T, T), unit, 0), st, NEG_INF)
            pt = jnp.exp(st - lse_ref[...])
            dpt = lax.dot_general(v_ref[...], dov, _NT, preferred_element_type=F32)
            dst = pt * (dpt - dl_ref[...])
            vacc[...] += jnp.dot(pt.astype(BF16), dov, preferred_element_type=F32)
            kacc[...] += jnp.dot(dst.astype(BF16), qv, preferred_element_type=F32)
            if has_dec:
                dacc[...] += _lane_sum(dst)

        @pl.when(i == j)
        def _():
            step(True)

        @pl.when(i > j)
        def _():
            step(False)

        @pl.when(i == n - 1)
        def _():
            dk_ref[...] = kacc[...].astype(dk_dtype)
            dv_ref[...] = vacc[...].astype(BF16)
            if has_dec:
                dd_ref[...] = jnp.broadcast_to(-jnp.sum(dacc[...], axis=1, keepdims=True), (T, LANES))

    qb = pl.BlockSpec((T, LANES), lambda h, t, qi, kj: (qi[t], h))
    kb = pl.BlockSpec((T, LANES), lambda h, t, qi, kj: (kj[t], h))
    rowq = pl.BlockSpec((None, 1, T), lambda h, t, qi, kj: (h, 0, qi[t]))
    repk = pl.BlockSpec((None, T, LANES), lambda h, t, qi, kj: (h, kj[t], 0))
    in_specs = [qb, kb, kb, qb, rowq, rowq] + ([repk] if has_dec else [])
    args = (q, k, v, do, lse_row, delta_row) + ((dec_rep,) if has_dec else ())
    scratch = [pltpu.VMEM((T, LANES), F32)] * (3 if has_dec else 2)
    out_specs = (kb, kb) + ((repk,) if has_dec else ())
    out_shape = (jax.ShapeDtypeStruct((S, W), dk_dtype), jax.ShapeDtypeStruct((S, W), BF16))
    if has_dec:
        out_shape = out_shape + (jax.ShapeDtypeStruct((H, S, LANES), F32),)
    res = pl.pallas_call(
        body, name=name,
        grid_spec=pltpu.PrefetchScalarGridSpec(num_scalar_prefetch=2, grid=(H, npairs), in_specs=in_specs,
                                               out_specs=out_specs, scratch_shapes=scratch),
        out_shape=out_shape, compiler_params=_cparams(("parallel", "arbitrary")),
    )(qi, kj, *args)
    return res if has_dec else (res[0], res[1], None)


def _fa_bwd(q, k, v, o, lse, do, dec_row, dec_rep, *, unit, dq_dtype, dk_dtype, name):
    H, S = lse.shape[0], lse.shape[1]
    delta = _fa_delta(do, o)
    dq, dd_q = _fa_bwd_dq(q, k, v, do, lse, delta, dec_row, unit=unit, out_dtype=dq_dtype, name=name + "_dq")
    dk, dv, dd_k = _fa_bwd_dkv(q, k, v, do, lse[:, :, 0].reshape(H, 1, S), delta[:, :, 0].reshape(H, 1, S), dec_rep,
                               unit=unit, dk_dtype=dk_dtype, name=name + "_dkv")
    return dq, dk, dv, (None if dd_k is None else dd_q[:, :, 0] + dd_k[:, :, 0])


def _merge_fwd(ya, yb, yc, gate_logit, gate_b):
    S, D = ya.shape
    tm = min(256, S)

    def body(a_ref, b_ref, c_ref, gl_ref, gb_ref, o_ref):
        g = jax.nn.sigmoid(gl_ref[...] + gb_ref[...])
        o_ref[...] = (g[:, 0:D] * a_ref[...] + g[:, D:2 * D] * b_ref[...] + g[:, 2 * D:3 * D] * c_ref[...]).astype(BF16)

    row = pl.BlockSpec((tm, D), lambda i: (i, 0))
    return pl.pallas_call(
        body, name="merge_fwd", grid=(S // tm,),
        in_specs=[row, row, row, pl.BlockSpec((tm, 3 * D), lambda i: (i, 0)), pl.BlockSpec((1, 3 * D), lambda i: (0, 0))],
        out_specs=row, out_shape=jax.ShapeDtypeStruct((S, D), BF16), compiler_params=_cparams(("parallel",)),
    )(ya, yb, yc, gate_logit, gate_b.reshape(1, 3 * D))


def _merge_bwd(dm, ya, yb, yc, gate_logit, gate_b):
    S, D = ya.shape
    tm = min(256, S)

    def body(dm_ref, a_ref, b_ref, c_ref, gl_ref, gb_ref, da_ref, db_ref, dc_ref, dgl_ref, dgb_ref):
        g = jax.nn.sigmoid(gl_ref[...] + gb_ref[...])
        dmv = dm_ref[...]
        parts = []
        for n, (y_ref, dy_ref) in enumerate(((a_ref, da_ref), (b_ref, db_ref), (c_ref, dc_ref))):
            gn = g[:, n * D:(n + 1) * D]
            dy_ref[...] = (dmv * gn).astype(BF16)
            parts.append(dmv * y_ref[...] * gn * (1.0 - gn))
        dgl = jnp.concatenate(parts, axis=1)
        dgl_ref[...] = dgl.astype(BF16)

        @pl.when(pl.program_id(0) == 0)
        def _():
            dgb_ref[...] = jnp.zeros_like(dgb_ref)

        dgb_ref[...] += jnp.sum(dgl, axis=0, keepdims=True)

    row = pl.BlockSpec((tm, D), lambda i: (i, 0))
    wide = pl.BlockSpec((tm, 3 * D), lambda i: (i, 0))
    vec = pl.BlockSpec((1, 3 * D), lambda i: (0, 0))
    act = jax.ShapeDtypeStruct((S, D), BF16)
    da, db, dc, dgl, dgb = pl.pallas_call(
        body, name="merge_bwd", grid=(S // tm,), in_specs=[row, row, row, row, wide, vec],
        out_specs=(row, row, row, wide, vec),
        out_shape=(act, act, act, jax.ShapeDtypeStruct((S, 3 * D), BF16), jax.ShapeDtypeStruct((1, 3 * D), F32)),
        compiler_params=_cparams(("arbitrary",)),
    )(dm, ya, yb, yc, gate_logit, gate_b.reshape(1, 3 * D))
    return da, db, dc, dgl, dgb.reshape(3 * D)


def _swiglu_fwd(hf):
    S, W2 = hf.shape
    F = W2 // 2
    tm = min(128, S)

    def body(h_ref, o_ref):
        gt, up = h_ref[:, 0:F], h_ref[:, F:W2]
        o_ref[...] = (gt * jax.nn.sigmoid(gt) * up).astype(BF16)

    return pl.pallas_call(
        body, name="swiglu_fwd", grid=(S // tm,), in_specs=[pl.BlockSpec((tm, W2), lambda i: (i, 0))],
        out_specs=pl.BlockSpec((tm, F), lambda i: (i, 0)), out_shape=jax.ShapeDtypeStruct((S, F), BF16),
        compiler_params=_cparams(("parallel",)),
    )(hf)


def _swiglu_bwd(dact, hf):
    S, W2 = hf.shape
    F = W2 // 2
    tm = min(128, S)

    def body(d_ref, h_ref, o_ref):
        gt, up = h_ref[:, 0:F], h_ref[:, F:W2]
        sg = jax.nn.sigmoid(gt)
        dv = d_ref[...]
        o_ref[:, 0:F] = (dv * up * sg * (1.0 + gt * (1.0 - sg))).astype(BF16)
        o_ref[:, F:W2] = (dv * gt * sg).astype(BF16)

    return pl.pallas_call(
        body, name="swiglu_bwd", grid=(S // tm,),
        in_specs=[pl.BlockSpec((tm, F), lambda i: (i, 0)), pl.BlockSpec((tm, W2), lambda i: (i, 0))],
        out_specs=pl.BlockSpec((tm, W2), lambda i: (i, 0)), out_shape=jax.ShapeDtypeStruct((S, W2), BF16),
        compiler_params=_cparams(("parallel",)),
    )(dact, hf)


def _ple_fwd(x, pre, e):
    S, D = x.shape
    tm = _rows(S)

    def body(x_ref, p_ref, e_ref, o_ref):
        o_ref[...] = x_ref[...] + jax.nn.sigmoid(p_ref[...]) * e_ref[...]

    row = pl.BlockSpec((tm, D), lambda i: (i, 0))
    return pl.pallas_call(body, name="ple_fwd", grid=(S // tm,), in_specs=[row, row, row], out_specs=row,
                          out_shape=jax.ShapeDtypeStruct((S, D), F32), compiler_params=_cparams(("parallel",)))(x, pre, e)


def _ple_bwd(dx, pre, e):
    S, D = dx.shape
    tm = _rows(S)

    def body(dx_ref, p_ref, e_ref, dp_ref, de_ref):
        pg = jax.nn.sigmoid(p_ref[...])
        dxv = dx_ref[...]
        dp_ref[...] = (dxv * e_ref[...] * pg * (1.0 - pg)).astype(BF16)
        de_ref[...] = (dxv * pg).astype(BF16)

    row = pl.BlockSpec((tm, D), lambda i: (i, 0))
    act = jax.ShapeDtypeStruct((S, D), BF16)
    return pl.pallas_call(body, name="ple_bwd", grid=(S // tm,), in_specs=[row, row, row], out_specs=(row, row),
                          out_shape=(act, act), compiler_params=_cparams(("parallel",)))(dx, pre, e)


def _pad_heads(w, real):
    K = w.shape[0]
    w = w.reshape(K, HEADS, real)
    return jnp.pad(w, ((0, 0), (0, 0), (0, HEAD_PAD - real))).reshape(K, HEADS * HEAD_PAD)


def _unpad_heads(w, real):
    K = w.shape[0]
    return w.reshape(K, HEADS, HEAD_PAD)[:, :, :real].reshape(K, HEADS * real)


def _pad_head_rows(w, real):
    N = w.shape[1]
    w = w.reshape(HEADS, real, N)
    return jnp.pad(w, ((0, 0), (0, HEAD_PAD - real), (0, 0))).reshape(HEADS * HEAD_PAD, N)


def _unpad_head_rows(w, real):
    N = w.shape[1]
    return w.reshape(HEADS, HEAD_PAD, N)[:, :real].reshape(HEADS * real, N)


def _block_diag(w):
    w = w.reshape(4, 2, 64, 64)
    z = jnp.zeros((4, 64, 64), w.dtype)
    top = jnp.concatenate([w[:, 0], z], axis=2)
    bot = jnp.concatenate([z, w[:, 1]], axis=2)
    return jnp.concatenate([top, bot], axis=1)


def _block_diag_t(w):
    return jnp.stack([w[:, :64, :64], w[:, 64:, 64:]], axis=1).reshape(8, 64, 64)


_IN_SPLITS = (512, 512, 384, 288, 512, 512, 512, 8, 3072)
_IN_OFF = np.concatenate([[0], np.cumsum(_IN_SPLITS)])
_KR_OFF = 64
_SEG_NAMES = ("u", "ug", "cq", "ckv", "kr", "fq", "fk", "fv", "fl", "gate")


def _in_segments(w_in):
    c = lambda n: w_in[:, int(_IN_OFF[n]):int(_IN_OFF[n + 1])]
    kv = c(3)
    kr = jnp.pad(kv[:, MLA_KV_LORA:], ((0, 0), (_KR_OFF, LANES - _KR_OFF - MLA_ROPE)))
    fl = jnp.pad(c(7), ((0, 0), (0, LANES - HEADS)))
    fq = _pad_heads(c(4), FOX_HEAD_DIM) * jnp.asarray(FOX_SCALE, w_in.dtype)
    return [c(0), c(1), c(2), kv[:, :MLA_KV_LORA], kr, fq, _pad_heads(c(5), FOX_HEAD_DIM), _pad_heads(c(6), FOX_HEAD_DIM), fl, c(8)]


def _in_unsegment(dw_p, widths):
    offs = np.concatenate([[0], np.cumsum(widths)])
    seg = [dw_p[:, int(offs[n]):int(offs[n + 1])] for n in range(len(widths))]
    u, ug, cq, ckv, kr, fq, fk, fv, fl, gate = seg
    return jnp.concatenate([
        u, ug, cq, ckv, kr[:, _KR_OFF:_KR_OFF + MLA_ROPE], _unpad_heads(fq, FOX_HEAD_DIM) * FOX_SCALE,
        _unpad_heads(fk, FOX_HEAD_DIM), _unpad_heads(fv, FOX_HEAD_DIM), fl[:, :HEADS], gate], axis=1)


def _split_wuq(wuq):
    return _pad_heads(wuq, MLA_NOPE + MLA_ROPE)


def _split_wukv(wukv):
    w = wukv.reshape(MLA_KV_LORA, HEADS, MLA_NOPE + MLA_V)
    pad = lambda t: jnp.pad(t, ((0, 0), (0, 0), (0, HEAD_PAD - t.shape[2]))).reshape(MLA_KV_LORA, HEADS * HEAD_PAD)
    return pad(w[:, :, :MLA_NOPE]), pad(w[:, :, MLA_NOPE:])


def _merge_wukv(dk_p, dv_p):
    k = dk_p.reshape(MLA_KV_LORA, HEADS, HEAD_PAD)[:, :, :MLA_NOPE]
    v = dv_p.reshape(MLA_KV_LORA, HEADS, HEAD_PAD)[:, :, :MLA_V]
    return jnp.concatenate([k, v], axis=2).reshape(MLA_KV_LORA, HEADS * (MLA_NOPE + MLA_V))


def _heads_layout(d):
    S = d.shape[0]
    t = d[:, :HEADS].T
    return t.reshape(HEADS, 1, S), jnp.broadcast_to(t[:, :, None], (HEADS, S, LANES))


def _layer_fwd(x, p_i, w, tabs):
    c_q, c_k, s_lo, s_hi = tabs
    sv = {"x0": x}
    segs = _in_segments(w["w_in"])
    h = _rmsnorm_fwd(x, w["mix_norm"], "mix_norm_fwd")
    z = {}
    for nm, ws in zip(_SEG_NAMES, segs):
        z[nm] = _mm(h, ws, out_dtype=BF16 if nm in ("fq", "fk", "fv") else F32, bias=_ones_lane_bias() if nm == "fv" else None,
                    name="in_" + nm)
    sv.update(h=h, z=z)
    wa_bd, wx_bd = _block_diag(w["lru_wa"]).astype(BF16), _block_diag(w["lru_wx"]).astype(BF16)
    oa, xc, hs = _lru_fwd(z["u"], z["ug"], w["conv_w"], w["conv_b"], wa_bd, wx_bd, w["lru_ba"], w["lru_bx"], w["lru_lambda"])
    sv.update(oa=oa, xc=xc, hs=hs)
    qn = _rmsnorm_fwd(z["cq"], w["mla_q_norm"], "q_norm_fwd")
    kvn = _rmsnorm_fwd(z["ckv"], w["mla_kv_norm"], "kv_norm_fwd")
    wuq_p = _split_wuq(w["mla_wuq"])
    wk_p, wv_p = _split_wukv(w["mla_wukv"])
    qb = _rope_q(_mm(qn, wuq_p, name="mla_q"), c_q, s_lo, s_hi, transpose=False, out_dtype=BF16, name="rope_q")
    kb = _rope_k(_mm(kvn, wk_p, name="mla_k"), z["kr"], c_k, s_lo, s_hi)
    vb = _mm(kvn, wv_p, out_dtype=BF16, bias=_ones_lane_bias(), name="mla_v")
    ob, lse_b = _fa_fwd(qb, kb, vb, None, unit=64, name="mla_attn")
    sv.update(qn=qn, kvn=kvn, qb=qb, kb=kb, vb=vb, ob=ob, lse_b=lse_b)
    bf = jnp.pad(w["fox_bf"], (0, LANES - HEADS)).reshape(1, LANES)
    dec = _decay_fwd(z["fl"], bf)
    drow, drep = _heads_layout(dec)
    oc, lse_c = _fa_fwd(z["fq"], z["fk"], z["fv"], drow, unit=1, name="fox_attn")
    sv.update(drow=drow, drep=drep, oc=oc, lse_c=lse_c)
    ya = _mm(oa, w["w_br_a"], name="br_a")
    yb = _mm(ob, _pad_head_rows(w["w_br_b"], MLA_V), name="br_b")
    yc = _mm(oc, _pad_head_rows(w["w_br_c"], FOX_HEAD_DIM), name="br_c")
    merged = _merge_fwd(ya, yb, yc, z["gate"], w["gate_b"])
    x1 = _mm(merged, w["w_o"], res=x, name="w_o")
    sv.update(ya=ya, yb=yb, yc=yc, merged=merged, x1=x1)
    hn = _rmsnorm_fwd(x1, w["ffn_norm"], "ffn_norm_fwd")
    hf = _mm(hn, w["w_gate_up"], name="ffn_up")
    act = _swiglu_fwd(hf)
    x2 = _mm(act, w["w_down"], res=x1, name="ffn_down")
    sv.update(hn=hn, hf=hf, act=act, x2=x2)
    pn = _rmsnorm_fwd(x2, w["ple_norm"], "ple_norm_fwd")
    pre = _mm(pn, w["w_ple_gate"], name="ple_gate")
    e = _mm(p_i, w["w_ple"], name="ple_embed")
    x3 = _ple_fwd(x2, pre, e)
    sv.update(pn=pn, pre=pre, e=e, p_i=p_i)
    return x3, sv


def _layer_bwd(dx3, w, sv, tabs):
    c_q, c_k, s_lo, s_hi = tabs
    g = {}
    z = sv["z"]
    dpre, de = _ple_bwd(dx3, sv["pre"], sv["e"])
    g["w_ple"] = _mm(sv["p_i"], de, ta=True, name="d_w_ple")
    g["w_ple_gate"] = _mm(sv["pn"], dpre, ta=True, name="d_w_ple_gate")
    dpn = _mm(dpre, w["w_ple_gate"], tb=True, name="d_pn")
    dx2, g["ple_norm"] = _rmsnorm_bwd(sv["x2"], w["ple_norm"], dpn, add=dx3, name="ple_norm_bwd")
    dact = _mm(dx2, w["w_down"], tb=True, name="d_act")
    g["w_down"] = _mm(sv["act"], dx2, ta=True, name="d_w_down")
    dhf = _swiglu_bwd(dact, sv["hf"])
    g["w_gate_up"] = _mm(sv["hn"], dhf, ta=True, name="d_w_gate_up")
    dhn = _mm(dhf, w["w_gate_up"], tb=True, name="d_hn")
    dx1, g["ffn_norm"] = _rmsnorm_bwd(sv["x1"], w["ffn_norm"], dhn, add=dx2, name="ffn_norm_bwd")
    g["w_o"] = _mm(sv["merged"], dx1, ta=True, name="d_w_o")
    dm = _mm(dx1, w["w_o"], tb=True, name="d_merged")
    dya, dyb, dyc, dgate, g["gate_b"] = _merge_bwd(dm, sv["ya"], sv["yb"], sv["yc"], z["gate"], w["gate_b"])
    wbb_p, wbc_p = _pad_head_rows(w["w_br_b"], MLA_V), _pad_head_rows(w["w_br_c"], FOX_HEAD_DIM)
    g["w_br_a"] = _mm(sv["oa"], dya, ta=True, name="d_w_br_a")
    g["w_br_b"] = _unpad_head_rows(_mm(sv["ob"], dyb, ta=True, name="d_w_br_b"), MLA_V)
    g["w_br_c"] = _unpad_head_rows(_mm(sv["oc"], dyc, ta=True, name="d_w_br_c"), FOX_HEAD_DIM)
    doa = _mm(dya, w["w_br_a"], tb=True, name="d_oa")
    dob = _mm(dyb, wbb_p, tb=True, out_dtype=BF16, name="d_ob")
    doc = _mm(dyc, wbc_p, tb=True, out_dtype=BF16, name="d_oc")
    dfq, dfk, dfv, d_dec = _fa_bwd(z["fq"], z["fk"], z["fv"], sv["oc"], sv["lse_c"], doc, sv["drow"], sv["drep"],
                                   unit=1, dq_dtype=BF16, dk_dtype=BF16, name="fox_attn_bwd")
    d_dec = jnp.pad(d_dec.T, ((0, 0), (0, LANES - HEADS)))
    bf = jnp.pad(w["fox_bf"], (0, LANES - HEADS)).reshape(1, LANES)
    dfl, dbf = _decay_bwd(d_dec, z["fl"], bf)
    g["fox_bf"] = dbf[0, :HEADS]
    dqb, dkb, dvb, _ = _fa_bwd(sv["qb"], sv["kb"], sv["vb"], sv["ob"], sv["lse_b"], dob, None, None,
                               unit=64, dq_dtype=F32, dk_dtype=F32, name="mla_attn_bwd")
    wuq_p = _split_wuq(w["mla_wuq"])
    wk_p, wv_p = _split_wukv(w["mla_wukv"])
    dq_pre = _rope_q(dqb, c_q, s_lo, s_hi, transpose=True, out_dtype=BF16, name="rope_q_bwd")
    dkr = _rope_k_bwd(dkb, c_k, s_lo, s_hi)
    g["mla_wuq"] = _unpad_heads(_mm(sv["qn"], dq_pre, ta=True, name="d_wuq"), MLA_NOPE + MLA_ROPE)
    g["mla_wukv"] = _merge_wukv(_mm(sv["kvn"], dkb, ta=True, name="d_wuk"), _mm(sv["kvn"], dvb, ta=True, name="d_wuv"))
    dqn = _mm(dq_pre, wuq_p, tb=True, name="d_qn")
    dkvn = _mm(dvb, wv_p, tb=True, res=_mm(dkb, wk_p, tb=True, name="d_kvn_k"), name="d_kvn")
    dcq, g["mla_q_norm"] = _rmsnorm_bwd(z["cq"], w["mla_q_norm"], dqn, out_dtype=BF16, name="q_norm_bwd")
    dckv, g["mla_kv_norm"] = _rmsnorm_bwd(z["ckv"], w["mla_kv_norm"], dkvn, out_dtype=BF16, name="kv_norm_bwd")
    wa_bd, wx_bd = _block_diag(w["lru_wa"]).astype(BF16), _block_diag(w["lru_wx"]).astype(BF16)
    du, dug, dcw, dcb, dba, dbx, dlam, dwa, dwx = _lru_bwd(
        doa, z["u"], z["ug"], sv["xc"], sv["hs"], w["conv_w"], wa_bd, wx_bd, w["lru_ba"], w["lru_bx"], w["lru_lambda"])
    g["conv_w"], g["conv_b"], g["lru_ba"], g["lru_bx"] = dcw, dcb[0], dba[0], dbx[0]
    g["lru_lambda"] = dlam[0] * LRU_C * jax.nn.sigmoid(-w["lru_lambda"])
    g["lru_wa"], g["lru_wx"] = _block_diag_t(dwa), _block_diag_t(dwx)
    dsegs = [du, dug, dcq, dckv, dkr, dfq, dfk, dfv, dfl, dgate]
    dz = jnp.concatenate(dsegs, axis=1)
    w_in_p = jnp.concatenate(_in_segments(w["w_in"]), axis=1)
    g["w_in"] = _in_unsegment(_mm(sv["h"], dz, ta=True, name="d_w_in"), [d.shape[1] for d in dsegs])
    dh = _mm(dz, w_in_p, tb=True, name="d_h")
    dx0, g["mix_norm"] = _rmsnorm_bwd(sv["x0"], w["mix_norm"], dh, add=dx1, name="mix_norm_bwd")
    return dx0, g


_LAYER_WEIGHTS = ("mix_norm", "w_in", "gate_b", "conv_w", "conv_b", "lru_wa", "lru_ba", "lru_wx", "lru_bx", "lru_lambda",
                  "mla_q_norm", "mla_wuq", "mla_kv_norm", "mla_wukv", "fox_bf", "w_br_a", "w_br_b", "w_br_c", "w_o",
                  "ffn_norm", "w_gate_up", "w_down", "ple_norm", "w_ple_gate", "w_ple")
_BIG = ("w_in", "mla_wuq", "mla_wukv", "w_br_a", "w_br_b", "w_br_c", "w_o", "w_gate_up", "w_down", "w_ple_gate", "w_ple")
_ROW_SHARDED = ("w_o", "w_down", "w_ple_gate")
_SMALL = ("mix_norm", "gate_b", "conv_b", "lru_wa", "lru_ba", "lru_wx", "lru_bx", "lru_lambda", "mla_q_norm", "mla_kv_norm",
          "fox_bf", "ffn_norm", "ple_norm")


def _local_step(x, p, layers, final_norm, target):
    tabs = _rope_tables(x.shape[0])
    saved = []
    for i in range(DEPTH):
        x, sv = _layer_fwd(x, p[i], layers[i], tabs)
        saved.append(sv)
    loss, dx, d_final = _loss_head(x, final_norm, target)
    grads = [None] * DEPTH
    for i in reversed(range(DEPTH)):
        dx, grads[i] = _layer_bwd(dx, layers[i], saved[i], tabs)
    return loss, dx, grads, d_final


def _hbm():
    return pl.BlockSpec(memory_space=pltpu.HBM)


def _peers(x, y):
    return [(1 - x, y), (x, 1 - y), (1 - x, 1 - y)]


def _gather_chips(shard, name):
    R, W = shard.shape

    def body(src_ref, out_ref, send_sems, recv_sems, local_sem):
        x, y, c = lax.axis_index("x"), lax.axis_index("y"), lax.axis_index("c")
        me = 2 * x + y
        mine = pltpu.make_async_copy(src_ref, out_ref.at[me], local_sem)
        mine.start()

        def copy(j, slot, to):
            return pltpu.make_async_remote_copy(src_ref=src_ref, dst_ref=out_ref.at[slot], send_sem=send_sems.at[j],
                                                recv_sem=recv_sems.at[j], device_id=(to[0], to[1], c), device_id_type=MESH)

        sends = [copy(j, me, peer) for j, peer in enumerate(_peers(x, y))]
        for cp in sends:
            cp.start()
        for j, peer in enumerate(_peers(x, y)):
            copy(j, 2 * peer[0] + peer[1], peer).wait_recv()
        for cp in sends:
            cp.wait_send()
        mine.wait()

    return pl.pallas_call(
        body, name=name, in_specs=[_hbm()], out_specs=_hbm(), out_shape=jax.ShapeDtypeStruct((4, R, W), shard.dtype),
        scratch_shapes=[pltpu.SemaphoreType.DMA((3,)), pltpu.SemaphoreType.DMA((3,)), pltpu.SemaphoreType.DMA],
    )(shard)


def _pair_swap_halves(g4):
    n, R, W = g4.shape
    Rh = R // 2

    def body(src_ref, out_ref, send_sem, recv_sem):
        x, y, c = lax.axis_index("x"), lax.axis_index("y"), lax.axis_index("c")
        cp = pltpu.make_async_remote_copy(src_ref=src_ref.at[:, pl.ds((1 - c) * Rh, Rh), :], dst_ref=out_ref, send_sem=send_sem,
                                          recv_sem=recv_sem, device_id=(x, y, 1 - c), device_id_type=MESH)
        cp.start()
        cp.wait()

    return pl.pallas_call(
        body, name="grad_pair_swap", in_specs=[_hbm()], out_specs=_hbm(), out_shape=jax.ShapeDtypeStruct((n, Rh, W), g4.dtype),
        scratch_shapes=[pltpu.SemaphoreType.DMA, pltpu.SemaphoreType.DMA],
    )(g4)


def _pair_add(g4, sib, c_arr):
    n, R, W = g4.shape
    Rh = R // 2
    tr = _tile_rows(Rh)
    nb = Rh // tr

    def body(c_ref, a_ref, b_ref, o_ref):
        o_ref[...] = (a_ref[...].astype(F32) + b_ref[...].astype(F32)).astype(o_ref.dtype)

    return pl.pallas_call(
        body, name="grad_pair_add",
        grid_spec=pltpu.PrefetchScalarGridSpec(
            num_scalar_prefetch=1, grid=(n, nb),
            in_specs=[pl.BlockSpec((None, tr, W), lambda s, i, c: (s, c[0] * nb + i, 0)), pl.BlockSpec((None, tr, W), lambda s, i, c: (s, i, 0))],
            out_specs=pl.BlockSpec((None, tr, W), lambda s, i, c: (s, i, 0))),
        out_shape=jax.ShapeDtypeStruct((n, Rh, W), g4.dtype), compiler_params=_cparams(("parallel", "parallel")),
    )(c_arr, g4, sib)


def _tile_rows(n):
    for t in (512, 480, 400, 320, 256, 240, 160, 128, 80, 64, 40, 32, 16, 8):
        if n % t == 0:
            return t
    return n


def _chips_exchange(part):
    n, Rh, W = part.shape

    def body(src_ref, out_ref, send_sems, recv_sems):
        x, y, c = lax.axis_index("x"), lax.axis_index("y"), lax.axis_index("c")

        def copy(j, to):
            return pltpu.make_async_remote_copy(src_ref=src_ref.at[2 * to[0] + to[1]], dst_ref=out_ref.at[j], send_sem=send_sems.at[j],
                                                recv_sem=recv_sems.at[j], device_id=(to[0], to[1], c), device_id_type=MESH)

        cps = [copy(j, peer) for j, peer in enumerate(_peers(x, y))]
        for cp in cps:
            cp.start()
        for cp in cps:
            cp.wait()

    return pl.pallas_call(
        body, name="grad_chips_exchange", in_specs=[_hbm()], out_specs=_hbm(), out_shape=jax.ShapeDtypeStruct((3, Rh, W), part.dtype),
        scratch_shapes=[pltpu.SemaphoreType.DMA((3,)), pltpu.SemaphoreType.DMA((3,))],
    )(part)


def _chips_add(part, got, k_arr):
    n, Rh, W = part.shape
    tr = _tile_rows(Rh)

    def body(k_ref, a_ref, b_ref, o_ref):
        o_ref[...] = ((a_ref[...].astype(F32) + b_ref[0].astype(F32)) + b_ref[1].astype(F32)) + b_ref[2].astype(F32)

    return pl.pallas_call(
        body, name="grad_chips_add",
        grid_spec=pltpu.PrefetchScalarGridSpec(
            num_scalar_prefetch=1, grid=(Rh // tr,),
            in_specs=[pl.BlockSpec((None, tr, W), lambda i, k: (k[0], i, 0)), pl.BlockSpec((3, tr, W), lambda i, k: (0, i, 0))],
            out_specs=pl.BlockSpec((tr, W), lambda i, k: (i, 0))),
        out_shape=jax.ShapeDtypeStruct((Rh, W), F32), compiler_params=_cparams(("parallel",)),
    )(k_arr, part, got)


def _pair_gather(half):
    Rh, W = half.shape

    def body(src_ref, out_ref, send_sem, recv_sem, local_sem):
        x, y, c = lax.axis_index("x"), lax.axis_index("y"), lax.axis_index("c")
        mine = pltpu.make_async_copy(src_ref, out_ref.at[c], local_sem)
        mine.start()
        pltpu.make_async_remote_copy(src_ref=src_ref, dst_ref=out_ref.at[c], send_sem=send_sem, recv_sem=recv_sem,
                                     device_id=(x, y, 1 - c), device_id_type=MESH).start()
        pltpu.make_async_remote_copy(src_ref=src_ref, dst_ref=out_ref.at[1 - c], send_sem=send_sem, recv_sem=recv_sem,
                                     device_id=(x, y, 1 - c), device_id_type=MESH).wait()
        mine.wait()

    return pl.pallas_call(
        body, name="grad_pair_gather", in_specs=[_hbm()], out_specs=_hbm(), out_shape=jax.ShapeDtypeStruct((2, Rh, W), half.dtype),
        scratch_shapes=[pltpu.SemaphoreType.DMA, pltpu.SemaphoreType.DMA, pltpu.SemaphoreType.DMA],
    )(half)


def _gather_all(buf):
    R, W = buf.shape

    def body(src_ref, out_ref, send_sems, recv_sems, local_sem):
        x, y, c = lax.axis_index("x"), lax.axis_index("y"), lax.axis_index("c")
        me = 4 * x + 2 * y + c
        mine = pltpu.make_async_copy(src_ref, out_ref.at[me], local_sem)
        mine.start()
        rel = [((x + (r >> 2 & 1)) % 2, (y + (r >> 1 & 1)) % 2, (c + (r & 1)) % 2) for r in range(1, 8)]

        def copy(j, slot, to):
            return pltpu.make_async_remote_copy(src_ref=src_ref, dst_ref=out_ref.at[slot], send_sem=send_sems.at[j],
                                                recv_sem=recv_sems.at[j], device_id=to, device_id_type=MESH)

        sends = [copy(j, me, to) for j, to in enumerate(rel)]
        for cp in sends:
            cp.start()
        for j, to in enumerate(rel):
            copy(j, 4 * to[0] + 2 * to[1] + to[2], to).wait_recv()
        for cp in sends:
            cp.wait_send()
        mine.wait()

    return pl.pallas_call(
        body, name="small_gather", in_specs=[_hbm()], out_specs=_hbm(), out_shape=jax.ShapeDtypeStruct((8, R, W), buf.dtype),
        scratch_shapes=[pltpu.SemaphoreType.DMA((7,)), pltpu.SemaphoreType.DMA((7,)), pltpu.SemaphoreType.DMA],
    )(buf)


def _sum_slots(stack):
    n, R, W = stack.shape
    tr = _tile_rows(R)

    def body(s_ref, o_ref):
        tot = s_ref[0]
        for j in range(1, n):
            tot = tot + s_ref[j]
        o_ref[...] = tot

    return pl.pallas_call(
        body, name="small_sum", grid=(R // tr,), in_specs=[pl.BlockSpec((n, tr, W), lambda i: (0, i, 0))],
        out_specs=pl.BlockSpec((tr, W), lambda i: (i, 0)), out_shape=jax.ShapeDtypeStruct((R, W), F32),
        compiler_params=_cparams(("parallel",)),
    )(stack)


def _adamw(wp, gp, mp, vp, name):
    R, W = wp.shape
    tr = R
    for t in (1024, 512, 256, 128, 64, 32, 16, 8):
        if R % t == 0 and t * W <= 512 * 1024:
            tr = t
            break
    c1 = 1.0 - ADAM_B1 ** ADAM_STEP
    c2 = 1.0 - ADAM_B2 ** ADAM_STEP

    def body(w_ref, g_ref, m_ref, v_ref, d_ref, mo_ref, vo_ref):
        gv = g_ref[...]
        m = ADAM_B1 * m_ref[...] + (1.0 - ADAM_B1) * gv
        v = ADAM_B2 * v_ref[...] + (1.0 - ADAM_B2) * (gv * gv)
        m_hat = m / c1
        v_hat = v / c2
        d_ref[...] = -ADAM_LR * (m_hat / (jnp.sqrt(v_hat) + ADAM_EPS) + ADAM_WD * w_ref[...])
        mo_ref[...] = m
        vo_ref[...] = v

    blk = pl.BlockSpec((tr, W), lambda i: (i, 0))
    shp = jax.ShapeDtypeStruct((R, W), F32)
    return pl.pallas_call(body, name=name, grid=(R // tr,), in_specs=[blk] * 4, out_specs=(blk,) * 3, out_shape=(shp,) * 3,
                          compiler_params=_cparams(("parallel",)))(wp, gp, mp, vp)


def _pack(arrs, rows):
    flat = jnp.concatenate([a.reshape(-1) for a in arrs])
    return jnp.pad(flat, (0, rows * PACK_W - flat.shape[0])).reshape(rows, PACK_W)


def _unpack(buf, shapes):
    flat = buf.reshape(-1)
    out, off = [], 0
    for shp in shapes:
        n = int(np.prod(shp))
        out.append(flat[off:off + n].reshape(shp))
        off += n
    return out


def _rows_for(shapes, mult):
    n = sum(int(np.prod(s)) for s in shapes)
    rows = -(-n // PACK_W)
    return -(-rows // mult) * mult


def _shard_major(g, name):
    L, K, N = g.shape
    if name in _ROW_SHARDED:
        t = g.reshape(L, 4, K // 4, N).transpose(1, 0, 2, 3)
    else:
        t = g.reshape(L, K, 4, N // 4).transpose(2, 0, 1, 3)
    return t.reshape(4, -1, PACK_W)


def _join_shards(blocks, name):
    return jnp.concatenate(blocks, axis=1 if name in _ROW_SHARDED else 2)


def kernel(x, p, mix_norm, w_in, gate_b, conv_w, conv_b, lru_wa, lru_ba, lru_wx, lru_bx, lru_lambda, mla_q_norm, mla_wuq, mla_kv_norm, mla_wukv, fox_bf, w_br_a, w_br_b, w_br_c, w_o, ffn_norm, w_gate_up, w_down, ple_norm, w_ple_gate, w_ple, final_norm, loss_target, m_mix_norm, m_w_in, m_gate_b, m_conv_w, m_conv_b, m_lru_wa, m_lru_ba, m_lru_wx, m_lru_bx, m_lru_lambda, m_mla_q_norm, m_mla_wuq, m_mla_kv_norm, m_mla_wukv, m_fox_bf, m_w_br_a, m_w_br_b, m_w_br_c, m_w_o, m_ffn_norm, m_w_gate_up, m_w_down, m_ple_norm, m_w_ple_gate, m_w_ple, m_final_norm, v_mix_norm, v_w_in, v_gate_b, v_conv_w, v_conv_b, v_lru_wa, v_lru_ba, v_lru_wx, v_lru_bx, v_lru_lambda, v_mla_q_norm, v_mla_wuq, v_mla_kv_norm, v_mla_wukv, v_fox_bf, v_w_br_a, v_w_br_b, v_w_br_c, v_w_o, v_ffn_norm, v_w_gate_up, v_w_down, v_ple_norm, v_w_ple_gate, v_w_ple, v_final_norm):
    a = dict(locals())
    names = list(_LAYER_WEIGHTS) + ["final_norm"]
    W = {n: a[n] for n in names}
    M = {n: a["m_" + n] for n in names}
    V = {n: a["v_" + n] for n in names}
    ix, iy, ic = lax.axis_index("x"), lax.axis_index("y"), lax.axis_index("c")

    sharded = list(_BIG) + ["conv_w"]
    shard_shapes = [W[n].shape for n in sharded]
    R = _rows_for(shard_shapes, 64)
    gathered = _gather_chips(_pack([W[n].astype(BF16) for n in sharded], R), "weight_gather")
    per_chip = [_unpack(gathered[k], shard_shapes) for k in range(4)]
    full = {n: _join_shards([per_chip[k][j] for k in range(4)], n) for j, n in enumerate(_BIG)}
    conv_blocks = _gather_chips(conv_w.reshape(DEPTH * CONV_WIDTH, LANES), "conv_w_gather")
    conv_w_full = jnp.concatenate([conv_blocks[k].reshape(DEPTH, CONV_WIDTH, LANES) for k in range(4)], axis=-1)
    layers = []
    for i in range(DEPTH):
        lw = {n: W[n][i] for n in _SMALL}
        for n in _BIG:
            lw[n] = full[n][i]
        lw["conv_w"] = conv_w_full[i]
        layers.append(lw)

    loss_sum, dx, grads, d_final = _local_step(x[0], p[:, 0], layers, final_norm, loss_target[0])
    loss = lax.psum(loss_sum, ("x", "y", "c"))

    parts = [_shard_major(jnp.stack([grads[i][n] for i in range(DEPTH)]), n).astype(BF16) for n in sharded]
    used = sum(t.shape[1] for t in parts)
    g4 = jnp.concatenate(parts + [jnp.zeros((4, R - used, PACK_W), BF16)], axis=1)
    c_arr = jnp.reshape(ic, (1,)).astype(jnp.int32)
    k_arr = jnp.reshape(2 * ix + iy, (1,)).astype(jnp.int32)
    pair = _pair_add(g4, _pair_swap_halves(g4), c_arr)
    mine = _chips_add(pair, _chips_exchange(pair), k_arr)
    g_pack = _pair_gather(mine).reshape(R, PACK_W)
    big_out = {}
    for n, gsh in zip(sharded, _unpack(g_pack, shard_shapes)):
        view = lambda t: t.reshape(-1, t.shape[-1])
        d, nm, nv = _adamw(view(W[n]), view(gsh), view(M[n]), view(V[n]), "adamw_" + n)
        for key, arr in (("g", gsh), ("d", d), ("m", nm), ("v", nv)):
            big_out[(key, n)] = arr.reshape(W[n].shape)

    pick = lambda src, n, i: src[n] if i is None else src[n][i]
    small = [(n, i) for i in range(DEPTH) for n in _SMALL] + [("final_norm", None)]
    small_shapes = [pick(W, n, i).shape for n, i in small]
    Rs = _rows_for(small_shapes, 8)
    sg = _pack([d_final if i is None else grads[i][n] for n, i in small], Rs)
    sg = _sum_slots(_gather_all(sg))
    sw = _pack([pick(W, n, i) for n, i in small], Rs)
    sm = _pack([pick(M, n, i) for n, i in small], Rs)
    sv_ = _pack([pick(V, n, i) for n, i in small], Rs)
    sd, snm, snv = _adamw(sw, sg, sm, sv_, "adamw_replicated")
    small_out = {}
    for key, buf in (("g", sg), ("d", sd), ("m", snm), ("v", snv)):
        for (n, i), arr in zip(small, _unpack(buf, small_shapes)):
            small_out[(key, n, i)] = arr

    def assemble(key, n):
        if n == "final_norm":
            return small_out[(key, n, None)]
        if n in sharded:
            return big_out[(key, n)]
        return jnp.stack([small_out[(key, n, i)] for i in range(DEPTH)])

    outs = [loss, dx[None]]
    for key in ("g", "d", "m", "v"):
        outs += [assemble(key, n) for n in names]
    return tuple(outs)
```

```python
import functools
import math

import numpy as np
import jax
import jax.numpy as jnp
from jax import lax
from jax.experimental import pallas as pl
from jax.experimental.pallas import tpu as pltpu

F32, BF16 = jnp.float32, jnp.bfloat16
MESH = pl.DeviceIdType.MESH

D_MODEL = 1024
DEPTH = 2
EPS = 1e-6
NEG_INF = -1e30
LRU_WIDTH = 512
LRU_HEADS = 8
LRU_C = 8.0
CONV_WIDTH = 4
HEADS = 8
MLA_Q_LORA = 384
MLA_KV_LORA = 256
MLA_NOPE = 64
MLA_ROPE = 32
MLA_V = 64
ROPE_BASE = 10000.0
FOX_HEAD_DIM = 64
D_FF = 2816
PLE_DIM = 256
HEAD_PAD = 128
MLA_SCALE = (MLA_NOPE + MLA_ROPE) ** -0.5
FOX_SCALE = FOX_HEAD_DIM ** -0.5

ADAM_LR, ADAM_B1, ADAM_B2, ADAM_EPS, ADAM_WD, ADAM_STEP = 0.001, 0.9, 0.999, 1e-08, 0.01, 10

VMEM_LIMIT_BYTES = 48 * 1024 * 1024
LANES = 128
PACK_W = 1024

ROW_TILE = 512
ATTN_TILE = 1024
LRU_CHUNK = 512


def _cparams(dims):
    return pltpu.CompilerParams(dimension_semantics=dims, vmem_limit_bytes=VMEM_LIMIT_BYTES)


def _tile(n, cap):
    if n <= cap:
        return n
    t = (cap // LANES) * LANES
    while t >= LANES:
        if n % t == 0:
            return t
        t -= LANES
    raise ValueError(f"no tile for {n} under {cap}")


def _rows(n):
    return min(ROW_TILE, n)


def _mm(a, b, *, ta=False, tb=False, out_dtype=F32, res=None, bias=None, name):
    K, M = a.shape if ta else a.shape[::-1]
    N, K2 = b.shape if tb else b.shape[::-1]
    assert K == K2, (name, a.shape, b.shape)
    assert res is None or bias is None
    tm, tn, tk = _tile(M, 512), _tile(N, 1792), _tile(K, 1408)
    nk = K // tk
    a_spec = pl.BlockSpec((tk, tm), lambda i, j, k: (k, i)) if ta else pl.BlockSpec((tm, tk), lambda i, j, k: (i, k))
    b_spec = pl.BlockSpec((tn, tk), lambda i, j, k: (j, k)) if tb else pl.BlockSpec((tk, tn), lambda i, j, k: (k, j))
    o_spec = pl.BlockSpec((tm, tn), lambda i, j, k: (i, j))
    dn = (((0,) if ta else (1,), (1,) if tb else (0,)), ((), ()))
    if bias is not None:
        res, r_spec = bias, pl.BlockSpec((1, tn), lambda i, j, k: (0, j))
    else:
        r_spec = o_spec
    has_res = res is not None

    def body(*refs):
        a_ref, b_ref = refs[0], refs[1]
        r_ref = refs[2] if has_res else None
        o_ref = refs[3] if has_res else refs[2]
        av, bv = a_ref[...], b_ref[...]
        if av.dtype != BF16:
            av = av.astype(BF16)
        if bv.dtype != BF16:
            bv = bv.astype(BF16)
        part = lax.dot_general(av, bv, dn, preferred_element_type=F32)

        def finish(total):
            if has_res:
                total = total + r_ref[...]
            o_ref[...] = total.astype(out_dtype)

        if nk == 1:
            finish(part)
        else:
            acc = refs[-1]
            k = pl.program_id(2)

            @pl.when(k == 0)
            def _():
                acc[...] = part

            @pl.when(k > 0)
            def _():
                acc[...] += part

            @pl.when(k == nk - 1)
            def _():
                finish(acc[...])

    in_specs = [a_spec, b_spec] + ([r_spec] if has_res else [])
    args = (a, b) + ((res,) if has_res else ())
    return pl.pallas_call(
        body, name=name, grid=(M // tm, N // tn, nk), in_specs=in_specs, out_specs=o_spec,
        out_shape=jax.ShapeDtypeStruct((M, N), out_dtype),
        scratch_shapes=[pltpu.VMEM((tm, tn), F32)] if nk > 1 else [],
        compiler_params=_cparams(("parallel", "parallel", "arbitrary")),
    )(*args)


def _rmsnorm_fwd(x, g, name):
    S, W = x.shape
    tm = _rows(S)

    def body(x_ref, g_ref, o_ref):
        xf = x_ref[...]
        rstd = lax.rsqrt(jnp.mean(xf * xf, axis=1, keepdims=True) + EPS)
        o_ref[...] = (xf * rstd * g_ref[...]).astype(BF16)

    return pl.pallas_call(
        body, name=name, grid=(S // tm,),
        in_specs=[pl.BlockSpec((tm, W), lambda i: (i, 0)), pl.BlockSpec((1, W), lambda i: (0, 0))],
        out_specs=pl.BlockSpec((tm, W), lambda i: (i, 0)),
        out_shape=jax.ShapeDtypeStruct((S, W), BF16), compiler_params=_cparams(("parallel",)),
    )(x, g.reshape(1, W))


def _rmsnorm_bwd(x, g, dy, *, add=None, out_dtype=F32, name):
    S, W = x.shape
    tm = _rows(S)
    has_add = add is not None

    def body(*refs):
        x_ref, g_ref, dy_ref = refs[:3]
        add_ref = refs[3] if has_add else None
        dx_ref, dg_ref = refs[-2], refs[-1]
        xf = x_ref[...]
        rstd = lax.rsqrt(jnp.mean(xf * xf, axis=1, keepdims=True) + EPS)
        xhat = xf * rstd
        dyv = dy_ref[...]
        dxh = dyv * g_ref[...]
        dx = rstd * (dxh - xhat * jnp.mean(dxh * xhat, axis=1, keepdims=True))
        if has_add:
            dx = dx + add_ref[...]
        dx_ref[...] = dx.astype(out_dtype)

        @pl.when(pl.program_id(0) == 0)
        def _():
            dg_ref[...] = jnp.zeros_like(dg_ref)

        dg_ref[...] += jnp.sum(dyv * xhat, axis=0, keepdims=True)

    row = pl.BlockSpec((tm, W), lambda i: (i, 0))
    vec = pl.BlockSpec((1, W), lambda i: (0, 0))
    dx, dg = pl.pallas_call(
        body, name=name, grid=(S // tm,),
        in_specs=[row, vec, row] + ([row] if has_add else []),
        out_specs=(row, vec),
        out_shape=(jax.ShapeDtypeStruct((S, W), out_dtype), jax.ShapeDtypeStruct((1, W), F32)),
        compiler_params=_cparams(("arbitrary",)),
    )(x, g.reshape(1, W), dy, *((add,) if has_add else ()))
    return dx, dg.reshape(W)


def _loss_head(x, g, target):
    S, W = x.shape
    tm = _rows(S)

    def body(x_ref, g_ref, t_ref, loss_ref, dx_ref, dg_ref):
        xf = x_ref[...]
        gv = g_ref[...]
        rstd = lax.rsqrt(jnp.mean(xf * xf, axis=1, keepdims=True) + EPS)
        xhat = xf * rstd
        err = xhat * gv - t_ref[...]
        part = 0.5 * jnp.sum(jnp.mean(err * err, axis=1, keepdims=True), axis=0, keepdims=True)
        dyv = err * (1.0 / W)
        dxh = dyv * gv
        dx_ref[...] = rstd * (dxh - xhat * jnp.mean(dxh * xhat, axis=1, keepdims=True))

        @pl.when(pl.program_id(0) == 0)
        def _():
            dg_ref[...] = jnp.zeros_like(dg_ref)
            loss_ref[...] = jnp.zeros_like(loss_ref)

        dg_ref[...] += jnp.sum(dyv * xhat, axis=0, keepdims=True)
        loss_ref[...] += part

    row = pl.BlockSpec((tm, W), lambda i: (i, 0))
    vec = pl.BlockSpec((1, W), lambda i: (0, 0))
    loss, dx, dg = pl.pallas_call(
        body, name="loss_head", grid=(S // tm,), in_specs=[row, vec, row],
        out_specs=(pl.BlockSpec((1, 1), lambda i: (0, 0)), row, vec),
        out_shape=(jax.ShapeDtypeStruct((1, 1), F32), jax.ShapeDtypeStruct((S, W), F32), jax.ShapeDtypeStruct((1, W), F32)),
        compiler_params=_cparams(("arbitrary",)),
    )(x, g.reshape(1, W), target)
    return loss[0, 0], dx, dg.reshape(W)


def _scan_fwd(a, b, row):
    T = a.shape[0]
    d = 1
    while d < T:
        keep = row >= d
        b = jnp.where(keep, a * pltpu.roll(b, d, axis=0) + b, b)
        a = jnp.where(keep, a * pltpu.roll(a, d, axis=0), a)
        d *= 2
    return a, b


def _scan_bwd(a, b, row):
    T = a.shape[0]
    d = 1
    while d < T:
        keep = row < T - d
        b = jnp.where(keep, a * pltpu.roll(b, T - d, axis=0) + b, b)
        a = jnp.where(keep, a * pltpu.roll(a, T - d, axis=0), a)
        d *= 2
    return a, b


def _expm1(x):
    small = x * (1.0 + x * (0.5 + x * (1.0 / 6 + x * (1.0 / 24 + x * (1.0 / 120 + x * (1.0 / 720 + x * (1.0 / 5040)))))))
    return jnp.where(jnp.abs(x) < 0.25, small, jnp.exp(x) - 1.0)


_GELU_C = math.sqrt(2.0 / math.pi)


def _gelu_and_grad(x):
    inner = _GELU_C * (x + 0.044715 * x * x * x)
    th = jnp.tanh(inner)
    val = 0.5 * x * (1.0 + th)
    grad = 0.5 * (1.0 + th) + 0.5 * x * (1.0 - th * th) * _GELU_C * (1.0 + 3 * 0.044715 * x * x)
    return val, grad


def _lru_gates(xc, wa, wx, ba, bx, lam):
    xcb = xc.astype(BF16)
    r = jax.nn.sigmoid(jnp.dot(xcb, wa, preferred_element_type=F32) + ba)
    ig = jax.nn.sigmoid(jnp.dot(xcb, wx, preferred_element_type=F32) + bx)
    sp = jax.nn.softplus(-lam)
    log_a = -LRU_C * r * sp
    a = jnp.exp(log_a)
    mult = jnp.sqrt(-_expm1(2.0 * log_a))
    return xcb, r, ig, sp, a, mult


def _lru_fwd(u, ug, conv_w, conv_b, wa_bd, wx_bd, ba, bx, lam):
    S, W = u.shape
    T = min(LRU_CHUNK, S)
    nl, nc = W // LANES, S // T

    def body(u_ref, ug_ref, cw_ref, cb_ref, wa_ref, wx_ref, ba_ref, bx_ref, lam_ref, ya_ref, xc_ref, h_ref, prev_u, h_carry):
        c = pl.program_id(1)

        @pl.when(c == 0)
        def _():
            prev_u[...] = jnp.zeros_like(prev_u)
            h_carry[...] = jnp.zeros_like(h_carry)

        uv = u_ref[...]
        row = lax.broadcasted_iota(jnp.int32, (T, LANES), 0)
        row8 = lax.broadcasted_iota(jnp.int32, (8, LANES), 0)
        cw = cw_ref[...]
        xc = cb_ref[...] + uv * cw[3:4, :]
        pv = prev_u[...]
        for k in range(1, CONV_WIDTH):
            us = pltpu.roll(uv, k, axis=0)
            top = jnp.where(row8 < k, pltpu.roll(pv, k, axis=0), us[0:8])
            us = jnp.concatenate([top, us[8:]], axis=0)
            xc = xc + us * cw[3 - k:4 - k, :]
        prev_u[...] = uv[T - 8:T]
        _, r, ig, sp, a, mult = _lru_gates(xc, wa_ref[...], wx_ref[...], ba_ref[...], bx_ref[...], lam_ref[...])
        bb = mult * (ig * xc)
        aa, hh = _scan_fwd(a, bb, row)
        h = hh + aa * h_carry[7:8, :]
        h_carry[...] = h[T - 8:T]
        gl, _ = _gelu_and_grad(ug_ref[...])
        ya_ref[...] = (h * gl).astype(BF16)
        xc_ref[...] = xc
        h_ref[...] = h

    seq = pl.BlockSpec((T, LANES), lambda l, c: (c, l))
    vec = pl.BlockSpec((1, LANES), lambda l, c: (0, l))
    mat = pl.BlockSpec((None, LANES, LANES), lambda l, c: (l, 0, 0))
    return pl.pallas_call(
        body, name="lru_fwd", grid=(nl, nc),
        in_specs=[seq, seq, pl.BlockSpec((CONV_WIDTH, LANES), lambda l, c: (0, l)), vec, mat, mat, vec, vec, vec],
        out_specs=(seq, seq, seq),
        out_shape=(jax.ShapeDtypeStruct((S, W), BF16), jax.ShapeDtypeStruct((S, W), F32), jax.ShapeDtypeStruct((S, W), F32)),
        scratch_shapes=[pltpu.VMEM((8, LANES), F32), pltpu.VMEM((8, LANES), F32)],
        compiler_params=_cparams(("parallel", "arbitrary")),
    )(u, ug, conv_w, conv_b.reshape(1, W), wa_bd, wx_bd, ba.reshape(1, W), bx.reshape(1, W), lam.reshape(1, W))


def _lru_bwd(dya, u, ug, xc, h, conv_w, wa_bd, wx_bd, ba, bx, lam):
    S, W = u.shape
    T = min(LRU_CHUNK, S)
    nl, nc = W // LANES, S // T
    tb8 = T // 8

    def body(dya_ref, u_ref, ug_ref, xc_ref, h_ref, hp_ref, cw_ref, wa_ref, wx_ref, ba_ref, bx_ref, lam_ref,
             du_ref, dug_ref, dcw_ref, dcb_ref, dba_ref, dbx_ref, dlam_ref, dwa_ref, dwx_ref,
             g_next, a_next, dxc_next):
        c = pl.program_id(1)

        @pl.when(c == 0)
        def _():
            g_next[...] = jnp.zeros_like(g_next)
            a_next[...] = jnp.zeros_like(a_next)
            dxc_next[...] = jnp.zeros_like(dxc_next)
            for ref in (dcw_ref, dcb_ref, dba_ref, dbx_ref, dlam_ref, dwa_ref, dwx_ref):
                ref[...] = jnp.zeros_like(ref)

        row = lax.broadcasted_iota(jnp.int32, (T, LANES), 0)
        row8 = lax.broadcasted_iota(jnp.int32, (8, LANES), 0)
        xcv = xc_ref[...]
        wa, wx = wa_ref[...], wx_ref[...]
        xcb, r, ig, sp, a, mult = _lru_gates(xcv, wa, wx, ba_ref[...], bx_ref[...], lam_ref[...])
        gl, dgl = _gelu_and_grad(ug_ref[...])
        dyav = dya_ref[...]
        hv = h_ref[...]
        dug_ref[...] = (dyav * hv * dgl).astype(BF16)
        dh = dyav * gl
        a_up = pltpu.roll(a, T - 1, axis=0)
        a_up = jnp.where(row == T - 1, a_next[0:1, :], a_up)
        prod, gg = _scan_bwd(a_up, dh, row)
        g = gg + prod * g_next[0:1, :]
        h_prev = pltpu.roll(hv, 1, axis=0)
        first_chunk = c == nc - 1
        h_before = jnp.where(first_chunk, 0.0, hp_ref[7:8, :])
        h_prev = jnp.where(row == 0, h_before, h_prev)
        da = g * h_prev
        d_mult = g * (ig * xcv)
        d_ig = g * mult * xcv
        dxc = g * mult * ig
        d_log_a = da * a - d_mult * (a * a) / mult
        d_r = d_log_a * (-LRU_C * sp)
        d_pa = d_r * r * (1.0 - r)
        d_px = d_ig * ig * (1.0 - ig)
        d_pab, d_pxb = d_pa.astype(BF16), d_px.astype(BF16)
        nt = (((1,), (1,)), ((), ()))
        tn = (((0,), (0,)), ((), ()))
        dxc = dxc + lax.dot_general(d_pab, wa, nt, preferred_element_type=F32) + lax.dot_general(d_pxb, wx, nt, preferred_element_type=F32)
        dwa_ref[...] += lax.dot_general(xcb, d_pab, tn, preferred_element_type=F32)
        dwx_ref[...] += lax.dot_general(xcb, d_pxb, tn, preferred_element_type=F32)
        dlam_ref[...] += jnp.sum(d_log_a * r, axis=0, keepdims=True)
        dba_ref[...] += jnp.sum(d_pa, axis=0, keepdims=True)
        dbx_ref[...] += jnp.sum(d_px, axis=0, keepdims=True)
        dcb_ref[...] += jnp.sum(dxc, axis=0, keepdims=True)
        uv = u_ref[...]
        cw = cw_ref[...]
        nxt = dxc_next[...]
        du = dxc * cw[3:4, :]
        dcw_ref[3:4, :] += jnp.sum(uv * dxc, axis=0, keepdims=True)
        for k in range(1, CONV_WIDTH):
            ds = pltpu.roll(dxc, T - k, axis=0)
            bot = jnp.where(row8 >= 8 - k, pltpu.roll(nxt, 8 - k, axis=0), ds[T - 8:T])
            ds = jnp.concatenate([ds[:T - 8], bot], axis=0)
            du = du + ds * cw[3 - k:4 - k, :]
            dcw_ref[3 - k:4 - k, :] += jnp.sum(uv * ds, axis=0, keepdims=True)
        du_ref[...] = du.astype(BF16)
        g_next[...] = g[0:8]
        a_next[...] = a[0:8]
        dxc_next[...] = dxc[0:8]

    seq = pl.BlockSpec((T, LANES), lambda l, c: (nc - 1 - c, l))
    before = pl.BlockSpec((8, LANES), lambda l, c: (jnp.maximum((nc - 1 - c) * tb8 - 1, 0), l))
    vec = pl.BlockSpec((1, LANES), lambda l, c: (0, l))
    cwb = pl.BlockSpec((CONV_WIDTH, LANES), lambda l, c: (0, l))
    mat = pl.BlockSpec((None, LANES, LANES), lambda l, c: (l, 0, 0))
    vshape = jax.ShapeDtypeStruct((1, W), F32)
    mshape = jax.ShapeDtypeStruct((nl, LANES, LANES), F32)
    return pl.pallas_call(
        body, name="lru_bwd", grid=(nl, nc),
        in_specs=[seq, seq, seq, seq, seq, before, cwb, mat, mat, vec, vec, vec],
        out_specs=(seq, seq, cwb, vec, vec, vec, vec, mat, mat),
        out_shape=(jax.ShapeDtypeStruct((S, W), BF16), jax.ShapeDtypeStruct((S, W), BF16),
                   jax.ShapeDtypeStruct((CONV_WIDTH, W), F32), vshape, vshape, vshape, vshape, mshape, mshape),
        scratch_shapes=[pltpu.VMEM((8, LANES), F32)] * 3,
        compiler_params=_cparams(("parallel", "arbitrary")),
    )(dya, u, ug, xc, h, h, conv_w, wa_bd, wx_bd, ba.reshape(1, W), bx.reshape(1, W), lam.reshape(1, W))


def _decay_fwd(f_logit, bf):
    S = f_logit.shape[0]
    T = min(LRU_CHUNK, S)

    def body(f_ref, b_ref, o_ref, carry):
        @pl.when(pl.program_id(0) == 0)
        def _():
            carry[...] = jnp.zeros_like(carry)

        row = lax.broadcasted_iota(jnp.int32, (T, LANES), 0)
        v = jax.nn.log_sigmoid(f_ref[...] + b_ref[...])
        d = 1
        while d < T:
            v = jnp.where(row >= d, v + pltpu.roll(v, d, axis=0), v)
            d *= 2
        v = v + carry[7:8, :]
        carry[...] = v[T - 8:T]
        o_ref[...] = v

    return pl.pallas_call(
        body, name="decay_fwd", grid=(S // T,),
        in_specs=[pl.BlockSpec((T, LANES), lambda c: (c, 0)), pl.BlockSpec((1, LANES), lambda c: (0, 0))],
        out_specs=pl.BlockSpec((T, LANES), lambda c: (c, 0)),
        out_shape=jax.ShapeDtypeStruct((S, LANES), F32), scratch_shapes=[pltpu.VMEM((8, LANES), F32)],
        compiler_params=_cparams(("arbitrary",)),
    )(f_logit, bf)


def _decay_bwd(d_dec, f_logit, bf):
    S = f_logit.shape[0]
    T = min(LRU_CHUNK, S)
    nc = S // T

    def body(dd_ref, f_ref, b_ref, df_ref, db_ref, carry):
        @pl.when(pl.program_id(0) == 0)
        def _():
            carry[...] = jnp.zeros_like(carry)
            db_ref[...] = jnp.zeros_like(db_ref)

        row = lax.broadcasted_iota(jnp.int32, (T, LANES), 0)
        v = dd_ref[...]
        d = 1
        while d < T:
            v = jnp.where(row < T - d, v + pltpu.roll(v, T - d, axis=0), v)
            d *= 2
        v = v + carry[0:1, :]
        carry[...] = v[0:8]
        df = v * jax.nn.sigmoid(-(f_ref[...] + b_ref[...]))
        df_ref[...] = df.astype(BF16)
        db_ref[...] += jnp.sum(df, axis=0, keepdims=True)

    seq = pl.BlockSpec((T, LANES), lambda c: (nc - 1 - c, 0))
    vec = pl.BlockSpec((1, LANES), lambda c: (0, 0))
    return pl.pallas_call(
        body, name="decay_bwd", grid=(nc,), in_specs=[seq, seq, vec], out_specs=(seq, vec),
        out_shape=(jax.ShapeDtypeStruct((S, LANES), BF16), jax.ShapeDtypeStruct((1, LANES), F32)),
        scratch_shapes=[pltpu.VMEM((8, LANES), F32)], compiler_params=_cparams(("arbitrary",)),
    )(d_dec, f_logit, bf)


def _rope_tables(S):
    pos = jnp.arange(S, dtype=F32)
    inv_freq = ROPE_BASE ** (-jnp.arange(0, MLA_ROPE, 2, dtype=F32) / MLA_ROPE)
    ang = pos[:, None] * inv_freq[None, :]
    cos, sin = jnp.cos(ang), jnp.sin(ang)
    half = MLA_ROPE // 2
    z = lambda n: jnp.zeros((S, n), F32)
    c_q = jnp.concatenate([jnp.ones((S, MLA_NOPE), F32), cos, cos, z(HEAD_PAD - MLA_NOPE - MLA_ROPE)], axis=1)
    c_k = jnp.concatenate([z(MLA_NOPE), cos, cos, z(HEAD_PAD - MLA_NOPE - MLA_ROPE)], axis=1)
    s_lo = jnp.concatenate([z(MLA_NOPE), -sin, z(HEAD_PAD - MLA_NOPE - half)], axis=1)
    s_hi = jnp.concatenate([z(MLA_NOPE + half), sin, z(HEAD_PAD - MLA_NOPE - MLA_ROPE)], axis=1)
    return c_q, c_k, s_lo, s_hi


def _rot(v, c, s_lo, s_hi):
    half = MLA_ROPE // 2
    return v * c + pltpu.roll(v, LANES - half, axis=1) * s_lo + pltpu.roll(v, half, axis=1) * s_hi


def _rot_t(dv, c, s_lo, s_hi):
    half = MLA_ROPE // 2
    return dv * c + pltpu.roll(dv * s_lo, half, axis=1) + pltpu.roll(dv * s_hi, LANES - half, axis=1)


def _rope_q(q_pre, c_q, s_lo, s_hi, *, transpose, out_dtype, name):
    S, W = q_pre.shape
    tm = _rows(S)
    fn = _rot_t if transpose else _rot

    def body(q_ref, c_ref, lo_ref, hi_ref, o_ref):
        c, lo, hi = c_ref[...], lo_ref[...], hi_ref[...]
        for hd in range(W // LANES):
            cols = slice(hd * LANES, (hd + 1) * LANES)
            o_ref[:, cols] = fn(q_ref[:, cols] * MLA_SCALE, c, lo, hi).astype(out_dtype)

    blk = pl.BlockSpec((tm, W), lambda i: (i, 0))
    tab = pl.BlockSpec((tm, LANES), lambda i: (i, 0))
    return pl.pallas_call(
        body, name=name, grid=(S // tm,), in_specs=[blk, tab, tab, tab], out_specs=blk,
        out_shape=jax.ShapeDtypeStruct((S, W), out_dtype), compiler_params=_cparams(("parallel",)),
    )(q_pre, c_q, s_lo, s_hi)


def _rope_k(k_pre, k_rope, c_k, s_lo, s_hi):
    S, W = k_pre.shape
    tm = _rows(S)

    def body(k_ref, r_ref, c_ref, lo_ref, hi_ref, o_ref):
        rot = _rot(r_ref[...], c_ref[...], lo_ref[...], hi_ref[...])
        for hd in range(W // LANES):
            cols = slice(hd * LANES, (hd + 1) * LANES)
            o_ref[:, cols] = (k_ref[:, cols] + rot).astype(BF16)

    blk = pl.BlockSpec((tm, W), lambda i: (i, 0))
    tab = pl.BlockSpec((tm, LANES), lambda i: (i, 0))
    return pl.pallas_call(
        body, name="rope_k", grid=(S // tm,), in_specs=[blk, tab, tab, tab, tab], out_specs=blk,
        out_shape=jax.ShapeDtypeStruct((S, W), BF16), compiler_params=_cparams(("parallel",)),
    )(k_pre, k_rope, c_k, s_lo, s_hi)


def _rope_k_bwd(dk, c_k, s_lo, s_hi):
    S, W = dk.shape
    tm = _rows(S)

    def body(dk_ref, c_ref, lo_ref, hi_ref, o_ref):
        tot = dk_ref[:, 0:LANES]
        for hd in range(1, W // LANES):
            tot = tot + dk_ref[:, hd * LANES:(hd + 1) * LANES]
        o_ref[...] = _rot_t(tot, c_ref[...], lo_ref[...], hi_ref[...]).astype(BF16)

    tab = pl.BlockSpec((tm, LANES), lambda i: (i, 0))
    return pl.pallas_call(
        body, name="rope_k_bwd", grid=(S // tm,), in_specs=[pl.BlockSpec((tm, W), lambda i: (i, 0)), tab, tab, tab],
        out_specs=tab, out_shape=jax.ShapeDtypeStruct((S, LANES), BF16), compiler_params=_cparams(("parallel",)),
    )(dk, c_k, s_lo, s_hi)


def _pairs(n, by_key):
    if by_key:
        pr = [(i, j) for j in range(n) for i in range(j, n)]
    else:
        pr = [(i, j) for i in range(n) for j in range(i + 1)]
    return (jnp.asarray(np.array([p[0] for p in pr], np.int32)), jnp.asarray(np.array([p[1] for p in pr], np.int32)), len(pr))


def _unit_mask(shape, unit, key_axis):
    q = lax.broadcasted_iota(jnp.int32, shape, 1 - key_axis)
    k = lax.broadcasted_iota(jnp.int32, shape, key_axis)
    if unit > 1:
        q, k = q // unit, k // unit
    return q >= k


_NT = (((1,), (1,)), ((), ()))


def _attn_fwd(q, k, v, dec_col, dec_row, *, unit, name):
    S, W = q.shape
    H = W // LANES
    T = min(ATTN_TILE, S)
    n = S // T
    qi, kj, npairs = _pairs(n, by_key=False)
    has_dec = dec_col is not None

    def body(qi_ref, kj_ref, *refs):
        if has_dec:
            q_ref, k_ref, v_ref, dc_ref, dr_ref, o_ref, lse_ref, m_s, l_s, acc = refs
        else:
            q_ref, k_ref, v_ref, o_ref, lse_ref, m_s, l_s, acc = refs
        t = pl.program_id(1)
        i, j = qi_ref[t], kj_ref[t]

        @pl.when(j == 0)
        def _():
            m_s[...] = jnp.full_like(m_s, NEG_INF)
            l_s[...] = jnp.zeros_like(l_s)
            acc[...] = jnp.zeros_like(acc)

        def step(diag):
            s = lax.dot_general(q_ref[...], k_ref[...], _NT, preferred_element_type=F32)
            if has_dec:
                s = s + (dc_ref[...] - dr_ref[...])
            if diag:
                s = jnp.where(_unit_mask((T, T), unit, 1), s, NEG_INF)
            m_prev = m_s[...]
            m_new = jnp.maximum(m_prev, jnp.max(s, axis=1, keepdims=True))
            alpha = jnp.exp(m_prev - m_new)
            p = jnp.exp(s - m_new)
            l_s[...] = alpha * l_s[...] + jnp.sum(p, axis=1, keepdims=True)
            acc[...] = alpha * acc[...] + jnp.dot(p.astype(BF16), v_ref[...], preferred_element_type=F32)
            m_s[...] = m_new

        @pl.when(j < i)
        def _():
            step(False)

        @pl.when(j == i)
        def _():
            step(True)
            o_ref[...] = (acc[...] / l_s[...]).astype(BF16)
            lse_ref[...] = m_s[...] + jnp.log(l_s[...])

    qb = pl.BlockSpec((T, LANES), lambda h, t, qi, kj: (qi[t], h))
    kb = pl.BlockSpec((T, LANES), lambda h, t, qi, kj: (kj[t], h))
    colq = pl.BlockSpec((None, T, 1), lambda h, t, qi, kj: (h, qi[t], 0))
    rowk = pl.BlockSpec((None, 1, T), lambda h, t, qi, kj: (h, 0, kj[t]))
    in_specs = [qb, kb, kb] + ([colq, rowk] if has_dec else [])
    args = (q, k, v) + ((dec_col, dec_row) if has_dec else ())
    return pl.pallas_call(
        body, name=name,
        grid_spec=pltpu.PrefetchScalarGridSpec(
            num_scalar_prefetch=2, grid=(H, npairs), in_specs=in_specs, out_specs=(qb, colq),
            scratch_shapes=[pltpu.VMEM((T, 1), F32), pltpu.VMEM((T, 1), F32), pltpu.VMEM((T, LANES), F32)]),
        out_shape=(jax.ShapeDtypeStruct((S, W), BF16), jax.ShapeDtypeStruct((H, S, 1), F32)),
        compiler_params=_cparams(("parallel", "arbitrary")),
    )(qi, kj, *args)


def _attn_delta(do, o):
    S, W = o.shape
    H = W // LANES
    tm = _rows(S)

    def body(do_ref, o_ref, d_ref):
        d_ref[...] = jnp.sum(do_ref[...].astype(F32) * o_ref[...].astype(F32), axis=1, keepdims=True)

    blk = pl.BlockSpec((tm, LANES), lambda h, i: (i, h))
    return pl.pallas_call(
        body, name="attn_delta", grid=(H, S // tm), in_specs=[blk, blk],
        out_specs=pl.BlockSpec((None, tm, 1), lambda h, i: (h, i, 0)),
        out_shape=jax.ShapeDtypeStruct((H, S, 1), F32), compiler_params=_cparams(("parallel", "parallel")),
    )(do, o)


def _attn_bwd_dq(q, k, v, do, lse, delta, dec_col, dec_row, *, unit, out_dtype, name):
    S, W = q.shape
    H = W // LANES
    T = min(ATTN_TILE, S)
    n = S // T
    qi, kj, npairs = _pairs(n, by_key=False)
    has_dec = dec_col is not None

    def body(qi_ref, kj_ref, *refs):
        if has_dec:
            q_ref, k_ref, v_ref, do_ref, lse_ref, dl_ref, dc_ref, dr_ref, dq_ref, dd_ref, acc, dacc = refs
        else:
            q_ref, k_ref, v_ref, do_ref, lse_ref, dl_ref, dq_ref, acc = refs
        t = pl.program_id(1)
        i, j = qi_ref[t], kj_ref[t]

        @pl.when(j == 0)
        def _():
            acc[...] = jnp.zeros_like(acc)
            if has_dec:
                dacc[...] = jnp.zeros_like(dacc)

        def step(diag):
            kv = k_ref[...]
            s = lax.dot_general(q_ref[...], kv, _NT, preferred_element_type=F32)
            if has_dec:
                s = s + (dc_ref[...] - dr_ref[...])
            if diag:
                s = jnp.where(_unit_mask((T, T), unit, 1), s, NEG_INF)
            p = jnp.exp(s - lse_ref[...])
            dp = lax.dot_general(do_ref[...], v_ref[...], _NT, preferred_element_type=F32)
            ds = p * (dp - dl_ref[...])
            acc[...] += jnp.dot(ds.astype(BF16), kv, preferred_element_type=F32)
            if has_dec:
                dacc[...] += jnp.sum(ds, axis=1, keepdims=True)

        @pl.when(j < i)
        def _():
            step(False)

        @pl.when(j == i)
        def _():
            step(True)
            dq_ref[...] = acc[...].astype(out_dtype)
            if has_dec:
                dd_ref[...] = dacc[...]

    qb = pl.BlockSpec((T, LANES), lambda h, t, qi, kj: (qi[t], h))
    kb = pl.BlockSpec((T, LANES), lambda h, t, qi, kj: (kj[t], h))
    colq = pl.BlockSpec((None, T, 1), lambda h, t, qi, kj: (h, qi[t], 0))
    rowk = pl.BlockSpec((None, 1, T), lambda h, t, qi, kj: (h, 0, kj[t]))
    in_specs = [qb, kb, kb, qb, colq, colq] + ([colq, rowk] if has_dec else [])
    args = (q, k, v, do, lse, delta) + ((dec_col, dec_row) if has_dec else ())
    scratch = [pltpu.VMEM((T, LANES), F32)] + ([pltpu.VMEM((T, 1), F32)] if has_dec else [])
    out_specs = (qb, colq) if has_dec else qb
    out_shape = jax.ShapeDtypeStruct((S, W), out_dtype)
    if has_dec:
        out_shape = (out_shape, jax.ShapeDtypeStruct((H, S, 1), F32))
    res = pl.pallas_call(
        body, name=name,
        grid_spec=pltpu.PrefetchScalarGridSpec(num_scalar_prefetch=2, grid=(H, npairs), in_specs=in_specs,
                                               out_specs=out_specs, scratch_shapes=scratch),
        out_shape=out_shape, compiler_params=_cparams(("parallel", "arbitrary")),
    )(qi, kj, *args)
    return res if has_dec else (res, None)


def _attn_bwd_dkv(q, k, v, do, lse_row, delta_row, dec_col, dec_row, *, unit, dk_dtype, name):
    S, W = q.shape
    H = W // LANES
    T = min(ATTN_TILE, S)
    n = S // T
    qi, kj, npairs = _pairs(n, by_key=True)
    has_dec = dec_col is not None

    def body(qi_ref, kj_ref, *refs):
        if has_dec:
            q_ref, k_ref, v_ref, do_ref, lse_ref, dl_ref, dc_ref, dr_ref, dk_ref, dv_ref, dd_ref, kacc, vacc, dacc = refs
        else:
            q_ref, k_ref, v_ref, do_ref, lse_ref, dl_ref, dk_ref, dv_ref, kacc, vacc = refs
        t = pl.program_id(1)
        i, j = qi_ref[t], kj_ref[t]

        @pl.when(i == j)
        def _():
            kacc[...] = jnp.zeros_like(kacc)
            vacc[...] = jnp.zeros_like(vacc)
            if has_dec:
                dacc[...] = jnp.zeros_like(dacc)

        def step(diag):
            qv, dov = q_ref[...], do_ref[...]
            st = lax.dot_general(k_ref[...], qv, _NT, preferred_element_type=F32)
            if has_dec:
                st = st + (dr_ref[...] - dc_ref[...])
            if diag:
                st = jnp.where(_unit_mask((T, T), unit, 0), st, NEG_INF)
            pt = jnp.exp(st - lse_ref[...])
            dpt = lax.dot_general(v_ref[...], dov, _NT, preferred_element_type=F32)
            dst = pt * (dpt - dl_ref[...])
            vacc[...] += jnp.dot(pt.astype(BF16), dov, preferred_element_type=F32)
            kacc[...] += jnp.dot(dst.astype(BF16), qv, preferred_element_type=F32)
            if has_dec:
                dacc[...] -= jnp.sum(dst, axis=1, keepdims=True)

        @pl.when(i == j)
        def _():
            step(True)

        @pl.when(i > j)
        def _():
            step(False)

        @pl.when(i == n - 1)
        def _():
            dk_ref[...] = kacc[...].astype(dk_dtype)
            dv_ref[...] = vacc[...].astype(BF16)
            if has_dec:
                dd_ref[...] = dacc[...]

    qb = pl.BlockSpec((T, LANES), lambda h, t, qi, kj: (qi[t], h))
    kb = pl.BlockSpec((T, LANES), lambda h, t, qi, kj: (kj[t], h))
    rowq = pl.BlockSpec((None, 1, T), lambda h, t, qi, kj: (h, 0, qi[t]))
    colk = pl.BlockSpec((None, T, 1), lambda h, t, qi, kj: (h, kj[t], 0))
    in_specs = [qb, kb, kb, qb, rowq, rowq] + ([colk, rowq] if has_dec else [])
    args = (q, k, v, do, lse_row, delta_row) + ((dec_col, dec_row) if has_dec else ())
    scratch = [pltpu.VMEM((T, LANES), F32)] * 2 + ([pltpu.VMEM((T, 1), F32)] if has_dec else [])
    out_specs = (kb, kb) + ((colk,) if has_dec else ())
    out_shape = (jax.ShapeDtypeStruct((S, W), dk_dtype), jax.ShapeDtypeStruct((S, W), BF16))
    if has_dec:
        out_shape = out_shape + (jax.ShapeDtypeStruct((H, S, 1), F32),)
    res = pl.pallas_call(
        body, name=name,
        grid_spec=pltpu.PrefetchScalarGridSpec(num_scalar_prefetch=2, grid=(H, npairs), in_specs=in_specs,
                                               out_specs=out_specs, scratch_shapes=scratch),
        out_shape=out_shape, compiler_params=_cparams(("parallel", "arbitrary")),
    )(qi, kj, *args)
    return res if has_dec else (res[0], res[1], None)


def _attn_bwd(q, k, v, o, lse, do, dec_col, dec_row, *, unit, dq_dtype, dk_dtype, name):
    H, S = lse.shape[0], lse.shape[1]
    delta = _attn_delta(do, o)
    dq, dd_q = _attn_bwd_dq(q, k, v, do, lse, delta, dec_col, dec_row, unit=unit, out_dtype=dq_dtype, name=name + "_dq")
    dk, dv, dd_k = _attn_bwd_dkv(q, k, v, do, lse.reshape(H, 1, S), delta.reshape(H, 1, S), dec_col, dec_row,
                                 unit=unit, dk_dtype=dk_dtype, name=name + "_dkv")
    d_dec = None if dec_col is None else dd_q + dd_k
    return dq, dk, dv, d_dec


ONES_LANE = 64


def _ones_lane_bias():
    one = np.zeros((HEADS, HEAD_PAD), np.float32)
    one[:, ONES_LANE] = 1.0
    return jnp.asarray(one.reshape(1, HEADS * HEAD_PAD))


def _lane_sum(t):
    tot = t[:, 0:LANES]
    for c in range(1, t.shape[1] // LANES):
        tot = tot + t[:, c * LANES:(c + 1) * LANES]
    return tot


def _fa_fwd(q, k, v, dec_row, *, unit, name):
    S, W = q.shape
    H = W // LANES
    T = min(ATTN_TILE, S)
    n, reps = S // T, T // LANES
    qi, kj, npairs = _pairs(n, by_key=False)
    has_dec = dec_row is not None

    def body(qi_ref, kj_ref, *refs):
        if has_dec:
            q_ref, k_ref, v_ref, dr_ref, o_ref, lse_ref, m_s, acc = refs
        else:
            q_ref, k_ref, v_ref, o_ref, lse_ref, m_s, acc = refs
        t = pl.program_id(1)
        i, j = qi_ref[t], kj_ref[t]

        @pl.when(j == 0)
        def _():
            m_s[...] = jnp.full_like(m_s, NEG_INF)
            acc[...] = jnp.zeros_like(acc)

        def step(diag):
            s = lax.dot_general(q_ref[...], k_ref[...], _NT, preferred_element_type=F32)
            if has_dec:
                s = s - dr_ref[...]
            if diag:
                s = jnp.where(_unit_mask((T, T), unit, 1), s, NEG_INF)
            m_prev = m_s[...]
            m_new = jnp.maximum(m_prev, jnp.max(s, axis=1, keepdims=True))
            alpha = jnp.exp(m_prev - m_new)
            p = jnp.exp(s - jnp.tile(m_new, (1, reps)))
            acc[...] = alpha * acc[...] + jnp.dot(p.astype(BF16), v_ref[...], preferred_element_type=F32)
            m_s[...] = m_new

        @pl.when(j < i)
        def _():
            step(False)

        @pl.when(j == i)
        def _():
            step(True)
            av = acc[...]
            l = av[:, ONES_LANE:ONES_LANE + 1]
            lane = lax.broadcasted_iota(jnp.int32, (T, LANES), 1)
            o_ref[...] = jnp.where(lane < ONES_LANE, av / l, 0.0).astype(BF16)
            lse_ref[...] = m_s[...] + jnp.log(l)

    qb = pl.BlockSpec((T, LANES), lambda h, t, qi, kj: (qi[t], h))
    kb = pl.BlockSpec((T, LANES), lambda h, t, qi, kj: (kj[t], h))
    repq = pl.BlockSpec((None, T, LANES), lambda h, t, qi, kj: (h, qi[t], 0))
    rowk = pl.BlockSpec((None, 1, T), lambda h, t, qi, kj: (h, 0, kj[t]))
    in_specs = [qb, kb, kb] + ([rowk] if has_dec else [])
    args = (q, k, v) + ((dec_row,) if has_dec else ())
    return pl.pallas_call(
        body, name=name,
        grid_spec=pltpu.PrefetchScalarGridSpec(
            num_scalar_prefetch=2, grid=(H, npairs), in_specs=in_specs, out_specs=(qb, repq),
            scratch_shapes=[pltpu.VMEM((T, LANES), F32), pltpu.VMEM((T, LANES), F32)]),
        out_shape=(jax.ShapeDtypeStruct((S, W), BF16), jax.ShapeDtypeStruct((H, S, LANES), F32)),
        compiler_params=_cparams(("parallel", "arbitrary")),
    )(qi, kj, *args)


def _fa_delta(do, o):
    S, W = o.shape
    H = W // LANES
    tm = _rows(S)

    def body(do_ref, o_ref, d_ref):
        for hd in range(H):
            cols = slice(hd * LANES, (hd + 1) * LANES)
            d = jnp.sum(do_ref[:, cols].astype(F32) * o_ref[:, cols].astype(F32), axis=1, keepdims=True)
            d_ref[hd] = jnp.broadcast_to(d, (tm, LANES))

    blk = pl.BlockSpec((tm, W), lambda i: (i, 0))
    return pl.pallas_call(
        body, name="attn_delta", grid=(S // tm,), in_specs=[blk, blk],
        out_specs=pl.BlockSpec((H, tm, LANES), lambda i: (0, i, 0)),
        out_shape=jax.ShapeDtypeStruct((H, S, LANES), F32), compiler_params=_cparams(("parallel",)),
    )(do, o)


def _fa_bwd_dq(q, k, v, do, lse, delta, dec_row, *, unit, out_dtype, name):
    S, W = q.shape
    H = W // LANES
    T = min(ATTN_TILE, S)
    n, reps = S // T, T // LANES
    qi, kj, npairs = _pairs(n, by_key=False)
    has_dec = dec_row is not None

    def body(qi_ref, kj_ref, *refs):
        if has_dec:
            q_ref, k_ref, v_ref, do_ref, lse_ref, dl_ref, dr_ref, dq_ref, dd_ref, acc, dacc = refs
        else:
            q_ref, k_ref, v_ref, do_ref, lse_ref, dl_ref, dq_ref, acc = refs
        t = pl.program_id(1)
        i, j = qi_ref[t], kj_ref[t]

        @pl.when(j == 0)
        def _():
            acc[...] = jnp.zeros_like(acc)
            if has_dec:
                dacc[...] = jnp.zeros_like(dacc)

        def step(diag):
            kv = k_ref[...]
            s = lax.dot_general(q_ref[...], kv, _NT, preferred_element_type=F32)
            if has_dec:
                s = s - dr_ref[...]
            if diag:
                s = jnp.where(_unit_mask((T, T), unit, 1), s, NEG_INF)
            p = jnp.exp(s - jnp.tile(lse_ref[...], (1, reps)))
            dp = lax.dot_general(do_ref[...], v_ref[...], _NT, preferred_element_type=F32)
            ds = p * (dp - jnp.tile(dl_ref[...], (1, reps)))
            acc[...] += jnp.dot(ds.astype(BF16), kv, preferred_element_type=F32)
            if has_dec:
                dacc[...] += _lane_sum(ds)

        @pl.when(j < i)
        def _():
            step(False)

        @pl.when(j == i)
        def _():
            step(True)
            dq_ref[...] = acc[...].astype(out_dtype)
            if has_dec:
                dd_ref[...] = jnp.broadcast_to(jnp.sum(dacc[...], axis=1, keepdims=True), (T, LANES))

    qb = pl.BlockSpec((T, LANES), lambda h, t, qi, kj: (qi[t], h))
    kb = pl.BlockSpec((T, LANES), lambda h, t, qi, kj: (kj[t], h))
    repq = pl.BlockSpec((None, T, LANES), lambda h, t, qi, kj: (h, qi[t], 0))
    rowk = pl.BlockSpec((None, 1, T), lambda h, t, qi, kj: (h, 0, kj[t]))
    in_specs = [qb, kb, kb, qb, repq, repq] + ([rowk] if has_dec else [])
    args = (q, k, v, do, lse, delta) + ((dec_row,) if has_dec else ())
    out_shape = jax.ShapeDtypeStruct((S, W), out_dtype)
    res = pl.pallas_call(
        body, name=name,
        grid_spec=pltpu.PrefetchScalarGridSpec(num_scalar_prefetch=2, grid=(H, npairs), in_specs=in_specs,
                                               out_specs=(qb, repq) if has_dec else qb,
                                               scratch_shapes=[pltpu.VMEM((T, LANES), F32)] * (2 if has_dec else 1)),
        out_shape=(out_shape, jax.ShapeDtypeStruct((H, S, LANES), F32)) if has_dec else out_shape,
        compiler_params=_cparams(("parallel", "arbitrary")),
    )(qi, kj, *args)
    return res if has_dec else (res, None)


def _fa_bwd_dkv(q, k, v, do, lse_row, delta_row, dec_rep, *, unit, dk_dtype, name):
    S, W = q.shape
    H = W // LANES
    T = min(ATTN_TILE, S)
    n, reps = S // T, T // LANES
    qi, kj, npairs = _pairs(n, by_key=True)
    has_dec = dec_rep is not None

    def body(qi_ref, kj_ref, *refs):
        if has_dec:
            q_ref, k_ref, v_ref, do_ref, lse_ref, dl_ref, dc_ref, dk_ref, dv_ref, dd_ref, kacc, vacc, dacc = refs
        else:
            q_ref, k_ref, v_ref, do_ref, lse_ref, dl_ref, dk_ref, dv_ref, kacc, vacc = refs
        t = pl.program_id(1)
        i, j = qi_ref[t], kj_ref[t]

        @pl.when(i == j)
        def _():
            kacc[...] = jnp.zeros_like(kacc)
            vacc[...] = jnp.zeros_like(vacc)
            if has_dec:
                dacc[...] = jnp.zeros_like(dacc)

        def step(diag):
            qv, dov = q_ref[...], do_ref[...]
            st = lax.dot_general(k_ref[...], qv, _NT, preferred_element_type=F32)
            if has_dec:
                st = st - jnp.tile(dc_ref[...], (1, reps))
            if diag:
                st = jnp.where(_unit_mask((T, T), unit, 0), st, NEG_INF)
            pt = jnp.exp(st - lse_ref[...])
            dpt = lax.dot_general(v_ref[...], dov, _NT, preferred_element_type=F32)
            dst = pt * (dpt - dl_ref[...])
            vacc[...] += jnp.dot(pt.astype(BF16), dov, preferred_element_type=F32)
            kacc[...] += jnp.dot(dst.astype(BF16), qv, preferred_element_type=F32)
            if has_dec:
                dacc[...] += _lane_sum(dst)

        @pl.when(i == j)
        def _():
            step(True)

        @pl.when(i > j)
        def _():
            step(False)

        @pl.when(i == n - 1)
        def _():
            dk_ref[...] = kacc[...].astype(dk_dtype)
            dv_ref[...] = vacc[...].astype(BF16)
            if has_dec:
                dd_ref[...] = jnp.broadcast_to(-jnp.sum(dacc[...], axis=1, keepdims=True), (T, LANES))

    qb = pl.BlockSpec((T, LANES), lambda h, t, qi, kj: (qi[t], h))
    kb = pl.BlockSpec((T, LANES), lambda h, t, qi, kj: (kj[t], h))
    rowq = pl.BlockSpec((None, 1, T), lambda h, t, qi, kj: (h, 0, qi[t]))
    repk = pl.BlockSpec((None, T, LANES), lambda h, t, qi, kj: (h, kj[t], 0))
    in_specs = [qb, kb, kb, qb, rowq, rowq] + ([repk] if has_dec else [])
    args = (q, k, v, do, lse_row, delta_row) + ((dec_rep,) if has_dec else ())
    scratch = [pltpu.VMEM((T, LANES), F32)] * (3 if has_dec else 2)
    out_specs = (kb, kb) + ((repk,) if has_dec else ())
    out_shape = (jax.ShapeDtypeStruct((S, W), dk_dtype), jax.ShapeDtypeStruct((S, W), BF16))
    if has_dec:
        out_shape = out_shape + (jax.ShapeDtypeStruct((H, S, LANES), F32),)
    res = pl.pallas_call(
        body, name=name,
        grid_spec=pltpu.PrefetchScalarGridSpec(num_scalar_prefetch=2, grid=(H, npairs), in_specs=in_specs,
                                               out_specs=out_specs, scratch_shapes=scratch),
        out_shape=out_shape, compiler_params=_cparams(("parallel", "arbitrary")),
    )(qi, kj, *args)
    return res if has_dec else (res[0], res[1], None)


def _fa_bwd(q, k, v, o, lse, do, dec_row, dec_rep, *, unit, dq_dtype, dk_dtype, name):
    H, S = lse.shape[0], lse.shape[1]
    delta = _fa_delta(do, o)
    dq, dd_q = _fa_bwd_dq(q, k, v, do, lse, delta, dec_row, unit=unit, out_dtype=dq_dtype, name=name + "_dq")
    one = lambda t: jnp.max(t, axis=2)
    dk, dv, dd_k = _fa_bwd_dkv(q, k, v, do, one(lse).reshape(H, 1, S), one(delta).reshape(H, 1, S), dec_rep,
                               unit=unit, dk_dtype=dk_dtype, name=name + "_dkv")
    return dq, dk, dv, (None if dd_k is None else one(dd_q) + one(dd_k))


def _merge_fwd(ya, yb, yc, gate_logit, gate_b):
    S, D = ya.shape
    tm = min(256, S)

    def body(a_ref, b_ref, c_ref, gl_ref, gb_ref, o_ref):
        g = jax.nn.sigmoid(gl_ref[...] + gb_ref[...])
        o_ref[...] = (g[:, 0:D] * a_ref[...] + g[:, D:2 * D] * b_ref[...] + g[:, 2 * D:3 * D] * c_ref[...]).astype(BF16)

    row = pl.BlockSpec((tm, D), lambda i: (i, 0))
    return pl.pallas_call(
        body, name="merge_fwd", grid=(S // tm,),
        in_specs=[row, row, row, pl.BlockSpec((tm, 3 * D), lambda i: (i, 0)), pl.BlockSpec((1, 3 * D), lambda i: (0, 0))],
        out_specs=row, out_shape=jax.ShapeDtypeStruct((S, D), BF16), compiler_params=_cparams(("parallel",)),
    )(ya, yb, yc, gate_logit, gate_b.reshape(1, 3 * D))


def _merge_bwd(dm, ya, yb, yc, gate_logit, gate_b):
    S, D = ya.shape
    tm = min(256, S)

    def body(dm_ref, a_ref, b_ref, c_ref, gl_ref, gb_ref, da_ref, db_ref, dc_ref, dgl_ref, dgb_ref):
        g = jax.nn.sigmoid(gl_ref[...] + gb_ref[...])
        dmv = dm_ref[...]
        parts = []
        for n, (y_ref, dy_ref) in enumerate(((a_ref, da_ref), (b_ref, db_ref), (c_ref, dc_ref))):
            gn = g[:, n * D:(n + 1) * D]
            dy_ref[...] = (dmv * gn).astype(BF16)
            parts.append(dmv * y_ref[...] * gn * (1.0 - gn))
        dgl = jnp.concatenate(parts, axis=1)
        dgl_ref[...] = dgl.astype(BF16)

        @pl.when(pl.program_id(0) == 0)
        def _():
            dgb_ref[...] = jnp.zeros_like(dgb_ref)

        dgb_ref[...] += jnp.sum(dgl, axis=0, keepdims=True)

    row = pl.BlockSpec((tm, D), lambda i: (i, 0))
    wide = pl.BlockSpec((tm, 3 * D), lambda i: (i, 0))
    vec = pl.BlockSpec((1, 3 * D), lambda i: (0, 0))
    act = jax.ShapeDtypeStruct((S, D), BF16)
    da, db, dc, dgl, dgb = pl.pallas_call(
        body, name="merge_bwd", grid=(S // tm,), in_specs=[row, row, row, row, wide, vec],
        out_specs=(row, row, row, wide, vec),
        out_shape=(act, act, act, jax.ShapeDtypeStruct((S, 3 * D), BF16), jax.ShapeDtypeStruct((1, 3 * D), F32)),
        compiler_params=_cparams(("arbitrary",)),
    )(dm, ya, yb, yc, gate_logit, gate_b.reshape(1, 3 * D))
    return da, db, dc, dgl, dgb.reshape(3 * D)


def _swiglu_fwd(hf):
    S, W2 = hf.shape
    F = W2 // 2
    tm = min(128, S)

    def body(h_ref, o_ref):
        gt, up = h_ref[:, 0:F], h_ref[:, F:W2]
        o_ref[...] = (gt * jax.nn.sigmoid(gt) * up).astype(BF16)

    return pl.pallas_call(
        body, name="swiglu_fwd", grid=(S // tm,), in_specs=[pl.BlockSpec((tm, W2), lambda i: (i, 0))],
        out_specs=pl.BlockSpec((tm, F), lambda i: (i, 0)), out_shape=jax.ShapeDtypeStruct((S, F), BF16),
        compiler_params=_cparams(("parallel",)),
    )(hf)


def _swiglu_bwd(dact, hf):
    S, W2 = hf.shape
    F = W2 // 2
    tm = min(128, S)

    def body(d_ref, h_ref, o_ref):
        gt, up = h_ref[:, 0:F], h_ref[:, F:W2]
        sg = jax.nn.sigmoid(gt)
        dv = d_ref[...]
        o_ref[:, 0:F] = (dv * up * sg * (1.0 + gt * (1.0 - sg))).astype(BF16)
        o_ref[:, F:W2] = (dv * gt * sg).astype(BF16)

    return pl.pallas_call(
        body, name="swiglu_bwd", grid=(S // tm,),
        in_specs=[pl.BlockSpec((tm, F), lambda i: (i, 0)), pl.BlockSpec((tm, W2), lambda i: (i, 0))],
        out_specs=pl.BlockSpec((tm, W2), lambda i: (i, 0)), out_shape=jax.ShapeDtypeStruct((S, W2), BF16),
        compiler_params=_cparams(("parallel",)),
    )(dact, hf)


def _ple_fwd(x, pre, e):
    S, D = x.shape
    tm = _rows(S)

    def body(x_ref, p_ref, e_ref, o_ref):
        o_ref[...] = x_ref[...] + jax.nn.sigmoid(p_ref[...]) * e_ref[...]

    row = pl.BlockSpec((tm, D), lambda i: (i, 0))
    return pl.pallas_call(body, name="ple_fwd", grid=(S // tm,), in_specs=[row, row, row], out_specs=row,
                          out_shape=jax.ShapeDtypeStruct((S, D), F32), compiler_params=_cparams(("parallel",)))(x, pre, e)


def _ple_bwd(dx, pre, e):
    S, D = dx.shape
    tm = _rows(S)

    def body(dx_ref, p_ref, e_ref, dp_ref, de_ref):
        pg = jax.nn.sigmoid(p_ref[...])
        dxv = dx_ref[...]
        dp_ref[...] = (dxv * e_ref[...] * pg * (1.0 - pg)).astype(BF16)
        de_ref[...] = (dxv * pg).astype(BF16)

    row = pl.BlockSpec((tm, D), lambda i: (i, 0))
    act = jax.ShapeDtypeStruct((S, D), BF16)
    return pl.pallas_call(body, name="ple_bwd", grid=(S // tm,), in_specs=[row, row, row], out_specs=(row, row),
                          out_shape=(act, act), compiler_params=_cparams(("parallel",)))(dx, pre, e)


def _pad_heads(w, real):
    K = w.shape[0]
    w = w.reshape(K, HEADS, real)
    return jnp.pad(w, ((0, 0), (0, 0), (0, HEAD_PAD - real))).reshape(K, HEADS * HEAD_PAD)


def _unpad_heads(w, real):
    K = w.shape[0]
    return w.reshape(K, HEADS, HEAD_PAD)[:, :, :real].reshape(K, HEADS * real)


def _pad_head_rows(w, real):
    N = w.shape[1]
    w = w.reshape(HEADS, real, N)
    return jnp.pad(w, ((0, 0), (0, HEAD_PAD - real), (0, 0))).reshape(HEADS * HEAD_PAD, N)


def _unpad_head_rows(w, real):
    N = w.shape[1]
    return w.reshape(HEADS, HEAD_PAD, N)[:, :real].reshape(HEADS * real, N)


def _block_diag(w):
    w = w.reshape(4, 2, 64, 64)
    z = jnp.zeros((4, 64, 64), w.dtype)
    top = jnp.concatenate([w[:, 0], z], axis=2)
    bot = jnp.concatenate([z, w[:, 1]], axis=2)
    return jnp.concatenate([top, bot], axis=1)


def _block_diag_t(w):
    return jnp.stack([w[:, :64, :64], w[:, 64:, 64:]], axis=1).reshape(8, 64, 64)


_IN_SPLITS = (512, 512, 384, 288, 512, 512, 512, 8, 3072)
_IN_OFF = np.concatenate([[0], np.cumsum(_IN_SPLITS)])
_KR_OFF = 64
_SEG_NAMES = ("u", "ug", "cq", "ckv", "kr", "fq", "fk", "fv", "fl", "gate")


def _in_segments(w_in):
    c = lambda n: w_in[:, int(_IN_OFF[n]):int(_IN_OFF[n + 1])]
    kv = c(3)
    kr = jnp.pad(kv[:, MLA_KV_LORA:], ((0, 0), (_KR_OFF, LANES - _KR_OFF - MLA_ROPE)))
    fl = jnp.pad(c(7), ((0, 0), (0, LANES - HEADS)))
    fq = _pad_heads(c(4), FOX_HEAD_DIM) * jnp.asarray(FOX_SCALE, w_in.dtype)
    return [c(0), c(1), c(2), kv[:, :MLA_KV_LORA], kr, fq, _pad_heads(c(5), FOX_HEAD_DIM), _pad_heads(c(6), FOX_HEAD_DIM), fl, c(8)]


def _in_unsegment(dw_p, widths):
    offs = np.concatenate([[0], np.cumsum(widths)])
    seg = [dw_p[:, int(offs[n]):int(offs[n + 1])] for n in range(len(widths))]
    u, ug, cq, ckv, kr, fq, fk, fv, fl, gate = seg
    return jnp.concatenate([
        u, ug, cq, ckv, kr[:, _KR_OFF:_KR_OFF + MLA_ROPE], _unpad_heads(fq, FOX_HEAD_DIM) * FOX_SCALE,
        _unpad_heads(fk, FOX_HEAD_DIM), _unpad_heads(fv, FOX_HEAD_DIM), fl[:, :HEADS], gate], axis=1)


def _split_wuq(wuq):
    return _pad_heads(wuq, MLA_NOPE + MLA_ROPE)


def _split_wukv(wukv):
    w = wukv.reshape(MLA_KV_LORA, HEADS, MLA_NOPE + MLA_V)
    pad = lambda t: jnp.pad(t, ((0, 0), (0, 0), (0, HEAD_PAD - t.shape[2]))).reshape(MLA_KV_LORA, HEADS * HEAD_PAD)
    return pad(w[:, :, :MLA_NOPE]), pad(w[:, :, MLA_NOPE:])


def _merge_wukv(dk_p, dv_p):
    k = dk_p.reshape(MLA_KV_LORA, HEADS, HEAD_PAD)[:, :, :MLA_NOPE]
    v = dv_p.reshape(MLA_KV_LORA, HEADS, HEAD_PAD)[:, :, :MLA_V]
    return jnp.concatenate([k, v], axis=2).reshape(MLA_KV_LORA, HEADS * (MLA_NOPE + MLA_V))


def _heads_layout(d):
    S = d.shape[0]
    t = d[:, :HEADS].T
    return t.reshape(HEADS, 1, S), jnp.broadcast_to(t[:, :, None], (HEADS, S, LANES))


def _layer_fwd(x, p_i, w, tabs):
    c_q, c_k, s_lo, s_hi = tabs
    sv = {"x0": x}
    segs = _in_segments(w["w_in"])
    h = _rmsnorm_fwd(x, w["mix_norm"], "mix_norm_fwd")
    z = {}
    for nm, ws in zip(_SEG_NAMES, segs):
        z[nm] = _mm(h, ws, out_dtype=BF16 if nm in ("fq", "fk", "fv") else F32, bias=_ones_lane_bias() if nm == "fv" else None,
                    name="in_" + nm)
    sv.update(h=h, z=z)
    wa_bd, wx_bd = _block_diag(w["lru_wa"]).astype(BF16), _block_diag(w["lru_wx"]).astype(BF16)
    oa, xc, hs = _lru_fwd(z["u"], z["ug"], w["conv_w"], w["conv_b"], wa_bd, wx_bd, w["lru_ba"], w["lru_bx"], w["lru_lambda"])
    sv.update(oa=oa, xc=xc, hs=hs)
    qn = _rmsnorm_fwd(z["cq"], w["mla_q_norm"], "q_norm_fwd")
    kvn = _rmsnorm_fwd(z["ckv"], w["mla_kv_norm"], "kv_norm_fwd")
    wuq_p = _split_wuq(w["mla_wuq"])
    wk_p, wv_p = _split_wukv(w["mla_wukv"])
    qb = _rope_q(_mm(qn, wuq_p, name="mla_q"), c_q, s_lo, s_hi, transpose=False, out_dtype=BF16, name="rope_q")
    kb = _rope_k(_mm(kvn, wk_p, name="mla_k"), z["kr"], c_k, s_lo, s_hi)
    vb = _mm(kvn, wv_p, out_dtype=BF16, bias=_ones_lane_bias(), name="mla_v")
    ob, lse_b = _fa_fwd(qb, kb, vb, None, unit=64, name="mla_attn")
    sv.update(qn=qn, kvn=kvn, qb=qb, kb=kb, vb=vb, ob=ob, lse_b=lse_b)
    bf = jnp.pad(w["fox_bf"], (0, LANES - HEADS)).reshape(1, LANES)
    dec = _decay_fwd(z["fl"], bf)
    drow, drep = _heads_layout(dec)
    oc, lse_c = _fa_fwd(z["fq"], z["fk"], z["fv"], drow, unit=1, name="fox_attn")
    sv.update(drow=drow, drep=drep, oc=oc, lse_c=lse_c)
    ya = _mm(oa, w["w_br_a"], name="br_a")
    yb = _mm(ob, _pad_head_rows(w["w_br_b"], MLA_V), name="br_b")
    yc = _mm(oc, _pad_head_rows(w["w_br_c"], FOX_HEAD_DIM), name="br_c")
    merged = _merge_fwd(ya, yb, yc, z["gate"], w["gate_b"])
    x1 = _mm(merged, w["w_o"], res=x, name="w_o")
    sv.update(ya=ya, yb=yb, yc=yc, merged=merged, x1=x1)
    hn = _rmsnorm_fwd(x1, w["ffn_norm"], "ffn_norm_fwd")
    hf = _mm(hn, w["w_gate_up"], name="ffn_up")
    act = _swiglu_fwd(hf)
    x2 = _mm(act, w["w_down"], res=x1, name="ffn_down")
    sv.update(hn=hn, hf=hf, act=act, x2=x2)
    pn = _rmsnorm_fwd(x2, w["ple_norm"], "ple_norm_fwd")
    pre = _mm(pn, w["w_ple_gate"], name="ple_gate")
    e = _mm(p_i, w["w_ple"], name="ple_embed")
    x3 = _ple_fwd(x2, pre, e)
    sv.update(pn=pn, pre=pre, e=e, p_i=p_i)
    return x3, sv


def _layer_bwd(dx3, w, sv, tabs):
    c_q, c_k, s_lo, s_hi = tabs
    g = {}
    z = sv["z"]
    dpre, de = _ple_bwd(dx3, sv["pre"], sv["e"])
    g["w_ple"] = _mm(sv["p_i"], de, ta=True, name="d_w_ple")
    g["w_ple_gate"] = _mm(sv["pn"], dpre, ta=True, name="d_w_ple_gate")
    dpn = _mm(dpre, w["w_ple_gate"], tb=True, name="d_pn")
    dx2, g["ple_norm"] = _rmsnorm_bwd(sv["x2"], w["ple_norm"], dpn, add=dx3, name="ple_norm_bwd")
    dact = _mm(dx2, w["w_down"], tb=True, name="d_act")
    g["w_down"] = _mm(sv["act"], dx2, ta=True, name="d_w_down")
    dhf = _swiglu_bwd(dact, sv["hf"])
    g["w_gate_up"] = _mm(sv["hn"], dhf, ta=True, name="d_w_gate_up")
    dhn = _mm(dhf, w["w_gate_up"], tb=True, name="d_hn")
    dx1, g["ffn_norm"] = _rmsnorm_bwd(sv["x1"], w["ffn_norm"], dhn, add=dx2, name="ffn_norm_bwd")
    g["w_o"] = _mm(sv["merged"], dx1, ta=True, name="d_w_o")
    dm = _mm(dx1, w["w_o"], tb=True, name="d_merged")
    dya, dyb, dyc, dgate, g["gate_b"] = _merge_bwd(dm, sv["ya"], sv["yb"], sv["yc"], z["gate"], w["gate_b"])
    wbb_p, wbc_p = _pad_head_rows(w["w_br_b"], MLA_V), _pad_head_rows(w["w_br_c"], FOX_HEAD_DIM)
    g["w_br_a"] = _mm(sv["oa"], dya, ta=True, name="d_w_br_a")
    g["w_br_b"] = _unpad_head_rows(_mm(sv["ob"], dyb, ta=True, name="d_w_br_b"), MLA_V)
    g["w_br_c"] = _unpad_head_rows(_mm(sv["oc"], dyc, ta=True, name="d_w_br_c"), FOX_HEAD_DIM)
    doa = _mm(dya, w["w_br_a"], tb=True, name="d_oa")
    dob = _mm(dyb, wbb_p, tb=True, out_dtype=BF16, name="d_ob")
    doc = _mm(dyc, wbc_p, tb=True, out_dtype=BF16, name="d_oc")
    dfq, dfk, dfv, d_dec = _fa_bwd(z["fq"], z["fk"], z["fv"], sv["oc"], sv["lse_c"], doc, sv["drow"], sv["drep"],
                                   unit=1, dq_dtype=BF16, dk_dtype=BF16, name="fox_attn_bwd")
    d_dec = jnp.pad(d_dec.T, ((0, 0), (0, LANES - HEADS)))
    bf = jnp.pad(w["fox_bf"], (0, LANES - HEADS)).reshape(1, LANES)
    dfl, dbf = _decay_bwd(d_dec, z["fl"], bf)
    g["fox_bf"] = dbf[0, :HEADS]
    dqb, dkb, dvb, _ = _fa_bwd(sv["qb"], sv["kb"], sv["vb"], sv["ob"], sv["lse_b"], dob, None, None,
                               unit=64, dq_dtype=F32, dk_dtype=F32, name="mla_attn_bwd")
    wuq_p = _split_wuq(w["mla_wuq"])
    wk_p, wv_p = _split_wukv(w["mla_wukv"])
    dq_pre = _rope_q(dqb, c_q, s_lo, s_hi, transpose=True, out_dtype=BF16, name="rope_q_bwd")
    dkr = _rope_k_bwd(dkb, c_k, s_lo, s_hi)
    g["mla_wuq"] = _unpad_heads(_mm(sv["qn"], dq_pre, ta=True, name="d_wuq"), MLA_NOPE + MLA_ROPE)
    g["mla_wukv"] = _merge_wukv(_mm(sv["kvn"], dkb, ta=True, name="d_wuk"), _mm(sv["kvn"], dvb, ta=True, name="d_wuv"))
    dqn = _mm(dq_pre, wuq_p, tb=True, name="d_qn")
    dkvn = _mm(dvb, wv_p, tb=True, res=_mm(dkb, wk_p, tb=True, name="d_kvn_k"), name="d_kvn")
    dcq, g["mla_q_norm"] = _rmsnorm_bwd(z["cq"], w["mla_q_norm"], dqn, out_dtype=BF16, name="q_norm_bwd")
    dckv, g["mla_kv_norm"] = _rmsnorm_bwd(z["ckv"], w["mla_kv_norm"], dkvn, out_dtype=BF16, name="kv_norm_bwd")
    wa_bd, wx_bd = _block_diag(w["lru_wa"]).astype(BF16), _block_diag(w["lru_wx"]).astype(BF16)
    du, dug, dcw, dcb, dba, dbx, dlam, dwa, dwx = _lru_bwd(
        doa, z["u"], z["ug"], sv["xc"], sv["hs"], w["conv_w"], wa_bd, wx_bd, w["lru_ba"], w["lru_bx"], w["lru_lambda"])
    g["conv_w"], g["conv_b"], g["lru_ba"], g["lru_bx"] = dcw, dcb[0], dba[0], dbx[0]
    g["lru_lambda"] = dlam[0] * LRU_C * jax.nn.sigmoid(-w["lru_lambda"])
    g["lru_wa"], g["lru_wx"] = _block_diag_t(dwa), _block_diag_t(dwx)
    dsegs = [du, dug, dcq, dckv, dkr, dfq, dfk, dfv, dfl, dgate]
    dz = jnp.concatenate(dsegs, axis=1)
    w_in_p = jnp.concatenate(_in_segments(w["w_in"]), axis=1)
    g["w_in"] = _in_unsegment(_mm(sv["h"], dz, ta=True, name="d_w_in"), [d.shape[1] for d in dsegs])
    dh = _mm(dz, w_in_p, tb=True, name="d_h")
    dx0, g["mix_norm"] = _rmsnorm_bwd(sv["x0"], w["mix_norm"], dh, add=dx1, name="mix_norm_bwd")
    return dx0, g


_LAYER_WEIGHTS = ("mix_norm", "w_in", "gate_b", "conv_w", "conv_b", "lru_wa", "lru_ba", "lru_wx", "lru_bx", "lru_lambda",
                  "mla_q_norm", "mla_wuq", "mla_kv_norm", "mla_wukv", "fox_bf", "w_br_a", "w_br_b", "w_br_c", "w_o",
                  "ffn_norm", "w_gate_up", "w_down", "ple_norm", "w_ple_gate", "w_ple")
_BIG = ("w_in", "mla_wuq", "mla_wukv", "w_br_a", "w_br_b", "w_br_c", "w_o", "w_gate_up", "w_down", "w_ple_gate", "w_ple")
_ROW_SHARDED = ("w_o", "w_down", "w_ple_gate")
_SMALL = ("mix_norm", "gate_b", "conv_b", "lru_wa", "lru_ba", "lru_wx", "lru_bx", "lru_lambda", "mla_q_norm", "mla_kv_norm",
          "fox_bf", "ffn_norm", "ple_norm")


def _local_step(x, p, layers, final_norm, target):
    tabs = _rope_tables(x.shape[0])
    saved = []
    for i in range(DEPTH):
        x, sv = _layer_fwd(x, p[i], layers[i], tabs)
        saved.append(sv)
    loss, dx, d_final = _loss_head(x, final_norm, target)
    grads = [None] * DEPTH
    for i in reversed(range(DEPTH)):
        dx, grads[i] = _layer_bwd(dx, layers[i], saved[i], tabs)
    return loss, dx, grads, d_final


def _hbm():
    return pl.BlockSpec(memory_space=pltpu.HBM)


def _peers(x, y):
    return [(1 - x, y), (x, 1 - y), (1 - x, 1 - y)]


def _gather_chips_two_level(shard, name):
    R, W = shard.shape
    Rh = R // 2

    def body(src_ref, out_ref, send_sems, recv_sems):
        x, y, c = lax.axis_index("x"), lax.axis_index("y"), lax.axis_index("c")
        me = 2 * x + y
        mine, other = pl.ds(c * Rh, Rh), pl.ds((1 - c) * Rh, Rh)
        peers = _peers(x, y)

        def copy(j, src, slot, rows, to):
            return pltpu.make_async_remote_copy(src_ref=src, dst_ref=out_ref.at[slot, rows], send_sem=send_sems.at[j],
                                                recv_sem=recv_sems.at[j], device_id=to, device_id_type=MESH)

        first = [copy(j, src_ref.at[mine], me, mine, (px, py, c)) for j, (px, py) in enumerate(peers)]
        for cp in first:
            cp.start()
        passed = []
        for j, (px, py) in enumerate(peers):
            slot = 2 * px + py
            copy(j, src_ref.at[mine], slot, mine, (px, py, c)).wait_recv()
            cp = copy(3 + j, out_ref.at[slot, mine], slot, mine, (x, y, 1 - c))
            cp.start()
            passed.append(cp)
        for j, (px, py) in enumerate(peers):
            copy(3 + j, src_ref.at[other], 2 * px + py, other, (x, y, 1 - c)).wait_recv()
        for cp in first + passed:
            cp.wait_send()

    return pl.pallas_call(
        body, name=name, in_specs=[_hbm()], out_specs=_hbm(), out_shape=jax.ShapeDtypeStruct((4, R, W), shard.dtype),
        scratch_shapes=[pltpu.SemaphoreType.DMA((6,)), pltpu.SemaphoreType.DMA((6,))],
    )(shard)


def _gather_chips(shard, name):
    R, W = shard.shape

    def body(src_ref, out_ref, send_sems, recv_sems, local_sem):
        x, y, c = lax.axis_index("x"), lax.axis_index("y"), lax.axis_index("c")
        me = 2 * x + y
        mine = pltpu.make_async_copy(src_ref, out_ref.at[me], local_sem)
        mine.start()

        def copy(j, slot, to):
            return pltpu.make_async_remote_copy(src_ref=src_ref, dst_ref=out_ref.at[slot], send_sem=send_sems.at[j],
                                                recv_sem=recv_sems.at[j], device_id=(to[0], to[1], c), device_id_type=MESH)

        sends = [copy(j, me, peer) for j, peer in enumerate(_peers(x, y))]
        for cp in sends:
            cp.start()
        for j, peer in enumerate(_peers(x, y)):
            copy(j, 2 * peer[0] + peer[1], peer).wait_recv()
        for cp in sends:
            cp.wait_send()
        mine.wait()

    return pl.pallas_call(
        body, name=name, in_specs=[_hbm()], out_specs=_hbm(), out_shape=jax.ShapeDtypeStruct((4, R, W), shard.dtype),
        scratch_shapes=[pltpu.SemaphoreType.DMA((3,)), pltpu.SemaphoreType.DMA((3,)), pltpu.SemaphoreType.DMA],
    )(shard)


def _pair_swap_halves(g4):
    n, R, W = g4.shape
    Rh = R // 2

    def body(src_ref, out_ref, send_sem, recv_sem):
        x, y, c = lax.axis_index("x"), lax.axis_index("y"), lax.axis_index("c")
        cp = pltpu.make_async_remote_copy(src_ref=src_ref.at[:, pl.ds((1 - c) * Rh, Rh), :], dst_ref=out_ref, send_sem=send_sem,
                                          recv_sem=recv_sem, device_id=(x, y, 1 - c), device_id_type=MESH)
        cp.start()
        cp.wait()

    return pl.pallas_call(
        body, name="grad_pair_swap", in_specs=[_hbm()], out_specs=_hbm(), out_shape=jax.ShapeDtypeStruct((n, Rh, W), g4.dtype),
        scratch_shapes=[pltpu.SemaphoreType.DMA, pltpu.SemaphoreType.DMA],
    )(g4)


def _pair_add(g4, sib, c_arr):
    n, R, W = g4.shape
    Rh = R // 2
    tr = _tile_rows(Rh)
    nb = Rh // tr

    def body(c_ref, a_ref, b_ref, o_ref):
        o_ref[...] = (a_ref[...].astype(F32) + b_ref[...].astype(F32)).astype(o_ref.dtype)

    return pl.pallas_call(
        body, name="grad_pair_add",
        grid_spec=pltpu.PrefetchScalarGridSpec(
            num_scalar_prefetch=1, grid=(n, nb),
            in_specs=[pl.BlockSpec((None, tr, W), lambda s, i, c: (s, c[0] * nb + i, 0)), pl.BlockSpec((None, tr, W), lambda s, i, c: (s, i, 0))],
            out_specs=pl.BlockSpec((None, tr, W), lambda s, i, c: (s, i, 0))),
        out_shape=jax.ShapeDtypeStruct((n, Rh, W), g4.dtype), compiler_params=_cparams(("parallel", "parallel")),
    )(c_arr, g4, sib)


def _tile_rows(n):
    for t in (512, 480, 400, 320, 256, 240, 160, 128, 80, 64, 40, 32, 16, 8):
        if n % t == 0:
            return t
    return n


def _chips_exchange(part):
    n, Rh, W = part.shape

    def body(src_ref, out_ref, send_sems, recv_sems):
        x, y, c = lax.axis_index("x"), lax.axis_index("y"), lax.axis_index("c")

        def copy(j, to):
            return pltpu.make_async_remote_copy(src_ref=src_ref.at[2 * to[0] + to[1]], dst_ref=out_ref.at[j], send_sem=send_sems.at[j],
                                                recv_sem=recv_sems.at[j], device_id=(to[0], to[1], c), device_id_type=MESH)

        cps = [copy(j, peer) for j, peer in enumerate(_peers(x, y))]
        for cp in cps:
            cp.start()
        for cp in cps:
            cp.wait()

    return pl.pallas_call(
        body, name="grad_chips_exchange", in_specs=[_hbm()], out_specs=_hbm(), out_shape=jax.ShapeDtypeStruct((3, Rh, W), part.dtype),
        scratch_shapes=[pltpu.SemaphoreType.DMA((3,)), pltpu.SemaphoreType.DMA((3,))],
    )(part)


def _chips_add(part, got, k_arr, c_arr):
    n, Rh, W = part.shape
    tr = _tile_rows(Rh)
    nb = Rh // tr

    def body(k_ref, c_ref, a_ref, b_ref, o_ref):
        mine = pl.program_id(0) == c_ref[0]

        @pl.when(mine)
        def _():
            o_ref[...] = ((a_ref[...].astype(F32) + b_ref[0].astype(F32)) + b_ref[1].astype(F32)) + b_ref[2].astype(F32)

        @pl.when(jnp.logical_not(mine))
        def _():
            o_ref[...] = jnp.zeros_like(o_ref)

    return pl.pallas_call(
        body, name="grad_chips_add",
        grid_spec=pltpu.PrefetchScalarGridSpec(
            num_scalar_prefetch=2, grid=(2, nb),
            in_specs=[pl.BlockSpec((None, tr, W), lambda h, i, k, c: (k[0], i, 0)), pl.BlockSpec((3, tr, W), lambda h, i, k, c: (0, i, 0))],
            out_specs=pl.BlockSpec((tr, W), lambda h, i, k, c: (h * nb + i, 0))),
        out_shape=jax.ShapeDtypeStruct((2 * Rh, W), F32), compiler_params=_cparams(("parallel", "parallel")),
    )(k_arr, c_arr, part, got)


def _pair_gather(buf):
    R, W = buf.shape
    Rh = R // 2

    def body(src_ref, out_ref, send_sem, recv_sem):
        x, y, c = lax.axis_index("x"), lax.axis_index("y"), lax.axis_index("c")
        mine, other = pl.ds(c * Rh, Rh), pl.ds((1 - c) * Rh, Rh)
        pltpu.make_async_remote_copy(src_ref=src_ref.at[mine], dst_ref=out_ref.at[mine], send_sem=send_sem, recv_sem=recv_sem,
                                     device_id=(x, y, 1 - c), device_id_type=MESH).start()
        pltpu.make_async_remote_copy(src_ref=src_ref.at[mine], dst_ref=out_ref.at[other], send_sem=send_sem, recv_sem=recv_sem,
                                     device_id=(x, y, 1 - c), device_id_type=MESH).wait()

    return pl.pallas_call(
        body, name="grad_pair_gather", in_specs=[_hbm()], out_specs=_hbm(), out_shape=jax.ShapeDtypeStruct((R, W), buf.dtype),
        input_output_aliases={0: 0}, scratch_shapes=[pltpu.SemaphoreType.DMA, pltpu.SemaphoreType.DMA],
    )(buf)


def _gather_all(buf):
    R, W = buf.shape

    def body(src_ref, out_ref, send_sems, recv_sems, local_sem):
        x, y, c = lax.axis_index("x"), lax.axis_index("y"), lax.axis_index("c")
        me = 4 * x + 2 * y + c
        mine = pltpu.make_async_copy(src_ref, out_ref.at[me], local_sem)
        mine.start()
        rel = [((x + (r >> 2 & 1)) % 2, (y + (r >> 1 & 1)) % 2, (c + (r & 1)) % 2) for r in range(1, 8)]

        def copy(j, slot, to):
            return pltpu.make_async_remote_copy(src_ref=src_ref, dst_ref=out_ref.at[slot], send_sem=send_sems.at[j],
                                                recv_sem=recv_sems.at[j], device_id=to, device_id_type=MESH)

        sends = [copy(j, me, to) for j, to in enumerate(rel)]
        for cp in sends:
            cp.start()
        for j, to in enumerate(rel):
            copy(j, 4 * to[0] + 2 * to[1] + to[2], to).wait_recv()
        for cp in sends:
            cp.wait_send()
        mine.wait()

    return pl.pallas_call(
        body, name="small_gather", in_specs=[_hbm()], out_specs=_hbm(), out_shape=jax.ShapeDtypeStruct((8, R, W), buf.dtype),
        scratch_shapes=[pltpu.SemaphoreType.DMA((7,)), pltpu.SemaphoreType.DMA((7,)), pltpu.SemaphoreType.DMA],
    )(buf)


def _sum_slots(stack):
    n, R, W = stack.shape
    tr = _tile_rows(R)

    def body(s_ref, o_ref):
        tot = s_ref[0]
        for j in range(1, n):
            tot = tot + s_ref[j]
        o_ref[...] = tot

    return pl.pallas_call(
        body, name="small_sum", grid=(R // tr,), in_specs=[pl.BlockSpec((n, tr, W), lambda i: (0, i, 0))],
        out_specs=pl.BlockSpec((tr, W), lambda i: (i, 0)), out_shape=jax.ShapeDtypeStruct((R, W), F32),
        compiler_params=_cparams(("parallel",)),
    )(stack)


def _adamw(wp, gp, mp, vp, name):
    R, W = wp.shape
    tr = R
    for t in (1024, 512, 256, 128, 64, 32, 16, 8):
        if R % t == 0 and t * W <= 512 * 1024:
            tr = t
            break
    c1 = 1.0 - ADAM_B1 ** ADAM_STEP
    c2 = 1.0 - ADAM_B2 ** ADAM_STEP

    def body(w_ref, g_ref, m_ref, v_ref, d_ref, mo_ref, vo_ref):
        gv = g_ref[...]
        m = ADAM_B1 * m_ref[...] + (1.0 - ADAM_B1) * gv
        v = ADAM_B2 * v_ref[...] + (1.0 - ADAM_B2) * (gv * gv)
        m_hat = m / c1
        v_hat = v / c2
        d_ref[...] = -ADAM_LR * (m_hat / (jnp.sqrt(v_hat) + ADAM_EPS) + ADAM_WD * w_ref[...])
        mo_ref[...] = m
        vo_ref[...] = v

    blk = pl.BlockSpec((tr, W), lambda i: (i, 0))
    shp = jax.ShapeDtypeStruct((R, W), F32)
    return pl.pallas_call(body, name=name, grid=(R // tr,), in_specs=[blk] * 4, out_specs=(blk,) * 3, out_shape=(shp,) * 3,
                          compiler_params=_cparams(("parallel",)))(wp, gp, mp, vp)


def _pack(arrs, rows):
    flat = jnp.concatenate([a.reshape(-1) for a in arrs])
    return jnp.pad(flat, (0, rows * PACK_W - flat.shape[0])).reshape(rows, PACK_W)


def _unpack(buf, shapes):
    flat = buf.reshape(-1)
    out, off = [], 0
    for shp in shapes:
        n = int(np.prod(shp))
        out.append(flat[off:off + n].reshape(shp))
        off += n
    return out


def _rows_for(shapes, mult):
    n = sum(int(np.prod(s)) for s in shapes)
    rows = -(-n // PACK_W)
    return -(-rows // mult) * mult


def _shard_major(g, name):
    L, K, N = g.shape
    if name in _ROW_SHARDED:
        t = g.reshape(L, 4, K // 4, N).transpose(1, 0, 2, 3)
    else:
        t = g.reshape(L, K, 4, N // 4).transpose(2, 0, 1, 3)
    return t.reshape(4, -1, PACK_W)


def _join_shards(blocks, name):
    return jnp.concatenate(blocks, axis=1 if name in _ROW_SHARDED else 2)


def kernel(x, p, mix_norm, w_in, gate_b, conv_w, conv_b, lru_wa, lru_ba, lru_wx, lru_bx, lru_lambda, mla_q_norm, mla_wuq, mla_kv_norm, mla_wukv, fox_bf, w_br_a, w_br_b, w_br_c, w_o, ffn_norm, w_gate_up, w_down, ple_norm, w_ple_gate, w_ple, final_norm, loss_target, m_mix_norm, m_w_in, m_gate_b, m_conv_w, m_conv_b, m_lru_wa, m_lru_ba, m_lru_wx, m_lru_bx, m_lru_lambda, m_mla_q_norm, m_mla_wuq, m_mla_kv_norm, m_mla_wukv, m_fox_bf, m_w_br_a, m_w_br_b, m_w_br_c, m_w_o, m_ffn_norm, m_w_gate_up, m_w_down, m_ple_norm, m_w_ple_gate, m_w_ple, m_final_norm, v_mix_norm, v_w_in, v_gate_b, v_conv_w, v_conv_b, v_lru_wa, v_lru_ba, v_lru_wx, v_lru_bx, v_lru_lambda, v_mla_q_norm, v_mla_wuq, v_mla_kv_norm, v_mla_wukv, v_fox_bf, v_w_br_a, v_w_br_b, v_w_br_c, v_w_o, v_ffn_norm, v_w_gate_up, v_w_down, v_ple_norm, v_w_ple_gate, v_w_ple, v_final_norm):
    a = dict(locals())
    names = list(_LAYER_WEIGHTS) + ["final_norm"]
    W = {n: a[n] for n in names}
    M = {n: a["m_" + n] for n in names}
    V = {n: a["v_" + n] for n in names}
    ix, iy, ic = lax.axis_index("x"), lax.axis_index("y"), lax.axis_index("c")

    sharded = list(_BIG) + ["conv_w"]
    shard_shapes = [W[n].shape for n in sharded]
    R = _rows_for(shard_shapes, 64)
    w_bf = _pack([W[n].astype(BF16) for n in sharded], R)
    gathered = _gather_chips_two_level(w_bf, "weight_gather")
    gathered = lax.dynamic_update_slice(gathered, w_bf[None], (2 * ix + iy, 0, 0))
    per_chip = [_unpack(gathered[k], shard_shapes) for k in range(4)]
    full = {n: _join_shards([per_chip[k][j] for k in range(4)], n) for j, n in enumerate(_BIG)}
    conv_blocks = _gather_chips(conv_w.reshape(DEPTH * CONV_WIDTH, LANES), "conv_w_gather")
    conv_w_full = jnp.concatenate([conv_blocks[k].reshape(DEPTH, CONV_WIDTH, LANES) for k in range(4)], axis=-1)
    layers = []
    for i in range(DEPTH):
        lw = {n: W[n][i] for n in _SMALL}
        for n in _BIG:
            lw[n] = full[n][i]
        lw["conv_w"] = conv_w_full[i]
        layers.append(lw)

    loss_sum, dx, grads, d_final = _local_step(x[0], p[:, 0], layers, final_norm, loss_target[0])
    loss = lax.psum(loss_sum, ("x", "y", "c"))

    parts = [_shard_major(jnp.stack([grads[i][n] for i in range(DEPTH)]), n).astype(BF16) for n in sharded]
    g4, off = jnp.zeros((4, R, PACK_W), BF16), 0
    for t in parts:
        g4 = lax.dynamic_update_slice(g4, t, (0, off, 0))
        off += t.shape[1]
    c_arr = jnp.reshape(ic, (1,)).astype(jnp.int32)
    k_arr = jnp.reshape(2 * ix + iy, (1,)).astype(jnp.int32)
    pair = _pair_add(g4, _pair_swap_halves(g4), c_arr)
    g_pack = _pair_gather(_chips_add(pair, _chips_exchange(pair), k_arr, c_arr))
    big_out = {}
    for n, gsh in zip(sharded, _unpack(g_pack, shard_shapes)):
        view = lambda t: t.reshape(-1, t.shape[-1])
        d, nm, nv = _adamw(view(W[n]), view(gsh), view(M[n]), view(V[n]), "adamw_" + n)
        for key, arr in (("g", gsh), ("d", d), ("m", nm), ("v", nv)):
            big_out[(key, n)] = arr.reshape(W[n].shape)

    pick = lambda src, n, i: src[n] if i is None else src[n][i]
    small = [(n, i) for i in range(DEPTH) for n in _SMALL] + [("final_norm", None)]
    small_shapes = [pick(W, n, i).shape for n, i in small]
    Rs = _rows_for(small_shapes, 8)
    sg = _pack([d_final if i is None else grads[i][n] for n, i in small], Rs)
    sg = _sum_slots(_gather_all(sg))
    sw = _pack([pick(W, n, i) for n, i in small], Rs)
    sm = _pack([pick(M, n, i) for n, i in small], Rs)
    sv_ = _pack([pick(V, n, i) for n, i in small], Rs)
    sd, snm, snv = _adamw(sw, sg, sm, sv_, "adamw_replicated")
    small_out = {}
    for key, buf in (("g", sg), ("d", sd), ("m", snm), ("v", snv)):
        for (n, i), arr in zip(small, _unpack(buf, small_shapes)):
            small_out[(key, n, i)] = arr

    def assemble(key, n):
        if n == "final_norm":
            return small_out[(key, n, None)]
        if n in sharded:
            return big_out[(key, n)]
        return jnp.stack([small_out[(key, n, i)] for i in range(DEPTH)])

    outs = [loss, dx[None]]
    for key in ("g", "d", "m", "v"):
        outs += [assemble(key, n) for n in names]
    return tuple(outs)
```

```python
import functools
import math

import numpy as np
import jax
import jax.numpy as jnp
from jax import lax
from jax.experimental import pallas as pl
from jax.experimental.pallas import tpu as pltpu

F32, BF16 = jnp.float32, jnp.bfloat16
MESH = pl.DeviceIdType.MESH

D_MODEL = 1024
DEPTH = 2
EPS = 1e-6
NEG_INF = -1e30
LRU_WIDTH = 512
LRU_HEADS = 8
LRU_C = 8.0
CONV_WIDTH = 4
HEADS = 8
MLA_Q_LORA = 384
MLA_KV_LORA = 256
MLA_NOPE = 64
MLA_ROPE = 32
MLA_V = 64
ROPE_BASE = 10000.0
FOX_HEAD_DIM = 64
D_FF = 2816
PLE_DIM = 256
HEAD_PAD = 128
MLA_SCALE = (MLA_NOPE + MLA_ROPE) ** -0.5
FOX_SCALE = FOX_HEAD_DIM ** -0.5

ADAM_LR, ADAM_B1, ADAM_B2, ADAM_EPS, ADAM_WD, ADAM_STEP = 0.001, 0.9, 0.999, 1e-08, 0.01, 10

VMEM_LIMIT_BYTES = 48 * 1024 * 1024
LANES = 128
PACK_W = 1024

ROW_TILE = 512
ATTN_TILE = 1024
LRU_CHUNK = 512


def _cparams(dims):
    return pltpu.CompilerParams(dimension_semantics=dims, vmem_limit_bytes=VMEM_LIMIT_BYTES)


def _tile(n, cap):
    if n <= cap:
        return n
    t = (cap // LANES) * LANES
    while t >= LANES:
        if n % t == 0:
            return t
        t -= LANES
    raise ValueError(f"no tile for {n} under {cap}")


def _rows(n):
    return min(ROW_TILE, n)


MM_VMEM_BUDGET = 36 * 1024 * 1024


def _mm_tiles(M, N, K, a_bytes, b_bytes, o_bytes, has_res):
    best, best_work = None, 0
    for tm in {_tile(M, c) for c in (1024, 512, 256)}:
        for tn in {_tile(N, c) for c in (1792, 1024, 512)}:
            for tk in {_tile(K, c) for c in (2048, 1408, 1024, 512)}:
                need = 2 * (tm * tk * a_bytes + tk * tn * b_bytes + tm * tn * o_bytes + (tm * tn * 4 if has_res else 0))
                need += tm * tn * 4 if tk < K else 0
                need += tm * tn * 4
                if need <= MM_VMEM_BUDGET and tm * tn * tk > best_work:
                    best, best_work = (tm, tn, tk), tm * tn * tk
    assert best is not None, (M, N, K)
    return best

def _mm(a, b, *, ta=False, tb=False, out_dtype=F32, res=None, bias=None, name):
    K, M = a.shape if ta else a.shape[::-1]
    N, K2 = b.shape if tb else b.shape[::-1]
    assert K == K2, (name, a.shape, b.shape)
    assert res is None or bias is None
    tm, tn, tk = _mm_tiles(M, N, K, a.dtype.itemsize, b.dtype.itemsize, jnp.dtype(out_dtype).itemsize, res is not None)
    nk = K // tk
    a_spec = pl.BlockSpec((tk, tm), lambda i, j, k: (k, i)) if ta else pl.BlockSpec((tm, tk), lambda i, j, k: (i, k))
    b_spec = pl.BlockSpec((tn, tk), lambda i, j, k: (j, k)) if tb else pl.BlockSpec((tk, tn), lambda i, j, k: (k, j))
    o_spec = pl.BlockSpec((tm, tn), lambda i, j, k: (i, j))
    dn = (((0,) if ta else (1,), (1,) if tb else (0,)), ((), ()))
    if bias is not None:
        res, r_spec = bias, pl.BlockSpec((1, tn), lambda i, j, k: (0, j))
    else:
        r_spec = o_spec
    has_res = res is not None

    def body(*refs):
        a_ref, b_ref = refs[0], refs[1]
        r_ref = refs[2] if has_res else None
        o_ref = refs[3] if has_res else refs[2]
        av, bv = a_ref[...], b_ref[...]
        if av.dtype != BF16:
            av = av.astype(BF16)
        if bv.dtype != BF16:
            bv = bv.astype(BF16)
        part = lax.dot_general(av, bv, dn, preferred_element_type=F32)

        def finish(total):
            if has_res:
                total = total + r_ref[...]
            o_ref[...] = total.astype(out_dtype)

        if nk == 1:
            finish(part)
        else:
            acc = refs[-1]
            k = pl.program_id(2)

            @pl.when(k == 0)
            def _():
                acc[...] = part

            @pl.when(k > 0)
            def _():
                acc[...] += part

            @pl.when(k == nk - 1)
            def _():
                finish(acc[...])

    in_specs = [a_spec, b_spec] + ([r_spec] if has_res else [])
    args = (a, b) + ((res,) if has_res else ())
    return pl.pallas_call(
        body, name=name, grid=(M // tm, N // tn, nk), in_specs=in_specs, out_specs=o_spec,
        out_shape=jax.ShapeDtypeStruct((M, N), out_dtype),
        scratch_shapes=[pltpu.VMEM((tm, tn), F32)] if nk > 1 else [],
        compiler_params=_cparams(("parallel", "parallel", "arbitrary")),
    )(*args)


def _mm_res_norm(a, b, res, g, name):
    M, K = a.shape
    N = b.shape[1]
    tm, tk = _tile(M, 512), _tile(K, 1408)
    nk = K // tk

    def body(a_ref, b_ref, r_ref, g_ref, o_ref, h_ref, *scratch):
        part = jnp.dot(a_ref[...], b_ref[...], preferred_element_type=F32)

        def finish(total):
            xn = total + r_ref[...]
            o_ref[...] = xn
            rstd = lax.rsqrt(jnp.mean(xn * xn, axis=1, keepdims=True) + EPS)
            h_ref[...] = (xn * rstd * g_ref[...]).astype(BF16)

        if nk == 1:
            finish(part)
        else:
            acc = scratch[0]
            k = pl.program_id(1)

            @pl.when(k == 0)
            def _():
                acc[...] = part

            @pl.when(k > 0)
            def _():
                acc[...] += part

            @pl.when(k == nk - 1)
            def _():
                finish(acc[...])

    row = pl.BlockSpec((tm, N), lambda i, k: (i, 0))
    return pl.pallas_call(
        body, name=name, grid=(M // tm, nk),
        in_specs=[pl.BlockSpec((tm, tk), lambda i, k: (i, k)), pl.BlockSpec((tk, N), lambda i, k: (k, 0)), row,
                  pl.BlockSpec((1, N), lambda i, k: (0, 0))],
        out_specs=(row, row), out_shape=(jax.ShapeDtypeStruct((M, N), F32), jax.ShapeDtypeStruct((M, N), BF16)),
        scratch_shapes=[pltpu.VMEM((tm, N), F32)] if nk > 1 else [],
        compiler_params=_cparams(("parallel", "arbitrary")),
    )(a, b, res, g.reshape(1, N))


FFN_TILE = 1408


def _ffn_pair_columns(w_gate_up):
    F = w_gate_up.shape[-1] // 2
    parts = []
    for j in range(F // FFN_TILE):
        parts += [w_gate_up[..., j * FFN_TILE:(j + 1) * FFN_TILE], w_gate_up[..., F + j * FFN_TILE:F + (j + 1) * FFN_TILE]]
    return jnp.concatenate(parts, axis=-1)


def _ffn_unpair_columns(dw):
    F = dw.shape[-1] // 2
    n = F // FFN_TILE
    blk = [dw[..., j * FFN_TILE:(j + 1) * FFN_TILE] for j in range(2 * n)]
    return jnp.concatenate(blk[0::2] + blk[1::2], axis=-1)


def _ffn_up(hn, w_pair):
    S, D = hn.shape
    W2 = w_pair.shape[1]
    F, tf = W2 // 2, FFN_TILE
    tm = _rows(S)

    def body(h_ref, w_ref, hf_ref, act_ref):
        hf = jnp.dot(h_ref[...], w_ref[...], preferred_element_type=F32)
        hf_ref[...] = hf
        gt, up = hf[:, 0:tf], hf[:, tf:2 * tf]
        act_ref[...] = (gt * jax.nn.sigmoid(gt) * up).astype(BF16)

    return pl.pallas_call(
        body, name="ffn_up", grid=(S // tm, F // tf),
        in_specs=[pl.BlockSpec((tm, D), lambda i, j: (i, 0)), pl.BlockSpec((D, 2 * tf), lambda i, j: (0, j))],
        out_specs=(pl.BlockSpec((tm, 2 * tf), lambda i, j: (i, j)), pl.BlockSpec((tm, tf), lambda i, j: (i, j))),
        out_shape=(jax.ShapeDtypeStruct((S, W2), F32), jax.ShapeDtypeStruct((S, F), BF16)),
        compiler_params=_cparams(("parallel", "parallel")),
    )(hn, w_pair)


def _ffn_down_bwd(dx, w_down, hf):
    S, D = dx.shape
    F, tf = w_down.shape[0], FFN_TILE
    tm = _rows(S)

    def body(d_ref, w_ref, h_ref, o_ref):
        dact = lax.dot_general(d_ref[...].astype(BF16), w_ref[...], _NT, preferred_element_type=F32)
        gt, up = h_ref[:, 0:tf], h_ref[:, tf:2 * tf]
        sg = jax.nn.sigmoid(gt)
        o_ref[:, 0:tf] = (dact * up * sg * (1.0 + gt * (1.0 - sg))).astype(BF16)
        o_ref[:, tf:2 * tf] = (dact * gt * sg).astype(BF16)

    pair = pl.BlockSpec((tm, 2 * tf), lambda i, j: (i, j))
    return pl.pallas_call(
        body, name="ffn_down_bwd", grid=(S // tm, F // tf),
        in_specs=[pl.BlockSpec((tm, D), lambda i, j: (i, 0)), pl.BlockSpec((tf, D), lambda i, j: (j, 0)), pair],
        out_specs=pair, out_shape=jax.ShapeDtypeStruct((S, 2 * F), BF16),
        compiler_params=_cparams(("parallel", "parallel")),
    )(dx, w_down, hf)


def _rmsnorm_fwd(x, g, name):
    S, W = x.shape
    tm = _rows(S)

    def body(x_ref, g_ref, o_ref):
        xf = x_ref[...]
        rstd = lax.rsqrt(jnp.mean(xf * xf, axis=1, keepdims=True) + EPS)
        o_ref[...] = (xf * rstd * g_ref[...]).astype(BF16)

    return pl.pallas_call(
        body, name=name, grid=(S // tm,),
        in_specs=[pl.BlockSpec((tm, W), lambda i: (i, 0)), pl.BlockSpec((1, W), lambda i: (0, 0))],
        out_specs=pl.BlockSpec((tm, W), lambda i: (i, 0)),
        out_shape=jax.ShapeDtypeStruct((S, W), BF16), compiler_params=_cparams(("parallel",)),
    )(x, g.reshape(1, W))


def _rmsnorm_bwd(x, g, dy, *, add=None, out_dtype=F32, name):
    S, W = x.shape
    tm = _rows(S)
    has_add = add is not None

    def body(*refs):
        x_ref, g_ref, dy_ref = refs[:3]
        add_ref = refs[3] if has_add else None
        dx_ref, dg_ref = refs[-2], refs[-1]
        xf = x_ref[...]
        rstd = lax.rsqrt(jnp.mean(xf * xf, axis=1, keepdims=True) + EPS)
        xhat = xf * rstd
        dyv = dy_ref[...]
        dxh = dyv * g_ref[...]
        dx = rstd * (dxh - xhat * jnp.mean(dxh * xhat, axis=1, keepdims=True))
        if has_add:
            dx = dx + add_ref[...]
        dx_ref[...] = dx.astype(out_dtype)

        @pl.when(pl.program_id(0) == 0)
        def _():
            dg_ref[...] = jnp.zeros_like(dg_ref)

        dg_ref[...] += jnp.sum(dyv * xhat, axis=0, keepdims=True)

    row = pl.BlockSpec((tm, W), lambda i: (i, 0))
    vec = pl.BlockSpec((1, W), lambda i: (0, 0))
    dx, dg = pl.pallas_call(
        body, name=name, grid=(S // tm,),
        in_specs=[row, vec, row] + ([row] if has_add else []),
        out_specs=(row, vec),
        out_shape=(jax.ShapeDtypeStruct((S, W), out_dtype), jax.ShapeDtypeStruct((1, W), F32)),
        compiler_params=_cparams(("arbitrary",)),
    )(x, g.reshape(1, W), dy, *((add,) if has_add else ()))
    return dx, dg.reshape(W)


def _loss_head(x, g, target):
    S, W = x.shape
    tm = _rows(S)

    def body(x_ref, g_ref, t_ref, loss_ref, dx_ref, dg_ref):
        xf = x_ref[...]
        gv = g_ref[...]
        rstd = lax.rsqrt(jnp.mean(xf * xf, axis=1, keepdims=True) + EPS)
        xhat = xf * rstd
        err = xhat * gv - t_ref[...]
        part = 0.5 * jnp.sum(jnp.mean(err * err, axis=1, keepdims=True), axis=0, keepdims=True)
        dyv = err * (1.0 / W)
        dxh = dyv * gv
        dx_ref[...] = rstd * (dxh - xhat * jnp.mean(dxh * xhat, axis=1, keepdims=True))

        @pl.when(pl.program_id(0) == 0)
        def _():
            dg_ref[...] = jnp.zeros_like(dg_ref)
            loss_ref[...] = jnp.zeros_like(loss_ref)

        dg_ref[...] += jnp.sum(dyv * xhat, axis=0, keepdims=True)
        loss_ref[...] += part

    row = pl.BlockSpec((tm, W), lambda i: (i, 0))
    vec = pl.BlockSpec((1, W), lambda i: (0, 0))
    loss, dx, dg = pl.pallas_call(
        body, name="loss_head", grid=(S // tm,), in_specs=[row, vec, row],
        out_specs=(pl.BlockSpec((1, 1), lambda i: (0, 0)), row, vec),
        out_shape=(jax.ShapeDtypeStruct((1, 1), F32), jax.ShapeDtypeStruct((S, W), F32), jax.ShapeDtypeStruct((1, W), F32)),
        compiler_params=_cparams(("arbitrary",)),
    )(x, g.reshape(1, W), target)
    return loss[0, 0], dx, dg.reshape(W)


def _scan_fwd(a, b, row):
    T = a.shape[0]
    d = 1
    while d < T:
        keep = row >= d
        b = jnp.where(keep, a * pltpu.roll(b, d, axis=0) + b, b)
        a = jnp.where(keep, a * pltpu.roll(a, d, axis=0), a)
        d *= 2
    return a, b


def _scan_bwd(a, b, row):
    T = a.shape[0]
    d = 1
    while d < T:
        keep = row < T - d
        b = jnp.where(keep, a * pltpu.roll(b, T - d, axis=0) + b, b)
        a = jnp.where(keep, a * pltpu.roll(a, T - d, axis=0), a)
        d *= 2
    return a, b


def _expm1(x):
    small = x * (1.0 + x * (0.5 + x * (1.0 / 6 + x * (1.0 / 24 + x * (1.0 / 120 + x * (1.0 / 720 + x * (1.0 / 5040)))))))
    return jnp.where(jnp.abs(x) < 0.25, small, jnp.exp(x) - 1.0)


_GELU_C = math.sqrt(2.0 / math.pi)


def _gelu_and_grad(x):
    inner = _GELU_C * (x + 0.044715 * x * x * x)
    th = jnp.tanh(inner)
    val = 0.5 * x * (1.0 + th)
    grad = 0.5 * (1.0 + th) + 0.5 * x * (1.0 - th * th) * _GELU_C * (1.0 + 3 * 0.044715 * x * x)
    return val, grad


def _lru_gates(xc, wa, wx, ba, bx, lam):
    xcb = xc.astype(BF16)
    r = jax.nn.sigmoid(jnp.dot(xcb, wa, preferred_element_type=F32) + ba)
    ig = jax.nn.sigmoid(jnp.dot(xcb, wx, preferred_element_type=F32) + bx)
    sp = jax.nn.softplus(-lam)
    log_a = -LRU_C * r * sp
    a = jnp.exp(log_a)
    mult = jnp.sqrt(-_expm1(2.0 * log_a))
    return xcb, r, ig, sp, a, mult


def _lru_fwd(u, ug, conv_w, conv_b, wa_bd, wx_bd, ba, bx, lam):
    S, W = u.shape
    T = min(LRU_CHUNK, S)
    nl, nc = W // LANES, S // T

    def body(u_ref, ug_ref, cw_ref, cb_ref, wa_ref, wx_ref, ba_ref, bx_ref, lam_ref, ya_ref, xc_ref, h_ref, prev_u, h_carry):
        c = pl.program_id(1)

        @pl.when(c == 0)
        def _():
            prev_u[...] = jnp.zeros_like(prev_u)
            h_carry[...] = jnp.zeros_like(h_carry)

        uv = u_ref[...]
        row = lax.broadcasted_iota(jnp.int32, (T, LANES), 0)
        row8 = lax.broadcasted_iota(jnp.int32, (8, LANES), 0)
        cw = cw_ref[...]
        xc = cb_ref[...] + uv * cw[3:4, :]
        pv = prev_u[...]
        for k in range(1, CONV_WIDTH):
            us = pltpu.roll(uv, k, axis=0)
            top = jnp.where(row8 < k, pltpu.roll(pv, k, axis=0), us[0:8])
            us = jnp.concatenate([top, us[8:]], axis=0)
            xc = xc + us * cw[3 - k:4 - k, :]
        prev_u[...] = uv[T - 8:T]
        _, r, ig, sp, a, mult = _lru_gates(xc, wa_ref[...], wx_ref[...], ba_ref[...], bx_ref[...], lam_ref[...])
        bb = mult * (ig * xc)
        aa, hh = _scan_fwd(a, bb, row)
        h = hh + aa * h_carry[7:8, :]
        h_carry[...] = h[T - 8:T]
        gl, _ = _gelu_and_grad(ug_ref[...])
        ya_ref[...] = (h * gl).astype(BF16)
        xc_ref[...] = xc
        h_ref[...] = h

    seq = pl.BlockSpec((T, LANES), lambda l, c: (c, l))
    vec = pl.BlockSpec((1, LANES), lambda l, c: (0, l))
    mat = pl.BlockSpec((None, LANES, LANES), lambda l, c: (l, 0, 0))
    return pl.pallas_call(
        body, name="lru_fwd", grid=(nl, nc),
        in_specs=[seq, seq, pl.BlockSpec((CONV_WIDTH, LANES), lambda l, c: (0, l)), vec, mat, mat, vec, vec, vec],
        out_specs=(seq, seq, seq),
        out_shape=(jax.ShapeDtypeStruct((S, W), BF16), jax.ShapeDtypeStruct((S, W), F32), jax.ShapeDtypeStruct((S, W), F32)),
        scratch_shapes=[pltpu.VMEM((8, LANES), F32), pltpu.VMEM((8, LANES), F32)],
        compiler_params=_cparams(("parallel", "arbitrary")),
    )(u, ug, conv_w, conv_b.reshape(1, W), wa_bd, wx_bd, ba.reshape(1, W), bx.reshape(1, W), lam.reshape(1, W))


def _lru_bwd(dya, u, ug, xc, h, conv_w, wa_bd, wx_bd, ba, bx, lam):
    S, W = u.shape
    T = min(LRU_CHUNK, S)
    nl, nc = W // LANES, S // T
    tb8 = T // 8

    def body(dya_ref, u_ref, ug_ref, xc_ref, h_ref, hp_ref, cw_ref, wa_ref, wx_ref, ba_ref, bx_ref, lam_ref,
             du_ref, dug_ref, dcw_ref, dcb_ref, dba_ref, dbx_ref, dlam_ref, dwa_ref, dwx_ref,
             g_next, a_next, dxc_next):
        c = pl.program_id(1)

        @pl.when(c == 0)
        def _():
            g_next[...] = jnp.zeros_like(g_next)
            a_next[...] = jnp.zeros_like(a_next)
            dxc_next[...] = jnp.zeros_like(dxc_next)
            for ref in (dcw_ref, dcb_ref, dba_ref, dbx_ref, dlam_ref, dwa_ref, dwx_ref):
                ref[...] = jnp.zeros_like(ref)

        row = lax.broadcasted_iota(jnp.int32, (T, LANES), 0)
        row8 = lax.broadcasted_iota(jnp.int32, (8, LANES), 0)
        xcv = xc_ref[...]
        wa, wx = wa_ref[...], wx_ref[...]
        xcb, r, ig, sp, a, mult = _lru_gates(xcv, wa, wx, ba_ref[...], bx_ref[...], lam_ref[...])
        gl, dgl = _gelu_and_grad(ug_ref[...])
        dyav = dya_ref[...]
        hv = h_ref[...]
        dug_ref[...] = (dyav * hv * dgl).astype(BF16)
        dh = dyav * gl
        a_up = pltpu.roll(a, T - 1, axis=0)
        a_up = jnp.where(row == T - 1, a_next[0:1, :], a_up)
        prod, gg = _scan_bwd(a_up, dh, row)
        g = gg + prod * g_next[0:1, :]
        h_prev = pltpu.roll(hv, 1, axis=0)
        first_chunk = c == nc - 1
        h_before = jnp.where(first_chunk, 0.0, hp_ref[7:8, :])
        h_prev = jnp.where(row == 0, h_before, h_prev)
        da = g * h_prev
        d_mult = g * (ig * xcv)
        d_ig = g * mult * xcv
        dxc = g * mult * ig
        d_log_a = da * a - d_mult * (a * a) / mult
        d_r = d_log_a * (-LRU_C * sp)
        d_pa = d_r * r * (1.0 - r)
        d_px = d_ig * ig * (1.0 - ig)
        d_pab, d_pxb = d_pa.astype(BF16), d_px.astype(BF16)
        nt = (((1,), (1,)), ((), ()))
        tn = (((0,), (0,)), ((), ()))
        dxc = dxc + lax.dot_general(d_pab, wa, nt, preferred_element_type=F32) + lax.dot_general(d_pxb, wx, nt, preferred_element_type=F32)
        dwa_ref[...] += lax.dot_general(xcb, d_pab, tn, preferred_element_type=F32)
        dwx_ref[...] += lax.dot_general(xcb, d_pxb, tn, preferred_element_type=F32)
        dlam_ref[...] += jnp.sum(d_log_a * r, axis=0, keepdims=True)
        dba_ref[...] += jnp.sum(d_pa, axis=0, keepdims=True)
        dbx_ref[...] += jnp.sum(d_px, axis=0, keepdims=True)
        dcb_ref[...] += jnp.sum(dxc, axis=0, keepdims=True)
        uv = u_ref[...]
        cw = cw_ref[...]
        nxt = dxc_next[...]
        du = dxc * cw[3:4, :]
        dcw_ref[3:4, :] += jnp.sum(uv * dxc, axis=0, keepdims=True)
        for k in range(1, CONV_WIDTH):
            ds = pltpu.roll(dxc, T - k, axis=0)
            bot = jnp.where(row8 >= 8 - k, pltpu.roll(nxt, 8 - k, axis=0), ds[T - 8:T])
            ds = jnp.concatenate([ds[:T - 8], bot], axis=0)
            du = du + ds * cw[3 - k:4 - k, :]
            dcw_ref[3 - k:4 - k, :] += jnp.sum(uv * ds, axis=0, keepdims=True)
        du_ref[...] = du.astype(BF16)
        g_next[...] = g[0:8]
        a_next[...] = a[0:8]
        dxc_next[...] = dxc[0:8]

    seq = pl.BlockSpec((T, LANES), lambda l, c: (nc - 1 - c, l))
    before = pl.BlockSpec((8, LANES), lambda l, c: (jnp.maximum((nc - 1 - c) * tb8 - 1, 0), l))
    vec = pl.BlockSpec((1, LANES), lambda l, c: (0, l))
    cwb = pl.BlockSpec((CONV_WIDTH, LANES), lambda l, c: (0, l))
    mat = pl.BlockSpec((None, LANES, LANES), lambda l, c: (l, 0, 0))
    vshape = jax.ShapeDtypeStruct((1, W), F32)
    mshape = jax.ShapeDtypeStruct((nl, LANES, LANES), F32)
    return pl.pallas_call(
        body, name="lru_bwd", grid=(nl, nc),
        in_specs=[seq, seq, seq, seq, seq, before, cwb, mat, mat, vec, vec, vec],
        out_specs=(seq, seq, cwb, vec, vec, vec, vec, mat, mat),
        out_shape=(jax.ShapeDtypeStruct((S, W), BF16), jax.ShapeDtypeStruct((S, W), BF16),
                   jax.ShapeDtypeStruct((CONV_WIDTH, W), F32), vshape, vshape, vshape, vshape, mshape, mshape),
        scratch_shapes=[pltpu.VMEM((8, LANES), F32)] * 3,
        compiler_params=_cparams(("parallel", "arbitrary")),
    )(dya, u, ug, xc, h, h, conv_w, wa_bd, wx_bd, ba.reshape(1, W), bx.reshape(1, W), lam.reshape(1, W))


def _decay_fwd(f_logit, bf):
    S = f_logit.shape[0]
    T = min(LRU_CHUNK, S)

    def body(f_ref, b_ref, o_ref, carry):
        @pl.when(pl.program_id(0) == 0)
        def _():
            carry[...] = jnp.zeros_like(carry)

        row = lax.broadcasted_iota(jnp.int32, (T, LANES), 0)
        v = jax.nn.log_sigmoid(f_ref[...] + b_ref[...])
        d = 1
        while d < T:
            v = jnp.where(row >= d, v + pltpu.roll(v, d, axis=0), v)
            d *= 2
        v = v + carry[7:8, :]
        carry[...] = v[T - 8:T]
        o_ref[...] = v

    return pl.pallas_call(
        body, name="decay_fwd", grid=(S // T,),
        in_specs=[pl.BlockSpec((T, LANES), lambda c: (c, 0)), pl.BlockSpec((1, LANES), lambda c: (0, 0))],
        out_specs=pl.BlockSpec((T, LANES), lambda c: (c, 0)),
        out_shape=jax.ShapeDtypeStruct((S, LANES), F32), scratch_shapes=[pltpu.VMEM((8, LANES), F32)],
        compiler_params=_cparams(("arbitrary",)),
    )(f_logit, bf)


def _decay_bwd(d_dec, f_logit, bf):
    S = f_logit.shape[0]
    T = min(LRU_CHUNK, S)
    nc = S // T

    def body(dd_ref, f_ref, b_ref, df_ref, db_ref, carry):
        @pl.when(pl.program_id(0) == 0)
        def _():
            carry[...] = jnp.zeros_like(carry)
            db_ref[...] = jnp.zeros_like(db_ref)

        row = lax.broadcasted_iota(jnp.int32, (T, LANES), 0)
        v = dd_ref[...]
        d = 1
        while d < T:
            v = jnp.where(row < T - d, v + pltpu.roll(v, T - d, axis=0), v)
            d *= 2
        v = v + carry[0:1, :]
        carry[...] = v[0:8]
        df = v * jax.nn.sigmoid(-(f_ref[...] + b_ref[...]))
        df_ref[...] = df.astype(BF16)
        db_ref[...] += jnp.sum(df, axis=0, keepdims=True)

    seq = pl.BlockSpec((T, LANES), lambda c: (nc - 1 - c, 0))
    vec = pl.BlockSpec((1, LANES), lambda c: (0, 0))
    return pl.pallas_call(
        body, name="decay_bwd", grid=(nc,), in_specs=[seq, seq, vec], out_specs=(seq, vec),
        out_shape=(jax.ShapeDtypeStruct((S, LANES), BF16), jax.ShapeDtypeStruct((1, LANES), F32)),
        scratch_shapes=[pltpu.VMEM((8, LANES), F32)], compiler_params=_cparams(("arbitrary",)),
    )(d_dec, f_logit, bf)


def _rope_tables(S):
    pos = jnp.arange(S, dtype=F32)
    inv_freq = ROPE_BASE ** (-jnp.arange(0, MLA_ROPE, 2, dtype=F32) / MLA_ROPE)
    ang = pos[:, None] * inv_freq[None, :]
    cos, sin = jnp.cos(ang), jnp.sin(ang)
    half = MLA_ROPE // 2
    z = lambda n: jnp.zeros((S, n), F32)
    c_q = jnp.concatenate([jnp.ones((S, MLA_NOPE), F32), cos, cos, z(HEAD_PAD - MLA_NOPE - MLA_ROPE)], axis=1)
    c_k = jnp.concatenate([z(MLA_NOPE), cos, cos, z(HEAD_PAD - MLA_NOPE - MLA_ROPE)], axis=1)
    s_lo = jnp.concatenate([z(MLA_NOPE), -sin, z(HEAD_PAD - MLA_NOPE - half)], axis=1)
    s_hi = jnp.concatenate([z(MLA_NOPE + half), sin, z(HEAD_PAD - MLA_NOPE - MLA_ROPE)], axis=1)
    return c_q, c_k, s_lo, s_hi


def _rot(v, c, s_lo, s_hi):
    half = MLA_ROPE // 2
    return v * c + pltpu.roll(v, LANES - half, axis=1) * s_lo + pltpu.roll(v, half, axis=1) * s_hi


def _rot_t(dv, c, s_lo, s_hi):
    half = MLA_ROPE // 2
    return dv * c + pltpu.roll(dv * s_lo, half, axis=1) + pltpu.roll(dv * s_hi, LANES - half, axis=1)


def _rope_q(q_pre, c_q, s_lo, s_hi, *, transpose, out_dtype, name):
    S, W = q_pre.shape
    tm = _rows(S)
    fn = _rot_t if transpose else _rot

    def body(q_ref, c_ref, lo_ref, hi_ref, o_ref):
        c, lo, hi = c_ref[...], lo_ref[...], hi_ref[...]
        for hd in range(W // LANES):
            cols = slice(hd * LANES, (hd + 1) * LANES)
            o_ref[:, cols] = fn(q_ref[:, cols] * MLA_SCALE, c, lo, hi).astype(out_dtype)

    blk = pl.BlockSpec((tm, W), lambda i: (i, 0))
    tab = pl.BlockSpec((tm, LANES), lambda i: (i, 0))
    return pl.pallas_call(
        body, name=name, grid=(S // tm,), in_specs=[blk, tab, tab, tab], out_specs=blk,
        out_shape=jax.ShapeDtypeStruct((S, W), out_dtype), compiler_params=_cparams(("parallel",)),
    )(q_pre, c_q, s_lo, s_hi)


def _rope_k(k_pre, k_rope, c_k, s_lo, s_hi):
    S, W = k_pre.shape
    tm = _rows(S)

    def body(k_ref, r_ref, c_ref, lo_ref, hi_ref, o_ref):
        rot = _rot(r_ref[...], c_ref[...], lo_ref[...], hi_ref[...])
        for hd in range(W // LANES):
            cols = slice(hd * LANES, (hd + 1) * LANES)
            o_ref[:, cols] = (k_ref[:, cols] + rot).astype(BF16)

    blk = pl.BlockSpec((tm, W), lambda i: (i, 0))
    tab = pl.BlockSpec((tm, LANES), lambda i: (i, 0))
    return pl.pallas_call(
        body, name="rope_k", grid=(S // tm,), in_specs=[blk, tab, tab, tab, tab], out_specs=blk,
        out_shape=jax.ShapeDtypeStruct((S, W), BF16), compiler_params=_cparams(("parallel",)),
    )(k_pre, k_rope, c_k, s_lo, s_hi)


def _rope_k_bwd(dk, c_k, s_lo, s_hi):
    S, W = dk.shape
    tm = _rows(S)

    def body(dk_ref, c_ref, lo_ref, hi_ref, o_ref):
        tot = dk_ref[:, 0:LANES]
        for hd in range(1, W // LANES):
            tot = tot + dk_ref[:, hd * LANES:(hd + 1) * LANES]
        o_ref[...] = _rot_t(tot, c_ref[...], lo_ref[...], hi_ref[...]).astype(BF16)

    tab = pl.BlockSpec((tm, LANES), lambda i: (i, 0))
    return pl.pallas_call(
        body, name="rope_k_bwd", grid=(S // tm,), in_specs=[pl.BlockSpec((tm, W), lambda i: (i, 0)), tab, tab, tab],
        out_specs=tab, out_shape=jax.ShapeDtypeStruct((S, LANES), BF16), compiler_params=_cparams(("parallel",)),
    )(dk, c_k, s_lo, s_hi)


def _pairs(n, by_key):
    if by_key:
        pr = [(i, j) for j in range(n) for i in range(j, n)]
    else:
        pr = [(i, j) for i in range(n) for j in range(i + 1)]
    return (jnp.asarray(np.array([p[0] for p in pr], np.int32)), jnp.asarray(np.array([p[1] for p in pr], np.int32)), len(pr))


def _unit_mask(shape, unit, key_axis):
    q = lax.broadcasted_iota(jnp.int32, shape, 1 - key_axis)
    k = lax.broadcasted_iota(jnp.int32, shape, key_axis)
    if unit > 1:
        q, k = q // unit, k // unit
    return q >= k


_NT = (((1,), (1,)), ((), ()))


def _attn_fwd(q, k, v, dec_col, dec_row, *, unit, name):
    S, W = q.shape
    H = W // LANES
    T = min(ATTN_TILE, S)
    n = S // T
    qi, kj, npairs = _pairs(n, by_key=False)
    has_dec = dec_col is not None

    def body(qi_ref, kj_ref, *refs):
        if has_dec:
            q_ref, k_ref, v_ref, dc_ref, dr_ref, o_ref, lse_ref, m_s, l_s, acc = refs
        else:
            q_ref, k_ref, v_ref, o_ref, lse_ref, m_s, l_s, acc = refs
        t = pl.program_id(1)
        i, j = qi_ref[t], kj_ref[t]

        @pl.when(j == 0)
        def _():
            m_s[...] = jnp.full_like(m_s, NEG_INF)
            l_s[...] = jnp.zeros_like(l_s)
            acc[...] = jnp.zeros_like(acc)

        def step(diag):
            s = lax.dot_general(q_ref[...], k_ref[...], _NT, preferred_element_type=F32)
            if has_dec:
                s = s + (dc_ref[...] - dr_ref[...])
            if diag:
                s = jnp.where(_unit_mask((T, T), unit, 1), s, NEG_INF)
            m_prev = m_s[...]
            m_new = jnp.maximum(m_prev, jnp.max(s, axis=1, keepdims=True))
            alpha = jnp.exp(m_prev - m_new)
            p = jnp.exp(s - m_new)
            l_s[...] = alpha * l_s[...] + jnp.sum(p, axis=1, keepdims=True)
            acc[...] = alpha * acc[...] + jnp.dot(p.astype(BF16), v_ref[...], preferred_element_type=F32)
            m_s[...] = m_new

        @pl.when(j < i)
        def _():
            step(False)

        @pl.when(j == i)
        def _():
            step(True)
            o_ref[...] = (acc[...] / l_s[...]).astype(BF16)
            lse_ref[...] = m_s[...] + jnp.log(l_s[...])

    qb = pl.BlockSpec((T, LANES), lambda h, t, qi, kj: (qi[t], h))
    kb = pl.BlockSpec((T, LANES), lambda h, t, qi, kj: (kj[t], h))
    colq = pl.BlockSpec((None, T, 1), lambda h, t, qi, kj: (h, qi[t], 0))
    rowk = pl.BlockSpec((None, 1, T), lambda h, t, qi, kj: (h, 0, kj[t]))
    in_specs = [qb, kb, kb] + ([colq, rowk] if has_dec else [])
    args = (q, k, v) + ((dec_col, dec_row) if has_dec else ())
    return pl.pallas_call(
        body, name=name,
        grid_spec=pltpu.PrefetchScalarGridSpec(
            num_scalar_prefetch=2, grid=(H, npairs), in_specs=in_specs, out_specs=(qb, colq),
            scratch_shapes=[pltpu.VMEM((T, 1), F32), pltpu.VMEM((T, 1), F32), pltpu.VMEM((T, LANES), F32)]),
        out_shape=(jax.ShapeDtypeStruct((S, W), BF16), jax.ShapeDtypeStruct((H, S, 1), F32)),
        compiler_params=_cparams(("parallel", "arbitrary")),
    )(qi, kj, *args)


def _attn_delta(do, o):
    S, W = o.shape
    H = W // LANES
    tm = _rows(S)

    def body(do_ref, o_ref, d_ref):
        d_ref[...] = jnp.sum(do_ref[...].astype(F32) * o_ref[...].astype(F32), axis=1, keepdims=True)

    blk = pl.BlockSpec((tm, LANES), lambda h, i: (i, h))
    return pl.pallas_call(
        body, name="attn_delta", grid=(H, S // tm), in_specs=[blk, blk],
        out_specs=pl.BlockSpec((None, tm, 1), lambda h, i: (h, i, 0)),
        out_shape=jax.ShapeDtypeStruct((H, S, 1), F32), compiler_params=_cparams(("parallel", "parallel")),
    )(do, o)


def _attn_bwd_dq(q, k, v, do, lse, delta, dec_col, dec_row, *, unit, out_dtype, name):
    S, W = q.shape
    H = W // LANES
    T = min(ATTN_TILE, S)
    n = S // T
    qi, kj, npairs = _pairs(n, by_key=False)
    has_dec = dec_col is not None

    def body(qi_ref, kj_ref, *refs):
        if has_dec:
            q_ref, k_ref, v_ref, do_ref, lse_ref, dl_ref, dc_ref, dr_ref, dq_ref, dd_ref, acc, dacc = refs
        else:
            q_ref, k_ref, v_ref, do_ref, lse_ref, dl_ref, dq_ref, acc = refs
        t = pl.program_id(1)
        i, j = qi_ref[t], kj_ref[t]

        @pl.when(j == 0)
        def _():
            acc[...] = jnp.zeros_like(acc)
            if has_dec:
                dacc[...] = jnp.zeros_like(dacc)

        def step(diag):
            kv = k_ref[...]
            s = lax.dot_general(q_ref[...], kv, _NT, preferred_element_type=F32)
            if has_dec:
                s = s + (dc_ref[...] - dr_ref[...])
            if diag:
                s = jnp.where(_unit_mask((T, T), unit, 1), s, NEG_INF)
            p = jnp.exp(s - lse_ref[...])
            dp = lax.dot_general(do_ref[...], v_ref[...], _NT, preferred_element_type=F32)
            ds = p * (dp - dl_ref[...])
            acc[...] += jnp.dot(ds.astype(BF16), kv, preferred_element_type=F32)
            if has_dec:
                dacc[...] += jnp.sum(ds, axis=1, keepdims=True)

        @pl.when(j < i)
        def _():
            step(False)

        @pl.when(j == i)
        def _():
            step(True)
            dq_ref[...] = acc[...].astype(out_dtype)
            if has_dec:
                dd_ref[...] = dacc[...]

    qb = pl.BlockSpec((T, LANES), lambda h, t, qi, kj: (qi[t], h))
    kb = pl.BlockSpec((T, LANES), lambda h, t, qi, kj: (kj[t], h))
    colq = pl.BlockSpec((None, T, 1), lambda h, t, qi, kj: (h, qi[t], 0))
    rowk = pl.BlockSpec((None, 1, T), lambda h, t, qi, kj: (h, 0, kj[t]))
    in_specs = [qb, kb, kb, qb, colq, colq] + ([colq, rowk] if has_dec else [])
    args = (q, k, v, do, lse, delta) + ((dec_col, dec_row) if has_dec else ())
    scratch = [pltpu.VMEM((T, LANES), F32)] + ([pltpu.VMEM((T, 1), F32)] if has_dec else [])
    out_specs = (qb, colq) if has_dec else qb
    out_shape = jax.ShapeDtypeStruct((S, W), out_dtype)
    if has_dec:
        out_shape = (out_shape, jax.ShapeDtypeStruct((H, S, 1), F32))
    res = pl.pallas_call(
        body, name=name,
        grid_spec=pltpu.PrefetchScalarGridSpec(num_scalar_prefetch=2, grid=(H, npairs), in_specs=in_specs,
                                               out_specs=out_specs, scratch_shapes=scratch),
        out_shape=out_shape, compiler_params=_cparams(("parallel", "arbitrary")),
    )(qi, kj, *args)
    return res if has_dec else (res, None)


def _attn_bwd_dkv(q, k, v, do, lse_row, delta_row, dec_col, dec_row, *, unit, dk_dtype, name):
    S, W = q.shape
    H = W // LANES
    T = min(ATTN_TILE, S)
    n = S // T
    qi, kj, npairs = _pairs(n, by_key=True)
    has_dec = dec_col is not None

    def body(qi_ref, kj_ref, *refs):
        if has_dec:
            q_ref, k_ref, v_ref, do_ref, lse_ref, dl_ref, dc_ref, dr_ref, dk_ref, dv_ref, dd_ref, kacc, vacc, dacc = refs
        else:
            q_ref, k_ref, v_ref, do_ref, lse_ref, dl_ref, dk_ref, dv_ref, kacc, vacc = refs
        t = pl.program_id(1)
        i, j = qi_ref[t], kj_ref[t]

        @pl.when(i == j)
        def _():
            kacc[...] = jnp.zeros_like(kacc)
            vacc[...] = jnp.zeros_like(vacc)
            if has_dec:
                dacc[...] = jnp.zeros_like(dacc)

        def step(diag):
            qv, dov = q_ref[...], do_ref[...]
            st = lax.dot_general(k_ref[...], qv, _NT, preferred_element_type=F32)
            if has_dec:
                st = st + (dr_ref[...] - dc_ref[...])
            if diag:
                st = jnp.where(_unit_mask((T, T), unit, 0), st, NEG_INF)
            pt = jnp.exp(st - lse_ref[...])
            dpt = lax.dot_general(v_ref[...], dov, _NT, preferred_element_type=F32)
            dst = pt * (dpt - dl_ref[...])
            vacc[...] += jnp.dot(pt.astype(BF16), dov, preferred_element_type=F32)
            kacc[...] += jnp.dot(dst.astype(BF16), qv, preferred_element_type=F32)
            if has_dec:
                dacc[...] -= jnp.sum(dst, axis=1, keepdims=True)

        @pl.when(i == j)
        def _():
            step(True)

        @pl.when(i > j)
        def _():
            step(False)

        @pl.when(i == n - 1)
        def _():
            dk_ref[...] = kacc[...].astype(dk_dtype)
            dv_ref[...] = vacc[...].astype(BF16)
            if has_dec:
                dd_ref[...] = dacc[...]

    qb = pl.BlockSpec((T, LANES), lambda h, t, qi, kj: (qi[t], h))
    kb = pl.BlockSpec((T, LANES), lambda h, t, qi, kj: (kj[t], h))
    rowq = pl.BlockSpec((None, 1, T), lambda h, t, qi, kj: (h, 0, qi[t]))
    colk = pl.BlockSpec((None, T, 1), lambda h, t, qi, kj: (h, kj[t], 0))
    in_specs = [qb, kb, kb, qb, rowq, rowq] + ([colk, rowq] if has_dec else [])
    args = (q, k, v, do, lse_row, delta_row) + ((dec_col, dec_row) if has_dec else ())
    scratch = [pltpu.VMEM((T, LANES), F32)] * 2 + ([pltpu.VMEM((T, 1), F32)] if has_dec else [])
    out_specs = (kb, kb) + ((colk,) if has_dec else ())
    out_shape = (jax.ShapeDtypeStruct((S, W), dk_dtype), jax.ShapeDtypeStruct((S, W), BF16))
    if has_dec:
        out_shape = out_shape + (jax.ShapeDtypeStruct((H, S, 1), F32),)
    res = pl.pallas_call(
        body, name=name,
        grid_spec=pltpu.PrefetchScalarGridSpec(num_scalar_prefetch=2, grid=(H, npairs), in_specs=in_specs,
                                               out_specs=out_specs, scratch_shapes=scratch),
        out_shape=out_shape, compiler_params=_cparams(("parallel", "arbitrary")),
    )(qi, kj, *args)
    return res if has_dec else (res[0], res[1], None)


def _attn_bwd(q, k, v, o, lse, do, dec_col, dec_row, *, unit, dq_dtype, dk_dtype, name):
    H, S = lse.shape[0], lse.shape[1]
    delta = _attn_delta(do, o)
    dq, dd_q = _attn_bwd_dq(q, k, v, do, lse, delta, dec_col, dec_row, unit=unit, out_dtype=dq_dtype, name=name + "_dq")
    dk, dv, dd_k = _attn_bwd_dkv(q, k, v, do, lse.reshape(H, 1, S), delta.reshape(H, 1, S), dec_col, dec_row,
                                 unit=unit, dk_dtype=dk_dtype, name=name + "_dkv")
    d_dec = None if dec_col is None else dd_q + dd_k
    return dq, dk, dv, d_dec


ONES_LANE = 64


def _ones_lane_bias():
    one = np.zeros((HEADS, HEAD_PAD), np.float32)
    one[:, ONES_LANE] = 1.0
    return jnp.asarray(one.reshape(1, HEADS * HEAD_PAD))


def _lane_sum(t):
    tot = t[:, 0:LANES]
    for c in range(1, t.shape[1] // LANES):
        tot = tot + t[:, c * LANES:(c + 1) * LANES]
    return tot


def _fa_fwd(q, k, v, dec_row, *, unit, name):
    S, W = q.shape
    H = W // LANES
    T = min(ATTN_TILE, S)
    n, reps = S // T, T // LANES
    qi, kj, npairs = _pairs(n, by_key=False)
    has_dec = dec_row is not None

    def body(qi_ref, kj_ref, *refs):
        if has_dec:
            q_ref, k_ref, v_ref, dr_ref, o_ref, lse_ref, lrow_ref, m_s, acc = refs
        else:
            q_ref, k_ref, v_ref, o_ref, lse_ref, lrow_ref, m_s, acc = refs
        t = pl.program_id(1)
        i, j = qi_ref[t], kj_ref[t]

        @pl.when(j == 0)
        def _():
            m_s[...] = jnp.full_like(m_s, NEG_INF)
            acc[...] = jnp.zeros_like(acc)

        def step(diag):
            s = lax.dot_general(q_ref[...], k_ref[...], _NT, preferred_element_type=F32)
            if has_dec:
                s = s - dr_ref[...]
            if diag:
                s = jnp.where(_unit_mask((T, T), unit, 1), s, NEG_INF)
            m_prev = m_s[...]
            m_new = jnp.maximum(m_prev, jnp.max(s, axis=1, keepdims=True))
            alpha = jnp.exp(m_prev - m_new)
            p = jnp.exp(s - jnp.tile(m_new, (1, reps)))
            acc[...] = alpha * acc[...] + jnp.dot(p.astype(BF16), v_ref[...], preferred_element_type=F32)
            m_s[...] = m_new

        @pl.when(j < i)
        def _():
            step(False)

        @pl.when(j == i)
        def _():
            step(True)
            av = acc[...]
            l = av[:, ONES_LANE:ONES_LANE + 1]
            lane = lax.broadcasted_iota(jnp.int32, (T, LANES), 1)
            o_ref[...] = jnp.where(lane < ONES_LANE, av / l, 0.0).astype(BF16)
            lse = m_s[...] + jnp.log(l)
            lse_ref[...] = lse
            lrow_ref[...] = lse.T[0:1, :]

    qb = pl.BlockSpec((T, LANES), lambda h, t, qi, kj: (qi[t], h))
    kb = pl.BlockSpec((T, LANES), lambda h, t, qi, kj: (kj[t], h))
    repq = pl.BlockSpec((None, T, LANES), lambda h, t, qi, kj: (h, qi[t], 0))
    rowq = pl.BlockSpec((None, 1, T), lambda h, t, qi, kj: (h, 0, qi[t]))
    rowk = pl.BlockSpec((None, 1, T), lambda h, t, qi, kj: (h, 0, kj[t]))
    in_specs = [qb, kb, kb] + ([rowk] if has_dec else [])
    args = (q, k, v) + ((dec_row,) if has_dec else ())
    return pl.pallas_call(
        body, name=name,
        grid_spec=pltpu.PrefetchScalarGridSpec(
            num_scalar_prefetch=2, grid=(H, npairs), in_specs=in_specs, out_specs=(qb, repq, rowq),
            scratch_shapes=[pltpu.VMEM((T, LANES), F32), pltpu.VMEM((T, LANES), F32)]),
        out_shape=(jax.ShapeDtypeStruct((S, W), BF16), jax.ShapeDtypeStruct((H, S, LANES), F32),
                   jax.ShapeDtypeStruct((H, 1, S), F32)),
        compiler_params=_cparams(("parallel", "arbitrary")),
    )(qi, kj, *args)


def _fa_delta(do, o):
    S, W = o.shape
    H = W // LANES
    tm = _rows(S)

    def body(do_ref, o_ref, d_ref, drow_ref):
        for hd in range(H):
            cols = slice(hd * LANES, (hd + 1) * LANES)
            d = jnp.sum(do_ref[:, cols].astype(F32) * o_ref[:, cols].astype(F32), axis=1, keepdims=True)
            rep = jnp.broadcast_to(d, (tm, LANES))
            d_ref[hd] = rep
            drow_ref[hd] = rep.T[0:1, :]

    blk = pl.BlockSpec((tm, W), lambda i: (i, 0))
    return pl.pallas_call(
        body, name="attn_delta", grid=(S // tm,), in_specs=[blk, blk],
        out_specs=(pl.BlockSpec((H, tm, LANES), lambda i: (0, i, 0)), pl.BlockSpec((H, 1, tm), lambda i: (0, 0, i))),
        out_shape=(jax.ShapeDtypeStruct((H, S, LANES), F32), jax.ShapeDtypeStruct((H, 1, S), F32)),
        compiler_params=_cparams(("parallel",)),
    )(do, o)


def _fa_bwd_dq(q, k, v, do, lse, delta, dec_row, *, unit, out_dtype, name):
    S, W = q.shape
    H = W // LANES
    T = min(ATTN_TILE, S)
    n, reps = S // T, T // LANES
    qi, kj, npairs = _pairs(n, by_key=False)
    has_dec = dec_row is not None

    def body(qi_ref, kj_ref, *refs):
        if has_dec:
            q_ref, k_ref, v_ref, do_ref, lse_ref, dl_ref, dr_ref, dq_ref, dd_ref, acc, dacc = refs
        else:
            q_ref, k_ref, v_ref, do_ref, lse_ref, dl_ref, dq_ref, acc = refs
        t = pl.program_id(1)
        i, j = qi_ref[t], kj_ref[t]

        @pl.when(j == 0)
        def _():
            acc[...] = jnp.zeros_like(acc)
            if has_dec:
                dacc[...] = jnp.zeros_like(dacc)

        def step(diag):
            kv = k_ref[...]
            s = lax.dot_general(q_ref[...], kv, _NT, preferred_element_type=F32)
            if has_dec:
                s = s - dr_ref[...]
            if diag:
                s = jnp.where(_unit_mask((T, T), unit, 1), s, NEG_INF)
            p = jnp.exp(s - jnp.tile(lse_ref[...], (1, reps)))
            dp = lax.dot_general(do_ref[...], v_ref[...], _NT, preferred_element_type=F32)
            ds = p * (dp - jnp.tile(dl_ref[...], (1, reps)))
            acc[...] += jnp.dot(ds.astype(BF16), kv, preferred_element_type=F32)
            if has_dec:
                dacc[...] += _lane_sum(ds)

        @pl.when(j < i)
        def _():
            step(False)

        @pl.when(j == i)
        def _():
            step(True)
            dq_ref[...] = acc[...].astype(out_dtype)
            if has_dec:
                dd_ref[...] = jnp.broadcast_to(jnp.sum(dacc[...], axis=1, keepdims=True), (T, LANES))

    qb = pl.BlockSpec((T, LANES), lambda h, t, qi, kj: (qi[t], h))
    kb = pl.BlockSpec((T, LANES), lambda h, t, qi, kj: (kj[t], h))
    repq = pl.BlockSpec((None, T, LANES), lambda h, t, qi, kj: (h, qi[t], 0))
    rowk = pl.BlockSpec((None, 1, T), lambda h, t, qi, kj: (h, 0, kj[t]))
    in_specs = [qb, kb, kb, qb, repq, repq] + ([rowk] if has_dec else [])
    args = (q, k, v, do, lse, delta) + ((dec_row,) if has_dec else ())
    out_shape = jax.ShapeDtypeStruct((S, W), out_dtype)
    res = pl.pallas_call(
        body, name=name,
        grid_spec=pltpu.PrefetchScalarGridSpec(num_scalar_prefetch=2, grid=(H, npairs), in_specs=in_specs,
                                               out_specs=(qb, repq) if has_dec else qb,
                                               scratch_shapes=[pltpu.VMEM((T, LANES), F32)] * (2 if has_dec else 1)),
        out_shape=(out_shape, jax.ShapeDtypeStruct((H, S, LANES), F32)) if has_dec else out_shape,
        compiler_params=_cparams(("parallel", "arbitrary")),
    )(qi, kj, *args)
    return res if has_dec else (res, None)


def _fa_bwd_dkv(q, k, v, do, lse_row, delta_row, dec_rep, *, unit, dk_dtype, name):
    S, W = q.shape
    H = W // LANES
    T = min(ATTN_TILE, S)
    n, reps = S // T, T // LANES
    qi, kj, npairs = _pairs(n, by_key=True)
    has_dec = dec_rep is not None

    def body(qi_ref, kj_ref, *refs):
        if has_dec:
            q_ref, k_ref, v_ref, do_ref, lse_ref, dl_ref, dc_ref, dk_ref, dv_ref, dd_ref, kacc, vacc, dacc = refs
        else:
            q_ref, k_ref, v_ref, do_ref, lse_ref, dl_ref, dk_ref, dv_ref, kacc, vacc = refs
        t = pl.program_id(1)
        i, j = qi_ref[t], kj_ref[t]

        @pl.when(i == j)
        def _():
            kacc[...] = jnp.zeros_like(kacc)
            vacc[...] = jnp.zeros_like(vacc)
            if has_dec:
                dacc[...] = jnp.zeros_like(dacc)

        def step(diag):
            qv, dov = q_ref[...], do_ref[...]
            st = lax.dot_general(k_ref[...], qv, _NT, preferred_element_type=F32)
            if has_dec:
                st = st - jnp.tile(dc_ref[...], (1, reps))
            if diag:
                st = jnp.where(_unit_mask((T, T), unit, 0), st, NEG_INF)
            pt = jnp.exp(st - lse_ref[...])
            dpt = lax.dot_general(v_ref[...], dov, _NT, preferred_element_type=F32)
            dst = pt * (dpt - dl_ref[...])
            vacc[...] += jnp.dot(pt.astype(BF16), dov, preferred_element_type=F32)
            kacc[...] += jnp.dot(dst.astype(BF16), qv, preferred_element_type=F32)
            if has_dec:
                dacc[...] += _lane_sum(dst)

        @pl.when(i == j)
        def _():
            step(True)

        @pl.when(i > j)
        def _():
            step(False)

        @pl.when(i == n - 1)
        def _():
            dk_ref[...] = kacc[...].astype(dk_dtype)
            dv_ref[...] = vacc[...].astype(BF16)
            if has_dec:
                dd_ref[...] = jnp.broadcast_to(-jnp.sum(dacc[...], axis=1, keepdims=True), (T, LANES))

    qb = pl.BlockSpec((T, LANES), lambda h, t, qi, kj: (qi[t], h))
    kb = pl.BlockSpec((T, LANES), lambda h, t, qi, kj: (kj[t], h))
    rowq = pl.BlockSpec((None, 1, T), lambda h, t, qi, kj: (h, 0, qi[t]))
    repk = pl.BlockSpec((None, T, LANES), lambda h, t, qi, kj: (h, kj[t], 0))
    in_specs = [qb, kb, kb, qb, rowq, rowq] + ([repk] if has_dec else [])
    args = (q, k, v, do, lse_row, delta_row) + ((dec_rep,) if has_dec else ())
    scratch = [pltpu.VMEM((T, LANES), F32)] * (3 if has_dec else 2)
    out_specs = (kb, kb) + ((repk,) if has_dec else ())
    out_shape = (jax.ShapeDtypeStruct((S, W), dk_dtype), jax.ShapeDtypeStruct((S, W), BF16))
    if has_dec:
        out_shape = out_shape + (jax.ShapeDtypeStruct((H, S, LANES), F32),)
    res = pl.pallas_call(
        body, name=name,
        grid_spec=pltpu.PrefetchScalarGridSpec(num_scalar_prefetch=2, grid=(H, npairs), in_specs=in_specs,
                                               out_specs=out_specs, scratch_shapes=scratch),
        out_shape=out_shape, compiler_params=_cparams(("parallel", "arbitrary")),
    )(qi, kj, *args)
    return res if has_dec else (res[0], res[1], None)


def _fa_bwd(q, k, v, o, lse, lse_row, do, dec_row, dec_rep, *, unit, dq_dtype, dk_dtype, name):
    delta, delta_row = _fa_delta(do, o)
    dq, dd_q = _fa_bwd_dq(q, k, v, do, lse, delta, dec_row, unit=unit, out_dtype=dq_dtype, name=name + "_dq")
    one = lambda t: jnp.max(t, axis=2)
    dk, dv, dd_k = _fa_bwd_dkv(q, k, v, do, lse_row, delta_row, dec_rep, unit=unit, dk_dtype=dk_dtype, name=name + "_dkv")
    return dq, dk, dv, (None if dd_k is None else one(dd_q) + one(dd_k))


def _merge_fwd(ya, yb, yc, gate_logit, gate_b):
    S, D = ya.shape
    tm = min(256, S)

    def body(a_ref, b_ref, c_ref, gl_ref, gb_ref, o_ref):
        g = jax.nn.sigmoid(gl_ref[...] + gb_ref[...])
        o_ref[...] = (g[:, 0:D] * a_ref[...] + g[:, D:2 * D] * b_ref[...] + g[:, 2 * D:3 * D] * c_ref[...]).astype(BF16)

    row = pl.BlockSpec((tm, D), lambda i: (i, 0))
    return pl.pallas_call(
        body, name="merge_fwd", grid=(S // tm,),
        in_specs=[row, row, row, pl.BlockSpec((tm, 3 * D), lambda i: (i, 0)), pl.BlockSpec((1, 3 * D), lambda i: (0, 0))],
        out_specs=row, out_shape=jax.ShapeDtypeStruct((S, D), BF16), compiler_params=_cparams(("parallel",)),
    )(ya, yb, yc, gate_logit, gate_b.reshape(1, 3 * D))


def _merge_bwd(dm, ya, yb, yc, gate_logit, gate_b):
    S, D = ya.shape
    tm = min(256, S)

    def body(dm_ref, a_ref, b_ref, c_ref, gl_ref, gb_ref, da_ref, db_ref, dc_ref, dgl_ref, dgb_ref):
        g = jax.nn.sigmoid(gl_ref[...] + gb_ref[...])
        dmv = dm_ref[...]
        parts = []
        for n, (y_ref, dy_ref) in enumerate(((a_ref, da_ref), (b_ref, db_ref), (c_ref, dc_ref))):
            gn = g[:, n * D:(n + 1) * D]
            dy_ref[...] = (dmv * gn).astype(BF16)
            parts.append(dmv * y_ref[...] * gn * (1.0 - gn))
        dgl = jnp.concatenate(parts, axis=1)
        dgl_ref[...] = dgl.astype(BF16)

        @pl.when(pl.program_id(0) == 0)
        def _():
            dgb_ref[...] = jnp.zeros_like(dgb_ref)

        dgb_ref[...] += jnp.sum(dgl, axis=0, keepdims=True)

    row = pl.BlockSpec((tm, D), lambda i: (i, 0))
    wide = pl.BlockSpec((tm, 3 * D), lambda i: (i, 0))
    vec = pl.BlockSpec((1, 3 * D), lambda i: (0, 0))
    act = jax.ShapeDtypeStruct((S, D), BF16)
    da, db, dc, dgl, dgb = pl.pallas_call(
        body, name="merge_bwd", grid=(S // tm,), in_specs=[row, row, row, row, wide, vec],
        out_specs=(row, row, row, wide, vec),
        out_shape=(act, act, act, jax.ShapeDtypeStruct((S, 3 * D), BF16), jax.ShapeDtypeStruct((1, 3 * D), F32)),
        compiler_params=_cparams(("arbitrary",)),
    )(dm, ya, yb, yc, gate_logit, gate_b.reshape(1, 3 * D))
    return da, db, dc, dgl, dgb.reshape(3 * D)


def _swiglu_fwd(hf):
    S, W2 = hf.shape
    F = W2 // 2
    tm = min(128, S)

    def body(h_ref, o_ref):
        gt, up = h_ref[:, 0:F], h_ref[:, F:W2]
        o_ref[...] = (gt * jax.nn.sigmoid(gt) * up).astype(BF16)

    return pl.pallas_call(
        body, name="swiglu_fwd", grid=(S // tm,), in_specs=[pl.BlockSpec((tm, W2), lambda i: (i, 0))],
        out_specs=pl.BlockSpec((tm, F), lambda i: (i, 0)), out_shape=jax.ShapeDtypeStruct((S, F), BF16),
        compiler_params=_cparams(("parallel",)),
    )(hf)


def _swiglu_bwd(dact, hf):
    S, W2 = hf.shape
    F = W2 // 2
    tm = min(128, S)

    def body(d_ref, h_ref, o_ref):
        gt, up = h_ref[:, 0:F], h_ref[:, F:W2]
        sg = jax.nn.sigmoid(gt)
        dv = d_ref[...]
        o_ref[:, 0:F] = (dv * up * sg * (1.0 + gt * (1.0 - sg))).astype(BF16)
        o_ref[:, F:W2] = (dv * gt * sg).astype(BF16)

    return pl.pallas_call(
        body, name="swiglu_bwd", grid=(S // tm,),
        in_specs=[pl.BlockSpec((tm, F), lambda i: (i, 0)), pl.BlockSpec((tm, W2), lambda i: (i, 0))],
        out_specs=pl.BlockSpec((tm, W2), lambda i: (i, 0)), out_shape=jax.ShapeDtypeStruct((S, W2), BF16),
        compiler_params=_cparams(("parallel",)),
    )(dact, hf)


def _ple_fwd(x, pre, e, g_next):
    S, D = x.shape
    tm = _rows(S)
    with_norm = g_next is not None

    def body(*refs):
        x_ref, p_ref, e_ref = refs[:3]
        xn = x_ref[...] + jax.nn.sigmoid(p_ref[...]) * e_ref[...]
        if with_norm:
            g_ref, o_ref, h_ref = refs[3:]
            rstd = lax.rsqrt(jnp.mean(xn * xn, axis=1, keepdims=True) + EPS)
            h_ref[...] = (xn * rstd * g_ref[...]).astype(BF16)
        else:
            o_ref = refs[3]
        o_ref[...] = xn

    row = pl.BlockSpec((tm, D), lambda i: (i, 0))
    xs = jax.ShapeDtypeStruct((S, D), F32)
    if not with_norm:
        return pl.pallas_call(body, name="ple_fwd_last", grid=(S // tm,), in_specs=[row, row, row], out_specs=row,
                              out_shape=xs, compiler_params=_cparams(("parallel",)))(x, pre, e), None
    return pl.pallas_call(body, name="ple_fwd", grid=(S // tm,), in_specs=[row, row, row, pl.BlockSpec((1, D), lambda i: (0, 0))],
                          out_specs=(row, row), out_shape=(xs, jax.ShapeDtypeStruct((S, D), BF16)),
                          compiler_params=_cparams(("parallel",)))(x, pre, e, g_next.reshape(1, D))


def _ple_bwd(dx, pre, e):
    S, D = dx.shape
    tm = _rows(S)

    def body(dx_ref, p_ref, e_ref, dp_ref, de_ref):
        pg = jax.nn.sigmoid(p_ref[...])
        dxv = dx_ref[...]
        dp_ref[...] = (dxv * e_ref[...] * pg * (1.0 - pg)).astype(BF16)
        de_ref[...] = (dxv * pg).astype(BF16)

    row = pl.BlockSpec((tm, D), lambda i: (i, 0))
    act = jax.ShapeDtypeStruct((S, D), BF16)
    return pl.pallas_call(body, name="ple_bwd", grid=(S // tm,), in_specs=[row, row, row], out_specs=(row, row),
                          out_shape=(act, act), compiler_params=_cparams(("parallel",)))(dx, pre, e)


def _pad_heads(w, real):
    K = w.shape[0]
    w = w.reshape(K, HEADS, real)
    return jnp.pad(w, ((0, 0), (0, 0), (0, HEAD_PAD - real))).reshape(K, HEADS * HEAD_PAD)


def _unpad_heads(w, real):
    K = w.shape[0]
    return w.reshape(K, HEADS, HEAD_PAD)[:, :, :real].reshape(K, HEADS * real)


def _pad_head_rows(w, real):
    N = w.shape[1]
    w = w.reshape(HEADS, real, N)
    return jnp.pad(w, ((0, 0), (0, HEAD_PAD - real), (0, 0))).reshape(HEADS * HEAD_PAD, N)


def _unpad_head_rows(w, real):
    N = w.shape[1]
    return w.reshape(HEADS, HEAD_PAD, N)[:, :real].reshape(HEADS * real, N)


def _block_diag(w):
    w = w.reshape(4, 2, 64, 64)
    z = jnp.zeros((4, 64, 64), w.dtype)
    top = jnp.concatenate([w[:, 0], z], axis=2)
    bot = jnp.concatenate([z, w[:, 1]], axis=2)
    return jnp.concatenate([top, bot], axis=1)


def _block_diag_t(w):
    return jnp.stack([w[:, :64, :64], w[:, 64:, 64:]], axis=1).reshape(8, 64, 64)


_IN_SPLITS = (512, 512, 384, 288, 512, 512, 512, 8, 3072)
_IN_OFF = np.concatenate([[0], np.cumsum(_IN_SPLITS)])
_KR_OFF = 64
_SEG_NAMES = ("u", "ug", "cq", "ckv", "kr", "fq", "fk", "fv", "fl", "gate")


def _in_segments(w_in):
    c = lambda n: w_in[:, int(_IN_OFF[n]):int(_IN_OFF[n + 1])]
    kv = c(3)
    kr = jnp.pad(kv[:, MLA_KV_LORA:], ((0, 0), (_KR_OFF, LANES - _KR_OFF - MLA_ROPE)))
    fl = jnp.pad(c(7), ((0, 0), (0, LANES - HEADS)))
    fq = _pad_heads(c(4), FOX_HEAD_DIM) * jnp.asarray(FOX_SCALE, w_in.dtype)
    return [c(0), c(1), c(2), kv[:, :MLA_KV_LORA], kr, fq, _pad_heads(c(5), FOX_HEAD_DIM), _pad_heads(c(6), FOX_HEAD_DIM), fl, c(8)]


def _in_unsegment(dw_p, widths):
    offs = np.concatenate([[0], np.cumsum(widths)])
    seg = [dw_p[:, int(offs[n]):int(offs[n + 1])] for n in range(len(widths))]
    u, ug, cq, ckv, kr, fq, fk, fv, fl, gate = seg
    return jnp.concatenate([
        u, ug, cq, ckv, kr[:, _KR_OFF:_KR_OFF + MLA_ROPE], _unpad_heads(fq, FOX_HEAD_DIM) * FOX_SCALE,
        _unpad_heads(fk, FOX_HEAD_DIM), _unpad_heads(fv, FOX_HEAD_DIM), fl[:, :HEADS], gate], axis=1)


def _split_wuq(wuq):
    return _pad_heads(wuq, MLA_NOPE + MLA_ROPE)


def _split_wukv(wukv):
    w = wukv.reshape(MLA_KV_LORA, HEADS, MLA_NOPE + MLA_V)
    pad = lambda t: jnp.pad(t, ((0, 0), (0, 0), (0, HEAD_PAD - t.shape[2]))).reshape(MLA_KV_LORA, HEADS * HEAD_PAD)
    return pad(w[:, :, :MLA_NOPE]), pad(w[:, :, MLA_NOPE:])


def _merge_wukv(dk_p, dv_p):
    k = dk_p.reshape(MLA_KV_LORA, HEADS, HEAD_PAD)[:, :, :MLA_NOPE]
    v = dv_p.reshape(MLA_KV_LORA, HEADS, HEAD_PAD)[:, :, :MLA_V]
    return jnp.concatenate([k, v], axis=2).reshape(MLA_KV_LORA, HEADS * (MLA_NOPE + MLA_V))


def _heads_layout(d):
    S = d.shape[0]
    t = d[:, :HEADS].T
    return t.reshape(HEADS, 1, S), jnp.broadcast_to(t[:, :, None], (HEADS, S, LANES))


def _layer_fwd(x, h, p_i, w, g_next, tabs):
    c_q, c_k, s_lo, s_hi = tabs
    sv = {"x0": x}
    segs = _in_segments(w["w_in"])
    z = {}
    for nm, ws in zip(_SEG_NAMES, segs):
        z[nm] = _mm(h, ws, out_dtype=BF16 if nm in ("fq", "fk", "fv") else F32, bias=_ones_lane_bias() if nm == "fv" else None,
                    name="in_" + nm)
    sv.update(h=h, z=z)
    wa_bd, wx_bd = _block_diag(w["lru_wa"]).astype(BF16), _block_diag(w["lru_wx"]).astype(BF16)
    oa, xc, hs = _lru_fwd(z["u"], z["ug"], w["conv_w"], w["conv_b"], wa_bd, wx_bd, w["lru_ba"], w["lru_bx"], w["lru_lambda"])
    sv.update(oa=oa, xc=xc, hs=hs)
    qn = _rmsnorm_fwd(z["cq"], w["mla_q_norm"], "q_norm_fwd")
    kvn = _rmsnorm_fwd(z["ckv"], w["mla_kv_norm"], "kv_norm_fwd")
    wuq_p = _split_wuq(w["mla_wuq"])
    wk_p, wv_p = _split_wukv(w["mla_wukv"])
    qb = _rope_q(_mm(qn, wuq_p, name="mla_q"), c_q, s_lo, s_hi, transpose=False, out_dtype=BF16, name="rope_q")
    kb = _rope_k(_mm(kvn, wk_p, name="mla_k"), z["kr"], c_k, s_lo, s_hi)
    vb = _mm(kvn, wv_p, out_dtype=BF16, bias=_ones_lane_bias(), name="mla_v")
    ob, lse_b, lrow_b = _fa_fwd(qb, kb, vb, None, unit=64, name="mla_attn")
    sv.update(qn=qn, kvn=kvn, qb=qb, kb=kb, vb=vb, ob=ob, lse_b=lse_b, lrow_b=lrow_b)
    bf = jnp.pad(w["fox_bf"], (0, LANES - HEADS)).reshape(1, LANES)
    dec = _decay_fwd(z["fl"], bf)
    drow, drep = _heads_layout(dec)
    oc, lse_c, lrow_c = _fa_fwd(z["fq"], z["fk"], z["fv"], drow, unit=1, name="fox_attn")
    sv.update(drow=drow, drep=drep, oc=oc, lse_c=lse_c, lrow_c=lrow_c)
    ya = _mm(oa, w["w_br_a"], name="br_a")
    yb = _mm(ob, _pad_head_rows(w["w_br_b"], MLA_V), name="br_b")
    yc = _mm(oc, _pad_head_rows(w["w_br_c"], FOX_HEAD_DIM), name="br_c")
    merged = _merge_fwd(ya, yb, yc, z["gate"], w["gate_b"])
    x1, hn = _mm_res_norm(merged, w["w_o"], x, w["ffn_norm"], "w_o")
    sv.update(ya=ya, yb=yb, yc=yc, merged=merged, x1=x1)
    hf, act = _ffn_up(hn, _ffn_pair_columns(w["w_gate_up"]))
    x2, pn = _mm_res_norm(act, w["w_down"], x1, w["ple_norm"], "ffn_down")
    sv.update(hn=hn, hf=hf, act=act, x2=x2)
    pre = _mm(pn, w["w_ple_gate"], name="ple_gate")
    e = _mm(p_i, w["w_ple"], name="ple_embed")
    x3, h_next = _ple_fwd(x2, pre, e, g_next)
    sv.update(pn=pn, pre=pre, e=e, p_i=p_i)
    return x3, h_next, sv


def _layer_bwd(dx3, w, sv, tabs):
    c_q, c_k, s_lo, s_hi = tabs
    g = {}
    z = sv["z"]
    dpre, de = _ple_bwd(dx3, sv["pre"], sv["e"])
    g["w_ple"] = _mm(sv["p_i"], de, ta=True, name="d_w_ple")
    g["w_ple_gate"] = _mm(sv["pn"], dpre, ta=True, name="d_w_ple_gate")
    dpn = _mm(dpre, w["w_ple_gate"], tb=True, name="d_pn")
    dx2, g["ple_norm"] = _rmsnorm_bwd(sv["x2"], w["ple_norm"], dpn, add=dx3, name="ple_norm_bwd")
    g["w_down"] = _mm(sv["act"], dx2, ta=True, name="d_w_down")
    dhf = _ffn_down_bwd(dx2, w["w_down"], sv["hf"])
    g["w_gate_up"] = _ffn_unpair_columns(_mm(sv["hn"], dhf, ta=True, name="d_w_gate_up"))
    dhn = _mm(dhf, _ffn_pair_columns(w["w_gate_up"]), tb=True, name="d_hn")
    dx1, g["ffn_norm"] = _rmsnorm_bwd(sv["x1"], w["ffn_norm"], dhn, add=dx2, name="ffn_norm_bwd")
    g["w_o"] = _mm(sv["merged"], dx1, ta=True, name="d_w_o")
    dm = _mm(dx1, w["w_o"], tb=True, name="d_merged")
    dya, dyb, dyc, dgate, g["gate_b"] = _merge_bwd(dm, sv["ya"], sv["yb"], sv["yc"], z["gate"], w["gate_b"])
    wbb_p, wbc_p = _pad_head_rows(w["w_br_b"], MLA_V), _pad_head_rows(w["w_br_c"], FOX_HEAD_DIM)
    g["w_br_a"] = _mm(sv["oa"], dya, ta=True, name="d_w_br_a")
    g["w_br_b"] = _unpad_head_rows(_mm(sv["ob"], dyb, ta=True, name="d_w_br_b"), MLA_V)
    g["w_br_c"] = _unpad_head_rows(_mm(sv["oc"], dyc, ta=True, name="d_w_br_c"), FOX_HEAD_DIM)
    doa = _mm(dya, w["w_br_a"], tb=True, name="d_oa")
    dob = _mm(dyb, wbb_p, tb=True, out_dtype=BF16, name="d_ob")
    doc = _mm(dyc, wbc_p, tb=True, out_dtype=BF16, name="d_oc")
    dfq, dfk, dfv, d_dec = _fa_bwd(z["fq"], z["fk"], z["fv"], sv["oc"], sv["lse_c"], sv["lrow_c"], doc, sv["drow"], sv["drep"],
                                   unit=1, dq_dtype=BF16, dk_dtype=BF16, name="fox_attn_bwd")
    d_dec = jnp.pad(d_dec.T, ((0, 0), (0, LANES - HEADS)))
    bf = jnp.pad(w["fox_bf"], (0, LANES - HEADS)).reshape(1, LANES)
    dfl, dbf = _decay_bwd(d_dec, z["fl"], bf)
    g["fox_bf"] = dbf[0, :HEADS]
    dqb, dkb, dvb, _ = _fa_bwd(sv["qb"], sv["kb"], sv["vb"], sv["ob"], sv["lse_b"], sv["lrow_b"], dob, None, None,
                               unit=64, dq_dtype=F32, dk_dtype=F32, name="mla_attn_bwd")
    wuq_p = _split_wuq(w["mla_wuq"])
    wk_p, wv_p = _split_wukv(w["mla_wukv"])
    dq_pre = _rope_q(dqb, c_q, s_lo, s_hi, transpose=True, out_dtype=BF16, name="rope_q_bwd")
    dkr = _rope_k_bwd(dkb, c_k, s_lo, s_hi)
    g["mla_wuq"] = _unpad_heads(_mm(sv["qn"], dq_pre, ta=True, name="d_wuq"), MLA_NOPE + MLA_ROPE)
    g["mla_wukv"] = _merge_wukv(_mm(sv["kvn"], dkb, ta=True, name="d_wuk"), _mm(sv["kvn"], dvb, ta=True, name="d_wuv"))
    dqn = _mm(dq_pre, wuq_p, tb=True, name="d_qn")
    dkvn = _mm(dvb, wv_p, tb=True, res=_mm(dkb, wk_p, tb=True, name="d_kvn_k"), name="d_kvn")
    dcq, g["mla_q_norm"] = _rmsnorm_bwd(z["cq"], w["mla_q_norm"], dqn, out_dtype=BF16, name="q_norm_bwd")
    dckv, g["mla_kv_norm"] = _rmsnorm_bwd(z["ckv"], w["mla_kv_norm"], dkvn, out_dtype=BF16, name="kv_norm_bwd")
    wa_bd, wx_bd = _block_diag(w["lru_wa"]).astype(BF16), _block_diag(w["lru_wx"]).astype(BF16)
    du, dug, dcw, dcb, dba, dbx, dlam, dwa, dwx = _lru_bwd(
        doa, z["u"], z["ug"], sv["xc"], sv["hs"], w["conv_w"], wa_bd, wx_bd, w["lru_ba"], w["lru_bx"], w["lru_lambda"])
    g["conv_w"], g["conv_b"], g["lru_ba"], g["lru_bx"] = dcw, dcb[0], dba[0], dbx[0]
    g["lru_lambda"] = dlam[0] * LRU_C * jax.nn.sigmoid(-w["lru_lambda"])
    g["lru_wa"], g["lru_wx"] = _block_diag_t(dwa), _block_diag_t(dwx)
    dsegs = [du, dug, dcq, dckv, dkr, dfq, dfk, dfv, dfl, dgate]
    dz = jnp.concatenate(dsegs, axis=1)
    w_in_p = jnp.concatenate(_in_segments(w["w_in"]), axis=1)
    g["w_in"] = _in_unsegment(_mm(sv["h"], dz, ta=True, name="d_w_in"), [d.shape[1] for d in dsegs])
    dh = _mm(dz, w_in_p, tb=True, name="d_h")
    dx0, g["mix_norm"] = _rmsnorm_bwd(sv["x0"], w["mix_norm"], dh, add=dx1, name="mix_norm_bwd")
    return dx0, g


_LAYER_WEIGHTS = ("mix_norm", "w_in", "gate_b", "conv_w", "conv_b", "lru_wa", "lru_ba", "lru_wx", "lru_bx", "lru_lambda",
                  "mla_q_norm", "mla_wuq", "mla_kv_norm", "mla_wukv", "fox_bf", "w_br_a", "w_br_b", "w_br_c", "w_o",
                  "ffn_norm", "w_gate_up", "w_down", "ple_norm", "w_ple_gate", "w_ple")
_BIG = ("w_in", "mla_wuq", "mla_wukv", "w_br_a", "w_br_b", "w_br_c", "w_o", "w_gate_up", "w_down", "w_ple_gate", "w_ple")
_ROW_SHARDED = ("w_o", "w_down", "w_ple_gate")
_SMALL = ("mix_norm", "gate_b", "conv_b", "lru_wa", "lru_ba", "lru_wx", "lru_bx", "lru_lambda", "mla_q_norm", "mla_kv_norm",
          "fox_bf", "ffn_norm", "ple_norm")


def _local_step(x, p, layers, final_norm, target):
    tabs = _rope_tables(x.shape[0])
    saved = []
    h = _rmsnorm_fwd(x, layers[0]["mix_norm"], "mix_norm_fwd")
    for i in range(DEPTH):
        g_next = layers[i + 1]["mix_norm"] if i + 1 < DEPTH else None
        x, h, sv = _layer_fwd(x, h, p[i], layers[i], g_next, tabs)
        saved.append(sv)
    loss, dx, d_final = _loss_head(x, final_norm, target)
    grads = [None] * DEPTH
    for i in reversed(range(DEPTH)):
        dx, grads[i] = _layer_bwd(dx, layers[i], saved[i], tabs)
    return loss, dx, grads, d_final


def _hbm():
    return pl.BlockSpec(memory_space=pltpu.HBM)


def _peers(x, y):
    return [(1 - x, y), (x, 1 - y), (1 - x, 1 - y)]


def _gather_chips_two_level(shard, name):
    R, W = shard.shape
    Rh = R // 2

    def body(src_ref, out_ref, send_sems, recv_sems):
        x, y, c = lax.axis_index("x"), lax.axis_index("y"), lax.axis_index("c")
        me = 2 * x + y
        mine, other = pl.ds(c * Rh, Rh), pl.ds((1 - c) * Rh, Rh)
        peers = _peers(x, y)

        def copy(j, src, slot, rows, to):
            return pltpu.make_async_remote_copy(src_ref=src, dst_ref=out_ref.at[slot, rows], send_sem=send_sems.at[j],
                                                recv_sem=recv_sems.at[j], device_id=to, device_id_type=MESH)

        first = [copy(j, src_ref.at[mine], me, mine, (px, py, c)) for j, (px, py) in enumerate(peers)]
        for cp in first:
            cp.start()
        passed = []
        for j, (px, py) in enumerate(peers):
            slot = 2 * px + py
            copy(j, src_ref.at[mine], slot, mine, (px, py, c)).wait_recv()
            cp = copy(3 + j, out_ref.at[slot, mine], slot, mine, (x, y, 1 - c))
            cp.start()
            passed.append(cp)
        for j, (px, py) in enumerate(peers):
            copy(3 + j, src_ref.at[other], 2 * px + py, other, (x, y, 1 - c)).wait_recv()
        for cp in first + passed:
            cp.wait_send()

    return pl.pallas_call(
        body, name=name, in_specs=[_hbm()], out_specs=_hbm(), out_shape=jax.ShapeDtypeStruct((4, R, W), shard.dtype),
        scratch_shapes=[pltpu.SemaphoreType.DMA((6,)), pltpu.SemaphoreType.DMA((6,))],
    )(shard)


def _gather_chips(shard, name):
    R, W = shard.shape

    def body(src_ref, out_ref, send_sems, recv_sems, local_sem):
        x, y, c = lax.axis_index("x"), lax.axis_index("y"), lax.axis_index("c")
        me = 2 * x + y
        mine = pltpu.make_async_copy(src_ref, out_ref.at[me], local_sem)
        mine.start()

        def copy(j, slot, to):
            return pltpu.make_async_remote_copy(src_ref=src_ref, dst_ref=out_ref.at[slot], send_sem=send_sems.at[j],
                                                recv_sem=recv_sems.at[j], device_id=(to[0], to[1], c), device_id_type=MESH)

        sends = [copy(j, me, peer) for j, peer in enumerate(_peers(x, y))]
        for cp in sends:
            cp.start()
        for j, peer in enumerate(_peers(x, y)):
            copy(j, 2 * peer[0] + peer[1], peer).wait_recv()
        for cp in sends:
            cp.wait_send()
        mine.wait()

    return pl.pallas_call(
        body, name=name, in_specs=[_hbm()], out_specs=_hbm(), out_shape=jax.ShapeDtypeStruct((4, R, W), shard.dtype),
        scratch_shapes=[pltpu.SemaphoreType.DMA((3,)), pltpu.SemaphoreType.DMA((3,)), pltpu.SemaphoreType.DMA],
    )(shard)


def _pair_swap_halves(g4):
    n, R, W = g4.shape
    Rh = R // 2

    def body(src_ref, out_ref, send_sem, recv_sem):
        x, y, c = lax.axis_index("x"), lax.axis_index("y"), lax.axis_index("c")
        cp = pltpu.make_async_remote_copy(src_ref=src_ref.at[:, pl.ds((1 - c) * Rh, Rh), :], dst_ref=out_ref, send_sem=send_sem,
                                          recv_sem=recv_sem, device_id=(x, y, 1 - c), device_id_type=MESH)
        cp.start()
        cp.wait()

    return pl.pallas_call(
        body, name="grad_pair_swap", in_specs=[_hbm()], out_specs=_hbm(), out_shape=jax.ShapeDtypeStruct((n, Rh, W), g4.dtype),
        scratch_shapes=[pltpu.SemaphoreType.DMA, pltpu.SemaphoreType.DMA],
    )(g4)


def _pair_add(g4, sib, c_arr):
    n, R, W = g4.shape
    Rh = R // 2
    tr = _tile_rows(Rh)
    nb = Rh // tr

    def body(c_ref, a_ref, b_ref, o_ref):
        o_ref[...] = (a_ref[...].astype(F32) + b_ref[...].astype(F32)).astype(o_ref.dtype)

    return pl.pallas_call(
        body, name="grad_pair_add",
        grid_spec=pltpu.PrefetchScalarGridSpec(
            num_scalar_prefetch=1, grid=(n, nb),
            in_specs=[pl.BlockSpec((None, tr, W), lambda s, i, c: (s, c[0] * nb + i, 0)), pl.BlockSpec((None, tr, W), lambda s, i, c: (s, i, 0))],
            out_specs=pl.BlockSpec((None, tr, W), lambda s, i, c: (s, i, 0))),
        out_shape=jax.ShapeDtypeStruct((n, Rh, W), g4.dtype), compiler_params=_cparams(("parallel", "parallel")),
    )(c_arr, g4, sib)


def _tile_rows(n):
    for t in (512, 480, 400, 320, 256, 240, 160, 128, 80, 64, 40, 32, 16, 8):
        if n % t == 0:
            return t
    return n


def _chips_exchange(part):
    n, Rh, W = part.shape

    def body(src_ref, out_ref, send_sems, recv_sems):
        x, y, c = lax.axis_index("x"), lax.axis_index("y"), lax.axis_index("c")

        def copy(j, to):
            return pltpu.make_async_remote_copy(src_ref=src_ref.at[2 * to[0] + to[1]], dst_ref=out_ref.at[j], send_sem=send_sems.at[j],
                                                recv_sem=recv_sems.at[j], device_id=(to[0], to[1], c), device_id_type=MESH)

        cps = [copy(j, peer) for j, peer in enumerate(_peers(x, y))]
        for cp in cps:
            cp.start()
        for cp in cps:
            cp.wait()

    return pl.pallas_call(
        body, name="grad_chips_exchange", in_specs=[_hbm()], out_specs=_hbm(), out_shape=jax.ShapeDtypeStruct((3, Rh, W), part.dtype),
        scratch_shapes=[pltpu.SemaphoreType.DMA((3,)), pltpu.SemaphoreType.DMA((3,))],
    )(part)


def _chips_add(part, got, k_arr, c_arr):
    n, Rh, W = part.shape
    tr = _tile_rows(Rh)
    nb = Rh // tr

    def body(k_ref, c_ref, a_ref, b_ref, o_ref):
        mine = pl.program_id(0) == c_ref[0]

        @pl.when(mine)
        def _():
            o_ref[...] = ((a_ref[...].astype(F32) + b_ref[0].astype(F32)) + b_ref[1].astype(F32)) + b_ref[2].astype(F32)

        @pl.when(jnp.logical_not(mine))
        def _():
            o_ref[...] = jnp.zeros_like(o_ref)

    return pl.pallas_call(
        body, name="grad_chips_add",
        grid_spec=pltpu.PrefetchScalarGridSpec(
            num_scalar_prefetch=2, grid=(2, nb),
            in_specs=[pl.BlockSpec((None, tr, W), lambda h, i, k, c: (k[0], i, 0)), pl.BlockSpec((3, tr, W), lambda h, i, k, c: (0, i, 0))],
            out_specs=pl.BlockSpec((tr, W), lambda h, i, k, c: (h * nb + i, 0))),
        out_shape=jax.ShapeDtypeStruct((2 * Rh, W), F32), compiler_params=_cparams(("parallel", "parallel")),
    )(k_arr, c_arr, part, got)


def _pair_gather(buf):
    R, W = buf.shape
    Rh = R // 2

    def body(src_ref, out_ref, send_sem, recv_sem):
        x, y, c = lax.axis_index("x"), lax.axis_index("y"), lax.axis_index("c")
        mine, other = pl.ds(c * Rh, Rh), pl.ds((1 - c) * Rh, Rh)
        pltpu.make_async_remote_copy(src_ref=src_ref.at[mine], dst_ref=out_ref.at[mine], send_sem=send_sem, recv_sem=recv_sem,
                                     device_id=(x, y, 1 - c), device_id_type=MESH).start()
        pltpu.make_async_remote_copy(src_ref=src_ref.at[mine], dst_ref=out_ref.at[other], send_sem=send_sem, recv_sem=recv_sem,
                                     device_id=(x, y, 1 - c), device_id_type=MESH).wait()

    return pl.pallas_call(
        body, name="grad_pair_gather", in_specs=[_hbm()], out_specs=_hbm(), out_shape=jax.ShapeDtypeStruct((R, W), buf.dtype),
        input_output_aliases={0: 0}, scratch_shapes=[pltpu.SemaphoreType.DMA, pltpu.SemaphoreType.DMA],
    )(buf)


def _gather_all(buf):
    R, W = buf.shape

    def body(src_ref, out_ref, send_sems, recv_sems, local_sem):
        x, y, c = lax.axis_index("x"), lax.axis_index("y"), lax.axis_index("c")
        me = 4 * x + 2 * y + c
        mine = pltpu.make_async_copy(src_ref, out_ref.at[me], local_sem)
        mine.start()
        rel = [((x + (r >> 2 & 1)) % 2, (y + (r >> 1 & 1)) % 2, (c + (r & 1)) % 2) for r in range(1, 8)]

        def copy(j, slot, to):
            return pltpu.make_async_remote_copy(src_ref=src_ref, dst_ref=out_ref.at[slot], send_sem=send_sems.at[j],
                                                recv_sem=recv_sems.at[j], device_id=to, device_id_type=MESH)

        sends = [copy(j, me, to) for j, to in enumerate(rel)]
        for cp in sends:
            cp.start()
        for j, to in enumerate(rel):
            copy(j, 4 * to[0] + 2 * to[1] + to[2], to).wait_recv()
        for cp in sends:
            cp.wait_send()
        mine.wait()

    return pl.pallas_call(
        body, name="small_gather", in_specs=[_hbm()], out_specs=_hbm(), out_shape=jax.ShapeDtypeStruct((8, R, W), buf.dtype),
        scratch_shapes=[pltpu.SemaphoreType.DMA((7,)), pltpu.SemaphoreType.DMA((7,)), pltpu.SemaphoreType.DMA],
    )(buf)


def _sum_slots(stack):
    n, R, W = stack.shape
    tr = _tile_rows(R)

    def body(s_ref, o_ref):
        tot = s_ref[0]
        for j in range(1, n):
            tot = tot + s_ref[j]
        o_ref[...] = tot

    return pl.pallas_call(
        body, name="small_sum", grid=(R // tr,), in_specs=[pl.BlockSpec((n, tr, W), lambda i: (0, i, 0))],
        out_specs=pl.BlockSpec((tr, W), lambda i: (i, 0)), out_shape=jax.ShapeDtypeStruct((R, W), F32),
        compiler_params=_cparams(("parallel",)),
    )(stack)


def _adamw(wp, gp, mp, vp, name):
    R, W = wp.shape
    tr = R
    for t in (1024, 512, 256, 128, 64, 32, 16, 8):
        if R % t == 0 and t * W <= 512 * 1024:
            tr = t
            break
    c1 = 1.0 - ADAM_B1 ** ADAM_STEP
    c2 = 1.0 - ADAM_B2 ** ADAM_STEP

    def body(w_ref, g_ref, m_ref, v_ref, d_ref, mo_ref, vo_ref):
        gv = g_ref[...]
        m = ADAM_B1 * m_ref[...] + (1.0 - ADAM_B1) * gv
        v = ADAM_B2 * v_ref[...] + (1.0 - ADAM_B2) * (gv * gv)
        m_hat = m / c1
        v_hat = v / c2
        d_ref[...] = -ADAM_LR * (m_hat / (jnp.sqrt(v_hat) + ADAM_EPS) + ADAM_WD * w_ref[...])
        mo_ref[...] = m
        vo_ref[...] = v

    blk = pl.BlockSpec((tr, W), lambda i: (i, 0))
    shp = jax.ShapeDtypeStruct((R, W), F32)
    return pl.pallas_call(body, name=name, grid=(R // tr,), in_specs=[blk] * 4, out_specs=(blk,) * 3, out_shape=(shp,) * 3,
                          compiler_params=_cparams(("parallel",)))(wp, gp, mp, vp)


def _pack(arrs, rows):
    flat = jnp.concatenate([a.reshape(-1) for a in arrs])
    return jnp.pad(flat, (0, rows * PACK_W - flat.shape[0])).reshape(rows, PACK_W)


def _unpack(buf, shapes):
    flat = buf.reshape(-1)
    out, off = [], 0
    for shp in shapes:
        n = int(np.prod(shp))
        out.append(flat[off:off + n].reshape(shp))
        off += n
    return out


def _rows_for(shapes, mult):
    n = sum(int(np.prod(s)) for s in shapes)
    rows = -(-n // PACK_W)
    return -(-rows // mult) * mult


def _shard_major(g, name):
    L, K, N = g.shape
    if name in _ROW_SHARDED:
        t = g.reshape(L, 4, K // 4, N).transpose(1, 0, 2, 3)
    else:
        t = g.reshape(L, K, 4, N // 4).transpose(2, 0, 1, 3)
    return t.reshape(4, -1, PACK_W)


def _join_shards(blocks, name):
    return jnp.concatenate(blocks, axis=1 if name in _ROW_SHARDED else 2)


def kernel(x, p, mix_norm, w_in, gate_b, conv_w, conv_b, lru_wa, lru_ba, lru_wx, lru_bx, lru_lambda, mla_q_norm, mla_wuq, mla_kv_norm, mla_wukv, fox_bf, w_br_a, w_br_b, w_br_c, w_o, ffn_norm, w_gate_up, w_down, ple_norm, w_ple_gate, w_ple, final_norm, loss_target, m_mix_norm, m_w_in, m_gate_b, m_conv_w, m_conv_b, m_lru_wa, m_lru_ba, m_lru_wx, m_lru_bx, m_lru_lambda, m_mla_q_norm, m_mla_wuq, m_mla_kv_norm, m_mla_wukv, m_fox_bf, m_w_br_a, m_w_br_b, m_w_br_c, m_w_o, m_ffn_norm, m_w_gate_up, m_w_down, m_ple_norm, m_w_ple_gate, m_w_ple, m_final_norm, v_mix_norm, v_w_in, v_gate_b, v_conv_w, v_conv_b, v_lru_wa, v_lru_ba, v_lru_wx, v_lru_bx, v_lru_lambda, v_mla_q_norm, v_mla_wuq, v_mla_kv_norm, v_mla_wukv, v_fox_bf, v_w_br_a, v_w_br_b, v_w_br_c, v_w_o, v_ffn_norm, v_w_gate_up, v_w_down, v_ple_norm, v_w_ple_gate, v_w_ple, v_final_norm):
    a = dict(locals())
    names = list(_LAYER_WEIGHTS) + ["final_norm"]
    W = {n: a[n] for n in names}
    M = {n: a["m_" + n] for n in names}
    V = {n: a["v_" + n] for n in names}
    ix, iy, ic = lax.axis_index("x"), lax.axis_index("y"), lax.axis_index("c")

    sharded = list(_BIG) + ["conv_w"]
    shard_shapes = [W[n].shape for n in sharded]
    R = _rows_for(shard_shapes, 64)
    w_bf = _pack([W[n].astype(BF16) for n in sharded], R)
    gathered = _gather_chips_two_level(w_bf, "weight_gather")
    gathered = lax.dynamic_update_slice(gathered, w_bf[None], (2 * ix + iy, 0, 0))
    per_chip = [_unpack(gathered[k], shard_shapes) for k in range(4)]
    full = {n: _join_shards([per_chip[k][j] for k in range(4)], n) for j, n in enumerate(_BIG)}
    conv_blocks = _gather_chips(conv_w.reshape(DEPTH * CONV_WIDTH, LANES), "conv_w_gather")
    conv_w_full = jnp.concatenate([conv_blocks[k].reshape(DEPTH, CONV_WIDTH, LANES) for k in range(4)], axis=-1)
    layers = []
    for i in range(DEPTH):
        lw = {n: W[n][i] for n in _SMALL}
        for n in _BIG:
            lw[n] = full[n][i]
        lw["conv_w"] = conv_w_full[i]
        layers.append(lw)

    loss_sum, dx, grads, d_final = _local_step(x[0], p[:, 0], layers, final_norm, loss_target[0])
    loss = lax.psum(loss_sum, ("x", "y", "c"))

    parts = [_shard_major(jnp.stack([grads[i][n] for i in range(DEPTH)]), n).astype(BF16) for n in sharded]
    g4, off = jnp.zeros((4, R, PACK_W), BF16), 0
    for t in parts:
        g4 = lax.dynamic_update_slice(g4, t, (0, off, 0))
        off += t.shape[1]
    c_arr = jnp.reshape(ic, (1,)).astype(jnp.int32)
    k_arr = jnp.reshape(2 * ix + iy, (1,)).astype(jnp.int32)
    pair = _pair_add(g4, _pair_swap_halves(g4), c_arr)
    g_pack = _pair_gather(_chips_add(pair, _chips_exchange(pair), k_arr, c_arr))
    big_out = {}
    for n, gsh in zip(sharded, _unpack(g_pack, shard_shapes)):
        view = lambda t: t.reshape(-1, t.shape[-1])
        d, nm, nv = _adamw(view(W[n]), view(gsh), view(M[n]), view(V[n]), "adamw_" + n)
        for key, arr in (("g", gsh), ("d", d), ("m", nm), ("v", nv)):
            big_out[(key, n)] = arr.reshape(W[n].shape)

    pick = lambda src, n, i: src[n] if i is None else src[n][i]
    small = [(n, i) for i in range(DEPTH) for n in _SMALL] + [("final_norm", None)]
    small_shapes = [pick(W, n, i).shape for n, i in small]
    Rs = _rows_for(small_shapes, 8)
    sg = _pack([d_final if i is None else grads[i][n] for n, i in small], Rs)
    sg = _sum_slots(_gather_all(sg))
    sw = _pack([pick(W, n, i) for n, i in small], Rs)
    sm = _pack([pick(M, n, i) for n, i in small], Rs)
    sv_ = _pack([pick(V, n, i) for n, i in small], Rs)
    sd, snm, snv = _adamw(sw, sg, sm, sv_, "adamw_replicated")
    small_out = {}
    for key, buf in (("g", sg), ("d", sd), ("m", snm), ("v", snv)):
        for (n, i), arr in zip(small, _unpack(buf, small_shapes)):
            small_out[(key, n, i)] = arr

    def assemble(key, n):
        if n == "final_norm":
            return small_out[(key, n, None)]
        if n in sharded:
            return big_out[(key, n)]
        return jnp.stack([small_out[(key, n, i)] for i in range(DEPTH)])

    outs = [loss, dx[None]]
    for key in ("g", "d", "m", "v"):
        outs += [assemble(key, n) for n in names]
    return tuple(outs)
```

```python
import functools
import math

import numpy as np
import jax
import jax.numpy as jnp
from jax import lax
from jax.experimental import pallas as pl
from jax.experimental.pallas import tpu as pltpu

F32, BF16 = jnp.float32, jnp.bfloat16
MESH = pl.DeviceIdType.MESH

D_MODEL = 1024
DEPTH = 2
EPS = 1e-6
NEG_INF = -1e30
LRU_WIDTH = 512
LRU_HEADS = 8
LRU_C = 8.0
CONV_WIDTH = 4
HEADS = 8
MLA_Q_LORA = 384
MLA_KV_LORA = 256
MLA_NOPE = 64
MLA_ROPE = 32
MLA_V = 64
ROPE_BASE = 10000.0
FOX_HEAD_DIM = 64
D_FF = 2816
PLE_DIM = 256
HEAD_PAD = 128
MLA_SCALE = (MLA_NOPE + MLA_ROPE) ** -0.5
FOX_SCALE = FOX_HEAD_DIM ** -0.5

ADAM_LR, ADAM_B1, ADAM_B2, ADAM_EPS, ADAM_WD, ADAM_STEP = 0.001, 0.9, 0.999, 1e-08, 0.01, 10

VMEM_LIMIT_BYTES = 48 * 1024 * 1024
LANES = 128
PACK_W = 1024

ROW_TILE = 512
ATTN_TILE = 1024
LRU_CHUNK = 512


def _cparams(dims):
    return pltpu.CompilerParams(dimension_semantics=dims, vmem_limit_bytes=VMEM_LIMIT_BYTES)


def _tile(n, cap):
    if n <= cap:
        return n
    t = (cap // LANES) * LANES
    while t >= LANES:
        if n % t == 0:
            return t
        t -= LANES
    raise ValueError(f"no tile for {n} under {cap}")


def _rows(n):
    return min(ROW_TILE, n)


MM_VMEM_BUDGET = 36 * 1024 * 1024


def _mm_tiles(M, N, K, a_bytes, b_bytes, o_bytes, has_res):
    best, best_work = None, 0
    for tm in {_tile(M, c) for c in (1024, 512, 256)}:
        for tn in {_tile(N, c) for c in (1792, 1024, 512)}:
            for tk in {_tile(K, c) for c in (2048, 1408, 1024, 512)}:
                need = 2 * (tm * tk * a_bytes + tk * tn * b_bytes + tm * tn * o_bytes + (tm * tn * 4 if has_res else 0))
                need += tm * tn * 4 if tk < K else 0
                need += tm * tn * 4
                if need <= MM_VMEM_BUDGET and tm * tn * tk > best_work:
                    best, best_work = (tm, tn, tk), tm * tn * tk
    assert best is not None, (M, N, K)
    return best

def _mm(a, b, *, ta=False, tb=False, out_dtype=F32, res=None, bias=None, name):
    K, M = a.shape if ta else a.shape[::-1]
    N, K2 = b.shape if tb else b.shape[::-1]
    assert K == K2, (name, a.shape, b.shape)
    assert res is None or bias is None
    tm, tn, tk = _mm_tiles(M, N, K, a.dtype.itemsize, b.dtype.itemsize, jnp.dtype(out_dtype).itemsize, res is not None)
    nk = K // tk
    a_spec = pl.BlockSpec((tk, tm), lambda i, j, k: (k, i)) if ta else pl.BlockSpec((tm, tk), lambda i, j, k: (i, k))
    b_spec = pl.BlockSpec((tn, tk), lambda i, j, k: (j, k)) if tb else pl.BlockSpec((tk, tn), lambda i, j, k: (k, j))
    o_spec = pl.BlockSpec((tm, tn), lambda i, j, k: (i, j))
    dn = (((0,) if ta else (1,), (1,) if tb else (0,)), ((), ()))
    if bias is not None:
        res, r_spec = bias, pl.BlockSpec((1, tn), lambda i, j, k: (0, j))
    else:
        r_spec = o_spec
    has_res = res is not None

    def body(*refs):
        a_ref, b_ref = refs[0], refs[1]
        r_ref = refs[2] if has_res else None
        o_ref = refs[3] if has_res else refs[2]
        av, bv = a_ref[...], b_ref[...]
        if av.dtype != BF16:
            av = av.astype(BF16)
        if bv.dtype != BF16:
            bv = bv.astype(BF16)
        part = lax.dot_general(av, bv, dn, preferred_element_type=F32)

        def finish(total):
            if has_res:
                total = total + r_ref[...]
            o_ref[...] = total.astype(out_dtype)

        if nk == 1:
            finish(part)
        else:
            acc = refs[-1]
            k = pl.program_id(2)

            @pl.when(k == 0)
            def _():
                acc[...] = part

            @pl.when(k > 0)
            def _():
                acc[...] += part

            @pl.when(k == nk - 1)
            def _():
                finish(acc[...])

    in_specs = [a_spec, b_spec] + ([r_spec] if has_res else [])
    args = (a, b) + ((res,) if has_res else ())
    return pl.pallas_call(
        body, name=name, grid=(M // tm, N // tn, nk), in_specs=in_specs, out_specs=o_spec,
        out_shape=jax.ShapeDtypeStruct((M, N), out_dtype),
        scratch_shapes=[pltpu.VMEM((tm, tn), F32)] if nk > 1 else [],
        compiler_params=_cparams(("parallel", "parallel", "arbitrary")),
    )(*args)


def _mm_res_norm(a, b, res, g, name):
    M, K = a.shape
    N = b.shape[1]
    tm, tk = _tile(M, 512), _tile(K, 1408)
    nk = K // tk

    def body(a_ref, b_ref, r_ref, g_ref, o_ref, h_ref, *scratch):
        part = jnp.dot(a_ref[...], b_ref[...], preferred_element_type=F32)

        def finish(total):
            xn = total + r_ref[...]
            o_ref[...] = xn
            rstd = lax.rsqrt(jnp.mean(xn * xn, axis=1, keepdims=True) + EPS)
            h_ref[...] = (xn * rstd * g_ref[...]).astype(BF16)

        if nk == 1:
            finish(part)
        else:
            acc = scratch[0]
            k = pl.program_id(1)

            @pl.when(k == 0)
            def _():
                acc[...] = part

            @pl.when(k > 0)
            def _():
                acc[...] += part

            @pl.when(k == nk - 1)
            def _():
                finish(acc[...])

    row = pl.BlockSpec((tm, N), lambda i, k: (i, 0))
    return pl.pallas_call(
        body, name=name, grid=(M // tm, nk),
        in_specs=[pl.BlockSpec((tm, tk), lambda i, k: (i, k)), pl.BlockSpec((tk, N), lambda i, k: (k, 0)), row,
                  pl.BlockSpec((1, N), lambda i, k: (0, 0))],
        out_specs=(row, row), out_shape=(jax.ShapeDtypeStruct((M, N), F32), jax.ShapeDtypeStruct((M, N), BF16)),
        scratch_shapes=[pltpu.VMEM((tm, N), F32)] if nk > 1 else [],
        compiler_params=_cparams(("parallel", "arbitrary")),
    )(a, b, res, g.reshape(1, N))


FFN_TILE = 1408


def _ffn_pair_columns(w_gate_up):
    F = w_gate_up.shape[-1] // 2
    parts = []
    for j in range(F // FFN_TILE):
        parts += [w_gate_up[..., j * FFN_TILE:(j + 1) * FFN_TILE], w_gate_up[..., F + j * FFN_TILE:F + (j + 1) * FFN_TILE]]
    return jnp.concatenate(parts, axis=-1)


def _ffn_unpair_columns(dw):
    F = dw.shape[-1] // 2
    n = F // FFN_TILE
    blk = [dw[..., j * FFN_TILE:(j + 1) * FFN_TILE] for j in range(2 * n)]
    return jnp.concatenate(blk[0::2] + blk[1::2], axis=-1)


def _ffn_up(hn, w_pair):
    S, D = hn.shape
    W2 = w_pair.shape[1]
    F, tf = W2 // 2, FFN_TILE
    tm = _rows(S)

    def body(h_ref, w_ref, hf_ref, act_ref):
        hf = jnp.dot(h_ref[...], w_ref[...], preferred_element_type=F32)
        hf_ref[...] = hf
        gt, up = hf[:, 0:tf], hf[:, tf:2 * tf]
        act_ref[...] = (gt * jax.nn.sigmoid(gt) * up).astype(BF16)

    return pl.pallas_call(
        body, name="ffn_up", grid=(S // tm, F // tf),
        in_specs=[pl.BlockSpec((tm, D), lambda i, j: (i, 0)), pl.BlockSpec((D, 2 * tf), lambda i, j: (0, j))],
        out_specs=(pl.BlockSpec((tm, 2 * tf), lambda i, j: (i, j)), pl.BlockSpec((tm, tf), lambda i, j: (i, j))),
        out_shape=(jax.ShapeDtypeStruct((S, W2), F32), jax.ShapeDtypeStruct((S, F), BF16)),
        compiler_params=_cparams(("parallel", "parallel")),
    )(hn, w_pair)


def _ffn_down_bwd(dx, w_down, hf):
    S, D = dx.shape
    F, tf = w_down.shape[0], FFN_TILE
    tm = _rows(S)

    def body(d_ref, w_ref, h_ref, o_ref):
        dact = lax.dot_general(d_ref[...].astype(BF16), w_ref[...], _NT, preferred_element_type=F32)
        gt, up = h_ref[:, 0:tf], h_ref[:, tf:2 * tf]
        sg = jax.nn.sigmoid(gt)
        o_ref[:, 0:tf] = (dact * up * sg * (1.0 + gt * (1.0 - sg))).astype(BF16)
        o_ref[:, tf:2 * tf] = (dact * gt * sg).astype(BF16)

    pair = pl.BlockSpec((tm, 2 * tf), lambda i, j: (i, j))
    return pl.pallas_call(
        body, name="ffn_down_bwd", grid=(S // tm, F // tf),
        in_specs=[pl.BlockSpec((tm, D), lambda i, j: (i, 0)), pl.BlockSpec((tf, D), lambda i, j: (j, 0)), pair],
        out_specs=pair, out_shape=jax.ShapeDtypeStruct((S, 2 * F), BF16),
        compiler_params=_cparams(("parallel", "parallel")),
    )(dx, w_down, hf)


def _rmsnorm_fwd(x, g, name):
    S, W = x.shape
    tm = _rows(S)

    def body(x_ref, g_ref, o_ref):
        xf = x_ref[...]
        rstd = lax.rsqrt(jnp.mean(xf * xf, axis=1, keepdims=True) + EPS)
        o_ref[...] = (xf * rstd * g_ref[...]).astype(BF16)

    return pl.pallas_call(
        body, name=name, grid=(S // tm,),
        in_specs=[pl.BlockSpec((tm, W), lambda i: (i, 0)), pl.BlockSpec((1, W), lambda i: (0, 0))],
        out_specs=pl.BlockSpec((tm, W), lambda i: (i, 0)),
        out_shape=jax.ShapeDtypeStruct((S, W), BF16), compiler_params=_cparams(("parallel",)),
    )(x, g.reshape(1, W))


def _rmsnorm_bwd(x, g, dy, *, add=None, out_dtype=F32, name):
    S, W = x.shape
    tm = _rows(S)
    has_add = add is not None

    def body(*refs):
        x_ref, g_ref, dy_ref = refs[:3]
        add_ref = refs[3] if has_add else None
        dx_ref, dg_ref = refs[-2], refs[-1]
        xf = x_ref[...]
        rstd = lax.rsqrt(jnp.mean(xf * xf, axis=1, keepdims=True) + EPS)
        xhat = xf * rstd
        dyv = dy_ref[...]
        dxh = dyv * g_ref[...]
        dx = rstd * (dxh - xhat * jnp.mean(dxh * xhat, axis=1, keepdims=True))
        if has_add:
            dx = dx + add_ref[...]
        dx_ref[...] = dx.astype(out_dtype)

        @pl.when(pl.program_id(0) == 0)
        def _():
            dg_ref[...] = jnp.zeros_like(dg_ref)

        dg_ref[...] += jnp.sum(dyv * xhat, axis=0, keepdims=True)

    row = pl.BlockSpec((tm, W), lambda i: (i, 0))
    vec = pl.BlockSpec((1, W), lambda i: (0, 0))
    dx, dg = pl.pallas_call(
        body, name=name, grid=(S // tm,),
        in_specs=[row, vec, row] + ([row] if has_add else []),
        out_specs=(row, vec),
        out_shape=(jax.ShapeDtypeStruct((S, W), out_dtype), jax.ShapeDtypeStruct((1, W), F32)),
        compiler_params=_cparams(("arbitrary",)),
    )(x, g.reshape(1, W), dy, *((add,) if has_add else ()))
    return dx, dg.reshape(W)


def _loss_head(x, g, target):
    S, W = x.shape
    tm = _rows(S)

    def body(x_ref, g_ref, t_ref, loss_ref, dx_ref, dg_ref):
        xf = x_ref[...]
        gv = g_ref[...]
        rstd = lax.rsqrt(jnp.mean(xf * xf, axis=1, keepdims=True) + EPS)
        xhat = xf * rstd
        err = xhat * gv - t_ref[...]
        part = 0.5 * jnp.sum(jnp.mean(err * err, axis=1, keepdims=True), axis=0, keepdims=True)
        dyv = err * (1.0 / W)
        dxh = dyv * gv
        dx_ref[...] = rstd * (dxh - xhat * jnp.mean(dxh * xhat, axis=1, keepdims=True))

        @pl.when(pl.program_id(0) == 0)
        def _():
            dg_ref[...] = jnp.zeros_like(dg_ref)
            loss_ref[...] = jnp.zeros_like(loss_ref)

        dg_ref[...] += jnp.sum(dyv * xhat, axis=0, keepdims=True)
        loss_ref[...] += part

    row = pl.BlockSpec((tm, W), lambda i: (i, 0))
    vec = pl.BlockSpec((1, W), lambda i: (0, 0))
    loss, dx, dg = pl.pallas_call(
        body, name="loss_head", grid=(S // tm,), in_specs=[row, vec, row],
        out_specs=(pl.BlockSpec((1, 1), lambda i: (0, 0)), row, vec),
        out_shape=(jax.ShapeDtypeStruct((1, 1), F32), jax.ShapeDtypeStruct((S, W), F32), jax.ShapeDtypeStruct((1, W), F32)),
        compiler_params=_cparams(("arbitrary",)),
    )(x, g.reshape(1, W), target)
    return loss[0, 0], dx, dg.reshape(W)


def _scan_fwd(a, b, row):
    T = a.shape[0]
    d = 1
    while d < T:
        keep = row >= d
        b = jnp.where(keep, a * pltpu.roll(b, d, axis=0) + b, b)
        a = jnp.where(keep, a * pltpu.roll(a, d, axis=0), a)
        d *= 2
    return a, b


def _scan_bwd(a, b, row):
    T = a.shape[0]
    d = 1
    while d < T:
        keep = row < T - d
        b = jnp.where(keep, a * pltpu.roll(b, T - d, axis=0) + b, b)
        a = jnp.where(keep, a * pltpu.roll(a, T - d, axis=0), a)
        d *= 2
    return a, b


def _expm1(x):
    small = x * (1.0 + x * (0.5 + x * (1.0 / 6 + x * (1.0 / 24 + x * (1.0 / 120 + x * (1.0 / 720 + x * (1.0 / 5040)))))))
    return jnp.where(jnp.abs(x) < 0.25, small, jnp.exp(x) - 1.0)


_GELU_C = math.sqrt(2.0 / math.pi)


def _gelu_and_grad(x):
    inner = _GELU_C * (x + 0.044715 * x * x * x)
    th = jnp.tanh(inner)
    val = 0.5 * x * (1.0 + th)
    grad = 0.5 * (1.0 + th) + 0.5 * x * (1.0 - th * th) * _GELU_C * (1.0 + 3 * 0.044715 * x * x)
    return val, grad


def _lru_gates(xc, wa, wx, ba, bx, lam):
    xcb = xc.astype(BF16)
    r = jax.nn.sigmoid(jnp.dot(xcb, wa, preferred_element_type=F32) + ba)
    ig = jax.nn.sigmoid(jnp.dot(xcb, wx, preferred_element_type=F32) + bx)
    sp = jax.nn.softplus(-lam)
    log_a = -LRU_C * r * sp
    a = jnp.exp(log_a)
    mult = jnp.sqrt(-_expm1(2.0 * log_a))
    return xcb, r, ig, sp, a, mult


def _lru_fwd(u, ug, conv_w, conv_b, wa_bd, wx_bd, ba, bx, lam):
    S, W = u.shape
    T = min(LRU_CHUNK, S)
    nl, nc = W // LANES, S // T

    def body(u_ref, ug_ref, cw_ref, cb_ref, wa_ref, wx_ref, ba_ref, bx_ref, lam_ref, ya_ref, xc_ref, h_ref, prev_u, h_carry):
        c = pl.program_id(1)

        @pl.when(c == 0)
        def _():
            prev_u[...] = jnp.zeros_like(prev_u)
            h_carry[...] = jnp.zeros_like(h_carry)

        uv = u_ref[...]
        row = lax.broadcasted_iota(jnp.int32, (T, LANES), 0)
        row8 = lax.broadcasted_iota(jnp.int32, (8, LANES), 0)
        cw = cw_ref[...]
        xc = cb_ref[...] + uv * cw[3:4, :]
        pv = prev_u[...]
        for k in range(1, CONV_WIDTH):
            us = pltpu.roll(uv, k, axis=0)
            top = jnp.where(row8 < k, pltpu.roll(pv, k, axis=0), us[0:8])
            us = jnp.concatenate([top, us[8:]], axis=0)
            xc = xc + us * cw[3 - k:4 - k, :]
        prev_u[...] = uv[T - 8:T]
        _, r, ig, sp, a, mult = _lru_gates(xc, wa_ref[...], wx_ref[...], ba_ref[...], bx_ref[...], lam_ref[...])
        bb = mult * (ig * xc)
        aa, hh = _scan_fwd(a, bb, row)
        h = hh + aa * h_carry[7:8, :]
        h_carry[...] = h[T - 8:T]
        gl, _ = _gelu_and_grad(ug_ref[...])
        ya_ref[...] = (h * gl).astype(BF16)
        xc_ref[...] = xc
        h_ref[...] = h

    seq = pl.BlockSpec((T, LANES), lambda l, c: (c, l))
    vec = pl.BlockSpec((1, LANES), lambda l, c: (0, l))
    mat = pl.BlockSpec((None, LANES, LANES), lambda l, c: (l, 0, 0))
    return pl.pallas_call(
        body, name="lru_fwd", grid=(nl, nc),
        in_specs=[seq, seq, pl.BlockSpec((CONV_WIDTH, LANES), lambda l, c: (0, l)), vec, mat, mat, vec, vec, vec],
        out_specs=(seq, seq, seq),
        out_shape=(jax.ShapeDtypeStruct((S, W), BF16), jax.ShapeDtypeStruct((S, W), F32), jax.ShapeDtypeStruct((S, W), F32)),
        scratch_shapes=[pltpu.VMEM((8, LANES), F32), pltpu.VMEM((8, LANES), F32)],
        compiler_params=_cparams(("parallel", "arbitrary")),
    )(u, ug, conv_w, conv_b.reshape(1, W), wa_bd, wx_bd, ba.reshape(1, W), bx.reshape(1, W), lam.reshape(1, W))


def _lru_bwd(dya, u, ug, xc, h, conv_w, wa_bd, wx_bd, ba, bx, lam):
    S, W = u.shape
    T = min(LRU_CHUNK, S)
    nl, nc = W // LANES, S // T
    tb8 = T // 8

    def body(dya_ref, u_ref, ug_ref, xc_ref, h_ref, hp_ref, cw_ref, wa_ref, wx_ref, ba_ref, bx_ref, lam_ref,
             du_ref, dug_ref, dcw_ref, dcb_ref, dba_ref, dbx_ref, dlam_ref, dwa_ref, dwx_ref,
             g_next, a_next, dxc_next):
        c = pl.program_id(1)

        @pl.when(c == 0)
        def _():
            g_next[...] = jnp.zeros_like(g_next)
            a_next[...] = jnp.zeros_like(a_next)
            dxc_next[...] = jnp.zeros_like(dxc_next)
            for ref in (dcw_ref, dcb_ref, dba_ref, dbx_ref, dlam_ref, dwa_ref, dwx_ref):
                ref[...] = jnp.zeros_like(ref)

        row = lax.broadcasted_iota(jnp.int32, (T, LANES), 0)
        row8 = lax.broadcasted_iota(jnp.int32, (8, LANES), 0)
        xcv = xc_ref[...]
        wa, wx = wa_ref[...], wx_ref[...]
        xcb, r, ig, sp, a, mult = _lru_gates(xcv, wa, wx, ba_ref[...], bx_ref[...], lam_ref[...])
        gl, dgl = _gelu_and_grad(ug_ref[...])
        dyav = dya_ref[...]
        hv = h_ref[...]
        dug_ref[...] = (dyav * hv * dgl).astype(BF16)
        dh = dyav * gl
        a_up = pltpu.roll(a, T - 1, axis=0)
        a_up = jnp.where(row == T - 1, a_next[0:1, :], a_up)
        prod, gg = _scan_bwd(a_up, dh, row)
        g = gg + prod * g_next[0:1, :]
        h_prev = pltpu.roll(hv, 1, axis=0)
        first_chunk = c == nc - 1
        h_before = jnp.where(first_chunk, 0.0, hp_ref[7:8, :])
        h_prev = jnp.where(row == 0, h_before, h_prev)
        da = g * h_prev
        d_mult = g * (ig * xcv)
        d_ig = g * mult * xcv
        dxc = g * mult * ig
        d_log_a = da * a - d_mult * (a * a) / mult
        d_r = d_log_a * (-LRU_C * sp)
        d_pa = d_r * r * (1.0 - r)
        d_px = d_ig * ig * (1.0 - ig)
        d_pab, d_pxb = d_pa.astype(BF16), d_px.astype(BF16)
        nt = (((1,), (1,)), ((), ()))
        tn = (((0,), (0,)), ((), ()))
        dxc = dxc + lax.dot_general(d_pab, wa, nt, preferred_element_type=F32) + lax.dot_general(d_pxb, wx, nt, preferred_element_type=F32)
        dwa_ref[...] += lax.dot_general(xcb, d_pab, tn, preferred_element_type=F32)
        dwx_ref[...] += lax.dot_general(xcb, d_pxb, tn, preferred_element_type=F32)
        dlam_ref[...] += jnp.sum(d_log_a * r, axis=0, keepdims=True)
        dba_ref[...] += jnp.sum(d_pa, axis=0, keepdims=True)
        dbx_ref[...] += jnp.sum(d_px, axis=0, keepdims=True)
        dcb_ref[...] += jnp.sum(dxc, axis=0, keepdims=True)
        uv = u_ref[...]
        cw = cw_ref[...]
        nxt = dxc_next[...]
        du = dxc * cw[3:4, :]
        dcw_ref[3:4, :] += jnp.sum(uv * dxc, axis=0, keepdims=True)
        for k in range(1, CONV_WIDTH):
            ds = pltpu.roll(dxc, T - k, axis=0)
            bot = jnp.where(row8 >= 8 - k, pltpu.roll(nxt, 8 - k, axis=0), ds[T - 8:T])
            ds = jnp.concatenate([ds[:T - 8], bot], axis=0)
            du = du + ds * cw[3 - k:4 - k, :]
            dcw_ref[3 - k:4 - k, :] += jnp.sum(uv * ds, axis=0, keepdims=True)
        du_ref[...] = du.astype(BF16)
        g_next[...] = g[0:8]
        a_next[...] = a[0:8]
        dxc_next[...] = dxc[0:8]

    seq = pl.BlockSpec((T, LANES), lambda l, c: (nc - 1 - c, l))
    before = pl.BlockSpec((8, LANES), lambda l, c: (jnp.maximum((nc - 1 - c) * tb8 - 1, 0), l))
    vec = pl.BlockSpec((1, LANES), lambda l, c: (0, l))
    cwb = pl.BlockSpec((CONV_WIDTH, LANES), lambda l, c: (0, l))
    mat = pl.BlockSpec((None, LANES, LANES), lambda l, c: (l, 0, 0))
    vshape = jax.ShapeDtypeStruct((1, W), F32)
    mshape = jax.ShapeDtypeStruct((nl, LANES, LANES), F32)
    return pl.pallas_call(
        body, name="lru_bwd", grid=(nl, nc),
        in_specs=[seq, seq, seq, seq, seq, before, cwb, mat, mat, vec, vec, vec],
        out_specs=(seq, seq, cwb, vec, vec, vec, vec, mat, mat),
        out_shape=(jax.ShapeDtypeStruct((S, W), BF16), jax.ShapeDtypeStruct((S, W), BF16),
                   jax.ShapeDtypeStruct((CONV_WIDTH, W), F32), vshape, vshape, vshape, vshape, mshape, mshape),
        scratch_shapes=[pltpu.VMEM((8, LANES), F32)] * 3,
        compiler_params=_cparams(("parallel", "arbitrary")),
    )(dya, u, ug, xc, h, h, conv_w, wa_bd, wx_bd, ba.reshape(1, W), bx.reshape(1, W), lam.reshape(1, W))


def _decay_fwd(f_logit, bf):
    S = f_logit.shape[0]
    T = min(LRU_CHUNK, S)

    def body(f_ref, b_ref, o_ref, carry):
        @pl.when(pl.program_id(0) == 0)
        def _():
            carry[...] = jnp.zeros_like(carry)

        row = lax.broadcasted_iota(jnp.int32, (T, LANES), 0)
        v = jax.nn.log_sigmoid(f_ref[...] + b_ref[...])
        d = 1
        while d < T:
            v = jnp.where(row >= d, v + pltpu.roll(v, d, axis=0), v)
            d *= 2
        v = v + carry[7:8, :]
        carry[...] = v[T - 8:T]
        o_ref[...] = v

    return pl.pallas_call(
        body, name="decay_fwd", grid=(S // T,),
        in_specs=[pl.BlockSpec((T, LANES), lambda c: (c, 0)), pl.BlockSpec((1, LANES), lambda c: (0, 0))],
        out_specs=pl.BlockSpec((T, LANES), lambda c: (c, 0)),
        out_shape=jax.ShapeDtypeStruct((S, LANES), F32), scratch_shapes=[pltpu.VMEM((8, LANES), F32)],
        compiler_params=_cparams(("arbitrary",)),
    )(f_logit, bf)


def _decay_bwd(d_dec, f_logit, bf):
    S = f_logit.shape[0]
    T = min(LRU_CHUNK, S)
    nc = S // T

    def body(dd_ref, f_ref, b_ref, df_ref, db_ref, carry):
        @pl.when(pl.program_id(0) == 0)
        def _():
            carry[...] = jnp.zeros_like(carry)
            db_ref[...] = jnp.zeros_like(db_ref)

        row = lax.broadcasted_iota(jnp.int32, (T, LANES), 0)
        v = dd_ref[...]
        d = 1
        while d < T:
            v = jnp.where(row < T - d, v + pltpu.roll(v, T - d, axis=0), v)
            d *= 2
        v = v + carry[0:1, :]
        carry[...] = v[0:8]
        df = v * jax.nn.sigmoid(-(f_ref[...] + b_ref[...]))
        df_ref[...] = df.astype(BF16)
        db_ref[...] += jnp.sum(df, axis=0, keepdims=True)

    seq = pl.BlockSpec((T, LANES), lambda c: (nc - 1 - c, 0))
    vec = pl.BlockSpec((1, LANES), lambda c: (0, 0))
    return pl.pallas_call(
        body, name="decay_bwd", grid=(nc,), in_specs=[seq, seq, vec], out_specs=(seq, vec),
        out_shape=(jax.ShapeDtypeStruct((S, LANES), BF16), jax.ShapeDtypeStruct((1, LANES), F32)),
        scratch_shapes=[pltpu.VMEM((8, LANES), F32)], compiler_params=_cparams(("arbitrary",)),
    )(d_dec, f_logit, bf)


def _rope_tables(S):
    pos = jnp.arange(S, dtype=F32)
    inv_freq = ROPE_BASE ** (-jnp.arange(0, MLA_ROPE, 2, dtype=F32) / MLA_ROPE)
    ang = pos[:, None] * inv_freq[None, :]
    cos, sin = jnp.cos(ang), jnp.sin(ang)
    half = MLA_ROPE // 2
    z = lambda n: jnp.zeros((S, n), F32)
    c_q = jnp.concatenate([jnp.ones((S, MLA_NOPE), F32), cos, cos, z(HEAD_PAD - MLA_NOPE - MLA_ROPE)], axis=1)
    c_k = jnp.concatenate([z(MLA_NOPE), cos, cos, z(HEAD_PAD - MLA_NOPE - MLA_ROPE)], axis=1)
    s_lo = jnp.concatenate([z(MLA_NOPE), -sin, z(HEAD_PAD - MLA_NOPE - half)], axis=1)
    s_hi = jnp.concatenate([z(MLA_NOPE + half), sin, z(HEAD_PAD - MLA_NOPE - MLA_ROPE)], axis=1)
    return c_q, c_k, s_lo, s_hi


def _rot(v, c, s_lo, s_hi):
    half = MLA_ROPE // 2
    return v * c + pltpu.roll(v, LANES - half, axis=1) * s_lo + pltpu.roll(v, half, axis=1) * s_hi


def _rot_t(dv, c, s_lo, s_hi):
    half = MLA_ROPE // 2
    return dv * c + pltpu.roll(dv * s_lo, half, axis=1) + pltpu.roll(dv * s_hi, LANES - half, axis=1)


def _rope_q(q_pre, c_q, s_lo, s_hi, *, transpose, out_dtype, name):
    S, W = q_pre.shape
    tm = _rows(S)
    fn = _rot_t if transpose else _rot

    def body(q_ref, c_ref, lo_ref, hi_ref, o_ref):
        c, lo, hi = c_ref[...], lo_ref[...], hi_ref[...]
        for hd in range(W // LANES):
            cols = slice(hd * LANES, (hd + 1) * LANES)
            o_ref[:, cols] = fn(q_ref[:, cols] * MLA_SCALE, c, lo, hi).astype(out_dtype)

    blk = pl.BlockSpec((tm, W), lambda i: (i, 0))
    tab = pl.BlockSpec((tm, LANES), lambda i: (i, 0))
    return pl.pallas_call(
        body, name=name, grid=(S // tm,), in_specs=[blk, tab, tab, tab], out_specs=blk,
        out_shape=jax.ShapeDtypeStruct((S, W), out_dtype), compiler_params=_cparams(("parallel",)),
    )(q_pre, c_q, s_lo, s_hi)


def _rope_k(k_pre, k_rope, c_k, s_lo, s_hi):
    S, W = k_pre.shape
    tm = _rows(S)

    def body(k_ref, r_ref, c_ref, lo_ref, hi_ref, o_ref):
        rot = _rot(r_ref[...], c_ref[...], lo_ref[...], hi_ref[...])
        for hd in range(W // LANES):
            cols = slice(hd * LANES, (hd + 1) * LANES)
            o_ref[:, cols] = (k_ref[:, cols] + rot).astype(BF16)

    blk = pl.BlockSpec((tm, W), lambda i: (i, 0))
    tab = pl.BlockSpec((tm, LANES), lambda i: (i, 0))
    return pl.pallas_call(
        body, name="rope_k", grid=(S // tm,), in_specs=[blk, tab, tab, tab, tab], out_specs=blk,
        out_shape=jax.ShapeDtypeStruct((S, W), BF16), compiler_params=_cparams(("parallel",)),
    )(k_pre, k_rope, c_k, s_lo, s_hi)


def _rope_k_bwd(dk, c_k, s_lo, s_hi):
    S, W = dk.shape
    tm = _rows(S)

    def body(dk_ref, c_ref, lo_ref, hi_ref, o_ref):
        tot = dk_ref[:, 0:LANES]
        for hd in range(1, W // LANES):
            tot = tot + dk_ref[:, hd * LANES:(hd + 1) * LANES]
        o_ref[...] = _rot_t(tot, c_ref[...], lo_ref[...], hi_ref[...]).astype(BF16)

    tab = pl.BlockSpec((tm, LANES), lambda i: (i, 0))
    return pl.pallas_call(
        body, name="rope_k_bwd", grid=(S // tm,), in_specs=[pl.BlockSpec((tm, W), lambda i: (i, 0)), tab, tab, tab],
        out_specs=tab, out_shape=jax.ShapeDtypeStruct((S, LANES), BF16), compiler_params=_cparams(("parallel",)),
    )(dk, c_k, s_lo, s_hi)


def _pairs(n, by_key):
    if by_key:
        pr = [(i, j) for j in range(n) for i in range(j, n)]
    else:
        pr = [(i, j) for i in range(n) for j in range(i + 1)]
    return (jnp.asarray(np.array([p[0] for p in pr], np.int32)), jnp.asarray(np.array([p[1] for p in pr], np.int32)), len(pr))


def _unit_mask(shape, unit, key_axis):
    q = lax.broadcasted_iota(jnp.int32, shape, 1 - key_axis)
    k = lax.broadcasted_iota(jnp.int32, shape, key_axis)
    if unit > 1:
        q, k = q // unit, k // unit
    return q >= k


_NT = (((1,), (1,)), ((), ()))


def _attn_fwd(q, k, v, dec_col, dec_row, *, unit, name):
    S, W = q.shape
    H = W // LANES
    T = min(ATTN_TILE, S)
    n = S // T
    qi, kj, npairs = _pairs(n, by_key=False)
    has_dec = dec_col is not None

    def body(qi_ref, kj_ref, *refs):
        if has_dec:
            q_ref, k_ref, v_ref, dc_ref, dr_ref, o_ref, lse_ref, m_s, l_s, acc = refs
        else:
            q_ref, k_ref, v_ref, o_ref, lse_ref, m_s, l_s, acc = refs
        t = pl.program_id(1)
        i, j = qi_ref[t], kj_ref[t]

        @pl.when(j == 0)
        def _():
            m_s[...] = jnp.full_like(m_s, NEG_INF)
            l_s[...] = jnp.zeros_like(l_s)
            acc[...] = jnp.zeros_like(acc)

        def step(diag):
            s = lax.dot_general(q_ref[...], k_ref[...], _NT, preferred_element_type=F32)
            if has_dec:
                s = s + (dc_ref[...] - dr_ref[...])
            if diag:
                s = jnp.where(_unit_mask((T, T), unit, 1), s, NEG_INF)
            m_prev = m_s[...]
            m_new = jnp.maximum(m_prev, jnp.max(s, axis=1, keepdims=True))
            alpha = jnp.exp(m_prev - m_new)
            p = jnp.exp(s - m_new)
            l_s[...] = alpha * l_s[...] + jnp.sum(p, axis=1, keepdims=True)
            acc[...] = alpha * acc[...] + jnp.dot(p.astype(BF16), v_ref[...], preferred_element_type=F32)
            m_s[...] = m_new

        @pl.when(j < i)
        def _():
            step(False)

        @pl.when(j == i)
        def _():
            step(True)
            o_ref[...] = (acc[...] / l_s[...]).astype(BF16)
            lse_ref[...] = m_s[...] + jnp.log(l_s[...])

    qb = pl.BlockSpec((T, LANES), lambda h, t, qi, kj: (qi[t], h))
    kb = pl.BlockSpec((T, LANES), lambda h, t, qi, kj: (kj[t], h))
    colq = pl.BlockSpec((None, T, 1), lambda h, t, qi, kj: (h, qi[t], 0))
    rowk = pl.BlockSpec((None, 1, T), lambda h, t, qi, kj: (h, 0, kj[t]))
    in_specs = [qb, kb, kb] + ([colq, rowk] if has_dec else [])
    args = (q, k, v) + ((dec_col, dec_row) if has_dec else ())
    return pl.pallas_call(
        body, name=name,
        grid_spec=pltpu.PrefetchScalarGridSpec(
            num_scalar_prefetch=2, grid=(H, npairs), in_specs=in_specs, out_specs=(qb, colq),
            scratch_shapes=[pltpu.VMEM((T, 1), F32), pltpu.VMEM((T, 1), F32), pltpu.VMEM((T, LANES), F32)]),
        out_shape=(jax.ShapeDtypeStruct((S, W), BF16), jax.ShapeDtypeStruct((H, S, 1), F32)),
        compiler_params=_cparams(("parallel", "arbitrary")),
    )(qi, kj, *args)


def _attn_delta(do, o):
    S, W = o.shape
    H = W // LANES
    tm = _rows(S)

    def body(do_ref, o_ref, d_ref):
        d_ref[...] = jnp.sum(do_ref[...].astype(F32) * o_ref[...].astype(F32), axis=1, keepdims=True)

    blk = pl.BlockSpec((tm, LANES), lambda h, i: (i, h))
    return pl.pallas_call(
        body, name="attn_delta", grid=(H, S // tm), in_specs=[blk, blk],
        out_specs=pl.BlockSpec((None, tm, 1), lambda h, i: (h, i, 0)),
        out_shape=jax.ShapeDtypeStruct((H, S, 1), F32), compiler_params=_cparams(("parallel", "parallel")),
    )(do, o)


def _attn_bwd_dq(q, k, v, do, lse, delta, dec_col, dec_row, *, unit, out_dtype, name):
    S, W = q.shape
    H = W // LANES
    T = min(ATTN_TILE, S)
    n = S // T
    qi, kj, npairs = _pairs(n, by_key=False)
    has_dec = dec_col is not None

    def body(qi_ref, kj_ref, *refs):
        if has_dec:
            q_ref, k_ref, v_ref, do_ref, lse_ref, dl_ref, dc_ref, dr_ref, dq_ref, dd_ref, acc, dacc = refs
        else:
            q_ref, k_ref, v_ref, do_ref, lse_ref, dl_ref, dq_ref, acc = refs
        t = pl.program_id(1)
        i, j = qi_ref[t], kj_ref[t]

        @pl.when(j == 0)
        def _():
            acc[...] = jnp.zeros_like(acc)
            if has_dec:
                dacc[...] = jnp.zeros_like(dacc)

        def step(diag):
            kv = k_ref[...]
            s = lax.dot_general(q_ref[...], kv, _NT, preferred_element_type=F32)
            if has_dec:
                s = s + (dc_ref[...] - dr_ref[...])
            if diag:
                s = jnp.where(_unit_mask((T, T), unit, 1), s, NEG_INF)
            p = jnp.exp(s - lse_ref[...])
            dp = lax.dot_general(do_ref[...], v_ref[...], _NT, preferred_element_type=F32)
            ds = p * (dp - dl_ref[...])
            acc[...] += jnp.dot(ds.astype(BF16), kv, preferred_element_type=F32)
            if has_dec:
                dacc[...] += jnp.sum(ds, axis=1, keepdims=True)

        @pl.when(j < i)
        def _():
            step(False)

        @pl.when(j == i)
        def _():
            step(True)
            dq_ref[...] = acc[...].astype(out_dtype)
            if has_dec:
                dd_ref[...] = dacc[...]

    qb = pl.BlockSpec((T, LANES), lambda h, t, qi, kj: (qi[t], h))
    kb = pl.BlockSpec((T, LANES), lambda h, t, qi, kj: (kj[t], h))
    colq = pl.BlockSpec((None, T, 1), lambda h, t, qi, kj: (h, qi[t], 0))
    rowk = pl.BlockSpec((None, 1, T), lambda h, t, qi, kj: (h, 0, kj[t]))
    in_specs = [qb, kb, kb, qb, colq, colq] + ([colq, rowk] if has_dec else [])
    args = (q, k, v, do, lse, delta) + ((dec_col, dec_row) if has_dec else ())
    scratch = [pltpu.VMEM((T, LANES), F32)] + ([pltpu.VMEM((T, 1), F32)] if has_dec else [])
    out_specs = (qb, colq) if has_dec else qb
    out_shape = jax.ShapeDtypeStruct((S, W), out_dtype)
    if has_dec:
        out_shape = (out_shape, jax.ShapeDtypeStruct((H, S, 1), F32))
    res = pl.pallas_call(
        body, name=name,
        grid_spec=pltpu.PrefetchScalarGridSpec(num_scalar_prefetch=2, grid=(H, npairs), in_specs=in_specs,
                                               out_specs=out_specs, scratch_shapes=scratch),
        out_shape=out_shape, compiler_params=_cparams(("parallel", "arbitrary")),
    )(qi, kj, *args)
    return res if has_dec else (res, None)


def _attn_bwd_dkv(q, k, v, do, lse_row, delta_row, dec_col, dec_row, *, unit, dk_dtype, name):
    S, W = q.shape
    H = W // LANES
    T = min(ATTN_TILE, S)
    n = S // T
    qi, kj, npairs = _pairs(n, by_key=True)
    has_dec = dec_col is not None

    def body(qi_ref, kj_ref, *refs):
        if has_dec:
            q_ref, k_ref, v_ref, do_ref, lse_ref, dl_ref, dc_ref, dr_ref, dk_ref, dv_ref, dd_ref, kacc, vacc, dacc = refs
        else:
            q_ref, k_ref, v_ref, do_ref, lse_ref, dl_ref, dk_ref, dv_ref, kacc, vacc = refs
        t = pl.program_id(1)
        i, j = qi_ref[t], kj_ref[t]

        @pl.when(i == j)
        def _():
            kacc[...] = jnp.zeros_like(kacc)
            vacc[...] = jnp.zeros_like(vacc)
            if has_dec:
                dacc[...] = jnp.zeros_like(dacc)

        def step(diag):
            qv, dov = q_ref[...], do_ref[...]
            st = lax.dot_general(k_ref[...], qv, _NT, preferred_element_type=F32)
            if has_dec:
                st = st + (dr_ref[...] - dc_ref[...])
            if diag:
                st = jnp.where(_unit_mask((T, T), unit, 0), st, NEG_INF)
            pt = jnp.exp(st - lse_ref[...])
            dpt = lax.dot_general(v_ref[...], dov, _NT, preferred_element_type=F32)
            dst = pt * (dpt - dl_ref[...])
            vacc[...] += jnp.dot(pt.astype(BF16), dov, preferred_element_type=F32)
            kacc[...] += jnp.dot(dst.astype(BF16), qv, preferred_element_type=F32)
            if has_dec:
                dacc[...] -= jnp.sum(dst, axis=1, keepdims=True)

        @pl.when(i == j)
        def _():
            step(True)

        @pl.when(i > j)
        def _():
            step(False)

        @pl.when(i == n - 1)
        def _():
            dk_ref[...] = kacc[...].astype(dk_dtype)
            dv_ref[...] = vacc[...].astype(BF16)
            if has_dec:
                dd_ref[...] = dacc[...]

    qb = pl.BlockSpec((T, LANES), lambda h, t, qi, kj: (qi[t], h))
    kb = pl.BlockSpec((T, LANES), lambda h, t, qi, kj: (kj[t], h))
    rowq = pl.BlockSpec((None, 1, T), lambda h, t, qi, kj: (h, 0, qi[t]))
    colk = pl.BlockSpec((None, T, 1), lambda h, t, qi, kj: (h, kj[t], 0))
    in_specs = [qb, kb, kb, qb, rowq, rowq] + ([colk, rowq] if has_dec else [])
    args = (q, k, v, do, lse_row, delta_row) + ((dec_col, dec_row) if has_dec else ())
    scratch = [pltpu.VMEM((T, LANES), F32)] * 2 + ([pltpu.VMEM((T, 1), F32)] if has_dec else [])
    out_specs = (kb, kb) + ((colk,) if has_dec else ())
    out_shape = (jax.ShapeDtypeStruct((S, W), dk_dtype), jax.ShapeDtypeStruct((S, W), BF16))
    if has_dec:
        out_shape = out_shape + (jax.ShapeDtypeStruct((H, S, 1), F32),)
    res = pl.pallas_call(
        body, name=name,
        grid_spec=pltpu.PrefetchScalarGridSpec(num_scalar_prefetch=2, grid=(H, npairs), in_specs=in_specs,
                                               out_specs=out_specs, scratch_shapes=scratch),
        out_shape=out_shape, compiler_params=_cparams(("parallel", "arbitrary")),
    )(qi, kj, *args)
    return res if has_dec else (res[0], res[1], None)


def _attn_bwd(q, k, v, o, lse, do, dec_col, dec_row, *, unit, dq_dtype, dk_dtype, name):
    H, S = lse.shape[0], lse.shape[1]
    delta = _attn_delta(do, o)
    dq, dd_q = _attn_bwd_dq(q, k, v, do, lse, delta, dec_col, dec_row, unit=unit, out_dtype=dq_dtype, name=name + "_dq")
    dk, dv, dd_k = _attn_bwd_dkv(q, k, v, do, lse.reshape(H, 1, S), delta.reshape(H, 1, S), dec_col, dec_row,
                                 unit=unit, dk_dtype=dk_dtype, name=name + "_dkv")
    d_dec = None if dec_col is None else dd_q + dd_k
    return dq, dk, dv, d_dec


ONES_LANE = 64


def _ones_lane_bias():
    one = np.zeros((HEADS, HEAD_PAD), np.float32)
    one[:, ONES_LANE] = 1.0
    return jnp.asarray(one.reshape(1, HEADS * HEAD_PAD))


def _lane_sum(t):
    tot = t[:, 0:LANES]
    for c in range(1, t.shape[1] // LANES):
        tot = tot + t[:, c * LANES:(c + 1) * LANES]
    return tot


def _fa_fwd(q, k, v, dec_row, *, unit, name):
    S, W = q.shape
    H = W // LANES
    T = min(ATTN_TILE, S)
    n, reps = S // T, T // LANES
    qi, kj, npairs = _pairs(n, by_key=False)
    has_dec = dec_row is not None

    def body(qi_ref, kj_ref, *refs):
        if has_dec:
            q_ref, k_ref, v_ref, dr_ref, o_ref, lse_ref, lrow_ref, m_s, acc = refs
        else:
            q_ref, k_ref, v_ref, o_ref, lse_ref, lrow_ref, m_s, acc = refs
        t = pl.program_id(1)
        i, j = qi_ref[t], kj_ref[t]

        @pl.when(j == 0)
        def _():
            m_s[...] = jnp.full_like(m_s, NEG_INF)
            acc[...] = jnp.zeros_like(acc)

        def step(diag):
            s = lax.dot_general(q_ref[...], k_ref[...], _NT, preferred_element_type=F32)
            if has_dec:
                s = s - dr_ref[...]
            if diag:
                s = jnp.where(_unit_mask((T, T), unit, 1), s, NEG_INF)
            m_prev = m_s[...]
            m_new = jnp.maximum(m_prev, jnp.max(s, axis=1, keepdims=True))
            alpha = jnp.exp(m_prev - m_new)
            p = jnp.exp(s - jnp.tile(m_new, (1, reps)))
            acc[...] = alpha * acc[...] + jnp.dot(p.astype(BF16), v_ref[...], preferred_element_type=F32)
            m_s[...] = m_new

        @pl.when(j < i)
        def _():
            step(False)

        @pl.when(j == i)
        def _():
            step(True)
            av = acc[...]
            l = av[:, ONES_LANE:ONES_LANE + 1]
            lane = lax.broadcasted_iota(jnp.int32, (T, LANES), 1)
            o_ref[...] = jnp.where(lane < ONES_LANE, av / l, 0.0).astype(BF16)
            lse = m_s[...] + jnp.log(l)
            lse_ref[...] = lse
            lrow_ref[...] = lse.T[0:1, :]

    qb = pl.BlockSpec((T, LANES), lambda h, t, qi, kj: (qi[t], h))
    kb = pl.BlockSpec((T, LANES), lambda h, t, qi, kj: (kj[t], h))
    repq = pl.BlockSpec((None, T, LANES), lambda h, t, qi, kj: (h, qi[t], 0))
    rowq = pl.BlockSpec((None, 1, T), lambda h, t, qi, kj: (h, 0, qi[t]))
    rowk = pl.BlockSpec((None, 1, T), lambda h, t, qi, kj: (h, 0, kj[t]))
    in_specs = [qb, kb, kb] + ([rowk] if has_dec else [])
    args = (q, k, v) + ((dec_row,) if has_dec else ())
    return pl.pallas_call(
        body, name=name,
        grid_spec=pltpu.PrefetchScalarGridSpec(
            num_scalar_prefetch=2, grid=(H, npairs), in_specs=in_specs, out_specs=(qb, repq, rowq),
            scratch_shapes=[pltpu.VMEM((T, LANES), F32), pltpu.VMEM((T, LANES), F32)]),
        out_shape=(jax.ShapeDtypeStruct((S, W), BF16), jax.ShapeDtypeStruct((H, S, LANES), F32),
                   jax.ShapeDtypeStruct((H, 1, S), F32)),
        compiler_params=_cparams(("parallel", "arbitrary")),
    )(qi, kj, *args)


def _fa_delta(do, o):
    S, W = o.shape
    H = W // LANES
    tm = _rows(S)

    def body(do_ref, o_ref, d_ref, drow_ref):
        for hd in range(H):
            cols = slice(hd * LANES, (hd + 1) * LANES)
            d = jnp.sum(do_ref[:, cols].astype(F32) * o_ref[:, cols].astype(F32), axis=1, keepdims=True)
            rep = jnp.broadcast_to(d, (tm, LANES))
            d_ref[hd] = rep
            drow_ref[hd] = rep.T[0:1, :]

    blk = pl.BlockSpec((tm, W), lambda i: (i, 0))
    return pl.pallas_call(
        body, name="attn_delta", grid=(S // tm,), in_specs=[blk, blk],
        out_specs=(pl.BlockSpec((H, tm, LANES), lambda i: (0, i, 0)), pl.BlockSpec((H, 1, tm), lambda i: (0, 0, i))),
        out_shape=(jax.ShapeDtypeStruct((H, S, LANES), F32), jax.ShapeDtypeStruct((H, 1, S), F32)),
        compiler_params=_cparams(("parallel",)),
    )(do, o)


def _fa_bwd_dq(q, k, v, do, lse, delta, dec_row, *, unit, out_dtype, name):
    S, W = q.shape
    H = W // LANES
    T = min(ATTN_TILE, S)
    n, reps = S // T, T // LANES
    qi, kj, npairs = _pairs(n, by_key=False)
    has_dec = dec_row is not None

    def body(qi_ref, kj_ref, *refs):
        if has_dec:
            q_ref, k_ref, v_ref, do_ref, lse_ref, dl_ref, dr_ref, dq_ref, dd_ref, acc, dacc = refs
        else:
            q_ref, k_ref, v_ref, do_ref, lse_ref, dl_ref, dq_ref, acc = refs
        t = pl.program_id(1)
        i, j = qi_ref[t], kj_ref[t]

        @pl.when(j == 0)
        def _():
            acc[...] = jnp.zeros_like(acc)
            if has_dec:
                dacc[...] = jnp.zeros_like(dacc)

        def step(diag):
            kv = k_ref[...]
            s = lax.dot_general(q_ref[...], kv, _NT, preferred_element_type=F32)
            if has_dec:
                s = s - dr_ref[...]
            if diag:
                s = jnp.where(_unit_mask((T, T), unit, 1), s, NEG_INF)
            p = jnp.exp(s - jnp.tile(lse_ref[...], (1, reps)))
            dp = lax.dot_general(do_ref[...], v_ref[...], _NT, preferred_element_type=F32)
            ds = p * (dp - jnp.tile(dl_ref[...], (1, reps)))
            acc[...] += jnp.dot(ds.astype(BF16), kv, preferred_element_type=F32)
            if has_dec:
                dacc[...] += _lane_sum(ds)

        @pl.when(j < i)
        def _():
            step(False)

        @pl.when(j == i)
        def _():
            step(True)
            dq_ref[...] = acc[...].astype(out_dtype)
            if has_dec:
                dd_ref[...] = jnp.broadcast_to(jnp.sum(dacc[...], axis=1, keepdims=True), (T, LANES))

    qb = pl.BlockSpec((T, LANES), lambda h, t, qi, kj: (qi[t], h))
    kb = pl.BlockSpec((T, LANES), lambda h, t, qi, kj: (kj[t], h))
    repq = pl.BlockSpec((None, T, LANES), lambda h, t, qi, kj: (h, qi[t], 0))
    rowk = pl.BlockSpec((None, 1, T), lambda h, t, qi, kj: (h, 0, kj[t]))
    in_specs = [qb, kb, kb, qb, repq, repq] + ([rowk] if has_dec else [])
    args = (q, k, v, do, lse, delta) + ((dec_row,) if has_dec else ())
    out_shape = jax.ShapeDtypeStruct((S, W), out_dtype)
    res = pl.pallas_call(
        body, name=name,
        grid_spec=pltpu.PrefetchScalarGridSpec(num_scalar_prefetch=2, grid=(H, npairs), in_specs=in_specs,
                                               out_specs=(qb, repq) if has_dec else qb,
                                               scratch_shapes=[pltpu.VMEM((T, LANES), F32)] * (2 if has_dec else 1)),
        out_shape=(out_shape, jax.ShapeDtypeStruct((H, S, LANES), F32)) if has_dec else out_shape,
        compiler_params=_cparams(("parallel", "arbitrary")),
    )(qi, kj, *args)
    return res if has_dec else (res, None)


def _fa_bwd_dkv(q, k, v, do, lse_row, delta_row, dec_rep, *, unit, dk_dtype, name):
    S, W = q.shape
    H = W // LANES
    T = min(ATTN_TILE, S)
    n, reps = S // T, T // LANES
    qi, kj, npairs = _pairs(n, by_key=True)
    has_dec = dec_rep is not None

    def body(qi_ref, kj_ref, *refs):
        if has_dec:
            q_ref, k_ref, v_ref, do_ref, lse_ref, dl_ref, dc_ref, dk_ref, dv_ref, dd_ref, kacc, vacc, dacc = refs
        else:
            q_ref, k_ref, v_ref, do_ref, lse_ref, dl_ref, dk_ref, dv_ref, kacc, vacc = refs
        t = pl.program_id(1)
        i, j = qi_ref[t], kj_ref[t]

        @pl.when(i == j)
        def _():
            kacc[...] = jnp.zeros_like(kacc)
            vacc[...] = jnp.zeros_like(vacc)
            if has_dec:
                dacc[...] = jnp.zeros_like(dacc)

        def step(diag):
            qv, dov = q_ref[...], do_ref[...]
            st = lax.dot_general(k_ref[...], qv, _NT, preferred_element_type=F32)
            if has_dec:
                st = st - jnp.tile(dc_ref[...], (1, reps))
            if diag:
                st = jnp.where(_unit_mask((T, T), unit, 0), st, NEG_INF)
            pt = jnp.exp(st - lse_ref[...])
            dpt = lax.dot_general(v_ref[...], dov, _NT, preferred_element_type=F32)
            dst = pt * (dpt - dl_ref[...])
            vacc[...] += jnp.dot(pt.astype(BF16), dov, preferred_element_type=F32)
            kacc[...] += jnp.dot(dst.astype(BF16), qv, preferred_element_type=F32)
            if has_dec:
                dacc[...] += _lane_sum(dst)

        @pl.when(i == j)
        def _():
            step(True)

        @pl.when(i > j)
        def _():
            step(False)

        @pl.when(i == n - 1)
        def _():
            dk_ref[...] = kacc[...].astype(dk_dtype)
            dv_ref[...] = vacc[...].astype(BF16)
            if has_dec:
                dd_ref[...] = jnp.broadcast_to(-jnp.sum(dacc[...], axis=1, keepdims=True), (T, LANES))

    qb = pl.BlockSpec((T, LANES), lambda h, t, qi, kj: (qi[t], h))
    kb = pl.BlockSpec((T, LANES), lambda h, t, qi, kj: (kj[t], h))
    rowq = pl.BlockSpec((None, 1, T), lambda h, t, qi, kj: (h, 0, qi[t]))
    repk = pl.BlockSpec((None, T, LANES), lambda h, t, qi, kj: (h, kj[t], 0))
    in_specs = [qb, kb, kb, qb, rowq, rowq] + ([repk] if has_dec else [])
    args = (q, k, v, do, lse_row, delta_row) + ((dec_rep,) if has_dec else ())
    scratch = [pltpu.VMEM((T, LANES), F32)] * (3 if has_dec else 2)
    out_specs = (kb, kb) + ((repk,) if has_dec else ())
    out_shape = (jax.ShapeDtypeStruct((S, W), dk_dtype), jax.ShapeDtypeStruct((S, W), BF16))
    if has_dec:
        out_shape = out_shape + (jax.ShapeDtypeStruct((H, S, LANES), F32),)
    res = pl.pallas_call(
        body, name=name,
        grid_spec=pltpu.PrefetchScalarGridSpec(num_scalar_prefetch=2, grid=(H, npairs), in_specs=in_specs,
                                               out_specs=out_specs, scratch_shapes=scratch),
        out_shape=out_shape, compiler_params=_cparams(("parallel", "arbitrary")),
    )(qi, kj, *args)
    return res if has_dec else (res[0], res[1], None)


_TN = (((0,), (0,)), ((), ()))


def _fa_bwd_fused(q, k, v, do, o, lse, dec_row, *, unit, dq_dtype, dk_dtype, name):
    S, W = q.shape
    H = W // LANES
    T = min(ATTN_TILE, S)
    n, reps = S // T, T // LANES
    qi, kj, npairs = _pairs(n, by_key=False)
    has_dec = dec_row is not None

    def body(qi_ref, kj_ref, *refs):
        if has_dec:
            (q_ref, k_ref, v_ref, do_ref, o_ref, lse_ref, dr_ref, dq_ref, dk_ref, dv_ref, ddq_ref, ddk_ref,
             qacc, kacc, vacc, dl_ref, rsum, csum) = refs
        else:
            q_ref, k_ref, v_ref, do_ref, o_ref, lse_ref, dq_ref, dk_ref, dv_ref, qacc, kacc, vacc, dl_ref = refs
        t = pl.program_id(1)
        i, j = qi_ref[t], kj_ref[t]

        @pl.when(t == 0)
        def _():
            kacc[...] = jnp.zeros_like(kacc)
            vacc[...] = jnp.zeros_like(vacc)
            if has_dec:
                csum[...] = jnp.zeros_like(csum)

        @pl.when(j == 0)
        def _():
            qacc[...] = jnp.zeros_like(qacc)
            delta = jnp.sum(do_ref[...].astype(F32) * o_ref[...].astype(F32), axis=1, keepdims=True)
            dl_ref[...] = jnp.broadcast_to(delta, (T, LANES))
            if has_dec:
                rsum[...] = jnp.zeros_like(rsum)

        def step(diag):
            qv, kv, dov = q_ref[...], k_ref[...], do_ref[...]
            s = lax.dot_general(qv, kv, _NT, preferred_element_type=F32)
            if has_dec:
                s = s - dr_ref[...]
            if diag:
                s = jnp.where(_unit_mask((T, T), unit, 1), s, NEG_INF)
            p = jnp.exp(s - jnp.tile(lse_ref[...], (1, reps)))
            dp = lax.dot_general(dov, v_ref[...], _NT, preferred_element_type=F32)
            ds = p * (dp - jnp.tile(dl_ref[...], (1, reps)))
            pb, dsb = p.astype(BF16), ds.astype(BF16)
            qacc[...] += jnp.dot(dsb, kv, preferred_element_type=F32)
            vacc[j] += lax.dot_general(pb, dov, _TN, preferred_element_type=F32)
            kacc[j] += lax.dot_general(dsb, qv, _TN, preferred_element_type=F32)
            if has_dec:
                rsum[...] += _lane_sum(ds)
                csum[j] -= jnp.sum(ds, axis=0, keepdims=True)

        @pl.when(j < i)
        def _():
            step(False)

        @pl.when(j == i)
        def _():
            step(True)
            dq_ref[...] = qacc[...].astype(dq_dtype)
            if has_dec:
                ddq_ref[...] = jnp.broadcast_to(jnp.sum(rsum[...], axis=1, keepdims=True), (T, LANES))

        @pl.when(t == npairs - 1)
        def _():
            for jj in range(n):
                dk_ref[jj * T:(jj + 1) * T, :] = kacc[jj].astype(dk_dtype)
                dv_ref[jj * T:(jj + 1) * T, :] = vacc[jj].astype(BF16)
                if has_dec:
                    ddk_ref[:, jj * T:(jj + 1) * T] = csum[jj]

    qb = pl.BlockSpec((T, LANES), lambda h, t, qi, kj: (qi[t], h))
    kb = pl.BlockSpec((T, LANES), lambda h, t, qi, kj: (kj[t], h))
    head = pl.BlockSpec((S, LANES), lambda h, t, qi, kj: (0, h))
    repq = pl.BlockSpec((None, T, LANES), lambda h, t, qi, kj: (h, qi[t], 0))
    rowk = pl.BlockSpec((None, 1, T), lambda h, t, qi, kj: (h, 0, kj[t]))
    rowh = pl.BlockSpec((None, 1, S), lambda h, t, qi, kj: (h, 0, 0))
    in_specs = [qb, kb, kb, qb, qb, repq] + ([rowk] if has_dec else [])
    args = (q, k, v, do, o, lse) + ((dec_row,) if has_dec else ())
    out_specs = [qb, head, head] + ([repq, rowh] if has_dec else [])
    out_shape = [jax.ShapeDtypeStruct((S, W), dq_dtype), jax.ShapeDtypeStruct((S, W), dk_dtype), jax.ShapeDtypeStruct((S, W), BF16)]
    scratch = [pltpu.VMEM((T, LANES), F32), pltpu.VMEM((n, T, LANES), F32), pltpu.VMEM((n, T, LANES), F32),
               pltpu.VMEM((T, LANES), F32)]
    if has_dec:
        out_shape += [jax.ShapeDtypeStruct((H, S, LANES), F32), jax.ShapeDtypeStruct((H, 1, S), F32)]
        scratch += [pltpu.VMEM((T, LANES), F32), pltpu.VMEM((n, 1, T), F32)]
    res = pl.pallas_call(
        body, name=name,
        grid_spec=pltpu.PrefetchScalarGridSpec(num_scalar_prefetch=2, grid=(H, npairs), in_specs=in_specs,
                                               out_specs=tuple(out_specs), scratch_shapes=scratch),
        out_shape=tuple(out_shape),
        compiler_params=pltpu.CompilerParams(dimension_semantics=("parallel", "arbitrary"), vmem_limit_bytes=FUSED_BWD_VMEM_BYTES),
    )(qi, kj, *args)
    return res if has_dec else (res[0], res[1], res[2], None, None)


FUSED_BWD_VMEM_BYTES = 58 * 1024 * 1024


def _fa_bwd(q, k, v, o, lse, lse_row, do, dec_row, dec_rep, *, unit, dq_dtype, dk_dtype, name):
    dq, dk, dv, dd_q, dd_k = _fa_bwd_fused(q, k, v, do, o, lse, dec_row, unit=unit, dq_dtype=dq_dtype, dk_dtype=dk_dtype, name=name)
    if dd_k is None:
        return dq, dk, dv, None
    return dq, dk, dv, jnp.max(dd_q, axis=2) + dd_k.reshape(dd_k.shape[0], dd_k.shape[2])


def _fa_bwd_split(q, k, v, o, lse, lse_row, do, dec_row, dec_rep, *, unit, dq_dtype, dk_dtype, name):
    delta, delta_row = _fa_delta(do, o)
    dq, dd_q = _fa_bwd_dq(q, k, v, do, lse, delta, dec_row, unit=unit, out_dtype=dq_dtype, name=name + "_dq")
    one = lambda t: jnp.max(t, axis=2)
    dk, dv, dd_k = _fa_bwd_dkv(q, k, v, do, lse_row, delta_row, dec_rep, unit=unit, dk_dtype=dk_dtype, name=name + "_dkv")
    return dq, dk, dv, (None if dd_k is None else one(dd_q) + one(dd_k))


def _merge_fwd(ya, yb, yc, gate_logit, gate_b):
    S, D = ya.shape
    tm = min(256, S)

    def body(a_ref, b_ref, c_ref, gl_ref, gb_ref, o_ref):
        g = jax.nn.sigmoid(gl_ref[...] + gb_ref[...])
        o_ref[...] = (g[:, 0:D] * a_ref[...] + g[:, D:2 * D] * b_ref[...] + g[:, 2 * D:3 * D] * c_ref[...]).astype(BF16)

    row = pl.BlockSpec((tm, D), lambda i: (i, 0))
    return pl.pallas_call(
        body, name="merge_fwd", grid=(S // tm,),
        in_specs=[row, row, row, pl.BlockSpec((tm, 3 * D), lambda i: (i, 0)), pl.BlockSpec((1, 3 * D), lambda i: (0, 0))],
        out_specs=row, out_shape=jax.ShapeDtypeStruct((S, D), BF16), compiler_params=_cparams(("parallel",)),
    )(ya, yb, yc, gate_logit, gate_b.reshape(1, 3 * D))


def _merge_bwd(dm, ya, yb, yc, gate_logit, gate_b):
    S, D = ya.shape
    tm = min(256, S)

    def body(dm_ref, a_ref, b_ref, c_ref, gl_ref, gb_ref, da_ref, db_ref, dc_ref, dgl_ref, dgb_ref):
        g = jax.nn.sigmoid(gl_ref[...] + gb_ref[...])
        dmv = dm_ref[...]
        parts = []
        for n, (y_ref, dy_ref) in enumerate(((a_ref, da_ref), (b_ref, db_ref), (c_ref, dc_ref))):
            gn = g[:, n * D:(n + 1) * D]
            dy_ref[...] = (dmv * gn).astype(BF16)
            parts.append(dmv * y_ref[...] * gn * (1.0 - gn))
        dgl = jnp.concatenate(parts, axis=1)
        dgl_ref[...] = dgl.astype(BF16)

        @pl.when(pl.program_id(0) == 0)
        def _():
            dgb_ref[...] = jnp.zeros_like(dgb_ref)

        dgb_ref[...] += jnp.sum(dgl, axis=0, keepdims=True)

    row = pl.BlockSpec((tm, D), lambda i: (i, 0))
    wide = pl.BlockSpec((tm, 3 * D), lambda i: (i, 0))
    vec = pl.BlockSpec((1, 3 * D), lambda i: (0, 0))
    act = jax.ShapeDtypeStruct((S, D), BF16)
    da, db, dc, dgl, dgb = pl.pallas_call(
        body, name="merge_bwd", grid=(S // tm,), in_specs=[row, row, row, row, wide, vec],
        out_specs=(row, row, row, wide, vec),
        out_shape=(act, act, act, jax.ShapeDtypeStruct((S, 3 * D), BF16), jax.ShapeDtypeStruct((1, 3 * D), F32)),
        compiler_params=_cparams(("arbitrary",)),
    )(dm, ya, yb, yc, gate_logit, gate_b.reshape(1, 3 * D))
    return da, db, dc, dgl, dgb.reshape(3 * D)


def _swiglu_fwd(hf):
    S, W2 = hf.shape
    F = W2 // 2
    tm = min(128, S)

    def body(h_ref, o_ref):
        gt, up = h_ref[:, 0:F], h_ref[:, F:W2]
        o_ref[...] = (gt * jax.nn.sigmoid(gt) * up).astype(BF16)

    return pl.pallas_call(
        body, name="swiglu_fwd", grid=(S // tm,), in_specs=[pl.BlockSpec((tm, W2), lambda i: (i, 0))],
        out_specs=pl.BlockSpec((tm, F), lambda i: (i, 0)), out_shape=jax.ShapeDtypeStruct((S, F), BF16),
        compiler_params=_cparams(("parallel",)),
    )(hf)


def _swiglu_bwd(dact, hf):
    S, W2 = hf.shape
    F = W2 // 2
    tm = min(128, S)

    def body(d_ref, h_ref, o_ref):
        gt, up = h_ref[:, 0:F], h_ref[:, F:W2]
        sg = jax.nn.sigmoid(gt)
        dv = d_ref[...]
        o_ref[:, 0:F] = (dv * up * sg * (1.0 + gt * (1.0 - sg))).astype(BF16)
        o_ref[:, F:W2] = (dv * gt * sg).astype(BF16)

    return pl.pallas_call(
        body, name="swiglu_bwd", grid=(S // tm,),
        in_specs=[pl.BlockSpec((tm, F), lambda i: (i, 0)), pl.BlockSpec((tm, W2), lambda i: (i, 0))],
        out_specs=pl.BlockSpec((tm, W2), lambda i: (i, 0)), out_shape=jax.ShapeDtypeStruct((S, W2), BF16),
        compiler_params=_cparams(("parallel",)),
    )(dact, hf)


def _ple_fwd(x, pre, e, g_next):
    S, D = x.shape
    tm = _rows(S)
    with_norm = g_next is not None

    def body(*refs):
        x_ref, p_ref, e_ref = refs[:3]
        xn = x_ref[...] + jax.nn.sigmoid(p_ref[...]) * e_ref[...]
        if with_norm:
            g_ref, o_ref, h_ref = refs[3:]
            rstd = lax.rsqrt(jnp.mean(xn * xn, axis=1, keepdims=True) + EPS)
            h_ref[...] = (xn * rstd * g_ref[...]).astype(BF16)
        else:
            o_ref = refs[3]
        o_ref[...] = xn

    row = pl.BlockSpec((tm, D), lambda i: (i, 0))
    xs = jax.ShapeDtypeStruct((S, D), F32)
    if not with_norm:
        return pl.pallas_call(body, name="ple_fwd_last", grid=(S // tm,), in_specs=[row, row, row], out_specs=row,
                              out_shape=xs, compiler_params=_cparams(("parallel",)))(x, pre, e), None
    return pl.pallas_call(body, name="ple_fwd", grid=(S // tm,), in_specs=[row, row, row, pl.BlockSpec((1, D), lambda i: (0, 0))],
                          out_specs=(row, row), out_shape=(xs, jax.ShapeDtypeStruct((S, D), BF16)),
                          compiler_params=_cparams(("parallel",)))(x, pre, e, g_next.reshape(1, D))


def _ple_bwd(dx, pre, e):
    S, D = dx.shape
    tm = _rows(S)

    def body(dx_ref, p_ref, e_ref, dp_ref, de_ref):
        pg = jax.nn.sigmoid(p_ref[...])
        dxv = dx_ref[...]
        dp_ref[...] = (dxv * e_ref[...] * pg * (1.0 - pg)).astype(BF16)
        de_ref[...] = (dxv * pg).astype(BF16)

    row = pl.BlockSpec((tm, D), lambda i: (i, 0))
    act = jax.ShapeDtypeStruct((S, D), BF16)
    return pl.pallas_call(body, name="ple_bwd", grid=(S // tm,), in_specs=[row, row, row], out_specs=(row, row),
                          out_shape=(act, act), compiler_params=_cparams(("parallel",)))(dx, pre, e)


def _pad_heads(w, real):
    K = w.shape[0]
    w = w.reshape(K, HEADS, real)
    return jnp.pad(w, ((0, 0), (0, 0), (0, HEAD_PAD - real))).reshape(K, HEADS * HEAD_PAD)


def _unpad_heads(w, real):
    K = w.shape[0]
    return w.reshape(K, HEADS, HEAD_PAD)[:, :, :real].reshape(K, HEADS * real)


def _pad_head_rows(w, real):
    N = w.shape[1]
    w = w.reshape(HEADS, real, N)
    return jnp.pad(w, ((0, 0), (0, HEAD_PAD - real), (0, 0))).reshape(HEADS * HEAD_PAD, N)


def _unpad_head_rows(w, real):
    N = w.shape[1]
    return w.reshape(HEADS, HEAD_PAD, N)[:, :real].reshape(HEADS * real, N)


def _block_diag(w):
    w = w.reshape(4, 2, 64, 64)
    z = jnp.zeros((4, 64, 64), w.dtype)
    top = jnp.concatenate([w[:, 0], z], axis=2)
    bot = jnp.concatenate([z, w[:, 1]], axis=2)
    return jnp.concatenate([top, bot], axis=1)


def _block_diag_t(w):
    return jnp.stack([w[:, :64, :64], w[:, 64:, 64:]], axis=1).reshape(8, 64, 64)


_IN_SPLITS = (512, 512, 384, 288, 512, 512, 512, 8, 3072)
_IN_OFF = np.concatenate([[0], np.cumsum(_IN_SPLITS)])
_KR_OFF = 64
_SEG_NAMES = ("u", "ug", "cq", "ckv", "kr", "fq", "fk", "fv", "fl", "gate")


def _in_segments(w_in):
    c = lambda n: w_in[:, int(_IN_OFF[n]):int(_IN_OFF[n + 1])]
    kv = c(3)
    kr = jnp.pad(kv[:, MLA_KV_LORA:], ((0, 0), (_KR_OFF, LANES - _KR_OFF - MLA_ROPE)))
    fl = jnp.pad(c(7), ((0, 0), (0, LANES - HEADS)))
    fq = _pad_heads(c(4), FOX_HEAD_DIM) * jnp.asarray(FOX_SCALE, w_in.dtype)
    return [c(0), c(1), c(2), kv[:, :MLA_KV_LORA], kr, fq, _pad_heads(c(5), FOX_HEAD_DIM), _pad_heads(c(6), FOX_HEAD_DIM), fl, c(8)]


def _in_unsegment(dw_p, widths):
    offs = np.concatenate([[0], np.cumsum(widths)])
    seg = [dw_p[:, int(offs[n]):int(offs[n + 1])] for n in range(len(widths))]
    u, ug, cq, ckv, kr, fq, fk, fv, fl, gate = seg
    return jnp.concatenate([
        u, ug, cq, ckv, kr[:, _KR_OFF:_KR_OFF + MLA_ROPE], _unpad_heads(fq, FOX_HEAD_DIM) * FOX_SCALE,
        _unpad_heads(fk, FOX_HEAD_DIM), _unpad_heads(fv, FOX_HEAD_DIM), fl[:, :HEADS], gate], axis=1)


def _split_wuq(wuq):
    return _pad_heads(wuq, MLA_NOPE + MLA_ROPE)


def _split_wukv(wukv):
    w = wukv.reshape(MLA_KV_LORA, HEADS, MLA_NOPE + MLA_V)
    pad = lambda t: jnp.pad(t, ((0, 0), (0, 0), (0, HEAD_PAD - t.shape[2]))).reshape(MLA_KV_LORA, HEADS * HEAD_PAD)
    return pad(w[:, :, :MLA_NOPE]), pad(w[:, :, MLA_NOPE:])


def _merge_wukv(dk_p, dv_p):
    k = dk_p.reshape(MLA_KV_LORA, HEADS, HEAD_PAD)[:, :, :MLA_NOPE]
    v = dv_p.reshape(MLA_KV_LORA, HEADS, HEAD_PAD)[:, :, :MLA_V]
    return jnp.concatenate([k, v], axis=2).reshape(MLA_KV_LORA, HEADS * (MLA_NOPE + MLA_V))


def _heads_layout(d):
    S = d.shape[0]
    t = d[:, :HEADS].T
    return t.reshape(HEADS, 1, S), jnp.broadcast_to(t[:, :, None], (HEADS, S, LANES))


def _layer_fwd(x, h, p_i, w, g_next, tabs):
    c_q, c_k, s_lo, s_hi = tabs
    sv = {"x0": x}
    segs = _in_segments(w["w_in"])
    z = {}
    for nm, ws in zip(_SEG_NAMES, segs):
        z[nm] = _mm(h, ws, out_dtype=BF16 if nm in ("fq", "fk", "fv") else F32, bias=_ones_lane_bias() if nm == "fv" else None,
                    name="in_" + nm)
    sv.update(h=h, z=z)
    wa_bd, wx_bd = _block_diag(w["lru_wa"]).astype(BF16), _block_diag(w["lru_wx"]).astype(BF16)
    oa, xc, hs = _lru_fwd(z["u"], z["ug"], w["conv_w"], w["conv_b"], wa_bd, wx_bd, w["lru_ba"], w["lru_bx"], w["lru_lambda"])
    sv.update(oa=oa, xc=xc, hs=hs)
    qn = _rmsnorm_fwd(z["cq"], w["mla_q_norm"], "q_norm_fwd")
    kvn = _rmsnorm_fwd(z["ckv"], w["mla_kv_norm"], "kv_norm_fwd")
    wuq_p = _split_wuq(w["mla_wuq"])
    wk_p, wv_p = _split_wukv(w["mla_wukv"])
    qb = _rope_q(_mm(qn, wuq_p, name="mla_q"), c_q, s_lo, s_hi, transpose=False, out_dtype=BF16, name="rope_q")
    kb = _rope_k(_mm(kvn, wk_p, name="mla_k"), z["kr"], c_k, s_lo, s_hi)
    vb = _mm(kvn, wv_p, out_dtype=BF16, bias=_ones_lane_bias(), name="mla_v")
    ob, lse_b, lrow_b = _fa_fwd(qb, kb, vb, None, unit=64, name="mla_attn")
    sv.update(qn=qn, kvn=kvn, qb=qb, kb=kb, vb=vb, ob=ob, lse_b=lse_b, lrow_b=lrow_b)
    bf = jnp.pad(w["fox_bf"], (0, LANES - HEADS)).reshape(1, LANES)
    dec = _decay_fwd(z["fl"], bf)
    drow, drep = _heads_layout(dec)
    oc, lse_c, lrow_c = _fa_fwd(z["fq"], z["fk"], z["fv"], drow, unit=1, name="fox_attn")
    sv.update(drow=drow, drep=drep, oc=oc, lse_c=lse_c, lrow_c=lrow_c)
    ya = _mm(oa, w["w_br_a"], name="br_a")
    yb = _mm(ob, _pad_head_rows(w["w_br_b"], MLA_V), name="br_b")
    yc = _mm(oc, _pad_head_rows(w["w_br_c"], FOX_HEAD_DIM), name="br_c")
    merged = _merge_fwd(ya, yb, yc, z["gate"], w["gate_b"])
    x1, hn = _mm_res_norm(merged, w["w_o"], x, w["ffn_norm"], "w_o")
    sv.update(ya=ya, yb=yb, yc=yc, merged=merged, x1=x1)
    hf, act = _ffn_up(hn, _ffn_pair_columns(w["w_gate_up"]))
    x2, pn = _mm_res_norm(act, w["w_down"], x1, w["ple_norm"], "ffn_down")
    sv.update(hn=hn, hf=hf, act=act, x2=x2)
    pre = _mm(pn, w["w_ple_gate"], name="ple_gate")
    e = _mm(p_i, w["w_ple"], name="ple_embed")
    x3, h_next = _ple_fwd(x2, pre, e, g_next)
    sv.update(pn=pn, pre=pre, e=e, p_i=p_i)
    return x3, h_next, sv


def _layer_bwd(dx3, w, sv, tabs):
    c_q, c_k, s_lo, s_hi = tabs
    g = {}
    z = sv["z"]
    dpre, de = _ple_bwd(dx3, sv["pre"], sv["e"])
    g["w_ple"] = _mm(sv["p_i"], de, ta=True, name="d_w_ple")
    g["w_ple_gate"] = _mm(sv["pn"], dpre, ta=True, name="d_w_ple_gate")
    dpn = _mm(dpre, w["w_ple_gate"], tb=True, name="d_pn")
    dx2, g["ple_norm"] = _rmsnorm_bwd(sv["x2"], w["ple_norm"], dpn, add=dx3, name="ple_norm_bwd")
    g["w_down"] = _mm(sv["act"], dx2, ta=True, name="d_w_down")
    dhf = _ffn_down_bwd(dx2, w["w_down"], sv["hf"])
    g["w_gate_up"] = _ffn_unpair_columns(_mm(sv["hn"], dhf, ta=True, name="d_w_gate_up"))
    dhn = _mm(dhf, _ffn_pair_columns(w["w_gate_up"]), tb=True, name="d_hn")
    dx1, g["ffn_norm"] = _rmsnorm_bwd(sv["x1"], w["ffn_norm"], dhn, add=dx2, name="ffn_norm_bwd")
    g["w_o"] = _mm(sv["merged"], dx1, ta=True, name="d_w_o")
    dm = _mm(dx1, w["w_o"], tb=True, name="d_merged")
    dya, dyb, dyc, dgate, g["gate_b"] = _merge_bwd(dm, sv["ya"], sv["yb"], sv["yc"], z["gate"], w["gate_b"])
    wbb_p, wbc_p = _pad_head_rows(w["w_br_b"], MLA_V), _pad_head_rows(w["w_br_c"], FOX_HEAD_DIM)
    g["w_br_a"] = _mm(sv["oa"], dya, ta=True, name="d_w_br_a")
    g["w_br_b"] = _unpad_head_rows(_mm(sv["ob"], dyb, ta=True, name="d_w_br_b"), MLA_V)
    g["w_br_c"] = _unpad_head_rows(_mm(sv["oc"], dyc, ta=True, name="d_w_br_c"), FOX_HEAD_DIM)
    doa = _mm(dya, w["w_br_a"], tb=True, name="d_oa")
    dob = _mm(dyb, wbb_p, tb=True, out_dtype=BF16, name="d_ob")
    doc = _mm(dyc, wbc_p, tb=True, out_dtype=BF16, name="d_oc")
    dfq, dfk, dfv, d_dec = _fa_bwd(z["fq"], z["fk"], z["fv"], sv["oc"], sv["lse_c"], sv["lrow_c"], doc, sv["drow"], sv["drep"],
                                   unit=1, dq_dtype=BF16, dk_dtype=BF16, name="fox_attn_bwd")
    d_dec = jnp.pad(d_dec.T, ((0, 0), (0, LANES - HEADS)))
    bf = jnp.pad(w["fox_bf"], (0, LANES - HEADS)).reshape(1, LANES)
    dfl, dbf = _decay_bwd(d_dec, z["fl"], bf)
    g["fox_bf"] = dbf[0, :HEADS]
    dqb, dkb, dvb, _ = _fa_bwd(sv["qb"], sv["kb"], sv["vb"], sv["ob"], sv["lse_b"], sv["lrow_b"], dob, None, None,
                               unit=64, dq_dtype=F32, dk_dtype=F32, name="mla_attn_bwd")
    wuq_p = _split_wuq(w["mla_wuq"])
    wk_p, wv_p = _split_wukv(w["mla_wukv"])
    dq_pre = _rope_q(dqb, c_q, s_lo, s_hi, transpose=True, out_dtype=BF16, name="rope_q_bwd")
    dkr = _rope_k_bwd(dkb, c_k, s_lo, s_hi)
    g["mla_wuq"] = _unpad_heads(_mm(sv["qn"], dq_pre, ta=True, name="d_wuq"), MLA_NOPE + MLA_ROPE)
    g["mla_wukv"] = _merge_wukv(_mm(sv["kvn"], dkb, ta=True, name="d_wuk"), _mm(sv["kvn"], dvb, ta=True, name="d_wuv"))
    dqn = _mm(dq_pre, wuq_p, tb=True, name="d_qn")
    dkvn = _mm(dvb, wv_p, tb=True, res=_mm(dkb, wk_p, tb=True, name="d_kvn_k"), name="d_kvn")
    dcq, g["mla_q_norm"] = _rmsnorm_bwd(z["cq"], w["mla_q_norm"], dqn, out_dtype=BF16, name="q_norm_bwd")
    dckv, g["mla_kv_norm"] = _rmsnorm_bwd(z["ckv"], w["mla_kv_norm"], dkvn, out_dtype=BF16, name="kv_norm_bwd")
    wa_bd, wx_bd = _block_diag(w["lru_wa"]).astype(BF16), _block_diag(w["lru_wx"]).astype(BF16)
    du, dug, dcw, dcb, dba, dbx, dlam, dwa, dwx = _lru_bwd(
        doa, z["u"], z["ug"], sv["xc"], sv["hs"], w["conv_w"], wa_bd, wx_bd, w["lru_ba"], w["lru_bx"], w["lru_lambda"])
    g["conv_w"], g["conv_b"], g["lru_ba"], g["lru_bx"] = dcw, dcb[0], dba[0], dbx[0]
    g["lru_lambda"] = dlam[0] * LRU_C * jax.nn.sigmoid(-w["lru_lambda"])
    g["lru_wa"], g["lru_wx"] = _block_diag_t(dwa), _block_diag_t(dwx)
    dsegs = [du, dug, dcq, dckv, dkr, dfq, dfk, dfv, dfl, dgate]
    dz = jnp.concatenate(dsegs, axis=1)
    w_in_p = jnp.concatenate(_in_segments(w["w_in"]), axis=1)
    g["w_in"] = _in_unsegment(_mm(sv["h"], dz, ta=True, name="d_w_in"), [d.shape[1] for d in dsegs])
    dh = _mm(dz, w_in_p, tb=True, name="d_h")
    dx0, g["mix_norm"] = _rmsnorm_bwd(sv["x0"], w["mix_norm"], dh, add=dx1, name="mix_norm_bwd")
    return dx0, g


_LAYER_WEIGHTS = ("mix_norm", "w_in", "gate_b", "conv_w", "conv_b", "lru_wa", "lru_ba", "lru_wx", "lru_bx", "lru_lambda",
                  "mla_q_norm", "mla_wuq", "mla_kv_norm", "mla_wukv", "fox_bf", "w_br_a", "w_br_b", "w_br_c", "w_o",
                  "ffn_norm", "w_gate_up", "w_down", "ple_norm", "w_ple_gate", "w_ple")
_BIG = ("w_in", "mla_wuq", "mla_wukv", "w_br_a", "w_br_b", "w_br_c", "w_o", "w_gate_up", "w_down", "w_ple_gate", "w_ple")
_ROW_SHARDED = ("w_o", "w_down", "w_ple_gate")
_SMALL = ("mix_norm", "gate_b", "conv_b", "lru_wa", "lru_ba", "lru_wx", "lru_bx", "lru_lambda", "mla_q_norm", "mla_kv_norm",
          "fox_bf", "ffn_norm", "ple_norm")


def _local_step(x, p, layers, final_norm, target):
    tabs = _rope_tables(x.shape[0])
    saved = []
    h = _rmsnorm_fwd(x, layers[0]["mix_norm"], "mix_norm_fwd")
    for i in range(DEPTH):
        g_next = layers[i + 1]["mix_norm"] if i + 1 < DEPTH else None
        x, h, sv = _layer_fwd(x, h, p[i], layers[i], g_next, tabs)
        saved.append(sv)
    loss, dx, d_final = _loss_head(x, final_norm, target)
    grads = [None] * DEPTH
    for i in reversed(range(DEPTH)):
        dx, grads[i] = _layer_bwd(dx, layers[i], saved[i], tabs)
    return loss, dx, grads, d_final


def _hbm():
    return pl.BlockSpec(memory_space=pltpu.HBM)


def _peers(x, y):
    return [(1 - x, y), (x, 1 - y), (1 - x, 1 - y)]


def _gather_chips_two_level(shard, name):
    R, W = shard.shape
    Rh = R // 2

    def body(src_ref, out_ref, send_sems, recv_sems):
        x, y, c = lax.axis_index("x"), lax.axis_index("y"), lax.axis_index("c")
        me = 2 * x + y
        mine, other = pl.ds(c * Rh, Rh), pl.ds((1 - c) * Rh, Rh)
        peers = _peers(x, y)

        def copy(j, src, slot, rows, to):
            return pltpu.make_async_remote_copy(src_ref=src, dst_ref=out_ref.at[slot, rows], send_sem=send_sems.at[j],
                                                recv_sem=recv_sems.at[j], device_id=to, device_id_type=MESH)

        first = [copy(j, src_ref.at[mine], me, mine, (px, py, c)) for j, (px, py) in enumerate(peers)]
        for cp in first:
            cp.start()
        passed = []
        for j, (px, py) in enumerate(peers):
            slot = 2 * px + py
            copy(j, src_ref.at[mine], slot, mine, (px, py, c)).wait_recv()
            cp = copy(3 + j, out_ref.at[slot, mine], slot, mine, (x, y, 1 - c))
            cp.start()
            passed.append(cp)
        for j, (px, py) in enumerate(peers):
            copy(3 + j, src_ref.at[other], 2 * px + py, other, (x, y, 1 - c)).wait_recv()
        for cp in first + passed:
            cp.wait_send()

    return pl.pallas_call(
        body, name=name, in_specs=[_hbm()], out_specs=_hbm(), out_shape=jax.ShapeDtypeStruct((4, R, W), shard.dtype),
        scratch_shapes=[pltpu.SemaphoreType.DMA((6,)), pltpu.SemaphoreType.DMA((6,))],
    )(shard)


def _gather_chips(shard, name):
    R, W = shard.shape

    def body(src_ref, out_ref, send_sems, recv_sems, local_sem):
        x, y, c = lax.axis_index("x"), lax.axis_index("y"), lax.axis_index("c")
        me = 2 * x + y
        mine = pltpu.make_async_copy(src_ref, out_ref.at[me], local_sem)
        mine.start()

        def copy(j, slot, to):
            return pltpu.make_async_remote_copy(src_ref=src_ref, dst_ref=out_ref.at[slot], send_sem=send_sems.at[j],
                                                recv_sem=recv_sems.at[j], device_id=(to[0], to[1], c), device_id_type=MESH)

        sends = [copy(j, me, peer) for j, peer in enumerate(_peers(x, y))]
        for cp in sends:
            cp.start()
        for j, peer in enumerate(_peers(x, y)):
            copy(j, 2 * peer[0] + peer[1], peer).wait_recv()
        for cp in sends:
            cp.wait_send()
        mine.wait()

    return pl.pallas_call(
        body, name=name, in_specs=[_hbm()], out_specs=_hbm(), out_shape=jax.ShapeDtypeStruct((4, R, W), shard.dtype),
        scratch_shapes=[pltpu.SemaphoreType.DMA((3,)), pltpu.SemaphoreType.DMA((3,)), pltpu.SemaphoreType.DMA],
    )(shard)


def _pair_swap_halves(g4):
    n, R, W = g4.shape
    Rh = R // 2

    def body(src_ref, out_ref, send_sem, recv_sem):
        x, y, c = lax.axis_index("x"), lax.axis_index("y"), lax.axis_index("c")
        cp = pltpu.make_async_remote_copy(src_ref=src_ref.at[:, pl.ds((1 - c) * Rh, Rh), :], dst_ref=out_ref, send_sem=send_sem,
                                          recv_sem=recv_sem, device_id=(x, y, 1 - c), device_id_type=MESH)
        cp.start()
        cp.wait()

    return pl.pallas_call(
        body, name="grad_pair_swap", in_specs=[_hbm()], out_specs=_hbm(), out_shape=jax.ShapeDtypeStruct((n, Rh, W), g4.dtype),
        scratch_shapes=[pltpu.SemaphoreType.DMA, pltpu.SemaphoreType.DMA],
    )(g4)


def _pair_add(g4, sib, c_arr):
    n, R, W = g4.shape
    Rh = R // 2
    tr = _tile_rows(Rh)
    nb = Rh // tr

    def body(c_ref, a_ref, b_ref, o_ref):
        o_ref[...] = (a_ref[...].astype(F32) + b_ref[...].astype(F32)).astype(o_ref.dtype)

    return pl.pallas_call(
        body, name="grad_pair_add",
        grid_spec=pltpu.PrefetchScalarGridSpec(
            num_scalar_prefetch=1, grid=(n, nb),
            in_specs=[pl.BlockSpec((None, tr, W), lambda s, i, c: (s, c[0] * nb + i, 0)), pl.BlockSpec((None, tr, W), lambda s, i, c: (s, i, 0))],
            out_specs=pl.BlockSpec((None, tr, W), lambda s, i, c: (s, i, 0))),
        out_shape=jax.ShapeDtypeStruct((n, Rh, W), g4.dtype), compiler_params=_cparams(("parallel", "parallel")),
    )(c_arr, g4, sib)


def _tile_rows(n):
    for t in (512, 480, 400, 320, 256, 240, 160, 128, 80, 64, 40, 32, 16, 8):
        if n % t == 0:
            return t
    return n


def _chips_exchange(part):
    n, Rh, W = part.shape

    def body(src_ref, out_ref, send_sems, recv_sems):
        x, y, c = lax.axis_index("x"), lax.axis_index("y"), lax.axis_index("c")

        def copy(j, to):
            return pltpu.make_async_remote_copy(src_ref=src_ref.at[2 * to[0] + to[1]], dst_ref=out_ref.at[j], send_sem=send_sems.at[j],
                                                recv_sem=recv_sems.at[j], device_id=(to[0], to[1], c), device_id_type=MESH)

        cps = [copy(j, peer) for j, peer in enumerate(_peers(x, y))]
        for cp in cps:
            cp.start()
        for cp in cps:
            cp.wait()

    return pl.pallas_call(
        body, name="grad_chips_exchange", in_specs=[_hbm()], out_specs=_hbm(), out_shape=jax.ShapeDtypeStruct((3, Rh, W), part.dtype),
        scratch_shapes=[pltpu.SemaphoreType.DMA((3,)), pltpu.SemaphoreType.DMA((3,))],
    )(part)


def _chips_add(part, got, k_arr, c_arr):
    n, Rh, W = part.shape
    tr = _tile_rows(Rh)
    nb = Rh // tr

    def body(k_ref, c_ref, a_ref, b_ref, o_ref):
        mine = pl.program_id(0) == c_ref[0]

        @pl.when(mine)
        def _():
            o_ref[...] = ((a_ref[...].astype(F32) + b_ref[0].astype(F32)) + b_ref[1].astype(F32)) + b_ref[2].astype(F32)

        @pl.when(jnp.logical_not(mine))
        def _():
            o_ref[...] = jnp.zeros_like(o_ref)

    return pl.pallas_call(
        body, name="grad_chips_add",
        grid_spec=pltpu.PrefetchScalarGridSpec(
            num_scalar_prefetch=2, grid=(2, nb),
            in_specs=[pl.BlockSpec((None, tr, W), lambda h, i, k, c: (k[0], i, 0)), pl.BlockSpec((3, tr, W), lambda h, i, k, c: (0, i, 0))],
            out_specs=pl.BlockSpec((tr, W), lambda h, i, k, c: (h * nb + i, 0))),
        out_shape=jax.ShapeDtypeStruct((2 * Rh, W), F32), compiler_params=_cparams(("parallel", "parallel")),
    )(k_arr, c_arr, part, got)


def _pair_gather(buf):
    R, W = buf.shape
    Rh = R // 2

    def body(src_ref, out_ref, send_sem, recv_sem):
        x, y, c = lax.axis_index("x"), lax.axis_index("y"), lax.axis_index("c")
        mine, other = pl.ds(c * Rh, Rh), pl.ds((1 - c) * Rh, Rh)
        pltpu.make_async_remote_copy(src_ref=src_ref.at[mine], dst_ref=out_ref.at[mine], send_sem=send_sem, recv_sem=recv_sem,
                                     device_id=(x, y, 1 - c), device_id_type=MESH).start()
        pltpu.make_async_remote_copy(src_ref=src_ref.at[mine], dst_ref=out_ref.at[other], send_sem=send_sem, recv_sem=recv_sem,
                                     device_id=(x, y, 1 - c), device_id_type=MESH).wait()

    return pl.pallas_call(
        body, name="grad_pair_gather", in_specs=[_hbm()], out_specs=_hbm(), out_shape=jax.ShapeDtypeStruct((R, W), buf.dtype),
        input_output_aliases={0: 0}, scratch_shapes=[pltpu.SemaphoreType.DMA, pltpu.SemaphoreType.DMA],
    )(buf)


def _gather_all(buf):
    R, W = buf.shape

    def body(src_ref, out_ref, send_sems, recv_sems, local_sem):
        x, y, c = lax.axis_index("x"), lax.axis_index("y"), lax.axis_index("c")
        me = 4 * x + 2 * y + c
        mine = pltpu.make_async_copy(src_ref, out_ref.at[me], local_sem)
        mine.start()
        rel = [((x + (r >> 2 & 1)) % 2, (y + (r >> 1 & 1)) % 2, (c + (r & 1)) % 2) for r in range(1, 8)]

        def copy(j, slot, to):
            return pltpu.make_async_remote_copy(src_ref=src_ref, dst_ref=out_ref.at[slot], send_sem=send_sems.at[j],
                                                recv_sem=recv_sems.at[j], device_id=to, device_id_type=MESH)

        sends = [copy(j, me, to) for j, to in enumerate(rel)]
        for cp in sends:
            cp.start()
        for j, to in enumerate(rel):
            copy(j, 4 * to[0] + 2 * to[1] + to[2], to).wait_recv()
        for cp in sends:
            cp.wait_send()
        mine.wait()

    return pl.pallas_call(
        body, name="small_gather", in_specs=[_hbm()], out_specs=_hbm(), out_shape=jax.ShapeDtypeStruct((8, R, W), buf.dtype),
        scratch_shapes=[pltpu.SemaphoreType.DMA((7,)), pltpu.SemaphoreType.DMA((7,)), pltpu.SemaphoreType.DMA],
    )(buf)


def _sum_slots(stack):
    n, R, W = stack.shape
    tr = _tile_rows(R)

    def body(s_ref, o_ref):
        tot = s_ref[0]
        for j in range(1, n):
            tot = tot + s_ref[j]
        o_ref[...] = tot

    return pl.pallas_call(
        body, name="small_sum", grid=(R // tr,), in_specs=[pl.BlockSpec((n, tr, W), lambda i: (0, i, 0))],
        out_specs=pl.BlockSpec((tr, W), lambda i: (i, 0)), out_shape=jax.ShapeDtypeStruct((R, W), F32),
        compiler_params=_cparams(("parallel",)),
    )(stack)


def _adamw(wp, gp, mp, vp, name):
    R, W = wp.shape
    tr = R
    for t in (1024, 512, 256, 128, 64, 32, 16, 8):
        if R % t == 0 and t * W <= 512 * 1024:
            tr = t
            break
    c1 = 1.0 - ADAM_B1 ** ADAM_STEP
    c2 = 1.0 - ADAM_B2 ** ADAM_STEP

    def body(w_ref, g_ref, m_ref, v_ref, d_ref, mo_ref, vo_ref):
        gv = g_ref[...]
        m = ADAM_B1 * m_ref[...] + (1.0 - ADAM_B1) * gv
        v = ADAM_B2 * v_ref[...] + (1.0 - ADAM_B2) * (gv * gv)
        m_hat = m / c1
        v_hat = v / c2
        d_ref[...] = -ADAM_LR * (m_hat / (jnp.sqrt(v_hat) + ADAM_EPS) + ADAM_WD * w_ref[...])
        mo_ref[...] = m
        vo_ref[...] = v

    blk = pl.BlockSpec((tr, W), lambda i: (i, 0))
    shp = jax.ShapeDtypeStruct((R, W), F32)
    return pl.pallas_call(body, name=name, grid=(R // tr,), in_specs=[blk] * 4, out_specs=(blk,) * 3, out_shape=(shp,) * 3,
                          compiler_params=_cparams(("parallel",)))(wp, gp, mp, vp)


def _pack(arrs, rows):
    flat = jnp.concatenate([a.reshape(-1) for a in arrs])
    return jnp.pad(flat, (0, rows * PACK_W - flat.shape[0])).reshape(rows, PACK_W)


def _unpack(buf, shapes):
    flat = buf.reshape(-1)
    out, off = [], 0
    for shp in shapes:
        n = int(np.prod(shp))
        out.append(flat[off:off + n].reshape(shp))
        off += n
    return out


def _rows_for(shapes, mult):
    n = sum(int(np.prod(s)) for s in shapes)
    rows = -(-n // PACK_W)
    return -(-rows // mult) * mult


def _shard_major(g, name):
    L, K, N = g.shape
    if name in _ROW_SHARDED:
        t = g.reshape(L, 4, K // 4, N).transpose(1, 0, 2, 3)
    else:
        t = g.reshape(L, K, 4, N // 4).transpose(2, 0, 1, 3)
    return t.reshape(4, -1, PACK_W)


def _join_shards(blocks, name):
    return jnp.concatenate(blocks, axis=1 if name in _ROW_SHARDED else 2)


def kernel(x, p, mix_norm, w_in, gate_b, conv_w, conv_b, lru_wa, lru_ba, lru_wx, lru_bx, lru_lambda, mla_q_norm, mla_wuq, mla_kv_norm, mla_wukv, fox_bf, w_br_a, w_br_b, w_br_c, w_o, ffn_norm, w_gate_up, w_down, ple_norm, w_ple_gate, w_ple, final_norm, loss_target, m_mix_norm, m_w_in, m_gate_b, m_conv_w, m_conv_b, m_lru_wa, m_lru_ba, m_lru_wx, m_lru_bx, m_lru_lambda, m_mla_q_norm, m_mla_wuq, m_mla_kv_norm, m_mla_wukv, m_fox_bf, m_w_br_a, m_w_br_b, m_w_br_c, m_w_o, m_ffn_norm, m_w_gate_up, m_w_down, m_ple_norm, m_w_ple_gate, m_w_ple, m_final_norm, v_mix_norm, v_w_in, v_gate_b, v_conv_w, v_conv_b, v_lru_wa, v_lru_ba, v_lru_wx, v_lru_bx, v_lru_lambda, v_mla_q_norm, v_mla_wuq, v_mla_kv_norm, v_mla_wukv, v_fox_bf, v_w_br_a, v_w_br_b, v_w_br_c, v_w_o, v_ffn_norm, v_w_gate_up, v_w_down, v_ple_norm, v_w_ple_gate, v_w_ple, v_final_norm):
    a = dict(locals())
    names = list(_LAYER_WEIGHTS) + ["final_norm"]
    W = {n: a[n] for n in names}
    M = {n: a["m_" + n] for n in names}
    V = {n: a["v_" + n] for n in names}
    ix, iy, ic = lax.axis_index("x"), lax.axis_index("y"), lax.axis_index("c")

    sharded = list(_BIG) + ["conv_w"]
    shard_shapes = [W[n].shape for n in sharded]
    R = _rows_for(shard_shapes, 64)
    w_bf = _pack([W[n].astype(BF16) for n in sharded], R)
    gathered = _gather_chips_two_level(w_bf, "weight_gather")
    gathered = lax.dynamic_update_slice(gathered, w_bf[None], (2 * ix + iy, 0, 0))
    per_chip = [_unpack(gathered[k], shard_shapes) for k in range(4)]
    full = {n: _join_shards([per_chip[k][j] for k in range(4)], n) for j, n in enumerate(_BIG)}
    conv_blocks = _gather_chips(conv_w.reshape(DEPTH * CONV_WIDTH, LANES), "conv_w_gather")
    conv_w_full = jnp.concatenate([conv_blocks[k].reshape(DEPTH, CONV_WIDTH, LANES) for k in range(4)], axis=-1)
    layers = []
    for i in range(DEPTH):
        lw = {n: W[n][i] for n in _SMALL}
        for n in _BIG:
            lw[n] = full[n][i]
        lw["conv_w"] = conv_w_full[i]
        layers.append(lw)

    loss_sum, dx, grads, d_final = _local_step(x[0], p[:, 0], layers, final_norm, loss_target[0])
    loss = lax.psum(loss_sum, ("x", "y", "c"))

    parts = [_shard_major(jnp.stack([grads[i][n] for i in range(DEPTH)]), n).astype(BF16) for n in sharded]
    g4, off = jnp.zeros((4, R, PACK_W), BF16), 0
    for t in parts:
        g4 = lax.dynamic_update_slice(g4, t, (0, off, 0))
        off += t.shape[1]
    c_arr = jnp.reshape(ic, (1,)).astype(jnp.int32)
    k_arr = jnp.reshape(2 * ix + iy, (1,)).astype(jnp.int32)
    pair = _pair_add(g4, _pair_swap_halves(g4), c_arr)
    g_pack = _pair_gather(_chips_add(pair, _chips_exchange(pair), k_arr, c_arr))
    big_out = {}
    for n, gsh in zip(sharded, _unpack(g_pack, shard_shapes)):
        view = lambda t: t.reshape(-1, t.shape[-1])
        d, nm, nv = _adamw(view(W[n]), view(gsh), view(M[n]), view(V[n]), "adamw_" + n)
        for key, arr in (("g", gsh), ("d", d), ("m", nm), ("v", nv)):
            big_out[(key, n)] = arr.reshape(W[n].shape)

    pick = lambda src, n, i: src[n] if i is None else src[n][i]
    small = [(n, i) for i in range(DEPTH) for n in _SMALL] + [("final_norm", None)]
    small_shapes = [pick(W, n, i).shape for n, i in small]
    Rs = _rows_for(small_shapes, 8)
    sg = _pack([d_final if i is None else grads[i][n] for n, i in small], Rs)
    sg = _sum_slots(_gather_all(sg))
    sw = _pack([pick(W, n, i) for n, i in small], Rs)
    sm = _pack([pick(M, n, i) for n, i in small], Rs)
    sv_ = _pack([pick(V, n, i) for n, i in small], Rs)
    sd, snm, snv = _adamw(sw, sg, sm, sv_, "adamw_replicated")
    small_out = {}
    for key, buf in (("g", sg), ("d", sd), ("m", snm), ("v", snv)):
        for (n, i), arr in zip(small, _unpack(buf, small_shapes)):
            small_out[(key, n, i)] = arr

    def assemble(key, n):
        if n == "final_norm":
            return small_out[(key, n, None)]
        if n in sharded:
            return big_out[(key, n)]
        return jnp.stack([small_out[(key, n, i)] for i in range(DEPTH)])

    outs = [loss, dx[None]]
    for key in ("g", "d", "m", "v"):
        outs += [assemble(key, n) for n in names]
    return tuple(outs)
```

```python
import functools
import math

import numpy as np
import jax
import jax.numpy as jnp
from jax import lax
from jax.experimental import pallas as pl
from jax.experimental.pallas import tpu as pltpu

F32, BF16 = jnp.float32, jnp.bfloat16
MESH = pl.DeviceIdType.MESH

D_MODEL = 1024
DEPTH = 2
EPS = 1e-6
NEG_INF = -1e30
LRU_WIDTH = 512
LRU_HEADS = 8
LRU_C = 8.0
CONV_WIDTH = 4
HEADS = 8
MLA_Q_LORA = 384
MLA_KV_LORA = 256
MLA_NOPE = 64
MLA_ROPE = 32
MLA_V = 64
ROPE_BASE = 10000.0
FOX_HEAD_DIM = 64
D_FF = 2816
PLE_DIM = 256
HEAD_PAD = 128
MLA_SCALE = (MLA_NOPE + MLA_ROPE) ** -0.5
FOX_SCALE = FOX_HEAD_DIM ** -0.5

ADAM_LR, ADAM_B1, ADAM_B2, ADAM_EPS, ADAM_WD, ADAM_STEP = 0.001, 0.9, 0.999, 1e-08, 0.01, 10

VMEM_LIMIT_BYTES = 48 * 1024 * 1024
LANES = 128
PACK_W = 1024

ROW_TILE = 512
ATTN_TILE = 1024
LRU_CHUNK = 512


def _cparams(dims):
    return pltpu.CompilerParams(dimension_semantics=dims, vmem_limit_bytes=VMEM_LIMIT_BYTES)


def _tile(n, cap):
    if n <= cap:
        return n
    t = (cap // LANES) * LANES
    while t >= LANES:
        if n % t == 0:
            return t
        t -= LANES
    raise ValueError(f"no tile for {n} under {cap}")


def _rows(n):
    return min(ROW_TILE, n)


MM_VMEM_BUDGET = 36 * 1024 * 1024


def _mm_tiles(M, N, K, a_bytes, b_bytes, o_bytes, has_res):
    best, best_work = None, 0
    for tm in {_tile(M, c) for c in (1024, 512, 256)}:
        for tn in {_tile(N, c) for c in (1792, 1024, 512)}:
            for tk in {_tile(K, c) for c in (2048, 1408, 1024, 512)}:
                need = 2 * (tm * tk * a_bytes + tk * tn * b_bytes + tm * tn * o_bytes + (tm * tn * 4 if has_res else 0))
                need += tm * tn * 4 if tk < K else 0
                need += tm * tn * 4
                if need <= MM_VMEM_BUDGET and tm * tn * tk > best_work:
                    best, best_work = (tm, tn, tk), tm * tn * tk
    assert best is not None, (M, N, K)
    return best

def _mm(a, b, *, ta=False, tb=False, out_dtype=F32, res=None, bias=None, name):
    K, M = a.shape if ta else a.shape[::-1]
    N, K2 = b.shape if tb else b.shape[::-1]
    assert K == K2, (name, a.shape, b.shape)
    assert res is None or bias is None
    tm, tn, tk = _mm_tiles(M, N, K, a.dtype.itemsize, b.dtype.itemsize, jnp.dtype(out_dtype).itemsize, res is not None)
    nk = K // tk
    a_spec = pl.BlockSpec((tk, tm), lambda i, j, k: (k, i)) if ta else pl.BlockSpec((tm, tk), lambda i, j, k: (i, k))
    b_spec = pl.BlockSpec((tn, tk), lambda i, j, k: (j, k)) if tb else pl.BlockSpec((tk, tn), lambda i, j, k: (k, j))
    o_spec = pl.BlockSpec((tm, tn), lambda i, j, k: (i, j))
    dn = (((0,) if ta else (1,), (1,) if tb else (0,)), ((), ()))
    if bias is not None:
        res, r_spec = bias, pl.BlockSpec((1, tn), lambda i, j, k: (0, j))
    else:
        r_spec = o_spec
    has_res = res is not None

    def body(*refs):
        a_ref, b_ref = refs[0], refs[1]
        r_ref = refs[2] if has_res else None
        o_ref = refs[3] if has_res else refs[2]
        av, bv = a_ref[...], b_ref[...]
        if av.dtype != BF16:
            av = av.astype(BF16)
        if bv.dtype != BF16:
            bv = bv.astype(BF16)
        part = lax.dot_general(av, bv, dn, preferred_element_type=F32)

        def finish(total):
            if has_res:
                total = total + r_ref[...]
            o_ref[...] = total.astype(out_dtype)

        if nk == 1:
            finish(part)
        else:
            acc = refs[-1]
            k = pl.program_id(2)

            @pl.when(k == 0)
            def _():
                acc[...] = part

            @pl.when(k > 0)
            def _():
                acc[...] += part

            @pl.when(k == nk - 1)
            def _():
                finish(acc[...])

    in_specs = [a_spec, b_spec] + ([r_spec] if has_res else [])
    args = (a, b) + ((res,) if has_res else ())
    return pl.pallas_call(
        body, name=name, grid=(M // tm, N // tn, nk), in_specs=in_specs, out_specs=o_spec,
        out_shape=jax.ShapeDtypeStruct((M, N), out_dtype),
        scratch_shapes=[pltpu.VMEM((tm, tn), F32)] if nk > 1 else [],
        compiler_params=_cparams(("parallel", "parallel", "arbitrary")),
    )(*args)


def _mm_res_norm(a, b, res, g, name):
    M, K = a.shape
    N = b.shape[1]
    tm, tk = _tile(M, 512), _tile(K, 1408)
    nk = K // tk

    def body(a_ref, b_ref, r_ref, g_ref, o_ref, h_ref, *scratch):
        part = jnp.dot(a_ref[...], b_ref[...], preferred_element_type=F32)

        def finish(total):
            xn = total + r_ref[...]
            o_ref[...] = xn
            rstd = lax.rsqrt(jnp.mean(xn * xn, axis=1, keepdims=True) + EPS)
            h_ref[...] = (xn * rstd * g_ref[...]).astype(BF16)

        if nk == 1:
            finish(part)
        else:
            acc = scratch[0]
            k = pl.program_id(1)

            @pl.when(k == 0)
            def _():
                acc[...] = part

            @pl.when(k > 0)
            def _():
                acc[...] += part

            @pl.when(k == nk - 1)
            def _():
                finish(acc[...])

    row = pl.BlockSpec((tm, N), lambda i, k: (i, 0))
    return pl.pallas_call(
        body, name=name, grid=(M // tm, nk),
        in_specs=[pl.BlockSpec((tm, tk), lambda i, k: (i, k)), pl.BlockSpec((tk, N), lambda i, k: (k, 0)), row,
                  pl.BlockSpec((1, N), lambda i, k: (0, 0))],
        out_specs=(row, row), out_shape=(jax.ShapeDtypeStruct((M, N), F32), jax.ShapeDtypeStruct((M, N), BF16)),
        scratch_shapes=[pltpu.VMEM((tm, N), F32)] if nk > 1 else [],
        compiler_params=_cparams(("parallel", "arbitrary")),
    )(a, b, res, g.reshape(1, N))


FFN_TILE = 1408


def _ffn_pair_columns(w_gate_up):
    F = w_gate_up.shape[-1] // 2
    parts = []
    for j in range(F // FFN_TILE):
        parts += [w_gate_up[..., j * FFN_TILE:(j + 1) * FFN_TILE], w_gate_up[..., F + j * FFN_TILE:F + (j + 1) * FFN_TILE]]
    return jnp.concatenate(parts, axis=-1)


def _ffn_unpair_columns(dw):
    F = dw.shape[-1] // 2
    n = F // FFN_TILE
    blk = [dw[..., j * FFN_TILE:(j + 1) * FFN_TILE] for j in range(2 * n)]
    return jnp.concatenate(blk[0::2] + blk[1::2], axis=-1)


def _ffn_up(hn, w_pair):
    S, D = hn.shape
    W2 = w_pair.shape[1]
    F, tf = W2 // 2, FFN_TILE
    tm = _rows(S)

    def body(h_ref, w_ref, hf_ref, act_ref):
        hf = jnp.dot(h_ref[...], w_ref[...], preferred_element_type=F32)
        hf_ref[...] = hf
        gt, up = hf[:, 0:tf], hf[:, tf:2 * tf]
        act_ref[...] = (gt * jax.nn.sigmoid(gt) * up).astype(BF16)

    return pl.pallas_call(
        body, name="ffn_up", grid=(S // tm, F // tf),
        in_specs=[pl.BlockSpec((tm, D), lambda i, j: (i, 0)), pl.BlockSpec((D, 2 * tf), lambda i, j: (0, j))],
        out_specs=(pl.BlockSpec((tm, 2 * tf), lambda i, j: (i, j)), pl.BlockSpec((tm, tf), lambda i, j: (i, j))),
        out_shape=(jax.ShapeDtypeStruct((S, W2), F32), jax.ShapeDtypeStruct((S, F), BF16)),
        compiler_params=_cparams(("parallel", "parallel")),
    )(hn, w_pair)


def _ffn_down_bwd(dx, w_down, hf):
    S, D = dx.shape
    F, tf = w_down.shape[0], FFN_TILE
    tm = _rows(S)

    def body(d_ref, w_ref, h_ref, o_ref):
        dact = lax.dot_general(d_ref[...].astype(BF16), w_ref[...], _NT, preferred_element_type=F32)
        gt, up = h_ref[:, 0:tf], h_ref[:, tf:2 * tf]
        sg = jax.nn.sigmoid(gt)
        o_ref[:, 0:tf] = (dact * up * sg * (1.0 + gt * (1.0 - sg))).astype(BF16)
        o_ref[:, tf:2 * tf] = (dact * gt * sg).astype(BF16)

    pair = pl.BlockSpec((tm, 2 * tf), lambda i, j: (i, j))
    return pl.pallas_call(
        body, name="ffn_down_bwd", grid=(S // tm, F // tf),
        in_specs=[pl.BlockSpec((tm, D), lambda i, j: (i, 0)), pl.BlockSpec((tf, D), lambda i, j: (j, 0)), pair],
        out_specs=pair, out_shape=jax.ShapeDtypeStruct((S, 2 * F), BF16),
        compiler_params=_cparams(("parallel", "parallel")),
    )(dx, w_down, hf)


def _rmsnorm_fwd(x, g, name):
    S, W = x.shape
    tm = _rows(S)

    def body(x_ref, g_ref, o_ref):
        xf = x_ref[...]
        rstd = lax.rsqrt(jnp.mean(xf * xf, axis=1, keepdims=True) + EPS)
        o_ref[...] = (xf * rstd * g_ref[...]).astype(BF16)

    return pl.pallas_call(
        body, name=name, grid=(S // tm,),
        in_specs=[pl.BlockSpec((tm, W), lambda i: (i, 0)), pl.BlockSpec((1, W), lambda i: (0, 0))],
        out_specs=pl.BlockSpec((tm, W), lambda i: (i, 0)),
        out_shape=jax.ShapeDtypeStruct((S, W), BF16), compiler_params=_cparams(("parallel",)),
    )(x, g.reshape(1, W))


def _rmsnorm_bwd(x, g, dy, *, add=None, out_dtype=F32, name):
    S, W = x.shape
    tm = _rows(S)
    has_add = add is not None

    def body(*refs):
        x_ref, g_ref, dy_ref = refs[:3]
        add_ref = refs[3] if has_add else None
        dx_ref, dg_ref = refs[-2], refs[-1]
        xf = x_ref[...]
        rstd = lax.rsqrt(jnp.mean(xf * xf, axis=1, keepdims=True) + EPS)
        xhat = xf * rstd
        dyv = dy_ref[...]
        dxh = dyv * g_ref[...]
        dx = rstd * (dxh - xhat * jnp.mean(dxh * xhat, axis=1, keepdims=True))
        if has_add:
            dx = dx + add_ref[...]
        dx_ref[...] = dx.astype(out_dtype)

        @pl.when(pl.program_id(0) == 0)
        def _():
            dg_ref[...] = jnp.zeros_like(dg_ref)

        dg_ref[...] += jnp.sum(dyv * xhat, axis=0, keepdims=True)

    row = pl.BlockSpec((tm, W), lambda i: (i, 0))
    vec = pl.BlockSpec((1, W), lambda i: (0, 0))
    dx, dg = pl.pallas_call(
        body, name=name, grid=(S // tm,),
        in_specs=[row, vec, row] + ([row] if has_add else []),
        out_specs=(row, vec),
        out_shape=(jax.ShapeDtypeStruct((S, W), out_dtype), jax.ShapeDtypeStruct((1, W), F32)),
        compiler_params=_cparams(("arbitrary",)),
    )(x, g.reshape(1, W), dy, *((add,) if has_add else ()))
    return dx, dg.reshape(W)


def _loss_head(x, g, target):
    S, W = x.shape
    tm = _rows(S)

    def body(x_ref, g_ref, t_ref, loss_ref, dx_ref, dg_ref):
        xf = x_ref[...]
        gv = g_ref[...]
        rstd = lax.rsqrt(jnp.mean(xf * xf, axis=1, keepdims=True) + EPS)
        xhat = xf * rstd
        err = xhat * gv - t_ref[...]
        part = 0.5 * jnp.sum(jnp.mean(err * err, axis=1, keepdims=True), axis=0, keepdims=True)
        dyv = err * (1.0 / W)
        dxh = dyv * gv
        dx_ref[...] = rstd * (dxh - xhat * jnp.mean(dxh * xhat, axis=1, keepdims=True))

        @pl.when(pl.program_id(0) == 0)
        def _():
            dg_ref[...] = jnp.zeros_like(dg_ref)
            loss_ref[...] = jnp.zeros_like(loss_ref)

        dg_ref[...] += jnp.sum(dyv * xhat, axis=0, keepdims=True)
        loss_ref[...] += part

    row = pl.BlockSpec((tm, W), lambda i: (i, 0))
    vec = pl.BlockSpec((1, W), lambda i: (0, 0))
    loss, dx, dg = pl.pallas_call(
        body, name="loss_head", grid=(S // tm,), in_specs=[row, vec, row],
        out_specs=(pl.BlockSpec((1, 1), lambda i: (0, 0)), row, vec),
        out_shape=(jax.ShapeDtypeStruct((1, 1), F32), jax.ShapeDtypeStruct((S, W), F32), jax.ShapeDtypeStruct((1, W), F32)),
        compiler_params=_cparams(("arbitrary",)),
    )(x, g.reshape(1, W), target)
    return loss[0, 0], dx, dg.reshape(W)


def _scan_fwd(a, b, row):
    T = a.shape[0]
    d = 1
    while d < T:
        keep = row >= d
        b = jnp.where(keep, a * pltpu.roll(b, d, axis=0) + b, b)
        a = jnp.where(keep, a * pltpu.roll(a, d, axis=0), a)
        d *= 2
    return a, b


def _scan_bwd(a, b, row):
    T = a.shape[0]
    d = 1
    while d < T:
        keep = row < T - d
        b = jnp.where(keep, a * pltpu.roll(b, T - d, axis=0) + b, b)
        a = jnp.where(keep, a * pltpu.roll(a, T - d, axis=0), a)
        d *= 2
    return a, b


def _expm1(x):
    small = x * (1.0 + x * (0.5 + x * (1.0 / 6 + x * (1.0 / 24 + x * (1.0 / 120 + x * (1.0 / 720 + x * (1.0 / 5040)))))))
    return jnp.where(jnp.abs(x) < 0.25, small, jnp.exp(x) - 1.0)


_GELU_C = math.sqrt(2.0 / math.pi)


def _gelu_and_grad(x):
    inner = _GELU_C * (x + 0.044715 * x * x * x)
    th = jnp.tanh(inner)
    val = 0.5 * x * (1.0 + th)
    grad = 0.5 * (1.0 + th) + 0.5 * x * (1.0 - th * th) * _GELU_C * (1.0 + 3 * 0.044715 * x * x)
    return val, grad


def _lru_gates(xc, wa, wx, ba, bx, lam):
    xcb = xc.astype(BF16)
    r = jax.nn.sigmoid(jnp.dot(xcb, wa, preferred_element_type=F32) + ba)
    ig = jax.nn.sigmoid(jnp.dot(xcb, wx, preferred_element_type=F32) + bx)
    sp = jax.nn.softplus(-lam)
    log_a = -LRU_C * r * sp
    a = jnp.exp(log_a)
    mult = jnp.sqrt(-_expm1(2.0 * log_a))
    return xcb, r, ig, sp, a, mult


def _lru_fwd(u, ug, conv_w, conv_b, wa_bd, wx_bd, ba, bx, lam):
    S, W = u.shape
    T = min(LRU_CHUNK, S)
    nl, nc = W // LANES, S // T

    def body(u_ref, ug_ref, cw_ref, cb_ref, wa_ref, wx_ref, ba_ref, bx_ref, lam_ref, ya_ref, xc_ref, h_ref, prev_u, h_carry):
        c = pl.program_id(1)

        @pl.when(c == 0)
        def _():
            prev_u[...] = jnp.zeros_like(prev_u)
            h_carry[...] = jnp.zeros_like(h_carry)

        uv = u_ref[...]
        row = lax.broadcasted_iota(jnp.int32, (T, LANES), 0)
        row8 = lax.broadcasted_iota(jnp.int32, (8, LANES), 0)
        cw = cw_ref[...]
        xc = cb_ref[...] + uv * cw[3:4, :]
        pv = prev_u[...]
        for k in range(1, CONV_WIDTH):
            us = pltpu.roll(uv, k, axis=0)
            top = jnp.where(row8 < k, pltpu.roll(pv, k, axis=0), us[0:8])
            us = jnp.concatenate([top, us[8:]], axis=0)
            xc = xc + us * cw[3 - k:4 - k, :]
        prev_u[...] = uv[T - 8:T]
        _, r, ig, sp, a, mult = _lru_gates(xc, wa_ref[...], wx_ref[...], ba_ref[...], bx_ref[...], lam_ref[...])
        bb = mult * (ig * xc)
        aa, hh = _scan_fwd(a, bb, row)
        h = hh + aa * h_carry[7:8, :]
        h_carry[...] = h[T - 8:T]
        gl, _ = _gelu_and_grad(ug_ref[...])
        ya_ref[...] = (h * gl).astype(BF16)
        xc_ref[...] = xc
        h_ref[...] = h

    seq = pl.BlockSpec((T, LANES), lambda l, c: (c, l))
    vec = pl.BlockSpec((1, LANES), lambda l, c: (0, l))
    mat = pl.BlockSpec((None, LANES, LANES), lambda l, c: (l, 0, 0))
    return pl.pallas_call(
        body, name="lru_fwd", grid=(nl, nc),
        in_specs=[seq, seq, pl.BlockSpec((CONV_WIDTH, LANES), lambda l, c: (0, l)), vec, mat, mat, vec, vec, vec],
        out_specs=(seq, seq, seq),
        out_shape=(jax.ShapeDtypeStruct((S, W), BF16), jax.ShapeDtypeStruct((S, W), F32), jax.ShapeDtypeStruct((S, W), F32)),
        scratch_shapes=[pltpu.VMEM((8, LANES), F32), pltpu.VMEM((8, LANES), F32)],
        compiler_params=_cparams(("parallel", "arbitrary")),
    )(u, ug, conv_w, conv_b.reshape(1, W), wa_bd, wx_bd, ba.reshape(1, W), bx.reshape(1, W), lam.reshape(1, W))


def _lru_bwd(dya, u, ug, xc, h, conv_w, wa_bd, wx_bd, ba, bx, lam):
    S, W = u.shape
    T = min(LRU_CHUNK, S)
    nl, nc = W // LANES, S // T
    tb8 = T // 8

    def body(dya_ref, u_ref, ug_ref, xc_ref, h_ref, hp_ref, cw_ref, wa_ref, wx_ref, ba_ref, bx_ref, lam_ref,
             du_ref, dug_ref, dcw_ref, dcb_ref, dba_ref, dbx_ref, dlam_ref, dwa_ref, dwx_ref,
             g_next, a_next, dxc_next):
        c = pl.program_id(1)

        @pl.when(c == 0)
        def _():
            g_next[...] = jnp.zeros_like(g_next)
            a_next[...] = jnp.zeros_like(a_next)
            dxc_next[...] = jnp.zeros_like(dxc_next)
            for ref in (dcw_ref, dcb_ref, dba_ref, dbx_ref, dlam_ref, dwa_ref, dwx_ref):
                ref[...] = jnp.zeros_like(ref)

        row = lax.broadcasted_iota(jnp.int32, (T, LANES), 0)
        row8 = lax.broadcasted_iota(jnp.int32, (8, LANES), 0)
        xcv = xc_ref[...]
        wa, wx = wa_ref[...], wx_ref[...]
        xcb, r, ig, sp, a, mult = _lru_gates(xcv, wa, wx, ba_ref[...], bx_ref[...], lam_ref[...])
        gl, dgl = _gelu_and_grad(ug_ref[...])
        dyav = dya_ref[...]
        hv = h_ref[...]
        dug_ref[...] = (dyav * hv * dgl).astype(BF16)
        dh = dyav * gl
        a_up = pltpu.roll(a, T - 1, axis=0)
        a_up = jnp.where(row == T - 1, a_next[0:1, :], a_up)
        prod, gg = _scan_bwd(a_up, dh, row)
        g = gg + prod * g_next[0:1, :]
        h_prev = pltpu.roll(hv, 1, axis=0)
        first_chunk = c == nc - 1
        h_before = jnp.where(first_chunk, 0.0, hp_ref[7:8, :])
        h_prev = jnp.where(row == 0, h_before, h_prev)
        da = g * h_prev
        d_mult = g * (ig * xcv)
        d_ig = g * mult * xcv
        dxc = g * mult * ig
        d_log_a = da * a - d_mult * (a * a) / mult
        d_r = d_log_a * (-LRU_C * sp)
        d_pa = d_r * r * (1.0 - r)
        d_px = d_ig * ig * (1.0 - ig)
        d_pab, d_pxb = d_pa.astype(BF16), d_px.astype(BF16)
        nt = (((1,), (1,)), ((), ()))
        tn = (((0,), (0,)), ((), ()))
        dxc = dxc + lax.dot_general(d_pab, wa, nt, preferred_element_type=F32) + lax.dot_general(d_pxb, wx, nt, preferred_element_type=F32)
        dwa_ref[...] += lax.dot_general(xcb, d_pab, tn, preferred_element_type=F32)
        dwx_ref[...] += lax.dot_general(xcb, d_pxb, tn, preferred_element_type=F32)
        dlam_ref[...] += jnp.sum(d_log_a * r, axis=0, keepdims=True)
        dba_ref[...] += jnp.sum(d_pa, axis=0, keepdims=True)
        dbx_ref[...] += jnp.sum(d_px, axis=0, keepdims=True)
        dcb_ref[...] += jnp.sum(dxc, axis=0, keepdims=True)
        uv = u_ref[...]
        cw = cw_ref[...]
        nxt = dxc_next[...]
        du = dxc * cw[3:4, :]
        dcw_ref[3:4, :] += jnp.sum(uv * dxc, axis=0, keepdims=True)
        for k in range(1, CONV_WIDTH):
            ds = pltpu.roll(dxc, T - k, axis=0)
            bot = jnp.where(row8 >= 8 - k, pltpu.roll(nxt, 8 - k, axis=0), ds[T - 8:T])
            ds = jnp.concatenate([ds[:T - 8], bot], axis=0)
            du = du + ds * cw[3 - k:4 - k, :]
            dcw_ref[3 - k:4 - k, :] += jnp.sum(uv * ds, axis=0, keepdims=True)
        du_ref[...] = du.astype(BF16)
        g_next[...] = g[0:8]
        a_next[...] = a[0:8]
        dxc_next[...] = dxc[0:8]

    seq = pl.BlockSpec((T, LANES), lambda l, c: (nc - 1 - c, l))
    before = pl.BlockSpec((8, LANES), lambda l, c: (jnp.maximum((nc - 1 - c) * tb8 - 1, 0), l))
    vec = pl.BlockSpec((1, LANES), lambda l, c: (0, l))
    cwb = pl.BlockSpec((CONV_WIDTH, LANES), lambda l, c: (0, l))
    mat = pl.BlockSpec((None, LANES, LANES), lambda l, c: (l, 0, 0))
    vshape = jax.ShapeDtypeStruct((1, W), F32)
    mshape = jax.ShapeDtypeStruct((nl, LANES, LANES), F32)
    return pl.pallas_call(
        body, name="lru_bwd", grid=(nl, nc),
        in_specs=[seq, seq, seq, seq, seq, before, cwb, mat, mat, vec, vec, vec],
        out_specs=(seq, seq, cwb, vec, vec, vec, vec, mat, mat),
        out_shape=(jax.ShapeDtypeStruct((S, W), BF16), jax.ShapeDtypeStruct((S, W), BF16),
                   jax.ShapeDtypeStruct((CONV_WIDTH, W), F32), vshape, vshape, vshape, vshape, mshape, mshape),
        scratch_shapes=[pltpu.VMEM((8, LANES), F32)] * 3,
        compiler_params=_cparams(("parallel", "arbitrary")),
    )(dya, u, ug, xc, h, h, conv_w, wa_bd, wx_bd, ba.reshape(1, W), bx.reshape(1, W), lam.reshape(1, W))


def _decay_fwd(f_logit, bf):
    S = f_logit.shape[0]
    T = min(LRU_CHUNK, S)

    def body(f_ref, b_ref, o_ref, carry):
        @pl.when(pl.program_id(0) == 0)
        def _():
            carry[...] = jnp.zeros_like(carry)

        row = lax.broadcasted_iota(jnp.int32, (T, LANES), 0)
        v = jax.nn.log_sigmoid(f_ref[...] + b_ref[...])
        d = 1
        while d < T:
            v = jnp.where(row >= d, v + pltpu.roll(v, d, axis=0), v)
            d *= 2
        v = v + carry[7:8, :]
        carry[...] = v[T - 8:T]
        o_ref[...] = v

    return pl.pallas_call(
        body, name="decay_fwd", grid=(S // T,),
        in_specs=[pl.BlockSpec((T, LANES), lambda c: (c, 0)), pl.BlockSpec((1, LANES), lambda c: (0, 0))],
        out_specs=pl.BlockSpec((T, LANES), lambda c: (c, 0)),
        out_shape=jax.ShapeDtypeStruct((S, LANES), F32), scratch_shapes=[pltpu.VMEM((8, LANES), F32)],
        compiler_params=_cparams(("arbitrary",)),
    )(f_logit, bf)


def _decay_bwd(d_dec, f_logit, bf):
    S = f_logit.shape[0]
    T = min(LRU_CHUNK, S)
    nc = S // T

    def body(dd_ref, f_ref, b_ref, df_ref, db_ref, carry):
        @pl.when(pl.program_id(0) == 0)
        def _():
            carry[...] = jnp.zeros_like(carry)
            db_ref[...] = jnp.zeros_like(db_ref)

        row = lax.broadcasted_iota(jnp.int32, (T, LANES), 0)
        v = dd_ref[...]
        d = 1
        while d < T:
            v = jnp.where(row < T - d, v + pltpu.roll(v, T - d, axis=0), v)
            d *= 2
        v = v + carry[0:1, :]
        carry[...] = v[0:8]
        df = v * jax.nn.sigmoid(-(f_ref[...] + b_ref[...]))
        df_ref[...] = df.astype(BF16)
        db_ref[...] += jnp.sum(df, axis=0, keepdims=True)

    seq = pl.BlockSpec((T, LANES), lambda c: (nc - 1 - c, 0))
    vec = pl.BlockSpec((1, LANES), lambda c: (0, 0))
    return pl.pallas_call(
        body, name="decay_bwd", grid=(nc,), in_specs=[seq, seq, vec], out_specs=(seq, vec),
        out_shape=(jax.ShapeDtypeStruct((S, LANES), BF16), jax.ShapeDtypeStruct((1, LANES), F32)),
        scratch_shapes=[pltpu.VMEM((8, LANES), F32)], compiler_params=_cparams(("arbitrary",)),
    )(d_dec, f_logit, bf)


def _rope_tables(S):
    pos = jnp.arange(S, dtype=F32)
    inv_freq = ROPE_BASE ** (-jnp.arange(0, MLA_ROPE, 2, dtype=F32) / MLA_ROPE)
    ang = pos[:, None] * inv_freq[None, :]
    cos, sin = jnp.cos(ang), jnp.sin(ang)
    half = MLA_ROPE // 2
    z = lambda n: jnp.zeros((S, n), F32)
    c_q = jnp.concatenate([jnp.ones((S, MLA_NOPE), F32), cos, cos, z(HEAD_PAD - MLA_NOPE - MLA_ROPE)], axis=1)
    c_k = jnp.concatenate([z(MLA_NOPE), cos, cos, z(HEAD_PAD - MLA_NOPE - MLA_ROPE)], axis=1)
    s_lo = jnp.concatenate([z(MLA_NOPE), -sin, z(HEAD_PAD - MLA_NOPE - half)], axis=1)
    s_hi = jnp.concatenate([z(MLA_NOPE + half), sin, z(HEAD_PAD - MLA_NOPE - MLA_ROPE)], axis=1)
    return c_q, c_k, s_lo, s_hi


def _rot(v, c, s_lo, s_hi):
    half = MLA_ROPE // 2
    return v * c + pltpu.roll(v, LANES - half, axis=1) * s_lo + pltpu.roll(v, half, axis=1) * s_hi


def _rot_t(dv, c, s_lo, s_hi):
    half = MLA_ROPE // 2
    return dv * c + pltpu.roll(dv * s_lo, half, axis=1) + pltpu.roll(dv * s_hi, LANES - half, axis=1)


def _rope_q(q_pre, c_q, s_lo, s_hi, *, transpose, out_dtype, name):
    S, W = q_pre.shape
    tm = _rows(S)
    fn = _rot_t if transpose else _rot

    def body(q_ref, c_ref, lo_ref, hi_ref, o_ref):
        c, lo, hi = c_ref[...], lo_ref[...], hi_ref[...]
        for hd in range(W // LANES):
            cols = slice(hd * LANES, (hd + 1) * LANES)
            o_ref[:, cols] = fn(q_ref[:, cols] * MLA_SCALE, c, lo, hi).astype(out_dtype)

    blk = pl.BlockSpec((tm, W), lambda i: (i, 0))
    tab = pl.BlockSpec((tm, LANES), lambda i: (i, 0))
    return pl.pallas_call(
        body, name=name, grid=(S // tm,), in_specs=[blk, tab, tab, tab], out_specs=blk,
        out_shape=jax.ShapeDtypeStruct((S, W), out_dtype), compiler_params=_cparams(("parallel",)),
    )(q_pre, c_q, s_lo, s_hi)


def _rope_k(k_pre, k_rope, c_k, s_lo, s_hi):
    S, W = k_pre.shape
    tm = _rows(S)

    def body(k_ref, r_ref, c_ref, lo_ref, hi_ref, o_ref):
        rot = _rot(r_ref[...], c_ref[...], lo_ref[...], hi_ref[...])
        for hd in range(W // LANES):
            cols = slice(hd * LANES, (hd + 1) * LANES)
            o_ref[:, cols] = (k_ref[:, cols] + rot).astype(BF16)

    blk = pl.BlockSpec((tm, W), lambda i: (i, 0))
    tab = pl.BlockSpec((tm, LANES), lambda i: (i, 0))
    return pl.pallas_call(
        body, name="rope_k", grid=(S // tm,), in_specs=[blk, tab, tab, tab, tab], out_specs=blk,
        out_shape=jax.ShapeDtypeStruct((S, W), BF16), compiler_params=_cparams(("parallel",)),
    )(k_pre, k_rope, c_k, s_lo, s_hi)


def _rope_k_bwd(dk, c_k, s_lo, s_hi):
    S, W = dk.shape
    tm = _rows(S)

    def body(dk_ref, c_ref, lo_ref, hi_ref, o_ref):
        tot = dk_ref[:, 0:LANES]
        for hd in range(1, W // LANES):
            tot = tot + dk_ref[:, hd * LANES:(hd + 1) * LANES]
        o_ref[...] = _rot_t(tot, c_ref[...], lo_ref[...], hi_ref[...]).astype(BF16)

    tab = pl.BlockSpec((tm, LANES), lambda i: (i, 0))
    return pl.pallas_call(
        body, name="rope_k_bwd", grid=(S // tm,), in_specs=[pl.BlockSpec((tm, W), lambda i: (i, 0)), tab, tab, tab],
        out_specs=tab, out_shape=jax.ShapeDtypeStruct((S, LANES), BF16), compiler_params=_cparams(("parallel",)),
    )(dk, c_k, s_lo, s_hi)


def _pairs(n, by_key):
    if by_key:
        pr = [(i, j) for j in range(n) for i in range(j, n)]
    else:
        pr = [(i, j) for i in range(n) for j in range(i + 1)]
    return (jnp.asarray(np.array([p[0] for p in pr], np.int32)), jnp.asarray(np.array([p[1] for p in pr], np.int32)), len(pr))


def _unit_mask(shape, unit, key_axis, q_off=0):
    q = lax.broadcasted_iota(jnp.int32, shape, 1 - key_axis) + q_off
    k = lax.broadcasted_iota(jnp.int32, shape, key_axis)
    if unit > 1:
        q, k = q // unit, k // unit
    return q >= k


_NT = (((1,), (1,)), ((), ()))


def _attn_fwd(q, k, v, dec_col, dec_row, *, unit, name):
    S, W = q.shape
    H = W // LANES
    T = min(ATTN_TILE, S)
    n = S // T
    qi, kj, npairs = _pairs(n, by_key=False)
    has_dec = dec_col is not None

    def body(qi_ref, kj_ref, *refs):
        if has_dec:
            q_ref, k_ref, v_ref, dc_ref, dr_ref, o_ref, lse_ref, m_s, l_s, acc = refs
        else:
            q_ref, k_ref, v_ref, o_ref, lse_ref, m_s, l_s, acc = refs
        t = pl.program_id(1)
        i, j = qi_ref[t], kj_ref[t]

        @pl.when(j == 0)
        def _():
            m_s[...] = jnp.full_like(m_s, NEG_INF)
            l_s[...] = jnp.zeros_like(l_s)
            acc[...] = jnp.zeros_like(acc)

        def step(diag):
            s = lax.dot_general(q_ref[...], k_ref[...], _NT, preferred_element_type=F32)
            if has_dec:
                s = s + (dc_ref[...] - dr_ref[...])
            if diag:
                s = jnp.where(_unit_mask((T, T), unit, 1), s, NEG_INF)
            m_prev = m_s[...]
            m_new = jnp.maximum(m_prev, jnp.max(s, axis=1, keepdims=True))
            alpha = jnp.exp(m_prev - m_new)
            p = jnp.exp(s - m_new)
            l_s[...] = alpha * l_s[...] + jnp.sum(p, axis=1, keepdims=True)
            acc[...] = alpha * acc[...] + jnp.dot(p.astype(BF16), v_ref[...], preferred_element_type=F32)
            m_s[...] = m_new

        @pl.when(j < i)
        def _():
            step(False)

        @pl.when(j == i)
        def _():
            step(True)
            o_ref[...] = (acc[...] / l_s[...]).astype(BF16)
            lse_ref[...] = m_s[...] + jnp.log(l_s[...])

    qb = pl.BlockSpec((T, LANES), lambda h, t, qi, kj: (qi[t], h))
    kb = pl.BlockSpec((T, LANES), lambda h, t, qi, kj: (kj[t], h))
    colq = pl.BlockSpec((None, T, 1), lambda h, t, qi, kj: (h, qi[t], 0))
    rowk = pl.BlockSpec((None, 1, T), lambda h, t, qi, kj: (h, 0, kj[t]))
    in_specs = [qb, kb, kb] + ([colq, rowk] if has_dec else [])
    args = (q, k, v) + ((dec_col, dec_row) if has_dec else ())
    return pl.pallas_call(
        body, name=name,
        grid_spec=pltpu.PrefetchScalarGridSpec(
            num_scalar_prefetch=2, grid=(H, npairs), in_specs=in_specs, out_specs=(qb, colq),
            scratch_shapes=[pltpu.VMEM((T, 1), F32), pltpu.VMEM((T, 1), F32), pltpu.VMEM((T, LANES), F32)]),
        out_shape=(jax.ShapeDtypeStruct((S, W), BF16), jax.ShapeDtypeStruct((H, S, 1), F32)),
        compiler_params=_cparams(("parallel", "arbitrary")),
    )(qi, kj, *args)


def _attn_delta(do, o):
    S, W = o.shape
    H = W // LANES
    tm = _rows(S)

    def body(do_ref, o_ref, d_ref):
        d_ref[...] = jnp.sum(do_ref[...].astype(F32) * o_ref[...].astype(F32), axis=1, keepdims=True)

    blk = pl.BlockSpec((tm, LANES), lambda h, i: (i, h))
    return pl.pallas_call(
        body, name="attn_delta", grid=(H, S // tm), in_specs=[blk, blk],
        out_specs=pl.BlockSpec((None, tm, 1), lambda h, i: (h, i, 0)),
        out_shape=jax.ShapeDtypeStruct((H, S, 1), F32), compiler_params=_cparams(("parallel", "parallel")),
    )(do, o)


def _attn_bwd_dq(q, k, v, do, lse, delta, dec_col, dec_row, *, unit, out_dtype, name):
    S, W = q.shape
    H = W // LANES
    T = min(ATTN_TILE, S)
    n = S // T
    qi, kj, npairs = _pairs(n, by_key=False)
    has_dec = dec_col is not None

    def body(qi_ref, kj_ref, *refs):
        if has_dec:
            q_ref, k_ref, v_ref, do_ref, lse_ref, dl_ref, dc_ref, dr_ref, dq_ref, dd_ref, acc, dacc = refs
        else:
            q_ref, k_ref, v_ref, do_ref, lse_ref, dl_ref, dq_ref, acc = refs
        t = pl.program_id(1)
        i, j = qi_ref[t], kj_ref[t]

        @pl.when(j == 0)
        def _():
            acc[...] = jnp.zeros_like(acc)
            if has_dec:
                dacc[...] = jnp.zeros_like(dacc)

        def step(diag):
            kv = k_ref[...]
            s = lax.dot_general(q_ref[...], kv, _NT, preferred_element_type=F32)
            if has_dec:
                s = s + (dc_ref[...] - dr_ref[...])
            if diag:
                s = jnp.where(_unit_mask((T, T), unit, 1), s, NEG_INF)
            p = jnp.exp(s - lse_ref[...])
            dp = lax.dot_general(do_ref[...], v_ref[...], _NT, preferred_element_type=F32)
            ds = p * (dp - dl_ref[...])
            acc[...] += jnp.dot(ds.astype(BF16), kv, preferred_element_type=F32)
            if has_dec:
                dacc[...] += jnp.sum(ds, axis=1, keepdims=True)

        @pl.when(j < i)
        def _():
            step(False)

        @pl.when(j == i)
        def _():
            step(True)
            dq_ref[...] = acc[...].astype(out_dtype)
            if has_dec:
                dd_ref[...] = dacc[...]

    qb = pl.BlockSpec((T, LANES), lambda h, t, qi, kj: (qi[t], h))
    kb = pl.BlockSpec((T, LANES), lambda h, t, qi, kj: (kj[t], h))
    colq = pl.BlockSpec((None, T, 1), lambda h, t, qi, kj: (h, qi[t], 0))
    rowk = pl.BlockSpec((None, 1, T), lambda h, t, qi, kj: (h, 0, kj[t]))
    in_specs = [qb, kb, kb, qb, colq, colq] + ([colq, rowk] if has_dec else [])
    args = (q, k, v, do, lse, delta) + ((dec_col, dec_row) if has_dec else ())
    scratch = [pltpu.VMEM((T, LANES), F32)] + ([pltpu.VMEM((T, 1), F32)] if has_dec else [])
    out_specs = (qb, colq) if has_dec else qb
    out_shape = jax.ShapeDtypeStruct((S, W), out_dtype)
    if has_dec:
        out_shape = (out_shape, jax.ShapeDtypeStruct((H, S, 1), F32))
    res = pl.pallas_call(
        body, name=name,
        grid_spec=pltpu.PrefetchScalarGridSpec(num_scalar_prefetch=2, grid=(H, npairs), in_specs=in_specs,
                                               out_specs=out_specs, scratch_shapes=scratch),
        out_shape=out_shape, compiler_params=_cparams(("parallel", "arbitrary")),
    )(qi, kj, *args)
    return res if has_dec else (res, None)


def _attn_bwd_dkv(q, k, v, do, lse_row, delta_row, dec_col, dec_row, *, unit, dk_dtype, name):
    S, W = q.shape
    H = W // LANES
    T = min(ATTN_TILE, S)
    n = S // T
    qi, kj, npairs = _pairs(n, by_key=True)
    has_dec = dec_col is not None

    def body(qi_ref, kj_ref, *refs):
        if has_dec:
            q_ref, k_ref, v_ref, do_ref, lse_ref, dl_ref, dc_ref, dr_ref, dk_ref, dv_ref, dd_ref, kacc, vacc, dacc = refs
        else:
            q_ref, k_ref, v_ref, do_ref, lse_ref, dl_ref, dk_ref, dv_ref, kacc, vacc = refs
        t = pl.program_id(1)
        i, j = qi_ref[t], kj_ref[t]

        @pl.when(i == j)
        def _():
            kacc[...] = jnp.zeros_like(kacc)
            vacc[...] = jnp.zeros_like(vacc)
            if has_dec:
                dacc[...] = jnp.zeros_like(dacc)

        def step(diag):
            qv, dov = q_ref[...], do_ref[...]
            st = lax.dot_general(k_ref[...], qv, _NT, preferred_element_type=F32)
            if has_dec:
                st = st + (dr_ref[...] - dc_ref[...])
            if diag:
                st = jnp.where(_unit_mask((T, T), unit, 0), st, NEG_INF)
            pt = jnp.exp(st - lse_ref[...])
            dpt = lax.dot_general(v_ref[...], dov, _NT, preferred_element_type=F32)
            dst = pt * (dpt - dl_ref[...])
            vacc[...] += jnp.dot(pt.astype(BF16), dov, preferred_element_type=F32)
            kacc[...] += jnp.dot(dst.astype(BF16), qv, preferred_element_type=F32)
            if has_dec:
                dacc[...] -= jnp.sum(dst, axis=1, keepdims=True)

        @pl.when(i == j)
        def _():
            step(True)

        @pl.when(i > j)
        def _():
            step(False)

        @pl.when(i == n - 1)
        def _():
            dk_ref[...] = kacc[...].astype(dk_dtype)
            dv_ref[...] = vacc[...].astype(BF16)
            if has_dec:
                dd_ref[...] = dacc[...]

    qb = pl.BlockSpec((T, LANES), lambda h, t, qi, kj: (qi[t], h))
    kb = pl.BlockSpec((T, LANES), lambda h, t, qi, kj: (kj[t], h))
    rowq = pl.BlockSpec((None, 1, T), lambda h, t, qi, kj: (h, 0, qi[t]))
    colk = pl.BlockSpec((None, T, 1), lambda h, t, qi, kj: (h, kj[t], 0))
    in_specs = [qb, kb, kb, qb, rowq, rowq] + ([colk, rowq] if has_dec else [])
    args = (q, k, v, do, lse_row, delta_row) + ((dec_col, dec_row) if has_dec else ())
    scratch = [pltpu.VMEM((T, LANES), F32)] * 2 + ([pltpu.VMEM((T, 1), F32)] if has_dec else [])
    out_specs = (kb, kb) + ((colk,) if has_dec else ())
    out_shape = (jax.ShapeDtypeStruct((S, W), dk_dtype), jax.ShapeDtypeStruct((S, W), BF16))
    if has_dec:
        out_shape = out_shape + (jax.ShapeDtypeStruct((H, S, 1), F32),)
    res = pl.pallas_call(
        body, name=name,
        grid_spec=pltpu.PrefetchScalarGridSpec(num_scalar_prefetch=2, grid=(H, npairs), in_specs=in_specs,
                                               out_specs=out_specs, scratch_shapes=scratch),
        out_shape=out_shape, compiler_params=_cparams(("parallel", "arbitrary")),
    )(qi, kj, *args)
    return res if has_dec else (res[0], res[1], None)


def _attn_bwd(q, k, v, o, lse, do, dec_col, dec_row, *, unit, dq_dtype, dk_dtype, name):
    H, S = lse.shape[0], lse.shape[1]
    delta = _attn_delta(do, o)
    dq, dd_q = _attn_bwd_dq(q, k, v, do, lse, delta, dec_col, dec_row, unit=unit, out_dtype=dq_dtype, name=name + "_dq")
    dk, dv, dd_k = _attn_bwd_dkv(q, k, v, do, lse.reshape(H, 1, S), delta.reshape(H, 1, S), dec_col, dec_row,
                                 unit=unit, dk_dtype=dk_dtype, name=name + "_dkv")
    d_dec = None if dec_col is None else dd_q + dd_k
    return dq, dk, dv, d_dec


ONES_LANE = 64
ROW_SPLIT = 2
ROW_SPLIT_BWD = 4


def _ones_lane_bias():
    one = np.zeros((HEADS, HEAD_PAD), np.float32)
    one[:, ONES_LANE] = 1.0
    return jnp.asarray(one.reshape(1, HEADS * HEAD_PAD))


def _lane_sum(t):
    tot = t[:, 0:LANES]
    for c in range(1, t.shape[1] // LANES):
        tot = tot + t[:, c * LANES:(c + 1) * LANES]
    return tot


def _fa_fwd(q, k, v, dec_row, *, unit, name):
    S, W = q.shape
    H = W // LANES
    T = min(ATTN_TILE, S)
    n, hT = S // T, T // ROW_SPLIT
    qi, kj, npairs = _pairs(n, by_key=False)
    has_dec = dec_row is not None

    def body(qi_ref, kj_ref, *refs):
        if has_dec:
            q_ref, k_ref, v_ref, dr_ref, o_ref, lse_ref, lrow_ref, m_s, acc = refs
        else:
            q_ref, k_ref, v_ref, o_ref, lse_ref, lrow_ref, m_s, acc = refs
        t = pl.program_id(1)
        i, j = qi_ref[t], kj_ref[t]

        @pl.when(j == 0)
        def _():
            m_s[...] = jnp.full_like(m_s, NEG_INF)
            acc[...] = jnp.zeros_like(acc)

        def step(diag):
            for r in range(ROW_SPLIT):
                rows = slice(r * hT, (r + 1) * hT)
                nk = (r + 1) * hT if diag else T
                s = lax.dot_general(q_ref[rows, :], k_ref[0:nk, :], _NT, preferred_element_type=F32)
                if has_dec:
                    s = s - dr_ref[:, 0:nk]
                if diag:
                    s = jnp.where(_unit_mask((hT, nk), unit, 1, r * hT), s, NEG_INF)
                m_prev = m_s[rows, :]
                m_new = jnp.maximum(m_prev, jnp.max(s, axis=1, keepdims=True))
                alpha = jnp.exp(m_prev - m_new)
                p = jnp.exp(s - jnp.tile(m_new, (1, nk // LANES)))
                acc[rows, :] = alpha * acc[rows, :] + jnp.dot(p.astype(BF16), v_ref[0:nk, :], preferred_element_type=F32)
                m_s[rows, :] = m_new

        @pl.when(j < i)
        def _():
            step(False)

        @pl.when(j == i)
        def _():
            step(True)
            av = acc[...]
            l = av[:, ONES_LANE:ONES_LANE + 1]
            lane = lax.broadcasted_iota(jnp.int32, (T, LANES), 1)
            o_ref[...] = jnp.where(lane < ONES_LANE, av / l, 0.0).astype(BF16)
            lse = m_s[...] + jnp.log(l)
            lse_ref[...] = lse
            lrow_ref[...] = lse.T[0:1, :]

    qb = pl.BlockSpec((T, LANES), lambda h, t, qi, kj: (qi[t], h))
    kb = pl.BlockSpec((T, LANES), lambda h, t, qi, kj: (kj[t], h))
    repq = pl.BlockSpec((None, T, LANES), lambda h, t, qi, kj: (h, qi[t], 0))
    rowq = pl.BlockSpec((None, 1, T), lambda h, t, qi, kj: (h, 0, qi[t]))
    rowk = pl.BlockSpec((None, 1, T), lambda h, t, qi, kj: (h, 0, kj[t]))
    in_specs = [qb, kb, kb] + ([rowk] if has_dec else [])
    args = (q, k, v) + ((dec_row,) if has_dec else ())
    return pl.pallas_call(
        body, name=name,
        grid_spec=pltpu.PrefetchScalarGridSpec(
            num_scalar_prefetch=2, grid=(H, npairs), in_specs=in_specs, out_specs=(qb, repq, rowq),
            scratch_shapes=[pltpu.VMEM((T, LANES), F32), pltpu.VMEM((T, LANES), F32)]),
        out_shape=(jax.ShapeDtypeStruct((S, W), BF16), jax.ShapeDtypeStruct((H, S, LANES), F32),
                   jax.ShapeDtypeStruct((H, 1, S), F32)),
        compiler_params=_cparams(("parallel", "arbitrary")),
    )(qi, kj, *args)


def _fa_delta(do, o):
    S, W = o.shape
    H = W // LANES
    tm = _rows(S)

    def body(do_ref, o_ref, d_ref, drow_ref):
        for hd in range(H):
            cols = slice(hd * LANES, (hd + 1) * LANES)
            d = jnp.sum(do_ref[:, cols].astype(F32) * o_ref[:, cols].astype(F32), axis=1, keepdims=True)
            rep = jnp.broadcast_to(d, (tm, LANES))
            d_ref[hd] = rep
            drow_ref[hd] = rep.T[0:1, :]

    blk = pl.BlockSpec((tm, W), lambda i: (i, 0))
    return pl.pallas_call(
        body, name="attn_delta", grid=(S // tm,), in_specs=[blk, blk],
        out_specs=(pl.BlockSpec((H, tm, LANES), lambda i: (0, i, 0)), pl.BlockSpec((H, 1, tm), lambda i: (0, 0, i))),
        out_shape=(jax.ShapeDtypeStruct((H, S, LANES), F32), jax.ShapeDtypeStruct((H, 1, S), F32)),
        compiler_params=_cparams(("parallel",)),
    )(do, o)


def _fa_bwd_dq(q, k, v, do, lse, delta, dec_row, *, unit, out_dtype, name):
    S, W = q.shape
    H = W // LANES
    T = min(ATTN_TILE, S)
    n, reps = S // T, T // LANES
    qi, kj, npairs = _pairs(n, by_key=False)
    has_dec = dec_row is not None

    def body(qi_ref, kj_ref, *refs):
        if has_dec:
            q_ref, k_ref, v_ref, do_ref, lse_ref, dl_ref, dr_ref, dq_ref, dd_ref, acc, dacc = refs
        else:
            q_ref, k_ref, v_ref, do_ref, lse_ref, dl_ref, dq_ref, acc = refs
        t = pl.program_id(1)
        i, j = qi_ref[t], kj_ref[t]

        @pl.when(j == 0)
        def _():
            acc[...] = jnp.zeros_like(acc)
            if has_dec:
                dacc[...] = jnp.zeros_like(dacc)

        def step(diag):
            kv = k_ref[...]
            s = lax.dot_general(q_ref[...], kv, _NT, preferred_element_type=F32)
            if has_dec:
                s = s - dr_ref[...]
            if diag:
                s = jnp.where(_unit_mask((T, T), unit, 1), s, NEG_INF)
            p = jnp.exp(s - jnp.tile(lse_ref[...], (1, reps)))
            dp = lax.dot_general(do_ref[...], v_ref[...], _NT, preferred_element_type=F32)
            ds = p * (dp - jnp.tile(dl_ref[...], (1, reps)))
            acc[...] += jnp.dot(ds.astype(BF16), kv, preferred_element_type=F32)
            if has_dec:
                dacc[...] += _lane_sum(ds)

        @pl.when(j < i)
        def _():
            step(False)

        @pl.when(j == i)
        def _():
            step(True)
            dq_ref[...] = acc[...].astype(out_dtype)
            if has_dec:
                dd_ref[...] = jnp.broadcast_to(jnp.sum(dacc[...], axis=1, keepdims=True), (T, LANES))

    qb = pl.BlockSpec((T, LANES), lambda h, t, qi, kj: (qi[t], h))
    kb = pl.BlockSpec((T, LANES), lambda h, t, qi, kj: (kj[t], h))
    repq = pl.BlockSpec((None, T, LANES), lambda h, t, qi, kj: (h, qi[t], 0))
    rowk = pl.BlockSpec((None, 1, T), lambda h, t, qi, kj: (h, 0, kj[t]))
    in_specs = [qb, kb, kb, qb, repq, repq] + ([rowk] if has_dec else [])
    args = (q, k, v, do, lse, delta) + ((dec_row,) if has_dec else ())
    out_shape = jax.ShapeDtypeStruct((S, W), out_dtype)
    res = pl.pallas_call(
        body, name=name,
        grid_spec=pltpu.PrefetchScalarGridSpec(num_scalar_prefetch=2, grid=(H, npairs), in_specs=in_specs,
                                               out_specs=(qb, repq) if has_dec else qb,
                                               scratch_shapes=[pltpu.VMEM((T, LANES), F32)] * (2 if has_dec else 1)),
        out_shape=(out_shape, jax.ShapeDtypeStruct((H, S, LANES), F32)) if has_dec else out_shape,
        compiler_params=_cparams(("parallel", "arbitrary")),
    )(qi, kj, *args)
    return res if has_dec else (res, None)


def _fa_bwd_dkv(q, k, v, do, lse_row, delta_row, dec_rep, *, unit, dk_dtype, name):
    S, W = q.shape
    H = W // LANES
    T = min(ATTN_TILE, S)
    n, reps = S // T, T // LANES
    qi, kj, npairs = _pairs(n, by_key=True)
    has_dec = dec_rep is not None

    def body(qi_ref, kj_ref, *refs):
        if has_dec:
            q_ref, k_ref, v_ref, do_ref, lse_ref, dl_ref, dc_ref, dk_ref, dv_ref, dd_ref, kacc, vacc, dacc = refs
        else:
            q_ref, k_ref, v_ref, do_ref, lse_ref, dl_ref, dk_ref, dv_ref, kacc, vacc = refs
        t = pl.program_id(1)
        i, j = qi_ref[t], kj_ref[t]

        @pl.when(i == j)
        def _():
            kacc[...] = jnp.zeros_like(kacc)
            vacc[...] = jnp.zeros_like(vacc)
            if has_dec:
                dacc[...] = jnp.zeros_like(dacc)

        def step(diag):
            qv, dov = q_ref[...], do_ref[...]
            st = lax.dot_general(k_ref[...], qv, _NT, preferred_element_type=F32)
            if has_dec:
                st = st - jnp.tile(dc_ref[...], (1, reps))
            if diag:
                st = jnp.where(_unit_mask((T, T), unit, 0), st, NEG_INF)
            pt = jnp.exp(st - lse_ref[...])
            dpt = lax.dot_general(v_ref[...], dov, _NT, preferred_element_type=F32)
            dst = pt * (dpt - dl_ref[...])
            vacc[...] += jnp.dot(pt.astype(BF16), dov, preferred_element_type=F32)
            kacc[...] += jnp.dot(dst.astype(BF16), qv, preferred_element_type=F32)
            if has_dec:
                dacc[...] += _lane_sum(dst)

        @pl.when(i == j)
        def _():
            step(True)

        @pl.when(i > j)
        def _():
            step(False)

        @pl.when(i == n - 1)
        def _():
            dk_ref[...] = kacc[...].astype(dk_dtype)
            dv_ref[...] = vacc[...].astype(BF16)
            if has_dec:
                dd_ref[...] = jnp.broadcast_to(-jnp.sum(dacc[...], axis=1, keepdims=True), (T, LANES))

    qb = pl.BlockSpec((T, LANES), lambda h, t, qi, kj: (qi[t], h))
    kb = pl.BlockSpec((T, LANES), lambda h, t, qi, kj: (kj[t], h))
    rowq = pl.BlockSpec((None, 1, T), lambda h, t, qi, kj: (h, 0, qi[t]))
    repk = pl.BlockSpec((None, T, LANES), lambda h, t, qi, kj: (h, kj[t], 0))
    in_specs = [qb, kb, kb, qb, rowq, rowq] + ([repk] if has_dec else [])
    args = (q, k, v, do, lse_row, delta_row) + ((dec_rep,) if has_dec else ())
    scratch = [pltpu.VMEM((T, LANES), F32)] * (3 if has_dec else 2)
    out_specs = (kb, kb) + ((repk,) if has_dec else ())
    out_shape = (jax.ShapeDtypeStruct((S, W), dk_dtype), jax.ShapeDtypeStruct((S, W), BF16))
    if has_dec:
        out_shape = out_shape + (jax.ShapeDtypeStruct((H, S, LANES), F32),)
    res = pl.pallas_call(
        body, name=name,
        grid_spec=pltpu.PrefetchScalarGridSpec(num_scalar_prefetch=2, grid=(H, npairs), in_specs=in_specs,
                                               out_specs=out_specs, scratch_shapes=scratch),
        out_shape=out_shape, compiler_params=_cparams(("parallel", "arbitrary")),
    )(qi, kj, *args)
    return res if has_dec else (res[0], res[1], None)


_TN = (((0,), (0,)), ((), ()))


def _fa_bwd_fused(q, k, v, do, o, lse, dec_row, *, unit, dq_dtype, dk_dtype, name):
    S, W = q.shape
    H = W // LANES
    T = min(ATTN_TILE, S)
    n, hT = S // T, T // ROW_SPLIT_BWD
    qi, kj, npairs = _pairs(n, by_key=False)
    has_dec = dec_row is not None

    def body(qi_ref, kj_ref, *refs):
        if has_dec:
            (q_ref, k_ref, v_ref, do_ref, o_ref, lse_ref, dr_ref, dq_ref, dk_ref, dv_ref, ddq_ref, ddk_ref,
             qacc, kacc, vacc, dl_ref, rsum, csum) = refs
        else:
            q_ref, k_ref, v_ref, do_ref, o_ref, lse_ref, dq_ref, dk_ref, dv_ref, qacc, kacc, vacc, dl_ref = refs
        t = pl.program_id(1)
        i, j = qi_ref[t], kj_ref[t]

        @pl.when(t == 0)
        def _():
            kacc[...] = jnp.zeros_like(kacc)
            vacc[...] = jnp.zeros_like(vacc)
            if has_dec:
                csum[...] = jnp.zeros_like(csum)

        @pl.when(j == 0)
        def _():
            qacc[...] = jnp.zeros_like(qacc)
            delta = jnp.sum(do_ref[...].astype(F32) * o_ref[...].astype(F32), axis=1, keepdims=True)
            dl_ref[...] = jnp.broadcast_to(delta, (T, LANES))
            if has_dec:
                rsum[...] = jnp.zeros_like(rsum)

        def step(diag):
            for r in range(ROW_SPLIT_BWD):
                rows = slice(r * hT, (r + 1) * hT)
                nk = (r + 1) * hT if diag else T
                qv, dov, kv = q_ref[rows, :], do_ref[rows, :], k_ref[0:nk, :]
                s = lax.dot_general(qv, kv, _NT, preferred_element_type=F32)
                if has_dec:
                    s = s - dr_ref[:, 0:nk]
                if diag:
                    s = jnp.where(_unit_mask((hT, nk), unit, 1, r * hT), s, NEG_INF)
                p = jnp.exp(s - jnp.tile(lse_ref[rows, :], (1, nk // LANES)))
                dp = lax.dot_general(dov, v_ref[0:nk, :], _NT, preferred_element_type=F32)
                ds = p * (dp - jnp.tile(dl_ref[rows, :], (1, nk // LANES)))
                pb, dsb = p.astype(BF16), ds.astype(BF16)
                qacc[rows, :] += jnp.dot(dsb, kv, preferred_element_type=F32)
                vacc[j, 0:nk, :] += lax.dot_general(pb, dov, _TN, preferred_element_type=F32)
                kacc[j, 0:nk, :] += lax.dot_general(dsb, qv, _TN, preferred_element_type=F32)
                if has_dec:
                    rsum[rows, :] += _lane_sum(ds)
                    csum[j, :, 0:nk] -= jnp.sum(ds, axis=0, keepdims=True)

        @pl.when(j < i)
        def _():
            step(False)

        @pl.when(j == i)
        def _():
            step(True)
            dq_ref[...] = qacc[...].astype(dq_dtype)
            if has_dec:
                ddq_ref[...] = jnp.broadcast_to(jnp.sum(rsum[...], axis=1, keepdims=True), (T, LANES))

        @pl.when(t == npairs - 1)
        def _():
            for jj in range(n):
                dk_ref[jj * T:(jj + 1) * T, :] = kacc[jj].astype(dk_dtype)
                dv_ref[jj * T:(jj + 1) * T, :] = vacc[jj].astype(BF16)
                if has_dec:
                    ddk_ref[:, jj * T:(jj + 1) * T] = csum[jj]

    qb = pl.BlockSpec((T, LANES), lambda h, t, qi, kj: (qi[t], h))
    kb = pl.BlockSpec((T, LANES), lambda h, t, qi, kj: (kj[t], h))
    head = pl.BlockSpec((S, LANES), lambda h, t, qi, kj: (0, h))
    repq = pl.BlockSpec((None, T, LANES), lambda h, t, qi, kj: (h, qi[t], 0))
    rowk = pl.BlockSpec((None, 1, T), lambda h, t, qi, kj: (h, 0, kj[t]))
    rowh = pl.BlockSpec((None, 1, S), lambda h, t, qi, kj: (h, 0, 0))
    in_specs = [qb, kb, kb, qb, qb, repq] + ([rowk] if has_dec else [])
    args = (q, k, v, do, o, lse) + ((dec_row,) if has_dec else ())
    out_specs = [qb, head, head] + ([repq, rowh] if has_dec else [])
    out_shape = [jax.ShapeDtypeStruct((S, W), dq_dtype), jax.ShapeDtypeStruct((S, W), dk_dtype), jax.ShapeDtypeStruct((S, W), BF16)]
    scratch = [pltpu.VMEM((T, LANES), F32), pltpu.VMEM((n, T, LANES), F32), pltpu.VMEM((n, T, LANES), F32),
               pltpu.VMEM((T, LANES), F32)]
    if has_dec:
        out_shape += [jax.ShapeDtypeStruct((H, S, LANES), F32), jax.ShapeDtypeStruct((H, 1, S), F32)]
        scratch += [pltpu.VMEM((T, LANES), F32), pltpu.VMEM((n, 1, T), F32)]
    res = pl.pallas_call(
        body, name=name,
        grid_spec=pltpu.PrefetchScalarGridSpec(num_scalar_prefetch=2, grid=(H, npairs), in_specs=in_specs,
                                               out_specs=tuple(out_specs), scratch_shapes=scratch),
        out_shape=tuple(out_shape),
        compiler_params=pltpu.CompilerParams(dimension_semantics=("parallel", "arbitrary"), vmem_limit_bytes=FUSED_BWD_VMEM_BYTES),
    )(qi, kj, *args)
    return res if has_dec else (res[0], res[1], res[2], None, None)


FUSED_BWD_VMEM_BYTES = 58 * 1024 * 1024


def _fa_bwd(q, k, v, o, lse, lse_row, do, dec_row, dec_rep, *, unit, dq_dtype, dk_dtype, name):
    dq, dk, dv, dd_q, dd_k = _fa_bwd_fused(q, k, v, do, o, lse, dec_row, unit=unit, dq_dtype=dq_dtype, dk_dtype=dk_dtype, name=name)
    if dd_k is None:
        return dq, dk, dv, None
    return dq, dk, dv, jnp.max(dd_q, axis=2) + dd_k.reshape(dd_k.shape[0], dd_k.shape[2])


def _fa_bwd_split(q, k, v, o, lse, lse_row, do, dec_row, dec_rep, *, unit, dq_dtype, dk_dtype, name):
    delta, delta_row = _fa_delta(do, o)
    dq, dd_q = _fa_bwd_dq(q, k, v, do, lse, delta, dec_row, unit=unit, out_dtype=dq_dtype, name=name + "_dq")
    one = lambda t: jnp.max(t, axis=2)
    dk, dv, dd_k = _fa_bwd_dkv(q, k, v, do, lse_row, delta_row, dec_rep, unit=unit, dk_dtype=dk_dtype, name=name + "_dkv")
    return dq, dk, dv, (None if dd_k is None else one(dd_q) + one(dd_k))


def _merge_fwd(ya, yb, yc, gate_logit, gate_b):
    S, D = ya.shape
    tm = min(256, S)

    def body(a_ref, b_ref, c_ref, gl_ref, gb_ref, o_ref):
        g = jax.nn.sigmoid(gl_ref[...] + gb_ref[...])
        o_ref[...] = (g[:, 0:D] * a_ref[...] + g[:, D:2 * D] * b_ref[...] + g[:, 2 * D:3 * D] * c_ref[...]).astype(BF16)

    row = pl.BlockSpec((tm, D), lambda i: (i, 0))
    return pl.pallas_call(
        body, name="merge_fwd", grid=(S // tm,),
        in_specs=[row, row, row, pl.BlockSpec((tm, 3 * D), lambda i: (i, 0)), pl.BlockSpec((1, 3 * D), lambda i: (0, 0))],
        out_specs=row, out_shape=jax.ShapeDtypeStruct((S, D), BF16), compiler_params=_cparams(("parallel",)),
    )(ya, yb, yc, gate_logit, gate_b.reshape(1, 3 * D))


def _merge_bwd(dm, ya, yb, yc, gate_logit, gate_b):
    S, D = ya.shape
    tm = min(256, S)

    def body(dm_ref, a_ref, b_ref, c_ref, gl_ref, gb_ref, da_ref, db_ref, dc_ref, dgl_ref, dgb_ref):
        g = jax.nn.sigmoid(gl_ref[...] + gb_ref[...])
        dmv = dm_ref[...]
        parts = []
        for n, (y_ref, dy_ref) in enumerate(((a_ref, da_ref), (b_ref, db_ref), (c_ref, dc_ref))):
            gn = g[:, n * D:(n + 1) * D]
            dy_ref[...] = (dmv * gn).astype(BF16)
            parts.append(dmv * y_ref[...] * gn * (1.0 - gn))
        dgl = jnp.concatenate(parts, axis=1)
        dgl_ref[...] = dgl.astype(BF16)

        @pl.when(pl.program_id(0) == 0)
        def _():
            dgb_ref[...] = jnp.zeros_like(dgb_ref)

        dgb_ref[...] += jnp.sum(dgl, axis=0, keepdims=True)

    row = pl.BlockSpec((tm, D), lambda i: (i, 0))
    wide = pl.BlockSpec((tm, 3 * D), lambda i: (i, 0))
    vec = pl.BlockSpec((1, 3 * D), lambda i: (0, 0))
    act = jax.ShapeDtypeStruct((S, D), BF16)
    da, db, dc, dgl, dgb = pl.pallas_call(
        body, name="merge_bwd", grid=(S // tm,), in_specs=[row, row, row, row, wide, vec],
        out_specs=(row, row, row, wide, vec),
        out_shape=(act, act, act, jax.ShapeDtypeStruct((S, 3 * D), BF16), jax.ShapeDtypeStruct((1, 3 * D), F32)),
        compiler_params=_cparams(("arbitrary",)),
    )(dm, ya, yb, yc, gate_logit, gate_b.reshape(1, 3 * D))
    return da, db, dc, dgl, dgb.reshape(3 * D)


def _swiglu_fwd(hf):
    S, W2 = hf.shape
    F = W2 // 2
    tm = min(128, S)

    def body(h_ref, o_ref):
        gt, up = h_ref[:, 0:F], h_ref[:, F:W2]
        o_ref[...] = (gt * jax.nn.sigmoid(gt) * up).astype(BF16)

    return pl.pallas_call(
        body, name="swiglu_fwd", grid=(S // tm,), in_specs=[pl.BlockSpec((tm, W2), lambda i: (i, 0))],
        out_specs=pl.BlockSpec((tm, F), lambda i: (i, 0)), out_shape=jax.ShapeDtypeStruct((S, F), BF16),
        compiler_params=_cparams(("parallel",)),
    )(hf)


def _swiglu_bwd(dact, hf):
    S, W2 = hf.shape
    F = W2 // 2
    tm = min(128, S)

    def body(d_ref, h_ref, o_ref):
        gt, up = h_ref[:, 0:F], h_ref[:, F:W2]
        sg = jax.nn.sigmoid(gt)
        dv = d_ref[...]
        o_ref[:, 0:F] = (dv * up * sg * (1.0 + gt * (1.0 - sg))).astype(BF16)
        o_ref[:, F:W2] = (dv * gt * sg).astype(BF16)

    return pl.pallas_call(
        body, name="swiglu_bwd", grid=(S // tm,),
        in_specs=[pl.BlockSpec((tm, F), lambda i: (i, 0)), pl.BlockSpec((tm, W2), lambda i: (i, 0))],
        out_specs=pl.BlockSpec((tm, W2), lambda i: (i, 0)), out_shape=jax.ShapeDtypeStruct((S, W2), BF16),
        compiler_params=_cparams(("parallel",)),
    )(dact, hf)


def _ple_fwd(x, pre, e, g_next):
    S, D = x.shape
    tm = _rows(S)
    with_norm = g_next is not None

    def body(*refs):
        x_ref, p_ref, e_ref = refs[:3]
        xn = x_ref[...] + jax.nn.sigmoid(p_ref[...]) * e_ref[...]
        if with_norm:
            g_ref, o_ref, h_ref = refs[3:]
            rstd = lax.rsqrt(jnp.mean(xn * xn, axis=1, keepdims=True) + EPS)
            h_ref[...] = (xn * rstd * g_ref[...]).astype(BF16)
        else:
            o_ref = refs[3]
        o_ref[...] = xn

    row = pl.BlockSpec((tm, D), lambda i: (i, 0))
    xs = jax.ShapeDtypeStruct((S, D), F32)
    if not with_norm:
        return pl.pallas_call(body, name="ple_fwd_last", grid=(S // tm,), in_specs=[row, row, row], out_specs=row,
                              out_shape=xs, compiler_params=_cparams(("parallel",)))(x, pre, e), None
    return pl.pallas_call(body, name="ple_fwd", grid=(S // tm,), in_specs=[row, row, row, pl.BlockSpec((1, D), lambda i: (0, 0))],
                          out_specs=(row, row), out_shape=(xs, jax.ShapeDtypeStruct((S, D), BF16)),
                          compiler_params=_cparams(("parallel",)))(x, pre, e, g_next.reshape(1, D))


def _ple_bwd(dx, pre, e):
    S, D = dx.shape
    tm = _rows(S)

    def body(dx_ref, p_ref, e_ref, dp_ref, de_ref):
        pg = jax.nn.sigmoid(p_ref[...])
        dxv = dx_ref[...]
        dp_ref[...] = (dxv * e_ref[...] * pg * (1.0 - pg)).astype(BF16)
        de_ref[...] = (dxv * pg).astype(BF16)

    row = pl.BlockSpec((tm, D), lambda i: (i, 0))
    act = jax.ShapeDtypeStruct((S, D), BF16)
    return pl.pallas_call(body, name="ple_bwd", grid=(S // tm,), in_specs=[row, row, row], out_specs=(row, row),
                          out_shape=(act, act), compiler_params=_cparams(("parallel",)))(dx, pre, e)


def _pad_heads(w, real):
    K = w.shape[0]
    w = w.reshape(K, HEADS, real)
    return jnp.pad(w, ((0, 0), (0, 0), (0, HEAD_PAD - real))).reshape(K, HEADS * HEAD_PAD)


def _unpad_heads(w, real):
    K = w.shape[0]
    return w.reshape(K, HEADS, HEAD_PAD)[:, :, :real].reshape(K, HEADS * real)


def _pad_head_rows(w, real):
    N = w.shape[1]
    w = w.reshape(HEADS, real, N)
    return jnp.pad(w, ((0, 0), (0, HEAD_PAD - real), (0, 0))).reshape(HEADS * HEAD_PAD, N)


def _unpad_head_rows(w, real):
    N = w.shape[1]
    return w.reshape(HEADS, HEAD_PAD, N)[:, :real].reshape(HEADS * real, N)


def _block_diag(w):
    w = w.reshape(4, 2, 64, 64)
    z = jnp.zeros((4, 64, 64), w.dtype)
    top = jnp.concatenate([w[:, 0], z], axis=2)
    bot = jnp.concatenate([z, w[:, 1]], axis=2)
    return jnp.concatenate([top, bot], axis=1)


def _block_diag_t(w):
    return jnp.stack([w[:, :64, :64], w[:, 64:, 64:]], axis=1).reshape(8, 64, 64)


_IN_SPLITS = (512, 512, 384, 288, 512, 512, 512, 8, 3072)
_IN_OFF = np.concatenate([[0], np.cumsum(_IN_SPLITS)])
_KR_OFF = 64
_SEG_NAMES = ("u", "ug", "cq", "ckv", "kr", "fq", "fk", "fv", "fl", "gate")


def _in_segments(w_in):
    c = lambda n: w_in[:, int(_IN_OFF[n]):int(_IN_OFF[n + 1])]
    kv = c(3)
    kr = jnp.pad(kv[:, MLA_KV_LORA:], ((0, 0), (_KR_OFF, LANES - _KR_OFF - MLA_ROPE)))
    fl = jnp.pad(c(7), ((0, 0), (0, LANES - HEADS)))
    fq = _pad_heads(c(4), FOX_HEAD_DIM) * jnp.asarray(FOX_SCALE, w_in.dtype)
    return [c(0), c(1), c(2), kv[:, :MLA_KV_LORA], kr, fq, _pad_heads(c(5), FOX_HEAD_DIM), _pad_heads(c(6), FOX_HEAD_DIM), fl, c(8)]


def _in_unsegment(dw_p, widths):
    offs = np.concatenate([[0], np.cumsum(widths)])
    seg = [dw_p[:, int(offs[n]):int(offs[n + 1])] for n in range(len(widths))]
    u, ug, cq, ckv, kr, fq, fk, fv, fl, gate = seg
    return jnp.concatenate([
        u, ug, cq, ckv, kr[:, _KR_OFF:_KR_OFF + MLA_ROPE], _unpad_heads(fq, FOX_HEAD_DIM) * FOX_SCALE,
        _unpad_heads(fk, FOX_HEAD_DIM), _unpad_heads(fv, FOX_HEAD_DIM), fl[:, :HEADS], gate], axis=1)


def _split_wuq(wuq):
    return _pad_heads(wuq, MLA_NOPE + MLA_ROPE)


def _split_wukv(wukv):
    w = wukv.reshape(MLA_KV_LORA, HEADS, MLA_NOPE + MLA_V)
    pad = lambda t: jnp.pad(t, ((0, 0), (0, 0), (0, HEAD_PAD - t.shape[2]))).reshape(MLA_KV_LORA, HEADS * HEAD_PAD)
    return pad(w[:, :, :MLA_NOPE]), pad(w[:, :, MLA_NOPE:])


def _merge_wukv(dk_p, dv_p):
    k = dk_p.reshape(MLA_KV_LORA, HEADS, HEAD_PAD)[:, :, :MLA_NOPE]
    v = dv_p.reshape(MLA_KV_LORA, HEADS, HEAD_PAD)[:, :, :MLA_V]
    return jnp.concatenate([k, v], axis=2).reshape(MLA_KV_LORA, HEADS * (MLA_NOPE + MLA_V))


def _heads_layout(d):
    S = d.shape[0]
    t = d[:, :HEADS].T
    return t.reshape(HEADS, 1, S), jnp.broadcast_to(t[:, :, None], (HEADS, S, LANES))


def _layer_fwd(x, h, p_i, w, g_next, tabs):
    c_q, c_k, s_lo, s_hi = tabs
    sv = {"x0": x}
    segs = _in_segments(w["w_in"])
    z = {}
    for nm, ws in zip(_SEG_NAMES, segs):
        z[nm] = _mm(h, ws, out_dtype=BF16 if nm in ("fq", "fk", "fv") else F32, bias=_ones_lane_bias() if nm == "fv" else None,
                    name="in_" + nm)
    sv.update(h=h, z=z)
    wa_bd, wx_bd = _block_diag(w["lru_wa"]).astype(BF16), _block_diag(w["lru_wx"]).astype(BF16)
    oa, xc, hs = _lru_fwd(z["u"], z["ug"], w["conv_w"], w["conv_b"], wa_bd, wx_bd, w["lru_ba"], w["lru_bx"], w["lru_lambda"])
    sv.update(oa=oa, xc=xc, hs=hs)
    qn = _rmsnorm_fwd(z["cq"], w["mla_q_norm"], "q_norm_fwd")
    kvn = _rmsnorm_fwd(z["ckv"], w["mla_kv_norm"], "kv_norm_fwd")
    wuq_p = _split_wuq(w["mla_wuq"])
    wk_p, wv_p = _split_wukv(w["mla_wukv"])
    qb = _rope_q(_mm(qn, wuq_p, name="mla_q"), c_q, s_lo, s_hi, transpose=False, out_dtype=BF16, name="rope_q")
    kb = _rope_k(_mm(kvn, wk_p, name="mla_k"), z["kr"], c_k, s_lo, s_hi)
    vb = _mm(kvn, wv_p, out_dtype=BF16, bias=_ones_lane_bias(), name="mla_v")
    ob, lse_b, lrow_b = _fa_fwd(qb, kb, vb, None, unit=64, name="mla_attn")
    sv.update(qn=qn, kvn=kvn, qb=qb, kb=kb, vb=vb, ob=ob, lse_b=lse_b, lrow_b=lrow_b)
    bf = jnp.pad(w["fox_bf"], (0, LANES - HEADS)).reshape(1, LANES)
    dec = _decay_fwd(z["fl"], bf)
    drow, drep = _heads_layout(dec)
    oc, lse_c, lrow_c = _fa_fwd(z["fq"], z["fk"], z["fv"], drow, unit=1, name="fox_attn")
    sv.update(drow=drow, drep=drep, oc=oc, lse_c=lse_c, lrow_c=lrow_c)
    ya = _mm(oa, w["w_br_a"], name="br_a")
    yb = _mm(ob, _pad_head_rows(w["w_br_b"], MLA_V), name="br_b")
    yc = _mm(oc, _pad_head_rows(w["w_br_c"], FOX_HEAD_DIM), name="br_c")
    merged = _merge_fwd(ya, yb, yc, z["gate"], w["gate_b"])
    x1, hn = _mm_res_norm(merged, w["w_o"], x, w["ffn_norm"], "w_o")
    sv.update(ya=ya, yb=yb, yc=yc, merged=merged, x1=x1)
    hf, act = _ffn_up(hn, _ffn_pair_columns(w["w_gate_up"]))
    x2, pn = _mm_res_norm(act, w["w_down"], x1, w["ple_norm"], "ffn_down")
    sv.update(hn=hn, hf=hf, act=act, x2=x2)
    pre = _mm(pn, w["w_ple_gate"], name="ple_gate")
    e = _mm(p_i, w["w_ple"], name="ple_embed")
    x3, h_next = _ple_fwd(x2, pre, e, g_next)
    sv.update(pn=pn, pre=pre, e=e, p_i=p_i)
    return x3, h_next, sv


def _layer_bwd(dx3, w, sv, tabs):
    c_q, c_k, s_lo, s_hi = tabs
    g = {}
    z = sv["z"]
    dpre, de = _ple_bwd(dx3, sv["pre"], sv["e"])
    g["w_ple"] = _mm(sv["p_i"], de, ta=True, name="d_w_ple")
    g["w_ple_gate"] = _mm(sv["pn"], dpre, ta=True, name="d_w_ple_gate")
    dpn = _mm(dpre, w["w_ple_gate"], tb=True, name="d_pn")
    dx2, g["ple_norm"] = _rmsnorm_bwd(sv["x2"], w["ple_norm"], dpn, add=dx3, name="ple_norm_bwd")
    g["w_down"] = _mm(sv["act"], dx2, ta=True, name="d_w_down")
    dhf = _ffn_down_bwd(dx2, w["w_down"], sv["hf"])
    g["w_gate_up"] = _ffn_unpair_columns(_mm(sv["hn"], dhf, ta=True, name="d_w_gate_up"))
    dhn = _mm(dhf, _ffn_pair_columns(w["w_gate_up"]), tb=True, name="d_hn")
    dx1, g["ffn_norm"] = _rmsnorm_bwd(sv["x1"], w["ffn_norm"], dhn, add=dx2, name="ffn_norm_bwd")
    g["w_o"] = _mm(sv["merged"], dx1, ta=True, name="d_w_o")
    dm = _mm(dx1, w["w_o"], tb=True, name="d_merged")
    dya, dyb, dyc, dgate, g["gate_b"] = _merge_bwd(dm, sv["ya"], sv["yb"], sv["yc"], z["gate"], w["gate_b"])
    wbb_p, wbc_p = _pad_head_rows(w["w_br_b"], MLA_V), _pad_head_rows(w["w_br_c"], FOX_HEAD_DIM)
    g["w_br_a"] = _mm(sv["oa"], dya, ta=True, name="d_w_br_a")
    g["w_br_b"] = _unpad_head_rows(_mm(sv["ob"], dyb, ta=True, name="d_w_br_b"), MLA_V)
    g["w_br_c"] = _unpad_head_rows(_mm(sv["oc"], dyc, ta=True, name="d_w_br_c"), FOX_HEAD_DIM)
    doa = _mm(dya, w["w_br_a"], tb=True, name="d_oa")
    dob = _mm(dyb, wbb_p, tb=True, out_dtype=BF16, name="d_ob")
    doc = _mm(dyc, wbc_p, tb=True, out_dtype=BF16, name="d_oc")
    dfq, dfk, dfv, d_dec = _fa_bwd(z["fq"], z["fk"], z["fv"], sv["oc"], sv["lse_c"], sv["lrow_c"], doc, sv["drow"], sv["drep"],
                                   unit=1, dq_dtype=BF16, dk_dtype=BF16, name="fox_attn_bwd")
    d_dec = jnp.pad(d_dec.T, ((0, 0), (0, LANES - HEADS)))
    bf = jnp.pad(w["fox_bf"], (0, LANES - HEADS)).reshape(1, LANES)
    dfl, dbf = _decay_bwd(d_dec, z["fl"], bf)
    g["fox_bf"] = dbf[0, :HEADS]
    dqb, dkb, dvb, _ = _fa_bwd(sv["qb"], sv["kb"], sv["vb"], sv["ob"], sv["lse_b"], sv["lrow_b"], dob, None, None,
                               unit=64, dq_dtype=F32, dk_dtype=F32, name="mla_attn_bwd")
    wuq_p = _split_wuq(w["mla_wuq"])
    wk_p, wv_p = _split_wukv(w["mla_wukv"])
    dq_pre = _rope_q(dqb, c_q, s_lo, s_hi, transpose=True, out_dtype=BF16, name="rope_q_bwd")
    dkr = _rope_k_bwd(dkb, c_k, s_lo, s_hi)
    g["mla_wuq"] = _unpad_heads(_mm(sv["qn"], dq_pre, ta=True, name="d_wuq"), MLA_NOPE + MLA_ROPE)
    g["mla_wukv"] = _merge_wukv(_mm(sv["kvn"], dkb, ta=True, name="d_wuk"), _mm(sv["kvn"], dvb, ta=True, name="d_wuv"))
    dqn = _mm(dq_pre, wuq_p, tb=True, name="d_qn")
    dkvn = _mm(dvb, wv_p, tb=True, res=_mm(dkb, wk_p, tb=True, name="d_kvn_k"), name="d_kvn")
    dcq, g["mla_q_norm"] = _rmsnorm_bwd(z["cq"], w["mla_q_norm"], dqn, out_dtype=BF16, name="q_norm_bwd")
    dckv, g["mla_kv_norm"] = _rmsnorm_bwd(z["ckv"], w["mla_kv_norm"], dkvn, out_dtype=BF16, name="kv_norm_bwd")
    wa_bd, wx_bd = _block_diag(w["lru_wa"]).astype(BF16), _block_diag(w["lru_wx"]).astype(BF16)
    du, dug, dcw, dcb, dba, dbx, dlam, dwa, dwx = _lru_bwd(
        doa, z["u"], z["ug"], sv["xc"], sv["hs"], w["conv_w"], wa_bd, wx_bd, w["lru_ba"], w["lru_bx"], w["lru_lambda"])
    g["conv_w"], g["conv_b"], g["lru_ba"], g["lru_bx"] = dcw, dcb[0], dba[0], dbx[0]
    g["lru_lambda"] = dlam[0] * LRU_C * jax.nn.sigmoid(-w["lru_lambda"])
    g["lru_wa"], g["lru_wx"] = _block_diag_t(dwa), _block_diag_t(dwx)
    dsegs = [du, dug, dcq, dckv, dkr, dfq, dfk, dfv, dfl, dgate]
    dz = jnp.concatenate(dsegs, axis=1)
    w_in_p = jnp.concatenate(_in_segments(w["w_in"]), axis=1)
    g["w_in"] = _in_unsegment(_mm(sv["h"], dz, ta=True, name="d_w_in"), [d.shape[1] for d in dsegs])
    dh = _mm(dz, w_in_p, tb=True, name="d_h")
    dx0, g["mix_norm"] = _rmsnorm_bwd(sv["x0"], w["mix_norm"], dh, add=dx1, name="mix_norm_bwd")
    return dx0, g


_LAYER_WEIGHTS = ("mix_norm", "w_in", "gate_b", "conv_w", "conv_b", "lru_wa", "lru_ba", "lru_wx", "lru_bx", "lru_lambda",
                  "mla_q_norm", "mla_wuq", "mla_kv_norm", "mla_wukv", "fox_bf", "w_br_a", "w_br_b", "w_br_c", "w_o",
                  "ffn_norm", "w_gate_up", "w_down", "ple_norm", "w_ple_gate", "w_ple")
_BIG = ("w_in", "mla_wuq", "mla_wukv", "w_br_a", "w_br_b", "w_br_c", "w_o", "w_gate_up", "w_down", "w_ple_gate", "w_ple")
_ROW_SHARDED = ("w_o", "w_down", "w_ple_gate")
_SMALL = ("mix_norm", "gate_b", "conv_b", "lru_wa", "lru_ba", "lru_wx", "lru_bx", "lru_lambda", "mla_q_norm", "mla_kv_norm",
          "fox_bf", "ffn_norm", "ple_norm")


def _local_step(x, p, layers, final_norm, target):
    tabs = _rope_tables(x.shape[0])
    saved = []
    h = _rmsnorm_fwd(x, layers[0]["mix_norm"], "mix_norm_fwd")
    for i in range(DEPTH):
        g_next = layers[i + 1]["mix_norm"] if i + 1 < DEPTH else None
        x, h, sv = _layer_fwd(x, h, p[i], layers[i], g_next, tabs)
        saved.append(sv)
    loss, dx, d_final = _loss_head(x, final_norm, target)
    grads = [None] * DEPTH
    for i in reversed(range(DEPTH)):
        dx, grads[i] = _layer_bwd(dx, layers[i], saved[i], tabs)
    return loss, dx, grads, d_final


def _hbm():
    return pl.BlockSpec(memory_space=pltpu.HBM)


def _peers(x, y):
    return [(1 - x, y), (x, 1 - y), (1 - x, 1 - y)]


def _gather_chips_two_level(shard, name):
    R, W = shard.shape
    Rh = R // 2

    def body(src_ref, out_ref, send_sems, recv_sems):
        x, y, c = lax.axis_index("x"), lax.axis_index("y"), lax.axis_index("c")
        me = 2 * x + y
        mine, other = pl.ds(c * Rh, Rh), pl.ds((1 - c) * Rh, Rh)
        peers = _peers(x, y)

        def copy(j, src, slot, rows, to):
            return pltpu.make_async_remote_copy(src_ref=src, dst_ref=out_ref.at[slot, rows], send_sem=send_sems.at[j],
                                                recv_sem=recv_sems.at[j], device_id=to, device_id_type=MESH)

        first = [copy(j, src_ref.at[mine], me, mine, (px, py, c)) for j, (px, py) in enumerate(peers)]
        for cp in first:
            cp.start()
        passed = []
        for j, (px, py) in enumerate(peers):
            slot = 2 * px + py
            copy(j, src_ref.at[mine], slot, mine, (px, py, c)).wait_recv()
            cp = copy(3 + j, out_ref.at[slot, mine], slot, mine, (x, y, 1 - c))
            cp.start()
            passed.append(cp)
        for j, (px, py) in enumerate(peers):
            copy(3 + j, src_ref.at[other], 2 * px + py, other, (x, y, 1 - c)).wait_recv()
        for cp in first + passed:
            cp.wait_send()

    return pl.pallas_call(
        body, name=name, in_specs=[_hbm()], out_specs=_hbm(), out_shape=jax.ShapeDtypeStruct((4, R, W), shard.dtype),
        scratch_shapes=[pltpu.SemaphoreType.DMA((6,)), pltpu.SemaphoreType.DMA((6,))],
    )(shard)


def _gather_chips(shard, name):
    R, W = shard.shape

    def body(src_ref, out_ref, send_sems, recv_sems, local_sem):
        x, y, c = lax.axis_index("x"), lax.axis_index("y"), lax.axis_index("c")
        me = 2 * x + y
        mine = pltpu.make_async_copy(src_ref, out_ref.at[me], local_sem)
        mine.start()

        def copy(j, slot, to):
            return pltpu.make_async_remote_copy(src_ref=src_ref, dst_ref=out_ref.at[slot], send_sem=send_sems.at[j],
                                                recv_sem=recv_sems.at[j], device_id=(to[0], to[1], c), device_id_type=MESH)

        sends = [copy(j, me, peer) for j, peer in enumerate(_peers(x, y))]
        for cp in sends:
            cp.start()
        for j, peer in enumerate(_peers(x, y)):
            copy(j, 2 * peer[0] + peer[1], peer).wait_recv()
        for cp in sends:
            cp.wait_send()
        mine.wait()

    return pl.pallas_call(
        body, name=name, in_specs=[_hbm()], out_specs=_hbm(), out_shape=jax.ShapeDtypeStruct((4, R, W), shard.dtype),
        scratch_shapes=[pltpu.SemaphoreType.DMA((3,)), pltpu.SemaphoreType.DMA((3,)), pltpu.SemaphoreType.DMA],
    )(shard)


def _pair_swap_halves(g4):
    n, R, W = g4.shape
    Rh = R // 2

    def body(src_ref, out_ref, send_sem, recv_sem):
        x, y, c = lax.axis_index("x"), lax.axis_index("y"), lax.axis_index("c")
        cp = pltpu.make_async_remote_copy(src_ref=src_ref.at[:, pl.ds((1 - c) * Rh, Rh), :], dst_ref=out_ref, send_sem=send_sem,
                                          recv_sem=recv_sem, device_id=(x, y, 1 - c), device_id_type=MESH)
        cp.start()
        cp.wait()

    return pl.pallas_call(
        body, name="grad_pair_swap", in_specs=[_hbm()], out_specs=_hbm(), out_shape=jax.ShapeDtypeStruct((n, Rh, W), g4.dtype),
        scratch_shapes=[pltpu.SemaphoreType.DMA, pltpu.SemaphoreType.DMA],
    )(g4)


def _pair_add(g4, sib, c_arr):
    n, R, W = g4.shape
    Rh = R // 2
    tr = _tile_rows(Rh)
    nb = Rh // tr

    def body(c_ref, a_ref, b_ref, o_ref):
        o_ref[...] = (a_ref[...].astype(F32) + b_ref[...].astype(F32)).astype(o_ref.dtype)

    return pl.pallas_call(
        body, name="grad_pair_add",
        grid_spec=pltpu.PrefetchScalarGridSpec(
            num_scalar_prefetch=1, grid=(n, nb),
            in_specs=[pl.BlockSpec((None, tr, W), lambda s, i, c: (s, c[0] * nb + i, 0)), pl.BlockSpec((None, tr, W), lambda s, i, c: (s, i, 0))],
            out_specs=pl.BlockSpec((None, tr, W), lambda s, i, c: (s, i, 0))),
        out_shape=jax.ShapeDtypeStruct((n, Rh, W), g4.dtype), compiler_params=_cparams(("parallel", "parallel")),
    )(c_arr, g4, sib)


def _tile_rows(n):
    for t in (512, 480, 400, 320, 256, 240, 160, 128, 80, 64, 40, 32, 16, 8):
        if n % t == 0:
            return t
    return n


def _chips_exchange(part):
    n, Rh, W = part.shape

    def body(src_ref, out_ref, send_sems, recv_sems):
        x, y, c = lax.axis_index("x"), lax.axis_index("y"), lax.axis_index("c")

        def copy(j, to):
            return pltpu.make_async_remote_copy(src_ref=src_ref.at[2 * to[0] + to[1]], dst_ref=out_ref.at[j], send_sem=send_sems.at[j],
                                                recv_sem=recv_sems.at[j], device_id=(to[0], to[1], c), device_id_type=MESH)

        cps = [copy(j, peer) for j, peer in enumerate(_peers(x, y))]
        for cp in cps:
            cp.start()
        for cp in cps:
            cp.wait()

    return pl.pallas_call(
        body, name="grad_chips_exchange", in_specs=[_hbm()], out_specs=_hbm(), out_shape=jax.ShapeDtypeStruct((3, Rh, W), part.dtype),
        scratch_shapes=[pltpu.SemaphoreType.DMA((3,)), pltpu.SemaphoreType.DMA((3,))],
    )(part)


def _chips_add(part, got, k_arr, c_arr):
    n, Rh, W = part.shape
    tr = _tile_rows(Rh)
    nb = Rh // tr

    def body(k_ref, c_ref, a_ref, b_ref, o_ref):
        mine = pl.program_id(0) == c_ref[0]

        @pl.when(mine)
        def _():
            o_ref[...] = ((a_ref[...].astype(F32) + b_ref[0].astype(F32)) + b_ref[1].astype(F32)) + b_ref[2].astype(F32)

        @pl.when(jnp.logical_not(mine))
        def _():
            o_ref[...] = jnp.zeros_like(o_ref)

    return pl.pallas_call(
        body, name="grad_chips_add",
        grid_spec=pltpu.PrefetchScalarGridSpec(
            num_scalar_prefetch=2, grid=(2, nb),
            in_specs=[pl.BlockSpec((None, tr, W), lambda h, i, k, c: (k[0], i, 0)), pl.BlockSpec((3, tr, W), lambda h, i, k, c: (0, i, 0))],
            out_specs=pl.BlockSpec((tr, W), lambda h, i, k, c: (h * nb + i, 0))),
        out_shape=jax.ShapeDtypeStruct((2 * Rh, W), F32), compiler_params=_cparams(("parallel", "parallel")),
    )(k_arr, c_arr, part, got)


def _pair_gather(buf):
    R, W = buf.shape
    Rh = R // 2

    def body(src_ref, out_ref, send_sem, recv_sem):
        x, y, c = lax.axis_index("x"), lax.axis_index("y"), lax.axis_index("c")
        mine, other = pl.ds(c * Rh, Rh), pl.ds((1 - c) * Rh, Rh)
        pltpu.make_async_remote_copy(src_ref=src_ref.at[mine], dst_ref=out_ref.at[mine], send_sem=send_sem, recv_sem=recv_sem,
                                     device_id=(x, y, 1 - c), device_id_type=MESH).start()
        pltpu.make_async_remote_copy(src_ref=src_ref.at[mine], dst_ref=out_ref.at[other], send_sem=send_sem, recv_sem=recv_sem,
                                     device_id=(x, y, 1 - c), device_id_type=MESH).wait()

    return pl.pallas_call(
        body, name="grad_pair_gather", in_specs=[_hbm()], out_specs=_hbm(), out_shape=jax.ShapeDtypeStruct((R, W), buf.dtype),
        input_output_aliases={0: 0}, scratch_shapes=[pltpu.SemaphoreType.DMA, pltpu.SemaphoreType.DMA],
    )(buf)


def _gather_all(buf):
    R, W = buf.shape

    def body(src_ref, out_ref, send_sems, recv_sems, local_sem):
        x, y, c = lax.axis_index("x"), lax.axis_index("y"), lax.axis_index("c")
        me = 4 * x + 2 * y + c
        mine = pltpu.make_async_copy(src_ref, out_ref.at[me], local_sem)
        mine.start()
        rel = [((x + (r >> 2 & 1)) % 2, (y + (r >> 1 & 1)) % 2, (c + (r & 1)) % 2) for r in range(1, 8)]

        def copy(j, slot, to):
            return pltpu.make_async_remote_copy(src_ref=src_ref, dst_ref=out_ref.at[slot], send_sem=send_sems.at[j],
                                                recv_sem=recv_sems.at[j], device_id=to, device_id_type=MESH)

        sends = [copy(j, me, to) for j, to in enumerate(rel)]
        for cp in sends:
            cp.start()
        for j, to in enumerate(rel):
            copy(j, 4 * to[0] + 2 * to[1] + to[2], to).wait_recv()
        for cp in sends:
            cp.wait_send()
        mine.wait()

    return pl.pallas_call(
        body, name="small_gather", in_specs=[_hbm()], out_specs=_hbm(), out_shape=jax.ShapeDtypeStruct((8, R, W), buf.dtype),
        scratch_shapes=[pltpu.SemaphoreType.DMA((7,)), pltpu.SemaphoreType.DMA((7,)), pltpu.SemaphoreType.DMA],
    )(buf)


def _sum_slots(stack):
    n, R, W = stack.shape
    tr = _tile_rows(R)

    def body(s_ref, o_ref):
        tot = s_ref[0]
        for j in range(1, n):
            tot = tot + s_ref[j]
        o_ref[...] = tot

    return pl.pallas_call(
        body, name="small_sum", grid=(R // tr,), in_specs=[pl.BlockSpec((n, tr, W), lambda i: (0, i, 0))],
        out_specs=pl.BlockSpec((tr, W), lambda i: (i, 0)), out_shape=jax.ShapeDtypeStruct((R, W), F32),
        compiler_params=_cparams(("parallel",)),
    )(stack)


def _adamw(wp, gp, mp, vp, name):
    R, W = wp.shape
    tr = R
    for t in (1024, 512, 256, 128, 64, 32, 16, 8):
        if R % t == 0 and t * W <= 512 * 1024:
            tr = t
            break
    c1 = 1.0 - ADAM_B1 ** ADAM_STEP
    c2 = 1.0 - ADAM_B2 ** ADAM_STEP

    def body(w_ref, g_ref, m_ref, v_ref, d_ref, mo_ref, vo_ref):
        gv = g_ref[...]
        m = ADAM_B1 * m_ref[...] + (1.0 - ADAM_B1) * gv
        v = ADAM_B2 * v_ref[...] + (1.0 - ADAM_B2) * (gv * gv)
        m_hat = m / c1
        v_hat = v / c2
        d_ref[...] = -ADAM_LR * (m_hat / (jnp.sqrt(v_hat) + ADAM_EPS) + ADAM_WD * w_ref[...])
        mo_ref[...] = m
        vo_ref[...] = v

    blk = pl.BlockSpec((tr, W), lambda i: (i, 0))
    shp = jax.ShapeDtypeStruct((R, W), F32)
    return pl.pallas_call(body, name=name, grid=(R // tr,), in_specs=[blk] * 4, out_specs=(blk,) * 3, out_shape=(shp,) * 3,
                          compiler_params=_cparams(("parallel",)))(wp, gp, mp, vp)


def _pack(arrs, rows):
    flat = jnp.concatenate([a.reshape(-1) for a in arrs])
    return jnp.pad(flat, (0, rows * PACK_W - flat.shape[0])).reshape(rows, PACK_W)


def _unpack(buf, shapes):
    flat = buf.reshape(-1)
    out, off = [], 0
    for shp in shapes:
        n = int(np.prod(shp))
        out.append(flat[off:off + n].reshape(shp))
        off += n
    return out


def _rows_for(shapes, mult):
    n = sum(int(np.prod(s)) for s in shapes)
    rows = -(-n // PACK_W)
    return -(-rows // mult) * mult


def _shard_major(g, name):
    L, K, N = g.shape
    if name in _ROW_SHARDED:
        t = g.reshape(L, 4, K // 4, N).transpose(1, 0, 2, 3)
    else:
        t = g.reshape(L, K, 4, N // 4).transpose(2, 0, 1, 3)
    return t.reshape(4, -1, PACK_W)


def _join_shards(blocks, name):
    return jnp.concatenate(blocks, axis=1 if name in _ROW_SHARDED else 2)


def kernel(x, p, mix_norm, w_in, gate_b, conv_w, conv_b, lru_wa, lru_ba, lru_wx, lru_bx, lru_lambda, mla_q_norm, mla_wuq, mla_kv_norm, mla_wukv, fox_bf, w_br_a, w_br_b, w_br_c, w_o, ffn_norm, w_gate_up, w_down, ple_norm, w_ple_gate, w_ple, final_norm, loss_target, m_mix_norm, m_w_in, m_gate_b, m_conv_w, m_conv_b, m_lru_wa, m_lru_ba, m_lru_wx, m_lru_bx, m_lru_lambda, m_mla_q_norm, m_mla_wuq, m_mla_kv_norm, m_mla_wukv, m_fox_bf, m_w_br_a, m_w_br_b, m_w_br_c, m_w_o, m_ffn_norm, m_w_gate_up, m_w_down, m_ple_norm, m_w_ple_gate, m_w_ple, m_final_norm, v_mix_norm, v_w_in, v_gate_b, v_conv_w, v_conv_b, v_lru_wa, v_lru_ba, v_lru_wx, v_lru_bx, v_lru_lambda, v_mla_q_norm, v_mla_wuq, v_mla_kv_norm, v_mla_wukv, v_fox_bf, v_w_br_a, v_w_br_b, v_w_br_c, v_w_o, v_ffn_norm, v_w_gate_up, v_w_down, v_ple_norm, v_w_ple_gate, v_w_ple, v_final_norm):
    a = dict(locals())
    names = list(_LAYER_WEIGHTS) + ["final_norm"]
    W = {n: a[n] for n in names}
    M = {n: a["m_" + n] for n in names}
    V = {n: a["v_" + n] for n in names}
    ix, iy, ic = lax.axis_index("x"), lax.axis_index("y"), lax.axis_index("c")

    sharded = list(_BIG) + ["conv_w"]
    shard_shapes = [W[n].shape for n in sharded]
    R = _rows_for(shard_shapes, 64)
    w_bf = _pack([W[n].astype(BF16) for n in sharded], R)
    gathered = _gather_chips_two_level(w_bf, "weight_gather")
    gathered = lax.dynamic_update_slice(gathered, w_bf[None], (2 * ix + iy, 0, 0))
    per_chip = [_unpack(gathered[k], shard_shapes) for k in range(4)]
    full = {n: _join_shards([per_chip[k][j] for k in range(4)], n) for j, n in enumerate(_BIG)}
    conv_blocks = _gather_chips(conv_w.reshape(DEPTH * CONV_WIDTH, LANES), "conv_w_gather")
    conv_w_full = jnp.concatenate([conv_blocks[k].reshape(DEPTH, CONV_WIDTH, LANES) for k in range(4)], axis=-1)
    layers = []
    for i in range(DEPTH):
        lw = {n: W[n][i] for n in _SMALL}
        for n in _BIG:
            lw[n] = full[n][i]
        lw["conv_w"] = conv_w_full[i]
        layers.append(lw)

    loss_sum, dx, grads, d_final = _local_step(x[0], p[:, 0], layers, final_norm, loss_target[0])
    loss = lax.psum(loss_sum, ("x", "y", "c"))

    parts = [_shard_major(jnp.stack([grads[i][n] for i in range(DEPTH)]), n).astype(BF16) for n in sharded]
    g4, off = jnp.zeros((4, R, PACK_W), BF16), 0
    for t in parts:
        g4 = lax.dynamic_update_slice(g4, t, (0, off, 0))
        off += t.shape[1]
    c_arr = jnp.reshape(ic, (1,)).astype(jnp.int32)
    k_arr = jnp.reshape(2 * ix + iy, (1,)).astype(jnp.int32)
    pair = _pair_add(g4, _pair_swap_halves(g4), c_arr)
    g_pack = _pair_gather(_chips_add(pair, _chips_exchange(pair), k_arr, c_arr))
    big_out = {}
    for n, gsh in zip(sharded, _unpack(g_pack, shard_shapes)):
        view = lambda t: t.reshape(-1, t.shape[-1])
        d, nm, nv = _adamw(view(W[n]), view(gsh), view(M[n]), view(V[n]), "adamw_" + n)
        for key, arr in (("g", gsh), ("d", d), ("m", nm), ("v", nv)):
            big_out[(key, n)] = arr.reshape(W[n].shape)

    pick = lambda src, n, i: src[n] if i is None else src[n][i]
    small = [(n, i) for i in range(DEPTH) for n in _SMALL] + [("final_norm", None)]
    small_shapes = [pick(W, n, i).shape for n, i in small]
    Rs = _rows_for(small_shapes, 8)
    sg = _pack([d_final if i is None else grads[i][n] for n, i in small], Rs)
    sg = _sum_slots(_gather_all(sg))
    sw = _pack([pick(W, n, i) for n, i in small], Rs)
    sm = _pack([pick(M, n, i) for n, i in small], Rs)
    sv_ = _pack([pick(V, n, i) for n, i in small], Rs)
    sd, snm, snv = _adamw(sw, sg, sm, sv_, "adamw_replicated")
    small_out = {}
    for key, buf in (("g", sg), ("d", sd), ("m", snm), ("v", snv)):
        for (n, i), arr in zip(small, _unpack(buf, small_shapes)):
            small_out[(key, n, i)] = arr

    def assemble(key, n):
        if n == "final_norm":
            return small_out[(key, n, None)]
        if n in sharded:
            return big_out[(key, n)]
        return jnp.stack([small_out[(key, n, i)] for i in range(DEPTH)])

    outs = [loss, dx[None]]
    for key in ("g", "d", "m", "v"):
        outs += [assemble(key, n) for n in names]
    return tuple(outs)
```

```python
import functools
import math

import numpy as np
import jax
import jax.numpy as jnp
from jax import lax
from jax.experimental import pallas as pl
from jax.experimental.pallas import tpu as pltpu

F32, BF16 = jnp.float32, jnp.bfloat16
MESH = pl.DeviceIdType.MESH

D_MODEL = 1024
DEPTH = 2
EPS = 1e-6
NEG_INF = -1e30
LRU_WIDTH = 512
LRU_HEADS = 8
LRU_C = 8.0
CONV_WIDTH = 4
HEADS = 8
MLA_Q_LORA = 384
MLA_KV_LORA = 256
MLA_NOPE = 64
MLA_ROPE = 32
MLA_V = 64
ROPE_BASE = 10000.0
FOX_HEAD_DIM = 64
D_FF = 2816
PLE_DIM = 256
HEAD_PAD = 128
MLA_SCALE = (MLA_NOPE + MLA_ROPE) ** -0.5
FOX_SCALE = FOX_HEAD_DIM ** -0.5

ADAM_LR, ADAM_B1, ADAM_B2, ADAM_EPS, ADAM_WD, ADAM_STEP = 0.001, 0.9, 0.999, 1e-08, 0.01, 10

VMEM_LIMIT_BYTES = 48 * 1024 * 1024
LANES = 128
PACK_W = 1024

ROW_TILE = 512
ATTN_TILE = 1024
LRU_CHUNK = 512


def _cparams(dims):
    return pltpu.CompilerParams(dimension_semantics=dims, vmem_limit_bytes=VMEM_LIMIT_BYTES)


def _tile(n, cap):
    if n <= cap:
        return n
    t = (cap // LANES) * LANES
    while t >= LANES:
        if n % t == 0:
            return t
        t -= LANES
    raise ValueError(f"no tile for {n} under {cap}")


def _rows(n):
    return min(ROW_TILE, n)


MM_VMEM_BUDGET = 36 * 1024 * 1024


def _mm_tiles(M, N, K, a_bytes, b_bytes, o_bytes, has_res):
    best, best_work = None, 0
    for tm in {_tile(M, c) for c in (1024, 512, 256)}:
        for tn in {_tile(N, c) for c in (1792, 1024, 512)}:
            for tk in {_tile(K, c) for c in (2048, 1408, 1024, 512)}:
                need = 2 * (tm * tk * a_bytes + tk * tn * b_bytes + tm * tn * o_bytes + (tm * tn * 4 if has_res else 0))
                need += tm * tn * 4 if tk < K else 0
                need += tm * tn * 4
                if need <= MM_VMEM_BUDGET and tm * tn * tk > best_work:
                    best, best_work = (tm, tn, tk), tm * tn * tk
    assert best is not None, (M, N, K)
    return best

def _mm(a, b, *, ta=False, tb=False, out_dtype=F32, res=None, bias=None, name):
    K, M = a.shape if ta else a.shape[::-1]
    N, K2 = b.shape if tb else b.shape[::-1]
    assert K == K2, (name, a.shape, b.shape)
    assert res is None or bias is None
    tm, tn, tk = _mm_tiles(M, N, K, a.dtype.itemsize, b.dtype.itemsize, jnp.dtype(out_dtype).itemsize, res is not None)
    nk = K // tk
    a_spec = pl.BlockSpec((tk, tm), lambda i, j, k: (k, i)) if ta else pl.BlockSpec((tm, tk), lambda i, j, k: (i, k))
    b_spec = pl.BlockSpec((tn, tk), lambda i, j, k: (j, k)) if tb else pl.BlockSpec((tk, tn), lambda i, j, k: (k, j))
    o_spec = pl.BlockSpec((tm, tn), lambda i, j, k: (i, j))
    dn = (((0,) if ta else (1,), (1,) if tb else (0,)), ((), ()))
    if bias is not None:
        res, r_spec = bias, pl.BlockSpec((1, tn), lambda i, j, k: (0, j))
    else:
        r_spec = o_spec
    has_res = res is not None

    def body(*refs):
        a_ref, b_ref = refs[0], refs[1]
        r_ref = refs[2] if has_res else None
        o_ref = refs[3] if has_res else refs[2]
        av, bv = a_ref[...], b_ref[...]
        if av.dtype != BF16:
            av = av.astype(BF16)
        if bv.dtype != BF16:
            bv = bv.astype(BF16)
        part = lax.dot_general(av, bv, dn, preferred_element_type=F32)

        def finish(total):
            if has_res:
                total = total + r_ref[...]
            o_ref[...] = total.astype(out_dtype)

        if nk == 1:
            finish(part)
        else:
            acc = refs[-1]
            k = pl.program_id(2)

            @pl.when(k == 0)
            def _():
                acc[...] = part

            @pl.when(k > 0)
            def _():
                acc[...] += part

            @pl.when(k == nk - 1)
            def _():
                finish(acc[...])

    in_specs = [a_spec, b_spec] + ([r_spec] if has_res else [])
    args = (a, b) + ((res,) if has_res else ())
    return pl.pallas_call(
        body, name=name, grid=(M // tm, N // tn, nk), in_specs=in_specs, out_specs=o_spec,
        out_shape=jax.ShapeDtypeStruct((M, N), out_dtype),
        scratch_shapes=[pltpu.VMEM((tm, tn), F32)] if nk > 1 else [],
        compiler_params=_cparams(("parallel", "parallel", "arbitrary")),
    )(*args)


def _mm_res_norm(a, b, res, g, name):
    M, K = a.shape
    N = b.shape[1]
    tm, tk = _tile(M, 512), _tile(K, 1408)
    nk = K // tk

    def body(a_ref, b_ref, r_ref, g_ref, o_ref, h_ref, *scratch):
        part = jnp.dot(a_ref[...], b_ref[...], preferred_element_type=F32)

        def finish(total):
            xn = total + r_ref[...]
            o_ref[...] = xn
            rstd = lax.rsqrt(jnp.mean(xn * xn, axis=1, keepdims=True) + EPS)
            h_ref[...] = (xn * rstd * g_ref[...]).astype(BF16)

        if nk == 1:
            finish(part)
        else:
            acc = scratch[0]
            k = pl.program_id(1)

            @pl.when(k == 0)
            def _():
                acc[...] = part

            @pl.when(k > 0)
            def _():
                acc[...] += part

            @pl.when(k == nk - 1)
            def _():
                finish(acc[...])

    row = pl.BlockSpec((tm, N), lambda i, k: (i, 0))
    return pl.pallas_call(
        body, name=name, grid=(M // tm, nk),
        in_specs=[pl.BlockSpec((tm, tk), lambda i, k: (i, k)), pl.BlockSpec((tk, N), lambda i, k: (k, 0)), row,
                  pl.BlockSpec((1, N), lambda i, k: (0, 0))],
        out_specs=(row, row), out_shape=(jax.ShapeDtypeStruct((M, N), F32), jax.ShapeDtypeStruct((M, N), BF16)),
        scratch_shapes=[pltpu.VMEM((tm, N), F32)] if nk > 1 else [],
        compiler_params=_cparams(("parallel", "arbitrary")),
    )(a, b, res, g.reshape(1, N))


FFN_TILE = 1408


def _ffn_pair_columns(w_gate_up):
    F = w_gate_up.shape[-1] // 2
    parts = []
    for j in range(F // FFN_TILE):
        parts += [w_gate_up[..., j * FFN_TILE:(j + 1) * FFN_TILE], w_gate_up[..., F + j * FFN_TILE:F + (j + 1) * FFN_TILE]]
    return jnp.concatenate(parts, axis=-1)


def _ffn_unpair_columns(dw):
    F = dw.shape[-1] // 2
    n = F // FFN_TILE
    blk = [dw[..., j * FFN_TILE:(j + 1) * FFN_TILE] for j in range(2 * n)]
    return jnp.concatenate(blk[0::2] + blk[1::2], axis=-1)


def _ffn_up(hn, w_pair):
    S, D = hn.shape
    W2 = w_pair.shape[1]
    F, tf = W2 // 2, FFN_TILE
    tm = _rows(S)

    def body(h_ref, w_ref, hf_ref, act_ref):
        hf = jnp.dot(h_ref[...], w_ref[...], preferred_element_type=F32)
        hf_ref[...] = hf
        gt, up = hf[:, 0:tf], hf[:, tf:2 * tf]
        act_ref[...] = (gt * jax.nn.sigmoid(gt) * up).astype(BF16)

    return pl.pallas_call(
        body, name="ffn_up", grid=(S // tm, F // tf),
        in_specs=[pl.BlockSpec((tm, D), lambda i, j: (i, 0)), pl.BlockSpec((D, 2 * tf), lambda i, j: (0, j))],
        out_specs=(pl.BlockSpec((tm, 2 * tf), lambda i, j: (i, j)), pl.BlockSpec((tm, tf), lambda i, j: (i, j))),
        out_shape=(jax.ShapeDtypeStruct((S, W2), F32), jax.ShapeDtypeStruct((S, F), BF16)),
        compiler_params=_cparams(("parallel", "parallel")),
    )(hn, w_pair)


def _ffn_down_bwd(dx, w_down, hf):
    S, D = dx.shape
    F, tf = w_down.shape[0], FFN_TILE
    tm = _rows(S)

    def body(d_ref, w_ref, h_ref, o_ref):
        dact = lax.dot_general(d_ref[...].astype(BF16), w_ref[...], _NT, preferred_element_type=F32)
        gt, up = h_ref[:, 0:tf], h_ref[:, tf:2 * tf]
        sg = jax.nn.sigmoid(gt)
        o_ref[:, 0:tf] = (dact * up * sg * (1.0 + gt * (1.0 - sg))).astype(BF16)
        o_ref[:, tf:2 * tf] = (dact * gt * sg).astype(BF16)

    pair = pl.BlockSpec((tm, 2 * tf), lambda i, j: (i, j))
    return pl.pallas_call(
        body, name="ffn_down_bwd", grid=(S // tm, F // tf),
        in_specs=[pl.BlockSpec((tm, D), lambda i, j: (i, 0)), pl.BlockSpec((tf, D), lambda i, j: (j, 0)), pair],
        out_specs=pair, out_shape=jax.ShapeDtypeStruct((S, 2 * F), BF16),
        compiler_params=_cparams(("parallel", "parallel")),
    )(dx, w_down, hf)


def _rmsnorm_fwd(x, g, name):
    S, W = x.shape
    tm = _rows(S)

    def body(x_ref, g_ref, o_ref):
        xf = x_ref[...]
        rstd = lax.rsqrt(jnp.mean(xf * xf, axis=1, keepdims=True) + EPS)
        o_ref[...] = (xf * rstd * g_ref[...]).astype(BF16)

    return pl.pallas_call(
        body, name=name, grid=(S // tm,),
        in_specs=[pl.BlockSpec((tm, W), lambda i: (i, 0)), pl.BlockSpec((1, W), lambda i: (0, 0))],
        out_specs=pl.BlockSpec((tm, W), lambda i: (i, 0)),
        out_shape=jax.ShapeDtypeStruct((S, W), BF16), compiler_params=_cparams(("parallel",)),
    )(x, g.reshape(1, W))


def _rmsnorm_bwd(x, g, dy, *, add=None, out_dtype=F32, name):
    S, W = x.shape
    tm = _rows(S)
    has_add = add is not None

    def body(*refs):
        x_ref, g_ref, dy_ref = refs[:3]
        add_ref = refs[3] if has_add else None
        dx_ref, dg_ref = refs[-2], refs[-1]
        xf = x_ref[...]
        rstd = lax.rsqrt(jnp.mean(xf * xf, axis=1, keepdims=True) + EPS)
        xhat = xf * rstd
        dyv = dy_ref[...]
        dxh = dyv * g_ref[...]
        dx = rstd * (dxh - xhat * jnp.mean(dxh * xhat, axis=1, keepdims=True))
        if has_add:
            dx = dx + add_ref[...]
        dx_ref[...] = dx.astype(out_dtype)

        @pl.when(pl.program_id(0) == 0)
        def _():
            dg_ref[...] = jnp.zeros_like(dg_ref)

        dg_ref[...] += jnp.sum(dyv * xhat, axis=0, keepdims=True)

    row = pl.BlockSpec((tm, W), lambda i: (i, 0))
    vec = pl.BlockSpec((1, W), lambda i: (0, 0))
    dx, dg = pl.pallas_call(
        body, name=name, grid=(S // tm,),
        in_specs=[row, vec, row] + ([row] if has_add else []),
        out_specs=(row, vec),
        out_shape=(jax.ShapeDtypeStruct((S, W), out_dtype), jax.ShapeDtypeStruct((1, W), F32)),
        compiler_params=_cparams(("arbitrary",)),
    )(x, g.reshape(1, W), dy, *((add,) if has_add else ()))
    return dx, dg.reshape(W)


def _loss_head(x, g, target):
    S, W = x.shape
    tm = _rows(S)

    def body(x_ref, g_ref, t_ref, loss_ref, dx_ref, dg_ref):
        xf = x_ref[...]
        gv = g_ref[...]
        rstd = lax.rsqrt(jnp.mean(xf * xf, axis=1, keepdims=True) + EPS)
        xhat = xf * rstd
        err = xhat * gv - t_ref[...]
        part = 0.5 * jnp.sum(jnp.mean(err * err, axis=1, keepdims=True), axis=0, keepdims=True)
        dyv = err * (1.0 / W)
        dxh = dyv * gv
        dx_ref[...] = rstd * (dxh - xhat * jnp.mean(dxh * xhat, axis=1, keepdims=True))

        @pl.when(pl.program_id(0) == 0)
        def _():
            dg_ref[...] = jnp.zeros_like(dg_ref)
            loss_ref[...] = jnp.zeros_like(loss_ref)

        dg_ref[...] += jnp.sum(dyv * xhat, axis=0, keepdims=True)
        loss_ref[...] += part

    row = pl.BlockSpec((tm, W), lambda i: (i, 0))
    vec = pl.BlockSpec((1, W), lambda i: (0, 0))
    loss, dx, dg = pl.pallas_call(
        body, name="loss_head", grid=(S // tm,), in_specs=[row, vec, row],
        out_specs=(pl.BlockSpec((1, 1), lambda i: (0, 0)), row, vec),
        out_shape=(jax.ShapeDtypeStruct((1, 1), F32), jax.ShapeDtypeStruct((S, W), F32), jax.ShapeDtypeStruct((1, W), F32)),
        compiler_params=_cparams(("arbitrary",)),
    )(x, g.reshape(1, W), target)
    return loss[0, 0], dx, dg.reshape(W)


def _scan_fwd(a, b, row):
    T = a.shape[0]
    d = 1
    while d < T:
        keep = row >= d
        b = jnp.where(keep, a * pltpu.roll(b, d, axis=0) + b, b)
        a = jnp.where(keep, a * pltpu.roll(a, d, axis=0), a)
        d *= 2
    return a, b


def _scan_bwd(a, b, row):
    T = a.shape[0]
    d = 1
    while d < T:
        keep = row < T - d
        b = jnp.where(keep, a * pltpu.roll(b, T - d, axis=0) + b, b)
        a = jnp.where(keep, a * pltpu.roll(a, T - d, axis=0), a)
        d *= 2
    return a, b


def _expm1(x):
    small = x * (1.0 + x * (0.5 + x * (1.0 / 6 + x * (1.0 / 24 + x * (1.0 / 120 + x * (1.0 / 720 + x * (1.0 / 5040)))))))
    return jnp.where(jnp.abs(x) < 0.25, small, jnp.exp(x) - 1.0)


_GELU_C = math.sqrt(2.0 / math.pi)


def _gelu_and_grad(x):
    inner = _GELU_C * (x + 0.044715 * x * x * x)
    th = jnp.tanh(inner)
    val = 0.5 * x * (1.0 + th)
    grad = 0.5 * (1.0 + th) + 0.5 * x * (1.0 - th * th) * _GELU_C * (1.0 + 3 * 0.044715 * x * x)
    return val, grad


def _lru_gates(xc, wa, wx, ba, bx, lam):
    xcb = xc.astype(BF16)
    r = jax.nn.sigmoid(jnp.dot(xcb, wa, preferred_element_type=F32) + ba)
    ig = jax.nn.sigmoid(jnp.dot(xcb, wx, preferred_element_type=F32) + bx)
    sp = jax.nn.softplus(-lam)
    log_a = -LRU_C * r * sp
    a = jnp.exp(log_a)
    mult = jnp.sqrt(-_expm1(2.0 * log_a))
    return xcb, r, ig, sp, a, mult


def _lru_fwd(u, ug, conv_w, conv_b, wa_bd, wx_bd, ba, bx, lam):
    S, W = u.shape
    T = min(LRU_CHUNK, S)
    nl, nc = W // LANES, S // T

    def body(u_ref, ug_ref, cw_ref, cb_ref, wa_ref, wx_ref, ba_ref, bx_ref, lam_ref, ya_ref, xc_ref, h_ref, prev_u, h_carry):
        c = pl.program_id(1)

        @pl.when(c == 0)
        def _():
            prev_u[...] = jnp.zeros_like(prev_u)
            h_carry[...] = jnp.zeros_like(h_carry)

        uv = u_ref[...]
        row = lax.broadcasted_iota(jnp.int32, (T, LANES), 0)
        row8 = lax.broadcasted_iota(jnp.int32, (8, LANES), 0)
        cw = cw_ref[...]
        xc = cb_ref[...] + uv * cw[3:4, :]
        pv = prev_u[...]
        for k in range(1, CONV_WIDTH):
            us = pltpu.roll(uv, k, axis=0)
            top = jnp.where(row8 < k, pltpu.roll(pv, k, axis=0), us[0:8])
            us = jnp.concatenate([top, us[8:]], axis=0)
            xc = xc + us * cw[3 - k:4 - k, :]
        prev_u[...] = uv[T - 8:T]
        _, r, ig, sp, a, mult = _lru_gates(xc, wa_ref[...], wx_ref[...], ba_ref[...], bx_ref[...], lam_ref[...])
        bb = mult * (ig * xc)
        aa, hh = _scan_fwd(a, bb, row)
        h = hh + aa * h_carry[7:8, :]
        h_carry[...] = h[T - 8:T]
        gl, _ = _gelu_and_grad(ug_ref[...])
        ya_ref[...] = (h * gl).astype(BF16)
        xc_ref[...] = xc
        h_ref[...] = h

    seq = pl.BlockSpec((T, LANES), lambda l, c: (c, l))
    vec = pl.BlockSpec((1, LANES), lambda l, c: (0, l))
    mat = pl.BlockSpec((None, LANES, LANES), lambda l, c: (l, 0, 0))
    return pl.pallas_call(
        body, name="lru_fwd", grid=(nl, nc),
        in_specs=[seq, seq, pl.BlockSpec((CONV_WIDTH, LANES), lambda l, c: (0, l)), vec, mat, mat, vec, vec, vec],
        out_specs=(seq, seq, seq),
        out_shape=(jax.ShapeDtypeStruct((S, W), BF16), jax.ShapeDtypeStruct((S, W), F32), jax.ShapeDtypeStruct((S, W), F32)),
        scratch_shapes=[pltpu.VMEM((8, LANES), F32), pltpu.VMEM((8, LANES), F32)],
        compiler_params=_cparams(("parallel", "arbitrary")),
    )(u, ug, conv_w, conv_b.reshape(1, W), wa_bd, wx_bd, ba.reshape(1, W), bx.reshape(1, W), lam.reshape(1, W))


def _lru_bwd(dya, u, ug, xc, h, conv_w, wa_bd, wx_bd, ba, bx, lam):
    S, W = u.shape
    T = min(LRU_CHUNK, S)
    nl, nc = W // LANES, S // T
    tb8 = T // 8

    def body(dya_ref, u_ref, ug_ref, xc_ref, h_ref, hp_ref, cw_ref, wa_ref, wx_ref, ba_ref, bx_ref, lam_ref,
             du_ref, dug_ref, dcw_ref, dcb_ref, dba_ref, dbx_ref, dlam_ref, dwa_ref, dwx_ref,
             g_next, a_next, dxc_next):
        c = pl.program_id(1)

        @pl.when(c == 0)
        def _():
            g_next[...] = jnp.zeros_like(g_next)
            a_next[...] = jnp.zeros_like(a_next)
            dxc_next[...] = jnp.zeros_like(dxc_next)
            for ref in (dcw_ref, dcb_ref, dba_ref, dbx_ref, dlam_ref, dwa_ref, dwx_ref):
                ref[...] = jnp.zeros_like(ref)

        row = lax.broadcasted_iota(jnp.int32, (T, LANES), 0)
        row8 = lax.broadcasted_iota(jnp.int32, (8, LANES), 0)
        xcv = xc_ref[...]
        wa, wx = wa_ref[...], wx_ref[...]
        xcb, r, ig, sp, a, mult = _lru_gates(xcv, wa, wx, ba_ref[...], bx_ref[...], lam_ref[...])
        gl, dgl = _gelu_and_grad(ug_ref[...])
        dyav = dya_ref[...]
        hv = h_ref[...]
        dug_ref[...] = (dyav * hv * dgl).astype(BF16)
        dh = dyav * gl
        a_up = pltpu.roll(a, T - 1, axis=0)
        a_up = jnp.where(row == T - 1, a_next[0:1, :], a_up)
        prod, gg = _scan_bwd(a_up, dh, row)
        g = gg + prod * g_next[0:1, :]
        h_prev = pltpu.roll(hv, 1, axis=0)
        first_chunk = c == nc - 1
        h_before = jnp.where(first_chunk, 0.0, hp_ref[7:8, :])
        h_prev = jnp.where(row == 0, h_before, h_prev)
        da = g * h_prev
        d_mult = g * (ig * xcv)
        d_ig = g * mult * xcv
        dxc = g * mult * ig
        d_log_a = da * a - d_mult * (a * a) / mult
        d_r = d_log_a * (-LRU_C * sp)
        d_pa = d_r * r * (1.0 - r)
        d_px = d_ig * ig * (1.0 - ig)
        d_pab, d_pxb = d_pa.astype(BF16), d_px.astype(BF16)
        nt = (((1,), (1,)), ((), ()))
        tn = (((0,), (0,)), ((), ()))
        dxc = dxc + lax.dot_general(d_pab, wa, nt, preferred_element_type=F32) + lax.dot_general(d_pxb, wx, nt, preferred_element_type=F32)
        dwa_ref[...] += lax.dot_general(xcb, d_pab, tn, preferred_element_type=F32)
        dwx_ref[...] += lax.dot_general(xcb, d_pxb, tn, preferred_element_type=F32)
        dlam_ref[...] += jnp.sum(d_log_a * r, axis=0, keepdims=True)
        dba_ref[...] += jnp.sum(d_pa, axis=0, keepdims=True)
        dbx_ref[...] += jnp.sum(d_px, axis=0, keepdims=True)
        dcb_ref[...] += jnp.sum(dxc, axis=0, keepdims=True)
        uv = u_ref[...]
        cw = cw_ref[...]
        nxt = dxc_next[...]
        du = dxc * cw[3:4, :]
        dcw_ref[3:4, :] += jnp.sum(uv * dxc, axis=0, keepdims=True)
        for k in range(1, CONV_WIDTH):
            ds = pltpu.roll(dxc, T - k, axis=0)
            bot = jnp.where(row8 >= 8 - k, pltpu.roll(nxt, 8 - k, axis=0), ds[T - 8:T])
            ds = jnp.concatenate([ds[:T - 8], bot], axis=0)
            du = du + ds * cw[3 - k:4 - k, :]
            dcw_ref[3 - k:4 - k, :] += jnp.sum(uv * ds, axis=0, keepdims=True)
        du_ref[...] = du.astype(BF16)
        g_next[...] = g[0:8]
        a_next[...] = a[0:8]
        dxc_next[...] = dxc[0:8]

    seq = pl.BlockSpec((T, LANES), lambda l, c: (nc - 1 - c, l))
    before = pl.BlockSpec((8, LANES), lambda l, c: (jnp.maximum((nc - 1 - c) * tb8 - 1, 0), l))
    vec = pl.BlockSpec((1, LANES), lambda l, c: (0, l))
    cwb = pl.BlockSpec((CONV_WIDTH, LANES), lambda l, c: (0, l))
    mat = pl.BlockSpec((None, LANES, LANES), lambda l, c: (l, 0, 0))
    vshape = jax.ShapeDtypeStruct((1, W), F32)
    mshape = jax.ShapeDtypeStruct((nl, LANES, LANES), F32)
    return pl.pallas_call(
        body, name="lru_bwd", grid=(nl, nc),
        in_specs=[seq, seq, seq, seq, seq, before, cwb, mat, mat, vec, vec, vec],
        out_specs=(seq, seq, cwb, vec, vec, vec, vec, mat, mat),
        out_shape=(jax.ShapeDtypeStruct((S, W), BF16), jax.ShapeDtypeStruct((S, W), BF16),
                   jax.ShapeDtypeStruct((CONV_WIDTH, W), F32), vshape, vshape, vshape, vshape, mshape, mshape),
        scratch_shapes=[pltpu.VMEM((8, LANES), F32)] * 3,
        compiler_params=_cparams(("parallel", "arbitrary")),
    )(dya, u, ug, xc, h, h, conv_w, wa_bd, wx_bd, ba.reshape(1, W), bx.reshape(1, W), lam.reshape(1, W))


def _decay_fwd(f_logit, bf):
    S = f_logit.shape[0]
    T = min(LRU_CHUNK, S)

    def body(f_ref, b_ref, o_ref, carry):
        @pl.when(pl.program_id(0) == 0)
        def _():
            carry[...] = jnp.zeros_like(carry)

        row = lax.broadcasted_iota(jnp.int32, (T, LANES), 0)
        v = jax.nn.log_sigmoid(f_ref[...] + b_ref[...])
        d = 1
        while d < T:
            v = jnp.where(row >= d, v + pltpu.roll(v, d, axis=0), v)
            d *= 2
        v = v + carry[7:8, :]
        carry[...] = v[T - 8:T]
        o_ref[...] = v

    return pl.pallas_call(
        body, name="decay_fwd", grid=(S // T,),
        in_specs=[pl.BlockSpec((T, LANES), lambda c: (c, 0)), pl.BlockSpec((1, LANES), lambda c: (0, 0))],
        out_specs=pl.BlockSpec((T, LANES), lambda c: (c, 0)),
        out_shape=jax.ShapeDtypeStruct((S, LANES), F32), scratch_shapes=[pltpu.VMEM((8, LANES), F32)],
        compiler_params=_cparams(("arbitrary",)),
    )(f_logit, bf)


def _decay_bwd(d_dec, f_logit, bf):
    S = f_logit.shape[0]
    T = min(LRU_CHUNK, S)
    nc = S // T

    def body(dd_ref, f_ref, b_ref, df_ref, db_ref, carry):
        @pl.when(pl.program_id(0) == 0)
        def _():
            carry[...] = jnp.zeros_like(carry)
            db_ref[...] = jnp.zeros_like(db_ref)

        row = lax.broadcasted_iota(jnp.int32, (T, LANES), 0)
        v = dd_ref[...]
        d = 1
        while d < T:
            v = jnp.where(row < T - d, v + pltpu.roll(v, T - d, axis=0), v)
            d *= 2
        v = v + carry[0:1, :]
        carry[...] = v[0:8]
        df = v * jax.nn.sigmoid(-(f_ref[...] + b_ref[...]))
        df_ref[...] = df.astype(BF16)
        db_ref[...] += jnp.sum(df, axis=0, keepdims=True)

    seq = pl.BlockSpec((T, LANES), lambda c: (nc - 1 - c, 0))
    vec = pl.BlockSpec((1, LANES), lambda c: (0, 0))
    return pl.pallas_call(
        body, name="decay_bwd", grid=(nc,), in_specs=[seq, seq, vec], out_specs=(seq, vec),
        out_shape=(jax.ShapeDtypeStruct((S, LANES), BF16), jax.ShapeDtypeStruct((1, LANES), F32)),
        scratch_shapes=[pltpu.VMEM((8, LANES), F32)], compiler_params=_cparams(("arbitrary",)),
    )(d_dec, f_logit, bf)


def _rope_tables(S):
    pos = jnp.arange(S, dtype=F32)
    inv_freq = ROPE_BASE ** (-jnp.arange(0, MLA_ROPE, 2, dtype=F32) / MLA_ROPE)
    ang = pos[:, None] * inv_freq[None, :]
    cos, sin = jnp.cos(ang), jnp.sin(ang)
    half = MLA_ROPE // 2
    z = lambda n: jnp.zeros((S, n), F32)
    c_q = jnp.concatenate([jnp.ones((S, MLA_NOPE), F32), cos, cos, z(HEAD_PAD - MLA_NOPE - MLA_ROPE)], axis=1)
    c_k = jnp.concatenate([z(MLA_NOPE), cos, cos, z(HEAD_PAD - MLA_NOPE - MLA_ROPE)], axis=1)
    s_lo = jnp.concatenate([z(MLA_NOPE), -sin, z(HEAD_PAD - MLA_NOPE - half)], axis=1)
    s_hi = jnp.concatenate([z(MLA_NOPE + half), sin, z(HEAD_PAD - MLA_NOPE - MLA_ROPE)], axis=1)
    return c_q, c_k, s_lo, s_hi


def _rot(v, c, s_lo, s_hi):
    half = MLA_ROPE // 2
    return v * c + pltpu.roll(v, LANES - half, axis=1) * s_lo + pltpu.roll(v, half, axis=1) * s_hi


def _rot_t(dv, c, s_lo, s_hi):
    half = MLA_ROPE // 2
    return dv * c + pltpu.roll(dv * s_lo, half, axis=1) + pltpu.roll(dv * s_hi, LANES - half, axis=1)


def _rope_q(q_pre, c_q, s_lo, s_hi, *, transpose, out_dtype, name):
    S, W = q_pre.shape
    tm = _rows(S)
    fn = _rot_t if transpose else _rot

    def body(q_ref, c_ref, lo_ref, hi_ref, o_ref):
        c, lo, hi = c_ref[...], lo_ref[...], hi_ref[...]
        for hd in range(W // LANES):
            cols = slice(hd * LANES, (hd + 1) * LANES)
            o_ref[:, cols] = fn(q_ref[:, cols] * MLA_SCALE, c, lo, hi).astype(out_dtype)

    blk = pl.BlockSpec((tm, W), lambda i: (i, 0))
    tab = pl.BlockSpec((tm, LANES), lambda i: (i, 0))
    return pl.pallas_call(
        body, name=name, grid=(S // tm,), in_specs=[blk, tab, tab, tab], out_specs=blk,
        out_shape=jax.ShapeDtypeStruct((S, W), out_dtype), compiler_params=_cparams(("parallel",)),
    )(q_pre, c_q, s_lo, s_hi)


def _rope_k(k_pre, k_rope, c_k, s_lo, s_hi):
    S, W = k_pre.shape
    tm = _rows(S)

    def body(k_ref, r_ref, c_ref, lo_ref, hi_ref, o_ref):
        rot = _rot(r_ref[...], c_ref[...], lo_ref[...], hi_ref[...])
        for hd in range(W // LANES):
            cols = slice(hd * LANES, (hd + 1) * LANES)
            o_ref[:, cols] = (k_ref[:, cols] + rot).astype(BF16)

    blk = pl.BlockSpec((tm, W), lambda i: (i, 0))
    tab = pl.BlockSpec((tm, LANES), lambda i: (i, 0))
    return pl.pallas_call(
        body, name="rope_k", grid=(S // tm,), in_specs=[blk, tab, tab, tab, tab], out_specs=blk,
        out_shape=jax.ShapeDtypeStruct((S, W), BF16), compiler_params=_cparams(("parallel",)),
    )(k_pre, k_rope, c_k, s_lo, s_hi)


def _rope_k_bwd(dk, c_k, s_lo, s_hi):
    S, W = dk.shape
    tm = _rows(S)

    def body(dk_ref, c_ref, lo_ref, hi_ref, o_ref):
        tot = dk_ref[:, 0:LANES]
        for hd in range(1, W // LANES):
            tot = tot + dk_ref[:, hd * LANES:(hd + 1) * LANES]
        o_ref[...] = _rot_t(tot, c_ref[...], lo_ref[...], hi_ref[...]).astype(BF16)

    tab = pl.BlockSpec((tm, LANES), lambda i: (i, 0))
    return pl.pallas_call(
        body, name="rope_k_bwd", grid=(S // tm,), in_specs=[pl.BlockSpec((tm, W), lambda i: (i, 0)), tab, tab, tab],
        out_specs=tab, out_shape=jax.ShapeDtypeStruct((S, LANES), BF16), compiler_params=_cparams(("parallel",)),
    )(dk, c_k, s_lo, s_hi)


def _pairs(n, by_key):
    if by_key:
        pr = [(i, j) for j in range(n) for i in range(j, n)]
    else:
        pr = [(i, j) for i in range(n) for j in range(i + 1)]
    return (jnp.asarray(np.array([p[0] for p in pr], np.int32)), jnp.asarray(np.array([p[1] for p in pr], np.int32)), len(pr))


def _unit_mask(shape, unit, key_axis, q_off=0):
    q = lax.broadcasted_iota(jnp.int32, shape, 1 - key_axis) + q_off
    k = lax.broadcasted_iota(jnp.int32, shape, key_axis)
    if unit > 1:
        q, k = q // unit, k // unit
    return q >= k


_NT = (((1,), (1,)), ((), ()))


def _attn_fwd(q, k, v, dec_col, dec_row, *, unit, name):
    S, W = q.shape
    H = W // LANES
    T = min(ATTN_TILE, S)
    n = S // T
    qi, kj, npairs = _pairs(n, by_key=False)
    has_dec = dec_col is not None

    def body(qi_ref, kj_ref, *refs):
        if has_dec:
            q_ref, k_ref, v_ref, dc_ref, dr_ref, o_ref, lse_ref, m_s, l_s, acc = refs
        else:
            q_ref, k_ref, v_ref, o_ref, lse_ref, m_s, l_s, acc = refs
        t = pl.program_id(1)
        i, j = qi_ref[t], kj_ref[t]

        @pl.when(j == 0)
        def _():
            m_s[...] = jnp.full_like(m_s, NEG_INF)
            l_s[...] = jnp.zeros_like(l_s)
            acc[...] = jnp.zeros_like(acc)

        def step(diag):
            s = lax.dot_general(q_ref[...], k_ref[...], _NT, preferred_element_type=F32)
            if has_dec:
                s = s + (dc_ref[...] - dr_ref[...])
            if diag:
                s = jnp.where(_unit_mask((T, T), unit, 1), s, NEG_INF)
            m_prev = m_s[...]
            m_new = jnp.maximum(m_prev, jnp.max(s, axis=1, keepdims=True))
            alpha = jnp.exp(m_prev - m_new)
            p = jnp.exp(s - m_new)
            l_s[...] = alpha * l_s[...] + jnp.sum(p, axis=1, keepdims=True)
            acc[...] = alpha * acc[...] + jnp.dot(p.astype(BF16), v_ref[...], preferred_element_type=F32)
            m_s[...] = m_new

        @pl.when(j < i)
        def _():
            step(False)

        @pl.when(j == i)
        def _():
            step(True)
            o_ref[...] = (acc[...] / l_s[...]).astype(BF16)
            lse_ref[...] = m_s[...] + jnp.log(l_s[...])

    qb = pl.BlockSpec((T, LANES), lambda h, t, qi, kj: (qi[t], h))
    kb = pl.BlockSpec((T, LANES), lambda h, t, qi, kj: (kj[t], h))
    colq = pl.BlockSpec((None, T, 1), lambda h, t, qi, kj: (h, qi[t], 0))
    rowk = pl.BlockSpec((None, 1, T), lambda h, t, qi, kj: (h, 0, kj[t]))
    in_specs = [qb, kb, kb] + ([colq, rowk] if has_dec else [])
    args = (q, k, v) + ((dec_col, dec_row) if has_dec else ())
    return pl.pallas_call(
        body, name=name,
        grid_spec=pltpu.PrefetchScalarGridSpec(
            num_scalar_prefetch=2, grid=(H, npairs), in_specs=in_specs, out_specs=(qb, colq),
            scratch_shapes=[pltpu.VMEM((T, 1), F32), pltpu.VMEM((T, 1), F32), pltpu.VMEM((T, LANES), F32)]),
        out_shape=(jax.ShapeDtypeStruct((S, W), BF16), jax.ShapeDtypeStruct((H, S, 1), F32)),
        compiler_params=_cparams(("parallel", "arbitrary")),
    )(qi, kj, *args)


def _attn_delta(do, o):
    S, W = o.shape
    H = W // LANES
    tm = _rows(S)

    def body(do_ref, o_ref, d_ref):
        d_ref[...] = jnp.sum(do_ref[...].astype(F32) * o_ref[...].astype(F32), axis=1, keepdims=True)

    blk = pl.BlockSpec((tm, LANES), lambda h, i: (i, h))
    return pl.pallas_call(
        body, name="attn_delta", grid=(H, S // tm), in_specs=[blk, blk],
        out_specs=pl.BlockSpec((None, tm, 1), lambda h, i: (h, i, 0)),
        out_shape=jax.ShapeDtypeStruct((H, S, 1), F32), compiler_params=_cparams(("parallel", "parallel")),
    )(do, o)


def _attn_bwd_dq(q, k, v, do, lse, delta, dec_col, dec_row, *, unit, out_dtype, name):
    S, W = q.shape
    H = W // LANES
    T = min(ATTN_TILE, S)
    n = S // T
    qi, kj, npairs = _pairs(n, by_key=False)
    has_dec = dec_col is not None

    def body(qi_ref, kj_ref, *refs):
        if has_dec:
            q_ref, k_ref, v_ref, do_ref, lse_ref, dl_ref, dc_ref, dr_ref, dq_ref, dd_ref, acc, dacc = refs
        else:
            q_ref, k_ref, v_ref, do_ref, lse_ref, dl_ref, dq_ref, acc = refs
        t = pl.program_id(1)
        i, j = qi_ref[t], kj_ref[t]

        @pl.when(j == 0)
        def _():
            acc[...] = jnp.zeros_like(acc)
            if has_dec:
                dacc[...] = jnp.zeros_like(dacc)

        def step(diag):
            kv = k_ref[...]
            s = lax.dot_general(q_ref[...], kv, _NT, preferred_element_type=F32)
            if has_dec:
                s = s + (dc_ref[...] - dr_ref[...])
            if diag:
                s = jnp.where(_unit_mask((T, T), unit, 1), s, NEG_INF)
            p = jnp.exp(s - lse_ref[...])
            dp = lax.dot_general(do_ref[...], v_ref[...], _NT, preferred_element_type=F32)
            ds = p * (dp - dl_ref[...])
            acc[...] += jnp.dot(ds.astype(BF16), kv, preferred_element_type=F32)
            if has_dec:
                dacc[...] += jnp.sum(ds, axis=1, keepdims=True)

        @pl.when(j < i)
        def _():
            step(False)

        @pl.when(j == i)
        def _():
            step(True)
            dq_ref[...] = acc[...].astype(out_dtype)
            if has_dec:
                dd_ref[...] = dacc[...]

    qb = pl.BlockSpec((T, LANES), lambda h, t, qi, kj: (qi[t], h))
    kb = pl.BlockSpec((T, LANES), lambda h, t, qi, kj: (kj[t], h))
    colq = pl.BlockSpec((None, T, 1), lambda h, t, qi, kj: (h, qi[t], 0))
    rowk = pl.BlockSpec((None, 1, T), lambda h, t, qi, kj: (h, 0, kj[t]))
    in_specs = [qb, kb, kb, qb, colq, colq] + ([colq, rowk] if has_dec else [])
    args = (q, k, v, do, lse, delta) + ((dec_col, dec_row) if has_dec else ())
    scratch = [pltpu.VMEM((T, LANES), F32)] + ([pltpu.VMEM((T, 1), F32)] if has_dec else [])
    out_specs = (qb, colq) if has_dec else qb
    out_shape = jax.ShapeDtypeStruct((S, W), out_dtype)
    if has_dec:
        out_shape = (out_shape, jax.ShapeDtypeStruct((H, S, 1), F32))
    res = pl.pallas_call(
        body, name=name,
        grid_spec=pltpu.PrefetchScalarGridSpec(num_scalar_prefetch=2, grid=(H, npairs), in_specs=in_specs,
                                               out_specs=out_specs, scratch_shapes=scratch),
        out_shape=out_shape, compiler_params=_cparams(("parallel", "arbitrary")),
    )(qi, kj, *args)
    return res if has_dec else (res, None)


def _attn_bwd_dkv(q, k, v, do, lse_row, delta_row, dec_col, dec_row, *, unit, dk_dtype, name):
    S, W = q.shape
    H = W // LANES
    T = min(ATTN_TILE, S)
    n = S // T
    qi, kj, npairs = _pairs(n, by_key=True)
    has_dec = dec_col is not None

    def body(qi_ref, kj_ref, *refs):
        if has_dec:
            q_ref, k_ref, v_ref, do_ref, lse_ref, dl_ref, dc_ref, dr_ref, dk_ref, dv_ref, dd_ref, kacc, vacc, dacc = refs
        else:
            q_ref, k_ref, v_ref, do_ref, lse_ref, dl_ref, dk_ref, dv_ref, kacc, vacc = refs
        t = pl.program_id(1)
        i, j = qi_ref[t], kj_ref[t]

        @pl.when(i == j)
        def _():
            kacc[...] = jnp.zeros_like(kacc)
            vacc[...] = jnp.zeros_like(vacc)
            if has_dec:
                dacc[...] = jnp.zeros_like(dacc)

        def step(diag):
            qv, dov = q_ref[...], do_ref[...]
            st = lax.dot_general(k_ref[...], qv, _NT, preferred_element_type=F32)
            if has_dec:
                st = st + (dr_ref[...] - dc_ref[...])
            if diag:
                st = jnp.where(_unit_mask((T, T), unit, 0), st, NEG_INF)
            pt = jnp.exp(st - lse_ref[...])
            dpt = lax.dot_general(v_ref[...], dov, _NT, preferred_element_type=F32)
            dst = pt * (dpt - dl_ref[...])
            vacc[...] += jnp.dot(pt.astype(BF16), dov, preferred_element_type=F32)
            kacc[...] += jnp.dot(dst.astype(BF16), qv, preferred_element_type=F32)
            if has_dec:
                dacc[...] -= jnp.sum(dst, axis=1, keepdims=True)

        @pl.when(i == j)
        def _():
            step(True)

        @pl.when(i > j)
        def _():
            step(False)

        @pl.when(i == n - 1)
        def _():
            dk_ref[...] = kacc[...].astype(dk_dtype)
            dv_ref[...] = vacc[...].astype(BF16)
            if has_dec:
                dd_ref[...] = dacc[...]

    qb = pl.BlockSpec((T, LANES), lambda h, t, qi, kj: (qi[t], h))
    kb = pl.BlockSpec((T, LANES), lambda h, t, qi, kj: (kj[t], h))
    rowq = pl.BlockSpec((None, 1, T), lambda h, t, qi, kj: (h, 0, qi[t]))
    colk = pl.BlockSpec((None, T, 1), lambda h, t, qi, kj: (h, kj[t], 0))
    in_specs = [qb, kb, kb, qb, rowq, rowq] + ([colk, rowq] if has_dec else [])
    args = (q, k, v, do, lse_row, delta_row) + ((dec_col, dec_row) if has_dec else ())
    scratch = [pltpu.VMEM((T, LANES), F32)] * 2 + ([pltpu.VMEM((T, 1), F32)] if has_dec else [])
    out_specs = (kb, kb) + ((colk,) if has_dec else ())
    out_shape = (jax.ShapeDtypeStruct((S, W), dk_dtype), jax.ShapeDtypeStruct((S, W), BF16))
    if has_dec:
        out_shape = out_shape + (jax.ShapeDtypeStruct((H, S, 1), F32),)
    res = pl.pallas_call(
        body, name=name,
        grid_spec=pltpu.PrefetchScalarGridSpec(num_scalar_prefetch=2, grid=(H, npairs), in_specs=in_specs,
                                               out_specs=out_specs, scratch_shapes=scratch),
        out_shape=out_shape, compiler_params=_cparams(("parallel", "arbitrary")),
    )(qi, kj, *args)
    return res if has_dec else (res[0], res[1], None)


def _attn_bwd(q, k, v, o, lse, do, dec_col, dec_row, *, unit, dq_dtype, dk_dtype, name):
    H, S = lse.shape[0], lse.shape[1]
    delta = _attn_delta(do, o)
    dq, dd_q = _attn_bwd_dq(q, k, v, do, lse, delta, dec_col, dec_row, unit=unit, out_dtype=dq_dtype, name=name + "_dq")
    dk, dv, dd_k = _attn_bwd_dkv(q, k, v, do, lse.reshape(H, 1, S), delta.reshape(H, 1, S), dec_col, dec_row,
                                 unit=unit, dk_dtype=dk_dtype, name=name + "_dkv")
    d_dec = None if dec_col is None else dd_q + dd_k
    return dq, dk, dv, d_dec


ONES_LANE = 64
ROW_SPLIT = 1
ROW_SPLIT_BWD = 4


def _ones_lane_bias():
    one = np.zeros((HEADS, HEAD_PAD), np.float32)
    one[:, ONES_LANE] = 1.0
    return jnp.asarray(one.reshape(1, HEADS * HEAD_PAD))


def _lane_sum(t):
    tot = t[:, 0:LANES]
    for c in range(1, t.shape[1] // LANES):
        tot = tot + t[:, c * LANES:(c + 1) * LANES]
    return tot


def _fa_fwd(q, k, v, dec_row, *, unit, name):
    S, W = q.shape
    H = W // LANES
    T = min(ATTN_TILE, S)
    n, hT = S // T, T // ROW_SPLIT
    qi, kj, npairs = _pairs(n, by_key=False)
    has_dec = dec_row is not None

    def body(qi_ref, kj_ref, *refs):
        if has_dec:
            q_ref, k_ref, v_ref, dr_ref, o_ref, lse_ref, lrow_ref, m_s, acc = refs
        else:
            q_ref, k_ref, v_ref, o_ref, lse_ref, lrow_ref, m_s, acc = refs
        t = pl.program_id(1)
        i, j = qi_ref[t], kj_ref[t]

        @pl.when(j == 0)
        def _():
            m_s[...] = jnp.full_like(m_s, NEG_INF)
            acc[...] = jnp.zeros_like(acc)

        def step(diag):
            for r in range(ROW_SPLIT):
                rows = slice(r * hT, (r + 1) * hT)
                nk = (r + 1) * hT if diag else T
                s = lax.dot_general(q_ref[rows, :], k_ref[0:nk, :], _NT, preferred_element_type=F32)
                if has_dec:
                    s = s - dr_ref[:, 0:nk]
                if diag:
                    s = jnp.where(_unit_mask((hT, nk), unit, 1, r * hT), s, NEG_INF)
                m_prev = m_s[rows, :]
                m_new = jnp.maximum(m_prev, jnp.max(s, axis=1, keepdims=True))
                alpha = jnp.exp(m_prev - m_new)
                p = jnp.exp(s - jnp.tile(m_new, (1, nk // LANES)))
                acc[rows, :] = alpha * acc[rows, :] + jnp.dot(p.astype(BF16), v_ref[0:nk, :], preferred_element_type=F32)
                m_s[rows, :] = m_new

        @pl.when(j < i)
        def _():
            step(False)

        @pl.when(j == i)
        def _():
            step(True)
            av = acc[...]
            l = av[:, ONES_LANE:ONES_LANE + 1]
            lane = lax.broadcasted_iota(jnp.int32, (T, LANES), 1)
            o_ref[...] = jnp.where(lane < ONES_LANE, av / l, 0.0).astype(BF16)
            lse = m_s[...] + jnp.log(l)
            lse_ref[...] = lse
            lrow_ref[...] = lse.T[0:1, :]

    qb = pl.BlockSpec((T, LANES), lambda h, t, qi, kj: (qi[t], h))
    kb = pl.BlockSpec((T, LANES), lambda h, t, qi, kj: (kj[t], h))
    repq = pl.BlockSpec((None, T, LANES), lambda h, t, qi, kj: (h, qi[t], 0))
    rowq = pl.BlockSpec((None, 1, T), lambda h, t, qi, kj: (h, 0, qi[t]))
    rowk = pl.BlockSpec((None, 1, T), lambda h, t, qi, kj: (h, 0, kj[t]))
    in_specs = [qb, kb, kb] + ([rowk] if has_dec else [])
    args = (q, k, v) + ((dec_row,) if has_dec else ())
    return pl.pallas_call(
        body, name=name,
        grid_spec=pltpu.PrefetchScalarGridSpec(
            num_scalar_prefetch=2, grid=(H, npairs), in_specs=in_specs, out_specs=(qb, repq, rowq),
            scratch_shapes=[pltpu.VMEM((T, LANES), F32), pltpu.VMEM((T, LANES), F32)]),
        out_shape=(jax.ShapeDtypeStruct((S, W), BF16), jax.ShapeDtypeStruct((H, S, LANES), F32),
                   jax.ShapeDtypeStruct((H, 1, S), F32)),
        compiler_params=_cparams(("parallel", "arbitrary")),
    )(qi, kj, *args)


def _fa_delta(do, o):
    S, W = o.shape
    H = W // LANES
    tm = _rows(S)

    def body(do_ref, o_ref, d_ref, drow_ref):
        for hd in range(H):
            cols = slice(hd * LANES, (hd + 1) * LANES)
            d = jnp.sum(do_ref[:, cols].astype(F32) * o_ref[:, cols].astype(F32), axis=1, keepdims=True)
            rep = jnp.broadcast_to(d, (tm, LANES))
            d_ref[hd] = rep
            drow_ref[hd] = rep.T[0:1, :]

    blk = pl.BlockSpec((tm, W), lambda i: (i, 0))
    return pl.pallas_call(
        body, name="attn_delta", grid=(S // tm,), in_specs=[blk, blk],
        out_specs=(pl.BlockSpec((H, tm, LANES), lambda i: (0, i, 0)), pl.BlockSpec((H, 1, tm), lambda i: (0, 0, i))),
        out_shape=(jax.ShapeDtypeStruct((H, S, LANES), F32), jax.ShapeDtypeStruct((H, 1, S), F32)),
        compiler_params=_cparams(("parallel",)),
    )(do, o)


def _fa_bwd_dq(q, k, v, do, lse, delta, dec_row, *, unit, out_dtype, name):
    S, W = q.shape
    H = W // LANES
    T = min(ATTN_TILE, S)
    n, reps = S // T, T // LANES
    qi, kj, npairs = _pairs(n, by_key=False)
    has_dec = dec_row is not None

    def body(qi_ref, kj_ref, *refs):
        if has_dec:
            q_ref, k_ref, v_ref, do_ref, lse_ref, dl_ref, dr_ref, dq_ref, dd_ref, acc, dacc = refs
        else:
            q_ref, k_ref, v_ref, do_ref, lse_ref, dl_ref, dq_ref, acc = refs
        t = pl.program_id(1)
        i, j = qi_ref[t], kj_ref[t]

        @pl.when(j == 0)
        def _():
            acc[...] = jnp.zeros_like(acc)
            if has_dec:
                dacc[...] = jnp.zeros_like(dacc)

        def step(diag):
            kv = k_ref[...]
            s = lax.dot_general(q_ref[...], kv, _NT, preferred_element_type=F32)
            if has_dec:
                s = s - dr_ref[...]
            if diag:
                s = jnp.where(_unit_mask((T, T), unit, 1), s, NEG_INF)
            p = jnp.exp(s - jnp.tile(lse_ref[...], (1, reps)))
            dp = lax.dot_general(do_ref[...], v_ref[...], _NT, preferred_element_type=F32)
            ds = p * (dp - jnp.tile(dl_ref[...], (1, reps)))
            acc[...] += jnp.dot(ds.astype(BF16), kv, preferred_element_type=F32)
            if has_dec:
                dacc[...] += _lane_sum(ds)

        @pl.when(j < i)
        def _():
            step(False)

        @pl.when(j == i)
        def _():
            step(True)
            dq_ref[...] = acc[...].astype(out_dtype)
            if has_dec:
                dd_ref[...] = jnp.broadcast_to(jnp.sum(dacc[...], axis=1, keepdims=True), (T, LANES))

    qb = pl.BlockSpec((T, LANES), lambda h, t, qi, kj: (qi[t], h))
    kb = pl.BlockSpec((T, LANES), lambda h, t, qi, kj: (kj[t], h))
    repq = pl.BlockSpec((None, T, LANES), lambda h, t, qi, kj: (h, qi[t], 0))
    rowk = pl.BlockSpec((None, 1, T), lambda h, t, qi, kj: (h, 0, kj[t]))
    in_specs = [qb, kb, kb, qb, repq, repq] + ([rowk] if has_dec else [])
    args = (q, k, v, do, lse, delta) + ((dec_row,) if has_dec else ())
    out_shape = jax.ShapeDtypeStruct((S, W), out_dtype)
    res = pl.pallas_call(
        body, name=name,
        grid_spec=pltpu.PrefetchScalarGridSpec(num_scalar_prefetch=2, grid=(H, npairs), in_specs=in_specs,
                                               out_specs=(qb, repq) if has_dec else qb,
                                               scratch_shapes=[pltpu.VMEM((T, LANES), F32)] * (2 if has_dec else 1)),
        out_shape=(out_shape, jax.ShapeDtypeStruct((H, S, LANES), F32)) if has_dec else out_shape,
        compiler_params=_cparams(("parallel", "arbitrary")),
    )(qi, kj, *args)
    return res if has_dec else (res, None)


def _fa_bwd_dkv(q, k, v, do, lse_row, delta_row, dec_rep, *, unit, dk_dtype, name):
    S, W = q.shape
    H = W // LANES
    T = min(ATTN_TILE, S)
    n, reps = S // T, T // LANES
    qi, kj, npairs = _pairs(n, by_key=True)
    has_dec = dec_rep is not None

    def body(qi_ref, kj_ref, *refs):
        if has_dec:
            q_ref, k_ref, v_ref, do_ref, lse_ref, dl_ref, dc_ref, dk_ref, dv_ref, dd_ref, kacc, vacc, dacc = refs
        else:
            q_ref, k_ref, v_ref, do_ref, lse_ref, dl_ref, dk_ref, dv_ref, kacc, vacc = refs
        t = pl.program_id(1)
        i, j = qi_ref[t], kj_ref[t]

        @pl.when(i == j)
        def _():
            kacc[...] = jnp.zeros_like(kacc)
            vacc[...] = jnp.zeros_like(vacc)
            if has_dec:
                dacc[...] = jnp.zeros_like(dacc)

        def step(diag):
            qv, dov = q_ref[...], do_ref[...]
            st = lax.dot_general(k_ref[...], qv, _NT, preferred_element_type=F32)
            if has_dec:
                st = st - jnp.tile(dc_ref[...], (1, reps))
            if diag:
                st = jnp.where(_unit_mask((T, T), unit, 0), st, NEG_INF)
            pt = jnp.exp(st - lse_ref[...])
            dpt = lax.dot_general(v_ref[...], dov, _NT, preferred_element_type=F32)
            dst = pt * (dpt - dl_ref[...])
            vacc[...] += jnp.dot(pt.astype(BF16), dov, preferred_element_type=F32)
            kacc[...] += jnp.dot(dst.astype(BF16), qv, preferred_element_type=F32)
            if has_dec:
                dacc[...] += _lane_sum(dst)

        @pl.when(i == j)
        def _():
            step(True)

        @pl.when(i > j)
        def _():
            step(False)

        @pl.when(i == n - 1)
        def _():
            dk_ref[...] = kacc[...].astype(dk_dtype)
            dv_ref[...] = vacc[...].astype(BF16)
            if has_dec:
                dd_ref[...] = jnp.broadcast_to(-jnp.sum(dacc[...], axis=1, keepdims=True), (T, LANES))

    qb = pl.BlockSpec((T, LANES), lambda h, t, qi, kj: (qi[t], h))
    kb = pl.BlockSpec((T, LANES), lambda h, t, qi, kj: (kj[t], h))
    rowq = pl.BlockSpec((None, 1, T), lambda h, t, qi, kj: (h, 0, qi[t]))
    repk = pl.BlockSpec((None, T, LANES), lambda h, t, qi, kj: (h, kj[t], 0))
    in_specs = [qb, kb, kb, qb, rowq, rowq] + ([repk] if has_dec else [])
    args = (q, k, v, do, lse_row, delta_row) + ((dec_rep,) if has_dec else ())
    scratch = [pltpu.VMEM((T, LANES), F32)] * (3 if has_dec else 2)
    out_specs = (kb, kb) + ((repk,) if has_dec else ())
    out_shape = (jax.ShapeDtypeStruct((S, W), dk_dtype), jax.ShapeDtypeStruct((S, W), BF16))
    if has_dec:
        out_shape = out_shape + (jax.ShapeDtypeStruct((H, S, LANES), F32),)
    res = pl.pallas_call(
        body, name=name,
        grid_spec=pltpu.PrefetchScalarGridSpec(num_scalar_prefetch=2, grid=(H, npairs), in_specs=in_specs,
                                               out_specs=out_specs, scratch_shapes=scratch),
        out_shape=out_shape, compiler_params=_cparams(("parallel", "arbitrary")),
    )(qi, kj, *args)
    return res if has_dec else (res[0], res[1], None)


_TN = (((0,), (0,)), ((), ()))


def _fa_bwd_fused(q, k, v, do, o, lse, dec_row, *, unit, dq_dtype, dk_dtype, name):
    S, W = q.shape
    H = W // LANES
    T = min(ATTN_TILE, S)
    n, hT = S // T, T // ROW_SPLIT_BWD
    qi, kj, npairs = _pairs(n, by_key=False)
    has_dec = dec_row is not None

    def body(qi_ref, kj_ref, *refs):
        if has_dec:
            (q_ref, k_ref, v_ref, do_ref, o_ref, lse_ref, dr_ref, dq_ref, dk_ref, dv_ref, ddq_ref, ddk_ref,
             qacc, kacc, vacc, dl_ref, rsum, csum) = refs
        else:
            q_ref, k_ref, v_ref, do_ref, o_ref, lse_ref, dq_ref, dk_ref, dv_ref, qacc, kacc, vacc, dl_ref = refs
        t = pl.program_id(1)
        i, j = qi_ref[t], kj_ref[t]

        @pl.when(t == 0)
        def _():
            kacc[...] = jnp.zeros_like(kacc)
            vacc[...] = jnp.zeros_like(vacc)
            if has_dec:
                csum[...] = jnp.zeros_like(csum)

        @pl.when(j == 0)
        def _():
            qacc[...] = jnp.zeros_like(qacc)
            delta = jnp.sum(do_ref[...].astype(F32) * o_ref[...].astype(F32), axis=1, keepdims=True)
            dl_ref[...] = jnp.broadcast_to(delta, (T, LANES))
            if has_dec:
                rsum[...] = jnp.zeros_like(rsum)

        def step(diag):
            for r in range(ROW_SPLIT_BWD):
                rows = slice(r * hT, (r + 1) * hT)
                nk = (r + 1) * hT if diag else T
                qv, dov, kv = q_ref[rows, :], do_ref[rows, :], k_ref[0:nk, :]
                s = lax.dot_general(qv, kv, _NT, preferred_element_type=F32)
                if has_dec:
                    s = s - dr_ref[:, 0:nk]
                if diag:
                    s = jnp.where(_unit_mask((hT, nk), unit, 1, r * hT), s, NEG_INF)
                p = jnp.exp(s - jnp.tile(lse_ref[rows, :], (1, nk // LANES)))
                dp = lax.dot_general(dov, v_ref[0:nk, :], _NT, preferred_element_type=F32)
                ds = p * (dp - jnp.tile(dl_ref[rows, :], (1, nk // LANES)))
                pb, dsb = p.astype(BF16), ds.astype(BF16)
                qacc[rows, :] += jnp.dot(dsb, kv, preferred_element_type=F32)
                vacc[j, 0:nk, :] += lax.dot_general(pb, dov, _TN, preferred_element_type=F32)
                kacc[j, 0:nk, :] += lax.dot_general(dsb, qv, _TN, preferred_element_type=F32)
                if has_dec:
                    rsum[rows, :] += _lane_sum(ds)
                    csum[j, :, 0:nk] -= jnp.sum(ds, axis=0, keepdims=True)

        @pl.when(j < i)
        def _():
            step(False)

        @pl.when(j == i)
        def _():
            step(True)
            dq_ref[...] = qacc[...].astype(dq_dtype)
            if has_dec:
                ddq_ref[...] = jnp.broadcast_to(jnp.sum(rsum[...], axis=1, keepdims=True), (T, LANES))

        @pl.when(t == npairs - 1)
        def _():
            for jj in range(n):
                dk_ref[jj * T:(jj + 1) * T, :] = kacc[jj].astype(dk_dtype)
                dv_ref[jj * T:(jj + 1) * T, :] = vacc[jj].astype(BF16)
                if has_dec:
                    ddk_ref[:, jj * T:(jj + 1) * T] = csum[jj]

    qb = pl.BlockSpec((T, LANES), lambda h, t, qi, kj: (qi[t], h))
    kb = pl.BlockSpec((T, LANES), lambda h, t, qi, kj: (kj[t], h))
    head = pl.BlockSpec((S, LANES), lambda h, t, qi, kj: (0, h))
    repq = pl.BlockSpec((None, T, LANES), lambda h, t, qi, kj: (h, qi[t], 0))
    rowk = pl.BlockSpec((None, 1, T), lambda h, t, qi, kj: (h, 0, kj[t]))
    rowh = pl.BlockSpec((None, 1, S), lambda h, t, qi, kj: (h, 0, 0))
    in_specs = [qb, kb, kb, qb, qb, repq] + ([rowk] if has_dec else [])
    args = (q, k, v, do, o, lse) + ((dec_row,) if has_dec else ())
    out_specs = [qb, head, head] + ([repq, rowh] if has_dec else [])
    out_shape = [jax.ShapeDtypeStruct((S, W), dq_dtype), jax.ShapeDtypeStruct((S, W), dk_dtype), jax.ShapeDtypeStruct((S, W), BF16)]
    scratch = [pltpu.VMEM((T, LANES), F32), pltpu.VMEM((n, T, LANES), F32), pltpu.VMEM((n, T, LANES), F32),
               pltpu.VMEM((T, LANES), F32)]
    if has_dec:
        out_shape += [jax.ShapeDtypeStruct((H, S, LANES), F32), jax.ShapeDtypeStruct((H, 1, S), F32)]
        scratch += [pltpu.VMEM((T, LANES), F32), pltpu.VMEM((n, 1, T), F32)]
    res = pl.pallas_call(
        body, name=name,
        grid_spec=pltpu.PrefetchScalarGridSpec(num_scalar_prefetch=2, grid=(H, npairs), in_specs=in_specs,
                                               out_specs=tuple(out_specs), scratch_shapes=scratch),
        out_shape=tuple(out_shape),
        compiler_params=pltpu.CompilerParams(dimension_semantics=("parallel", "arbitrary"), vmem_limit_bytes=FUSED_BWD_VMEM_BYTES),
    )(qi, kj, *args)
    return res if has_dec else (res[0], res[1], res[2], None, None)


FUSED_BWD_VMEM_BYTES = 58 * 1024 * 1024


def _fa_bwd(q, k, v, o, lse, lse_row, do, dec_row, dec_rep, *, unit, dq_dtype, dk_dtype, name):
    dq, dk, dv, dd_q, dd_k = _fa_bwd_fused(q, k, v, do, o, lse, dec_row, unit=unit, dq_dtype=dq_dtype, dk_dtype=dk_dtype, name=name)
    if dd_k is None:
        return dq, dk, dv, None
    return dq, dk, dv, jnp.max(dd_q, axis=2) + dd_k.reshape(dd_k.shape[0], dd_k.shape[2])


def _fa_bwd_split(q, k, v, o, lse, lse_row, do, dec_row, dec_rep, *, unit, dq_dtype, dk_dtype, name):
    delta, delta_row = _fa_delta(do, o)
    dq, dd_q = _fa_bwd_dq(q, k, v, do, lse, delta, dec_row, unit=unit, out_dtype=dq_dtype, name=name + "_dq")
    one = lambda t: jnp.max(t, axis=2)
    dk, dv, dd_k = _fa_bwd_dkv(q, k, v, do, lse_row, delta_row, dec_rep, unit=unit, dk_dtype=dk_dtype, name=name + "_dkv")
    return dq, dk, dv, (None if dd_k is None else one(dd_q) + one(dd_k))


def _merge_fwd(ya, yb, yc, gate_logit, gate_b):
    S, D = ya.shape
    tm = min(256, S)

    def body(a_ref, b_ref, c_ref, gl_ref, gb_ref, o_ref):
        g = jax.nn.sigmoid(gl_ref[...] + gb_ref[...])
        o_ref[...] = (g[:, 0:D] * a_ref[...] + g[:, D:2 * D] * b_ref[...] + g[:, 2 * D:3 * D] * c_ref[...]).astype(BF16)

    row = pl.BlockSpec((tm, D), lambda i: (i, 0))
    return pl.pallas_call(
        body, name="merge_fwd", grid=(S // tm,),
        in_specs=[row, row, row, pl.BlockSpec((tm, 3 * D), lambda i: (i, 0)), pl.BlockSpec((1, 3 * D), lambda i: (0, 0))],
        out_specs=row, out_shape=jax.ShapeDtypeStruct((S, D), BF16), compiler_params=_cparams(("parallel",)),
    )(ya, yb, yc, gate_logit, gate_b.reshape(1, 3 * D))


def _merge_bwd(dm, ya, yb, yc, gate_logit, gate_b):
    S, D = ya.shape
    tm = min(256, S)

    def body(dm_ref, a_ref, b_ref, c_ref, gl_ref, gb_ref, da_ref, db_ref, dc_ref, dgl_ref, dgb_ref):
        g = jax.nn.sigmoid(gl_ref[...] + gb_ref[...])
        dmv = dm_ref[...]
        parts = []
        for n, (y_ref, dy_ref) in enumerate(((a_ref, da_ref), (b_ref, db_ref), (c_ref, dc_ref))):
            gn = g[:, n * D:(n + 1) * D]
            dy_ref[...] = (dmv * gn).astype(BF16)
            parts.append(dmv * y_ref[...] * gn * (1.0 - gn))
        dgl = jnp.concatenate(parts, axis=1)
        dgl_ref[...] = dgl.astype(BF16)

        @pl.when(pl.program_id(0) == 0)
        def _():
            dgb_ref[...] = jnp.zeros_like(dgb_ref)

        dgb_ref[...] += jnp.sum(dgl, axis=0, keepdims=True)

    row = pl.BlockSpec((tm, D), lambda i: (i, 0))
    wide = pl.BlockSpec((tm, 3 * D), lambda i: (i, 0))
    vec = pl.BlockSpec((1, 3 * D), lambda i: (0, 0))
    act = jax.ShapeDtypeStruct((S, D), BF16)
    da, db, dc, dgl, dgb = pl.pallas_call(
        body, name="merge_bwd", grid=(S // tm,), in_specs=[row, row, row, row, wide, vec],
        out_specs=(row, row, row, wide, vec),
        out_shape=(act, act, act, jax.ShapeDtypeStruct((S, 3 * D), BF16), jax.ShapeDtypeStruct((1, 3 * D), F32)),
        compiler_params=_cparams(("arbitrary",)),
    )(dm, ya, yb, yc, gate_logit, gate_b.reshape(1, 3 * D))
    return da, db, dc, dgl, dgb.reshape(3 * D)


def _swiglu_fwd(hf):
    S, W2 = hf.shape
    F = W2 // 2
    tm = min(128, S)

    def body(h_ref, o_ref):
        gt, up = h_ref[:, 0:F], h_ref[:, F:W2]
        o_ref[...] = (gt * jax.nn.sigmoid(gt) * up).astype(BF16)

    return pl.pallas_call(
        body, name="swiglu_fwd", grid=(S // tm,), in_specs=[pl.BlockSpec((tm, W2), lambda i: (i, 0))],
        out_specs=pl.BlockSpec((tm, F), lambda i: (i, 0)), out_shape=jax.ShapeDtypeStruct((S, F), BF16),
        compiler_params=_cparams(("parallel",)),
    )(hf)


def _swiglu_bwd(dact, hf):
    S, W2 = hf.shape
    F = W2 // 2
    tm = min(128, S)

    def body(d_ref, h_ref, o_ref):
        gt, up = h_ref[:, 0:F], h_ref[:, F:W2]
        sg = jax.nn.sigmoid(gt)
        dv = d_ref[...]
        o_ref[:, 0:F] = (dv * up * sg * (1.0 + gt * (1.0 - sg))).astype(BF16)
        o_ref[:, F:W2] = (dv * gt * sg).astype(BF16)

    return pl.pallas_call(
        body, name="swiglu_bwd", grid=(S // tm,),
        in_specs=[pl.BlockSpec((tm, F), lambda i: (i, 0)), pl.BlockSpec((tm, W2), lambda i: (i, 0))],
        out_specs=pl.BlockSpec((tm, W2), lambda i: (i, 0)), out_shape=jax.ShapeDtypeStruct((S, W2), BF16),
        compiler_params=_cparams(("parallel",)),
    )(dact, hf)


def _ple_fwd(x, pre, e, g_next):
    S, D = x.shape
    tm = _rows(S)
    with_norm = g_next is not None

    def body(*refs):
        x_ref, p_ref, e_ref = refs[:3]
        xn = x_ref[...] + jax.nn.sigmoid(p_ref[...]) * e_ref[...]
        if with_norm:
            g_ref, o_ref, h_ref = refs[3:]
            rstd = lax.rsqrt(jnp.mean(xn * xn, axis=1, keepdims=True) + EPS)
            h_ref[...] = (xn * rstd * g_ref[...]).astype(BF16)
        else:
            o_ref = refs[3]
        o_ref[...] = xn

    row = pl.BlockSpec((tm, D), lambda i: (i, 0))
    xs = jax.ShapeDtypeStruct((S, D), F32)
    if not with_norm:
        return pl.pallas_call(body, name="ple_fwd_last", grid=(S // tm,), in_specs=[row, row, row], out_specs=row,
                              out_shape=xs, compiler_params=_cparams(("parallel",)))(x, pre, e), None
    return pl.pallas_call(body, name="ple_fwd", grid=(S // tm,), in_specs=[row, row, row, pl.BlockSpec((1, D), lambda i: (0, 0))],
                          out_specs=(row, row), out_shape=(xs, jax.ShapeDtypeStruct((S, D), BF16)),
                          compiler_params=_cparams(("parallel",)))(x, pre, e, g_next.reshape(1, D))


def _ple_bwd(dx, pre, e):
    S, D = dx.shape
    tm = _rows(S)

    def body(dx_ref, p_ref, e_ref, dp_ref, de_ref):
        pg = jax.nn.sigmoid(p_ref[...])
        dxv = dx_ref[...]
        dp_ref[...] = (dxv * e_ref[...] * pg * (1.0 - pg)).astype(BF16)
        de_ref[...] = (dxv * pg).astype(BF16)

    row = pl.BlockSpec((tm, D), lambda i: (i, 0))
    act = jax.ShapeDtypeStruct((S, D), BF16)
    return pl.pallas_call(body, name="ple_bwd", grid=(S // tm,), in_specs=[row, row, row], out_specs=(row, row),
                          out_shape=(act, act), compiler_params=_cparams(("parallel",)))(dx, pre, e)


def _pad_heads(w, real):
    K = w.shape[0]
    w = w.reshape(K, HEADS, real)
    return jnp.pad(w, ((0, 0), (0, 0), (0, HEAD_PAD - real))).reshape(K, HEADS * HEAD_PAD)


def _unpad_heads(w, real):
    K = w.shape[0]
    return w.reshape(K, HEADS, HEAD_PAD)[:, :, :real].reshape(K, HEADS * real)


def _pad_head_rows(w, real):
    N = w.shape[1]
    w = w.reshape(HEADS, real, N)
    return jnp.pad(w, ((0, 0), (0, HEAD_PAD - real), (0, 0))).reshape(HEADS * HEAD_PAD, N)


def _unpad_head_rows(w, real):
    N = w.shape[1]
    return w.reshape(HEADS, HEAD_PAD, N)[:, :real].reshape(HEADS * real, N)


def _block_diag(w):
    w = w.reshape(4, 2, 64, 64)
    z = jnp.zeros((4, 64, 64), w.dtype)
    top = jnp.concatenate([w[:, 0], z], axis=2)
    bot = jnp.concatenate([z, w[:, 1]], axis=2)
    return jnp.concatenate([top, bot], axis=1)


def _block_diag_t(w):
    return jnp.stack([w[:, :64, :64], w[:, 64:, 64:]], axis=1).reshape(8, 64, 64)


_IN_SPLITS = (512, 512, 384, 288, 512, 512, 512, 8, 3072)
_IN_OFF = np.concatenate([[0], np.cumsum(_IN_SPLITS)])
_KR_OFF = 64
_SEG_NAMES = ("u", "ug", "cq", "ckv", "kr", "fq", "fk", "fv", "fl", "gate")


def _in_segments(w_in):
    c = lambda n: w_in[:, int(_IN_OFF[n]):int(_IN_OFF[n + 1])]
    kv = c(3)
    kr = jnp.pad(kv[:, MLA_KV_LORA:], ((0, 0), (_KR_OFF, LANES - _KR_OFF - MLA_ROPE)))
    fl = jnp.pad(c(7), ((0, 0), (0, LANES - HEADS)))
    fq = _pad_heads(c(4), FOX_HEAD_DIM) * jnp.asarray(FOX_SCALE, w_in.dtype)
    return [c(0), c(1), c(2), kv[:, :MLA_KV_LORA], kr, fq, _pad_heads(c(5), FOX_HEAD_DIM), _pad_heads(c(6), FOX_HEAD_DIM), fl, c(8)]


def _in_unsegment(dw_p, widths):
    offs = np.concatenate([[0], np.cumsum(widths)])
    seg = [dw_p[:, int(offs[n]):int(offs[n + 1])] for n in range(len(widths))]
    u, ug, cq, ckv, kr, fq, fk, fv, fl, gate = seg
    return jnp.concatenate([
        u, ug, cq, ckv, kr[:, _KR_OFF:_KR_OFF + MLA_ROPE], _unpad_heads(fq, FOX_HEAD_DIM) * FOX_SCALE,
        _unpad_heads(fk, FOX_HEAD_DIM), _unpad_heads(fv, FOX_HEAD_DIM), fl[:, :HEADS], gate], axis=1)


def _split_wuq(wuq):
    return _pad_heads(wuq, MLA_NOPE + MLA_ROPE)


def _split_wukv(wukv):
    w = wukv.reshape(MLA_KV_LORA, HEADS, MLA_NOPE + MLA_V)
    pad = lambda t: jnp.pad(t, ((0, 0), (0, 0), (0, HEAD_PAD - t.shape[2]))).reshape(MLA_KV_LORA, HEADS * HEAD_PAD)
    return pad(w[:, :, :MLA_NOPE]), pad(w[:, :, MLA_NOPE:])


def _merge_wukv(dk_p, dv_p):
    k = dk_p.reshape(MLA_KV_LORA, HEADS, HEAD_PAD)[:, :, :MLA_NOPE]
    v = dv_p.reshape(MLA_KV_LORA, HEADS, HEAD_PAD)[:, :, :MLA_V]
    return jnp.concatenate([k, v], axis=2).reshape(MLA_KV_LORA, HEADS * (MLA_NOPE + MLA_V))


def _heads_layout(d):
    S = d.shape[0]
    t = d[:, :HEADS].T
    return t.reshape(HEADS, 1, S), jnp.broadcast_to(t[:, :, None], (HEADS, S, LANES))


def _layer_fwd(x, h, p_i, w, g_next, tabs):
    c_q, c_k, s_lo, s_hi = tabs
    sv = {"x0": x}
    segs = _in_segments(w["w_in"])
    z = {}
    for nm, ws in zip(_SEG_NAMES, segs):
        z[nm] = _mm(h, ws, out_dtype=BF16 if nm in ("fq", "fk", "fv") else F32, bias=_ones_lane_bias() if nm == "fv" else None,
                    name="in_" + nm)
    sv.update(h=h, z=z)
    wa_bd, wx_bd = _block_diag(w["lru_wa"]).astype(BF16), _block_diag(w["lru_wx"]).astype(BF16)
    oa, xc, hs = _lru_fwd(z["u"], z["ug"], w["conv_w"], w["conv_b"], wa_bd, wx_bd, w["lru_ba"], w["lru_bx"], w["lru_lambda"])
    sv.update(oa=oa, xc=xc, hs=hs)
    qn = _rmsnorm_fwd(z["cq"], w["mla_q_norm"], "q_norm_fwd")
    kvn = _rmsnorm_fwd(z["ckv"], w["mla_kv_norm"], "kv_norm_fwd")
    wuq_p = _split_wuq(w["mla_wuq"])
    wk_p, wv_p = _split_wukv(w["mla_wukv"])
    qb = _rope_q(_mm(qn, wuq_p, name="mla_q"), c_q, s_lo, s_hi, transpose=False, out_dtype=BF16, name="rope_q")
    kb = _rope_k(_mm(kvn, wk_p, name="mla_k"), z["kr"], c_k, s_lo, s_hi)
    vb = _mm(kvn, wv_p, out_dtype=BF16, bias=_ones_lane_bias(), name="mla_v")
    ob, lse_b, lrow_b = _fa_fwd(qb, kb, vb, None, unit=64, name="mla_attn")
    sv.update(qn=qn, kvn=kvn, qb=qb, kb=kb, vb=vb, ob=ob, lse_b=lse_b, lrow_b=lrow_b)
    bf = jnp.pad(w["fox_bf"], (0, LANES - HEADS)).reshape(1, LANES)
    dec = _decay_fwd(z["fl"], bf)
    drow, drep = _heads_layout(dec)
    oc, lse_c, lrow_c = _fa_fwd(z["fq"], z["fk"], z["fv"], drow, unit=1, name="fox_attn")
    sv.update(drow=drow, drep=drep, oc=oc, lse_c=lse_c, lrow_c=lrow_c)
    ya = _mm(oa, w["w_br_a"], name="br_a")
    yb = _mm(ob, _pad_head_rows(w["w_br_b"], MLA_V), name="br_b")
    yc = _mm(oc, _pad_head_rows(w["w_br_c"], FOX_HEAD_DIM), name="br_c")
    merged = _merge_fwd(ya, yb, yc, z["gate"], w["gate_b"])
    x1, hn = _mm_res_norm(merged, w["w_o"], x, w["ffn_norm"], "w_o")
    sv.update(ya=ya, yb=yb, yc=yc, merged=merged, x1=x1)
    hf, act = _ffn_up(hn, _ffn_pair_columns(w["w_gate_up"]))
    x2, pn = _mm_res_norm(act, w["w_down"], x1, w["ple_norm"], "ffn_down")
    sv.update(hn=hn, hf=hf, act=act, x2=x2)
    pre = _mm(pn, w["w_ple_gate"], name="ple_gate")
    e = _mm(p_i, w["w_ple"], name="ple_embed")
    x3, h_next = _ple_fwd(x2, pre, e, g_next)
    sv.update(pn=pn, pre=pre, e=e, p_i=p_i)
    return x3, h_next, sv


def _layer_bwd(dx3, w, sv, tabs):
    c_q, c_k, s_lo, s_hi = tabs
    g = {}
    z = sv["z"]
    dpre, de = _ple_bwd(dx3, sv["pre"], sv["e"])
    g["w_ple"] = _mm(sv["p_i"], de, ta=True, name="d_w_ple")
    g["w_ple_gate"] = _mm(sv["pn"], dpre, ta=True, name="d_w_ple_gate")
    dpn = _mm(dpre, w["w_ple_gate"], tb=True, name="d_pn")
    dx2, g["ple_norm"] = _rmsnorm_bwd(sv["x2"], w["ple_norm"], dpn, add=dx3, name="ple_norm_bwd")
    g["w_down"] = _mm(sv["act"], dx2, ta=True, name="d_w_down")
    dhf = _ffn_down_bwd(dx2, w["w_down"], sv["hf"])
    g["w_gate_up"] = _ffn_unpair_columns(_mm(sv["hn"], dhf, ta=True, name="d_w_gate_up"))
    dhn = _mm(dhf, _ffn_pair_columns(w["w_gate_up"]), tb=True, name="d_hn")
    dx1, g["ffn_norm"] = _rmsnorm_bwd(sv["x1"], w["ffn_norm"], dhn, add=dx2, name="ffn_norm_bwd")
    g["w_o"] = _mm(sv["merged"], dx1, ta=True, name="d_w_o")
    dm = _mm(dx1, w["w_o"], tb=True, name="d_merged")
    dya, dyb, dyc, dgate, g["gate_b"] = _merge_bwd(dm, sv["ya"], sv["yb"], sv["yc"], z["gate"], w["gate_b"])
    wbb_p, wbc_p = _pad_head_rows(w["w_br_b"], MLA_V), _pad_head_rows(w["w_br_c"], FOX_HEAD_DIM)
    g["w_br_a"] = _mm(sv["oa"], dya, ta=True, name="d_w_br_a")
    g["w_br_b"] = _unpad_head_rows(_mm(sv["ob"], dyb, ta=True, name="d_w_br_b"), MLA_V)
    g["w_br_c"] = _unpad_head_rows(_mm(sv["oc"], dyc, ta=True, name="d_w_br_c"), FOX_HEAD_DIM)
    doa = _mm(dya, w["w_br_a"], tb=True, name="d_oa")
    dob = _mm(dyb, wbb_p, tb=True, out_dtype=BF16, name="d_ob")
    doc = _mm(dyc, wbc_p, tb=True, out_dtype=BF16, name="d_oc")
    dfq, dfk, dfv, d_dec = _fa_bwd(z["fq"], z["fk"], z["fv"], sv["oc"], sv["lse_c"], sv["lrow_c"], doc, sv["drow"], sv["drep"],
                                   unit=1, dq_dtype=BF16, dk_dtype=BF16, name="fox_attn_bwd")
    d_dec = jnp.pad(d_dec.T, ((0, 0), (0, LANES - HEADS)))
    bf = jnp.pad(w["fox_bf"], (0, LANES - HEADS)).reshape(1, LANES)
    dfl, dbf = _decay_bwd(d_dec, z["fl"], bf)
    g["fox_bf"] = dbf[0, :HEADS]
    dqb, dkb, dvb, _ = _fa_bwd(sv["qb"], sv["kb"], sv["vb"], sv["ob"], sv["lse_b"], sv["lrow_b"], dob, None, None,
                               unit=64, dq_dtype=F32, dk_dtype=F32, name="mla_attn_bwd")
    wuq_p = _split_wuq(w["mla_wuq"])
    wk_p, wv_p = _split_wukv(w["mla_wukv"])
    dq_pre = _rope_q(dqb, c_q, s_lo, s_hi, transpose=True, out_dtype=BF16, name="rope_q_bwd")
    dkr = _rope_k_bwd(dkb, c_k, s_lo, s_hi)
    g["mla_wuq"] = _unpad_heads(_mm(sv["qn"], dq_pre, ta=True, name="d_wuq"), MLA_NOPE + MLA_ROPE)
    g["mla_wukv"] = _merge_wukv(_mm(sv["kvn"], dkb, ta=True, name="d_wuk"), _mm(sv["kvn"], dvb, ta=True, name="d_wuv"))
    dqn = _mm(dq_pre, wuq_p, tb=True, name="d_qn")
    dkvn = _mm(dvb, wv_p, tb=True, res=_mm(dkb, wk_p, tb=True, name="d_kvn_k"), name="d_kvn")
    dcq, g["mla_q_norm"] = _rmsnorm_bwd(z["cq"], w["mla_q_norm"], dqn, out_dtype=BF16, name="q_norm_bwd")
    dckv, g["mla_kv_norm"] = _rmsnorm_bwd(z["ckv"], w["mla_kv_norm"], dkvn, out_dtype=BF16, name="kv_norm_bwd")
    wa_bd, wx_bd = _block_diag(w["lru_wa"]).astype(BF16), _block_diag(w["lru_wx"]).astype(BF16)
    du, dug, dcw, dcb, dba, dbx, dlam, dwa, dwx = _lru_bwd(
        doa, z["u"], z["ug"], sv["xc"], sv["hs"], w["conv_w"], wa_bd, wx_bd, w["lru_ba"], w["lru_bx"], w["lru_lambda"])
    g["conv_w"], g["conv_b"], g["lru_ba"], g["lru_bx"] = dcw, dcb[0], dba[0], dbx[0]
    g["lru_lambda"] = dlam[0] * LRU_C * jax.nn.sigmoid(-w["lru_lambda"])
    g["lru_wa"], g["lru_wx"] = _block_diag_t(dwa), _block_diag_t(dwx)
    dsegs = [du, dug, dcq, dckv, dkr, dfq, dfk, dfv, dfl, dgate]
    dz = jnp.concatenate(dsegs, axis=1)
    w_in_p = jnp.concatenate(_in_segments(w["w_in"]), axis=1)
    g["w_in"] = _in_unsegment(_mm(sv["h"], dz, ta=True, name="d_w_in"), [d.shape[1] for d in dsegs])
    dh = _mm(dz, w_in_p, tb=True, name="d_h")
    dx0, g["mix_norm"] = _rmsnorm_bwd(sv["x0"], w["mix_norm"], dh, add=dx1, name="mix_norm_bwd")
    return dx0, g


_LAYER_WEIGHTS = ("mix_norm", "w_in", "gate_b", "conv_w", "conv_b", "lru_wa", "lru_ba", "lru_wx", "lru_bx", "lru_lambda",
                  "mla_q_norm", "mla_wuq", "mla_kv_norm", "mla_wukv", "fox_bf", "w_br_a", "w_br_b", "w_br_c", "w_o",
                  "ffn_norm", "w_gate_up", "w_down", "ple_norm", "w_ple_gate", "w_ple")
_BIG = ("w_in", "mla_wuq", "mla_wukv", "w_br_a", "w_br_b", "w_br_c", "w_o", "w_gate_up", "w_down", "w_ple_gate", "w_ple")
_ROW_SHARDED = ("w_o", "w_down", "w_ple_gate")
_SMALL = ("mix_norm", "gate_b", "conv_b", "lru_wa", "lru_ba", "lru_wx", "lru_bx", "lru_lambda", "mla_q_norm", "mla_kv_norm",
          "fox_bf", "ffn_norm", "ple_norm")


def _local_step(x, p, layers, final_norm, target):
    tabs = _rope_tables(x.shape[0])
    saved = []
    h = _rmsnorm_fwd(x, layers[0]["mix_norm"], "mix_norm_fwd")
    for i in range(DEPTH):
        g_next = layers[i + 1]["mix_norm"] if i + 1 < DEPTH else None
        x, h, sv = _layer_fwd(x, h, p[i], layers[i], g_next, tabs)
        saved.append(sv)
    loss, dx, d_final = _loss_head(x, final_norm, target)
    grads = [None] * DEPTH
    for i in reversed(range(DEPTH)):
        dx, grads[i] = _layer_bwd(dx, layers[i], saved[i], tabs)
    return loss, dx, grads, d_final


def _hbm():
    return pl.BlockSpec(memory_space=pltpu.HBM)


def _peers(x, y):
    return [(1 - x, y), (x, 1 - y), (1 - x, 1 - y)]


def _gather_chips_two_level(shard, name):
    R, W = shard.shape
    Rh = R // 2

    def body(src_ref, out_ref, send_sems, recv_sems):
        x, y, c = lax.axis_index("x"), lax.axis_index("y"), lax.axis_index("c")
        me = 2 * x + y
        mine, other = pl.ds(c * Rh, Rh), pl.ds((1 - c) * Rh, Rh)
        peers = _peers(x, y)

        def copy(j, src, slot, rows, to):
            return pltpu.make_async_remote_copy(src_ref=src, dst_ref=out_ref.at[slot, rows], send_sem=send_sems.at[j],
                                                recv_sem=recv_sems.at[j], device_id=to, device_id_type=MESH)

        first = [copy(j, src_ref.at[mine], me, mine, (px, py, c)) for j, (px, py) in enumerate(peers)]
        for cp in first:
            cp.start()
        passed = []
        for j, (px, py) in enumerate(peers):
            slot = 2 * px + py
            copy(j, src_ref.at[mine], slot, mine, (px, py, c)).wait_recv()
            cp = copy(3 + j, out_ref.at[slot, mine], slot, mine, (x, y, 1 - c))
            cp.start()
            passed.append(cp)
        for j, (px, py) in enumerate(peers):
            copy(3 + j, src_ref.at[other], 2 * px + py, other, (x, y, 1 - c)).wait_recv()
        for cp in first + passed:
            cp.wait_send()

    return pl.pallas_call(
        body, name=name, in_specs=[_hbm()], out_specs=_hbm(), out_shape=jax.ShapeDtypeStruct((4, R, W), shard.dtype),
        scratch_shapes=[pltpu.SemaphoreType.DMA((6,)), pltpu.SemaphoreType.DMA((6,))],
    )(shard)


def _gather_weights(arrs, name):
    n_arr = len(arrs)

    def body(*refs):
        srcs, outs = refs[:n_arr], refs[n_arr:2 * n_arr]
        send_sems, recv_sems = refs[2 * n_arr:]
        x, y, c = lax.axis_index("x"), lax.axis_index("y"), lax.axis_index("c")
        me = 2 * x + y
        peers = _peers(x, y)

        def copy(sem, src, dst, to):
            return pltpu.make_async_remote_copy(src_ref=src, dst_ref=dst, send_sem=send_sems.at[sem], recv_sem=recv_sems.at[sem],
                                                device_id=to, device_id_type=MESH)

        started = []
        for a in range(n_arr):
            for j, (px, py) in enumerate(peers):
                cp = copy(6 * a + j, srcs[a].at[c], outs[a].at[me, c], (px, py, c))
                cp.start()
                started.append(cp)
        for a in range(n_arr):
            for j, (px, py) in enumerate(peers):
                landed = outs[a].at[2 * px + py, c]
                copy(6 * a + j, srcs[a].at[c], landed, (px, py, c)).wait_recv()
                cp = copy(6 * a + 3 + j, landed, landed, (x, y, 1 - c))
                cp.start()
                started.append(cp)
        for a in range(n_arr):
            for j, (px, py) in enumerate(peers):
                copy(6 * a + 3 + j, srcs[a].at[1 - c], outs[a].at[2 * px + py, 1 - c], (x, y, 1 - c)).wait_recv()
        for cp in started:
            cp.wait_send()

    return pl.pallas_call(
        body, name=name, in_specs=[_hbm()] * n_arr, out_specs=tuple([_hbm()] * n_arr),
        out_shape=tuple(jax.ShapeDtypeStruct((4,) + t.shape, t.dtype) for t in arrs),
        scratch_shapes=[pltpu.SemaphoreType.DMA((6 * n_arr,)), pltpu.SemaphoreType.DMA((6 * n_arr,))],
    )(*arrs)


def _gather_chips(shard, name):
    R, W = shard.shape

    def body(src_ref, out_ref, send_sems, recv_sems, local_sem):
        x, y, c = lax.axis_index("x"), lax.axis_index("y"), lax.axis_index("c")
        me = 2 * x + y
        mine = pltpu.make_async_copy(src_ref, out_ref.at[me], local_sem)
        mine.start()

        def copy(j, slot, to):
            return pltpu.make_async_remote_copy(src_ref=src_ref, dst_ref=out_ref.at[slot], send_sem=send_sems.at[j],
                                                recv_sem=recv_sems.at[j], device_id=(to[0], to[1], c), device_id_type=MESH)

        sends = [copy(j, me, peer) for j, peer in enumerate(_peers(x, y))]
        for cp in sends:
            cp.start()
        for j, peer in enumerate(_peers(x, y)):
            copy(j, 2 * peer[0] + peer[1], peer).wait_recv()
        for cp in sends:
            cp.wait_send()
        mine.wait()

    return pl.pallas_call(
        body, name=name, in_specs=[_hbm()], out_specs=_hbm(), out_shape=jax.ShapeDtypeStruct((4, R, W), shard.dtype),
        scratch_shapes=[pltpu.SemaphoreType.DMA((3,)), pltpu.SemaphoreType.DMA((3,)), pltpu.SemaphoreType.DMA],
    )(shard)


def _pair_swap_halves(g4):
    n, R, W = g4.shape
    Rh = R // 2

    def body(src_ref, out_ref, send_sem, recv_sem):
        x, y, c = lax.axis_index("x"), lax.axis_index("y"), lax.axis_index("c")
        cp = pltpu.make_async_remote_copy(src_ref=src_ref.at[:, pl.ds((1 - c) * Rh, Rh), :], dst_ref=out_ref, send_sem=send_sem,
                                          recv_sem=recv_sem, device_id=(x, y, 1 - c), device_id_type=MESH)
        cp.start()
        cp.wait()

    return pl.pallas_call(
        body, name="grad_pair_swap", in_specs=[_hbm()], out_specs=_hbm(), out_shape=jax.ShapeDtypeStruct((n, Rh, W), g4.dtype),
        scratch_shapes=[pltpu.SemaphoreType.DMA, pltpu.SemaphoreType.DMA],
    )(g4)


def _pair_add(g4, sib, c_arr):
    n, R, W = g4.shape
    Rh = R // 2
    tr = _tile_rows(Rh)
    nb = Rh // tr

    def body(c_ref, a_ref, b_ref, o_ref):
        o_ref[...] = (a_ref[...].astype(F32) + b_ref[...].astype(F32)).astype(o_ref.dtype)

    return pl.pallas_call(
        body, name="grad_pair_add",
        grid_spec=pltpu.PrefetchScalarGridSpec(
            num_scalar_prefetch=1, grid=(n, nb),
            in_specs=[pl.BlockSpec((None, tr, W), lambda s, i, c: (s, c[0] * nb + i, 0)), pl.BlockSpec((None, tr, W), lambda s, i, c: (s, i, 0))],
            out_specs=pl.BlockSpec((None, tr, W), lambda s, i, c: (s, i, 0))),
        out_shape=jax.ShapeDtypeStruct((n, Rh, W), g4.dtype), compiler_params=_cparams(("parallel", "parallel")),
    )(c_arr, g4, sib)


def _tile_rows(n):
    for t in (512, 480, 400, 320, 256, 240, 160, 128, 80, 64, 40, 32, 16, 8):
        if n % t == 0:
            return t
    return n


def _chips_exchange(part):
    n, Rh, W = part.shape

    def body(src_ref, out_ref, send_sems, recv_sems):
        x, y, c = lax.axis_index("x"), lax.axis_index("y"), lax.axis_index("c")

        def copy(j, to):
            return pltpu.make_async_remote_copy(src_ref=src_ref.at[2 * to[0] + to[1]], dst_ref=out_ref.at[j], send_sem=send_sems.at[j],
                                                recv_sem=recv_sems.at[j], device_id=(to[0], to[1], c), device_id_type=MESH)

        cps = [copy(j, peer) for j, peer in enumerate(_peers(x, y))]
        for cp in cps:
            cp.start()
        for cp in cps:
            cp.wait()

    return pl.pallas_call(
        body, name="grad_chips_exchange", in_specs=[_hbm()], out_specs=_hbm(), out_shape=jax.ShapeDtypeStruct((3, Rh, W), part.dtype),
        scratch_shapes=[pltpu.SemaphoreType.DMA((3,)), pltpu.SemaphoreType.DMA((3,))],
    )(part)


def _chips_add(part, got, k_arr, c_arr):
    n, Rh, W = part.shape
    tr = _tile_rows(Rh)
    nb = Rh // tr

    def body(k_ref, c_ref, a_ref, b_ref, o_ref):
        mine = pl.program_id(0) == c_ref[0]

        @pl.when(mine)
        def _():
            o_ref[...] = ((a_ref[...].astype(F32) + b_ref[0].astype(F32)) + b_ref[1].astype(F32)) + b_ref[2].astype(F32)

        @pl.when(jnp.logical_not(mine))
        def _():
            o_ref[...] = jnp.zeros_like(o_ref)

    return pl.pallas_call(
        body, name="grad_chips_add",
        grid_spec=pltpu.PrefetchScalarGridSpec(
            num_scalar_prefetch=2, grid=(2, nb),
            in_specs=[pl.BlockSpec((None, tr, W), lambda h, i, k, c: (k[0], i, 0)), pl.BlockSpec((3, tr, W), lambda h, i, k, c: (0, i, 0))],
            out_specs=pl.BlockSpec((tr, W), lambda h, i, k, c: (h * nb + i, 0))),
        out_shape=jax.ShapeDtypeStruct((2 * Rh, W), F32), compiler_params=_cparams(("parallel", "parallel")),
    )(k_arr, c_arr, part, got)


def _pair_gather(buf):
    R, W = buf.shape
    Rh = R // 2

    def body(src_ref, out_ref, send_sem, recv_sem):
        x, y, c = lax.axis_index("x"), lax.axis_index("y"), lax.axis_index("c")
        mine, other = pl.ds(c * Rh, Rh), pl.ds((1 - c) * Rh, Rh)
        pltpu.make_async_remote_copy(src_ref=src_ref.at[mine], dst_ref=out_ref.at[mine], send_sem=send_sem, recv_sem=recv_sem,
                                     device_id=(x, y, 1 - c), device_id_type=MESH).start()
        pltpu.make_async_remote_copy(src_ref=src_ref.at[mine], dst_ref=out_ref.at[other], send_sem=send_sem, recv_sem=recv_sem,
                                     device_id=(x, y, 1 - c), device_id_type=MESH).wait()

    return pl.pallas_call(
        body, name="grad_pair_gather", in_specs=[_hbm()], out_specs=_hbm(), out_shape=jax.ShapeDtypeStruct((R, W), buf.dtype),
        input_output_aliases={0: 0}, scratch_shapes=[pltpu.SemaphoreType.DMA, pltpu.SemaphoreType.DMA],
    )(buf)


def _gather_all(buf):
    R, W = buf.shape

    def body(src_ref, out_ref, send_sems, recv_sems, local_sem):
        x, y, c = lax.axis_index("x"), lax.axis_index("y"), lax.axis_index("c")
        me = 4 * x + 2 * y + c
        mine = pltpu.make_async_copy(src_ref, out_ref.at[me], local_sem)
        mine.start()
        rel = [((x + (r >> 2 & 1)) % 2, (y + (r >> 1 & 1)) % 2, (c + (r & 1)) % 2) for r in range(1, 8)]

        def copy(j, slot, to):
            return pltpu.make_async_remote_copy(src_ref=src_ref, dst_ref=out_ref.at[slot], send_sem=send_sems.at[j],
                                                recv_sem=recv_sems.at[j], device_id=to, device_id_type=MESH)

        sends = [copy(j, me, to) for j, to in enumerate(rel)]
        for cp in sends:
            cp.start()
        for j, to in enumerate(rel):
            copy(j, 4 * to[0] + 2 * to[1] + to[2], to).wait_recv()
        for cp in sends:
            cp.wait_send()
        mine.wait()

    return pl.pallas_call(
        body, name="small_gather", in_specs=[_hbm()], out_specs=_hbm(), out_shape=jax.ShapeDtypeStruct((8, R, W), buf.dtype),
        scratch_shapes=[pltpu.SemaphoreType.DMA((7,)), pltpu.SemaphoreType.DMA((7,)), pltpu.SemaphoreType.DMA],
    )(buf)


def _sum_slots(stack):
    n, R, W = stack.shape
    tr = _tile_rows(R)

    def body(s_ref, o_ref):
        tot = s_ref[0]
        for j in range(1, n):
            tot = tot + s_ref[j]
        o_ref[...] = tot

    return pl.pallas_call(
        body, name="small_sum", grid=(R // tr,), in_specs=[pl.BlockSpec((n, tr, W), lambda i: (0, i, 0))],
        out_specs=pl.BlockSpec((tr, W), lambda i: (i, 0)), out_shape=jax.ShapeDtypeStruct((R, W), F32),
        compiler_params=_cparams(("parallel",)),
    )(stack)


def _adamw(wp, gp, mp, vp, name):
    R, W = wp.shape
    tr = R
    for t in (1024, 512, 256, 128, 64, 32, 16, 8):
        if R % t == 0 and t * W <= 512 * 1024:
            tr = t
            break
    c1 = 1.0 - ADAM_B1 ** ADAM_STEP
    c2 = 1.0 - ADAM_B2 ** ADAM_STEP

    def body(w_ref, g_ref, m_ref, v_ref, d_ref, mo_ref, vo_ref):
        gv = g_ref[...]
        m = ADAM_B1 * m_ref[...] + (1.0 - ADAM_B1) * gv
        v = ADAM_B2 * v_ref[...] + (1.0 - ADAM_B2) * (gv * gv)
        m_hat = m / c1
        v_hat = v / c2
        d_ref[...] = -ADAM_LR * (m_hat / (jnp.sqrt(v_hat) + ADAM_EPS) + ADAM_WD * w_ref[...])
        mo_ref[...] = m
        vo_ref[...] = v

    blk = pl.BlockSpec((tr, W), lambda i: (i, 0))
    shp = jax.ShapeDtypeStruct((R, W), F32)
    return pl.pallas_call(body, name=name, grid=(R // tr,), in_specs=[blk] * 4, out_specs=(blk,) * 3, out_shape=(shp,) * 3,
                          compiler_params=_cparams(("parallel",)))(wp, gp, mp, vp)


def _pack(arrs, rows):
    flat = jnp.concatenate([a.reshape(-1) for a in arrs])
    return jnp.pad(flat, (0, rows * PACK_W - flat.shape[0])).reshape(rows, PACK_W)


def _unpack(buf, shapes):
    flat = buf.reshape(-1)
    out, off = [], 0
    for shp in shapes:
        n = int(np.prod(shp))
        out.append(flat[off:off + n].reshape(shp))
        off += n
    return out


def _rows_for(shapes, mult):
    n = sum(int(np.prod(s)) for s in shapes)
    rows = -(-n // PACK_W)
    return -(-rows // mult) * mult


def _shard_major(g, name):
    L, K, N = g.shape
    if name in _ROW_SHARDED:
        t = g.reshape(L, 4, K // 4, N).transpose(1, 0, 2, 3)
    else:
        t = g.reshape(L, K, 4, N // 4).transpose(2, 0, 1, 3)
    return t.reshape(4, -1, PACK_W)


def _join_shards(blocks, name):
    return jnp.concatenate(blocks, axis=1 if name in _ROW_SHARDED else 2)


def kernel(x, p, mix_norm, w_in, gate_b, conv_w, conv_b, lru_wa, lru_ba, lru_wx, lru_bx, lru_lambda, mla_q_norm, mla_wuq, mla_kv_norm, mla_wukv, fox_bf, w_br_a, w_br_b, w_br_c, w_o, ffn_norm, w_gate_up, w_down, ple_norm, w_ple_gate, w_ple, final_norm, loss_target, m_mix_norm, m_w_in, m_gate_b, m_conv_w, m_conv_b, m_lru_wa, m_lru_ba, m_lru_wx, m_lru_bx, m_lru_lambda, m_mla_q_norm, m_mla_wuq, m_mla_kv_norm, m_mla_wukv, m_fox_bf, m_w_br_a, m_w_br_b, m_w_br_c, m_w_o, m_ffn_norm, m_w_gate_up, m_w_down, m_ple_norm, m_w_ple_gate, m_w_ple, m_final_norm, v_mix_norm, v_w_in, v_gate_b, v_conv_w, v_conv_b, v_lru_wa, v_lru_ba, v_lru_wx, v_lru_bx, v_lru_lambda, v_mla_q_norm, v_mla_wuq, v_mla_kv_norm, v_mla_wukv, v_fox_bf, v_w_br_a, v_w_br_b, v_w_br_c, v_w_o, v_ffn_norm, v_w_gate_up, v_w_down, v_ple_norm, v_w_ple_gate, v_w_ple, v_final_norm):
    a = dict(locals())
    names = list(_LAYER_WEIGHTS) + ["final_norm"]
    W = {n: a[n] for n in names}
    M = {n: a["m_" + n] for n in names}
    V = {n: a["v_" + n] for n in names}
    ix, iy, ic = lax.axis_index("x"), lax.axis_index("y"), lax.axis_index("c")

    sharded = list(_BIG) + ["conv_w"]
    shard_shapes = [W[n].shape for n in sharded]
    R = _rows_for(shard_shapes, 64)
    mine = [W[n].astype(BF16) for n in _BIG] + [conv_w]
    gathered = _gather_weights(mine, "weight_gather")
    me = 2 * ix + iy
    gathered = [lax.dynamic_update_slice(g, t[None], (me,) + (0,) * t.ndim) for g, t in zip(gathered, mine)]
    full = {n: _join_shards([g[k] for k in range(4)], n) for n, g in zip(sharded, gathered)}
    conv_w_full = full["conv_w"]
    layers = []
    for i in range(DEPTH):
        lw = {n: W[n][i] for n in _SMALL}
        for n in _BIG:
            lw[n] = full[n][i]
        lw["conv_w"] = conv_w_full[i]
        layers.append(lw)

    loss_sum, dx, grads, d_final = _local_step(x[0], p[:, 0], layers, final_norm, loss_target[0])
    loss = lax.psum(loss_sum, ("x", "y", "c"))

    parts = [_shard_major(jnp.stack([grads[i][n] for i in range(DEPTH)]), n).astype(BF16) for n in sharded]
    g4, off = jnp.zeros((4, R, PACK_W), BF16), 0
    for t in parts:
        g4 = lax.dynamic_update_slice(g4, t, (0, off, 0))
        off += t.shape[1]
    c_arr = jnp.reshape(ic, (1,)).astype(jnp.int32)
    k_arr = jnp.reshape(2 * ix + iy, (1,)).astype(jnp.int32)
    pair = _pair_add(g4, _pair_swap_halves(g4), c_arr)
    g_pack = _pair_gather(_chips_add(pair, _chips_exchange(pair), k_arr, c_arr))
    big_out = {}
    for n, gsh in zip(sharded, _unpack(g_pack, shard_shapes)):
        view = lambda t: t.reshape(-1, t.shape[-1])
        d, nm, nv = _adamw(view(W[n]), view(gsh), view(M[n]), view(V[n]), "adamw_" + n)
        for key, arr in (("g", gsh), ("d", d), ("m", nm), ("v", nv)):
            big_out[(key, n)] = arr.reshape(W[n].shape)

    pick = lambda src, n, i: src[n] if i is None else src[n][i]
    small = [(n, i) for i in range(DEPTH) for n in _SMALL] + [("final_norm", None)]
    small_shapes = [pick(W, n, i).shape for n, i in small]
    Rs = _rows_for(small_shapes, 8)
    sg = _pack([d_final if i is None else grads[i][n] for n, i in small], Rs)
    sg = _sum_slots(_gather_all(sg))
    sw = _pack([pick(W, n, i) for n, i in small], Rs)
    sm = _pack([pick(M, n, i) for n, i in small], Rs)
    sv_ = _pack([pick(V, n, i) for n, i in small], Rs)
    sd, snm, snv = _adamw(sw, sg, sm, sv_, "adamw_replicated")
    small_out = {}
    for key, buf in (("g", sg), ("d", sd), ("m", snm), ("v", snv)):
        for (n, i), arr in zip(small, _unpack(buf, small_shapes)):
            small_out[(key, n, i)] = arr

    def assemble(key, n):
        if n == "final_norm":
            return small_out[(key, n, None)]
        if n in sharded:
            return big_out[(key, n)]
        return jnp.stack([small_out[(key, n, i)] for i in range(DEPTH)])

    outs = [loss, dx[None]]
    for key in ("g", "d", "m", "v"):
        outs += [assemble(key, n) for n in names]
    return tuple(outs)
```

```python
import functools
import math

import numpy as np
import jax
import jax.numpy as jnp
from jax import lax
from jax.experimental import pallas as pl
from jax.experimental.pallas import tpu as pltpu

F32, BF16 = jnp.float32, jnp.bfloat16
MESH = pl.DeviceIdType.MESH

D_MODEL = 1024
DEPTH = 2
EPS = 1e-6
NEG_INF = -1e30
LRU_WIDTH = 512
LRU_HEADS = 8
LRU_C = 8.0
CONV_WIDTH = 4
HEADS = 8
MLA_Q_LORA = 384
MLA_KV_LORA = 256
MLA_NOPE = 64
MLA_ROPE = 32
MLA_V = 64
ROPE_BASE = 10000.0
FOX_HEAD_DIM = 64
D_FF = 2816
PLE_DIM = 256
HEAD_PAD = 128
MLA_SCALE = (MLA_NOPE + MLA_ROPE) ** -0.5
FOX_SCALE = FOX_HEAD_DIM ** -0.5

ADAM_LR, ADAM_B1, ADAM_B2, ADAM_EPS, ADAM_WD, ADAM_STEP = 0.001, 0.9, 0.999, 1e-08, 0.01, 10

VMEM_LIMIT_BYTES = 48 * 1024 * 1024
LANES = 128
PACK_W = 1024

ROW_TILE = 512
ATTN_TILE = 1024
LRU_CHUNK = 512


def _cparams(dims):
    return pltpu.CompilerParams(dimension_semantics=dims, vmem_limit_bytes=VMEM_LIMIT_BYTES)


def _tile(n, cap):
    if n <= cap:
        return n
    t = (cap // LANES) * LANES
    while t >= LANES:
        if n % t == 0:
            return t
        t -= LANES
    raise ValueError(f"no tile for {n} under {cap}")


def _rows(n):
    return min(ROW_TILE, n)


MM_VMEM_BUDGET = 36 * 1024 * 1024


def _mm_tiles(M, N, K, a_bytes, b_bytes, o_bytes, has_res):
    best, best_work = None, 0
    for tm in {_tile(M, c) for c in (1024, 512, 256)}:
        for tn in {_tile(N, c) for c in (1792, 1024, 512)}:
            for tk in {_tile(K, c) for c in (2048, 1408, 1024, 512)}:
                need = 2 * (tm * tk * a_bytes + tk * tn * b_bytes + tm * tn * o_bytes + (tm * tn * 4 if has_res else 0))
                need += tm * tn * 4 if tk < K else 0
                need += tm * tn * 4
                if need <= MM_VMEM_BUDGET and tm * tn * tk > best_work:
                    best, best_work = (tm, tn, tk), tm * tn * tk
    assert best is not None, (M, N, K)
    return best

def _mm(a, b, *, ta=False, tb=False, out_dtype=F32, res=None, bias=None, name):
    K, M = a.shape if ta else a.shape[::-1]
    N, K2 = b.shape if tb else b.shape[::-1]
    assert K == K2, (name, a.shape, b.shape)
    assert res is None or bias is None
    tm, tn, tk = _mm_tiles(M, N, K, a.dtype.itemsize, b.dtype.itemsize, jnp.dtype(out_dtype).itemsize, res is not None)
    nk = K // tk
    a_spec = pl.BlockSpec((tk, tm), lambda i, j, k: (k, i)) if ta else pl.BlockSpec((tm, tk), lambda i, j, k: (i, k))
    b_spec = pl.BlockSpec((tn, tk), lambda i, j, k: (j, k)) if tb else pl.BlockSpec((tk, tn), lambda i, j, k: (k, j))
    o_spec = pl.BlockSpec((tm, tn), lambda i, j, k: (i, j))
    dn = (((0,) if ta else (1,), (1,) if tb else (0,)), ((), ()))
    if bias is not None:
        res, r_spec = bias, pl.BlockSpec((1, tn), lambda i, j, k: (0, j))
    else:
        r_spec = o_spec
    has_res = res is not None

    def body(*refs):
        a_ref, b_ref = refs[0], refs[1]
        r_ref = refs[2] if has_res else None
        o_ref = refs[3] if has_res else refs[2]
        av, bv = a_ref[...], b_ref[...]
        if av.dtype != BF16:
            av = av.astype(BF16)
        if bv.dtype != BF16:
            bv = bv.astype(BF16)
        part = lax.dot_general(av, bv, dn, preferred_element_type=F32)

        def finish(total):
            if has_res:
                total = total + r_ref[...]
            o_ref[...] = total.astype(out_dtype)

        if nk == 1:
            finish(part)
        else:
            acc = refs[-1]
            k = pl.program_id(2)

            @pl.when(k == 0)
            def _():
                acc[...] = part

            @pl.when(k > 0)
            def _():
                acc[...] += part

            @pl.when(k == nk - 1)
            def _():
                finish(acc[...])

    in_specs = [a_spec, b_spec] + ([r_spec] if has_res else [])
    args = (a, b) + ((res,) if has_res else ())
    return pl.pallas_call(
        body, name=name, grid=(M // tm, N // tn, nk), in_specs=in_specs, out_specs=o_spec,
        out_shape=jax.ShapeDtypeStruct((M, N), out_dtype),
        scratch_shapes=[pltpu.VMEM((tm, tn), F32)] if nk > 1 else [],
        compiler_params=_cparams(("parallel", "parallel", "arbitrary")),
    )(*args)


def _mm_res_norm(a, b, res, g, name):
    M, K = a.shape
    N = b.shape[1]
    tm, tk = _tile(M, 512), _tile(K, 1408)
    nk = K // tk

    def body(a_ref, b_ref, r_ref, g_ref, o_ref, h_ref, *scratch):
        part = jnp.dot(a_ref[...], b_ref[...], preferred_element_type=F32)

        def finish(total):
            xn = total + r_ref[...]
            o_ref[...] = xn
            rstd = lax.rsqrt(jnp.mean(xn * xn, axis=1, keepdims=True) + EPS)
            h_ref[...] = (xn * rstd * g_ref[...]).astype(BF16)

        if nk == 1:
            finish(part)
        else:
            acc = scratch[0]
            k = pl.program_id(1)

            @pl.when(k == 0)
            def _():
                acc[...] = part

            @pl.when(k > 0)
            def _():
                acc[...] += part

            @pl.when(k == nk - 1)
            def _():
                finish(acc[...])

    row = pl.BlockSpec((tm, N), lambda i, k: (i, 0))
    return pl.pallas_call(
        body, name=name, grid=(M // tm, nk),
        in_specs=[pl.BlockSpec((tm, tk), lambda i, k: (i, k)), pl.BlockSpec((tk, N), lambda i, k: (k, 0)), row,
                  pl.BlockSpec((1, N), lambda i, k: (0, 0))],
        out_specs=(row, row), out_shape=(jax.ShapeDtypeStruct((M, N), F32), jax.ShapeDtypeStruct((M, N), BF16)),
        scratch_shapes=[pltpu.VMEM((tm, N), F32)] if nk > 1 else [],
        compiler_params=_cparams(("parallel", "arbitrary")),
    )(a, b, res, g.reshape(1, N))


FFN_TILE = 1408
FFN_SUBTILES = ((0, 512), (512, 1024), (1024, 1408))


def _ffn_pair_columns(w_gate_up):
    F = w_gate_up.shape[-1] // 2
    parts = []
    for j in range(F // FFN_TILE):
        parts += [w_gate_up[..., j * FFN_TILE:(j + 1) * FFN_TILE], w_gate_up[..., F + j * FFN_TILE:F + (j + 1) * FFN_TILE]]
    return jnp.concatenate(parts, axis=-1)


def _ffn_unpair_columns(dw):
    F = dw.shape[-1] // 2
    n = F // FFN_TILE
    blk = [dw[..., j * FFN_TILE:(j + 1) * FFN_TILE] for j in range(2 * n)]
    return jnp.concatenate(blk[0::2] + blk[1::2], axis=-1)


def _ffn_up(hn, w_pair):
    S, D = hn.shape
    W2 = w_pair.shape[1]
    F, tf = W2 // 2, FFN_TILE
    tm = _rows(S)

    def body(h_ref, w_ref, hf_ref, act_ref):
        hv = h_ref[...]
        for lo, hi in FFN_SUBTILES:
            gt = jnp.dot(hv, w_ref[:, lo:hi], preferred_element_type=F32)
            up = jnp.dot(hv, w_ref[:, tf + lo:tf + hi], preferred_element_type=F32)
            hf_ref[:, lo:hi] = gt
            hf_ref[:, tf + lo:tf + hi] = up
            act_ref[:, lo:hi] = (gt * jax.nn.sigmoid(gt) * up).astype(BF16)

    return pl.pallas_call(
        body, name="ffn_up", grid=(S // tm, F // tf),
        in_specs=[pl.BlockSpec((tm, D), lambda i, j: (i, 0)), pl.BlockSpec((D, 2 * tf), lambda i, j: (0, j))],
        out_specs=(pl.BlockSpec((tm, 2 * tf), lambda i, j: (i, j)), pl.BlockSpec((tm, tf), lambda i, j: (i, j))),
        out_shape=(jax.ShapeDtypeStruct((S, W2), F32), jax.ShapeDtypeStruct((S, F), BF16)),
        compiler_params=_cparams(("parallel", "parallel")),
    )(hn, w_pair)


def _ffn_down_bwd(dx, w_down, hf):
    S, D = dx.shape
    F, tf = w_down.shape[0], FFN_TILE
    tm = _rows(S)

    def body(d_ref, w_ref, h_ref, o_ref):
        dv = d_ref[...].astype(BF16)
        for lo, hi in FFN_SUBTILES:
            dact = lax.dot_general(dv, w_ref[lo:hi, :], _NT, preferred_element_type=F32)
            gt, up = h_ref[:, lo:hi], h_ref[:, tf + lo:tf + hi]
            sg = jax.nn.sigmoid(gt)
            o_ref[:, lo:hi] = (dact * up * sg * (1.0 + gt * (1.0 - sg))).astype(BF16)
            o_ref[:, tf + lo:tf + hi] = (dact * gt * sg).astype(BF16)

    pair = pl.BlockSpec((tm, 2 * tf), lambda i, j: (i, j))
    return pl.pallas_call(
        body, name="ffn_down_bwd", grid=(S // tm, F // tf),
        in_specs=[pl.BlockSpec((tm, D), lambda i, j: (i, 0)), pl.BlockSpec((tf, D), lambda i, j: (j, 0)), pair],
        out_specs=pair, out_shape=jax.ShapeDtypeStruct((S, 2 * F), BF16),
        compiler_params=_cparams(("parallel", "parallel")),
    )(dx, w_down, hf)


def _rmsnorm_fwd(x, g, name):
    S, W = x.shape
    tm = _rows(S)

    def body(x_ref, g_ref, o_ref):
        xf = x_ref[...]
        rstd = lax.rsqrt(jnp.mean(xf * xf, axis=1, keepdims=True) + EPS)
        o_ref[...] = (xf * rstd * g_ref[...]).astype(BF16)

    return pl.pallas_call(
        body, name=name, grid=(S // tm,),
        in_specs=[pl.BlockSpec((tm, W), lambda i: (i, 0)), pl.BlockSpec((1, W), lambda i: (0, 0))],
        out_specs=pl.BlockSpec((tm, W), lambda i: (i, 0)),
        out_shape=jax.ShapeDtypeStruct((S, W), BF16), compiler_params=_cparams(("parallel",)),
    )(x, g.reshape(1, W))


def _rmsnorm_bwd(x, g, dy, *, add=None, out_dtype=F32, name):
    S, W = x.shape
    tm = _rows(S)
    has_add = add is not None

    def body(*refs):
        x_ref, g_ref, dy_ref = refs[:3]
        add_ref = refs[3] if has_add else None
        dx_ref, dg_ref = refs[-2], refs[-1]
        xf = x_ref[...]
        rstd = lax.rsqrt(jnp.mean(xf * xf, axis=1, keepdims=True) + EPS)
        xhat = xf * rstd
        dyv = dy_ref[...]
        dxh = dyv * g_ref[...]
        dx = rstd * (dxh - xhat * jnp.mean(dxh * xhat, axis=1, keepdims=True))
        if has_add:
            dx = dx + add_ref[...]
        dx_ref[...] = dx.astype(out_dtype)

        @pl.when(pl.program_id(0) == 0)
        def _():
            dg_ref[...] = jnp.zeros_like(dg_ref)

        dg_ref[...] += jnp.sum(dyv * xhat, axis=0, keepdims=True)

    row = pl.BlockSpec((tm, W), lambda i: (i, 0))
    vec = pl.BlockSpec((1, W), lambda i: (0, 0))
    dx, dg = pl.pallas_call(
        body, name=name, grid=(S // tm,),
        in_specs=[row, vec, row] + ([row] if has_add else []),
        out_specs=(row, vec),
        out_shape=(jax.ShapeDtypeStruct((S, W), out_dtype), jax.ShapeDtypeStruct((1, W), F32)),
        compiler_params=_cparams(("arbitrary",)),
    )(x, g.reshape(1, W), dy, *((add,) if has_add else ()))
    return dx, dg.reshape(W)


def _loss_head(x, g, target):
    S, W = x.shape
    tm = _rows(S)

    def body(x_ref, g_ref, t_ref, loss_ref, dx_ref, dg_ref):
        xf = x_ref[...]
        gv = g_ref[...]
        rstd = lax.rsqrt(jnp.mean(xf * xf, axis=1, keepdims=True) + EPS)
        xhat = xf * rstd
        err = xhat * gv - t_ref[...]
        part = 0.5 * jnp.sum(jnp.mean(err * err, axis=1, keepdims=True), axis=0, keepdims=True)
        dyv = err * (1.0 / W)
        dxh = dyv * gv
        dx_ref[...] = rstd * (dxh - xhat * jnp.mean(dxh * xhat, axis=1, keepdims=True))

        @pl.when(pl.program_id(0) == 0)
        def _():
            dg_ref[...] = jnp.zeros_like(dg_ref)
            loss_ref[...] = jnp.zeros_like(loss_ref)

        dg_ref[...] += jnp.sum(dyv * xhat, axis=0, keepdims=True)
        loss_ref[...] += part

    row = pl.BlockSpec((tm, W), lambda i: (i, 0))
    vec = pl.BlockSpec((1, W), lambda i: (0, 0))
    loss, dx, dg = pl.pallas_call(
        body, name="loss_head", grid=(S // tm,), in_specs=[row, vec, row],
        out_specs=(pl.BlockSpec((1, 1), lambda i: (0, 0)), row, vec),
        out_shape=(jax.ShapeDtypeStruct((1, 1), F32), jax.ShapeDtypeStruct((S, W), F32), jax.ShapeDtypeStruct((1, W), F32)),
        compiler_params=_cparams(("arbitrary",)),
    )(x, g.reshape(1, W), target)
    return loss[0, 0], dx, dg.reshape(W)


def _scan_fwd(a, b, row):
    T = a.shape[0]
    d = 1
    while d < T:
        keep = row >= d
        b = jnp.where(keep, a * pltpu.roll(b, d, axis=0) + b, b)
        a = jnp.where(keep, a * pltpu.roll(a, d, axis=0), a)
        d *= 2
    return a, b


def _scan_bwd(a, b, row):
    T = a.shape[0]
    d = 1
    while d < T:
        keep = row < T - d
        b = jnp.where(keep, a * pltpu.roll(b, T - d, axis=0) + b, b)
        a = jnp.where(keep, a * pltpu.roll(a, T - d, axis=0), a)
        d *= 2
    return a, b


def _expm1(x):
    small = x * (1.0 + x * (0.5 + x * (1.0 / 6 + x * (1.0 / 24 + x * (1.0 / 120 + x * (1.0 / 720 + x * (1.0 / 5040)))))))
    return jnp.where(jnp.abs(x) < 0.25, small, jnp.exp(x) - 1.0)


_GELU_C = math.sqrt(2.0 / math.pi)


def _gelu_and_grad(x):
    inner = _GELU_C * (x + 0.044715 * x * x * x)
    th = jnp.tanh(inner)
    val = 0.5 * x * (1.0 + th)
    grad = 0.5 * (1.0 + th) + 0.5 * x * (1.0 - th * th) * _GELU_C * (1.0 + 3 * 0.044715 * x * x)
    return val, grad


def _lru_gates(xc, wa, wx, ba, bx, lam):
    xcb = xc.astype(BF16)
    r = jax.nn.sigmoid(jnp.dot(xcb, wa, preferred_element_type=F32) + ba)
    ig = jax.nn.sigmoid(jnp.dot(xcb, wx, preferred_element_type=F32) + bx)
    sp = jax.nn.softplus(-lam)
    log_a = -LRU_C * r * sp
    a = jnp.exp(log_a)
    mult = jnp.sqrt(-_expm1(2.0 * log_a))
    return xcb, r, ig, sp, a, mult


def _lru_fwd(u, ug, conv_w, conv_b, wa_bd, wx_bd, ba, bx, lam):
    S, W = u.shape
    T = min(LRU_CHUNK, S)
    nl, nc = W // LANES, S // T

    def body(u_ref, ug_ref, cw_ref, cb_ref, wa_ref, wx_ref, ba_ref, bx_ref, lam_ref, ya_ref, xc_ref, h_ref, prev_u, h_carry):
        c = pl.program_id(1)

        @pl.when(c == 0)
        def _():
            prev_u[...] = jnp.zeros_like(prev_u)
            h_carry[...] = jnp.zeros_like(h_carry)

        uv = u_ref[...]
        row = lax.broadcasted_iota(jnp.int32, (T, LANES), 0)
        row8 = lax.broadcasted_iota(jnp.int32, (8, LANES), 0)
        cw = cw_ref[...]
        xc = cb_ref[...] + uv * cw[3:4, :]
        pv = prev_u[...]
        for k in range(1, CONV_WIDTH):
            us = pltpu.roll(uv, k, axis=0)
            top = jnp.where(row8 < k, pltpu.roll(pv, k, axis=0), us[0:8])
            us = jnp.concatenate([top, us[8:]], axis=0)
            xc = xc + us * cw[3 - k:4 - k, :]
        prev_u[...] = uv[T - 8:T]
        _, r, ig, sp, a, mult = _lru_gates(xc, wa_ref[...], wx_ref[...], ba_ref[...], bx_ref[...], lam_ref[...])
        bb = mult * (ig * xc)
        aa, hh = _scan_fwd(a, bb, row)
        h = hh + aa * h_carry[7:8, :]
        h_carry[...] = h[T - 8:T]
        gl, _ = _gelu_and_grad(ug_ref[...])
        ya_ref[...] = (h * gl).astype(BF16)
        xc_ref[...] = xc
        h_ref[...] = h

    seq = pl.BlockSpec((T, LANES), lambda l, c: (c, l))
    vec = pl.BlockSpec((1, LANES), lambda l, c: (0, l))
    mat = pl.BlockSpec((None, LANES, LANES), lambda l, c: (l, 0, 0))
    return pl.pallas_call(
        body, name="lru_fwd", grid=(nl, nc),
        in_specs=[seq, seq, pl.BlockSpec((CONV_WIDTH, LANES), lambda l, c: (0, l)), vec, mat, mat, vec, vec, vec],
        out_specs=(seq, seq, seq),
        out_shape=(jax.ShapeDtypeStruct((S, W), BF16), jax.ShapeDtypeStruct((S, W), F32), jax.ShapeDtypeStruct((S, W), F32)),
        scratch_shapes=[pltpu.VMEM((8, LANES), F32), pltpu.VMEM((8, LANES), F32)],
        compiler_params=_cparams(("parallel", "arbitrary")),
    )(u, ug, conv_w, conv_b.reshape(1, W), wa_bd, wx_bd, ba.reshape(1, W), bx.reshape(1, W), lam.reshape(1, W))


def _lru_bwd(dya, u, ug, xc, h, conv_w, wa_bd, wx_bd, ba, bx, lam):
    S, W = u.shape
    T = min(LRU_CHUNK, S)
    nl, nc = W // LANES, S // T
    tb8 = T // 8

    def body(dya_ref, u_ref, ug_ref, xc_ref, h_ref, hp_ref, cw_ref, wa_ref, wx_ref, ba_ref, bx_ref, lam_ref,
             du_ref, dug_ref, dcw_ref, dcb_ref, dba_ref, dbx_ref, dlam_ref, dwa_ref, dwx_ref,
             g_next, a_next, dxc_next):
        c = pl.program_id(1)

        @pl.when(c == 0)
        def _():
            g_next[...] = jnp.zeros_like(g_next)
            a_next[...] = jnp.zeros_like(a_next)
            dxc_next[...] = jnp.zeros_like(dxc_next)
            for ref in (dcw_ref, dcb_ref, dba_ref, dbx_ref, dlam_ref, dwa_ref, dwx_ref):
                ref[...] = jnp.zeros_like(ref)

        row = lax.broadcasted_iota(jnp.int32, (T, LANES), 0)
        row8 = lax.broadcasted_iota(jnp.int32, (8, LANES), 0)
        xcv = xc_ref[...]
        wa, wx = wa_ref[...], wx_ref[...]
        xcb, r, ig, sp, a, mult = _lru_gates(xcv, wa, wx, ba_ref[...], bx_ref[...], lam_ref[...])
        gl, dgl = _gelu_and_grad(ug_ref[...])
        dyav = dya_ref[...]
        hv = h_ref[...]
        dug_ref[...] = (dyav * hv * dgl).astype(BF16)
        dh = dyav * gl
        a_up = pltpu.roll(a, T - 1, axis=0)
        a_up = jnp.where(row == T - 1, a_next[0:1, :], a_up)
        prod, gg = _scan_bwd(a_up, dh, row)
        g = gg + prod * g_next[0:1, :]
        h_prev = pltpu.roll(hv, 1, axis=0)
        first_chunk = c == nc - 1
        h_before = jnp.where(first_chunk, 0.0, hp_ref[7:8, :])
        h_prev = jnp.where(row == 0, h_before, h_prev)
        da = g * h_prev
        d_mult = g * (ig * xcv)
        d_ig = g * mult * xcv
        dxc = g * mult * ig
        d_log_a = da * a - d_mult * (a * a) / mult
        d_r = d_log_a * (-LRU_C * sp)
        d_pa = d_r * r * (1.0 - r)
        d_px = d_ig * ig * (1.0 - ig)
        d_pab, d_pxb = d_pa.astype(BF16), d_px.astype(BF16)
        nt = (((1,), (1,)), ((), ()))
        tn = (((0,), (0,)), ((), ()))
        dxc = dxc + lax.dot_general(d_pab, wa, nt, preferred_element_type=F32) + lax.dot_general(d_pxb, wx, nt, preferred_element_type=F32)
        dwa_ref[...] += lax.dot_general(xcb, d_pab, tn, preferred_element_type=F32)
        dwx_ref[...] += lax.dot_general(xcb, d_pxb, tn, preferred_element_type=F32)
        dlam_ref[...] += jnp.sum(d_log_a * r, axis=0, keepdims=True)
        dba_ref[...] += jnp.sum(d_pa, axis=0, keepdims=True)
        dbx_ref[...] += jnp.sum(d_px, axis=0, keepdims=True)
        dcb_ref[...] += jnp.sum(dxc, axis=0, keepdims=True)
        uv = u_ref[...]
        cw = cw_ref[...]
        nxt = dxc_next[...]
        du = dxc * cw[3:4, :]
        dcw_ref[3:4, :] += jnp.sum(uv * dxc, axis=0, keepdims=True)
        for k in range(1, CONV_WIDTH):
            ds = pltpu.roll(dxc, T - k, axis=0)
            bot = jnp.where(row8 >= 8 - k, pltpu.roll(nxt, 8 - k, axis=0), ds[T - 8:T])
            ds = jnp.concatenate([ds[:T - 8], bot], axis=0)
            du = du + ds * cw[3 - k:4 - k, :]
            dcw_ref[3 - k:4 - k, :] += jnp.sum(uv * ds, axis=0, keepdims=True)
        du_ref[...] = du.astype(BF16)
        g_next[...] = g[0:8]
        a_next[...] = a[0:8]
        dxc_next[...] = dxc[0:8]

    seq = pl.BlockSpec((T, LANES), lambda l, c: (nc - 1 - c, l))
    before = pl.BlockSpec((8, LANES), lambda l, c: (jnp.maximum((nc - 1 - c) * tb8 - 1, 0), l))
    vec = pl.BlockSpec((1, LANES), lambda l, c: (0, l))
    cwb = pl.BlockSpec((CONV_WIDTH, LANES), lambda l, c: (0, l))
    mat = pl.BlockSpec((None, LANES, LANES), lambda l, c: (l, 0, 0))
    vshape = jax.ShapeDtypeStruct((1, W), F32)
    mshape = jax.ShapeDtypeStruct((nl, LANES, LANES), F32)
    return pl.pallas_call(
        body, name="lru_bwd", grid=(nl, nc),
        in_specs=[seq, seq, seq, seq, seq, before, cwb, mat, mat, vec, vec, vec],
        out_specs=(seq, seq, cwb, vec, vec, vec, vec, mat, mat),
        out_shape=(jax.ShapeDtypeStruct((S, W), BF16), jax.ShapeDtypeStruct((S, W), BF16),
                   jax.ShapeDtypeStruct((CONV_WIDTH, W), F32), vshape, vshape, vshape, vshape, mshape, mshape),
        scratch_shapes=[pltpu.VMEM((8, LANES), F32)] * 3,
        compiler_params=_cparams(("parallel", "arbitrary")),
    )(dya, u, ug, xc, h, h, conv_w, wa_bd, wx_bd, ba.reshape(1, W), bx.reshape(1, W), lam.reshape(1, W))


def _decay_fwd(f_logit, bf):
    S = f_logit.shape[0]
    T = min(LRU_CHUNK, S)

    def body(f_ref, b_ref, o_ref, carry):
        @pl.when(pl.program_id(0) == 0)
        def _():
            carry[...] = jnp.zeros_like(carry)

        row = lax.broadcasted_iota(jnp.int32, (T, LANES), 0)
        v = jax.nn.log_sigmoid(f_ref[...] + b_ref[...])
        d = 1
        while d < T:
            v = jnp.where(row >= d, v + pltpu.roll(v, d, axis=0), v)
            d *= 2
        v = v + carry[7:8, :]
        carry[...] = v[T - 8:T]
        o_ref[...] = v

    return pl.pallas_call(
        body, name="decay_fwd", grid=(S // T,),
        in_specs=[pl.BlockSpec((T, LANES), lambda c: (c, 0)), pl.BlockSpec((1, LANES), lambda c: (0, 0))],
        out_specs=pl.BlockSpec((T, LANES), lambda c: (c, 0)),
        out_shape=jax.ShapeDtypeStruct((S, LANES), F32), scratch_shapes=[pltpu.VMEM((8, LANES), F32)],
        compiler_params=_cparams(("arbitrary",)),
    )(f_logit, bf)


def _decay_bwd(d_dec, f_logit, bf):
    S = f_logit.shape[0]
    T = min(LRU_CHUNK, S)
    nc = S // T

    def body(dd_ref, f_ref, b_ref, df_ref, db_ref, carry):
        @pl.when(pl.program_id(0) == 0)
        def _():
            carry[...] = jnp.zeros_like(carry)
            db_ref[...] = jnp.zeros_like(db_ref)

        row = lax.broadcasted_iota(jnp.int32, (T, LANES), 0)
        v = dd_ref[...]
        d = 1
        while d < T:
            v = jnp.where(row < T - d, v + pltpu.roll(v, T - d, axis=0), v)
            d *= 2
        v = v + carry[0:1, :]
        carry[...] = v[0:8]
        df = v * jax.nn.sigmoid(-(f_ref[...] + b_ref[...]))
        df_ref[...] = df.astype(BF16)
        db_ref[...] += jnp.sum(df, axis=0, keepdims=True)

    seq = pl.BlockSpec((T, LANES), lambda c: (nc - 1 - c, 0))
    vec = pl.BlockSpec((1, LANES), lambda c: (0, 0))
    return pl.pallas_call(
        body, name="decay_bwd", grid=(nc,), in_specs=[seq, seq, vec], out_specs=(seq, vec),
        out_shape=(jax.ShapeDtypeStruct((S, LANES), BF16), jax.ShapeDtypeStruct((1, LANES), F32)),
        scratch_shapes=[pltpu.VMEM((8, LANES), F32)], compiler_params=_cparams(("arbitrary",)),
    )(d_dec, f_logit, bf)


def _rope_tables(S):
    pos = jnp.arange(S, dtype=F32)
    inv_freq = ROPE_BASE ** (-jnp.arange(0, MLA_ROPE, 2, dtype=F32) / MLA_ROPE)
    ang = pos[:, None] * inv_freq[None, :]
    cos, sin = jnp.cos(ang), jnp.sin(ang)
    half = MLA_ROPE // 2
    z = lambda n: jnp.zeros((S, n), F32)
    c_q = jnp.concatenate([jnp.ones((S, MLA_NOPE), F32), cos, cos, z(HEAD_PAD - MLA_NOPE - MLA_ROPE)], axis=1)
    c_k = jnp.concatenate([z(MLA_NOPE), cos, cos, z(HEAD_PAD - MLA_NOPE - MLA_ROPE)], axis=1)
    s_lo = jnp.concatenate([z(MLA_NOPE), -sin, z(HEAD_PAD - MLA_NOPE - half)], axis=1)
    s_hi = jnp.concatenate([z(MLA_NOPE + half), sin, z(HEAD_PAD - MLA_NOPE - MLA_ROPE)], axis=1)
    return c_q, c_k, s_lo, s_hi


def _rot(v, c, s_lo, s_hi):
    half = MLA_ROPE // 2
    return v * c + pltpu.roll(v, LANES - half, axis=1) * s_lo + pltpu.roll(v, half, axis=1) * s_hi


def _rot_t(dv, c, s_lo, s_hi):
    half = MLA_ROPE // 2
    return dv * c + pltpu.roll(dv * s_lo, half, axis=1) + pltpu.roll(dv * s_hi, LANES - half, axis=1)


def _rope_q(q_pre, c_q, s_lo, s_hi, *, transpose, out_dtype, name):
    S, W = q_pre.shape
    tm = _rows(S)
    fn = _rot_t if transpose else _rot

    def body(q_ref, c_ref, lo_ref, hi_ref, o_ref):
        c, lo, hi = c_ref[...], lo_ref[...], hi_ref[...]
        for hd in range(W // LANES):
            cols = slice(hd * LANES, (hd + 1) * LANES)
            o_ref[:, cols] = fn(q_ref[:, cols] * MLA_SCALE, c, lo, hi).astype(out_dtype)

    blk = pl.BlockSpec((tm, W), lambda i: (i, 0))
    tab = pl.BlockSpec((tm, LANES), lambda i: (i, 0))
    return pl.pallas_call(
        body, name=name, grid=(S // tm,), in_specs=[blk, tab, tab, tab], out_specs=blk,
        out_shape=jax.ShapeDtypeStruct((S, W), out_dtype), compiler_params=_cparams(("parallel",)),
    )(q_pre, c_q, s_lo, s_hi)


def _rope_k(k_pre, k_rope, c_k, s_lo, s_hi):
    S, W = k_pre.shape
    tm = _rows(S)

    def body(k_ref, r_ref, c_ref, lo_ref, hi_ref, o_ref):
        rot = _rot(r_ref[...], c_ref[...], lo_ref[...], hi_ref[...])
        for hd in range(W // LANES):
            cols = slice(hd * LANES, (hd + 1) * LANES)
            o_ref[:, cols] = (k_ref[:, cols] + rot).astype(BF16)

    blk = pl.BlockSpec((tm, W), lambda i: (i, 0))
    tab = pl.BlockSpec((tm, LANES), lambda i: (i, 0))
    return pl.pallas_call(
        body, name="rope_k", grid=(S // tm,), in_specs=[blk, tab, tab, tab, tab], out_specs=blk,
        out_shape=jax.ShapeDtypeStruct((S, W), BF16), compiler_params=_cparams(("parallel",)),
    )(k_pre, k_rope, c_k, s_lo, s_hi)


def _rope_k_bwd(dk, c_k, s_lo, s_hi):
    S, W = dk.shape
    tm = _rows(S)

    def body(dk_ref, c_ref, lo_ref, hi_ref, o_ref):
        tot = dk_ref[:, 0:LANES]
        for hd in range(1, W // LANES):
            tot = tot + dk_ref[:, hd * LANES:(hd + 1) * LANES]
        o_ref[...] = _rot_t(tot, c_ref[...], lo_ref[...], hi_ref[...]).astype(BF16)

    tab = pl.BlockSpec((tm, LANES), lambda i: (i, 0))
    return pl.pallas_call(
        body, name="rope_k_bwd", grid=(S // tm,), in_specs=[pl.BlockSpec((tm, W), lambda i: (i, 0)), tab, tab, tab],
        out_specs=tab, out_shape=jax.ShapeDtypeStruct((S, LANES), BF16), compiler_params=_cparams(("parallel",)),
    )(dk, c_k, s_lo, s_hi)


def _pairs(n, by_key):
    if by_key:
        pr = [(i, j) for j in range(n) for i in range(j, n)]
    else:
        pr = [(i, j) for i in range(n) for j in range(i + 1)]
    return (jnp.asarray(np.array([p[0] for p in pr], np.int32)), jnp.asarray(np.array([p[1] for p in pr], np.int32)), len(pr))


def _unit_mask(shape, unit, key_axis, q_off=0):
    q = lax.broadcasted_iota(jnp.int32, shape, 1 - key_axis) + q_off
    k = lax.broadcasted_iota(jnp.int32, shape, key_axis)
    if unit > 1:
        q, k = q // unit, k // unit
    return q >= k


_NT = (((1,), (1,)), ((), ()))


def _attn_fwd(q, k, v, dec_col, dec_row, *, unit, name):
    S, W = q.shape
    H = W // LANES
    T = min(ATTN_TILE, S)
    n = S // T
    qi, kj, npairs = _pairs(n, by_key=False)
    has_dec = dec_col is not None

    def body(qi_ref, kj_ref, *refs):
        if has_dec:
            q_ref, k_ref, v_ref, dc_ref, dr_ref, o_ref, lse_ref, m_s, l_s, acc = refs
        else:
            q_ref, k_ref, v_ref, o_ref, lse_ref, m_s, l_s, acc = refs
        t = pl.program_id(1)
        i, j = qi_ref[t], kj_ref[t]

        @pl.when(j == 0)
        def _():
            m_s[...] = jnp.full_like(m_s, NEG_INF)
            l_s[...] = jnp.zeros_like(l_s)
            acc[...] = jnp.zeros_like(acc)

        def step(diag):
            s = lax.dot_general(q_ref[...], k_ref[...], _NT, preferred_element_type=F32)
            if has_dec:
                s = s + (dc_ref[...] - dr_ref[...])
            if diag:
                s = jnp.where(_unit_mask((T, T), unit, 1), s, NEG_INF)
            m_prev = m_s[...]
            m_new = jnp.maximum(m_prev, jnp.max(s, axis=1, keepdims=True))
            alpha = jnp.exp(m_prev - m_new)
            p = jnp.exp(s - m_new)
            l_s[...] = alpha * l_s[...] + jnp.sum(p, axis=1, keepdims=True)
            acc[...] = alpha * acc[...] + jnp.dot(p.astype(BF16), v_ref[...], preferred_element_type=F32)
            m_s[...] = m_new

        @pl.when(j < i)
        def _():
            step(False)

        @pl.when(j == i)
        def _():
            step(True)
            o_ref[...] = (acc[...] / l_s[...]).astype(BF16)
            lse_ref[...] = m_s[...] + jnp.log(l_s[...])

    qb = pl.BlockSpec((T, LANES), lambda h, t, qi, kj: (qi[t], h))
    kb = pl.BlockSpec((T, LANES), lambda h, t, qi, kj: (kj[t], h))
    colq = pl.BlockSpec((None, T, 1), lambda h, t, qi, kj: (h, qi[t], 0))
    rowk = pl.BlockSpec((None, 1, T), lambda h, t, qi, kj: (h, 0, kj[t]))
    in_specs = [qb, kb, kb] + ([colq, rowk] if has_dec else [])
    args = (q, k, v) + ((dec_col, dec_row) if has_dec else ())
    return pl.pallas_call(
        body, name=name,
        grid_spec=pltpu.PrefetchScalarGridSpec(
            num_scalar_prefetch=2, grid=(H, npairs), in_specs=in_specs, out_specs=(qb, colq),
            scratch_shapes=[pltpu.VMEM((T, 1), F32), pltpu.VMEM((T, 1), F32), pltpu.VMEM((T, LANES), F32)]),
        out_shape=(jax.ShapeDtypeStruct((S, W), BF16), jax.ShapeDtypeStruct((H, S, 1), F32)),
        compiler_params=_cparams(("parallel", "arbitrary")),
    )(qi, kj, *args)


def _attn_delta(do, o):
    S, W = o.shape
    H = W // LANES
    tm = _rows(S)

    def body(do_ref, o_ref, d_ref):
        d_ref[...] = jnp.sum(do_ref[...].astype(F32) * o_ref[...].astype(F32), axis=1, keepdims=True)

    blk = pl.BlockSpec((tm, LANES), lambda h, i: (i, h))
    return pl.pallas_call(
        body, name="attn_delta", grid=(H, S // tm), in_specs=[blk, blk],
        out_specs=pl.BlockSpec((None, tm, 1), lambda h, i: (h, i, 0)),
        out_shape=jax.ShapeDtypeStruct((H, S, 1), F32), compiler_params=_cparams(("parallel", "parallel")),
    )(do, o)


def _attn_bwd_dq(q, k, v, do, lse, delta, dec_col, dec_row, *, unit, out_dtype, name):
    S, W = q.shape
    H = W // LANES
    T = min(ATTN_TILE, S)
    n = S // T
    qi, kj, npairs = _pairs(n, by_key=False)
    has_dec = dec_col is not None

    def body(qi_ref, kj_ref, *refs):
        if has_dec:
            q_ref, k_ref, v_ref, do_ref, lse_ref, dl_ref, dc_ref, dr_ref, dq_ref, dd_ref, acc, dacc = refs
        else:
            q_ref, k_ref, v_ref, do_ref, lse_ref, dl_ref, dq_ref, acc = refs
        t = pl.program_id(1)
        i, j = qi_ref[t], kj_ref[t]

        @pl.when(j == 0)
        def _():
            acc[...] = jnp.zeros_like(acc)
            if has_dec:
                dacc[...] = jnp.zeros_like(dacc)

        def step(diag):
            kv = k_ref[...]
            s = lax.dot_general(q_ref[...], kv, _NT, preferred_element_type=F32)
            if has_dec:
                s = s + (dc_ref[...] - dr_ref[...])
            if diag:
                s = jnp.where(_unit_mask((T, T), unit, 1), s, NEG_INF)
            p = jnp.exp(s - lse_ref[...])
            dp = lax.dot_general(do_ref[...], v_ref[...], _NT, preferred_element_type=F32)
            ds = p * (dp - dl_ref[...])
            acc[...] += jnp.dot(ds.astype(BF16), kv, preferred_element_type=F32)
            if has_dec:
                dacc[...] += jnp.sum(ds, axis=1, keepdims=True)

        @pl.when(j < i)
        def _():
            step(False)

        @pl.when(j == i)
        def _():
            step(True)
            dq_ref[...] = acc[...].astype(out_dtype)
            if has_dec:
                dd_ref[...] = dacc[...]

    qb = pl.BlockSpec((T, LANES), lambda h, t, qi, kj: (qi[t], h))
    kb = pl.BlockSpec((T, LANES), lambda h, t, qi, kj: (kj[t], h))
    colq = pl.BlockSpec((None, T, 1), lambda h, t, qi, kj: (h, qi[t], 0))
    rowk = pl.BlockSpec((None, 1, T), lambda h, t, qi, kj: (h, 0, kj[t]))
    in_specs = [qb, kb, kb, qb, colq, colq] + ([colq, rowk] if has_dec else [])
    args = (q, k, v, do, lse, delta) + ((dec_col, dec_row) if has_dec else ())
    scratch = [pltpu.VMEM((T, LANES), F32)] + ([pltpu.VMEM((T, 1), F32)] if has_dec else [])
    out_specs = (qb, colq) if has_dec else qb
    out_shape = jax.ShapeDtypeStruct((S, W), out_dtype)
    if has_dec:
        out_shape = (out_shape, jax.ShapeDtypeStruct((H, S, 1), F32))
    res = pl.pallas_call(
        body, name=name,
        grid_spec=pltpu.PrefetchScalarGridSpec(num_scalar_prefetch=2, grid=(H, npairs), in_specs=in_specs,
                                               out_specs=out_specs, scratch_shapes=scratch),
        out_shape=out_shape, compiler_params=_cparams(("parallel", "arbitrary")),
    )(qi, kj, *args)
    return res if has_dec else (res, None)


def _attn_bwd_dkv(q, k, v, do, lse_row, delta_row, dec_col, dec_row, *, unit, dk_dtype, name):
    S, W = q.shape
    H = W // LANES
    T = min(ATTN_TILE, S)
    n = S // T
    qi, kj, npairs = _pairs(n, by_key=True)
    has_dec = dec_col is not None

    def body(qi_ref, kj_ref, *refs):
        if has_dec:
            q_ref, k_ref, v_ref, do_ref, lse_ref, dl_ref, dc_ref, dr_ref, dk_ref, dv_ref, dd_ref, kacc, vacc, dacc = refs
        else:
            q_ref, k_ref, v_ref, do_ref, lse_ref, dl_ref, dk_ref, dv_ref, kacc, vacc = refs
        t = pl.program_id(1)
        i, j = qi_ref[t], kj_ref[t]

        @pl.when(i == j)
        def _():
            kacc[...] = jnp.zeros_like(kacc)
            vacc[...] = jnp.zeros_like(vacc)
            if has_dec:
                dacc[...] = jnp.zeros_like(dacc)

        def step(diag):
            qv, dov = q_ref[...], do_ref[...]
            st = lax.dot_general(k_ref[...], qv, _NT, preferred_element_type=F32)
            if has_dec:
                st = st + (dr_ref[...] - dc_ref[...])
            if diag:
                st = jnp.where(_unit_mask((T, T), unit, 0), st, NEG_INF)
            pt = jnp.exp(st - lse_ref[...])
            dpt = lax.dot_general(v_ref[...], dov, _NT, preferred_element_type=F32)
            dst = pt * (dpt - dl_ref[...])
            vacc[...] += jnp.dot(pt.astype(BF16), dov, preferred_element_type=F32)
            kacc[...] += jnp.dot(dst.astype(BF16), qv, preferred_element_type=F32)
            if has_dec:
                dacc[...] -= jnp.sum(dst, axis=1, keepdims=True)

        @pl.when(i == j)
        def _():
            step(True)

        @pl.when(i > j)
        def _():
            step(False)

        @pl.when(i == n - 1)
        def _():
            dk_ref[...] = kacc[...].astype(dk_dtype)
            dv_ref[...] = vacc[...].astype(BF16)
            if has_dec:
                dd_ref[...] = dacc[...]

    qb = pl.BlockSpec((T, LANES), lambda h, t, qi, kj: (qi[t], h))
    kb = pl.BlockSpec((T, LANES), lambda h, t, qi, kj: (kj[t], h))
    rowq = pl.BlockSpec((None, 1, T), lambda h, t, qi, kj: (h, 0, qi[t]))
    colk = pl.BlockSpec((None, T, 1), lambda h, t, qi, kj: (h, kj[t], 0))
    in_specs = [qb, kb, kb, qb, rowq, rowq] + ([colk, rowq] if has_dec else [])
    args = (q, k, v, do, lse_row, delta_row) + ((dec_col, dec_row) if has_dec else ())
    scratch = [pltpu.VMEM((T, LANES), F32)] * 2 + ([pltpu.VMEM((T, 1), F32)] if has_dec else [])
    out_specs = (kb, kb) + ((colk,) if has_dec else ())
    out_shape = (jax.ShapeDtypeStruct((S, W), dk_dtype), jax.ShapeDtypeStruct((S, W), BF16))
    if has_dec:
        out_shape = out_shape + (jax.ShapeDtypeStruct((H, S, 1), F32),)
    res = pl.pallas_call(
        body, name=name,
        grid_spec=pltpu.PrefetchScalarGridSpec(num_scalar_prefetch=2, grid=(H, npairs), in_specs=in_specs,
                                               out_specs=out_specs, scratch_shapes=scratch),
        out_shape=out_shape, compiler_params=_cparams(("parallel", "arbitrary")),
    )(qi, kj, *args)
    return res if has_dec else (res[0], res[1], None)


def _attn_bwd(q, k, v, o, lse, do, dec_col, dec_row, *, unit, dq_dtype, dk_dtype, name):
    H, S = lse.shape[0], lse.shape[1]
    delta = _attn_delta(do, o)
    dq, dd_q = _attn_bwd_dq(q, k, v, do, lse, delta, dec_col, dec_row, unit=unit, out_dtype=dq_dtype, name=name + "_dq")
    dk, dv, dd_k = _attn_bwd_dkv(q, k, v, do, lse.reshape(H, 1, S), delta.reshape(H, 1, S), dec_col, dec_row,
                                 unit=unit, dk_dtype=dk_dtype, name=name + "_dkv")
    d_dec = None if dec_col is None else dd_q + dd_k
    return dq, dk, dv, d_dec


ONES_LANE = 64
ROW_SPLIT = 1
ROW_SPLIT_BWD = 4


def _ones_lane_bias():
    one = np.zeros((HEADS, HEAD_PAD), np.float32)
    one[:, ONES_LANE] = 1.0
    return jnp.asarray(one.reshape(1, HEADS * HEAD_PAD))


def _lane_sum(t):
    tot = t[:, 0:LANES]
    for c in range(1, t.shape[1] // LANES):
        tot = tot + t[:, c * LANES:(c + 1) * LANES]
    return tot


def _fa_fwd(q, k, v, dec_row, *, unit, name):
    S, W = q.shape
    H = W // LANES
    T = min(ATTN_TILE, S)
    n, hT = S // T, T // ROW_SPLIT
    qi, kj, npairs = _pairs(n, by_key=False)
    has_dec = dec_row is not None

    def body(qi_ref, kj_ref, *refs):
        if has_dec:
            q_ref, k_ref, v_ref, dr_ref, o_ref, lse_ref, lrow_ref, m_s, acc = refs
        else:
            q_ref, k_ref, v_ref, o_ref, lse_ref, lrow_ref, m_s, acc = refs
        t = pl.program_id(1)
        i, j = qi_ref[t], kj_ref[t]

        @pl.when(j == 0)
        def _():
            m_s[...] = jnp.full_like(m_s, NEG_INF)
            acc[...] = jnp.zeros_like(acc)

        def step(diag):
            for r in range(ROW_SPLIT):
                rows = slice(r * hT, (r + 1) * hT)
                nk = (r + 1) * hT if diag else T
                s = lax.dot_general(q_ref[rows, :], k_ref[0:nk, :], _NT, preferred_element_type=F32)
                if has_dec:
                    s = s - dr_ref[:, 0:nk]
                if diag:
                    s = jnp.where(_unit_mask((hT, nk), unit, 1, r * hT), s, NEG_INF)
                m_prev = m_s[rows, :]
                m_new = jnp.maximum(m_prev, jnp.max(s, axis=1, keepdims=True))
                alpha = jnp.exp(m_prev - m_new)
                p = jnp.exp(s - jnp.tile(m_new, (1, nk // LANES)))
                acc[rows, :] = alpha * acc[rows, :] + jnp.dot(p.astype(BF16), v_ref[0:nk, :], preferred_element_type=F32)
                m_s[rows, :] = m_new

        @pl.when(j < i)
        def _():
            step(False)

        @pl.when(j == i)
        def _():
            step(True)
            av = acc[...]
            l = av[:, ONES_LANE:ONES_LANE + 1]
            lane = lax.broadcasted_iota(jnp.int32, (T, LANES), 1)
            o_ref[...] = jnp.where(lane < ONES_LANE, av / l, 0.0).astype(BF16)
            lse = m_s[...] + jnp.log(l)
            lse_ref[...] = lse
            lrow_ref[...] = lse.T[0:1, :]

    qb = pl.BlockSpec((T, LANES), lambda h, t, qi, kj: (qi[t], h))
    kb = pl.BlockSpec((T, LANES), lambda h, t, qi, kj: (kj[t], h))
    repq = pl.BlockSpec((None, T, LANES), lambda h, t, qi, kj: (h, qi[t], 0))
    rowq = pl.BlockSpec((None, 1, T), lambda h, t, qi, kj: (h, 0, qi[t]))
    rowk = pl.BlockSpec((None, 1, T), lambda h, t, qi, kj: (h, 0, kj[t]))
    in_specs = [qb, kb, kb] + ([rowk] if has_dec else [])
    args = (q, k, v) + ((dec_row,) if has_dec else ())
    return pl.pallas_call(
        body, name=name,
        grid_spec=pltpu.PrefetchScalarGridSpec(
            num_scalar_prefetch=2, grid=(H, npairs), in_specs=in_specs, out_specs=(qb, repq, rowq),
            scratch_shapes=[pltpu.VMEM((T, LANES), F32), pltpu.VMEM((T, LANES), F32)]),
        out_shape=(jax.ShapeDtypeStruct((S, W), BF16), jax.ShapeDtypeStruct((H, S, LANES), F32),
                   jax.ShapeDtypeStruct((H, 1, S), F32)),
        compiler_params=_cparams(("parallel", "arbitrary")),
    )(qi, kj, *args)


def _fa_delta(do, o):
    S, W = o.shape
    H = W // LANES
    tm = _rows(S)

    def body(do_ref, o_ref, d_ref, drow_ref):
        for hd in range(H):
            cols = slice(hd * LANES, (hd + 1) * LANES)
            d = jnp.sum(do_ref[:, cols].astype(F32) * o_ref[:, cols].astype(F32), axis=1, keepdims=True)
            rep = jnp.broadcast_to(d, (tm, LANES))
            d_ref[hd] = rep
            drow_ref[hd] = rep.T[0:1, :]

    blk = pl.BlockSpec((tm, W), lambda i: (i, 0))
    return pl.pallas_call(
        body, name="attn_delta", grid=(S // tm,), in_specs=[blk, blk],
        out_specs=(pl.BlockSpec((H, tm, LANES), lambda i: (0, i, 0)), pl.BlockSpec((H, 1, tm), lambda i: (0, 0, i))),
        out_shape=(jax.ShapeDtypeStruct((H, S, LANES), F32), jax.ShapeDtypeStruct((H, 1, S), F32)),
        compiler_params=_cparams(("parallel",)),
    )(do, o)


def _fa_bwd_dq(q, k, v, do, lse, delta, dec_row, *, unit, out_dtype, name):
    S, W = q.shape
    H = W // LANES
    T = min(ATTN_TILE, S)
    n, reps = S // T, T // LANES
    qi, kj, npairs = _pairs(n, by_key=False)
    has_dec = dec_row is not None

    def body(qi_ref, kj_ref, *refs):
        if has_dec:
            q_ref, k_ref, v_ref, do_ref, lse_ref, dl_ref, dr_ref, dq_ref, dd_ref, acc, dacc = refs
        else:
            q_ref, k_ref, v_ref, do_ref, lse_ref, dl_ref, dq_ref, acc = refs
        t = pl.program_id(1)
        i, j = qi_ref[t], kj_ref[t]

        @pl.when(j == 0)
        def _():
            acc[...] = jnp.zeros_like(acc)
            if has_dec:
                dacc[...] = jnp.zeros_like(dacc)

        def step(diag):
            kv = k_ref[...]
            s = lax.dot_general(q_ref[...], kv, _NT, preferred_element_type=F32)
            if has_dec:
                s = s - dr_ref[...]
            if diag:
                s = jnp.where(_unit_mask((T, T), unit, 1), s, NEG_INF)
            p = jnp.exp(s - jnp.tile(lse_ref[...], (1, reps)))
            dp = lax.dot_general(do_ref[...], v_ref[...], _NT, preferred_element_type=F32)
            ds = p * (dp - jnp.tile(dl_ref[...], (1, reps)))
            acc[...] += jnp.dot(ds.astype(BF16), kv, preferred_element_type=F32)
            if has_dec:
                dacc[...] += _lane_sum(ds)

        @pl.when(j < i)
        def _():
            step(False)

        @pl.when(j == i)
        def _():
            step(True)
            dq_ref[...] = acc[...].astype(out_dtype)
            if has_dec:
                dd_ref[...] = jnp.broadcast_to(jnp.sum(dacc[...], axis=1, keepdims=True), (T, LANES))

    qb = pl.BlockSpec((T, LANES), lambda h, t, qi, kj: (qi[t], h))
    kb = pl.BlockSpec((T, LANES), lambda h, t, qi, kj: (kj[t], h))
    repq = pl.BlockSpec((None, T, LANES), lambda h, t, qi, kj: (h, qi[t], 0))
    rowk = pl.BlockSpec((None, 1, T), lambda h, t, qi, kj: (h, 0, kj[t]))
    in_specs = [qb, kb, kb, qb, repq, repq] + ([rowk] if has_dec else [])
    args = (q, k, v, do, lse, delta) + ((dec_row,) if has_dec else ())
    out_shape = jax.ShapeDtypeStruct((S, W), out_dtype)
    res = pl.pallas_call(
        body, name=name,
        grid_spec=pltpu.PrefetchScalarGridSpec(num_scalar_prefetch=2, grid=(H, npairs), in_specs=in_specs,
                                               out_specs=(qb, repq) if has_dec else qb,
                                               scratch_shapes=[pltpu.VMEM((T, LANES), F32)] * (2 if has_dec else 1)),
        out_shape=(out_shape, jax.ShapeDtypeStruct((H, S, LANES), F32)) if has_dec else out_shape,
        compiler_params=_cparams(("parallel", "arbitrary")),
    )(qi, kj, *args)
    return res if has_dec else (res, None)


def _fa_bwd_dkv(q, k, v, do, lse_row, delta_row, dec_rep, *, unit, dk_dtype, name):
    S, W = q.shape
    H = W // LANES
    T = min(ATTN_TILE, S)
    n, reps = S // T, T // LANES
    qi, kj, npairs = _pairs(n, by_key=True)
    has_dec = dec_rep is not None

    def body(qi_ref, kj_ref, *refs):
        if has_dec:
            q_ref, k_ref, v_ref, do_ref, lse_ref, dl_ref, dc_ref, dk_ref, dv_ref, dd_ref, kacc, vacc, dacc = refs
        else:
            q_ref, k_ref, v_ref, do_ref, lse_ref, dl_ref, dk_ref, dv_ref, kacc, vacc = refs
        t = pl.program_id(1)
        i, j = qi_ref[t], kj_ref[t]

        @pl.when(i == j)
        def _():
            kacc[...] = jnp.zeros_like(kacc)
            vacc[...] = jnp.zeros_like(vacc)
            if has_dec:
                dacc[...] = jnp.zeros_like(dacc)

        def step(diag):
            qv, dov = q_ref[...], do_ref[...]
            st = lax.dot_general(k_ref[...], qv, _NT, preferred_element_type=F32)
            if has_dec:
                st = st - jnp.tile(dc_ref[...], (1, reps))
            if diag:
                st = jnp.where(_unit_mask((T, T), unit, 0), st, NEG_INF)
            pt = jnp.exp(st - lse_ref[...])
            dpt = lax.dot_general(v_ref[...], dov, _NT, preferred_element_type=F32)
            dst = pt * (dpt - dl_ref[...])
            vacc[...] += jnp.dot(pt.astype(BF16), dov, preferred_element_type=F32)
            kacc[...] += jnp.dot(dst.astype(BF16), qv, preferred_element_type=F32)
            if has_dec:
                dacc[...] += _lane_sum(dst)

        @pl.when(i == j)
        def _():
            step(True)

        @pl.when(i > j)
        def _():
            step(False)

        @pl.when(i == n - 1)
        def _():
            dk_ref[...] = kacc[...].astype(dk_dtype)
            dv_ref[...] = vacc[...].astype(BF16)
            if has_dec:
                dd_ref[...] = jnp.broadcast_to(-jnp.sum(dacc[...], axis=1, keepdims=True), (T, LANES))

    qb = pl.BlockSpec((T, LANES), lambda h, t, qi, kj: (qi[t], h))
    kb = pl.BlockSpec((T, LANES), lambda h, t, qi, kj: (kj[t], h))
    rowq = pl.BlockSpec((None, 1, T), lambda h, t, qi, kj: (h, 0, qi[t]))
    repk = pl.BlockSpec((None, T, LANES), lambda h, t, qi, kj: (h, kj[t], 0))
    in_specs = [qb, kb, kb, qb, rowq, rowq] + ([repk] if has_dec else [])
    args = (q, k, v, do, lse_row, delta_row) + ((dec_rep,) if has_dec else ())
    scratch = [pltpu.VMEM((T, LANES), F32)] * (3 if has_dec else 2)
    out_specs = (kb, kb) + ((repk,) if has_dec else ())
    out_shape = (jax.ShapeDtypeStruct((S, W), dk_dtype), jax.ShapeDtypeStruct((S, W), BF16))
    if has_dec:
        out_shape = out_shape + (jax.ShapeDtypeStruct((H, S, LANES), F32),)
    res = pl.pallas_call(
        body, name=name,
        grid_spec=pltpu.PrefetchScalarGridSpec(num_scalar_prefetch=2, grid=(H, npairs), in_specs=in_specs,
                                               out_specs=out_specs, scratch_shapes=scratch),
        out_shape=out_shape, compiler_params=_cparams(("parallel", "arbitrary")),
    )(qi, kj, *args)
    return res if has_dec else (res[0], res[1], None)


_TN = (((0,), (0,)), ((), ()))


def _fa_bwd_fused(q, k, v, do, o, lse, dec_row, *, unit, dq_dtype, dk_dtype, name):
    S, W = q.shape
    H = W // LANES
    T = min(ATTN_TILE, S)
    n, hT = S // T, T // ROW_SPLIT_BWD
    qi, kj, npairs = _pairs(n, by_key=False)
    has_dec = dec_row is not None

    def body(qi_ref, kj_ref, *refs):
        if has_dec:
            (q_ref, k_ref, v_ref, do_ref, o_ref, lse_ref, dr_ref, dq_ref, dk_ref, dv_ref, ddq_ref, ddk_ref,
             qacc, kacc, vacc, dl_ref, rsum, csum) = refs
        else:
            q_ref, k_ref, v_ref, do_ref, o_ref, lse_ref, dq_ref, dk_ref, dv_ref, qacc, kacc, vacc, dl_ref = refs
        t = pl.program_id(1)
        i, j = qi_ref[t], kj_ref[t]

        @pl.when(t == 0)
        def _():
            kacc[...] = jnp.zeros_like(kacc)
            vacc[...] = jnp.zeros_like(vacc)
            if has_dec:
                csum[...] = jnp.zeros_like(csum)

        @pl.when(j == 0)
        def _():
            qacc[...] = jnp.zeros_like(qacc)
            delta = jnp.sum(do_ref[...].astype(F32) * o_ref[...].astype(F32), axis=1, keepdims=True)
            dl_ref[...] = jnp.broadcast_to(delta, (T, LANES))
            if has_dec:
                rsum[...] = jnp.zeros_like(rsum)

        def step(diag):
            for r in range(ROW_SPLIT_BWD):
                rows = slice(r * hT, (r + 1) * hT)
                nk = (r + 1) * hT if diag else T
                qv, dov, kv = q_ref[rows, :], do_ref[rows, :], k_ref[0:nk, :]
                s = lax.dot_general(qv, kv, _NT, preferred_element_type=F32)
                if has_dec:
                    s = s - dr_ref[:, 0:nk]
                if diag:
                    s = jnp.where(_unit_mask((hT, nk), unit, 1, r * hT), s, NEG_INF)
                p = jnp.exp(s - jnp.tile(lse_ref[rows, :], (1, nk // LANES)))
                dp = lax.dot_general(dov, v_ref[0:nk, :], _NT, preferred_element_type=F32)
                ds = p * (dp - jnp.tile(dl_ref[rows, :], (1, nk // LANES)))
                pb, dsb = p.astype(BF16), ds.astype(BF16)
                qacc[rows, :] += jnp.dot(dsb, kv, preferred_element_type=F32)
                vacc[j, 0:nk, :] += lax.dot_general(pb, dov, _TN, preferred_element_type=F32)
                kacc[j, 0:nk, :] += lax.dot_general(dsb, qv, _TN, preferred_element_type=F32)
                if has_dec:
                    rsum[rows, :] += _lane_sum(ds)
                    csum[j, :, 0:nk] -= jnp.sum(ds, axis=0, keepdims=True)

        @pl.when(j < i)
        def _():
            step(False)

        @pl.when(j == i)
        def _():
            step(True)
            dq_ref[...] = qacc[...].astype(dq_dtype)
            if has_dec:
                ddq_ref[...] = jnp.broadcast_to(jnp.sum(rsum[...], axis=1, keepdims=True), (T, LANES))

        @pl.when(t == npairs - 1)
        def _():
            for jj in range(n):
                dk_ref[jj * T:(jj + 1) * T, :] = kacc[jj].astype(dk_dtype)
                dv_ref[jj * T:(jj + 1) * T, :] = vacc[jj].astype(BF16)
                if has_dec:
                    ddk_ref[:, jj * T:(jj + 1) * T] = csum[jj]

    qb = pl.BlockSpec((T, LANES), lambda h, t, qi, kj: (qi[t], h))
    kb = pl.BlockSpec((T, LANES), lambda h, t, qi, kj: (kj[t], h))
    head = pl.BlockSpec((S, LANES), lambda h, t, qi, kj: (0, h))
    repq = pl.BlockSpec((None, T, LANES), lambda h, t, qi, kj: (h, qi[t], 0))
    rowk = pl.BlockSpec((None, 1, T), lambda h, t, qi, kj: (h, 0, kj[t]))
    rowh = pl.BlockSpec((None, 1, S), lambda h, t, qi, kj: (h, 0, 0))
    in_specs = [qb, kb, kb, qb, qb, repq] + ([rowk] if has_dec else [])
    args = (q, k, v, do, o, lse) + ((dec_row,) if has_dec else ())
    out_specs = [qb, head, head] + ([repq, rowh] if has_dec else [])
    out_shape = [jax.ShapeDtypeStruct((S, W), dq_dtype), jax.ShapeDtypeStruct((S, W), dk_dtype), jax.ShapeDtypeStruct((S, W), BF16)]
    scratch = [pltpu.VMEM((T, LANES), F32), pltpu.VMEM((n, T, LANES), F32), pltpu.VMEM((n, T, LANES), F32),
               pltpu.VMEM((T, LANES), F32)]
    if has_dec:
        out_shape += [jax.ShapeDtypeStruct((H, S, LANES), F32), jax.ShapeDtypeStruct((H, 1, S), F32)]
        scratch += [pltpu.VMEM((T, LANES), F32), pltpu.VMEM((n, 1, T), F32)]
    res = pl.pallas_call(
        body, name=name,
        grid_spec=pltpu.PrefetchScalarGridSpec(num_scalar_prefetch=2, grid=(H, npairs), in_specs=in_specs,
                                               out_specs=tuple(out_specs), scratch_shapes=scratch),
        out_shape=tuple(out_shape),
        compiler_params=pltpu.CompilerParams(dimension_semantics=("parallel", "arbitrary"), vmem_limit_bytes=FUSED_BWD_VMEM_BYTES),
    )(qi, kj, *args)
    return res if has_dec else (res[0], res[1], res[2], None, None)


FUSED_BWD_VMEM_BYTES = 58 * 1024 * 1024


def _fa_bwd(q, k, v, o, lse, lse_row, do, dec_row, dec_rep, *, unit, dq_dtype, dk_dtype, name):
    dq, dk, dv, dd_q, dd_k = _fa_bwd_fused(q, k, v, do, o, lse, dec_row, unit=unit, dq_dtype=dq_dtype, dk_dtype=dk_dtype, name=name)
    if dd_k is None:
        return dq, dk, dv, None
    return dq, dk, dv, jnp.max(dd_q, axis=2) + dd_k.reshape(dd_k.shape[0], dd_k.shape[2])


def _fa_bwd_split(q, k, v, o, lse, lse_row, do, dec_row, dec_rep, *, unit, dq_dtype, dk_dtype, name):
    delta, delta_row = _fa_delta(do, o)
    dq, dd_q = _fa_bwd_dq(q, k, v, do, lse, delta, dec_row, unit=unit, out_dtype=dq_dtype, name=name + "_dq")
    one = lambda t: jnp.max(t, axis=2)
    dk, dv, dd_k = _fa_bwd_dkv(q, k, v, do, lse_row, delta_row, dec_rep, unit=unit, dk_dtype=dk_dtype, name=name + "_dkv")
    return dq, dk, dv, (None if dd_k is None else one(dd_q) + one(dd_k))


def _merge_fwd(ya, yb, yc, gate_logit, gate_b):
    S, D = ya.shape
    tm = min(256, S)

    def body(a_ref, b_ref, c_ref, gl_ref, gb_ref, o_ref):
        g = jax.nn.sigmoid(gl_ref[...] + gb_ref[...])
        o_ref[...] = (g[:, 0:D] * a_ref[...] + g[:, D:2 * D] * b_ref[...] + g[:, 2 * D:3 * D] * c_ref[...]).astype(BF16)

    row = pl.BlockSpec((tm, D), lambda i: (i, 0))
    return pl.pallas_call(
        body, name="merge_fwd", grid=(S // tm,),
        in_specs=[row, row, row, pl.BlockSpec((tm, 3 * D), lambda i: (i, 0)), pl.BlockSpec((1, 3 * D), lambda i: (0, 0))],
        out_specs=row, out_shape=jax.ShapeDtypeStruct((S, D), BF16), compiler_params=_cparams(("parallel",)),
    )(ya, yb, yc, gate_logit, gate_b.reshape(1, 3 * D))


def _merge_bwd(dm, ya, yb, yc, gate_logit, gate_b):
    S, D = ya.shape
    tm = min(256, S)

    def body(dm_ref, a_ref, b_ref, c_ref, gl_ref, gb_ref, da_ref, db_ref, dc_ref, dgl_ref, dgb_ref):
        g = jax.nn.sigmoid(gl_ref[...] + gb_ref[...])
        dmv = dm_ref[...]
        parts = []
        for n, (y_ref, dy_ref) in enumerate(((a_ref, da_ref), (b_ref, db_ref), (c_ref, dc_ref))):
            gn = g[:, n * D:(n + 1) * D]
            dy_ref[...] = (dmv * gn).astype(BF16)
            parts.append(dmv * y_ref[...] * gn * (1.0 - gn))
        dgl = jnp.concatenate(parts, axis=1)
        dgl_ref[...] = dgl.astype(BF16)

        @pl.when(pl.program_id(0) == 0)
        def _():
            dgb_ref[...] = jnp.zeros_like(dgb_ref)

        dgb_ref[...] += jnp.sum(dgl, axis=0, keepdims=True)

    row = pl.BlockSpec((tm, D), lambda i: (i, 0))
    wide = pl.BlockSpec((tm, 3 * D), lambda i: (i, 0))
    vec = pl.BlockSpec((1, 3 * D), lambda i: (0, 0))
    act = jax.ShapeDtypeStruct((S, D), BF16)
    da, db, dc, dgl, dgb = pl.pallas_call(
        body, name="merge_bwd", grid=(S // tm,), in_specs=[row, row, row, row, wide, vec],
        out_specs=(row, row, row, wide, vec),
        out_shape=(act, act, act, jax.ShapeDtypeStruct((S, 3 * D), BF16), jax.ShapeDtypeStruct((1, 3 * D), F32)),
        compiler_params=_cparams(("arbitrary",)),
    )(dm, ya, yb, yc, gate_logit, gate_b.reshape(1, 3 * D))
    return da, db, dc, dgl, dgb.reshape(3 * D)


def _swiglu_fwd(hf):
    S, W2 = hf.shape
    F = W2 // 2
    tm = min(128, S)

    def body(h_ref, o_ref):
        gt, up = h_ref[:, 0:F], h_ref[:, F:W2]
        o_ref[...] = (gt * jax.nn.sigmoid(gt) * up).astype(BF16)

    return pl.pallas_call(
        body, name="swiglu_fwd", grid=(S // tm,), in_specs=[pl.BlockSpec((tm, W2), lambda i: (i, 0))],
        out_specs=pl.BlockSpec((tm, F), lambda i: (i, 0)), out_shape=jax.ShapeDtypeStruct((S, F), BF16),
        compiler_params=_cparams(("parallel",)),
    )(hf)


def _swiglu_bwd(dact, hf):
    S, W2 = hf.shape
    F = W2 // 2
    tm = min(128, S)

    def body(d_ref, h_ref, o_ref):
        gt, up = h_ref[:, 0:F], h_ref[:, F:W2]
        sg = jax.nn.sigmoid(gt)
        dv = d_ref[...]
        o_ref[:, 0:F] = (dv * up * sg * (1.0 + gt * (1.0 - sg))).astype(BF16)
        o_ref[:, F:W2] = (dv * gt * sg).astype(BF16)

    return pl.pallas_call(
        body, name="swiglu_bwd", grid=(S // tm,),
        in_specs=[pl.BlockSpec((tm, F), lambda i: (i, 0)), pl.BlockSpec((tm, W2), lambda i: (i, 0))],
        out_specs=pl.BlockSpec((tm, W2), lambda i: (i, 0)), out_shape=jax.ShapeDtypeStruct((S, W2), BF16),
        compiler_params=_cparams(("parallel",)),
    )(dact, hf)


def _ple_fwd(x, pre, e, g_next):
    S, D = x.shape
    tm = _rows(S)
    with_norm = g_next is not None

    def body(*refs):
        x_ref, p_ref, e_ref = refs[:3]
        xn = x_ref[...] + jax.nn.sigmoid(p_ref[...]) * e_ref[...]
        if with_norm:
            g_ref, o_ref, h_ref = refs[3:]
            rstd = lax.rsqrt(jnp.mean(xn * xn, axis=1, keepdims=True) + EPS)
            h_ref[...] = (xn * rstd * g_ref[...]).astype(BF16)
        else:
            o_ref = refs[3]
        o_ref[...] = xn

    row = pl.BlockSpec((tm, D), lambda i: (i, 0))
    xs = jax.ShapeDtypeStruct((S, D), F32)
    if not with_norm:
        return pl.pallas_call(body, name="ple_fwd_last", grid=(S // tm,), in_specs=[row, row, row], out_specs=row,
                              out_shape=xs, compiler_params=_cparams(("parallel",)))(x, pre, e), None
    return pl.pallas_call(body, name="ple_fwd", grid=(S // tm,), in_specs=[row, row, row, pl.BlockSpec((1, D), lambda i: (0, 0))],
                          out_specs=(row, row), out_shape=(xs, jax.ShapeDtypeStruct((S, D), BF16)),
                          compiler_params=_cparams(("parallel",)))(x, pre, e, g_next.reshape(1, D))


def _ple_bwd(dx, pre, e):
    S, D = dx.shape
    tm = _rows(S)

    def body(dx_ref, p_ref, e_ref, dp_ref, de_ref):
        pg = jax.nn.sigmoid(p_ref[...])
        dxv = dx_ref[...]
        dp_ref[...] = (dxv * e_ref[...] * pg * (1.0 - pg)).astype(BF16)
        de_ref[...] = (dxv * pg).astype(BF16)

    row = pl.BlockSpec((tm, D), lambda i: (i, 0))
    act = jax.ShapeDtypeStruct((S, D), BF16)
    return pl.pallas_call(body, name="ple_bwd", grid=(S // tm,), in_specs=[row, row, row], out_specs=(row, row),
                          out_shape=(act, act), compiler_params=_cparams(("parallel",)))(dx, pre, e)


def _pad_heads(w, real):
    K = w.shape[0]
    w = w.reshape(K, HEADS, real)
    return jnp.pad(w, ((0, 0), (0, 0), (0, HEAD_PAD - real))).reshape(K, HEADS * HEAD_PAD)


def _unpad_heads(w, real):
    K = w.shape[0]
    return w.reshape(K, HEADS, HEAD_PAD)[:, :, :real].reshape(K, HEADS * real)


def _pad_head_rows(w, real):
    N = w.shape[1]
    w = w.reshape(HEADS, real, N)
    return jnp.pad(w, ((0, 0), (0, HEAD_PAD - real), (0, 0))).reshape(HEADS * HEAD_PAD, N)


def _unpad_head_rows(w, real):
    N = w.shape[1]
    return w.reshape(HEADS, HEAD_PAD, N)[:, :real].reshape(HEADS * real, N)


def _block_diag(w):
    w = w.reshape(4, 2, 64, 64)
    z = jnp.zeros((4, 64, 64), w.dtype)
    top = jnp.concatenate([w[:, 0], z], axis=2)
    bot = jnp.concatenate([z, w[:, 1]], axis=2)
    return jnp.concatenate([top, bot], axis=1)


def _block_diag_t(w):
    return jnp.stack([w[:, :64, :64], w[:, 64:, 64:]], axis=1).reshape(8, 64, 64)


_IN_SPLITS = (512, 512, 384, 288, 512, 512, 512, 8, 3072)
_IN_OFF = np.concatenate([[0], np.cumsum(_IN_SPLITS)])
_KR_OFF = 64
_SEG_NAMES = ("u", "ug", "cq", "ckv", "kr", "fq", "fk", "fv", "fl", "gate")


def _in_segments(w_in):
    c = lambda n: w_in[:, int(_IN_OFF[n]):int(_IN_OFF[n + 1])]
    kv = c(3)
    kr = jnp.pad(kv[:, MLA_KV_LORA:], ((0, 0), (_KR_OFF, LANES - _KR_OFF - MLA_ROPE)))
    fl = jnp.pad(c(7), ((0, 0), (0, LANES - HEADS)))
    fq = _pad_heads(c(4), FOX_HEAD_DIM) * jnp.asarray(FOX_SCALE, w_in.dtype)
    return [c(0), c(1), c(2), kv[:, :MLA_KV_LORA], kr, fq, _pad_heads(c(5), FOX_HEAD_DIM), _pad_heads(c(6), FOX_HEAD_DIM), fl, c(8)]


def _in_unsegment(dw_p, widths):
    offs = np.concatenate([[0], np.cumsum(widths)])
    seg = [dw_p[:, int(offs[n]):int(offs[n + 1])] for n in range(len(widths))]
    u, ug, cq, ckv, kr, fq, fk, fv, fl, gate = seg
    return jnp.concatenate([
        u, ug, cq, ckv, kr[:, _KR_OFF:_KR_OFF + MLA_ROPE], _unpad_heads(fq, FOX_HEAD_DIM) * FOX_SCALE,
        _unpad_heads(fk, FOX_HEAD_DIM), _unpad_heads(fv, FOX_HEAD_DIM), fl[:, :HEADS], gate], axis=1)


def _split_wuq(wuq):
    return _pad_heads(wuq, MLA_NOPE + MLA_ROPE)


def _split_wukv(wukv):
    w = wukv.reshape(MLA_KV_LORA, HEADS, MLA_NOPE + MLA_V)
    pad = lambda t: jnp.pad(t, ((0, 0), (0, 0), (0, HEAD_PAD - t.shape[2]))).reshape(MLA_KV_LORA, HEADS * HEAD_PAD)
    return pad(w[:, :, :MLA_NOPE]), pad(w[:, :, MLA_NOPE:])


def _merge_wukv(dk_p, dv_p):
    k = dk_p.reshape(MLA_KV_LORA, HEADS, HEAD_PAD)[:, :, :MLA_NOPE]
    v = dv_p.reshape(MLA_KV_LORA, HEADS, HEAD_PAD)[:, :, :MLA_V]
    return jnp.concatenate([k, v], axis=2).reshape(MLA_KV_LORA, HEADS * (MLA_NOPE + MLA_V))


def _heads_layout(d):
    S = d.shape[0]
    t = d[:, :HEADS].T
    return t.reshape(HEADS, 1, S), jnp.broadcast_to(t[:, :, None], (HEADS, S, LANES))


def _layer_fwd(x, h, p_i, w, g_next, tabs):
    c_q, c_k, s_lo, s_hi = tabs
    sv = {"x0": x}
    segs = _in_segments(w["w_in"])
    z = {}
    for nm, ws in zip(_SEG_NAMES, segs):
        z[nm] = _mm(h, ws, out_dtype=BF16 if nm in ("fq", "fk", "fv") else F32, bias=_ones_lane_bias() if nm == "fv" else None,
                    name="in_" + nm)
    sv.update(h=h, z=z)
    wa_bd, wx_bd = _block_diag(w["lru_wa"]).astype(BF16), _block_diag(w["lru_wx"]).astype(BF16)
    oa, xc, hs = _lru_fwd(z["u"], z["ug"], w["conv_w"], w["conv_b"], wa_bd, wx_bd, w["lru_ba"], w["lru_bx"], w["lru_lambda"])
    sv.update(oa=oa, xc=xc, hs=hs)
    qn = _rmsnorm_fwd(z["cq"], w["mla_q_norm"], "q_norm_fwd")
    kvn = _rmsnorm_fwd(z["ckv"], w["mla_kv_norm"], "kv_norm_fwd")
    wuq_p = _split_wuq(w["mla_wuq"])
    wk_p, wv_p = _split_wukv(w["mla_wukv"])
    qb = _rope_q(_mm(qn, wuq_p, name="mla_q"), c_q, s_lo, s_hi, transpose=False, out_dtype=BF16, name="rope_q")
    kb = _rope_k(_mm(kvn, wk_p, name="mla_k"), z["kr"], c_k, s_lo, s_hi)
    vb = _mm(kvn, wv_p, out_dtype=BF16, bias=_ones_lane_bias(), name="mla_v")
    ob, lse_b, lrow_b = _fa_fwd(qb, kb, vb, None, unit=64, name="mla_attn")
    sv.update(qn=qn, kvn=kvn, qb=qb, kb=kb, vb=vb, ob=ob, lse_b=lse_b, lrow_b=lrow_b)
    bf = jnp.pad(w["fox_bf"], (0, LANES - HEADS)).reshape(1, LANES)
    dec = _decay_fwd(z["fl"], bf)
    drow, drep = _heads_layout(dec)
    oc, lse_c, lrow_c = _fa_fwd(z["fq"], z["fk"], z["fv"], drow, unit=1, name="fox_attn")
    sv.update(drow=drow, drep=drep, oc=oc, lse_c=lse_c, lrow_c=lrow_c)
    ya = _mm(oa, w["w_br_a"], name="br_a")
    yb = _mm(ob, _pad_head_rows(w["w_br_b"], MLA_V), name="br_b")
    yc = _mm(oc, _pad_head_rows(w["w_br_c"], FOX_HEAD_DIM), name="br_c")
    merged = _merge_fwd(ya, yb, yc, z["gate"], w["gate_b"])
    x1, hn = _mm_res_norm(merged, w["w_o"], x, w["ffn_norm"], "w_o")
    sv.update(ya=ya, yb=yb, yc=yc, merged=merged, x1=x1)
    hf, act = _ffn_up(hn, _ffn_pair_columns(w["w_gate_up"]))
    x2, pn = _mm_res_norm(act, w["w_down"], x1, w["ple_norm"], "ffn_down")
    sv.update(hn=hn, hf=hf, act=act, x2=x2)
    pre = _mm(pn, w["w_ple_gate"], name="ple_gate")
    e = _mm(p_i, w["w_ple"], name="ple_embed")
    x3, h_next = _ple_fwd(x2, pre, e, g_next)
    sv.update(pn=pn, pre=pre, e=e, p_i=p_i)
    return x3, h_next, sv


def _layer_bwd(dx3, w, sv, tabs):
    c_q, c_k, s_lo, s_hi = tabs
    g = {}
    z = sv["z"]
    dpre, de = _ple_bwd(dx3, sv["pre"], sv["e"])
    g["w_ple"] = _mm(sv["p_i"], de, ta=True, name="d_w_ple")
    g["w_ple_gate"] = _mm(sv["pn"], dpre, ta=True, name="d_w_ple_gate")
    dpn = _mm(dpre, w["w_ple_gate"], tb=True, name="d_pn")
    dx2, g["ple_norm"] = _rmsnorm_bwd(sv["x2"], w["ple_norm"], dpn, add=dx3, name="ple_norm_bwd")
    g["w_down"] = _mm(sv["act"], dx2, ta=True, name="d_w_down")
    dhf = _ffn_down_bwd(dx2, w["w_down"], sv["hf"])
    g["w_gate_up"] = _ffn_unpair_columns(_mm(sv["hn"], dhf, ta=True, name="d_w_gate_up"))
    dhn = _mm(dhf, _ffn_pair_columns(w["w_gate_up"]), tb=True, name="d_hn")
    dx1, g["ffn_norm"] = _rmsnorm_bwd(sv["x1"], w["ffn_norm"], dhn, add=dx2, name="ffn_norm_bwd")
    g["w_o"] = _mm(sv["merged"], dx1, ta=True, name="d_w_o")
    dm = _mm(dx1, w["w_o"], tb=True, name="d_merged")
    dya, dyb, dyc, dgate, g["gate_b"] = _merge_bwd(dm, sv["ya"], sv["yb"], sv["yc"], z["gate"], w["gate_b"])
    wbb_p, wbc_p = _pad_head_rows(w["w_br_b"], MLA_V), _pad_head_rows(w["w_br_c"], FOX_HEAD_DIM)
    g["w_br_a"] = _mm(sv["oa"], dya, ta=True, name="d_w_br_a")
    g["w_br_b"] = _unpad_head_rows(_mm(sv["ob"], dyb, ta=True, name="d_w_br_b"), MLA_V)
    g["w_br_c"] = _unpad_head_rows(_mm(sv["oc"], dyc, ta=True, name="d_w_br_c"), FOX_HEAD_DIM)
    doa = _mm(dya, w["w_br_a"], tb=True, name="d_oa")
    dob = _mm(dyb, wbb_p, tb=True, out_dtype=BF16, name="d_ob")
    doc = _mm(dyc, wbc_p, tb=True, out_dtype=BF16, name="d_oc")
    dfq, dfk, dfv, d_dec = _fa_bwd(z["fq"], z["fk"], z["fv"], sv["oc"], sv["lse_c"], sv["lrow_c"], doc, sv["drow"], sv["drep"],
                                   unit=1, dq_dtype=BF16, dk_dtype=BF16, name="fox_attn_bwd")
    d_dec = jnp.pad(d_dec.T, ((0, 0), (0, LANES - HEADS)))
    bf = jnp.pad(w["fox_bf"], (0, LANES - HEADS)).reshape(1, LANES)
    dfl, dbf = _decay_bwd(d_dec, z["fl"], bf)
    g["fox_bf"] = dbf[0, :HEADS]
    dqb, dkb, dvb, _ = _fa_bwd(sv["qb"], sv["kb"], sv["vb"], sv["ob"], sv["lse_b"], sv["lrow_b"], dob, None, None,
                               unit=64, dq_dtype=F32, dk_dtype=F32, name="mla_attn_bwd")
    wuq_p = _split_wuq(w["mla_wuq"])
    wk_p, wv_p = _split_wukv(w["mla_wukv"])
    dq_pre = _rope_q(dqb, c_q, s_lo, s_hi, transpose=True, out_dtype=BF16, name="rope_q_bwd")
    dkr = _rope_k_bwd(dkb, c_k, s_lo, s_hi)
    g["mla_wuq"] = _unpad_heads(_mm(sv["qn"], dq_pre, ta=True, name="d_wuq"), MLA_NOPE + MLA_ROPE)
    g["mla_wukv"] = _merge_wukv(_mm(sv["kvn"], dkb, ta=True, name="d_wuk"), _mm(sv["kvn"], dvb, ta=True, name="d_wuv"))
    dqn = _mm(dq_pre, wuq_p, tb=True, name="d_qn")
    dkvn = _mm(dvb, wv_p, tb=True, res=_mm(dkb, wk_p, tb=True, name="d_kvn_k"), name="d_kvn")
    dcq, g["mla_q_norm"] = _rmsnorm_bwd(z["cq"], w["mla_q_norm"], dqn, out_dtype=BF16, name="q_norm_bwd")
    dckv, g["mla_kv_norm"] = _rmsnorm_bwd(z["ckv"], w["mla_kv_norm"], dkvn, out_dtype=BF16, name="kv_norm_bwd")
    wa_bd, wx_bd = _block_diag(w["lru_wa"]).astype(BF16), _block_diag(w["lru_wx"]).astype(BF16)
    du, dug, dcw, dcb, dba, dbx, dlam, dwa, dwx = _lru_bwd(
        doa, z["u"], z["ug"], sv["xc"], sv["hs"], w["conv_w"], wa_bd, wx_bd, w["lru_ba"], w["lru_bx"], w["lru_lambda"])
    g["conv_w"], g["conv_b"], g["lru_ba"], g["lru_bx"] = dcw, dcb[0], dba[0], dbx[0]
    g["lru_lambda"] = dlam[0] * LRU_C * jax.nn.sigmoid(-w["lru_lambda"])
    g["lru_wa"], g["lru_wx"] = _block_diag_t(dwa), _block_diag_t(dwx)
    dsegs = [du, dug, dcq, dckv, dkr, dfq, dfk, dfv, dfl, dgate]
    dz = jnp.concatenate(dsegs, axis=1)
    w_in_p = jnp.concatenate(_in_segments(w["w_in"]), axis=1)
    g["w_in"] = _in_unsegment(_mm(sv["h"], dz, ta=True, name="d_w_in"), [d.shape[1] for d in dsegs])
    dh = _mm(dz, w_in_p, tb=True, name="d_h")
    dx0, g["mix_norm"] = _rmsnorm_bwd(sv["x0"], w["mix_norm"], dh, add=dx1, name="mix_norm_bwd")
    return dx0, g


_LAYER_WEIGHTS = ("mix_norm", "w_in", "gate_b", "conv_w", "conv_b", "lru_wa", "lru_ba", "lru_wx", "lru_bx", "lru_lambda",
                  "mla_q_norm", "mla_wuq", "mla_kv_norm", "mla_wukv", "fox_bf", "w_br_a", "w_br_b", "w_br_c", "w_o",
                  "ffn_norm", "w_gate_up", "w_down", "ple_norm", "w_ple_gate", "w_ple")
_BIG = ("w_in", "mla_wuq", "mla_wukv", "w_br_a", "w_br_b", "w_br_c", "w_o", "w_gate_up", "w_down", "w_ple_gate", "w_ple")
_ROW_SHARDED = ("w_o", "w_down", "w_ple_gate")
_SMALL = ("mix_norm", "gate_b", "conv_b", "lru_wa", "lru_ba", "lru_wx", "lru_bx", "lru_lambda", "mla_q_norm", "mla_kv_norm",
          "fox_bf", "ffn_norm", "ple_norm")


def _local_step(x, p, layers, final_norm, target):
    tabs = _rope_tables(x.shape[0])
    saved = []
    h = _rmsnorm_fwd(x, layers[0]["mix_norm"], "mix_norm_fwd")
    for i in range(DEPTH):
        g_next = layers[i + 1]["mix_norm"] if i + 1 < DEPTH else None
        x, h, sv = _layer_fwd(x, h, p[i], layers[i], g_next, tabs)
        saved.append(sv)
    loss, dx, d_final = _loss_head(x, final_norm, target)
    grads = [None] * DEPTH
    for i in reversed(range(DEPTH)):
        dx, grads[i] = _layer_bwd(dx, layers[i], saved[i], tabs)
    return loss, dx, grads, d_final


def _hbm():
    return pl.BlockSpec(memory_space=pltpu.HBM)


def _peers(x, y):
    return [(1 - x, y), (x, 1 - y), (1 - x, 1 - y)]


def _gather_chips_two_level(shard, name):
    R, W = shard.shape
    Rh = R // 2

    def body(src_ref, out_ref, send_sems, recv_sems):
        x, y, c = lax.axis_index("x"), lax.axis_index("y"), lax.axis_index("c")
        me = 2 * x + y
        mine, other = pl.ds(c * Rh, Rh), pl.ds((1 - c) * Rh, Rh)
        peers = _peers(x, y)

        def copy(j, src, slot, rows, to):
            return pltpu.make_async_remote_copy(src_ref=src, dst_ref=out_ref.at[slot, rows], send_sem=send_sems.at[j],
                                                recv_sem=recv_sems.at[j], device_id=to, device_id_type=MESH)

        first = [copy(j, src_ref.at[mine], me, mine, (px, py, c)) for j, (px, py) in enumerate(peers)]
        for cp in first:
            cp.start()
        passed = []
        for j, (px, py) in enumerate(peers):
            slot = 2 * px + py
            copy(j, src_ref.at[mine], slot, mine, (px, py, c)).wait_recv()
            cp = copy(3 + j, out_ref.at[slot, mine], slot, mine, (x, y, 1 - c))
            cp.start()
            passed.append(cp)
        for j, (px, py) in enumerate(peers):
            copy(3 + j, src_ref.at[other], 2 * px + py, other, (x, y, 1 - c)).wait_recv()
        for cp in first + passed:
            cp.wait_send()

    return pl.pallas_call(
        body, name=name, in_specs=[_hbm()], out_specs=_hbm(), out_shape=jax.ShapeDtypeStruct((4, R, W), shard.dtype),
        scratch_shapes=[pltpu.SemaphoreType.DMA((6,)), pltpu.SemaphoreType.DMA((6,))],
    )(shard)


def _gather_weights(arrs, name):
    n_arr = len(arrs)

    def body(*refs):
        srcs, outs = refs[:n_arr], refs[n_arr:2 * n_arr]
        send_sems, recv_sems = refs[2 * n_arr:]
        x, y, c = lax.axis_index("x"), lax.axis_index("y"), lax.axis_index("c")
        me = 2 * x + y
        peers = _peers(x, y)

        def copy(sem, src, dst, to):
            return pltpu.make_async_remote_copy(src_ref=src, dst_ref=dst, send_sem=send_sems.at[sem], recv_sem=recv_sems.at[sem],
                                                device_id=to, device_id_type=MESH)

        started = []
        for a in range(n_arr):
            for j, (px, py) in enumerate(peers):
                cp = copy(6 * a + j, srcs[a].at[c], outs[a].at[me, c], (px, py, c))
                cp.start()
                started.append(cp)
        for a in range(n_arr):
            for j, (px, py) in enumerate(peers):
                landed = outs[a].at[2 * px + py, c]
                copy(6 * a + j, srcs[a].at[c], landed, (px, py, c)).wait_recv()
                cp = copy(6 * a + 3 + j, landed, landed, (x, y, 1 - c))
                cp.start()
                started.append(cp)
        for a in range(n_arr):
            for j, (px, py) in enumerate(peers):
                copy(6 * a + 3 + j, srcs[a].at[1 - c], outs[a].at[2 * px + py, 1 - c], (x, y, 1 - c)).wait_recv()
        for cp in started:
            cp.wait_send()

    return pl.pallas_call(
        body, name=name, in_specs=[_hbm()] * n_arr, out_specs=tuple([_hbm()] * n_arr),
        out_shape=tuple(jax.ShapeDtypeStruct((4,) + t.shape, t.dtype) for t in arrs),
        scratch_shapes=[pltpu.SemaphoreType.DMA((6 * n_arr,)), pltpu.SemaphoreType.DMA((6 * n_arr,))],
    )(*arrs)


def _gather_chips(shard, name):
    R, W = shard.shape

    def body(src_ref, out_ref, send_sems, recv_sems, local_sem):
        x, y, c = lax.axis_index("x"), lax.axis_index("y"), lax.axis_index("c")
        me = 2 * x + y
        mine = pltpu.make_async_copy(src_ref, out_ref.at[me], local_sem)
        mine.start()

        def copy(j, slot, to):
            return pltpu.make_async_remote_copy(src_ref=src_ref, dst_ref=out_ref.at[slot], send_sem=send_sems.at[j],
                                                recv_sem=recv_sems.at[j], device_id=(to[0], to[1], c), device_id_type=MESH)

        sends = [copy(j, me, peer) for j, peer in enumerate(_peers(x, y))]
        for cp in sends:
            cp.start()
        for j, peer in enumerate(_peers(x, y)):
            copy(j, 2 * peer[0] + peer[1], peer).wait_recv()
        for cp in sends:
            cp.wait_send()
        mine.wait()

    return pl.pallas_call(
        body, name=name, in_specs=[_hbm()], out_specs=_hbm(), out_shape=jax.ShapeDtypeStruct((4, R, W), shard.dtype),
        scratch_shapes=[pltpu.SemaphoreType.DMA((3,)), pltpu.SemaphoreType.DMA((3,)), pltpu.SemaphoreType.DMA],
    )(shard)


def _pair_swap_halves(g4):
    n, R, W = g4.shape
    Rh = R // 2

    def body(src_ref, out_ref, send_sem, recv_sem):
        x, y, c = lax.axis_index("x"), lax.axis_index("y"), lax.axis_index("c")
        cp = pltpu.make_async_remote_copy(src_ref=src_ref.at[:, pl.ds((1 - c) * Rh, Rh), :], dst_ref=out_ref, send_sem=send_sem,
                                          recv_sem=recv_sem, device_id=(x, y, 1 - c), device_id_type=MESH)
        cp.start()
        cp.wait()

    return pl.pallas_call(
        body, name="grad_pair_swap", in_specs=[_hbm()], out_specs=_hbm(), out_shape=jax.ShapeDtypeStruct((n, Rh, W), g4.dtype),
        scratch_shapes=[pltpu.SemaphoreType.DMA, pltpu.SemaphoreType.DMA],
    )(g4)


def _pair_add(g4, sib, c_arr):
    n, R, W = g4.shape
    Rh = R // 2
    tr = _tile_rows(Rh)
    nb = Rh // tr

    def body(c_ref, a_ref, b_ref, o_ref):
        o_ref[...] = (a_ref[...].astype(F32) + b_ref[...].astype(F32)).astype(o_ref.dtype)

    return pl.pallas_call(
        body, name="grad_pair_add",
        grid_spec=pltpu.PrefetchScalarGridSpec(
            num_scalar_prefetch=1, grid=(n, nb),
            in_specs=[pl.BlockSpec((None, tr, W), lambda s, i, c: (s, c[0] * nb + i, 0)), pl.BlockSpec((None, tr, W), lambda s, i, c: (s, i, 0))],
            out_specs=pl.BlockSpec((None, tr, W), lambda s, i, c: (s, i, 0))),
        out_shape=jax.ShapeDtypeStruct((n, Rh, W), g4.dtype), compiler_params=_cparams(("parallel", "parallel")),
    )(c_arr, g4, sib)


def _tile_rows(n):
    for t in (512, 480, 400, 320, 256, 240, 160, 128, 80, 64, 40, 32, 16, 8):
        if n % t == 0:
            return t
    return n


def _chips_exchange(part):
    n, Rh, W = part.shape

    def body(src_ref, out_ref, send_sems, recv_sems):
        x, y, c = lax.axis_index("x"), lax.axis_index("y"), lax.axis_index("c")

        def copy(j, to):
            return pltpu.make_async_remote_copy(src_ref=src_ref.at[2 * to[0] + to[1]], dst_ref=out_ref.at[j], send_sem=send_sems.at[j],
                                                recv_sem=recv_sems.at[j], device_id=(to[0], to[1], c), device_id_type=MESH)

        cps = [copy(j, peer) for j, peer in enumerate(_peers(x, y))]
        for cp in cps:
            cp.start()
        for cp in cps:
            cp.wait()

    return pl.pallas_call(
        body, name="grad_chips_exchange", in_specs=[_hbm()], out_specs=_hbm(), out_shape=jax.ShapeDtypeStruct((3, Rh, W), part.dtype),
        scratch_shapes=[pltpu.SemaphoreType.DMA((3,)), pltpu.SemaphoreType.DMA((3,))],
    )(part)


def _chips_add(part, got, k_arr, c_arr):
    n, Rh, W = part.shape
    tr = _tile_rows(Rh)
    nb = Rh // tr

    def body(k_ref, c_ref, a_ref, b_ref, o_ref):
        mine = pl.program_id(0) == c_ref[0]

        @pl.when(mine)
        def _():
            o_ref[...] = ((a_ref[...].astype(F32) + b_ref[0].astype(F32)) + b_ref[1].astype(F32)) + b_ref[2].astype(F32)

        @pl.when(jnp.logical_not(mine))
        def _():
            o_ref[...] = jnp.zeros_like(o_ref)

    return pl.pallas_call(
        body, name="grad_chips_add",
        grid_spec=pltpu.PrefetchScalarGridSpec(
            num_scalar_prefetch=2, grid=(2, nb),
            in_specs=[pl.BlockSpec((None, tr, W), lambda h, i, k, c: (k[0], i, 0)), pl.BlockSpec((3, tr, W), lambda h, i, k, c: (0, i, 0))],
            out_specs=pl.BlockSpec((tr, W), lambda h, i, k, c: (h * nb + i, 0))),
        out_shape=jax.ShapeDtypeStruct((2 * Rh, W), F32), compiler_params=_cparams(("parallel", "parallel")),
    )(k_arr, c_arr, part, got)


def _pair_gather(buf):
    R, W = buf.shape
    Rh = R // 2

    def body(src_ref, out_ref, send_sem, recv_sem):
        x, y, c = lax.axis_index("x"), lax.axis_index("y"), lax.axis_index("c")
        mine, other = pl.ds(c * Rh, Rh), pl.ds((1 - c) * Rh, Rh)
        pltpu.make_async_remote_copy(src_ref=src_ref.at[mine], dst_ref=out_ref.at[mine], send_sem=send_sem, recv_sem=recv_sem,
                                     device_id=(x, y, 1 - c), device_id_type=MESH).start()
        pltpu.make_async_remote_copy(src_ref=src_ref.at[mine], dst_ref=out_ref.at[other], send_sem=send_sem, recv_sem=recv_sem,
                                     device_id=(x, y, 1 - c), device_id_type=MESH).wait()

    return pl.pallas_call(
        body, name="grad_pair_gather", in_specs=[_hbm()], out_specs=_hbm(), out_shape=jax.ShapeDtypeStruct((R, W), buf.dtype),
        input_output_aliases={0: 0}, scratch_shapes=[pltpu.SemaphoreType.DMA, pltpu.SemaphoreType.DMA],
    )(buf)


def _gather_all(buf):
    R, W = buf.shape

    def body(src_ref, out_ref, send_sems, recv_sems, local_sem):
        x, y, c = lax.axis_index("x"), lax.axis_index("y"), lax.axis_index("c")
        me = 4 * x + 2 * y + c
        mine = pltpu.make_async_copy(src_ref, out_ref.at[me], local_sem)
        mine.start()
        rel = [((x + (r >> 2 & 1)) % 2, (y + (r >> 1 & 1)) % 2, (c + (r & 1)) % 2) for r in range(1, 8)]

        def copy(j, slot, to):
            return pltpu.make_async_remote_copy(src_ref=src_ref, dst_ref=out_ref.at[slot], send_sem=send_sems.at[j],
                                                recv_sem=recv_sems.at[j], device_id=to, device_id_type=MESH)

        sends = [copy(j, me, to) for j, to in enumerate(rel)]
        for cp in sends:
            cp.start()
        for j, to in enumerate(rel):
            copy(j, 4 * to[0] + 2 * to[1] + to[2], to).wait_recv()
        for cp in sends:
            cp.wait_send()
        mine.wait()

    return pl.pallas_call(
        body, name="small_gather", in_specs=[_hbm()], out_specs=_hbm(), out_shape=jax.ShapeDtypeStruct((8, R, W), buf.dtype),
        scratch_shapes=[pltpu.SemaphoreType.DMA((7,)), pltpu.SemaphoreType.DMA((7,)), pltpu.SemaphoreType.DMA],
    )(buf)


def _sum_slots(stack):
    n, R, W = stack.shape
    tr = _tile_rows(R)

    def body(s_ref, o_ref):
        tot = s_ref[0]
        for j in range(1, n):
            tot = tot + s_ref[j]
        o_ref[...] = tot

    return pl.pallas_call(
        body, name="small_sum", grid=(R // tr,), in_specs=[pl.BlockSpec((n, tr, W), lambda i: (0, i, 0))],
        out_specs=pl.BlockSpec((tr, W), lambda i: (i, 0)), out_shape=jax.ShapeDtypeStruct((R, W), F32),
        compiler_params=_cparams(("parallel",)),
    )(stack)


def _adamw(wp, gp, mp, vp, name):
    R, W = wp.shape
    tr = R
    for t in (1024, 512, 256, 128, 64, 32, 16, 8):
        if R % t == 0 and t * W <= 512 * 1024:
            tr = t
            break
    c1 = 1.0 - ADAM_B1 ** ADAM_STEP
    c2 = 1.0 - ADAM_B2 ** ADAM_STEP

    def body(w_ref, g_ref, m_ref, v_ref, d_ref, mo_ref, vo_ref):
        gv = g_ref[...]
        m = ADAM_B1 * m_ref[...] + (1.0 - ADAM_B1) * gv
        v = ADAM_B2 * v_ref[...] + (1.0 - ADAM_B2) * (gv * gv)
        m_hat = m / c1
        v_hat = v / c2
        d_ref[...] = -ADAM_LR * (m_hat / (jnp.sqrt(v_hat) + ADAM_EPS) + ADAM_WD * w_ref[...])
        mo_ref[...] = m
        vo_ref[...] = v

    blk = pl.BlockSpec((tr, W), lambda i: (i, 0))
    shp = jax.ShapeDtypeStruct((R, W), F32)
    return pl.pallas_call(body, name=name, grid=(R // tr,), in_specs=[blk] * 4, out_specs=(blk,) * 3, out_shape=(shp,) * 3,
                          compiler_params=_cparams(("parallel",)))(wp, gp, mp, vp)


def _pack(arrs, rows):
    flat = jnp.concatenate([a.reshape(-1) for a in arrs])
    return jnp.pad(flat, (0, rows * PACK_W - flat.shape[0])).reshape(rows, PACK_W)


def _unpack(buf, shapes):
    flat = buf.reshape(-1)
    out, off = [], 0
    for shp in shapes:
        n = int(np.prod(shp))
        out.append(flat[off:off + n].reshape(shp))
        off += n
    return out


def _rows_for(shapes, mult):
    n = sum(int(np.prod(s)) for s in shapes)
    rows = -(-n // PACK_W)
    return -(-rows // mult) * mult


def _shard_major(g, name):
    L, K, N = g.shape
    if name in _ROW_SHARDED:
        t = g.reshape(L, 4, K // 4, N).transpose(1, 0, 2, 3)
    else:
        t = g.reshape(L, K, 4, N // 4).transpose(2, 0, 1, 3)
    return t.reshape(4, -1, PACK_W)


def _join_shards(blocks, name):
    return jnp.concatenate(blocks, axis=1 if name in _ROW_SHARDED else 2)


def kernel(x, p, mix_norm, w_in, gate_b, conv_w, conv_b, lru_wa, lru_ba, lru_wx, lru_bx, lru_lambda, mla_q_norm, mla_wuq, mla_kv_norm, mla_wukv, fox_bf, w_br_a, w_br_b, w_br_c, w_o, ffn_norm, w_gate_up, w_down, ple_norm, w_ple_gate, w_ple, final_norm, loss_target, m_mix_norm, m_w_in, m_gate_b, m_conv_w, m_conv_b, m_lru_wa, m_lru_ba, m_lru_wx, m_lru_bx, m_lru_lambda, m_mla_q_norm, m_mla_wuq, m_mla_kv_norm, m_mla_wukv, m_fox_bf, m_w_br_a, m_w_br_b, m_w_br_c, m_w_o, m_ffn_norm, m_w_gate_up, m_w_down, m_ple_norm, m_w_ple_gate, m_w_ple, m_final_norm, v_mix_norm, v_w_in, v_gate_b, v_conv_w, v_conv_b, v_lru_wa, v_lru_ba, v_lru_wx, v_lru_bx, v_lru_lambda, v_mla_q_norm, v_mla_wuq, v_mla_kv_norm, v_mla_wukv, v_fox_bf, v_w_br_a, v_w_br_b, v_w_br_c, v_w_o, v_ffn_norm, v_w_gate_up, v_w_down, v_ple_norm, v_w_ple_gate, v_w_ple, v_final_norm):
    a = dict(locals())
    names = list(_LAYER_WEIGHTS) + ["final_norm"]
    W = {n: a[n] for n in names}
    M = {n: a["m_" + n] for n in names}
    V = {n: a["v_" + n] for n in names}
    ix, iy, ic = lax.axis_index("x"), lax.axis_index("y"), lax.axis_index("c")

    sharded = list(_BIG) + ["conv_w"]
    shard_shapes = [W[n].shape for n in sharded]
    R = _rows_for(shard_shapes, 64)
    mine = [W[n].astype(BF16) for n in _BIG] + [conv_w]
    gathered = _gather_weights(mine, "weight_gather")
    me = 2 * ix + iy
    gathered = [lax.dynamic_update_slice(g, t[None], (me,) + (0,) * t.ndim) for g, t in zip(gathered, mine)]
    full = {n: _join_shards([g[k] for k in range(4)], n) for n, g in zip(sharded, gathered)}
    conv_w_full = full["conv_w"]
    layers = []
    for i in range(DEPTH):
        lw = {n: W[n][i] for n in _SMALL}
        for n in _BIG:
            lw[n] = full[n][i]
        lw["conv_w"] = conv_w_full[i]
        layers.append(lw)

    loss_sum, dx, grads, d_final = _local_step(x[0], p[:, 0], layers, final_norm, loss_target[0])
    loss = lax.psum(loss_sum, ("x", "y", "c"))

    parts = [_shard_major(jnp.stack([grads[i][n] for i in range(DEPTH)]), n).astype(BF16) for n in sharded]
    g4, off = jnp.zeros((4, R, PACK_W), BF16), 0
    for t in parts:
        g4 = lax.dynamic_update_slice(g4, t, (0, off, 0))
        off += t.shape[1]
    c_arr = jnp.reshape(ic, (1,)).astype(jnp.int32)
    k_arr = jnp.reshape(2 * ix + iy, (1,)).astype(jnp.int32)
    pair = _pair_add(g4, _pair_swap_halves(g4), c_arr)
    g_pack = _pair_gather(_chips_add(pair, _chips_exchange(pair), k_arr, c_arr))
    big_out = {}
    for n, gsh in zip(sharded, _unpack(g_pack, shard_shapes)):
        view = lambda t: t.reshape(-1, t.shape[-1])
        d, nm, nv = _adamw(view(W[n]), view(gsh), view(M[n]), view(V[n]), "adamw_" + n)
        for key, arr in (("g", gsh), ("d", d), ("m", nm), ("v", nv)):
            big_out[(key, n)] = arr.reshape(W[n].shape)

    pick = lambda src, n, i: src[n] if i is None else src[n][i]
    small = [(n, i) for i in range(DEPTH) for n in _SMALL] + [("final_norm", None)]
    small_shapes = [pick(W, n, i).shape for n, i in small]
    Rs = _rows_for(small_shapes, 8)
    sg = _pack([d_final if i is None else grads[i][n] for n, i in small], Rs)
    sg = _sum_slots(_gather_all(sg))
    sw = _pack([pick(W, n, i) for n, i in small], Rs)
    sm = _pack([pick(M, n, i) for n, i in small], Rs)
    sv_ = _pack([pick(V, n, i) for n, i in small], Rs)
    sd, snm, snv = _adamw(sw, sg, sm, sv_, "adamw_replicated")
    small_out = {}
    for key, buf in (("g", sg), ("d", sd), ("m", snm), ("v", snv)):
        for (n, i), arr in zip(small, _unpack(buf, small_shapes)):
            small_out[(key, n, i)] = arr

    def assemble(key, n):
        if n == "final_norm":
            return small_out[(key, n, None)]
        if n in sharded:
            return big_out[(key, n)]
        return jnp.stack([small_out[(key, n, i)] for i in range(DEPTH)])

    outs = [loss, dx[None]]
    for key in ("g", "d", "m", "v"):
        outs += [assemble(key, n) for n in names]
    return tuple(outs)
```

```python
import functools
import math

import numpy as np
import jax
import jax.numpy as jnp
from jax import lax
from jax.experimental import pallas as pl
from jax.experimental.pallas import tpu as pltpu

F32, BF16 = jnp.float32, jnp.bfloat16
MESH = pl.DeviceIdType.MESH

D_MODEL = 1024
DEPTH = 2
EPS = 1e-6
NEG_INF = -1e30
LRU_WIDTH = 512
LRU_HEADS = 8
LRU_C = 8.0
CONV_WIDTH = 4
HEADS = 8
MLA_Q_LORA = 384
MLA_KV_LORA = 256
MLA_NOPE = 64
MLA_ROPE = 32
MLA_V = 64
ROPE_BASE = 10000.0
FOX_HEAD_DIM = 64
D_FF = 2816
PLE_DIM = 256
HEAD_PAD = 128
MLA_SCALE = (MLA_NOPE + MLA_ROPE) ** -0.5
FOX_SCALE = FOX_HEAD_DIM ** -0.5

ADAM_LR, ADAM_B1, ADAM_B2, ADAM_EPS, ADAM_WD, ADAM_STEP = 0.001, 0.9, 0.999, 1e-08, 0.01, 10

VMEM_LIMIT_BYTES = 48 * 1024 * 1024
LANES = 128
PACK_W = 1024

ROW_TILE = 512
ATTN_TILE = 1024
LRU_CHUNK = 512


def _cparams(dims):
    return pltpu.CompilerParams(dimension_semantics=dims, vmem_limit_bytes=VMEM_LIMIT_BYTES)


def _tile(n, cap):
    if n <= cap:
        return n
    t = (cap // LANES) * LANES
    while t >= LANES:
        if n % t == 0:
            return t
        t -= LANES
    raise ValueError(f"no tile for {n} under {cap}")


def _rows(n):
    return min(ROW_TILE, n)


MM_VMEM_BUDGET = 36 * 1024 * 1024


def _mm_tiles(M, N, K, a_bytes, b_bytes, o_bytes, has_res):
    best, best_work = None, 0
    for tm in {_tile(M, c) for c in (1024, 512, 256)}:
        for tn in {_tile(N, c) for c in (1792, 1024, 512)}:
            for tk in {_tile(K, c) for c in (2048, 1408, 1024, 512)}:
                need = 2 * (tm * tk * a_bytes + tk * tn * b_bytes + tm * tn * o_bytes + (tm * tn * 4 if has_res else 0))
                need += tm * tn * 4 if tk < K else 0
                need += tm * tn * 4
                if need <= MM_VMEM_BUDGET and tm * tn * tk > best_work:
                    best, best_work = (tm, tn, tk), tm * tn * tk
    assert best is not None, (M, N, K)
    return best

def _mm(a, b, *, ta=False, tb=False, out_dtype=F32, res=None, bias=None, name):
    K, M = a.shape if ta else a.shape[::-1]
    N, K2 = b.shape if tb else b.shape[::-1]
    assert K == K2, (name, a.shape, b.shape)
    assert res is None or bias is None
    tm, tn, tk = _mm_tiles(M, N, K, a.dtype.itemsize, b.dtype.itemsize, jnp.dtype(out_dtype).itemsize, res is not None)
    nk = K // tk
    a_spec = pl.BlockSpec((tk, tm), lambda i, j, k: (k, i)) if ta else pl.BlockSpec((tm, tk), lambda i, j, k: (i, k))
    b_spec = pl.BlockSpec((tn, tk), lambda i, j, k: (j, k)) if tb else pl.BlockSpec((tk, tn), lambda i, j, k: (k, j))
    o_spec = pl.BlockSpec((tm, tn), lambda i, j, k: (i, j))
    dn = (((0,) if ta else (1,), (1,) if tb else (0,)), ((), ()))
    if bias is not None:
        res, r_spec = bias, pl.BlockSpec((1, tn), lambda i, j, k: (0, j))
    else:
        r_spec = o_spec
    has_res = res is not None

    def body(*refs):
        a_ref, b_ref = refs[0], refs[1]
        r_ref = refs[2] if has_res else None
        o_ref = refs[3] if has_res else refs[2]
        av, bv = a_ref[...], b_ref[...]
        if av.dtype != BF16:
            av = av.astype(BF16)
        if bv.dtype != BF16:
            bv = bv.astype(BF16)
        part = lax.dot_general(av, bv, dn, preferred_element_type=F32)

        def finish(total):
            if has_res:
                total = total + r_ref[...]
            o_ref[...] = total.astype(out_dtype)

        if nk == 1:
            finish(part)
        else:
            acc = refs[-1]
            k = pl.program_id(2)

            @pl.when(k == 0)
            def _():
                acc[...] = part

            @pl.when(k > 0)
            def _():
                acc[...] += part

            @pl.when(k == nk - 1)
            def _():
                finish(acc[...])

    in_specs = [a_spec, b_spec] + ([r_spec] if has_res else [])
    args = (a, b) + ((res,) if has_res else ())
    return pl.pallas_call(
        body, name=name, grid=(M // tm, N // tn, nk), in_specs=in_specs, out_specs=o_spec,
        out_shape=jax.ShapeDtypeStruct((M, N), out_dtype),
        scratch_shapes=[pltpu.VMEM((tm, tn), F32)] if nk > 1 else [],
        compiler_params=_cparams(("parallel", "parallel", "arbitrary")),
    )(*args)


def _mm_res_norm(a, b, res, g, name):
    M, K = a.shape
    N = b.shape[1]
    tm, tk = _tile(M, 512), _tile(K, 1408)
    nk = K // tk

    def body(a_ref, b_ref, r_ref, g_ref, o_ref, h_ref, *scratch):
        part = jnp.dot(a_ref[...], b_ref[...], preferred_element_type=F32)

        def finish(total):
            xn = total + r_ref[...]
            o_ref[...] = xn
            rstd = lax.rsqrt(jnp.mean(xn * xn, axis=1, keepdims=True) + EPS)
            h_ref[...] = (xn * rstd * g_ref[...]).astype(BF16)

        if nk == 1:
            finish(part)
        else:
            acc = scratch[0]
            k = pl.program_id(1)

            @pl.when(k == 0)
            def _():
                acc[...] = part

            @pl.when(k > 0)
            def _():
                acc[...] += part

            @pl.when(k == nk - 1)
            def _():
                finish(acc[...])

    row = pl.BlockSpec((tm, N), lambda i, k: (i, 0))
    return pl.pallas_call(
        body, name=name, grid=(M // tm, nk),
        in_specs=[pl.BlockSpec((tm, tk), lambda i, k: (i, k)), pl.BlockSpec((tk, N), lambda i, k: (k, 0)), row,
                  pl.BlockSpec((1, N), lambda i, k: (0, 0))],
        out_specs=(row, row), out_shape=(jax.ShapeDtypeStruct((M, N), F32), jax.ShapeDtypeStruct((M, N), BF16)),
        scratch_shapes=[pltpu.VMEM((tm, N), F32)] if nk > 1 else [],
        compiler_params=_cparams(("parallel", "arbitrary")),
    )(a, b, res, g.reshape(1, N))


FFN_TILE = 1408
FFN_SUBTILES = ((0, 512), (512, 1024), (1024, 1408))


def _ffn_pair_columns(w_gate_up):
    F = w_gate_up.shape[-1] // 2
    parts = []
    for j in range(F // FFN_TILE):
        parts += [w_gate_up[..., j * FFN_TILE:(j + 1) * FFN_TILE], w_gate_up[..., F + j * FFN_TILE:F + (j + 1) * FFN_TILE]]
    return jnp.concatenate(parts, axis=-1)


def _ffn_unpair_columns(dw):
    F = dw.shape[-1] // 2
    n = F // FFN_TILE
    blk = [dw[..., j * FFN_TILE:(j + 1) * FFN_TILE] for j in range(2 * n)]
    return jnp.concatenate(blk[0::2] + blk[1::2], axis=-1)


def _ffn_up(hn, w_pair):
    S, D = hn.shape
    W2 = w_pair.shape[1]
    F, tf = W2 // 2, FFN_TILE
    tm = _rows(S)

    def body(h_ref, w_ref, hf_ref, act_ref):
        hv = h_ref[...]
        for lo, hi in FFN_SUBTILES:
            gt = jnp.dot(hv, w_ref[:, lo:hi], preferred_element_type=F32)
            up = jnp.dot(hv, w_ref[:, tf + lo:tf + hi], preferred_element_type=F32)
            hf_ref[:, lo:hi] = gt.astype(BF16)
            hf_ref[:, tf + lo:tf + hi] = up.astype(BF16)
            act_ref[:, lo:hi] = (gt * jax.nn.sigmoid(gt) * up).astype(BF16)

    return pl.pallas_call(
        body, name="ffn_up", grid=(S // tm, F // tf),
        in_specs=[pl.BlockSpec((tm, D), lambda i, j: (i, 0)), pl.BlockSpec((D, 2 * tf), lambda i, j: (0, j))],
        out_specs=(pl.BlockSpec((tm, 2 * tf), lambda i, j: (i, j)), pl.BlockSpec((tm, tf), lambda i, j: (i, j))),
        out_shape=(jax.ShapeDtypeStruct((S, W2), BF16), jax.ShapeDtypeStruct((S, F), BF16)),
        compiler_params=_cparams(("parallel", "parallel")),
    )(hn, w_pair)


def _ffn_down_bwd(dx, w_down, hf):
    S, D = dx.shape
    F, tf = w_down.shape[0], FFN_TILE
    tm = _rows(S)

    def body(d_ref, w_ref, h_ref, o_ref):
        dv = d_ref[...].astype(BF16)
        for lo, hi in FFN_SUBTILES:
            dact = lax.dot_general(dv, w_ref[lo:hi, :], _NT, preferred_element_type=F32)
            gt, up = h_ref[:, lo:hi].astype(F32), h_ref[:, tf + lo:tf + hi].astype(F32)
            sg = jax.nn.sigmoid(gt)
            o_ref[:, lo:hi] = (dact * up * sg * (1.0 + gt * (1.0 - sg))).astype(BF16)
            o_ref[:, tf + lo:tf + hi] = (dact * gt * sg).astype(BF16)

    pair = pl.BlockSpec((tm, 2 * tf), lambda i, j: (i, j))
    return pl.pallas_call(
        body, name="ffn_down_bwd", grid=(S // tm, F // tf),
        in_specs=[pl.BlockSpec((tm, D), lambda i, j: (i, 0)), pl.BlockSpec((tf, D), lambda i, j: (j, 0)), pair],
        out_specs=pair, out_shape=jax.ShapeDtypeStruct((S, 2 * F), BF16),
        compiler_params=_cparams(("parallel", "parallel")),
    )(dx, w_down, hf)


def _rmsnorm_fwd(x, g, name):
    S, W = x.shape
    tm = _rows(S)

    def body(x_ref, g_ref, o_ref):
        xf = x_ref[...]
        rstd = lax.rsqrt(jnp.mean(xf * xf, axis=1, keepdims=True) + EPS)
        o_ref[...] = (xf * rstd * g_ref[...]).astype(BF16)

    return pl.pallas_call(
        body, name=name, grid=(S // tm,),
        in_specs=[pl.BlockSpec((tm, W), lambda i: (i, 0)), pl.BlockSpec((1, W), lambda i: (0, 0))],
        out_specs=pl.BlockSpec((tm, W), lambda i: (i, 0)),
        out_shape=jax.ShapeDtypeStruct((S, W), BF16), compiler_params=_cparams(("parallel",)),
    )(x, g.reshape(1, W))


def _rmsnorm_bwd(x, g, dy, *, add=None, out_dtype=F32, name):
    S, W = x.shape
    tm = _rows(S)
    has_add = add is not None

    def body(*refs):
        x_ref, g_ref, dy_ref = refs[:3]
        add_ref = refs[3] if has_add else None
        dx_ref, dg_ref = refs[-2], refs[-1]
        xf = x_ref[...]
        rstd = lax.rsqrt(jnp.mean(xf * xf, axis=1, keepdims=True) + EPS)
        xhat = xf * rstd
        dyv = dy_ref[...]
        dxh = dyv * g_ref[...]
        dx = rstd * (dxh - xhat * jnp.mean(dxh * xhat, axis=1, keepdims=True))
        if has_add:
            dx = dx + add_ref[...]
        dx_ref[...] = dx.astype(out_dtype)

        @pl.when(pl.program_id(0) == 0)
        def _():
            dg_ref[...] = jnp.zeros_like(dg_ref)

        dg_ref[...] += jnp.sum(dyv * xhat, axis=0, keepdims=True)

    row = pl.BlockSpec((tm, W), lambda i: (i, 0))
    vec = pl.BlockSpec((1, W), lambda i: (0, 0))
    dx, dg = pl.pallas_call(
        body, name=name, grid=(S // tm,),
        in_specs=[row, vec, row] + ([row] if has_add else []),
        out_specs=(row, vec),
        out_shape=(jax.ShapeDtypeStruct((S, W), out_dtype), jax.ShapeDtypeStruct((1, W), F32)),
        compiler_params=_cparams(("arbitrary",)),
    )(x, g.reshape(1, W), dy, *((add,) if has_add else ()))
    return dx, dg.reshape(W)


def _loss_head(x, g, target):
    S, W = x.shape
    tm = _rows(S)

    def body(x_ref, g_ref, t_ref, loss_ref, dx_ref, dg_ref):
        xf = x_ref[...]
        gv = g_ref[...]
        rstd = lax.rsqrt(jnp.mean(xf * xf, axis=1, keepdims=True) + EPS)
        xhat = xf * rstd
        err = xhat * gv - t_ref[...]
        part = 0.5 * jnp.sum(jnp.mean(err * err, axis=1, keepdims=True), axis=0, keepdims=True)
        dyv = err * (1.0 / W)
        dxh = dyv * gv
        dx_ref[...] = rstd * (dxh - xhat * jnp.mean(dxh * xhat, axis=1, keepdims=True))

        @pl.when(pl.program_id(0) == 0)
        def _():
            dg_ref[...] = jnp.zeros_like(dg_ref)
            loss_ref[...] = jnp.zeros_like(loss_ref)

        dg_ref[...] += jnp.sum(dyv * xhat, axis=0, keepdims=True)
        loss_ref[...] += part

    row = pl.BlockSpec((tm, W), lambda i: (i, 0))
    vec = pl.BlockSpec((1, W), lambda i: (0, 0))
    loss, dx, dg = pl.pallas_call(
        body, name="loss_head", grid=(S // tm,), in_specs=[row, vec, row],
        out_specs=(pl.BlockSpec((1, 1), lambda i: (0, 0)), row, vec),
        out_shape=(jax.ShapeDtypeStruct((1, 1), F32), jax.ShapeDtypeStruct((S, W), F32), jax.ShapeDtypeStruct((1, W), F32)),
        compiler_params=_cparams(("arbitrary",)),
    )(x, g.reshape(1, W), target)
    return loss[0, 0], dx, dg.reshape(W)


def _scan_fwd(a, b, row):
    T = a.shape[0]
    d = 1
    while d < T:
        keep = row >= d
        b = jnp.where(keep, a * pltpu.roll(b, d, axis=0) + b, b)
        a = jnp.where(keep, a * pltpu.roll(a, d, axis=0), a)
        d *= 2
    return a, b


def _scan_bwd(a, b, row):
    T = a.shape[0]
    d = 1
    while d < T:
        keep = row < T - d
        b = jnp.where(keep, a * pltpu.roll(b, T - d, axis=0) + b, b)
        a = jnp.where(keep, a * pltpu.roll(a, T - d, axis=0), a)
        d *= 2
    return a, b


def _expm1(x):
    small = x * (1.0 + x * (0.5 + x * (1.0 / 6 + x * (1.0 / 24 + x * (1.0 / 120 + x * (1.0 / 720 + x * (1.0 / 5040)))))))
    return jnp.where(jnp.abs(x) < 0.25, small, jnp.exp(x) - 1.0)


_GELU_C = math.sqrt(2.0 / math.pi)


def _gelu_and_grad(x):
    inner = _GELU_C * (x + 0.044715 * x * x * x)
    th = jnp.tanh(inner)
    val = 0.5 * x * (1.0 + th)
    grad = 0.5 * (1.0 + th) + 0.5 * x * (1.0 - th * th) * _GELU_C * (1.0 + 3 * 0.044715 * x * x)
    return val, grad


def _lru_gates(xc, wa, wx, ba, bx, lam):
    xcb = xc.astype(BF16)
    r = jax.nn.sigmoid(jnp.dot(xcb, wa, preferred_element_type=F32) + ba)
    ig = jax.nn.sigmoid(jnp.dot(xcb, wx, preferred_element_type=F32) + bx)
    sp = jax.nn.softplus(-lam)
    log_a = -LRU_C * r * sp
    a = jnp.exp(log_a)
    mult = jnp.sqrt(-_expm1(2.0 * log_a))
    return xcb, r, ig, sp, a, mult


def _lru_fwd(u, ug, conv_w, conv_b, wa_bd, wx_bd, ba, bx, lam):
    S, W = u.shape
    T = min(LRU_CHUNK, S)
    nl, nc = W // LANES, S // T

    def body(u_ref, ug_ref, cw_ref, cb_ref, wa_ref, wx_ref, ba_ref, bx_ref, lam_ref, ya_ref, xc_ref, h_ref, prev_u, h_carry):
        c = pl.program_id(1)

        @pl.when(c == 0)
        def _():
            prev_u[...] = jnp.zeros_like(prev_u)
            h_carry[...] = jnp.zeros_like(h_carry)

        uv = u_ref[...]
        row = lax.broadcasted_iota(jnp.int32, (T, LANES), 0)
        row8 = lax.broadcasted_iota(jnp.int32, (8, LANES), 0)
        cw = cw_ref[...]
        xc = cb_ref[...] + uv * cw[3:4, :]
        pv = prev_u[...]
        for k in range(1, CONV_WIDTH):
            us = pltpu.roll(uv, k, axis=0)
            top = jnp.where(row8 < k, pltpu.roll(pv, k, axis=0), us[0:8])
            us = jnp.concatenate([top, us[8:]], axis=0)
            xc = xc + us * cw[3 - k:4 - k, :]
        prev_u[...] = uv[T - 8:T]
        _, r, ig, sp, a, mult = _lru_gates(xc, wa_ref[...], wx_ref[...], ba_ref[...], bx_ref[...], lam_ref[...])
        bb = mult * (ig * xc)
        aa, hh = _scan_fwd(a, bb, row)
        h = hh + aa * h_carry[7:8, :]
        h_carry[...] = h[T - 8:T]
        gl, _ = _gelu_and_grad(ug_ref[...])
        ya_ref[...] = (h * gl).astype(BF16)
        xc_ref[...] = xc
        h_ref[...] = h

    seq = pl.BlockSpec((T, LANES), lambda l, c: (c, l))
    vec = pl.BlockSpec((1, LANES), lambda l, c: (0, l))
    mat = pl.BlockSpec((None, LANES, LANES), lambda l, c: (l, 0, 0))
    return pl.pallas_call(
        body, name="lru_fwd", grid=(nl, nc),
        in_specs=[seq, seq, pl.BlockSpec((CONV_WIDTH, LANES), lambda l, c: (0, l)), vec, mat, mat, vec, vec, vec],
        out_specs=(seq, seq, seq),
        out_shape=(jax.ShapeDtypeStruct((S, W), BF16), jax.ShapeDtypeStruct((S, W), F32), jax.ShapeDtypeStruct((S, W), F32)),
        scratch_shapes=[pltpu.VMEM((8, LANES), F32), pltpu.VMEM((8, LANES), F32)],
        compiler_params=_cparams(("parallel", "arbitrary")),
    )(u, ug, conv_w, conv_b.reshape(1, W), wa_bd, wx_bd, ba.reshape(1, W), bx.reshape(1, W), lam.reshape(1, W))


def _lru_bwd(dya, u, ug, xc, h, conv_w, wa_bd, wx_bd, ba, bx, lam):
    S, W = u.shape
    T = min(LRU_CHUNK, S)
    nl, nc = W // LANES, S // T
    tb8 = T // 8

    def body(dya_ref, u_ref, ug_ref, xc_ref, h_ref, hp_ref, cw_ref, wa_ref, wx_ref, ba_ref, bx_ref, lam_ref,
             du_ref, dug_ref, dcw_ref, dcb_ref, dba_ref, dbx_ref, dlam_ref, dwa_ref, dwx_ref,
             g_next, a_next, dxc_next):
        c = pl.program_id(1)

        @pl.when(c == 0)
        def _():
            g_next[...] = jnp.zeros_like(g_next)
            a_next[...] = jnp.zeros_like(a_next)
            dxc_next[...] = jnp.zeros_like(dxc_next)
            for ref in (dcw_ref, dcb_ref, dba_ref, dbx_ref, dlam_ref, dwa_ref, dwx_ref):
                ref[...] = jnp.zeros_like(ref)

        row = lax.broadcasted_iota(jnp.int32, (T, LANES), 0)
        row8 = lax.broadcasted_iota(jnp.int32, (8, LANES), 0)
        xcv = xc_ref[...]
        wa, wx = wa_ref[...], wx_ref[...]
        xcb, r, ig, sp, a, mult = _lru_gates(xcv, wa, wx, ba_ref[...], bx_ref[...], lam_ref[...])
        gl, dgl = _gelu_and_grad(ug_ref[...])
        dyav = dya_ref[...]
        hv = h_ref[...]
        dug_ref[...] = (dyav * hv * dgl).astype(BF16)
        dh = dyav * gl
        a_up = pltpu.roll(a, T - 1, axis=0)
        a_up = jnp.where(row == T - 1, a_next[0:1, :], a_up)
        prod, gg = _scan_bwd(a_up, dh, row)
        g = gg + prod * g_next[0:1, :]
        h_prev = pltpu.roll(hv, 1, axis=0)
        first_chunk = c == nc - 1
        h_before = jnp.where(first_chunk, 0.0, hp_ref[7:8, :])
        h_prev = jnp.where(row == 0, h_before, h_prev)
        da = g * h_prev
        d_mult = g * (ig * xcv)
        d_ig = g * mult * xcv
        dxc = g * mult * ig
        d_log_a = da * a - d_mult * (a * a) / mult
        d_r = d_log_a * (-LRU_C * sp)
        d_pa = d_r * r * (1.0 - r)
        d_px = d_ig * ig * (1.0 - ig)
        d_pab, d_pxb = d_pa.astype(BF16), d_px.astype(BF16)
        nt = (((1,), (1,)), ((), ()))
        tn = (((0,), (0,)), ((), ()))
        dxc = dxc + lax.dot_general(d_pab, wa, nt, preferred_element_type=F32) + lax.dot_general(d_pxb, wx, nt, preferred_element_type=F32)
        dwa_ref[...] += lax.dot_general(xcb, d_pab, tn, preferred_element_type=F32)
        dwx_ref[...] += lax.dot_general(xcb, d_pxb, tn, preferred_element_type=F32)
        dlam_ref[...] += jnp.sum(d_log_a * r, axis=0, keepdims=True)
        dba_ref[...] += jnp.sum(d_pa, axis=0, keepdims=True)
        dbx_ref[...] += jnp.sum(d_px, axis=0, keepdims=True)
        dcb_ref[...] += jnp.sum(dxc, axis=0, keepdims=True)
        uv = u_ref[...]
        cw = cw_ref[...]
        nxt = dxc_next[...]
        du = dxc * cw[3:4, :]
        dcw_ref[3:4, :] += jnp.sum(uv * dxc, axis=0, keepdims=True)
        for k in range(1, CONV_WIDTH):
            ds = pltpu.roll(dxc, T - k, axis=0)
            bot = jnp.where(row8 >= 8 - k, pltpu.roll(nxt, 8 - k, axis=0), ds[T - 8:T])
            ds = jnp.concatenate([ds[:T - 8], bot], axis=0)
            du = du + ds * cw[3 - k:4 - k, :]
            dcw_ref[3 - k:4 - k, :] += jnp.sum(uv * ds, axis=0, keepdims=True)
        du_ref[...] = du.astype(BF16)
        g_next[...] = g[0:8]
        a_next[...] = a[0:8]
        dxc_next[...] = dxc[0:8]

    seq = pl.BlockSpec((T, LANES), lambda l, c: (nc - 1 - c, l))
    before = pl.BlockSpec((8, LANES), lambda l, c: (jnp.maximum((nc - 1 - c) * tb8 - 1, 0), l))
    vec = pl.BlockSpec((1, LANES), lambda l, c: (0, l))
    cwb = pl.BlockSpec((CONV_WIDTH, LANES), lambda l, c: (0, l))
    mat = pl.BlockSpec((None, LANES, LANES), lambda l, c: (l, 0, 0))
    vshape = jax.ShapeDtypeStruct((1, W), F32)
    mshape = jax.ShapeDtypeStruct((nl, LANES, LANES), F32)
    return pl.pallas_call(
        body, name="lru_bwd", grid=(nl, nc),
        in_specs=[seq, seq, seq, seq, seq, before, cwb, mat, mat, vec, vec, vec],
        out_specs=(seq, seq, cwb, vec, vec, vec, vec, mat, mat),
        out_shape=(jax.ShapeDtypeStruct((S, W), BF16), jax.ShapeDtypeStruct((S, W), BF16),
                   jax.ShapeDtypeStruct((CONV_WIDTH, W), F32), vshape, vshape, vshape, vshape, mshape, mshape),
        scratch_shapes=[pltpu.VMEM((8, LANES), F32)] * 3,
        compiler_params=_cparams(("parallel", "arbitrary")),
    )(dya, u, ug, xc, h, h, conv_w, wa_bd, wx_bd, ba.reshape(1, W), bx.reshape(1, W), lam.reshape(1, W))


def _decay_fwd(f_logit, bf):
    S = f_logit.shape[0]
    T = min(LRU_CHUNK, S)

    def body(f_ref, b_ref, o_ref, carry):
        @pl.when(pl.program_id(0) == 0)
        def _():
            carry[...] = jnp.zeros_like(carry)

        row = lax.broadcasted_iota(jnp.int32, (T, LANES), 0)
        v = jax.nn.log_sigmoid(f_ref[...] + b_ref[...])
        d = 1
        while d < T:
            v = jnp.where(row >= d, v + pltpu.roll(v, d, axis=0), v)
            d *= 2
        v = v + carry[7:8, :]
        carry[...] = v[T - 8:T]
        o_ref[...] = v

    return pl.pallas_call(
        body, name="decay_fwd", grid=(S // T,),
        in_specs=[pl.BlockSpec((T, LANES), lambda c: (c, 0)), pl.BlockSpec((1, LANES), lambda c: (0, 0))],
        out_specs=pl.BlockSpec((T, LANES), lambda c: (c, 0)),
        out_shape=jax.ShapeDtypeStruct((S, LANES), F32), scratch_shapes=[pltpu.VMEM((8, LANES), F32)],
        compiler_params=_cparams(("arbitrary",)),
    )(f_logit, bf)


def _decay_bwd(d_dec, f_logit, bf):
    S = f_logit.shape[0]
    T = min(LRU_CHUNK, S)
    nc = S // T

    def body(dd_ref, f_ref, b_ref, df_ref, db_ref, carry):
        @pl.when(pl.program_id(0) == 0)
        def _():
            carry[...] = jnp.zeros_like(carry)
            db_ref[...] = jnp.zeros_like(db_ref)

        row = lax.broadcasted_iota(jnp.int32, (T, LANES), 0)
        v = dd_ref[...]
        d = 1
        while d < T:
            v = jnp.where(row < T - d, v + pltpu.roll(v, T - d, axis=0), v)
            d *= 2
        v = v + carry[0:1, :]
        carry[...] = v[0:8]
        df = v * jax.nn.sigmoid(-(f_ref[...] + b_ref[...]))
        df_ref[...] = df.astype(BF16)
        db_ref[...] += jnp.sum(df, axis=0, keepdims=True)

    seq = pl.BlockSpec((T, LANES), lambda c: (nc - 1 - c, 0))
    vec = pl.BlockSpec((1, LANES), lambda c: (0, 0))
    return pl.pallas_call(
        body, name="decay_bwd", grid=(nc,), in_specs=[seq, seq, vec], out_specs=(seq, vec),
        out_shape=(jax.ShapeDtypeStruct((S, LANES), BF16), jax.ShapeDtypeStruct((1, LANES), F32)),
        scratch_shapes=[pltpu.VMEM((8, LANES), F32)], compiler_params=_cparams(("arbitrary",)),
    )(d_dec, f_logit, bf)


def _rope_tables(S):
    pos = jnp.arange(S, dtype=F32)
    inv_freq = ROPE_BASE ** (-jnp.arange(0, MLA_ROPE, 2, dtype=F32) / MLA_ROPE)
    ang = pos[:, None] * inv_freq[None, :]
    cos, sin = jnp.cos(ang), jnp.sin(ang)
    half = MLA_ROPE // 2
    z = lambda n: jnp.zeros((S, n), F32)
    c_q = jnp.concatenate([jnp.ones((S, MLA_NOPE), F32), cos, cos, z(HEAD_PAD - MLA_NOPE - MLA_ROPE)], axis=1)
    c_k = jnp.concatenate([z(MLA_NOPE), cos, cos, z(HEAD_PAD - MLA_NOPE - MLA_ROPE)], axis=1)
    s_lo = jnp.concatenate([z(MLA_NOPE), -sin, z(HEAD_PAD - MLA_NOPE - half)], axis=1)
    s_hi = jnp.concatenate([z(MLA_NOPE + half), sin, z(HEAD_PAD - MLA_NOPE - MLA_ROPE)], axis=1)
    return c_q, c_k, s_lo, s_hi


def _rot(v, c, s_lo, s_hi):
    half = MLA_ROPE // 2
    return v * c + pltpu.roll(v, LANES - half, axis=1) * s_lo + pltpu.roll(v, half, axis=1) * s_hi


def _rot_t(dv, c, s_lo, s_hi):
    half = MLA_ROPE // 2
    return dv * c + pltpu.roll(dv * s_lo, half, axis=1) + pltpu.roll(dv * s_hi, LANES - half, axis=1)


def _rope_q(q_pre, c_q, s_lo, s_hi, *, transpose, out_dtype, name):
    S, W = q_pre.shape
    tm = _rows(S)
    fn = _rot_t if transpose else _rot

    def body(q_ref, c_ref, lo_ref, hi_ref, o_ref):
        c, lo, hi = c_ref[...], lo_ref[...], hi_ref[...]
        for hd in range(W // LANES):
            cols = slice(hd * LANES, (hd + 1) * LANES)
            o_ref[:, cols] = fn(q_ref[:, cols] * MLA_SCALE, c, lo, hi).astype(out_dtype)

    blk = pl.BlockSpec((tm, W), lambda i: (i, 0))
    tab = pl.BlockSpec((tm, LANES), lambda i: (i, 0))
    return pl.pallas_call(
        body, name=name, grid=(S // tm,), in_specs=[blk, tab, tab, tab], out_specs=blk,
        out_shape=jax.ShapeDtypeStruct((S, W), out_dtype), compiler_params=_cparams(("parallel",)),
    )(q_pre, c_q, s_lo, s_hi)


def _rope_k(k_pre, k_rope, c_k, s_lo, s_hi):
    S, W = k_pre.shape
    tm = _rows(S)

    def body(k_ref, r_ref, c_ref, lo_ref, hi_ref, o_ref):
        rot = _rot(r_ref[...], c_ref[...], lo_ref[...], hi_ref[...])
        for hd in range(W // LANES):
            cols = slice(hd * LANES, (hd + 1) * LANES)
            o_ref[:, cols] = (k_ref[:, cols] + rot).astype(BF16)

    blk = pl.BlockSpec((tm, W), lambda i: (i, 0))
    tab = pl.BlockSpec((tm, LANES), lambda i: (i, 0))
    return pl.pallas_call(
        body, name="rope_k", grid=(S // tm,), in_specs=[blk, tab, tab, tab, tab], out_specs=blk,
        out_shape=jax.ShapeDtypeStruct((S, W), BF16), compiler_params=_cparams(("parallel",)),
    )(k_pre, k_rope, c_k, s_lo, s_hi)


def _rope_k_bwd(dk, c_k, s_lo, s_hi):
    S, W = dk.shape
    tm = _rows(S)

    def body(dk_ref, c_ref, lo_ref, hi_ref, o_ref):
        tot = dk_ref[:, 0:LANES]
        for hd in range(1, W // LANES):
            tot = tot + dk_ref[:, hd * LANES:(hd + 1) * LANES]
        o_ref[...] = _rot_t(tot, c_ref[...], lo_ref[...], hi_ref[...]).astype(BF16)

    tab = pl.BlockSpec((tm, LANES), lambda i: (i, 0))
    return pl.pallas_call(
        body, name="rope_k_bwd", grid=(S // tm,), in_specs=[pl.BlockSpec((tm, W), lambda i: (i, 0)), tab, tab, tab],
        out_specs=tab, out_shape=jax.ShapeDtypeStruct((S, LANES), BF16), compiler_params=_cparams(("parallel",)),
    )(dk, c_k, s_lo, s_hi)


def _pairs(n, by_key):
    if by_key:
        pr = [(i, j) for j in range(n) for i in range(j, n)]
    else:
        pr = [(i, j) for i in range(n) for j in range(i + 1)]
    return (jnp.asarray(np.array([p[0] for p in pr], np.int32)), jnp.asarray(np.array([p[1] for p in pr], np.int32)), len(pr))


def _unit_mask(shape, unit, key_axis, q_off=0):
    q = lax.broadcasted_iota(jnp.int32, shape, 1 - key_axis) + q_off
    k = lax.broadcasted_iota(jnp.int32, shape, key_axis)
    if unit > 1:
        q, k = q // unit, k // unit
    return q >= k


_NT = (((1,), (1,)), ((), ()))


def _attn_fwd(q, k, v, dec_col, dec_row, *, unit, name):
    S, W = q.shape
    H = W // LANES
    T = min(ATTN_TILE, S)
    n = S // T
    qi, kj, npairs = _pairs(n, by_key=False)
    has_dec = dec_col is not None

    def body(qi_ref, kj_ref, *refs):
        if has_dec:
            q_ref, k_ref, v_ref, dc_ref, dr_ref, o_ref, lse_ref, m_s, l_s, acc = refs
        else:
            q_ref, k_ref, v_ref, o_ref, lse_ref, m_s, l_s, acc = refs
        t = pl.program_id(1)
        i, j = qi_ref[t], kj_ref[t]

        @pl.when(j == 0)
        def _():
            m_s[...] = jnp.full_like(m_s, NEG_INF)
            l_s[...] = jnp.zeros_like(l_s)
            acc[...] = jnp.zeros_like(acc)

        def step(diag):
            s = lax.dot_general(q_ref[...], k_ref[...], _NT, preferred_element_type=F32)
            if has_dec:
                s = s + (dc_ref[...] - dr_ref[...])
            if diag:
                s = jnp.where(_unit_mask((T, T), unit, 1), s, NEG_INF)
            m_prev = m_s[...]
            m_new = jnp.maximum(m_prev, jnp.max(s, axis=1, keepdims=True))
            alpha = jnp.exp(m_prev - m_new)
            p = jnp.exp(s - m_new)
            l_s[...] = alpha * l_s[...] + jnp.sum(p, axis=1, keepdims=True)
            acc[...] = alpha * acc[...] + jnp.dot(p.astype(BF16), v_ref[...], preferred_element_type=F32)
            m_s[...] = m_new

        @pl.when(j < i)
        def _():
            step(False)

        @pl.when(j == i)
        def _():
            step(True)
            o_ref[...] = (acc[...] / l_s[...]).astype(BF16)
            lse_ref[...] = m_s[...] + jnp.log(l_s[...])

    qb = pl.BlockSpec((T, LANES), lambda h, t, qi, kj: (qi[t], h))
    kb = pl.BlockSpec((T, LANES), lambda h, t, qi, kj: (kj[t], h))
    colq = pl.BlockSpec((None, T, 1), lambda h, t, qi, kj: (h, qi[t], 0))
    rowk = pl.BlockSpec((None, 1, T), lambda h, t, qi, kj: (h, 0, kj[t]))
    in_specs = [qb, kb, kb] + ([colq, rowk] if has_dec else [])
    args = (q, k, v) + ((dec_col, dec_row) if has_dec else ())
    return pl.pallas_call(
        body, name=name,
        grid_spec=pltpu.PrefetchScalarGridSpec(
            num_scalar_prefetch=2, grid=(H, npairs), in_specs=in_specs, out_specs=(qb, colq),
            scratch_shapes=[pltpu.VMEM((T, 1), F32), pltpu.VMEM((T, 1), F32), pltpu.VMEM((T, LANES), F32)]),
        out_shape=(jax.ShapeDtypeStruct((S, W), BF16), jax.ShapeDtypeStruct((H, S, 1), F32)),
        compiler_params=_cparams(("parallel", "arbitrary")),
    )(qi, kj, *args)


def _attn_delta(do, o):
    S, W = o.shape
    H = W // LANES
    tm = _rows(S)

    def body(do_ref, o_ref, d_ref):
        d_ref[...] = jnp.sum(do_ref[...].astype(F32) * o_ref[...].astype(F32), axis=1, keepdims=True)

    blk = pl.BlockSpec((tm, LANES), lambda h, i: (i, h))
    return pl.pallas_call(
        body, name="attn_delta", grid=(H, S // tm), in_specs=[blk, blk],
        out_specs=pl.BlockSpec((None, tm, 1), lambda h, i: (h, i, 0)),
        out_shape=jax.ShapeDtypeStruct((H, S, 1), F32), compiler_params=_cparams(("parallel", "parallel")),
    )(do, o)


def _attn_bwd_dq(q, k, v, do, lse, delta, dec_col, dec_row, *, unit, out_dtype, name):
    S, W = q.shape
    H = W // LANES
    T = min(ATTN_TILE, S)
    n = S // T
    qi, kj, npairs = _pairs(n, by_key=False)
    has_dec = dec_col is not None

    def body(qi_ref, kj_ref, *refs):
        if has_dec:
            q_ref, k_ref, v_ref, do_ref, lse_ref, dl_ref, dc_ref, dr_ref, dq_ref, dd_ref, acc, dacc = refs
        else:
            q_ref, k_ref, v_ref, do_ref, lse_ref, dl_ref, dq_ref, acc = refs
        t = pl.program_id(1)
        i, j = qi_ref[t], kj_ref[t]

        @pl.when(j == 0)
        def _():
            acc[...] = jnp.zeros_like(acc)
            if has_dec:
                dacc[...] = jnp.zeros_like(dacc)

        def step(diag):
            kv = k_ref[...]
            s = lax.dot_general(q_ref[...], kv, _NT, preferred_element_type=F32)
            if has_dec:
                s = s + (dc_ref[...] - dr_ref[...])
            if diag:
                s = jnp.where(_unit_mask((T, T), unit, 1), s, NEG_INF)
            p = jnp.exp(s - lse_ref[...])
            dp = lax.dot_general(do_ref[...], v_ref[...], _NT, preferred_element_type=F32)
            ds = p * (dp - dl_ref[...])
            acc[...] += jnp.dot(ds.astype(BF16), kv, preferred_element_type=F32)
            if has_dec:
                dacc[...] += jnp.sum(ds, axis=1, keepdims=True)

        @pl.when(j < i)
        def _():
            step(False)

        @pl.when(j == i)
        def _():
            step(True)
            dq_ref[...] = acc[...].astype(out_dtype)
            if has_dec:
                dd_ref[...] = dacc[...]

    qb = pl.BlockSpec((T, LANES), lambda h, t, qi, kj: (qi[t], h))
    kb = pl.BlockSpec((T, LANES), lambda h, t, qi, kj: (kj[t], h))
    colq = pl.BlockSpec((None, T, 1), lambda h, t, qi, kj: (h, qi[t], 0))
    rowk = pl.BlockSpec((None, 1, T), lambda h, t, qi, kj: (h, 0, kj[t]))
    in_specs = [qb, kb, kb, qb, colq, colq] + ([colq, rowk] if has_dec else [])
    args = (q, k, v, do, lse, delta) + ((dec_col, dec_row) if has_dec else ())
    scratch = [pltpu.VMEM((T, LANES), F32)] + ([pltpu.VMEM((T, 1), F32)] if has_dec else [])
    out_specs = (qb, colq) if has_dec else qb
    out_shape = jax.ShapeDtypeStruct((S, W), out_dtype)
    if has_dec:
        out_shape = (out_shape, jax.ShapeDtypeStruct((H, S, 1), F32))
    res = pl.pallas_call(
        body, name=name,
        grid_spec=pltpu.PrefetchScalarGridSpec(num_scalar_prefetch=2, grid=(H, npairs), in_specs=in_specs,
                                               out_specs=out_specs, scratch_shapes=scratch),
        out_shape=out_shape, compiler_params=_cparams(("parallel", "arbitrary")),
    )(qi, kj, *args)
    return res if has_dec else (res, None)


def _attn_bwd_dkv(q, k, v, do, lse_row, delta_row, dec_col, dec_row, *, unit, dk_dtype, name):
    S, W = q.shape
    H = W // LANES
    T = min(ATTN_TILE, S)
    n = S // T
    qi, kj, npairs = _pairs(n, by_key=True)
    has_dec = dec_col is not None

    def body(qi_ref, kj_ref, *refs):
        if has_dec:
            q_ref, k_ref, v_ref, do_ref, lse_ref, dl_ref, dc_ref, dr_ref, dk_ref, dv_ref, dd_ref, kacc, vacc, dacc = refs
        else:
            q_ref, k_ref, v_ref, do_ref, lse_ref, dl_ref, dk_ref, dv_ref, kacc, vacc = refs
        t = pl.program_id(1)
        i, j = qi_ref[t], kj_ref[t]

        @pl.when(i == j)
        def _():
            kacc[...] = jnp.zeros_like(kacc)
            vacc[...] = jnp.zeros_like(vacc)
            if has_dec:
                dacc[...] = jnp.zeros_like(dacc)

        def step(diag):
            qv, dov = q_ref[...], do_ref[...]
            st = lax.dot_general(k_ref[...], qv, _NT, preferred_element_type=F32)
            if has_dec:
                st = st + (dr_ref[...] - dc_ref[...])
            if diag:
                st = jnp.where(_unit_mask((T, T), unit, 0), st, NEG_INF)
            pt = jnp.exp(st - lse_ref[...])
            dpt = lax.dot_general(v_ref[...], dov, _NT, preferred_element_type=F32)
            dst = pt * (dpt - dl_ref[...])
            vacc[...] += jnp.dot(pt.astype(BF16), dov, preferred_element_type=F32)
            kacc[...] += jnp.dot(dst.astype(BF16), qv, preferred_element_type=F32)
            if has_dec:
                dacc[...] -= jnp.sum(dst, axis=1, keepdims=True)

        @pl.when(i == j)
        def _():
            step(True)

        @pl.when(i > j)
        def _():
            step(False)

        @pl.when(i == n - 1)
        def _():
            dk_ref[...] = kacc[...].astype(dk_dtype)
            dv_ref[...] = vacc[...].astype(BF16)
            if has_dec:
                dd_ref[...] = dacc[...]

    qb = pl.BlockSpec((T, LANES), lambda h, t, qi, kj: (qi[t], h))
    kb = pl.BlockSpec((T, LANES), lambda h, t, qi, kj: (kj[t], h))
    rowq = pl.BlockSpec((None, 1, T), lambda h, t, qi, kj: (h, 0, qi[t]))
    colk = pl.BlockSpec((None, T, 1), lambda h, t, qi, kj: (h, kj[t], 0))
    in_specs = [qb, kb, kb, qb, rowq, rowq] + ([colk, rowq] if has_dec else [])
    args = (q, k, v, do, lse_row, delta_row) + ((dec_col, dec_row) if has_dec else ())
    scratch = [pltpu.VMEM((T, LANES), F32)] * 2 + ([pltpu.VMEM((T, 1), F32)] if has_dec else [])
    out_specs = (kb, kb) + ((colk,) if has_dec else ())
    out_shape = (jax.ShapeDtypeStruct((S, W), dk_dtype), jax.ShapeDtypeStruct((S, W), BF16))
    if has_dec:
        out_shape = out_shape + (jax.ShapeDtypeStruct((H, S, 1), F32),)
    res = pl.pallas_call(
        body, name=name,
        grid_spec=pltpu.PrefetchScalarGridSpec(num_scalar_prefetch=2, grid=(H, npairs), in_specs=in_specs,
                                               out_specs=out_specs, scratch_shapes=scratch),
        out_shape=out_shape, compiler_params=_cparams(("parallel", "arbitrary")),
    )(qi, kj, *args)
    return res if has_dec else (res[0], res[1], None)


def _attn_bwd(q, k, v, o, lse, do, dec_col, dec_row, *, unit, dq_dtype, dk_dtype, name):
    H, S = lse.shape[0], lse.shape[1]
    delta = _attn_delta(do, o)
    dq, dd_q = _attn_bwd_dq(q, k, v, do, lse, delta, dec_col, dec_row, unit=unit, out_dtype=dq_dtype, name=name + "_dq")
    dk, dv, dd_k = _attn_bwd_dkv(q, k, v, do, lse.reshape(H, 1, S), delta.reshape(H, 1, S), dec_col, dec_row,
                                 unit=unit, dk_dtype=dk_dtype, name=name + "_dkv")
    d_dec = None if dec_col is None else dd_q + dd_k
    return dq, dk, dv, d_dec


ONES_LANE = 64
ROW_SPLIT = 1
ROW_SPLIT_BWD = 4


def _ones_lane_bias():
    one = np.zeros((HEADS, HEAD_PAD), np.float32)
    one[:, ONES_LANE] = 1.0
    return jnp.asarray(one.reshape(1, HEADS * HEAD_PAD))


def _lane_sum(t):
    tot = t[:, 0:LANES]
    for c in range(1, t.shape[1] // LANES):
        tot = tot + t[:, c * LANES:(c + 1) * LANES]
    return tot


def _fa_fwd(q, k, v, dec_row, *, unit, name):
    S, W = q.shape
    H = W // LANES
    T = min(ATTN_TILE, S)
    n, hT = S // T, T // ROW_SPLIT
    qi, kj, npairs = _pairs(n, by_key=False)
    has_dec = dec_row is not None

    def body(qi_ref, kj_ref, *refs):
        if has_dec:
            q_ref, k_ref, v_ref, dr_ref, o_ref, lse_ref, lrow_ref, m_s, acc = refs
        else:
            q_ref, k_ref, v_ref, o_ref, lse_ref, lrow_ref, m_s, acc = refs
        t = pl.program_id(1)
        i, j = qi_ref[t], kj_ref[t]

        @pl.when(j == 0)
        def _():
            m_s[...] = jnp.full_like(m_s, NEG_INF)
            acc[...] = jnp.zeros_like(acc)

        def step(diag):
            for r in range(ROW_SPLIT):
                rows = slice(r * hT, (r + 1) * hT)
                nk = (r + 1) * hT if diag else T
                s = lax.dot_general(q_ref[rows, :], k_ref[0:nk, :], _NT, preferred_element_type=F32)
                if has_dec:
                    s = s - dr_ref[:, 0:nk]
                if diag:
                    s = jnp.where(_unit_mask((hT, nk), unit, 1, r * hT), s, NEG_INF)
                m_prev = m_s[rows, :]
                m_new = jnp.maximum(m_prev, jnp.max(s, axis=1, keepdims=True))
                alpha = jnp.exp(m_prev - m_new)
                p = jnp.exp(s - jnp.tile(m_new, (1, nk // LANES)))
                acc[rows, :] = alpha * acc[rows, :] + jnp.dot(p.astype(BF16), v_ref[0:nk, :], preferred_element_type=F32)
                m_s[rows, :] = m_new

        @pl.when(j < i)
        def _():
            step(False)

        @pl.when(j == i)
        def _():
            step(True)
            av = acc[...]
            l = av[:, ONES_LANE:ONES_LANE + 1]
            lane = lax.broadcasted_iota(jnp.int32, (T, LANES), 1)
            o_ref[...] = jnp.where(lane < ONES_LANE, av / l, 0.0).astype(BF16)
            lse = m_s[...] + jnp.log(l)
            lse_ref[...] = lse
            lrow_ref[...] = lse.T[0:1, :]

    qb = pl.BlockSpec((T, LANES), lambda h, t, qi, kj: (qi[t], h))
    kb = pl.BlockSpec((T, LANES), lambda h, t, qi, kj: (kj[t], h))
    repq = pl.BlockSpec((None, T, LANES), lambda h, t, qi, kj: (h, qi[t], 0))
    rowq = pl.BlockSpec((None, 1, T), lambda h, t, qi, kj: (h, 0, qi[t]))
    rowk = pl.BlockSpec((None, 1, T), lambda h, t, qi, kj: (h, 0, kj[t]))
    in_specs = [qb, kb, kb] + ([rowk] if has_dec else [])
    args = (q, k, v) + ((dec_row,) if has_dec else ())
    return pl.pallas_call(
        body, name=name,
        grid_spec=pltpu.PrefetchScalarGridSpec(
            num_scalar_prefetch=2, grid=(H, npairs), in_specs=in_specs, out_specs=(qb, repq, rowq),
            scratch_shapes=[pltpu.VMEM((T, LANES), F32), pltpu.VMEM((T, LANES), F32)]),
        out_shape=(jax.ShapeDtypeStruct((S, W), BF16), jax.ShapeDtypeStruct((H, S, LANES), F32),
                   jax.ShapeDtypeStruct((H, 1, S), F32)),
        compiler_params=_cparams(("parallel", "arbitrary")),
    )(qi, kj, *args)


def _fa_delta(do, o):
    S, W = o.shape
    H = W // LANES
    tm = _rows(S)

    def body(do_ref, o_ref, d_ref, drow_ref):
        for hd in range(H):
            cols = slice(hd * LANES, (hd + 1) * LANES)
            d = jnp.sum(do_ref[:, cols].astype(F32) * o_ref[:, cols].astype(F32), axis=1, keepdims=True)
            rep = jnp.broadcast_to(d, (tm, LANES))
            d_ref[hd] = rep
            drow_ref[hd] = rep.T[0:1, :]

    blk = pl.BlockSpec((tm, W), lambda i: (i, 0))
    return pl.pallas_call(
        body, name="attn_delta", grid=(S // tm,), in_specs=[blk, blk],
        out_specs=(pl.BlockSpec((H, tm, LANES), lambda i: (0, i, 0)), pl.BlockSpec((H, 1, tm), lambda i: (0, 0, i))),
        out_shape=(jax.ShapeDtypeStruct((H, S, LANES), F32), jax.ShapeDtypeStruct((H, 1, S), F32)),
        compiler_params=_cparams(("parallel",)),
    )(do, o)


def _fa_bwd_dq(q, k, v, do, lse, delta, dec_row, *, unit, out_dtype, name):
    S, W = q.shape
    H = W // LANES
    T = min(ATTN_TILE, S)
    n, reps = S // T, T // LANES
    qi, kj, npairs = _pairs(n, by_key=False)
    has_dec = dec_row is not None

    def body(qi_ref, kj_ref, *refs):
        if has_dec:
            q_ref, k_ref, v_ref, do_ref, lse_ref, dl_ref, dr_ref, dq_ref, dd_ref, acc, dacc = refs
        else:
            q_ref, k_ref, v_ref, do_ref, lse_ref, dl_ref, dq_ref, acc = refs
        t = pl.program_id(1)
        i, j = qi_ref[t], kj_ref[t]

        @pl.when(j == 0)
        def _():
            acc[...] = jnp.zeros_like(acc)
            if has_dec:
                dacc[...] = jnp.zeros_like(dacc)

        def step(diag):
            kv = k_ref[...]
            s = lax.dot_general(q_ref[...], kv, _NT, preferred_element_type=F32)
            if has_dec:
                s = s - dr_ref[...]
            if diag:
                s = jnp.where(_unit_mask((T, T), unit, 1), s, NEG_INF)
            p = jnp.exp(s - jnp.tile(lse_ref[...], (1, reps)))
            dp = lax.dot_general(do_ref[...], v_ref[...], _NT, preferred_element_type=F32)
            ds = p * (dp - jnp.tile(dl_ref[...], (1, reps)))
            acc[...] += jnp.dot(ds.astype(BF16), kv, preferred_element_type=F32)
            if has_dec:
                dacc[...] += _lane_sum(ds)

        @pl.when(j < i)
        def _():
            step(False)

        @pl.when(j == i)
        def _():
            step(True)
            dq_ref[...] = acc[...].astype(out_dtype)
            if has_dec:
                dd_ref[...] = jnp.broadcast_to(jnp.sum(dacc[...], axis=1, keepdims=True), (T, LANES))

    qb = pl.BlockSpec((T, LANES), lambda h, t, qi, kj: (qi[t], h))
    kb = pl.BlockSpec((T, LANES), lambda h, t, qi, kj: (kj[t], h))
    repq = pl.BlockSpec((None, T, LANES), lambda h, t, qi, kj: (h, qi[t], 0))
    rowk = pl.BlockSpec((None, 1, T), lambda h, t, qi, kj: (h, 0, kj[t]))
    in_specs = [qb, kb, kb, qb, repq, repq] + ([rowk] if has_dec else [])
    args = (q, k, v, do, lse, delta) + ((dec_row,) if has_dec else ())
    out_shape = jax.ShapeDtypeStruct((S, W), out_dtype)
    res = pl.pallas_call(
        body, name=name,
        grid_spec=pltpu.PrefetchScalarGridSpec(num_scalar_prefetch=2, grid=(H, npairs), in_specs=in_specs,
                                               out_specs=(qb, repq) if has_dec else qb,
                                               scratch_shapes=[pltpu.VMEM((T, LANES), F32)] * (2 if has_dec else 1)),
        out_shape=(out_shape, jax.ShapeDtypeStruct((H, S, LANES), F32)) if has_dec else out_shape,
        compiler_params=_cparams(("parallel", "arbitrary")),
    )(qi, kj, *args)
    return res if has_dec else (res, None)


def _fa_bwd_dkv(q, k, v, do, lse_row, delta_row, dec_rep, *, unit, dk_dtype, name):
    S, W = q.shape
    H = W // LANES
    T = min(ATTN_TILE, S)
    n, reps = S // T, T // LANES
    qi, kj, npairs = _pairs(n, by_key=True)
    has_dec = dec_rep is not None

    def body(qi_ref, kj_ref, *refs):
        if has_dec:
            q_ref, k_ref, v_ref, do_ref, lse_ref, dl_ref, dc_ref, dk_ref, dv_ref, dd_ref, kacc, vacc, dacc = refs
        else:
            q_ref, k_ref, v_ref, do_ref, lse_ref, dl_ref, dk_ref, dv_ref, kacc, vacc = refs
        t = pl.program_id(1)
        i, j = qi_ref[t], kj_ref[t]

        @pl.when(i == j)
        def _():
            kacc[...] = jnp.zeros_like(kacc)
            vacc[...] = jnp.zeros_like(vacc)
            if has_dec:
                dacc[...] = jnp.zeros_like(dacc)

        def step(diag):
            qv, dov = q_ref[...], do_ref[...]
            st = lax.dot_general(k_ref[...], qv, _NT, preferred_element_type=F32)
            if has_dec:
                st = st - jnp.tile(dc_ref[...], (1, reps))
            if diag:
                st = jnp.where(_unit_mask((T, T), unit, 0), st, NEG_INF)
            pt = jnp.exp(st - lse_ref[...])
            dpt = lax.dot_general(v_ref[...], dov, _NT, preferred_element_type=F32)
            dst = pt * (dpt - dl_ref[...])
            vacc[...] += jnp.dot(pt.astype(BF16), dov, preferred_element_type=F32)
            kacc[...] += jnp.dot(dst.astype(BF16), qv, preferred_element_type=F32)
            if has_dec:
                dacc[...] += _lane_sum(dst)

        @pl.when(i == j)
        def _():
            step(True)

        @pl.when(i > j)
        def _():
            step(False)

        @pl.when(i == n - 1)
        def _():
            dk_ref[...] = kacc[...].astype(dk_dtype)
            dv_ref[...] = vacc[...].astype(BF16)
            if has_dec:
                dd_ref[...] = jnp.broadcast_to(-jnp.sum(dacc[...], axis=1, keepdims=True), (T, LANES))

    qb = pl.BlockSpec((T, LANES), lambda h, t, qi, kj: (qi[t], h))
    kb = pl.BlockSpec((T, LANES), lambda h, t, qi, kj: (kj[t], h))
    rowq = pl.BlockSpec((None, 1, T), lambda h, t, qi, kj: (h, 0, qi[t]))
    repk = pl.BlockSpec((None, T, LANES), lambda h, t, qi, kj: (h, kj[t], 0))
    in_specs = [qb, kb, kb, qb, rowq, rowq] + ([repk] if has_dec else [])
    args = (q, k, v, do, lse_row, delta_row) + ((dec_rep,) if has_dec else ())
    scratch = [pltpu.VMEM((T, LANES), F32)] * (3 if has_dec else 2)
    out_specs = (kb, kb) + ((repk,) if has_dec else ())
    out_shape = (jax.ShapeDtypeStruct((S, W), dk_dtype), jax.ShapeDtypeStruct((S, W), BF16))
    if has_dec:
        out_shape = out_shape + (jax.ShapeDtypeStruct((H, S, LANES), F32),)
    res = pl.pallas_call(
        body, name=name,
        grid_spec=pltpu.PrefetchScalarGridSpec(num_scalar_prefetch=2, grid=(H, npairs), in_specs=in_specs,
                                               out_specs=out_specs, scratch_shapes=scratch),
        out_shape=out_shape, compiler_params=_cparams(("parallel", "arbitrary")),
    )(qi, kj, *args)
    return res if has_dec else (res[0], res[1], None)


_TN = (((0,), (0,)), ((), ()))


def _fa_bwd_fused(q, k, v, do, o, lse, dec_row, *, unit, dq_dtype, dk_dtype, name):
    S, W = q.shape
    H = W // LANES
    T = min(ATTN_TILE, S)
    n, hT = S // T, T // ROW_SPLIT_BWD
    qi, kj, npairs = _pairs(n, by_key=False)
    has_dec = dec_row is not None

    def body(qi_ref, kj_ref, *refs):
        if has_dec:
            (q_ref, k_ref, v_ref, do_ref, o_ref, lse_ref, dr_ref, dq_ref, dk_ref, dv_ref, ddq_ref, ddk_ref,
             qacc, kacc, vacc, dl_ref, rsum, csum) = refs
        else:
            q_ref, k_ref, v_ref, do_ref, o_ref, lse_ref, dq_ref, dk_ref, dv_ref, qacc, kacc, vacc, dl_ref = refs
        t = pl.program_id(1)
        i, j = qi_ref[t], kj_ref[t]

        @pl.when(t == 0)
        def _():
            kacc[...] = jnp.zeros_like(kacc)
            vacc[...] = jnp.zeros_like(vacc)
            if has_dec:
                csum[...] = jnp.zeros_like(csum)

        @pl.when(j == 0)
        def _():
            qacc[...] = jnp.zeros_like(qacc)
            delta = jnp.sum(do_ref[...].astype(F32) * o_ref[...].astype(F32), axis=1, keepdims=True)
            dl_ref[...] = jnp.broadcast_to(delta, (T, LANES))
            if has_dec:
                rsum[...] = jnp.zeros_like(rsum)

        def step(diag):
            for r in range(ROW_SPLIT_BWD):
                rows = slice(r * hT, (r + 1) * hT)
                nk = (r + 1) * hT if diag else T
                qv, dov, kv = q_ref[rows, :], do_ref[rows, :], k_ref[0:nk, :]
                s = lax.dot_general(qv, kv, _NT, preferred_element_type=F32)
                if has_dec:
                    s = s - dr_ref[:, 0:nk]
                if diag:
                    s = jnp.where(_unit_mask((hT, nk), unit, 1, r * hT), s, NEG_INF)
                p = jnp.exp(s - jnp.tile(lse_ref[rows, :], (1, nk // LANES)))
                dp = lax.dot_general(dov, v_ref[0:nk, :], _NT, preferred_element_type=F32)
                ds = p * (dp - jnp.tile(dl_ref[rows, :], (1, nk // LANES)))
                pb, dsb = p.astype(BF16), ds.astype(BF16)
                qacc[rows, :] += jnp.dot(dsb, kv, preferred_element_type=F32)
                vacc[j, 0:nk, :] += lax.dot_general(pb, dov, _TN, preferred_element_type=F32)
                kacc[j, 0:nk, :] += lax.dot_general(dsb, qv, _TN, preferred_element_type=F32)
                if has_dec:
                    rsum[rows, :] += _lane_sum(ds)
                    csum[j, :, 0:nk] -= jnp.sum(ds, axis=0, keepdims=True)

        @pl.when(j < i)
        def _():
            step(False)

        @pl.when(j == i)
        def _():
            step(True)
            dq_ref[...] = qacc[...].astype(dq_dtype)
            if has_dec:
                ddq_ref[...] = jnp.broadcast_to(jnp.sum(rsum[...], axis=1, keepdims=True), (T, LANES))

        @pl.when(t == npairs - 1)
        def _():
            for jj in range(n):
                dk_ref[jj * T:(jj + 1) * T, :] = kacc[jj].astype(dk_dtype)
                dv_ref[jj * T:(jj + 1) * T, :] = vacc[jj].astype(BF16)
                if has_dec:
                    ddk_ref[:, jj * T:(jj + 1) * T] = csum[jj]

    qb = pl.BlockSpec((T, LANES), lambda h, t, qi, kj: (qi[t], h))
    kb = pl.BlockSpec((T, LANES), lambda h, t, qi, kj: (kj[t], h))
    head = pl.BlockSpec((S, LANES), lambda h, t, qi, kj: (0, h))
    repq = pl.BlockSpec((None, T, LANES), lambda h, t, qi, kj: (h, qi[t], 0))
    rowk = pl.BlockSpec((None, 1, T), lambda h, t, qi, kj: (h, 0, kj[t]))
    rowh = pl.BlockSpec((None, 1, S), lambda h, t, qi, kj: (h, 0, 0))
    in_specs = [qb, kb, kb, qb, qb, repq] + ([rowk] if has_dec else [])
    args = (q, k, v, do, o, lse) + ((dec_row,) if has_dec else ())
    out_specs = [qb, head, head] + ([repq, rowh] if has_dec else [])
    out_shape = [jax.ShapeDtypeStruct((S, W), dq_dtype), jax.ShapeDtypeStruct((S, W), dk_dtype), jax.ShapeDtypeStruct((S, W), BF16)]
    scratch = [pltpu.VMEM((T, LANES), F32), pltpu.VMEM((n, T, LANES), F32), pltpu.VMEM((n, T, LANES), F32),
               pltpu.VMEM((T, LANES), F32)]
    if has_dec:
        out_shape += [jax.ShapeDtypeStruct((H, S, LANES), F32), jax.ShapeDtypeStruct((H, 1, S), F32)]
        scratch += [pltpu.VMEM((T, LANES), F32), pltpu.VMEM((n, 1, T), F32)]
    res = pl.pallas_call(
        body, name=name,
        grid_spec=pltpu.PrefetchScalarGridSpec(num_scalar_prefetch=2, grid=(H, npairs), in_specs=in_specs,
                                               out_specs=tuple(out_specs), scratch_shapes=scratch),
        out_shape=tuple(out_shape),
        compiler_params=pltpu.CompilerParams(dimension_semantics=("parallel", "arbitrary"), vmem_limit_bytes=FUSED_BWD_VMEM_BYTES),
    )(qi, kj, *args)
    return res if has_dec else (res[0], res[1], res[2], None, None)


FUSED_BWD_VMEM_BYTES = 58 * 1024 * 1024


def _fa_bwd(q, k, v, o, lse, lse_row, do, dec_row, dec_rep, *, unit, dq_dtype, dk_dtype, name):
    dq, dk, dv, dd_q, dd_k = _fa_bwd_fused(q, k, v, do, o, lse, dec_row, unit=unit, dq_dtype=dq_dtype, dk_dtype=dk_dtype, name=name)
    if dd_k is None:
        return dq, dk, dv, None
    return dq, dk, dv, jnp.max(dd_q, axis=2) + dd_k.reshape(dd_k.shape[0], dd_k.shape[2])


def _fa_bwd_split(q, k, v, o, lse, lse_row, do, dec_row, dec_rep, *, unit, dq_dtype, dk_dtype, name):
    delta, delta_row = _fa_delta(do, o)
    dq, dd_q = _fa_bwd_dq(q, k, v, do, lse, delta, dec_row, unit=unit, out_dtype=dq_dtype, name=name + "_dq")
    one = lambda t: jnp.max(t, axis=2)
    dk, dv, dd_k = _fa_bwd_dkv(q, k, v, do, lse_row, delta_row, dec_rep, unit=unit, dk_dtype=dk_dtype, name=name + "_dkv")
    return dq, dk, dv, (None if dd_k is None else one(dd_q) + one(dd_k))


def _merge_fwd(ya, yb, yc, gate_logit, gate_b):
    S, D = ya.shape
    tm = min(256, S)

    def body(a_ref, b_ref, c_ref, gl_ref, gb_ref, o_ref):
        g = jax.nn.sigmoid(gl_ref[...] + gb_ref[...])
        o_ref[...] = (g[:, 0:D] * a_ref[...] + g[:, D:2 * D] * b_ref[...] + g[:, 2 * D:3 * D] * c_ref[...]).astype(BF16)

    row = pl.BlockSpec((tm, D), lambda i: (i, 0))
    return pl.pallas_call(
        body, name="merge_fwd", grid=(S // tm,),
        in_specs=[row, row, row, pl.BlockSpec((tm, 3 * D), lambda i: (i, 0)), pl.BlockSpec((1, 3 * D), lambda i: (0, 0))],
        out_specs=row, out_shape=jax.ShapeDtypeStruct((S, D), BF16), compiler_params=_cparams(("parallel",)),
    )(ya, yb, yc, gate_logit, gate_b.reshape(1, 3 * D))


def _merge_bwd(dm, ya, yb, yc, gate_logit, gate_b):
    S, D = ya.shape
    tm = min(256, S)

    def body(dm_ref, a_ref, b_ref, c_ref, gl_ref, gb_ref, da_ref, db_ref, dc_ref, dgl_ref, dgb_ref):
        g = jax.nn.sigmoid(gl_ref[...] + gb_ref[...])
        dmv = dm_ref[...]
        parts = []
        for n, (y_ref, dy_ref) in enumerate(((a_ref, da_ref), (b_ref, db_ref), (c_ref, dc_ref))):
            gn = g[:, n * D:(n + 1) * D]
            dy_ref[...] = (dmv * gn).astype(BF16)
            parts.append(dmv * y_ref[...] * gn * (1.0 - gn))
        dgl = jnp.concatenate(parts, axis=1)
        dgl_ref[...] = dgl.astype(BF16)

        @pl.when(pl.program_id(0) == 0)
        def _():
            dgb_ref[...] = jnp.zeros_like(dgb_ref)

        dgb_ref[...] += jnp.sum(dgl, axis=0, keepdims=True)

    row = pl.BlockSpec((tm, D), lambda i: (i, 0))
    wide = pl.BlockSpec((tm, 3 * D), lambda i: (i, 0))
    vec = pl.BlockSpec((1, 3 * D), lambda i: (0, 0))
    act = jax.ShapeDtypeStruct((S, D), BF16)
    da, db, dc, dgl, dgb = pl.pallas_call(
        body, name="merge_bwd", grid=(S // tm,), in_specs=[row, row, row, row, wide, vec],
        out_specs=(row, row, row, wide, vec),
        out_shape=(act, act, act, jax.ShapeDtypeStruct((S, 3 * D), BF16), jax.ShapeDtypeStruct((1, 3 * D), F32)),
        compiler_params=_cparams(("arbitrary",)),
    )(dm, ya, yb, yc, gate_logit, gate_b.reshape(1, 3 * D))
    return da, db, dc, dgl, dgb.reshape(3 * D)


def _swiglu_fwd(hf):
    S, W2 = hf.shape
    F = W2 // 2
    tm = min(128, S)

    def body(h_ref, o_ref):
        gt, up = h_ref[:, 0:F], h_ref[:, F:W2]
        o_ref[...] = (gt * jax.nn.sigmoid(gt) * up).astype(BF16)

    return pl.pallas_call(
        body, name="swiglu_fwd", grid=(S // tm,), in_specs=[pl.BlockSpec((tm, W2), lambda i: (i, 0))],
        out_specs=pl.BlockSpec((tm, F), lambda i: (i, 0)), out_shape=jax.ShapeDtypeStruct((S, F), BF16),
        compiler_params=_cparams(("parallel",)),
    )(hf)


def _swiglu_bwd(dact, hf):
    S, W2 = hf.shape
    F = W2 // 2
    tm = min(128, S)

    def body(d_ref, h_ref, o_ref):
        gt, up = h_ref[:, 0:F], h_ref[:, F:W2]
        sg = jax.nn.sigmoid(gt)
        dv = d_ref[...]
        o_ref[:, 0:F] = (dv * up * sg * (1.0 + gt * (1.0 - sg))).astype(BF16)
        o_ref[:, F:W2] = (dv * gt * sg).astype(BF16)

    return pl.pallas_call(
        body, name="swiglu_bwd", grid=(S // tm,),
        in_specs=[pl.BlockSpec((tm, F), lambda i: (i, 0)), pl.BlockSpec((tm, W2), lambda i: (i, 0))],
        out_specs=pl.BlockSpec((tm, W2), lambda i: (i, 0)), out_shape=jax.ShapeDtypeStruct((S, W2), BF16),
        compiler_params=_cparams(("parallel",)),
    )(dact, hf)


def _ple_fwd(x, pre, e, g_next):
    S, D = x.shape
    tm = _rows(S)
    with_norm = g_next is not None

    def body(*refs):
        x_ref, p_ref, e_ref = refs[:3]
        xn = x_ref[...] + jax.nn.sigmoid(p_ref[...]) * e_ref[...]
        if with_norm:
            g_ref, o_ref, h_ref = refs[3:]
            rstd = lax.rsqrt(jnp.mean(xn * xn, axis=1, keepdims=True) + EPS)
            h_ref[...] = (xn * rstd * g_ref[...]).astype(BF16)
        else:
            o_ref = refs[3]
        o_ref[...] = xn

    row = pl.BlockSpec((tm, D), lambda i: (i, 0))
    xs = jax.ShapeDtypeStruct((S, D), F32)
    if not with_norm:
        return pl.pallas_call(body, name="ple_fwd_last", grid=(S // tm,), in_specs=[row, row, row], out_specs=row,
                              out_shape=xs, compiler_params=_cparams(("parallel",)))(x, pre, e), None
    return pl.pallas_call(body, name="ple_fwd", grid=(S // tm,), in_specs=[row, row, row, pl.BlockSpec((1, D), lambda i: (0, 0))],
                          out_specs=(row, row), out_shape=(xs, jax.ShapeDtypeStruct((S, D), BF16)),
                          compiler_params=_cparams(("parallel",)))(x, pre, e, g_next.reshape(1, D))


def _ple_bwd(dx, pre, e):
    S, D = dx.shape
    tm = _rows(S)

    def body(dx_ref, p_ref, e_ref, dp_ref, de_ref):
        pg = jax.nn.sigmoid(p_ref[...])
        dxv = dx_ref[...]
        dp_ref[...] = (dxv * e_ref[...] * pg * (1.0 - pg)).astype(BF16)
        de_ref[...] = (dxv * pg).astype(BF16)

    row = pl.BlockSpec((tm, D), lambda i: (i, 0))
    act = jax.ShapeDtypeStruct((S, D), BF16)
    return pl.pallas_call(body, name="ple_bwd", grid=(S // tm,), in_specs=[row, row, row], out_specs=(row, row),
                          out_shape=(act, act), compiler_params=_cparams(("parallel",)))(dx, pre, e)


def _pad_heads(w, real):
    K = w.shape[0]
    w = w.reshape(K, HEADS, real)
    return jnp.pad(w, ((0, 0), (0, 0), (0, HEAD_PAD - real))).reshape(K, HEADS * HEAD_PAD)


def _unpad_heads(w, real):
    K = w.shape[0]
    return w.reshape(K, HEADS, HEAD_PAD)[:, :, :real].reshape(K, HEADS * real)


def _pad_head_rows(w, real):
    N = w.shape[1]
    w = w.reshape(HEADS, real, N)
    return jnp.pad(w, ((0, 0), (0, HEAD_PAD - real), (0, 0))).reshape(HEADS * HEAD_PAD, N)


def _unpad_head_rows(w, real):
    N = w.shape[1]
    return w.reshape(HEADS, HEAD_PAD, N)[:, :real].reshape(HEADS * real, N)


def _block_diag(w):
    w = w.reshape(4, 2, 64, 64)
    z = jnp.zeros((4, 64, 64), w.dtype)
    top = jnp.concatenate([w[:, 0], z], axis=2)
    bot = jnp.concatenate([z, w[:, 1]], axis=2)
    return jnp.concatenate([top, bot], axis=1)


def _block_diag_t(w):
    return jnp.stack([w[:, :64, :64], w[:, 64:, 64:]], axis=1).reshape(8, 64, 64)


_IN_SPLITS = (512, 512, 384, 288, 512, 512, 512, 8, 3072)
_IN_OFF = np.concatenate([[0], np.cumsum(_IN_SPLITS)])
_KR_OFF = 64
_SEG_NAMES = ("u", "ug", "cq", "ckv", "kr", "fq", "fk", "fv", "fl", "gate")


def _in_segments(w_in):
    c = lambda n: w_in[:, int(_IN_OFF[n]):int(_IN_OFF[n + 1])]
    kv = c(3)
    kr = jnp.pad(kv[:, MLA_KV_LORA:], ((0, 0), (_KR_OFF, LANES - _KR_OFF - MLA_ROPE)))
    fl = jnp.pad(c(7), ((0, 0), (0, LANES - HEADS)))
    fq = _pad_heads(c(4), FOX_HEAD_DIM) * jnp.asarray(FOX_SCALE, w_in.dtype)
    return [c(0), c(1), c(2), kv[:, :MLA_KV_LORA], kr, fq, _pad_heads(c(5), FOX_HEAD_DIM), _pad_heads(c(6), FOX_HEAD_DIM), fl, c(8)]


def _in_unsegment(dw_p, widths):
    offs = np.concatenate([[0], np.cumsum(widths)])
    seg = [dw_p[:, int(offs[n]):int(offs[n + 1])] for n in range(len(widths))]
    u, ug, cq, ckv, kr, fq, fk, fv, fl, gate = seg
    return jnp.concatenate([
        u, ug, cq, ckv, kr[:, _KR_OFF:_KR_OFF + MLA_ROPE], _unpad_heads(fq, FOX_HEAD_DIM) * FOX_SCALE,
        _unpad_heads(fk, FOX_HEAD_DIM), _unpad_heads(fv, FOX_HEAD_DIM), fl[:, :HEADS], gate], axis=1)


def _split_wuq(wuq):
    return _pad_heads(wuq, MLA_NOPE + MLA_ROPE)


def _split_wukv(wukv):
    w = wukv.reshape(MLA_KV_LORA, HEADS, MLA_NOPE + MLA_V)
    pad = lambda t: jnp.pad(t, ((0, 0), (0, 0), (0, HEAD_PAD - t.shape[2]))).reshape(MLA_KV_LORA, HEADS * HEAD_PAD)
    return pad(w[:, :, :MLA_NOPE]), pad(w[:, :, MLA_NOPE:])


def _merge_wukv(dk_p, dv_p):
    k = dk_p.reshape(MLA_KV_LORA, HEADS, HEAD_PAD)[:, :, :MLA_NOPE]
    v = dv_p.reshape(MLA_KV_LORA, HEADS, HEAD_PAD)[:, :, :MLA_V]
    return jnp.concatenate([k, v], axis=2).reshape(MLA_KV_LORA, HEADS * (MLA_NOPE + MLA_V))


def _heads_layout(d):
    S = d.shape[0]
    t = d[:, :HEADS].T
    return t.reshape(HEADS, 1, S), jnp.broadcast_to(t[:, :, None], (HEADS, S, LANES))


def _layer_fwd(x, h, p_i, w, g_next, tabs):
    c_q, c_k, s_lo, s_hi = tabs
    sv = {"x0": x}
    segs = _in_segments(w["w_in"])
    z = {}
    for nm, ws in zip(_SEG_NAMES, segs):
        z[nm] = _mm(h, ws, out_dtype=BF16 if nm in ("fq", "fk", "fv", "gate") else F32, bias=_ones_lane_bias() if nm == "fv" else None,
                    name="in_" + nm)
    sv.update(h=h, z=z)
    wa_bd, wx_bd = _block_diag(w["lru_wa"]).astype(BF16), _block_diag(w["lru_wx"]).astype(BF16)
    oa, xc, hs = _lru_fwd(z["u"], z["ug"], w["conv_w"], w["conv_b"], wa_bd, wx_bd, w["lru_ba"], w["lru_bx"], w["lru_lambda"])
    sv.update(oa=oa, xc=xc, hs=hs)
    qn = _rmsnorm_fwd(z["cq"], w["mla_q_norm"], "q_norm_fwd")
    kvn = _rmsnorm_fwd(z["ckv"], w["mla_kv_norm"], "kv_norm_fwd")
    wuq_p = _split_wuq(w["mla_wuq"])
    wk_p, wv_p = _split_wukv(w["mla_wukv"])
    qb = _rope_q(_mm(qn, wuq_p, name="mla_q"), c_q, s_lo, s_hi, transpose=False, out_dtype=BF16, name="rope_q")
    kb = _rope_k(_mm(kvn, wk_p, name="mla_k"), z["kr"], c_k, s_lo, s_hi)
    vb = _mm(kvn, wv_p, out_dtype=BF16, bias=_ones_lane_bias(), name="mla_v")
    ob, lse_b, lrow_b = _fa_fwd(qb, kb, vb, None, unit=64, name="mla_attn")
    sv.update(qn=qn, kvn=kvn, qb=qb, kb=kb, vb=vb, ob=ob, lse_b=lse_b, lrow_b=lrow_b)
    bf = jnp.pad(w["fox_bf"], (0, LANES - HEADS)).reshape(1, LANES)
    dec = _decay_fwd(z["fl"], bf)
    drow, drep = _heads_layout(dec)
    oc, lse_c, lrow_c = _fa_fwd(z["fq"], z["fk"], z["fv"], drow, unit=1, name="fox_attn")
    sv.update(drow=drow, drep=drep, oc=oc, lse_c=lse_c, lrow_c=lrow_c)
    ya = _mm(oa, w["w_br_a"], out_dtype=BF16, name="br_a")
    yb = _mm(ob, _pad_head_rows(w["w_br_b"], MLA_V), out_dtype=BF16, name="br_b")
    yc = _mm(oc, _pad_head_rows(w["w_br_c"], FOX_HEAD_DIM), out_dtype=BF16, name="br_c")
    merged = _merge_fwd(ya, yb, yc, z["gate"], w["gate_b"])
    x1, hn = _mm_res_norm(merged, w["w_o"], x, w["ffn_norm"], "w_o")
    sv.update(ya=ya, yb=yb, yc=yc, merged=merged, x1=x1)
    hf, act = _ffn_up(hn, _ffn_pair_columns(w["w_gate_up"]))
    x2, pn = _mm_res_norm(act, w["w_down"], x1, w["ple_norm"], "ffn_down")
    sv.update(hn=hn, hf=hf, act=act, x2=x2)
    pre = _mm(pn, w["w_ple_gate"], name="ple_gate")
    e = _mm(p_i, w["w_ple"], name="ple_embed")
    x3, h_next = _ple_fwd(x2, pre, e, g_next)
    sv.update(pn=pn, pre=pre, e=e, p_i=p_i)
    return x3, h_next, sv


def _layer_bwd(dx3, w, sv, tabs):
    c_q, c_k, s_lo, s_hi = tabs
    g = {}
    z = sv["z"]
    dpre, de = _ple_bwd(dx3, sv["pre"], sv["e"])
    g["w_ple"] = _mm(sv["p_i"], de, ta=True, name="d_w_ple")
    g["w_ple_gate"] = _mm(sv["pn"], dpre, ta=True, name="d_w_ple_gate")
    dpn = _mm(dpre, w["w_ple_gate"], tb=True, name="d_pn")
    dx2, g["ple_norm"] = _rmsnorm_bwd(sv["x2"], w["ple_norm"], dpn, add=dx3, name="ple_norm_bwd")
    g["w_down"] = _mm(sv["act"], dx2, ta=True, name="d_w_down")
    dhf = _ffn_down_bwd(dx2, w["w_down"], sv["hf"])
    g["w_gate_up"] = _ffn_unpair_columns(_mm(sv["hn"], dhf, ta=True, name="d_w_gate_up"))
    dhn = _mm(dhf, _ffn_pair_columns(w["w_gate_up"]), tb=True, name="d_hn")
    dx1, g["ffn_norm"] = _rmsnorm_bwd(sv["x1"], w["ffn_norm"], dhn, add=dx2, name="ffn_norm_bwd")
    g["w_o"] = _mm(sv["merged"], dx1, ta=True, name="d_w_o")
    dm = _mm(dx1, w["w_o"], tb=True, name="d_merged")
    dya, dyb, dyc, dgate, g["gate_b"] = _merge_bwd(dm, sv["ya"], sv["yb"], sv["yc"], z["gate"], w["gate_b"])
    wbb_p, wbc_p = _pad_head_rows(w["w_br_b"], MLA_V), _pad_head_rows(w["w_br_c"], FOX_HEAD_DIM)
    g["w_br_a"] = _mm(sv["oa"], dya, ta=True, name="d_w_br_a")
    g["w_br_b"] = _unpad_head_rows(_mm(sv["ob"], dyb, ta=True, name="d_w_br_b"), MLA_V)
    g["w_br_c"] = _unpad_head_rows(_mm(sv["oc"], dyc, ta=True, name="d_w_br_c"), FOX_HEAD_DIM)
    doa = _mm(dya, w["w_br_a"], tb=True, name="d_oa")
    dob = _mm(dyb, wbb_p, tb=True, out_dtype=BF16, name="d_ob")
    doc = _mm(dyc, wbc_p, tb=True, out_dtype=BF16, name="d_oc")
    dfq, dfk, dfv, d_dec = _fa_bwd(z["fq"], z["fk"], z["fv"], sv["oc"], sv["lse_c"], sv["lrow_c"], doc, sv["drow"], sv["drep"],
                                   unit=1, dq_dtype=BF16, dk_dtype=BF16, name="fox_attn_bwd")
    d_dec = jnp.pad(d_dec.T, ((0, 0), (0, LANES - HEADS)))
    bf = jnp.pad(w["fox_bf"], (0, LANES - HEADS)).reshape(1, LANES)
    dfl, dbf = _decay_bwd(d_dec, z["fl"], bf)
    g["fox_bf"] = dbf[0, :HEADS]
    dqb, dkb, dvb, _ = _fa_bwd(sv["qb"], sv["kb"], sv["vb"], sv["ob"], sv["lse_b"], sv["lrow_b"], dob, None, None,
                               unit=64, dq_dtype=F32, dk_dtype=F32, name="mla_attn_bwd")
    wuq_p = _split_wuq(w["mla_wuq"])
    wk_p, wv_p = _split_wukv(w["mla_wukv"])
    dq_pre = _rope_q(dqb, c_q, s_lo, s_hi, transpose=True, out_dtype=BF16, name="rope_q_bwd")
    dkr = _rope_k_bwd(dkb, c_k, s_lo, s_hi)
    g["mla_wuq"] = _unpad_heads(_mm(sv["qn"], dq_pre, ta=True, name="d_wuq"), MLA_NOPE + MLA_ROPE)
    g["mla_wukv"] = _merge_wukv(_mm(sv["kvn"], dkb, ta=True, name="d_wuk"), _mm(sv["kvn"], dvb, ta=True, name="d_wuv"))
    dqn = _mm(dq_pre, wuq_p, tb=True, name="d_qn")
    dkvn = _mm(dvb, wv_p, tb=True, res=_mm(dkb, wk_p, tb=True, name="d_kvn_k"), name="d_kvn")
    dcq, g["mla_q_norm"] = _rmsnorm_bwd(z["cq"], w["mla_q_norm"], dqn, out_dtype=BF16, name="q_norm_bwd")
    dckv, g["mla_kv_norm"] = _rmsnorm_bwd(z["ckv"], w["mla_kv_norm"], dkvn, out_dtype=BF16, name="kv_norm_bwd")
    wa_bd, wx_bd = _block_diag(w["lru_wa"]).astype(BF16), _block_diag(w["lru_wx"]).astype(BF16)
    du, dug, dcw, dcb, dba, dbx, dlam, dwa, dwx = _lru_bwd(
        doa, z["u"], z["ug"], sv["xc"], sv["hs"], w["conv_w"], wa_bd, wx_bd, w["lru_ba"], w["lru_bx"], w["lru_lambda"])
    g["conv_w"], g["conv_b"], g["lru_ba"], g["lru_bx"] = dcw, dcb[0], dba[0], dbx[0]
    g["lru_lambda"] = dlam[0] * LRU_C * jax.nn.sigmoid(-w["lru_lambda"])
    g["lru_wa"], g["lru_wx"] = _block_diag_t(dwa), _block_diag_t(dwx)
    dsegs = [du, dug, dcq, dckv, dkr, dfq, dfk, dfv, dfl, dgate]
    dz = jnp.concatenate(dsegs, axis=1)
    w_in_p = jnp.concatenate(_in_segments(w["w_in"]), axis=1)
    g["w_in"] = _in_unsegment(_mm(sv["h"], dz, ta=True, name="d_w_in"), [d.shape[1] for d in dsegs])
    dh = _mm(dz, w_in_p, tb=True, name="d_h")
    dx0, g["mix_norm"] = _rmsnorm_bwd(sv["x0"], w["mix_norm"], dh, add=dx1, name="mix_norm_bwd")
    return dx0, g


_LAYER_WEIGHTS = ("mix_norm", "w_in", "gate_b", "conv_w", "conv_b", "lru_wa", "lru_ba", "lru_wx", "lru_bx", "lru_lambda",
                  "mla_q_norm", "mla_wuq", "mla_kv_norm", "mla_wukv", "fox_bf", "w_br_a", "w_br_b", "w_br_c", "w_o",
                  "ffn_norm", "w_gate_up", "w_down", "ple_norm", "w_ple_gate", "w_ple")
_BIG = ("w_in", "mla_wuq", "mla_wukv", "w_br_a", "w_br_b", "w_br_c", "w_o", "w_gate_up", "w_down", "w_ple_gate", "w_ple")
_ROW_SHARDED = ("w_o", "w_down", "w_ple_gate")
_SMALL = ("mix_norm", "gate_b", "conv_b", "lru_wa", "lru_ba", "lru_wx", "lru_bx", "lru_lambda", "mla_q_norm", "mla_kv_norm",
          "fox_bf", "ffn_norm", "ple_norm")


def _local_step(x, p, layers, final_norm, target):
    tabs = _rope_tables(x.shape[0])
    saved = []
    h = _rmsnorm_fwd(x, layers[0]["mix_norm"], "mix_norm_fwd")
    for i in range(DEPTH):
        g_next = layers[i + 1]["mix_norm"] if i + 1 < DEPTH else None
        x, h, sv = _layer_fwd(x, h, p[i], layers[i], g_next, tabs)
        saved.append(sv)
    loss, dx, d_final = _loss_head(x, final_norm, target)
    grads = [None] * DEPTH
    for i in reversed(range(DEPTH)):
        dx, grads[i] = _layer_bwd(dx, layers[i], saved[i], tabs)
    return loss, dx, grads, d_final


def _hbm():
    return pl.BlockSpec(memory_space=pltpu.HBM)


def _peers(x, y):
    return [(1 - x, y), (x, 1 - y), (1 - x, 1 - y)]


def _gather_chips_two_level(shard, name):
    R, W = shard.shape
    Rh = R // 2

    def body(src_ref, out_ref, send_sems, recv_sems):
        x, y, c = lax.axis_index("x"), lax.axis_index("y"), lax.axis_index("c")
        me = 2 * x + y
        mine, other = pl.ds(c * Rh, Rh), pl.ds((1 - c) * Rh, Rh)
        peers = _peers(x, y)

        def copy(j, src, slot, rows, to):
            return pltpu.make_async_remote_copy(src_ref=src, dst_ref=out_ref.at[slot, rows], send_sem=send_sems.at[j],
                                                recv_sem=recv_sems.at[j], device_id=to, device_id_type=MESH)

        first = [copy(j, src_ref.at[mine], me, mine, (px, py, c)) for j, (px, py) in enumerate(peers)]
        for cp in first:
            cp.start()
        passed = []
        for j, (px, py) in enumerate(peers):
            slot = 2 * px + py
            copy(j, src_ref.at[mine], slot, mine, (px, py, c)).wait_recv()
            cp = copy(3 + j, out_ref.at[slot, mine], slot, mine, (x, y, 1 - c))
            cp.start()
            passed.append(cp)
        for j, (px, py) in enumerate(peers):
            copy(3 + j, src_ref.at[other], 2 * px + py, other, (x, y, 1 - c)).wait_recv()
        for cp in first + passed:
            cp.wait_send()

    return pl.pallas_call(
        body, name=name, in_specs=[_hbm()], out_specs=_hbm(), out_shape=jax.ShapeDtypeStruct((4, R, W), shard.dtype),
        scratch_shapes=[pltpu.SemaphoreType.DMA((6,)), pltpu.SemaphoreType.DMA((6,))],
    )(shard)


def _gather_weights(arrs, name):
    n_arr = len(arrs)

    def body(*refs):
        srcs, outs = refs[:n_arr], refs[n_arr:2 * n_arr]
        send_sems, recv_sems = refs[2 * n_arr:]
        x, y, c = lax.axis_index("x"), lax.axis_index("y"), lax.axis_index("c")
        me = 2 * x + y
        peers = _peers(x, y)

        def copy(sem, src, dst, to):
            return pltpu.make_async_remote_copy(src_ref=src, dst_ref=dst, send_sem=send_sems.at[sem], recv_sem=recv_sems.at[sem],
                                                device_id=to, device_id_type=MESH)

        started = []
        for a in range(n_arr):
            for j, (px, py) in enumerate(peers):
                cp = copy(6 * a + j, srcs[a].at[c], outs[a].at[me, c], (px, py, c))
                cp.start()
                started.append(cp)
        for a in range(n_arr):
            for j, (px, py) in enumerate(peers):
                landed = outs[a].at[2 * px + py, c]
                copy(6 * a + j, srcs[a].at[c], landed, (px, py, c)).wait_recv()
                cp = copy(6 * a + 3 + j, landed, landed, (x, y, 1 - c))
                cp.start()
                started.append(cp)
        for a in range(n_arr):
            for j, (px, py) in enumerate(peers):
                copy(6 * a + 3 + j, srcs[a].at[1 - c], outs[a].at[2 * px + py, 1 - c], (x, y, 1 - c)).wait_recv()
        for cp in started:
            cp.wait_send()

    return pl.pallas_call(
        body, name=name, in_specs=[_hbm()] * n_arr, out_specs=tuple([_hbm()] * n_arr),
        out_shape=tuple(jax.ShapeDtypeStruct((4,) + t.shape, t.dtype) for t in arrs),
        scratch_shapes=[pltpu.SemaphoreType.DMA((6 * n_arr,)), pltpu.SemaphoreType.DMA((6 * n_arr,))],
    )(*arrs)


def _gather_chips(shard, name):
    R, W = shard.shape

    def body(src_ref, out_ref, send_sems, recv_sems, local_sem):
        x, y, c = lax.axis_index("x"), lax.axis_index("y"), lax.axis_index("c")
        me = 2 * x + y
        mine = pltpu.make_async_copy(src_ref, out_ref.at[me], local_sem)
        mine.start()

        def copy(j, slot, to):
            return pltpu.make_async_remote_copy(src_ref=src_ref, dst_ref=out_ref.at[slot], send_sem=send_sems.at[j],
                                                recv_sem=recv_sems.at[j], device_id=(to[0], to[1], c), device_id_type=MESH)

        sends = [copy(j, me, peer) for j, peer in enumerate(_peers(x, y))]
        for cp in sends:
            cp.start()
        for j, peer in enumerate(_peers(x, y)):
            copy(j, 2 * peer[0] + peer[1], peer).wait_recv()
        for cp in sends:
            cp.wait_send()
        mine.wait()

    return pl.pallas_call(
        body, name=name, in_specs=[_hbm()], out_specs=_hbm(), out_shape=jax.ShapeDtypeStruct((4, R, W), shard.dtype),
        scratch_shapes=[pltpu.SemaphoreType.DMA((3,)), pltpu.SemaphoreType.DMA((3,)), pltpu.SemaphoreType.DMA],
    )(shard)


def _pair_swap_halves(g4):
    n, R, W = g4.shape
    Rh = R // 2

    def body(src_ref, out_ref, send_sem, recv_sem):
        x, y, c = lax.axis_index("x"), lax.axis_index("y"), lax.axis_index("c")
        cp = pltpu.make_async_remote_copy(src_ref=src_ref.at[:, pl.ds((1 - c) * Rh, Rh), :], dst_ref=out_ref, send_sem=send_sem,
                                          recv_sem=recv_sem, device_id=(x, y, 1 - c), device_id_type=MESH)
        cp.start()
        cp.wait()

    return pl.pallas_call(
        body, name="grad_pair_swap", in_specs=[_hbm()], out_specs=_hbm(), out_shape=jax.ShapeDtypeStruct((n, Rh, W), g4.dtype),
        scratch_shapes=[pltpu.SemaphoreType.DMA, pltpu.SemaphoreType.DMA],
    )(g4)


def _pair_add(g4, sib, c_arr):
    n, R, W = g4.shape
    Rh = R // 2
    tr = _tile_rows(Rh)
    nb = Rh // tr

    def body(c_ref, a_ref, b_ref, o_ref):
        o_ref[...] = (a_ref[...].astype(F32) + b_ref[...].astype(F32)).astype(o_ref.dtype)

    return pl.pallas_call(
        body, name="grad_pair_add",
        grid_spec=pltpu.PrefetchScalarGridSpec(
            num_scalar_prefetch=1, grid=(n, nb),
            in_specs=[pl.BlockSpec((None, tr, W), lambda s, i, c: (s, c[0] * nb + i, 0)), pl.BlockSpec((None, tr, W), lambda s, i, c: (s, i, 0))],
            out_specs=pl.BlockSpec((None, tr, W), lambda s, i, c: (s, i, 0))),
        out_shape=jax.ShapeDtypeStruct((n, Rh, W), g4.dtype), compiler_params=_cparams(("parallel", "parallel")),
    )(c_arr, g4, sib)


def _tile_rows(n):
    for t in (512, 480, 400, 320, 256, 240, 160, 128, 80, 64, 40, 32, 16, 8):
        if n % t == 0:
            return t
    return n


def _chips_exchange(part):
    n, Rh, W = part.shape

    def body(src_ref, out_ref, send_sems, recv_sems):
        x, y, c = lax.axis_index("x"), lax.axis_index("y"), lax.axis_index("c")

        def copy(j, to):
            return pltpu.make_async_remote_copy(src_ref=src_ref.at[2 * to[0] + to[1]], dst_ref=out_ref.at[j], send_sem=send_sems.at[j],
                                                recv_sem=recv_sems.at[j], device_id=(to[0], to[1], c), device_id_type=MESH)

        cps = [copy(j, peer) for j, peer in enumerate(_peers(x, y))]
        for cp in cps:
            cp.start()
        for cp in cps:
            cp.wait()

    return pl.pallas_call(
        body, name="grad_chips_exchange", in_specs=[_hbm()], out_specs=_hbm(), out_shape=jax.ShapeDtypeStruct((3, Rh, W), part.dtype),
        scratch_shapes=[pltpu.SemaphoreType.DMA((3,)), pltpu.SemaphoreType.DMA((3,))],
    )(part)


def _chips_add(part, got, k_arr, c_arr):
    n, Rh, W = part.shape
    tr = _tile_rows(Rh)
    nb = Rh // tr

    def body(k_ref, c_ref, a_ref, b_ref, o_ref):
        mine = pl.program_id(0) == c_ref[0]

        @pl.when(mine)
        def _():
            o_ref[...] = ((a_ref[...].astype(F32) + b_ref[0].astype(F32)) + b_ref[1].astype(F32)) + b_ref[2].astype(F32)

        @pl.when(jnp.logical_not(mine))
        def _():
            o_ref[...] = jnp.zeros_like(o_ref)

    return pl.pallas_call(
        body, name="grad_chips_add",
        grid_spec=pltpu.PrefetchScalarGridSpec(
            num_scalar_prefetch=2, grid=(2, nb),
            in_specs=[pl.BlockSpec((None, tr, W), lambda h, i, k, c: (k[0], i, 0)), pl.BlockSpec((3, tr, W), lambda h, i, k, c: (0, i, 0))],
            out_specs=pl.BlockSpec((tr, W), lambda h, i, k, c: (h * nb + i, 0))),
        out_shape=jax.ShapeDtypeStruct((2 * Rh, W), F32), compiler_params=_cparams(("parallel", "parallel")),
    )(k_arr, c_arr, part, got)


def _pair_gather(buf):
    R, W = buf.shape
    Rh = R // 2

    def body(src_ref, out_ref, send_sem, recv_sem):
        x, y, c = lax.axis_index("x"), lax.axis_index("y"), lax.axis_index("c")
        mine, other = pl.ds(c * Rh, Rh), pl.ds((1 - c) * Rh, Rh)
        pltpu.make_async_remote_copy(src_ref=src_ref.at[mine], dst_ref=out_ref.at[mine], send_sem=send_sem, recv_sem=recv_sem,
                                     device_id=(x, y, 1 - c), device_id_type=MESH).start()
        pltpu.make_async_remote_copy(src_ref=src_ref.at[mine], dst_ref=out_ref.at[other], send_sem=send_sem, recv_sem=recv_sem,
                                     device_id=(x, y, 1 - c), device_id_type=MESH).wait()

    return pl.pallas_call(
        body, name="grad_pair_gather", in_specs=[_hbm()], out_specs=_hbm(), out_shape=jax.ShapeDtypeStruct((R, W), buf.dtype),
        input_output_aliases={0: 0}, scratch_shapes=[pltpu.SemaphoreType.DMA, pltpu.SemaphoreType.DMA],
    )(buf)


def _gather_all(buf):
    R, W = buf.shape

    def body(src_ref, out_ref, send_sems, recv_sems, local_sem):
        x, y, c = lax.axis_index("x"), lax.axis_index("y"), lax.axis_index("c")
        me = 4 * x + 2 * y + c
        mine = pltpu.make_async_copy(src_ref, out_ref.at[me], local_sem)
        mine.start()
        rel = [((x + (r >> 2 & 1)) % 2, (y + (r >> 1 & 1)) % 2, (c + (r & 1)) % 2) for r in range(1, 8)]

        def copy(j, slot, to):
            return pltpu.make_async_remote_copy(src_ref=src_ref, dst_ref=out_ref.at[slot], send_sem=send_sems.at[j],
                                                recv_sem=recv_sems.at[j], device_id=to, device_id_type=MESH)

        sends = [copy(j, me, to) for j, to in enumerate(rel)]
        for cp in sends:
            cp.start()
        for j, to in enumerate(rel):
            copy(j, 4 * to[0] + 2 * to[1] + to[2], to).wait_recv()
        for cp in sends:
            cp.wait_send()
        mine.wait()

    return pl.pallas_call(
        body, name="small_gather", in_specs=[_hbm()], out_specs=_hbm(), out_shape=jax.ShapeDtypeStruct((8, R, W), buf.dtype),
        scratch_shapes=[pltpu.SemaphoreType.DMA((7,)), pltpu.SemaphoreType.DMA((7,)), pltpu.SemaphoreType.DMA],
    )(buf)


def _sum_slots(stack):
    n, R, W = stack.shape
    tr = _tile_rows(R)

    def body(s_ref, o_ref):
        tot = s_ref[0]
        for j in range(1, n):
            tot = tot + s_ref[j]
        o_ref[...] = tot

    return pl.pallas_call(
        body, name="small_sum", grid=(R // tr,), in_specs=[pl.BlockSpec((n, tr, W), lambda i: (0, i, 0))],
        out_specs=pl.BlockSpec((tr, W), lambda i: (i, 0)), out_shape=jax.ShapeDtypeStruct((R, W), F32),
        compiler_params=_cparams(("parallel",)),
    )(stack)


def _adamw(wp, gp, mp, vp, name):
    R, W = wp.shape
    tr = R
    for t in (1024, 512, 256, 128, 64, 32, 16, 8):
        if R % t == 0 and t * W <= 512 * 1024:
            tr = t
            break
    c1 = 1.0 - ADAM_B1 ** ADAM_STEP
    c2 = 1.0 - ADAM_B2 ** ADAM_STEP

    def body(w_ref, g_ref, m_ref, v_ref, d_ref, mo_ref, vo_ref):
        gv = g_ref[...]
        m = ADAM_B1 * m_ref[...] + (1.0 - ADAM_B1) * gv
        v = ADAM_B2 * v_ref[...] + (1.0 - ADAM_B2) * (gv * gv)
        m_hat = m / c1
        v_hat = v / c2
        d_ref[...] = -ADAM_LR * (m_hat / (jnp.sqrt(v_hat) + ADAM_EPS) + ADAM_WD * w_ref[...])
        mo_ref[...] = m
        vo_ref[...] = v

    blk = pl.BlockSpec((tr, W), lambda i: (i, 0))
    shp = jax.ShapeDtypeStruct((R, W), F32)
    return pl.pallas_call(body, name=name, grid=(R // tr,), in_specs=[blk] * 4, out_specs=(blk,) * 3, out_shape=(shp,) * 3,
                          compiler_params=_cparams(("parallel",)))(wp, gp, mp, vp)


def _pack(arrs, rows):
    flat = jnp.concatenate([a.reshape(-1) for a in arrs])
    return jnp.pad(flat, (0, rows * PACK_W - flat.shape[0])).reshape(rows, PACK_W)


def _unpack(buf, shapes):
    flat = buf.reshape(-1)
    out, off = [], 0
    for shp in shapes:
        n = int(np.prod(shp))
        out.append(flat[off:off + n].reshape(shp))
        off += n
    return out


def _rows_for(shapes, mult):
    n = sum(int(np.prod(s)) for s in shapes)
    rows = -(-n // PACK_W)
    return -(-rows // mult) * mult


def _shard_major(g, name):
    L, K, N = g.shape
    if name in _ROW_SHARDED:
        t = g.reshape(L, 4, K // 4, N).transpose(1, 0, 2, 3)
    else:
        t = g.reshape(L, K, 4, N // 4).transpose(2, 0, 1, 3)
    return t.reshape(4, -1, PACK_W)


def _join_shards(blocks, name):
    return jnp.concatenate(blocks, axis=1 if name in _ROW_SHARDED else 2)


def kernel(x, p, mix_norm, w_in, gate_b, conv_w, conv_b, lru_wa, lru_ba, lru_wx, lru_bx, lru_lambda, mla_q_norm, mla_wuq, mla_kv_norm, mla_wukv, fox_bf, w_br_a, w_br_b, w_br_c, w_o, ffn_norm, w_gate_up, w_down, ple_norm, w_ple_gate, w_ple, final_norm, loss_target, m_mix_norm, m_w_in, m_gate_b, m_conv_w, m_conv_b, m_lru_wa, m_lru_ba, m_lru_wx, m_lru_bx, m_lru_lambda, m_mla_q_norm, m_mla_wuq, m_mla_kv_norm, m_mla_wukv, m_fox_bf, m_w_br_a, m_w_br_b, m_w_br_c, m_w_o, m_ffn_norm, m_w_gate_up, m_w_down, m_ple_norm, m_w_ple_gate, m_w_ple, m_final_norm, v_mix_norm, v_w_in, v_gate_b, v_conv_w, v_conv_b, v_lru_wa, v_lru_ba, v_lru_wx, v_lru_bx, v_lru_lambda, v_mla_q_norm, v_mla_wuq, v_mla_kv_norm, v_mla_wukv, v_fox_bf, v_w_br_a, v_w_br_b, v_w_br_c, v_w_o, v_ffn_norm, v_w_gate_up, v_w_down, v_ple_norm, v_w_ple_gate, v_w_ple, v_final_norm):
    a = dict(locals())
    names = list(_LAYER_WEIGHTS) + ["final_norm"]
    W = {n: a[n] for n in names}
    M = {n: a["m_" + n] for n in names}
    V = {n: a["v_" + n] for n in names}
    ix, iy, ic = lax.axis_index("x"), lax.axis_index("y"), lax.axis_index("c")

    sharded = list(_BIG) + ["conv_w"]
    shard_shapes = [W[n].shape for n in sharded]
    R = _rows_for(shard_shapes, 64)
    mine = [W[n].astype(BF16) for n in _BIG] + [conv_w]
    gathered = _gather_weights(mine, "weight_gather")
    me = 2 * ix + iy
    gathered = [lax.dynamic_update_slice(g, t[None], (me,) + (0,) * t.ndim) for g, t in zip(gathered, mine)]
    full = {n: _join_shards([g[k] for k in range(4)], n) for n, g in zip(sharded, gathered)}
    conv_w_full = full["conv_w"]
    layers = []
    for i in range(DEPTH):
        lw = {n: W[n][i] for n in _SMALL}
        for n in _BIG:
            lw[n] = full[n][i]
        lw["conv_w"] = conv_w_full[i]
        layers.append(lw)

    loss_sum, dx, grads, d_final = _local_step(x[0], p[:, 0], layers, final_norm, loss_target[0])
    loss = lax.psum(loss_sum, ("x", "y", "c"))

    parts = [_shard_major(jnp.stack([grads[i][n] for i in range(DEPTH)]), n).astype(BF16) for n in sharded]
    g4, off = jnp.zeros((4, R, PACK_W), BF16), 0
    for t in parts:
        g4 = lax.dynamic_update_slice(g4, t, (0, off, 0))
        off += t.shape[1]
    c_arr = jnp.reshape(ic, (1,)).astype(jnp.int32)
    k_arr = jnp.reshape(2 * ix + iy, (1,)).astype(jnp.int32)
    pair = _pair_add(g4, _pair_swap_halves(g4), c_arr)
    g_pack = _pair_gather(_chips_add(pair, _chips_exchange(pair), k_arr, c_arr))
    big_out = {}
    for n, gsh in zip(sharded, _unpack(g_pack, shard_shapes)):
        view = lambda t: t.reshape(-1, t.shape[-1])
        d, nm, nv = _adamw(view(W[n]), view(gsh), view(M[n]), view(V[n]), "adamw_" + n)
        for key, arr in (("g", gsh), ("d", d), ("m", nm), ("v", nv)):
            big_out[(key, n)] = arr.reshape(W[n].shape)

    pick = lambda src, n, i: src[n] if i is None else src[n][i]
    small = [(n, i) for i in range(DEPTH) for n in _SMALL] + [("final_norm", None)]
    small_shapes = [pick(W, n, i).shape for n, i in small]
    Rs = _rows_for(small_shapes, 8)
    sg = _pack([d_final if i is None else grads[i][n] for n, i in small], Rs)
    sg = _sum_slots(_gather_all(sg))
    sw = _pack([pick(W, n, i) for n, i in small], Rs)
    sm = _pack([pick(M, n, i) for n, i in small], Rs)
    sv_ = _pack([pick(V, n, i) for n, i in small], Rs)
    sd, snm, snv = _adamw(sw, sg, sm, sv_, "adamw_replicated")
    small_out = {}
    for key, buf in (("g", sg), ("d", sd), ("m", snm), ("v", snv)):
        for (n, i), arr in zip(small, _unpack(buf, small_shapes)):
            small_out[(key, n, i)] = arr

    def assemble(key, n):
        if n == "final_norm":
            return small_out[(key, n, None)]
        if n in sharded:
            return big_out[(key, n)]
        return jnp.stack([small_out[(key, n, i)] for i in range(DEPTH)])

    outs = [loss, dx[None]]
    for key in ("g", "d", "m", "v"):
        outs += [assemble(key, n) for n in names]
    return tuple(outs)
```

```python
import functools
import math

import numpy as np
import jax
import jax.numpy as jnp
from jax import lax
from jax.experimental import pallas as pl
from jax.experimental.pallas import tpu as pltpu

F32, BF16 = jnp.float32, jnp.bfloat16
MESH = pl.DeviceIdType.MESH

D_MODEL = 1024
DEPTH = 2
EPS = 1e-6
NEG_INF = -1e30
LRU_WIDTH = 512
LRU_HEADS = 8
LRU_C = 8.0
CONV_WIDTH = 4
HEADS = 8
MLA_Q_LORA = 384
MLA_KV_LORA = 256
MLA_NOPE = 64
MLA_ROPE = 32
MLA_V = 64
ROPE_BASE = 10000.0
FOX_HEAD_DIM = 64
D_FF = 2816
PLE_DIM = 256
HEAD_PAD = 128
MLA_SCALE = (MLA_NOPE + MLA_ROPE) ** -0.5
FOX_SCALE = FOX_HEAD_DIM ** -0.5

ADAM_LR, ADAM_B1, ADAM_B2, ADAM_EPS, ADAM_WD, ADAM_STEP = 0.001, 0.9, 0.999, 1e-08, 0.01, 10

VMEM_LIMIT_BYTES = 48 * 1024 * 1024
LANES = 128
PACK_W = 1024

ROW_TILE = 512
ATTN_TILE = 1024
LRU_CHUNK = 512


def _cparams(dims):
    return pltpu.CompilerParams(dimension_semantics=dims, vmem_limit_bytes=VMEM_LIMIT_BYTES)


def _tile(n, cap):
    if n <= cap:
        return n
    t = (cap // LANES) * LANES
    while t >= LANES:
        if n % t == 0:
            return t
        t -= LANES
    raise ValueError(f"no tile for {n} under {cap}")


def _rows(n):
    return min(ROW_TILE, n)


MM_VMEM_BUDGET = 36 * 1024 * 1024


def _mm_tiles(M, N, K, a_bytes, b_bytes, o_bytes, has_res):
    best, best_work = None, 0
    for tm in {_tile(M, c) for c in (1024, 512, 256)}:
        for tn in {_tile(N, c) for c in (1792, 1024, 512)}:
            for tk in {_tile(K, c) for c in (2048, 1408, 1024, 512)}:
                need = 2 * (tm * tk * a_bytes + tk * tn * b_bytes + tm * tn * o_bytes + (tm * tn * 4 if has_res else 0))
                need += tm * tn * 4 if tk < K else 0
                need += tm * tn * 4
                if need <= MM_VMEM_BUDGET and tm * tn * tk > best_work:
                    best, best_work = (tm, tn, tk), tm * tn * tk
    assert best is not None, (M, N, K)
    return best

def _mm(a, b, *, ta=False, tb=False, out_dtype=F32, res=None, bias=None, name):
    K, M = a.shape if ta else a.shape[::-1]
    N, K2 = b.shape if tb else b.shape[::-1]
    assert K == K2, (name, a.shape, b.shape)
    assert res is None or bias is None
    tm, tn, tk = _mm_tiles(M, N, K, a.dtype.itemsize, b.dtype.itemsize, jnp.dtype(out_dtype).itemsize, res is not None)
    nk = K // tk
    a_spec = pl.BlockSpec((tk, tm), lambda i, j, k: (k, i)) if ta else pl.BlockSpec((tm, tk), lambda i, j, k: (i, k))
    b_spec = pl.BlockSpec((tn, tk), lambda i, j, k: (j, k)) if tb else pl.BlockSpec((tk, tn), lambda i, j, k: (k, j))
    o_spec = pl.BlockSpec((tm, tn), lambda i, j, k: (i, j))
    dn = (((0,) if ta else (1,), (1,) if tb else (0,)), ((), ()))
    if bias is not None:
        res, r_spec = bias, pl.BlockSpec((1, tn), lambda i, j, k: (0, j))
    else:
        r_spec = o_spec
    has_res = res is not None

    def body(*refs):
        a_ref, b_ref = refs[0], refs[1]
        r_ref = refs[2] if has_res else None
        o_ref = refs[3] if has_res else refs[2]
        av, bv = a_ref[...], b_ref[...]
        if av.dtype != BF16:
            av = av.astype(BF16)
        if bv.dtype != BF16:
            bv = bv.astype(BF16)
        part = lax.dot_general(av, bv, dn, preferred_element_type=F32)

        def finish(total):
            if has_res:
                total = total + r_ref[...]
            o_ref[...] = total.astype(out_dtype)

        if nk == 1:
            finish(part)
        else:
            acc = refs[-1]
            k = pl.program_id(2)

            @pl.when(k == 0)
            def _():
                acc[...] = part

            @pl.when(k > 0)
            def _():
                acc[...] += part

            @pl.when(k == nk - 1)
            def _():
                finish(acc[...])

    in_specs = [a_spec, b_spec] + ([r_spec] if has_res else [])
    args = (a, b) + ((res,) if has_res else ())
    return pl.pallas_call(
        body, name=name, grid=(M // tm, N // tn, nk), in_specs=in_specs, out_specs=o_spec,
        out_shape=jax.ShapeDtypeStruct((M, N), out_dtype),
        scratch_shapes=[pltpu.VMEM((tm, tn), F32)] if nk > 1 else [],
        compiler_params=_cparams(("parallel", "parallel", "arbitrary")),
    )(*args)


def _mm_res_norm(a, b, res, g, name):
    M, K = a.shape
    N = b.shape[1]
    tm, tk = _tile(M, 512), _tile(K, 1408)
    nk = K // tk

    def body(a_ref, b_ref, r_ref, g_ref, o_ref, h_ref, *scratch):
        part = jnp.dot(a_ref[...], b_ref[...], preferred_element_type=F32)

        def finish(total):
            xn = total + r_ref[...]
            o_ref[...] = xn
            rstd = lax.rsqrt(jnp.mean(xn * xn, axis=1, keepdims=True) + EPS)
            h_ref[...] = (xn * rstd * g_ref[...]).astype(BF16)

        if nk == 1:
            finish(part)
        else:
            acc = scratch[0]
            k = pl.program_id(1)

            @pl.when(k == 0)
            def _():
                acc[...] = part

            @pl.when(k > 0)
            def _():
                acc[...] += part

            @pl.when(k == nk - 1)
            def _():
                finish(acc[...])

    row = pl.BlockSpec((tm, N), lambda i, k: (i, 0))
    return pl.pallas_call(
        body, name=name, grid=(M // tm, nk),
        in_specs=[pl.BlockSpec((tm, tk), lambda i, k: (i, k)), pl.BlockSpec((tk, N), lambda i, k: (k, 0)), row,
                  pl.BlockSpec((1, N), lambda i, k: (0, 0))],
        out_specs=(row, row), out_shape=(jax.ShapeDtypeStruct((M, N), F32), jax.ShapeDtypeStruct((M, N), BF16)),
        scratch_shapes=[pltpu.VMEM((tm, N), F32)] if nk > 1 else [],
        compiler_params=_cparams(("parallel", "arbitrary")),
    )(a, b, res, g.reshape(1, N))


def _mm_norm_bwd(a, b, x, g, add, name):
    M, K = a.shape
    N = b.shape[0]
    tm, tk = _tile(M, 512), _tile(K, 1408)
    nk = K // tk

    def body(a_ref, b_ref, x_ref, g_ref, add_ref, dx_ref, dg_ref, *scratch):
        i, k = pl.program_id(0), pl.program_id(1)
        part = lax.dot_general(a_ref[...], b_ref[...], _NT, preferred_element_type=F32)

        @pl.when(jnp.logical_and(i == 0, k == 0))
        def _():
            dg_ref[...] = jnp.zeros_like(dg_ref)

        def finish(dyv):
            xf = x_ref[...]
            rstd = lax.rsqrt(jnp.mean(xf * xf, axis=1, keepdims=True) + EPS)
            xhat = xf * rstd
            dxh = dyv * g_ref[...]
            dx_ref[...] = rstd * (dxh - xhat * jnp.mean(dxh * xhat, axis=1, keepdims=True)) + add_ref[...]
            dg_ref[...] += jnp.sum(dyv * xhat, axis=0, keepdims=True)

        if nk == 1:
            finish(part)
        else:
            acc = scratch[0]

            @pl.when(k == 0)
            def _():
                acc[...] = part

            @pl.when(k > 0)
            def _():
                acc[...] += part

            @pl.when(k == nk - 1)
            def _():
                finish(acc[...])

    row = pl.BlockSpec((tm, N), lambda i, k: (i, 0))
    vec = pl.BlockSpec((1, N), lambda i, k: (0, 0))
    dx, dg = pl.pallas_call(
        body, name=name, grid=(M // tm, nk),
        in_specs=[pl.BlockSpec((tm, tk), lambda i, k: (i, k)), pl.BlockSpec((N, tk), lambda i, k: (0, k)), row, vec, row],
        out_specs=(row, vec), out_shape=(jax.ShapeDtypeStruct((M, N), F32), jax.ShapeDtypeStruct((1, N), F32)),
        scratch_shapes=[pltpu.VMEM((tm, N), F32)] if nk > 1 else [],
        compiler_params=_cparams(("arbitrary", "arbitrary")),
    )(a, b, x, g.reshape(1, N), add)
    return dx, dg.reshape(N)


FFN_TILE = 1408
FFN_SUBTILES = ((0, 512), (512, 1024), (1024, 1408))


def _ffn_pair_columns(w_gate_up):
    F = w_gate_up.shape[-1] // 2
    parts = []
    for j in range(F // FFN_TILE):
        parts += [w_gate_up[..., j * FFN_TILE:(j + 1) * FFN_TILE], w_gate_up[..., F + j * FFN_TILE:F + (j + 1) * FFN_TILE]]
    return jnp.concatenate(parts, axis=-1)


def _ffn_unpair_columns(dw):
    F = dw.shape[-1] // 2
    n = F // FFN_TILE
    blk = [dw[..., j * FFN_TILE:(j + 1) * FFN_TILE] for j in range(2 * n)]
    return jnp.concatenate(blk[0::2] + blk[1::2], axis=-1)


def _ffn_up(hn, w_pair):
    S, D = hn.shape
    W2 = w_pair.shape[1]
    F, tf = W2 // 2, FFN_TILE
    tm = _rows(S)

    def body(h_ref, w_ref, hf_ref, act_ref):
        hv = h_ref[...]
        for lo, hi in FFN_SUBTILES:
            gt = jnp.dot(hv, w_ref[:, lo:hi], preferred_element_type=F32)
            up = jnp.dot(hv, w_ref[:, tf + lo:tf + hi], preferred_element_type=F32)
            hf_ref[:, lo:hi] = gt.astype(BF16)
            hf_ref[:, tf + lo:tf + hi] = up.astype(BF16)
            act_ref[:, lo:hi] = (gt * jax.nn.sigmoid(gt) * up).astype(BF16)

    return pl.pallas_call(
        body, name="ffn_up", grid=(S // tm, F // tf),
        in_specs=[pl.BlockSpec((tm, D), lambda i, j: (i, 0)), pl.BlockSpec((D, 2 * tf), lambda i, j: (0, j))],
        out_specs=(pl.BlockSpec((tm, 2 * tf), lambda i, j: (i, j)), pl.BlockSpec((tm, tf), lambda i, j: (i, j))),
        out_shape=(jax.ShapeDtypeStruct((S, W2), BF16), jax.ShapeDtypeStruct((S, F), BF16)),
        compiler_params=_cparams(("parallel", "parallel")),
    )(hn, w_pair)


def _ffn_down_bwd(dx, w_down, hf):
    S, D = dx.shape
    F, tf = w_down.shape[0], FFN_TILE
    tm = _rows(S)

    def body(d_ref, w_ref, h_ref, o_ref):
        dv = d_ref[...].astype(BF16)
        for lo, hi in FFN_SUBTILES:
            dact = lax.dot_general(dv, w_ref[lo:hi, :], _NT, preferred_element_type=F32)
            gt, up = h_ref[:, lo:hi].astype(F32), h_ref[:, tf + lo:tf + hi].astype(F32)
            sg = jax.nn.sigmoid(gt)
            o_ref[:, lo:hi] = (dact * up * sg * (1.0 + gt * (1.0 - sg))).astype(BF16)
            o_ref[:, tf + lo:tf + hi] = (dact * gt * sg).astype(BF16)

    pair = pl.BlockSpec((tm, 2 * tf), lambda i, j: (i, j))
    return pl.pallas_call(
        body, name="ffn_down_bwd", grid=(S // tm, F // tf),
        in_specs=[pl.BlockSpec((tm, D), lambda i, j: (i, 0)), pl.BlockSpec((tf, D), lambda i, j: (j, 0)), pair],
        out_specs=pair, out_shape=jax.ShapeDtypeStruct((S, 2 * F), BF16),
        compiler_params=_cparams(("parallel", "parallel")),
    )(dx, w_down, hf)


def _rmsnorm_fwd(x, g, name):
    S, W = x.shape
    tm = _rows(S)

    def body(x_ref, g_ref, o_ref):
        xf = x_ref[...]
        rstd = lax.rsqrt(jnp.mean(xf * xf, axis=1, keepdims=True) + EPS)
        o_ref[...] = (xf * rstd * g_ref[...]).astype(BF16)

    return pl.pallas_call(
        body, name=name, grid=(S // tm,),
        in_specs=[pl.BlockSpec((tm, W), lambda i: (i, 0)), pl.BlockSpec((1, W), lambda i: (0, 0))],
        out_specs=pl.BlockSpec((tm, W), lambda i: (i, 0)),
        out_shape=jax.ShapeDtypeStruct((S, W), BF16), compiler_params=_cparams(("parallel",)),
    )(x, g.reshape(1, W))


def _rmsnorm_bwd(x, g, dy, *, add=None, out_dtype=F32, name):
    S, W = x.shape
    tm = _rows(S)
    has_add = add is not None

    def body(*refs):
        x_ref, g_ref, dy_ref = refs[:3]
        add_ref = refs[3] if has_add else None
        dx_ref, dg_ref = refs[-2], refs[-1]
        xf = x_ref[...]
        rstd = lax.rsqrt(jnp.mean(xf * xf, axis=1, keepdims=True) + EPS)
        xhat = xf * rstd
        dyv = dy_ref[...]
        dxh = dyv * g_ref[...]
        dx = rstd * (dxh - xhat * jnp.mean(dxh * xhat, axis=1, keepdims=True))
        if has_add:
            dx = dx + add_ref[...]
        dx_ref[...] = dx.astype(out_dtype)

        @pl.when(pl.program_id(0) == 0)
        def _():
            dg_ref[...] = jnp.zeros_like(dg_ref)

        dg_ref[...] += jnp.sum(dyv * xhat, axis=0, keepdims=True)

    row = pl.BlockSpec((tm, W), lambda i: (i, 0))
    vec = pl.BlockSpec((1, W), lambda i: (0, 0))
    dx, dg = pl.pallas_call(
        body, name=name, grid=(S // tm,),
        in_specs=[row, vec, row] + ([row] if has_add else []),
        out_specs=(row, vec),
        out_shape=(jax.ShapeDtypeStruct((S, W), out_dtype), jax.ShapeDtypeStruct((1, W), F32)),
        compiler_params=_cparams(("arbitrary",)),
    )(x, g.reshape(1, W), dy, *((add,) if has_add else ()))
    return dx, dg.reshape(W)


def _loss_head(x, g, target):
    S, W = x.shape
    tm = _rows(S)

    def body(x_ref, g_ref, t_ref, loss_ref, dx_ref, dg_ref):
        xf = x_ref[...]
        gv = g_ref[...]
        rstd = lax.rsqrt(jnp.mean(xf * xf, axis=1, keepdims=True) + EPS)
        xhat = xf * rstd
        err = xhat * gv - t_ref[...]
        part = 0.5 * jnp.sum(jnp.mean(err * err, axis=1, keepdims=True), axis=0, keepdims=True)
        dyv = err * (1.0 / W)
        dxh = dyv * gv
        dx_ref[...] = rstd * (dxh - xhat * jnp.mean(dxh * xhat, axis=1, keepdims=True))

        @pl.when(pl.program_id(0) == 0)
        def _():
            dg_ref[...] = jnp.zeros_like(dg_ref)
            loss_ref[...] = jnp.zeros_like(loss_ref)

        dg_ref[...] += jnp.sum(dyv * xhat, axis=0, keepdims=True)
        loss_ref[...] += part

    row = pl.BlockSpec((tm, W), lambda i: (i, 0))
    vec = pl.BlockSpec((1, W), lambda i: (0, 0))
    loss, dx, dg = pl.pallas_call(
        body, name="loss_head", grid=(S // tm,), in_specs=[row, vec, row],
        out_specs=(pl.BlockSpec((1, 1), lambda i: (0, 0)), row, vec),
        out_shape=(jax.ShapeDtypeStruct((1, 1), F32), jax.ShapeDtypeStruct((S, W), F32), jax.ShapeDtypeStruct((1, W), F32)),
        compiler_params=_cparams(("arbitrary",)),
    )(x, g.reshape(1, W), target)
    return loss[0, 0], dx, dg.reshape(W)


def _scan_fwd(a, b, row):
    T = a.shape[0]
    d = 1
    while d < T:
        keep = row >= d
        b = jnp.where(keep, a * pltpu.roll(b, d, axis=0) + b, b)
        a = jnp.where(keep, a * pltpu.roll(a, d, axis=0), a)
        d *= 2
    return a, b


def _scan_bwd(a, b, row):
    T = a.shape[0]
    d = 1
    while d < T:
        keep = row < T - d
        b = jnp.where(keep, a * pltpu.roll(b, T - d, axis=0) + b, b)
        a = jnp.where(keep, a * pltpu.roll(a, T - d, axis=0), a)
        d *= 2
    return a, b


def _expm1(x):
    small = x * (1.0 + x * (0.5 + x * (1.0 / 6 + x * (1.0 / 24 + x * (1.0 / 120 + x * (1.0 / 720 + x * (1.0 / 5040)))))))
    return jnp.where(jnp.abs(x) < 0.25, small, jnp.exp(x) - 1.0)


_GELU_C = math.sqrt(2.0 / math.pi)


def _gelu_and_grad(x):
    inner = _GELU_C * (x + 0.044715 * x * x * x)
    th = jnp.tanh(inner)
    val = 0.5 * x * (1.0 + th)
    grad = 0.5 * (1.0 + th) + 0.5 * x * (1.0 - th * th) * _GELU_C * (1.0 + 3 * 0.044715 * x * x)
    return val, grad


def _lru_gates(xc, wa, wx, ba, bx, lam):
    xcb = xc.astype(BF16)
    r = jax.nn.sigmoid(jnp.dot(xcb, wa, preferred_element_type=F32) + ba)
    ig = jax.nn.sigmoid(jnp.dot(xcb, wx, preferred_element_type=F32) + bx)
    sp = jax.nn.softplus(-lam)
    log_a = -LRU_C * r * sp
    a = jnp.exp(log_a)
    mult = jnp.sqrt(-_expm1(2.0 * log_a))
    return xcb, r, ig, sp, a, mult


def _lru_fwd(u, ug, conv_w, conv_b, wa_bd, wx_bd, ba, bx, lam):
    S, W = u.shape
    T = min(LRU_CHUNK, S)
    nl, nc = W // LANES, S // T

    def body(u_ref, ug_ref, cw_ref, cb_ref, wa_ref, wx_ref, ba_ref, bx_ref, lam_ref, ya_ref, xc_ref, h_ref, prev_u, h_carry):
        c = pl.program_id(1)

        @pl.when(c == 0)
        def _():
            prev_u[...] = jnp.zeros_like(prev_u)
            h_carry[...] = jnp.zeros_like(h_carry)

        uv = u_ref[...]
        row = lax.broadcasted_iota(jnp.int32, (T, LANES), 0)
        row8 = lax.broadcasted_iota(jnp.int32, (8, LANES), 0)
        cw = cw_ref[...]
        xc = cb_ref[...] + uv * cw[3:4, :]
        pv = prev_u[...]
        for k in range(1, CONV_WIDTH):
            us = pltpu.roll(uv, k, axis=0)
            top = jnp.where(row8 < k, pltpu.roll(pv, k, axis=0), us[0:8])
            us = jnp.concatenate([top, us[8:]], axis=0)
            xc = xc + us * cw[3 - k:4 - k, :]
        prev_u[...] = uv[T - 8:T]
        _, r, ig, sp, a, mult = _lru_gates(xc, wa_ref[...], wx_ref[...], ba_ref[...], bx_ref[...], lam_ref[...])
        bb = mult * (ig * xc)
        aa, hh = _scan_fwd(a, bb, row)
        h = hh + aa * h_carry[7:8, :]
        h_carry[...] = h[T - 8:T]
        gl, _ = _gelu_and_grad(ug_ref[...])
        ya_ref[...] = (h * gl).astype(BF16)
        xc_ref[...] = xc
        h_ref[...] = h

    seq = pl.BlockSpec((T, LANES), lambda l, c: (c, l))
    vec = pl.BlockSpec((1, LANES), lambda l, c: (0, l))
    mat = pl.BlockSpec((None, LANES, LANES), lambda l, c: (l, 0, 0))
    return pl.pallas_call(
        body, name="lru_fwd", grid=(nl, nc),
        in_specs=[seq, seq, pl.BlockSpec((CONV_WIDTH, LANES), lambda l, c: (0, l)), vec, mat, mat, vec, vec, vec],
        out_specs=(seq, seq, seq),
        out_shape=(jax.ShapeDtypeStruct((S, W), BF16), jax.ShapeDtypeStruct((S, W), F32), jax.ShapeDtypeStruct((S, W), F32)),
        scratch_shapes=[pltpu.VMEM((8, LANES), F32), pltpu.VMEM((8, LANES), F32)],
        compiler_params=_cparams(("parallel", "arbitrary")),
    )(u, ug, conv_w, conv_b.reshape(1, W), wa_bd, wx_bd, ba.reshape(1, W), bx.reshape(1, W), lam.reshape(1, W))


def _lru_bwd(dya, u, ug, xc, h, conv_w, wa_bd, wx_bd, ba, bx, lam):
    S, W = u.shape
    T = min(LRU_CHUNK, S)
    nl, nc = W // LANES, S // T
    tb8 = T // 8

    def body(dya_ref, u_ref, ug_ref, xc_ref, h_ref, hp_ref, cw_ref, wa_ref, wx_ref, ba_ref, bx_ref, lam_ref,
             du_ref, dug_ref, dcw_ref, dcb_ref, dba_ref, dbx_ref, dlam_ref, dwa_ref, dwx_ref,
             g_next, a_next, dxc_next):
        c = pl.program_id(1)

        @pl.when(c == 0)
        def _():
            g_next[...] = jnp.zeros_like(g_next)
            a_next[...] = jnp.zeros_like(a_next)
            dxc_next[...] = jnp.zeros_like(dxc_next)
            for ref in (dcw_ref, dcb_ref, dba_ref, dbx_ref, dlam_ref, dwa_ref, dwx_ref):
                ref[...] = jnp.zeros_like(ref)

        row = lax.broadcasted_iota(jnp.int32, (T, LANES), 0)
        row8 = lax.broadcasted_iota(jnp.int32, (8, LANES), 0)
        xcv = xc_ref[...]
        wa, wx = wa_ref[...], wx_ref[...]
        xcb, r, ig, sp, a, mult = _lru_gates(xcv, wa, wx, ba_ref[...], bx_ref[...], lam_ref[...])
        gl, dgl = _gelu_and_grad(ug_ref[...])
        dyav = dya_ref[...]
        hv = h_ref[...]
        dug_ref[...] = (dyav * hv * dgl).astype(BF16)
        dh = dyav * gl
        a_up = pltpu.roll(a, T - 1, axis=0)
        a_up = jnp.where(row == T - 1, a_next[0:1, :], a_up)
        prod, gg = _scan_bwd(a_up, dh, row)
        g = gg + prod * g_next[0:1, :]
        h_prev = pltpu.roll(hv, 1, axis=0)
        first_chunk = c == nc - 1
        h_before = jnp.where(first_chunk, 0.0, hp_ref[7:8, :])
        h_prev = jnp.where(row == 0, h_before, h_prev)
        da = g * h_prev
        d_mult = g * (ig * xcv)
        d_ig = g * mult * xcv
        dxc = g * mult * ig
        d_log_a = da * a - d_mult * (a * a) / mult
        d_r = d_log_a * (-LRU_C * sp)
        d_pa = d_r * r * (1.0 - r)
        d_px = d_ig * ig * (1.0 - ig)
        d_pab, d_pxb = d_pa.astype(BF16), d_px.astype(BF16)
        nt = (((1,), (1,)), ((), ()))
        tn = (((0,), (0,)), ((), ()))
        dxc = dxc + lax.dot_general(d_pab, wa, nt, preferred_element_type=F32) + lax.dot_general(d_pxb, wx, nt, preferred_element_type=F32)
        dwa_ref[...] += lax.dot_general(xcb, d_pab, tn, preferred_element_type=F32)
        dwx_ref[...] += lax.dot_general(xcb, d_pxb, tn, preferred_element_type=F32)
        dlam_ref[...] += jnp.sum(d_log_a * r, axis=0, keepdims=True)
        dba_ref[...] += jnp.sum(d_pa, axis=0, keepdims=True)
        dbx_ref[...] += jnp.sum(d_px, axis=0, keepdims=True)
        dcb_ref[...] += jnp.sum(dxc, axis=0, keepdims=True)
        uv = u_ref[...]
        cw = cw_ref[...]
        nxt = dxc_next[...]
        du = dxc * cw[3:4, :]
        dcw_ref[3:4, :] += jnp.sum(uv * dxc, axis=0, keepdims=True)
        for k in range(1, CONV_WIDTH):
            ds = pltpu.roll(dxc, T - k, axis=0)
            bot = jnp.where(row8 >= 8 - k, pltpu.roll(nxt, 8 - k, axis=0), ds[T - 8:T])
            ds = jnp.concatenate([ds[:T - 8], bot], axis=0)
            du = du + ds * cw[3 - k:4 - k, :]
            dcw_ref[3 - k:4 - k, :] += jnp.sum(uv * ds, axis=0, keepdims=True)
        du_ref[...] = du.astype(BF16)
        g_next[...] = g[0:8]
        a_next[...] = a[0:8]
        dxc_next[...] = dxc[0:8]

    seq = pl.BlockSpec((T, LANES), lambda l, c: (nc - 1 - c, l))
    before = pl.BlockSpec((8, LANES), lambda l, c: (jnp.maximum((nc - 1 - c) * tb8 - 1, 0), l))
    vec = pl.BlockSpec((1, LANES), lambda l, c: (0, l))
    cwb = pl.BlockSpec((CONV_WIDTH, LANES), lambda l, c: (0, l))
    mat = pl.BlockSpec((None, LANES, LANES), lambda l, c: (l, 0, 0))
    vshape = jax.ShapeDtypeStruct((1, W), F32)
    mshape = jax.ShapeDtypeStruct((nl, LANES, LANES), F32)
    return pl.pallas_call(
        body, name="lru_bwd", grid=(nl, nc),
        in_specs=[seq, seq, seq, seq, seq, before, cwb, mat, mat, vec, vec, vec],
        out_specs=(seq, seq, cwb, vec, vec, vec, vec, mat, mat),
        out_shape=(jax.ShapeDtypeStruct((S, W), BF16), jax.ShapeDtypeStruct((S, W), BF16),
                   jax.ShapeDtypeStruct((CONV_WIDTH, W), F32), vshape, vshape, vshape, vshape, mshape, mshape),
        scratch_shapes=[pltpu.VMEM((8, LANES), F32)] * 3,
        compiler_params=_cparams(("parallel", "arbitrary")),
    )(dya, u, ug, xc, h, h, conv_w, wa_bd, wx_bd, ba.reshape(1, W), bx.reshape(1, W), lam.reshape(1, W))


def _decay_fwd(f_logit, bf):
    S = f_logit.shape[0]
    T = min(LRU_CHUNK, S)

    def body(f_ref, b_ref, o_ref, carry):
        @pl.when(pl.program_id(0) == 0)
        def _():
            carry[...] = jnp.zeros_like(carry)

        row = lax.broadcasted_iota(jnp.int32, (T, LANES), 0)
        v = jax.nn.log_sigmoid(f_ref[...] + b_ref[...])
        d = 1
        while d < T:
            v = jnp.where(row >= d, v + pltpu.roll(v, d, axis=0), v)
            d *= 2
        v = v + carry[7:8, :]
        carry[...] = v[T - 8:T]
        o_ref[...] = v

    return pl.pallas_call(
        body, name="decay_fwd", grid=(S // T,),
        in_specs=[pl.BlockSpec((T, LANES), lambda c: (c, 0)), pl.BlockSpec((1, LANES), lambda c: (0, 0))],
        out_specs=pl.BlockSpec((T, LANES), lambda c: (c, 0)),
        out_shape=jax.ShapeDtypeStruct((S, LANES), F32), scratch_shapes=[pltpu.VMEM((8, LANES), F32)],
        compiler_params=_cparams(("arbitrary",)),
    )(f_logit, bf)


def _decay_bwd(d_dec, f_logit, bf):
    S = f_logit.shape[0]
    T = min(LRU_CHUNK, S)
    nc = S // T

    def body(dd_ref, f_ref, b_ref, df_ref, db_ref, carry):
        @pl.when(pl.program_id(0) == 0)
        def _():
            carry[...] = jnp.zeros_like(carry)
            db_ref[...] = jnp.zeros_like(db_ref)

        row = lax.broadcasted_iota(jnp.int32, (T, LANES), 0)
        v = dd_ref[...]
        d = 1
        while d < T:
            v = jnp.where(row < T - d, v + pltpu.roll(v, T - d, axis=0), v)
            d *= 2
        v = v + carry[0:1, :]
        carry[...] = v[0:8]
        df = v * jax.nn.sigmoid(-(f_ref[...] + b_ref[...]))
        df_ref[...] = df.astype(BF16)
        db_ref[...] += jnp.sum(df, axis=0, keepdims=True)

    seq = pl.BlockSpec((T, LANES), lambda c: (nc - 1 - c, 0))
    vec = pl.BlockSpec((1, LANES), lambda c: (0, 0))
    return pl.pallas_call(
        body, name="decay_bwd", grid=(nc,), in_specs=[seq, seq, vec], out_specs=(seq, vec),
        out_shape=(jax.ShapeDtypeStruct((S, LANES), BF16), jax.ShapeDtypeStruct((1, LANES), F32)),
        scratch_shapes=[pltpu.VMEM((8, LANES), F32)], compiler_params=_cparams(("arbitrary",)),
    )(d_dec, f_logit, bf)


def _rope_tables(S):
    pos = jnp.arange(S, dtype=F32)
    inv_freq = ROPE_BASE ** (-jnp.arange(0, MLA_ROPE, 2, dtype=F32) / MLA_ROPE)
    ang = pos[:, None] * inv_freq[None, :]
    cos, sin = jnp.cos(ang), jnp.sin(ang)
    half = MLA_ROPE // 2
    z = lambda n: jnp.zeros((S, n), F32)
    c_q = jnp.concatenate([jnp.ones((S, MLA_NOPE), F32), cos, cos, z(HEAD_PAD - MLA_NOPE - MLA_ROPE)], axis=1)
    c_k = jnp.concatenate([z(MLA_NOPE), cos, cos, z(HEAD_PAD - MLA_NOPE - MLA_ROPE)], axis=1)
    s_lo = jnp.concatenate([z(MLA_NOPE), -sin, z(HEAD_PAD - MLA_NOPE - half)], axis=1)
    s_hi = jnp.concatenate([z(MLA_NOPE + half), sin, z(HEAD_PAD - MLA_NOPE - MLA_ROPE)], axis=1)
    return c_q, c_k, s_lo, s_hi


def _rot(v, c, s_lo, s_hi):
    half = MLA_ROPE // 2
    return v * c + pltpu.roll(v, LANES - half, axis=1) * s_lo + pltpu.roll(v, half, axis=1) * s_hi


def _rot_t(dv, c, s_lo, s_hi):
    half = MLA_ROPE // 2
    return dv * c + pltpu.roll(dv * s_lo, half, axis=1) + pltpu.roll(dv * s_hi, LANES - half, axis=1)


def _rope_q(q_pre, c_q, s_lo, s_hi, *, transpose, out_dtype, name):
    S, W = q_pre.shape
    tm = _rows(S)
    fn = _rot_t if transpose else _rot

    def body(q_ref, c_ref, lo_ref, hi_ref, o_ref):
        c, lo, hi = c_ref[...], lo_ref[...], hi_ref[...]
        for hd in range(W // LANES):
            cols = slice(hd * LANES, (hd + 1) * LANES)
            o_ref[:, cols] = fn(q_ref[:, cols] * MLA_SCALE, c, lo, hi).astype(out_dtype)

    blk = pl.BlockSpec((tm, W), lambda i: (i, 0))
    tab = pl.BlockSpec((tm, LANES), lambda i: (i, 0))
    return pl.pallas_call(
        body, name=name, grid=(S // tm,), in_specs=[blk, tab, tab, tab], out_specs=blk,
        out_shape=jax.ShapeDtypeStruct((S, W), out_dtype), compiler_params=_cparams(("parallel",)),
    )(q_pre, c_q, s_lo, s_hi)


def _rope_k(k_pre, k_rope, c_k, s_lo, s_hi):
    S, W = k_pre.shape
    tm = _rows(S)

    def body(k_ref, r_ref, c_ref, lo_ref, hi_ref, o_ref):
        rot = _rot(r_ref[...], c_ref[...], lo_ref[...], hi_ref[...])
        for hd in range(W // LANES):
            cols = slice(hd * LANES, (hd + 1) * LANES)
            o_ref[:, cols] = (k_ref[:, cols] + rot).astype(BF16)

    blk = pl.BlockSpec((tm, W), lambda i: (i, 0))
    tab = pl.BlockSpec((tm, LANES), lambda i: (i, 0))
    return pl.pallas_call(
        body, name="rope_k", grid=(S // tm,), in_specs=[blk, tab, tab, tab, tab], out_specs=blk,
        out_shape=jax.ShapeDtypeStruct((S, W), BF16), compiler_params=_cparams(("parallel",)),
    )(k_pre, k_rope, c_k, s_lo, s_hi)


def _rope_k_bwd(dk, c_k, s_lo, s_hi):
    S, W = dk.shape
    tm = _rows(S)

    def body(dk_ref, c_ref, lo_ref, hi_ref, o_ref):
        tot = dk_ref[:, 0:LANES]
        for hd in range(1, W // LANES):
            tot = tot + dk_ref[:, hd * LANES:(hd + 1) * LANES]
        o_ref[...] = _rot_t(tot, c_ref[...], lo_ref[...], hi_ref[...]).astype(BF16)

    tab = pl.BlockSpec((tm, LANES), lambda i: (i, 0))
    return pl.pallas_call(
        body, name="rope_k_bwd", grid=(S // tm,), in_specs=[pl.BlockSpec((tm, W), lambda i: (i, 0)), tab, tab, tab],
        out_specs=tab, out_shape=jax.ShapeDtypeStruct((S, LANES), BF16), compiler_params=_cparams(("parallel",)),
    )(dk, c_k, s_lo, s_hi)


def _pairs(n, by_key):
    if by_key:
        pr = [(i, j) for j in range(n) for i in range(j, n)]
    else:
        pr = [(i, j) for i in range(n) for j in range(i + 1)]
    return (jnp.asarray(np.array([p[0] for p in pr], np.int32)), jnp.asarray(np.array([p[1] for p in pr], np.int32)), len(pr))


def _unit_mask(shape, unit, key_axis, q_off=0):
    q = lax.broadcasted_iota(jnp.int32, shape, 1 - key_axis) + q_off
    k = lax.broadcasted_iota(jnp.int32, shape, key_axis)
    if unit > 1:
        q, k = q // unit, k // unit
    return q >= k


_NT = (((1,), (1,)), ((), ()))


def _attn_fwd(q, k, v, dec_col, dec_row, *, unit, name):
    S, W = q.shape
    H = W // LANES
    T = min(ATTN_TILE, S)
    n = S // T
    qi, kj, npairs = _pairs(n, by_key=False)
    has_dec = dec_col is not None

    def body(qi_ref, kj_ref, *refs):
        if has_dec:
            q_ref, k_ref, v_ref, dc_ref, dr_ref, o_ref, lse_ref, m_s, l_s, acc = refs
        else:
            q_ref, k_ref, v_ref, o_ref, lse_ref, m_s, l_s, acc = refs
        t = pl.program_id(1)
        i, j = qi_ref[t], kj_ref[t]

        @pl.when(j == 0)
        def _():
            m_s[...] = jnp.full_like(m_s, NEG_INF)
            l_s[...] = jnp.zeros_like(l_s)
            acc[...] = jnp.zeros_like(acc)

        def step(diag):
            s = lax.dot_general(q_ref[...], k_ref[...], _NT, preferred_element_type=F32)
            if has_dec:
                s = s + (dc_ref[...] - dr_ref[...])
            if diag:
                s = jnp.where(_unit_mask((T, T), unit, 1), s, NEG_INF)
            m_prev = m_s[...]
            m_new = jnp.maximum(m_prev, jnp.max(s, axis=1, keepdims=True))
            alpha = jnp.exp(m_prev - m_new)
            p = jnp.exp(s - m_new)
            l_s[...] = alpha * l_s[...] + jnp.sum(p, axis=1, keepdims=True)
            acc[...] = alpha * acc[...] + jnp.dot(p.astype(BF16), v_ref[...], preferred_element_type=F32)
            m_s[...] = m_new

        @pl.when(j < i)
        def _():
            step(False)

        @pl.when(j == i)
        def _():
            step(True)
            o_ref[...] = (acc[...] / l_s[...]).astype(BF16)
            lse_ref[...] = m_s[...] + jnp.log(l_s[...])

    qb = pl.BlockSpec((T, LANES), lambda h, t, qi, kj: (qi[t], h))
    kb = pl.BlockSpec((T, LANES), lambda h, t, qi, kj: (kj[t], h))
    colq = pl.BlockSpec((None, T, 1), lambda h, t, qi, kj: (h, qi[t], 0))
    rowk = pl.BlockSpec((None, 1, T), lambda h, t, qi, kj: (h, 0, kj[t]))
    in_specs = [qb, kb, kb] + ([colq, rowk] if has_dec else [])
    args = (q, k, v) + ((dec_col, dec_row) if has_dec else ())
    return pl.pallas_call(
        body, name=name,
        grid_spec=pltpu.PrefetchScalarGridSpec(
            num_scalar_prefetch=2, grid=(H, npairs), in_specs=in_specs, out_specs=(qb, colq),
            scratch_shapes=[pltpu.VMEM((T, 1), F32), pltpu.VMEM((T, 1), F32), pltpu.VMEM((T, LANES), F32)]),
        out_shape=(jax.ShapeDtypeStruct((S, W), BF16), jax.ShapeDtypeStruct((H, S, 1), F32)),
        compiler_params=_cparams(("parallel", "arbitrary")),
    )(qi, kj, *args)


def _attn_delta(do, o):
    S, W = o.shape
    H = W // LANES
    tm = _rows(S)

    def body(do_ref, o_ref, d_ref):
        d_ref[...] = jnp.sum(do_ref[...].astype(F32) * o_ref[...].astype(F32), axis=1, keepdims=True)

    blk = pl.BlockSpec((tm, LANES), lambda h, i: (i, h))
    return pl.pallas_call(
        body, name="attn_delta", grid=(H, S // tm), in_specs=[blk, blk],
        out_specs=pl.BlockSpec((None, tm, 1), lambda h, i: (h, i, 0)),
        out_shape=jax.ShapeDtypeStruct((H, S, 1), F32), compiler_params=_cparams(("parallel", "parallel")),
    )(do, o)


def _attn_bwd_dq(q, k, v, do, lse, delta, dec_col, dec_row, *, unit, out_dtype, name):
    S, W = q.shape
    H = W // LANES
    T = min(ATTN_TILE, S)
    n = S // T
    qi, kj, npairs = _pairs(n, by_key=False)
    has_dec = dec_col is not None

    def body(qi_ref, kj_ref, *refs):
        if has_dec:
            q_ref, k_ref, v_ref, do_ref, lse_ref, dl_ref, dc_ref, dr_ref, dq_ref, dd_ref, acc, dacc = refs
        else:
            q_ref, k_ref, v_ref, do_ref, lse_ref, dl_ref, dq_ref, acc = refs
        t = pl.program_id(1)
        i, j = qi_ref[t], kj_ref[t]

        @pl.when(j == 0)
        def _():
            acc[...] = jnp.zeros_like(acc)
            if has_dec:
                dacc[...] = jnp.zeros_like(dacc)

        def step(diag):
            kv = k_ref[...]
            s = lax.dot_general(q_ref[...], kv, _NT, preferred_element_type=F32)
            if has_dec:
                s = s + (dc_ref[...] - dr_ref[...])
            if diag:
                s = jnp.where(_unit_mask((T, T), unit, 1), s, NEG_INF)
            p = jnp.exp(s - lse_ref[...])
            dp = lax.dot_general(do_ref[...], v_ref[...], _NT, preferred_element_type=F32)
            ds = p * (dp - dl_ref[...])
            acc[...] += jnp.dot(ds.astype(BF16), kv, preferred_element_type=F32)
            if has_dec:
                dacc[...] += jnp.sum(ds, axis=1, keepdims=True)

        @pl.when(j < i)
        def _():
            step(False)

        @pl.when(j == i)
        def _():
            step(True)
            dq_ref[...] = acc[...].astype(out_dtype)
            if has_dec:
                dd_ref[...] = dacc[...]

    qb = pl.BlockSpec((T, LANES), lambda h, t, qi, kj: (qi[t], h))
    kb = pl.BlockSpec((T, LANES), lambda h, t, qi, kj: (kj[t], h))
    colq = pl.BlockSpec((None, T, 1), lambda h, t, qi, kj: (h, qi[t], 0))
    rowk = pl.BlockSpec((None, 1, T), lambda h, t, qi, kj: (h, 0, kj[t]))
    in_specs = [qb, kb, kb, qb, colq, colq] + ([colq, rowk] if has_dec else [])
    args = (q, k, v, do, lse, delta) + ((dec_col, dec_row) if has_dec else ())
    scratch = [pltpu.VMEM((T, LANES), F32)] + ([pltpu.VMEM((T, 1), F32)] if has_dec else [])
    out_specs = (qb, colq) if has_dec else qb
    out_shape = jax.ShapeDtypeStruct((S, W), out_dtype)
    if has_dec:
        out_shape = (out_shape, jax.ShapeDtypeStruct((H, S, 1), F32))
    res = pl.pallas_call(
        body, name=name,
        grid_spec=pltpu.PrefetchScalarGridSpec(num_scalar_prefetch=2, grid=(H, npairs), in_specs=in_specs,
                                               out_specs=out_specs, scratch_shapes=scratch),
        out_shape=out_shape, compiler_params=_cparams(("parallel", "arbitrary")),
    )(qi, kj, *args)
    return res if has_dec else (res, None)


def _attn_bwd_dkv(q, k, v, do, lse_row, delta_row, dec_col, dec_row, *, unit, dk_dtype, name):
    S, W = q.shape
    H = W // LANES
    T = min(ATTN_TILE, S)
    n = S // T
    qi, kj, npairs = _pairs(n, by_key=True)
    has_dec = dec_col is not None

    def body(qi_ref, kj_ref, *refs):
        if has_dec:
            q_ref, k_ref, v_ref, do_ref, lse_ref, dl_ref, dc_ref, dr_ref, dk_ref, dv_ref, dd_ref, kacc, vacc, dacc = refs
        else:
            q_ref, k_ref, v_ref, do_ref, lse_ref, dl_ref, dk_ref, dv_ref, kacc, vacc = refs
        t = pl.program_id(1)
        i, j = qi_ref[t], kj_ref[t]

        @pl.when(i == j)
        def _():
            kacc[...] = jnp.zeros_like(kacc)
            vacc[...] = jnp.zeros_like(vacc)
            if has_dec:
                dacc[...] = jnp.zeros_like(dacc)

        def step(diag):
            qv, dov = q_ref[...], do_ref[...]
            st = lax.dot_general(k_ref[...], qv, _NT, preferred_element_type=F32)
            if has_dec:
                st = st + (dr_ref[...] - dc_ref[...])
            if diag:
                st = jnp.where(_unit_mask((T, T), unit, 0), st, NEG_INF)
            pt = jnp.exp(st - lse_ref[...])
            dpt = lax.dot_general(v_ref[...], dov, _NT, preferred_element_type=F32)
            dst = pt * (dpt - dl_ref[...])
            vacc[...] += jnp.dot(pt.astype(BF16), dov, preferred_element_type=F32)
            kacc[...] += jnp.dot(dst.astype(BF16), qv, preferred_element_type=F32)
            if has_dec:
                dacc[...] -= jnp.sum(dst, axis=1, keepdims=True)

        @pl.when(i == j)
        def _():
            step(True)

        @pl.when(i > j)
        def _():
            step(False)

        @pl.when(i == n - 1)
        def _():
            dk_ref[...] = kacc[...].astype(dk_dtype)
            dv_ref[...] = vacc[...].astype(BF16)
            if has_dec:
                dd_ref[...] = dacc[...]

    qb = pl.BlockSpec((T, LANES), lambda h, t, qi, kj: (qi[t], h))
    kb = pl.BlockSpec((T, LANES), lambda h, t, qi, kj: (kj[t], h))
    rowq = pl.BlockSpec((None, 1, T), lambda h, t, qi, kj: (h, 0, qi[t]))
    colk = pl.BlockSpec((None, T, 1), lambda h, t, qi, kj: (h, kj[t], 0))
    in_specs = [qb, kb, kb, qb, rowq, rowq] + ([colk, rowq] if has_dec else [])
    args = (q, k, v, do, lse_row, delta_row) + ((dec_col, dec_row) if has_dec else ())
    scratch = [pltpu.VMEM((T, LANES), F32)] * 2 + ([pltpu.VMEM((T, 1), F32)] if has_dec else [])
    out_specs = (kb, kb) + ((colk,) if has_dec else ())
    out_shape = (jax.ShapeDtypeStruct((S, W), dk_dtype), jax.ShapeDtypeStruct((S, W), BF16))
    if has_dec:
        out_shape = out_shape + (jax.ShapeDtypeStruct((H, S, 1), F32),)
    res = pl.pallas_call(
        body, name=name,
        grid_spec=pltpu.PrefetchScalarGridSpec(num_scalar_prefetch=2, grid=(H, npairs), in_specs=in_specs,
                                               out_specs=out_specs, scratch_shapes=scratch),
        out_shape=out_shape, compiler_params=_cparams(("parallel", "arbitrary")),
    )(qi, kj, *args)
    return res if has_dec else (res[0], res[1], None)


def _attn_bwd(q, k, v, o, lse, do, dec_col, dec_row, *, unit, dq_dtype, dk_dtype, name):
    H, S = lse.shape[0], lse.shape[1]
    delta = _attn_delta(do, o)
    dq, dd_q = _attn_bwd_dq(q, k, v, do, lse, delta, dec_col, dec_row, unit=unit, out_dtype=dq_dtype, name=name + "_dq")
    dk, dv, dd_k = _attn_bwd_dkv(q, k, v, do, lse.reshape(H, 1, S), delta.reshape(H, 1, S), dec_col, dec_row,
                                 unit=unit, dk_dtype=dk_dtype, name=name + "_dkv")
    d_dec = None if dec_col is None else dd_q + dd_k
    return dq, dk, dv, d_dec


ONES_LANE = 64
ROW_SPLIT = 1
ROW_SPLIT_BWD = 4


def _ones_lane_bias():
    one = np.zeros((HEADS, HEAD_PAD), np.float32)
    one[:, ONES_LANE] = 1.0
    return jnp.asarray(one.reshape(1, HEADS * HEAD_PAD))


def _lane_sum(t):
    tot = t[:, 0:LANES]
    for c in range(1, t.shape[1] // LANES):
        tot = tot + t[:, c * LANES:(c + 1) * LANES]
    return tot


def _fa_fwd(q, k, v, dec_row, *, unit, name):
    S, W = q.shape
    H = W // LANES
    T = min(ATTN_TILE, S)
    n, hT = S // T, T // ROW_SPLIT
    qi, kj, npairs = _pairs(n, by_key=False)
    has_dec = dec_row is not None

    def body(qi_ref, kj_ref, *refs):
        if has_dec:
            q_ref, k_ref, v_ref, dr_ref, o_ref, lse_ref, lrow_ref, m_s, acc = refs
        else:
            q_ref, k_ref, v_ref, o_ref, lse_ref, lrow_ref, m_s, acc = refs
        t = pl.program_id(1)
        i, j = qi_ref[t], kj_ref[t]

        @pl.when(j == 0)
        def _():
            m_s[...] = jnp.full_like(m_s, NEG_INF)
            acc[...] = jnp.zeros_like(acc)

        def step(diag):
            for r in range(ROW_SPLIT):
                rows = slice(r * hT, (r + 1) * hT)
                nk = (r + 1) * hT if diag else T
                s = lax.dot_general(q_ref[rows, :], k_ref[0:nk, :], _NT, preferred_element_type=F32)
                if has_dec:
                    s = s - dr_ref[:, 0:nk]
                if diag:
                    s = jnp.where(_unit_mask((hT, nk), unit, 1, r * hT), s, NEG_INF)
                m_prev = m_s[rows, :]
                m_new = jnp.maximum(m_prev, jnp.max(s, axis=1, keepdims=True))
                alpha = jnp.exp(m_prev - m_new)
                p = jnp.exp(s - jnp.tile(m_new, (1, nk // LANES)))
                acc[rows, :] = alpha * acc[rows, :] + jnp.dot(p.astype(BF16), v_ref[0:nk, :], preferred_element_type=F32)
                m_s[rows, :] = m_new

        @pl.when(j < i)
        def _():
            step(False)

        @pl.when(j == i)
        def _():
            step(True)
            av = acc[...]
            l = av[:, ONES_LANE:ONES_LANE + 1]
            lane = lax.broadcasted_iota(jnp.int32, (T, LANES), 1)
            o_ref[...] = jnp.where(lane < ONES_LANE, av / l, 0.0).astype(BF16)
            lse = m_s[...] + jnp.log(l)
            lse_ref[...] = lse
            lrow_ref[...] = lse.T[0:1, :]

    qb = pl.BlockSpec((T, LANES), lambda h, t, qi, kj: (qi[t], h))
    kb = pl.BlockSpec((T, LANES), lambda h, t, qi, kj: (kj[t], h))
    repq = pl.BlockSpec((None, T, LANES), lambda h, t, qi, kj: (h, qi[t], 0))
    rowq = pl.BlockSpec((None, 1, T), lambda h, t, qi, kj: (h, 0, qi[t]))
    rowk = pl.BlockSpec((None, 1, T), lambda h, t, qi, kj: (h, 0, kj[t]))
    in_specs = [qb, kb, kb] + ([rowk] if has_dec else [])
    args = (q, k, v) + ((dec_row,) if has_dec else ())
    return pl.pallas_call(
        body, name=name,
        grid_spec=pltpu.PrefetchScalarGridSpec(
            num_scalar_prefetch=2, grid=(H, npairs), in_specs=in_specs, out_specs=(qb, repq, rowq),
            scratch_shapes=[pltpu.VMEM((T, LANES), F32), pltpu.VMEM((T, LANES), F32)]),
        out_shape=(jax.ShapeDtypeStruct((S, W), BF16), jax.ShapeDtypeStruct((H, S, LANES), F32),
                   jax.ShapeDtypeStruct((H, 1, S), F32)),
        compiler_params=_cparams(("parallel", "arbitrary")),
    )(qi, kj, *args)


def _fa_delta(do, o):
    S, W = o.shape
    H = W // LANES
    tm = _rows(S)

    def body(do_ref, o_ref, d_ref, drow_ref):
        for hd in range(H):
            cols = slice(hd * LANES, (hd + 1) * LANES)
            d = jnp.sum(do_ref[:, cols].astype(F32) * o_ref[:, cols].astype(F32), axis=1, keepdims=True)
            rep = jnp.broadcast_to(d, (tm, LANES))
            d_ref[hd] = rep
            drow_ref[hd] = rep.T[0:1, :]

    blk = pl.BlockSpec((tm, W), lambda i: (i, 0))
    return pl.pallas_call(
        body, name="attn_delta", grid=(S // tm,), in_specs=[blk, blk],
        out_specs=(pl.BlockSpec((H, tm, LANES), lambda i: (0, i, 0)), pl.BlockSpec((H, 1, tm), lambda i: (0, 0, i))),
        out_shape=(jax.ShapeDtypeStruct((H, S, LANES), F32), jax.ShapeDtypeStruct((H, 1, S), F32)),
        compiler_params=_cparams(("parallel",)),
    )(do, o)


def _fa_bwd_dq(q, k, v, do, lse, delta, dec_row, *, unit, out_dtype, name):
    S, W = q.shape
    H = W // LANES
    T = min(ATTN_TILE, S)
    n, reps = S // T, T // LANES
    qi, kj, npairs = _pairs(n, by_key=False)
    has_dec = dec_row is not None

    def body(qi_ref, kj_ref, *refs):
        if has_dec:
            q_ref, k_ref, v_ref, do_ref, lse_ref, dl_ref, dr_ref, dq_ref, dd_ref, acc, dacc = refs
        else:
            q_ref, k_ref, v_ref, do_ref, lse_ref, dl_ref, dq_ref, acc = refs
        t = pl.program_id(1)
        i, j = qi_ref[t], kj_ref[t]

        @pl.when(j == 0)
        def _():
            acc[...] = jnp.zeros_like(acc)
            if has_dec:
                dacc[...] = jnp.zeros_like(dacc)

        def step(diag):
            kv = k_ref[...]
            s = lax.dot_general(q_ref[...], kv, _NT, preferred_element_type=F32)
            if has_dec:
                s = s - dr_ref[...]
            if diag:
                s = jnp.where(_unit_mask((T, T), unit, 1), s, NEG_INF)
            p = jnp.exp(s - jnp.tile(lse_ref[...], (1, reps)))
            dp = lax.dot_general(do_ref[...], v_ref[...], _NT, preferred_element_type=F32)
            ds = p * (dp - jnp.tile(dl_ref[...], (1, reps)))
            acc[...] += jnp.dot(ds.astype(BF16), kv, preferred_element_type=F32)
            if has_dec:
                dacc[...] += _lane_sum(ds)

        @pl.when(j < i)
        def _():
            step(False)

        @pl.when(j == i)
        def _():
            step(True)
            dq_ref[...] = acc[...].astype(out_dtype)
            if has_dec:
                dd_ref[...] = jnp.broadcast_to(jnp.sum(dacc[...], axis=1, keepdims=True), (T, LANES))

    qb = pl.BlockSpec((T, LANES), lambda h, t, qi, kj: (qi[t], h))
    kb = pl.BlockSpec((T, LANES), lambda h, t, qi, kj: (kj[t], h))
    repq = pl.BlockSpec((None, T, LANES), lambda h, t, qi, kj: (h, qi[t], 0))
    rowk = pl.BlockSpec((None, 1, T), lambda h, t, qi, kj: (h, 0, kj[t]))
    in_specs = [qb, kb, kb, qb, repq, repq] + ([rowk] if has_dec else [])
    args = (q, k, v, do, lse, delta) + ((dec_row,) if has_dec else ())
    out_shape = jax.ShapeDtypeStruct((S, W), out_dtype)
    res = pl.pallas_call(
        body, name=name,
        grid_spec=pltpu.PrefetchScalarGridSpec(num_scalar_prefetch=2, grid=(H, npairs), in_specs=in_specs,
                                               out_specs=(qb, repq) if has_dec else qb,
                                               scratch_shapes=[pltpu.VMEM((T, LANES), F32)] * (2 if has_dec else 1)),
        out_shape=(out_shape, jax.ShapeDtypeStruct((H, S, LANES), F32)) if has_dec else out_shape,
        compiler_params=_cparams(("parallel", "arbitrary")),
    )(qi, kj, *args)
    return res if has_dec else (res, None)


def _fa_bwd_dkv(q, k, v, do, lse_row, delta_row, dec_rep, *, unit, dk_dtype, name):
    S, W = q.shape
    H = W // LANES
    T = min(ATTN_TILE, S)
    n, reps = S // T, T // LANES
    qi, kj, npairs = _pairs(n, by_key=True)
    has_dec = dec_rep is not None

    def body(qi_ref, kj_ref, *refs):
        if has_dec:
            q_ref, k_ref, v_ref, do_ref, lse_ref, dl_ref, dc_ref, dk_ref, dv_ref, dd_ref, kacc, vacc, dacc = refs
        else:
            q_ref, k_ref, v_ref, do_ref, lse_ref, dl_ref, dk_ref, dv_ref, kacc, vacc = refs
        t = pl.program_id(1)
        i, j = qi_ref[t], kj_ref[t]

        @pl.when(i == j)
        def _():
            kacc[...] = jnp.zeros_like(kacc)
            vacc[...] = jnp.zeros_like(vacc)
            if has_dec:
                dacc[...] = jnp.zeros_like(dacc)

        def step(diag):
            qv, dov = q_ref[...], do_ref[...]
            st = lax.dot_general(k_ref[...], qv, _NT, preferred_element_type=F32)
            if has_dec:
                st = st - jnp.tile(dc_ref[...], (1, reps))
            if diag:
                st = jnp.where(_unit_mask((T, T), unit, 0), st, NEG_INF)
            pt = jnp.exp(st - lse_ref[...])
            dpt = lax.dot_general(v_ref[...], dov, _NT, preferred_element_type=F32)
            dst = pt * (dpt - dl_ref[...])
            vacc[...] += jnp.dot(pt.astype(BF16), dov, preferred_element_type=F32)
            kacc[...] += jnp.dot(dst.astype(BF16), qv, preferred_element_type=F32)
            if has_dec:
                dacc[...] += _lane_sum(dst)

        @pl.when(i == j)
        def _():
            step(True)

        @pl.when(i > j)
        def _():
            step(False)

        @pl.when(i == n - 1)
        def _():
            dk_ref[...] = kacc[...].astype(dk_dtype)
            dv_ref[...] = vacc[...].astype(BF16)
            if has_dec:
                dd_ref[...] = jnp.broadcast_to(-jnp.sum(dacc[...], axis=1, keepdims=True), (T, LANES))

    qb = pl.BlockSpec((T, LANES), lambda h, t, qi, kj: (qi[t], h))
    kb = pl.BlockSpec((T, LANES), lambda h, t, qi, kj: (kj[t], h))
    rowq = pl.BlockSpec((None, 1, T), lambda h, t, qi, kj: (h, 0, qi[t]))
    repk = pl.BlockSpec((None, T, LANES), lambda h, t, qi, kj: (h, kj[t], 0))
    in_specs = [qb, kb, kb, qb, rowq, rowq] + ([repk] if has_dec else [])
    args = (q, k, v, do, lse_row, delta_row) + ((dec_rep,) if has_dec else ())
    scratch = [pltpu.VMEM((T, LANES), F32)] * (3 if has_dec else 2)
    out_specs = (kb, kb) + ((repk,) if has_dec else ())
    out_shape = (jax.ShapeDtypeStruct((S, W), dk_dtype), jax.ShapeDtypeStruct((S, W), BF16))
    if has_dec:
        out_shape = out_shape + (jax.ShapeDtypeStruct((H, S, LANES), F32),)
    res = pl.pallas_call(
        body, name=name,
        grid_spec=pltpu.PrefetchScalarGridSpec(num_scalar_prefetch=2, grid=(H, npairs), in_specs=in_specs,
                                               out_specs=out_specs, scratch_shapes=scratch),
        out_shape=out_shape, compiler_params=_cparams(("parallel", "arbitrary")),
    )(qi, kj, *args)
    return res if has_dec else (res[0], res[1], None)


_TN = (((0,), (0,)), ((), ()))


def _fa_bwd_fused(q, k, v, do, o, lse, dec_row, *, unit, dq_dtype, dk_dtype, name):
    S, W = q.shape
    H = W // LANES
    T = min(ATTN_TILE, S)
    n, hT = S // T, T // ROW_SPLIT_BWD
    qi, kj, npairs = _pairs(n, by_key=False)
    has_dec = dec_row is not None

    def body(qi_ref, kj_ref, *refs):
        if has_dec:
            (q_ref, k_ref, v_ref, do_ref, o_ref, lse_ref, dr_ref, dq_ref, dk_ref, dv_ref, ddq_ref, ddk_ref,
             qacc, kacc, vacc, dl_ref, rsum, csum) = refs
        else:
            q_ref, k_ref, v_ref, do_ref, o_ref, lse_ref, dq_ref, dk_ref, dv_ref, qacc, kacc, vacc, dl_ref = refs
        t = pl.program_id(1)
        i, j = qi_ref[t], kj_ref[t]

        @pl.when(t == 0)
        def _():
            kacc[...] = jnp.zeros_like(kacc)
            vacc[...] = jnp.zeros_like(vacc)
            if has_dec:
                csum[...] = jnp.zeros_like(csum)

        @pl.when(j == 0)
        def _():
            qacc[...] = jnp.zeros_like(qacc)
            delta = jnp.sum(do_ref[...].astype(F32) * o_ref[...].astype(F32), axis=1, keepdims=True)
            dl_ref[...] = jnp.broadcast_to(delta, (T, LANES))
            if has_dec:
                rsum[...] = jnp.zeros_like(rsum)

        def step(diag):
            for r in range(ROW_SPLIT_BWD):
                rows = slice(r * hT, (r + 1) * hT)
                nk = (r + 1) * hT if diag else T
                qv, dov, kv = q_ref[rows, :], do_ref[rows, :], k_ref[0:nk, :]
                s = lax.dot_general(qv, kv, _NT, preferred_element_type=F32)
                if has_dec:
                    s = s - dr_ref[:, 0:nk]
                if diag:
                    s = jnp.where(_unit_mask((hT, nk), unit, 1, r * hT), s, NEG_INF)
                p = jnp.exp(s - jnp.tile(lse_ref[rows, :], (1, nk // LANES)))
                dp = lax.dot_general(dov, v_ref[0:nk, :], _NT, preferred_element_type=F32)
                ds = p * (dp - jnp.tile(dl_ref[rows, :], (1, nk // LANES)))
                pb, dsb = p.astype(BF16), ds.astype(BF16)
                qacc[rows, :] += jnp.dot(dsb, kv, preferred_element_type=F32)
                vacc[j, 0:nk, :] += lax.dot_general(pb, dov, _TN, preferred_element_type=F32)
                kacc[j, 0:nk, :] += lax.dot_general(dsb, qv, _TN, preferred_element_type=F32)
                if has_dec:
                    rsum[rows, :] += _lane_sum(ds)
                    csum[j, :, 0:nk] -= jnp.sum(ds, axis=0, keepdims=True)

        @pl.when(j < i)
        def _():
            step(False)

        @pl.when(j == i)
        def _():
            step(True)
            dq_ref[...] = qacc[...].astype(dq_dtype)
            if has_dec:
                ddq_ref[...] = jnp.broadcast_to(jnp.sum(rsum[...], axis=1, keepdims=True), (T, LANES))

        @pl.when(t == npairs - 1)
        def _():
            for jj in range(n):
                dk_ref[jj * T:(jj + 1) * T, :] = kacc[jj].astype(dk_dtype)
                dv_ref[jj * T:(jj + 1) * T, :] = vacc[jj].astype(BF16)
                if has_dec:
                    ddk_ref[:, jj * T:(jj + 1) * T] = csum[jj]

    qb = pl.BlockSpec((T, LANES), lambda h, t, qi, kj: (qi[t], h))
    kb = pl.BlockSpec((T, LANES), lambda h, t, qi, kj: (kj[t], h))
    head = pl.BlockSpec((S, LANES), lambda h, t, qi, kj: (0, h))
    repq = pl.BlockSpec((None, T, LANES), lambda h, t, qi, kj: (h, qi[t], 0))
    rowk = pl.BlockSpec((None, 1, T), lambda h, t, qi, kj: (h, 0, kj[t]))
    rowh = pl.BlockSpec((None, 1, S), lambda h, t, qi, kj: (h, 0, 0))
    in_specs = [qb, kb, kb, qb, qb, repq] + ([rowk] if has_dec else [])
    args = (q, k, v, do, o, lse) + ((dec_row,) if has_dec else ())
    out_specs = [qb, head, head] + ([repq, rowh] if has_dec else [])
    out_shape = [jax.ShapeDtypeStruct((S, W), dq_dtype), jax.ShapeDtypeStruct((S, W), dk_dtype), jax.ShapeDtypeStruct((S, W), BF16)]
    scratch = [pltpu.VMEM((T, LANES), F32), pltpu.VMEM((n, T, LANES), F32), pltpu.VMEM((n, T, LANES), F32),
               pltpu.VMEM((T, LANES), F32)]
    if has_dec:
        out_shape += [jax.ShapeDtypeStruct((H, S, LANES), F32), jax.ShapeDtypeStruct((H, 1, S), F32)]
        scratch += [pltpu.VMEM((T, LANES), F32), pltpu.VMEM((n, 1, T), F32)]
    res = pl.pallas_call(
        body, name=name,
        grid_spec=pltpu.PrefetchScalarGridSpec(num_scalar_prefetch=2, grid=(H, npairs), in_specs=in_specs,
                                               out_specs=tuple(out_specs), scratch_shapes=scratch),
        out_shape=tuple(out_shape),
        compiler_params=pltpu.CompilerParams(dimension_semantics=("parallel", "arbitrary"), vmem_limit_bytes=FUSED_BWD_VMEM_BYTES),
    )(qi, kj, *args)
    return res if has_dec else (res[0], res[1], res[2], None, None)


FUSED_BWD_VMEM_BYTES = 58 * 1024 * 1024


def _fa_bwd(q, k, v, o, lse, lse_row, do, dec_row, dec_rep, *, unit, dq_dtype, dk_dtype, name):
    dq, dk, dv, dd_q, dd_k = _fa_bwd_fused(q, k, v, do, o, lse, dec_row, unit=unit, dq_dtype=dq_dtype, dk_dtype=dk_dtype, name=name)
    if dd_k is None:
        return dq, dk, dv, None
    return dq, dk, dv, jnp.max(dd_q, axis=2) + dd_k.reshape(dd_k.shape[0], dd_k.shape[2])


def _fa_bwd_split(q, k, v, o, lse, lse_row, do, dec_row, dec_rep, *, unit, dq_dtype, dk_dtype, name):
    delta, delta_row = _fa_delta(do, o)
    dq, dd_q = _fa_bwd_dq(q, k, v, do, lse, delta, dec_row, unit=unit, out_dtype=dq_dtype, name=name + "_dq")
    one = lambda t: jnp.max(t, axis=2)
    dk, dv, dd_k = _fa_bwd_dkv(q, k, v, do, lse_row, delta_row, dec_rep, unit=unit, dk_dtype=dk_dtype, name=name + "_dkv")
    return dq, dk, dv, (None if dd_k is None else one(dd_q) + one(dd_k))


def _merge_fwd(ya, yb, yc, gate_logit, gate_b):
    S, D = ya.shape
    tm = min(256, S)

    def body(a_ref, b_ref, c_ref, gl_ref, gb_ref, o_ref):
        g = jax.nn.sigmoid(gl_ref[...] + gb_ref[...])
        o_ref[...] = (g[:, 0:D] * a_ref[...] + g[:, D:2 * D] * b_ref[...] + g[:, 2 * D:3 * D] * c_ref[...]).astype(BF16)

    row = pl.BlockSpec((tm, D), lambda i: (i, 0))
    return pl.pallas_call(
        body, name="merge_fwd", grid=(S // tm,),
        in_specs=[row, row, row, pl.BlockSpec((tm, 3 * D), lambda i: (i, 0)), pl.BlockSpec((1, 3 * D), lambda i: (0, 0))],
        out_specs=row, out_shape=jax.ShapeDtypeStruct((S, D), BF16), compiler_params=_cparams(("parallel",)),
    )(ya, yb, yc, gate_logit, gate_b.reshape(1, 3 * D))


def _merge_bwd(dm, ya, yb, yc, gate_logit, gate_b):
    S, D = ya.shape
    tm = min(256, S)

    def body(dm_ref, a_ref, b_ref, c_ref, gl_ref, gb_ref, da_ref, db_ref, dc_ref, dgl_ref, dgb_ref):
        g = jax.nn.sigmoid(gl_ref[...] + gb_ref[...])
        dmv = dm_ref[...]
        parts = []
        for n, (y_ref, dy_ref) in enumerate(((a_ref, da_ref), (b_ref, db_ref), (c_ref, dc_ref))):
            gn = g[:, n * D:(n + 1) * D]
            dy_ref[...] = (dmv * gn).astype(BF16)
            parts.append(dmv * y_ref[...] * gn * (1.0 - gn))
        dgl = jnp.concatenate(parts, axis=1)
        dgl_ref[...] = dgl.astype(BF16)

        @pl.when(pl.program_id(0) == 0)
        def _():
            dgb_ref[...] = jnp.zeros_like(dgb_ref)

        dgb_ref[...] += jnp.sum(dgl, axis=0, keepdims=True)

    row = pl.BlockSpec((tm, D), lambda i: (i, 0))
    wide = pl.BlockSpec((tm, 3 * D), lambda i: (i, 0))
    vec = pl.BlockSpec((1, 3 * D), lambda i: (0, 0))
    act = jax.ShapeDtypeStruct((S, D), BF16)
    da, db, dc, dgl, dgb = pl.pallas_call(
        body, name="merge_bwd", grid=(S // tm,), in_specs=[row, row, row, row, wide, vec],
        out_specs=(row, row, row, wide, vec),
        out_shape=(act, act, act, jax.ShapeDtypeStruct((S, 3 * D), BF16), jax.ShapeDtypeStruct((1, 3 * D), F32)),
        compiler_params=_cparams(("arbitrary",)),
    )(dm, ya, yb, yc, gate_logit, gate_b.reshape(1, 3 * D))
    return da, db, dc, dgl, dgb.reshape(3 * D)


def _swiglu_fwd(hf):
    S, W2 = hf.shape
    F = W2 // 2
    tm = min(128, S)

    def body(h_ref, o_ref):
        gt, up = h_ref[:, 0:F], h_ref[:, F:W2]
        o_ref[...] = (gt * jax.nn.sigmoid(gt) * up).astype(BF16)

    return pl.pallas_call(
        body, name="swiglu_fwd", grid=(S // tm,), in_specs=[pl.BlockSpec((tm, W2), lambda i: (i, 0))],
        out_specs=pl.BlockSpec((tm, F), lambda i: (i, 0)), out_shape=jax.ShapeDtypeStruct((S, F), BF16),
        compiler_params=_cparams(("parallel",)),
    )(hf)


def _swiglu_bwd(dact, hf):
    S, W2 = hf.shape
    F = W2 // 2
    tm = min(128, S)

    def body(d_ref, h_ref, o_ref):
        gt, up = h_ref[:, 0:F], h_ref[:, F:W2]
        sg = jax.nn.sigmoid(gt)
        dv = d_ref[...]
        o_ref[:, 0:F] = (dv * up * sg * (1.0 + gt * (1.0 - sg))).astype(BF16)
        o_ref[:, F:W2] = (dv * gt * sg).astype(BF16)

    return pl.pallas_call(
        body, name="swiglu_bwd", grid=(S // tm,),
        in_specs=[pl.BlockSpec((tm, F), lambda i: (i, 0)), pl.BlockSpec((tm, W2), lambda i: (i, 0))],
        out_specs=pl.BlockSpec((tm, W2), lambda i: (i, 0)), out_shape=jax.ShapeDtypeStruct((S, W2), BF16),
        compiler_params=_cparams(("parallel",)),
    )(dact, hf)


def _ple_fwd(x, pre, e, g_next):
    S, D = x.shape
    tm = _rows(S)
    with_norm = g_next is not None

    def body(*refs):
        x_ref, p_ref, e_ref = refs[:3]
        xn = x_ref[...] + jax.nn.sigmoid(p_ref[...]) * e_ref[...]
        if with_norm:
            g_ref, o_ref, h_ref = refs[3:]
            rstd = lax.rsqrt(jnp.mean(xn * xn, axis=1, keepdims=True) + EPS)
            h_ref[...] = (xn * rstd * g_ref[...]).astype(BF16)
        else:
            o_ref = refs[3]
        o_ref[...] = xn

    row = pl.BlockSpec((tm, D), lambda i: (i, 0))
    xs = jax.ShapeDtypeStruct((S, D), F32)
    if not with_norm:
        return pl.pallas_call(body, name="ple_fwd_last", grid=(S // tm,), in_specs=[row, row, row], out_specs=row,
                              out_shape=xs, compiler_params=_cparams(("parallel",)))(x, pre, e), None
    return pl.pallas_call(body, name="ple_fwd", grid=(S // tm,), in_specs=[row, row, row, pl.BlockSpec((1, D), lambda i: (0, 0))],
                          out_specs=(row, row), out_shape=(xs, jax.ShapeDtypeStruct((S, D), BF16)),
                          compiler_params=_cparams(("parallel",)))(x, pre, e, g_next.reshape(1, D))


def _ple_bwd(dx, pre, e):
    S, D = dx.shape
    tm = _rows(S)

    def body(dx_ref, p_ref, e_ref, dp_ref, de_ref):
        pg = jax.nn.sigmoid(p_ref[...])
        dxv = dx_ref[...]
        dp_ref[...] = (dxv * e_ref[...] * pg * (1.0 - pg)).astype(BF16)
        de_ref[...] = (dxv * pg).astype(BF16)

    row = pl.BlockSpec((tm, D), lambda i: (i, 0))
    act = jax.ShapeDtypeStruct((S, D), BF16)
    return pl.pallas_call(body, name="ple_bwd", grid=(S // tm,), in_specs=[row, row, row], out_specs=(row, row),
                          out_shape=(act, act), compiler_params=_cparams(("parallel",)))(dx, pre, e)


def _pad_heads(w, real):
    K = w.shape[0]
    w = w.reshape(K, HEADS, real)
    return jnp.pad(w, ((0, 0), (0, 0), (0, HEAD_PAD - real))).reshape(K, HEADS * HEAD_PAD)


def _unpad_heads(w, real):
    K = w.shape[0]
    return w.reshape(K, HEADS, HEAD_PAD)[:, :, :real].reshape(K, HEADS * real)


def _pad_head_rows(w, real):
    N = w.shape[1]
    w = w.reshape(HEADS, real, N)
    return jnp.pad(w, ((0, 0), (0, HEAD_PAD - real), (0, 0))).reshape(HEADS * HEAD_PAD, N)


def _unpad_head_rows(w, real):
    N = w.shape[1]
    return w.reshape(HEADS, HEAD_PAD, N)[:, :real].reshape(HEADS * real, N)


def _block_diag(w):
    w = w.reshape(4, 2, 64, 64)
    z = jnp.zeros((4, 64, 64), w.dtype)
    top = jnp.concatenate([w[:, 0], z], axis=2)
    bot = jnp.concatenate([z, w[:, 1]], axis=2)
    return jnp.concatenate([top, bot], axis=1)


def _block_diag_t(w):
    return jnp.stack([w[:, :64, :64], w[:, 64:, 64:]], axis=1).reshape(8, 64, 64)


_IN_SPLITS = (512, 512, 384, 288, 512, 512, 512, 8, 3072)
_IN_OFF = np.concatenate([[0], np.cumsum(_IN_SPLITS)])
_KR_OFF = 64
_SEG_NAMES = ("u", "ug", "cq", "ckv", "kr", "fq", "fk", "fv", "fl", "gate")


def _in_segments(w_in):
    c = lambda n: w_in[:, int(_IN_OFF[n]):int(_IN_OFF[n + 1])]
    kv = c(3)
    kr = jnp.pad(kv[:, MLA_KV_LORA:], ((0, 0), (_KR_OFF, LANES - _KR_OFF - MLA_ROPE)))
    fl = jnp.pad(c(7), ((0, 0), (0, LANES - HEADS)))
    fq = _pad_heads(c(4), FOX_HEAD_DIM) * jnp.asarray(FOX_SCALE, w_in.dtype)
    return [c(0), c(1), c(2), kv[:, :MLA_KV_LORA], kr, fq, _pad_heads(c(5), FOX_HEAD_DIM), _pad_heads(c(6), FOX_HEAD_DIM), fl, c(8)]


def _in_unsegment(dw_p, widths):
    offs = np.concatenate([[0], np.cumsum(widths)])
    seg = [dw_p[:, int(offs[n]):int(offs[n + 1])] for n in range(len(widths))]
    u, ug, cq, ckv, kr, fq, fk, fv, fl, gate = seg
    return jnp.concatenate([
        u, ug, cq, ckv, kr[:, _KR_OFF:_KR_OFF + MLA_ROPE], _unpad_heads(fq, FOX_HEAD_DIM) * FOX_SCALE,
        _unpad_heads(fk, FOX_HEAD_DIM), _unpad_heads(fv, FOX_HEAD_DIM), fl[:, :HEADS], gate], axis=1)


def _split_wuq(wuq):
    return _pad_heads(wuq, MLA_NOPE + MLA_ROPE)


def _split_wukv(wukv):
    w = wukv.reshape(MLA_KV_LORA, HEADS, MLA_NOPE + MLA_V)
    pad = lambda t: jnp.pad(t, ((0, 0), (0, 0), (0, HEAD_PAD - t.shape[2]))).reshape(MLA_KV_LORA, HEADS * HEAD_PAD)
    return pad(w[:, :, :MLA_NOPE]), pad(w[:, :, MLA_NOPE:])


def _merge_wukv(dk_p, dv_p):
    k = dk_p.reshape(MLA_KV_LORA, HEADS, HEAD_PAD)[:, :, :MLA_NOPE]
    v = dv_p.reshape(MLA_KV_LORA, HEADS, HEAD_PAD)[:, :, :MLA_V]
    return jnp.concatenate([k, v], axis=2).reshape(MLA_KV_LORA, HEADS * (MLA_NOPE + MLA_V))


def _heads_layout(d):
    S = d.shape[0]
    t = d[:, :HEADS].T
    return t.reshape(HEADS, 1, S), jnp.broadcast_to(t[:, :, None], (HEADS, S, LANES))


def _layer_fwd(x, h, p_i, w, g_next, tabs):
    c_q, c_k, s_lo, s_hi = tabs
    sv = {"x0": x}
    segs = _in_segments(w["w_in"])
    z = {}
    for nm, ws in zip(_SEG_NAMES, segs):
        z[nm] = _mm(h, ws, out_dtype=BF16 if nm in ("fq", "fk", "fv", "gate") else F32, bias=_ones_lane_bias() if nm == "fv" else None,
                    name="in_" + nm)
    sv.update(h=h, z=z)
    wa_bd, wx_bd = _block_diag(w["lru_wa"]).astype(BF16), _block_diag(w["lru_wx"]).astype(BF16)
    oa, xc, hs = _lru_fwd(z["u"], z["ug"], w["conv_w"], w["conv_b"], wa_bd, wx_bd, w["lru_ba"], w["lru_bx"], w["lru_lambda"])
    sv.update(oa=oa, xc=xc, hs=hs)
    qn = _rmsnorm_fwd(z["cq"], w["mla_q_norm"], "q_norm_fwd")
    kvn = _rmsnorm_fwd(z["ckv"], w["mla_kv_norm"], "kv_norm_fwd")
    wuq_p = _split_wuq(w["mla_wuq"])
    wk_p, wv_p = _split_wukv(w["mla_wukv"])
    qb = _rope_q(_mm(qn, wuq_p, name="mla_q"), c_q, s_lo, s_hi, transpose=False, out_dtype=BF16, name="rope_q")
    kb = _rope_k(_mm(kvn, wk_p, name="mla_k"), z["kr"], c_k, s_lo, s_hi)
    vb = _mm(kvn, wv_p, out_dtype=BF16, bias=_ones_lane_bias(), name="mla_v")
    ob, lse_b, lrow_b = _fa_fwd(qb, kb, vb, None, unit=64, name="mla_attn")
    sv.update(qn=qn, kvn=kvn, qb=qb, kb=kb, vb=vb, ob=ob, lse_b=lse_b, lrow_b=lrow_b)
    bf = jnp.pad(w["fox_bf"], (0, LANES - HEADS)).reshape(1, LANES)
    dec = _decay_fwd(z["fl"], bf)
    drow, drep = _heads_layout(dec)
    oc, lse_c, lrow_c = _fa_fwd(z["fq"], z["fk"], z["fv"], drow, unit=1, name="fox_attn")
    sv.update(drow=drow, drep=drep, oc=oc, lse_c=lse_c, lrow_c=lrow_c)
    ya = _mm(oa, w["w_br_a"], out_dtype=BF16, name="br_a")
    yb = _mm(ob, _pad_head_rows(w["w_br_b"], MLA_V), out_dtype=BF16, name="br_b")
    yc = _mm(oc, _pad_head_rows(w["w_br_c"], FOX_HEAD_DIM), out_dtype=BF16, name="br_c")
    merged = _merge_fwd(ya, yb, yc, z["gate"], w["gate_b"])
    x1, hn = _mm_res_norm(merged, w["w_o"], x, w["ffn_norm"], "w_o")
    sv.update(ya=ya, yb=yb, yc=yc, merged=merged, x1=x1)
    hf, act = _ffn_up(hn, _ffn_pair_columns(w["w_gate_up"]))
    x2, pn = _mm_res_norm(act, w["w_down"], x1, w["ple_norm"], "ffn_down")
    sv.update(hn=hn, hf=hf, act=act, x2=x2)
    pre = _mm(pn, w["w_ple_gate"], name="ple_gate")
    e = _mm(p_i, w["w_ple"], name="ple_embed")
    x3, h_next = _ple_fwd(x2, pre, e, g_next)
    sv.update(pn=pn, pre=pre, e=e, p_i=p_i)
    return x3, h_next, sv


def _layer_bwd(dx3, w, sv, tabs):
    c_q, c_k, s_lo, s_hi = tabs
    g = {}
    z = sv["z"]
    dpre, de = _ple_bwd(dx3, sv["pre"], sv["e"])
    g["w_ple"] = _mm(sv["p_i"], de, ta=True, name="d_w_ple")
    g["w_ple_gate"] = _mm(sv["pn"], dpre, ta=True, name="d_w_ple_gate")
    dx2, g["ple_norm"] = _mm_norm_bwd(dpre, w["w_ple_gate"], sv["x2"], w["ple_norm"], dx3, "ple_norm_bwd")
    g["w_down"] = _mm(sv["act"], dx2, ta=True, name="d_w_down")
    dhf = _ffn_down_bwd(dx2, w["w_down"], sv["hf"])
    g["w_gate_up"] = _ffn_unpair_columns(_mm(sv["hn"], dhf, ta=True, name="d_w_gate_up"))
    dx1, g["ffn_norm"] = _mm_norm_bwd(dhf, _ffn_pair_columns(w["w_gate_up"]), sv["x1"], w["ffn_norm"], dx2, "ffn_norm_bwd")
    g["w_o"] = _mm(sv["merged"], dx1, ta=True, name="d_w_o")
    dm = _mm(dx1, w["w_o"], tb=True, name="d_merged")
    dya, dyb, dyc, dgate, g["gate_b"] = _merge_bwd(dm, sv["ya"], sv["yb"], sv["yc"], z["gate"], w["gate_b"])
    wbb_p, wbc_p = _pad_head_rows(w["w_br_b"], MLA_V), _pad_head_rows(w["w_br_c"], FOX_HEAD_DIM)
    g["w_br_a"] = _mm(sv["oa"], dya, ta=True, name="d_w_br_a")
    g["w_br_b"] = _unpad_head_rows(_mm(sv["ob"], dyb, ta=True, name="d_w_br_b"), MLA_V)
    g["w_br_c"] = _unpad_head_rows(_mm(sv["oc"], dyc, ta=True, name="d_w_br_c"), FOX_HEAD_DIM)
    doa = _mm(dya, w["w_br_a"], tb=True, name="d_oa")
    dob = _mm(dyb, wbb_p, tb=True, out_dtype=BF16, name="d_ob")
    doc = _mm(dyc, wbc_p, tb=True, out_dtype=BF16, name="d_oc")
    dfq, dfk, dfv, d_dec = _fa_bwd(z["fq"], z["fk"], z["fv"], sv["oc"], sv["lse_c"], sv["lrow_c"], doc, sv["drow"], sv["drep"],
                                   unit=1, dq_dtype=BF16, dk_dtype=BF16, name="fox_attn_bwd")
    d_dec = jnp.pad(d_dec.T, ((0, 0), (0, LANES - HEADS)))
    bf = jnp.pad(w["fox_bf"], (0, LANES - HEADS)).reshape(1, LANES)
    dfl, dbf = _decay_bwd(d_dec, z["fl"], bf)
    g["fox_bf"] = dbf[0, :HEADS]
    dqb, dkb, dvb, _ = _fa_bwd(sv["qb"], sv["kb"], sv["vb"], sv["ob"], sv["lse_b"], sv["lrow_b"], dob, None, None,
                               unit=64, dq_dtype=F32, dk_dtype=F32, name="mla_attn_bwd")
    wuq_p = _split_wuq(w["mla_wuq"])
    wk_p, wv_p = _split_wukv(w["mla_wukv"])
    dq_pre = _rope_q(dqb, c_q, s_lo, s_hi, transpose=True, out_dtype=BF16, name="rope_q_bwd")
    dkr = _rope_k_bwd(dkb, c_k, s_lo, s_hi)
    g["mla_wuq"] = _unpad_heads(_mm(sv["qn"], dq_pre, ta=True, name="d_wuq"), MLA_NOPE + MLA_ROPE)
    g["mla_wukv"] = _merge_wukv(_mm(sv["kvn"], dkb, ta=True, name="d_wuk"), _mm(sv["kvn"], dvb, ta=True, name="d_wuv"))
    dqn = _mm(dq_pre, wuq_p, tb=True, name="d_qn")
    dkvn = _mm(dvb, wv_p, tb=True, res=_mm(dkb, wk_p, tb=True, name="d_kvn_k"), name="d_kvn")
    dcq, g["mla_q_norm"] = _rmsnorm_bwd(z["cq"], w["mla_q_norm"], dqn, out_dtype=BF16, name="q_norm_bwd")
    dckv, g["mla_kv_norm"] = _rmsnorm_bwd(z["ckv"], w["mla_kv_norm"], dkvn, out_dtype=BF16, name="kv_norm_bwd")
    wa_bd, wx_bd = _block_diag(w["lru_wa"]).astype(BF16), _block_diag(w["lru_wx"]).astype(BF16)
    du, dug, dcw, dcb, dba, dbx, dlam, dwa, dwx = _lru_bwd(
        doa, z["u"], z["ug"], sv["xc"], sv["hs"], w["conv_w"], wa_bd, wx_bd, w["lru_ba"], w["lru_bx"], w["lru_lambda"])
    g["conv_w"], g["conv_b"], g["lru_ba"], g["lru_bx"] = dcw, dcb[0], dba[0], dbx[0]
    g["lru_lambda"] = dlam[0] * LRU_C * jax.nn.sigmoid(-w["lru_lambda"])
    g["lru_wa"], g["lru_wx"] = _block_diag_t(dwa), _block_diag_t(dwx)
    dsegs = [du, dug, dcq, dckv, dkr, dfq, dfk, dfv, dfl, dgate]
    dz = jnp.concatenate(dsegs, axis=1)
    w_in_p = jnp.concatenate(_in_segments(w["w_in"]), axis=1)
    g["w_in"] = _in_unsegment(_mm(sv["h"], dz, ta=True, name="d_w_in"), [d.shape[1] for d in dsegs])
    dx0, g["mix_norm"] = _mm_norm_bwd(dz, w_in_p, sv["x0"], w["mix_norm"], dx1, "mix_norm_bwd")
    return dx0, g


_LAYER_WEIGHTS = ("mix_norm", "w_in", "gate_b", "conv_w", "conv_b", "lru_wa", "lru_ba", "lru_wx", "lru_bx", "lru_lambda",
                  "mla_q_norm", "mla_wuq", "mla_kv_norm", "mla_wukv", "fox_bf", "w_br_a", "w_br_b", "w_br_c", "w_o",
                  "ffn_norm", "w_gate_up", "w_down", "ple_norm", "w_ple_gate", "w_ple")
_BIG = ("w_in", "mla_wuq", "mla_wukv", "w_br_a", "w_br_b", "w_br_c", "w_o", "w_gate_up", "w_down", "w_ple_gate", "w_ple")
_ROW_SHARDED = ("w_o", "w_down", "w_ple_gate")
_SMALL = ("mix_norm", "gate_b", "conv_b", "lru_wa", "lru_ba", "lru_wx", "lru_bx", "lru_lambda", "mla_q_norm", "mla_kv_norm",
          "fox_bf", "ffn_norm", "ple_norm")


def _local_step(x, p, layers, final_norm, target):
    tabs = _rope_tables(x.shape[0])
    saved = []
    h = _rmsnorm_fwd(x, layers[0]["mix_norm"], "mix_norm_fwd")
    for i in range(DEPTH):
        g_next = layers[i + 1]["mix_norm"] if i + 1 < DEPTH else None
        x, h, sv = _layer_fwd(x, h, p[i], layers[i], g_next, tabs)
        saved.append(sv)
    loss, dx, d_final = _loss_head(x, final_norm, target)
    grads = [None] * DEPTH
    for i in reversed(range(DEPTH)):
        dx, grads[i] = _layer_bwd(dx, layers[i], saved[i], tabs)
    return loss, dx, grads, d_final


def _hbm():
    return pl.BlockSpec(memory_space=pltpu.HBM)


def _peers(x, y):
    return [(1 - x, y), (x, 1 - y), (1 - x, 1 - y)]


def _gather_chips_two_level(shard, name):
    R, W = shard.shape
    Rh = R // 2

    def body(src_ref, out_ref, send_sems, recv_sems):
        x, y, c = lax.axis_index("x"), lax.axis_index("y"), lax.axis_index("c")
        me = 2 * x + y
        mine, other = pl.ds(c * Rh, Rh), pl.ds((1 - c) * Rh, Rh)
        peers = _peers(x, y)

        def copy(j, src, slot, rows, to):
            return pltpu.make_async_remote_copy(src_ref=src, dst_ref=out_ref.at[slot, rows], send_sem=send_sems.at[j],
                                                recv_sem=recv_sems.at[j], device_id=to, device_id_type=MESH)

        first = [copy(j, src_ref.at[mine], me, mine, (px, py, c)) for j, (px, py) in enumerate(peers)]
        for cp in first:
            cp.start()
        passed = []
        for j, (px, py) in enumerate(peers):
            slot = 2 * px + py
            copy(j, src_ref.at[mine], slot, mine, (px, py, c)).wait_recv()
            cp = copy(3 + j, out_ref.at[slot, mine], slot, mine, (x, y, 1 - c))
            cp.start()
            passed.append(cp)
        for j, (px, py) in enumerate(peers):
            copy(3 + j, src_ref.at[other], 2 * px + py, other, (x, y, 1 - c)).wait_recv()
        for cp in first + passed:
            cp.wait_send()

    return pl.pallas_call(
        body, name=name, in_specs=[_hbm()], out_specs=_hbm(), out_shape=jax.ShapeDtypeStruct((4, R, W), shard.dtype),
        scratch_shapes=[pltpu.SemaphoreType.DMA((6,)), pltpu.SemaphoreType.DMA((6,))],
    )(shard)


def _gather_weights(arrs, name):
    n_arr = len(arrs)

    def body(*refs):
        srcs, outs = refs[:n_arr], refs[n_arr:2 * n_arr]
        send_sems, recv_sems = refs[2 * n_arr:]
        x, y, c = lax.axis_index("x"), lax.axis_index("y"), lax.axis_index("c")
        me = 2 * x + y
        peers = _peers(x, y)

        def copy(sem, src, dst, to):
            return pltpu.make_async_remote_copy(src_ref=src, dst_ref=dst, send_sem=send_sems.at[sem], recv_sem=recv_sems.at[sem],
                                                device_id=to, device_id_type=MESH)

        started = []
        for a in range(n_arr):
            for j, (px, py) in enumerate(peers):
                cp = copy(6 * a + j, srcs[a].at[c], outs[a].at[me, c], (px, py, c))
                cp.start()
                started.append(cp)
        for a in range(n_arr):
            for j, (px, py) in enumerate(peers):
                landed = outs[a].at[2 * px + py, c]
                copy(6 * a + j, srcs[a].at[c], landed, (px, py, c)).wait_recv()
                cp = copy(6 * a + 3 + j, landed, landed, (x, y, 1 - c))
                cp.start()
                started.append(cp)
        for a in range(n_arr):
            for j, (px, py) in enumerate(peers):
                copy(6 * a + 3 + j, srcs[a].at[1 - c], outs[a].at[2 * px + py, 1 - c], (x, y, 1 - c)).wait_recv()
        for cp in started:
            cp.wait_send()

    return pl.pallas_call(
        body, name=name, in_specs=[_hbm()] * n_arr, out_specs=tuple([_hbm()] * n_arr),
        out_shape=tuple(jax.ShapeDtypeStruct((4,) + t.shape, t.dtype) for t in arrs),
        scratch_shapes=[pltpu.SemaphoreType.DMA((6 * n_arr,)), pltpu.SemaphoreType.DMA((6 * n_arr,))],
    )(*arrs)


def _gather_chips(shard, name):
    R, W = shard.shape

    def body(src_ref, out_ref, send_sems, recv_sems, local_sem):
        x, y, c = lax.axis_index("x"), lax.axis_index("y"), lax.axis_index("c")
        me = 2 * x + y
        mine = pltpu.make_async_copy(src_ref, out_ref.at[me], local_sem)
        mine.start()

        def copy(j, slot, to):
            return pltpu.make_async_remote_copy(src_ref=src_ref, dst_ref=out_ref.at[slot], send_sem=send_sems.at[j],
                                                recv_sem=recv_sems.at[j], device_id=(to[0], to[1], c), device_id_type=MESH)

        sends = [copy(j, me, peer) for j, peer in enumerate(_peers(x, y))]
        for cp in sends:
            cp.start()
        for j, peer in enumerate(_peers(x, y)):
            copy(j, 2 * peer[0] + peer[1], peer).wait_recv()
        for cp in sends:
            cp.wait_send()
        mine.wait()

    return pl.pallas_call(
        body, name=name, in_specs=[_hbm()], out_specs=_hbm(), out_shape=jax.ShapeDtypeStruct((4, R, W), shard.dtype),
        scratch_shapes=[pltpu.SemaphoreType.DMA((3,)), pltpu.SemaphoreType.DMA((3,)), pltpu.SemaphoreType.DMA],
    )(shard)


def _pair_swap_halves(g4):
    n, R, W = g4.shape
    Rh = R // 2

    def body(src_ref, out_ref, send_sem, recv_sem):
        x, y, c = lax.axis_index("x"), lax.axis_index("y"), lax.axis_index("c")
        cp = pltpu.make_async_remote_copy(src_ref=src_ref.at[:, pl.ds((1 - c) * Rh, Rh), :], dst_ref=out_ref, send_sem=send_sem,
                                          recv_sem=recv_sem, device_id=(x, y, 1 - c), device_id_type=MESH)
        cp.start()
        cp.wait()

    return pl.pallas_call(
        body, name="grad_pair_swap", in_specs=[_hbm()], out_specs=_hbm(), out_shape=jax.ShapeDtypeStruct((n, Rh, W), g4.dtype),
        scratch_shapes=[pltpu.SemaphoreType.DMA, pltpu.SemaphoreType.DMA],
    )(g4)


def _pair_add(g4, sib, c_arr):
    n, R, W = g4.shape
    Rh = R // 2
    tr = _tile_rows(Rh)
    nb = Rh // tr

    def body(c_ref, a_ref, b_ref, o_ref):
        o_ref[...] = (a_ref[...].astype(F32) + b_ref[...].astype(F32)).astype(o_ref.dtype)

    return pl.pallas_call(
        body, name="grad_pair_add",
        grid_spec=pltpu.PrefetchScalarGridSpec(
            num_scalar_prefetch=1, grid=(n, nb),
            in_specs=[pl.BlockSpec((None, tr, W), lambda s, i, c: (s, c[0] * nb + i, 0)), pl.BlockSpec((None, tr, W), lambda s, i, c: (s, i, 0))],
            out_specs=pl.BlockSpec((None, tr, W), lambda s, i, c: (s, i, 0))),
        out_shape=jax.ShapeDtypeStruct((n, Rh, W), g4.dtype), compiler_params=_cparams(("parallel", "parallel")),
    )(c_arr, g4, sib)


def _tile_rows(n):
    for t in (512, 480, 400, 320, 256, 240, 160, 128, 80, 64, 40, 32, 16, 8):
        if n % t == 0:
            return t
    return n


def _chips_exchange(part):
    n, Rh, W = part.shape

    def body(src_ref, out_ref, send_sems, recv_sems):
        x, y, c = lax.axis_index("x"), lax.axis_index("y"), lax.axis_index("c")

        def copy(j, to):
            return pltpu.make_async_remote_copy(src_ref=src_ref.at[2 * to[0] + to[1]], dst_ref=out_ref.at[j], send_sem=send_sems.at[j],
                                                recv_sem=recv_sems.at[j], device_id=(to[0], to[1], c), device_id_type=MESH)

        cps = [copy(j, peer) for j, peer in enumerate(_peers(x, y))]
        for cp in cps:
            cp.start()
        for cp in cps:
            cp.wait()

    return pl.pallas_call(
        body, name="grad_chips_exchange", in_specs=[_hbm()], out_specs=_hbm(), out_shape=jax.ShapeDtypeStruct((3, Rh, W), part.dtype),
        scratch_shapes=[pltpu.SemaphoreType.DMA((3,)), pltpu.SemaphoreType.DMA((3,))],
    )(part)


def _chips_add(part, got, k_arr, c_arr):
    n, Rh, W = part.shape
    tr = _tile_rows(Rh)
    nb = Rh // tr

    def body(k_ref, c_ref, a_ref, b_ref, o_ref):
        mine = pl.program_id(0) == c_ref[0]

        @pl.when(mine)
        def _():
            o_ref[...] = ((a_ref[...].astype(F32) + b_ref[0].astype(F32)) + b_ref[1].astype(F32)) + b_ref[2].astype(F32)

        @pl.when(jnp.logical_not(mine))
        def _():
            o_ref[...] = jnp.zeros_like(o_ref)

    return pl.pallas_call(
        body, name="grad_chips_add",
        grid_spec=pltpu.PrefetchScalarGridSpec(
            num_scalar_prefetch=2, grid=(2, nb),
            in_specs=[pl.BlockSpec((None, tr, W), lambda h, i, k, c: (k[0], i, 0)), pl.BlockSpec((3, tr, W), lambda h, i, k, c: (0, i, 0))],
            out_specs=pl.BlockSpec((tr, W), lambda h, i, k, c: (h * nb + i, 0))),
        out_shape=jax.ShapeDtypeStruct((2 * Rh, W), F32), compiler_params=_cparams(("parallel", "parallel")),
    )(k_arr, c_arr, part, got)


def _pair_gather(buf):
    R, W = buf.shape
    Rh = R // 2

    def body(src_ref, out_ref, send_sem, recv_sem):
        x, y, c = lax.axis_index("x"), lax.axis_index("y"), lax.axis_index("c")
        mine, other = pl.ds(c * Rh, Rh), pl.ds((1 - c) * Rh, Rh)
        pltpu.make_async_remote_copy(src_ref=src_ref.at[mine], dst_ref=out_ref.at[mine], send_sem=send_sem, recv_sem=recv_sem,
                                     device_id=(x, y, 1 - c), device_id_type=MESH).start()
        pltpu.make_async_remote_copy(src_ref=src_ref.at[mine], dst_ref=out_ref.at[other], send_sem=send_sem, recv_sem=recv_sem,
                                     device_id=(x, y, 1 - c), device_id_type=MESH).wait()

    return pl.pallas_call(
        body, name="grad_pair_gather", in_specs=[_hbm()], out_specs=_hbm(), out_shape=jax.ShapeDtypeStruct((R, W), buf.dtype),
        input_output_aliases={0: 0}, scratch_shapes=[pltpu.SemaphoreType.DMA, pltpu.SemaphoreType.DMA],
    )(buf)


def _gather_all(buf):
    R, W = buf.shape

    def body(src_ref, out_ref, send_sems, recv_sems, local_sem):
        x, y, c = lax.axis_index("x"), lax.axis_index("y"), lax.axis_index("c")
        me = 4 * x + 2 * y + c
        mine = pltpu.make_async_copy(src_ref, out_ref.at[me], local_sem)
        mine.start()
        rel = [((x + (r >> 2 & 1)) % 2, (y + (r >> 1 & 1)) % 2, (c + (r & 1)) % 2) for r in range(1, 8)]

        def copy(j, slot, to):
            return pltpu.make_async_remote_copy(src_ref=src_ref, dst_ref=out_ref.at[slot], send_sem=send_sems.at[j],
                                                recv_sem=recv_sems.at[j], device_id=to, device_id_type=MESH)

        sends = [copy(j, me, to) for j, to in enumerate(rel)]
        for cp in sends:
            cp.start()
        for j, to in enumerate(rel):
            copy(j, 4 * to[0] + 2 * to[1] + to[2], to).wait_recv()
        for cp in sends:
            cp.wait_send()
        mine.wait()

    return pl.pallas_call(
        body, name="small_gather", in_specs=[_hbm()], out_specs=_hbm(), out_shape=jax.ShapeDtypeStruct((8, R, W), buf.dtype),
        scratch_shapes=[pltpu.SemaphoreType.DMA((7,)), pltpu.SemaphoreType.DMA((7,)), pltpu.SemaphoreType.DMA],
    )(buf)


def _sum_slots(stack):
    n, R, W = stack.shape
    tr = _tile_rows(R)

    def body(s_ref, o_ref):
        tot = s_ref[0]
        for j in range(1, n):
            tot = tot + s_ref[j]
        o_ref[...] = tot

    return pl.pallas_call(
        body, name="small_sum", grid=(R // tr,), in_specs=[pl.BlockSpec((n, tr, W), lambda i: (0, i, 0))],
        out_specs=pl.BlockSpec((tr, W), lambda i: (i, 0)), out_shape=jax.ShapeDtypeStruct((R, W), F32),
        compiler_params=_cparams(("parallel",)),
    )(stack)


def _adamw(wp, gp, mp, vp, name):
    R, W = wp.shape
    tr = R
    for t in (1024, 512, 256, 128, 64, 32, 16, 8):
        if R % t == 0 and t * W <= 512 * 1024:
            tr = t
            break
    c1 = 1.0 - ADAM_B1 ** ADAM_STEP
    c2 = 1.0 - ADAM_B2 ** ADAM_STEP

    def body(w_ref, g_ref, m_ref, v_ref, d_ref, mo_ref, vo_ref):
        gv = g_ref[...]
        m = ADAM_B1 * m_ref[...] + (1.0 - ADAM_B1) * gv
        v = ADAM_B2 * v_ref[...] + (1.0 - ADAM_B2) * (gv * gv)
        m_hat = m / c1
        v_hat = v / c2
        d_ref[...] = -ADAM_LR * (m_hat / (jnp.sqrt(v_hat) + ADAM_EPS) + ADAM_WD * w_ref[...])
        mo_ref[...] = m
        vo_ref[...] = v

    blk = pl.BlockSpec((tr, W), lambda i: (i, 0))
    shp = jax.ShapeDtypeStruct((R, W), F32)
    return pl.pallas_call(body, name=name, grid=(R // tr,), in_specs=[blk] * 4, out_specs=(blk,) * 3, out_shape=(shp,) * 3,
                          compiler_params=_cparams(("parallel",)))(wp, gp, mp, vp)


def _pack(arrs, rows):
    flat = jnp.concatenate([a.reshape(-1) for a in arrs])
    return jnp.pad(flat, (0, rows * PACK_W - flat.shape[0])).reshape(rows, PACK_W)


def _unpack(buf, shapes):
    flat = buf.reshape(-1)
    out, off = [], 0
    for shp in shapes:
        n = int(np.prod(shp))
        out.append(flat[off:off + n].reshape(shp))
        off += n
    return out


def _rows_for(shapes, mult):
    n = sum(int(np.prod(s)) for s in shapes)
    rows = -(-n // PACK_W)
    return -(-rows // mult) * mult


def _shard_major(g, name):
    L, K, N = g.shape
    if name in _ROW_SHARDED:
        t = g.reshape(L, 4, K // 4, N).transpose(1, 0, 2, 3)
    else:
        t = g.reshape(L, K, 4, N // 4).transpose(2, 0, 1, 3)
    return t.reshape(4, -1, PACK_W)


def _join_shards(blocks, name):
    return jnp.concatenate(blocks, axis=1 if name in _ROW_SHARDED else 2)


def kernel(x, p, mix_norm, w_in, gate_b, conv_w, conv_b, lru_wa, lru_ba, lru_wx, lru_bx, lru_lambda, mla_q_norm, mla_wuq, mla_kv_norm, mla_wukv, fox_bf, w_br_a, w_br_b, w_br_c, w_o, ffn_norm, w_gate_up, w_down, ple_norm, w_ple_gate, w_ple, final_norm, loss_target, m_mix_norm, m_w_in, m_gate_b, m_conv_w, m_conv_b, m_lru_wa, m_lru_ba, m_lru_wx, m_lru_bx, m_lru_lambda, m_mla_q_norm, m_mla_wuq, m_mla_kv_norm, m_mla_wukv, m_fox_bf, m_w_br_a, m_w_br_b, m_w_br_c, m_w_o, m_ffn_norm, m_w_gate_up, m_w_down, m_ple_norm, m_w_ple_gate, m_w_ple, m_final_norm, v_mix_norm, v_w_in, v_gate_b, v_conv_w, v_conv_b, v_lru_wa, v_lru_ba, v_lru_wx, v_lru_bx, v_lru_lambda, v_mla_q_norm, v_mla_wuq, v_mla_kv_norm, v_mla_wukv, v_fox_bf, v_w_br_a, v_w_br_b, v_w_br_c, v_w_o, v_ffn_norm, v_w_gate_up, v_w_down, v_ple_norm, v_w_ple_gate, v_w_ple, v_final_norm):
    a = dict(locals())
    names = list(_LAYER_WEIGHTS) + ["final_norm"]
    W = {n: a[n] for n in names}
    M = {n: a["m_" + n] for n in names}
    V = {n: a["v_" + n] for n in names}
    ix, iy, ic = lax.axis_index("x"), lax.axis_index("y"), lax.axis_index("c")

    sharded = list(_BIG) + ["conv_w"]
    shard_shapes = [W[n].shape for n in sharded]
    R = _rows_for(shard_shapes, 64)
    mine = [W[n].astype(BF16) for n in _BIG] + [conv_w]
    gathered = _gather_weights(mine, "weight_gather")
    me = 2 * ix + iy
    gathered = [lax.dynamic_update_slice(g, t[None], (me,) + (0,) * t.ndim) for g, t in zip(gathered, mine)]
    full = {n: _join_shards([g[k] for k in range(4)], n) for n, g in zip(sharded, gathered)}
    conv_w_full = full["conv_w"]
    layers = []
    for i in range(DEPTH):
        lw = {n: W[n][i] for n in _SMALL}
        for n in _BIG:
            lw[n] = full[n][i]
        lw["conv_w"] = conv_w_full[i]
        layers.append(lw)

    loss_sum, dx, grads, d_final = _local_step(x[0], p[:, 0], layers, final_norm, loss_target[0])
    loss = lax.psum(loss_sum, ("x", "y", "c"))

    parts = [_shard_major(jnp.stack([grads[i][n] for i in range(DEPTH)]), n).astype(BF16) for n in sharded]
    g4, off = jnp.zeros((4, R, PACK_W), BF16), 0
    for t in parts:
        g4 = lax.dynamic_update_slice(g4, t, (0, off, 0))
        off += t.shape[1]
    c_arr = jnp.reshape(ic, (1,)).astype(jnp.int32)
    k_arr = jnp.reshape(2 * ix + iy, (1,)).astype(jnp.int32)
    pair = _pair_add(g4, _pair_swap_halves(g4), c_arr)
    g_pack = _pair_gather(_chips_add(pair, _chips_exchange(pair), k_arr, c_arr))
    big_out = {}
    for n, gsh in zip(sharded, _unpack(g_pack, shard_shapes)):
        view = lambda t: t.reshape(-1, t.shape[-1])
        d, nm, nv = _adamw(view(W[n]), view(gsh), view(M[n]), view(V[n]), "adamw_" + n)
        for key, arr in (("g", gsh), ("d", d), ("m", nm), ("v", nv)):
            big_out[(key, n)] = arr.reshape(W[n].shape)

    pick = lambda src, n, i: src[n] if i is None else src[n][i]
    small = [(n, i) for i in range(DEPTH) for n in _SMALL] + [("final_norm", None)]
    small_shapes = [pick(W, n, i).shape for n, i in small]
    Rs = _rows_for(small_shapes, 8)
    sg = _pack([d_final if i is None else grads[i][n] for n, i in small], Rs)
    sg = _sum_slots(_gather_all(sg))
    sw = _pack([pick(W, n, i) for n, i in small], Rs)
    sm = _pack([pick(M, n, i) for n, i in small], Rs)
    sv_ = _pack([pick(V, n, i) for n, i in small], Rs)
    sd, snm, snv = _adamw(sw, sg, sm, sv_, "adamw_replicated")
    small_out = {}
    for key, buf in (("g", sg), ("d", sd), ("m", snm), ("v", snv)):
        for (n, i), arr in zip(small, _unpack(buf, small_shapes)):
            small_out[(key, n, i)] = arr

    def assemble(key, n):
        if n == "final_norm":
            return small_out[(key, n, None)]
        if n in sharded:
            return big_out[(key, n)]
        return jnp.stack([small_out[(key, n, i)] for i in range(DEPTH)])

    outs = [loss, dx[None]]
    for key in ("g", "d", "m", "v"):
        outs += [assemble(key, n) for n in names]
    return tuple(outs)
```

```python
import functools
import math

import numpy as np
import jax
import jax.numpy as jnp
from jax import lax
from jax.experimental import pallas as pl
from jax.experimental.pallas import tpu as pltpu

F32, BF16 = jnp.float32, jnp.bfloat16
MESH = pl.DeviceIdType.MESH

D_MODEL = 1024
DEPTH = 2
EPS = 1e-6
NEG_INF = -1e30
LRU_WIDTH = 512
LRU_HEADS = 8
LRU_C = 8.0
CONV_WIDTH = 4
HEADS = 8
MLA_Q_LORA = 384
MLA_KV_LORA = 256
MLA_NOPE = 64
MLA_ROPE = 32
MLA_V = 64
ROPE_BASE = 10000.0
FOX_HEAD_DIM = 64
D_FF = 2816
PLE_DIM = 256
HEAD_PAD = 128
MLA_SCALE = (MLA_NOPE + MLA_ROPE) ** -0.5
FOX_SCALE = FOX_HEAD_DIM ** -0.5

ADAM_LR, ADAM_B1, ADAM_B2, ADAM_EPS, ADAM_WD, ADAM_STEP = 0.001, 0.9, 0.999, 1e-08, 0.01, 10

VMEM_LIMIT_BYTES = 48 * 1024 * 1024
LANES = 128
PACK_W = 1024

ROW_TILE = 512
ATTN_TILE = 1024
LRU_CHUNK = 512


def _cparams(dims):
    return pltpu.CompilerParams(dimension_semantics=dims, vmem_limit_bytes=VMEM_LIMIT_BYTES)


def _tile(n, cap):
    if n <= cap:
        return n
    t = (cap // LANES) * LANES
    while t >= LANES:
        if n % t == 0:
            return t
        t -= LANES
    raise ValueError(f"no tile for {n} under {cap}")


def _rows(n):
    return min(ROW_TILE, n)


MM_VMEM_BUDGET = 36 * 1024 * 1024


def _mm_tiles(M, N, K, a_bytes, b_bytes, o_bytes, has_res):
    best, best_work = None, 0
    for tm in {_tile(M, c) for c in (1024, 512, 256)}:
        for tn in {_tile(N, c) for c in (1792, 1024, 512)}:
            for tk in {_tile(K, c) for c in (2048, 1408, 1024, 512)}:
                need = 2 * (tm * tk * a_bytes + tk * tn * b_bytes + tm * tn * o_bytes + (tm * tn * 4 if has_res else 0))
                need += tm * tn * 4 if tk < K else 0
                need += tm * tn * 4
                if need <= MM_VMEM_BUDGET and tm * tn * tk > best_work:
                    best, best_work = (tm, tn, tk), tm * tn * tk
    assert best is not None, (M, N, K)
    return best

def _mm(a, b, *, ta=False, tb=False, out_dtype=F32, res=None, bias=None, name):
    K, M = a.shape if ta else a.shape[::-1]
    N, K2 = b.shape if tb else b.shape[::-1]
    assert K == K2, (name, a.shape, b.shape)
    assert res is None or bias is None
    tm, tn, tk = _mm_tiles(M, N, K, a.dtype.itemsize, b.dtype.itemsize, jnp.dtype(out_dtype).itemsize, res is not None)
    nk = K // tk
    a_spec = pl.BlockSpec((tk, tm), lambda i, j, k: (k, i)) if ta else pl.BlockSpec((tm, tk), lambda i, j, k: (i, k))
    b_spec = pl.BlockSpec((tn, tk), lambda i, j, k: (j, k)) if tb else pl.BlockSpec((tk, tn), lambda i, j, k: (k, j))
    o_spec = pl.BlockSpec((tm, tn), lambda i, j, k: (i, j))
    dn = (((0,) if ta else (1,), (1,) if tb else (0,)), ((), ()))
    if bias is not None:
        res, r_spec = bias, pl.BlockSpec((1, tn), lambda i, j, k: (0, j))
    else:
        r_spec = o_spec
    has_res = res is not None

    def body(*refs):
        a_ref, b_ref = refs[0], refs[1]
        r_ref = refs[2] if has_res else None
        o_ref = refs[3] if has_res else refs[2]
        av, bv = a_ref[...], b_ref[...]
        if av.dtype != BF16:
            av = av.astype(BF16)
        if bv.dtype != BF16:
            bv = bv.astype(BF16)
        part = lax.dot_general(av, bv, dn, preferred_element_type=F32)

        def finish(total):
            if has_res:
                total = total + r_ref[...]
            o_ref[...] = total.astype(out_dtype)

        if nk == 1:
            finish(part)
        else:
            acc = refs[-1]
            k = pl.program_id(2)

            @pl.when(k == 0)
            def _():
                acc[...] = part

            @pl.when(k > 0)
            def _():
                acc[...] += part

            @pl.when(k == nk - 1)
            def _():
                finish(acc[...])

    in_specs = [a_spec, b_spec] + ([r_spec] if has_res else [])
    args = (a, b) + ((res,) if has_res else ())
    return pl.pallas_call(
        body, name=name, grid=(M // tm, N // tn, nk), in_specs=in_specs, out_specs=o_spec,
        out_shape=jax.ShapeDtypeStruct((M, N), out_dtype),
        scratch_shapes=[pltpu.VMEM((tm, tn), F32)] if nk > 1 else [],
        compiler_params=_cparams(("parallel", "parallel", "arbitrary")),
    )(*args)


def _mm_res_norm(a, b, res, g, name):
    M, K = a.shape
    N = b.shape[1]
    tm, tk = _tile(M, 512), _tile(K, 1408)
    nk = K // tk

    def body(a_ref, b_ref, r_ref, g_ref, o_ref, h_ref, *scratch):
        part = jnp.dot(a_ref[...], b_ref[...], preferred_element_type=F32)

        def finish(total):
            xn = total + r_ref[...]
            o_ref[...] = xn
            rstd = lax.rsqrt(jnp.mean(xn * xn, axis=1, keepdims=True) + EPS)
            h_ref[...] = (xn * rstd * g_ref[...]).astype(BF16)

        if nk == 1:
            finish(part)
        else:
            acc = scratch[0]
            k = pl.program_id(1)

            @pl.when(k == 0)
            def _():
                acc[...] = part

            @pl.when(k > 0)
            def _():
                acc[...] += part

            @pl.when(k == nk - 1)
            def _():
                finish(acc[...])

    row = pl.BlockSpec((tm, N), lambda i, k: (i, 0))
    return pl.pallas_call(
        body, name=name, grid=(M // tm, nk),
        in_specs=[pl.BlockSpec((tm, tk), lambda i, k: (i, k)), pl.BlockSpec((tk, N), lambda i, k: (k, 0)), row,
                  pl.BlockSpec((1, N), lambda i, k: (0, 0))],
        out_specs=(row, row), out_shape=(jax.ShapeDtypeStruct((M, N), F32), jax.ShapeDtypeStruct((M, N), BF16)),
        scratch_shapes=[pltpu.VMEM((tm, N), F32)] if nk > 1 else [],
        compiler_params=_cparams(("parallel", "arbitrary")),
    )(a, b, res, g.reshape(1, N))


def _mm_norm_bwd(a, b, x, g, add, name):
    M, K = a.shape
    N = b.shape[0]
    tm, tk = _tile(M, 512), _tile(K, 1408)
    nk = K // tk

    def body(a_ref, b_ref, x_ref, g_ref, add_ref, dx_ref, dg_ref, *scratch):
        i, k = pl.program_id(0), pl.program_id(1)
        part = lax.dot_general(a_ref[...], b_ref[...], _NT, preferred_element_type=F32)

        @pl.when(jnp.logical_and(i == 0, k == 0))
        def _():
            dg_ref[...] = jnp.zeros_like(dg_ref)

        def finish(dyv):
            xf = x_ref[...]
            rstd = lax.rsqrt(jnp.mean(xf * xf, axis=1, keepdims=True) + EPS)
            xhat = xf * rstd
            dxh = dyv * g_ref[...]
            dx_ref[...] = rstd * (dxh - xhat * jnp.mean(dxh * xhat, axis=1, keepdims=True)) + add_ref[...]
            dg_ref[...] += jnp.sum(dyv * xhat, axis=0, keepdims=True)

        if nk == 1:
            finish(part)
        else:
            acc = scratch[0]

            @pl.when(k == 0)
            def _():
                acc[...] = part

            @pl.when(k > 0)
            def _():
                acc[...] += part

            @pl.when(k == nk - 1)
            def _():
                finish(acc[...])

    row = pl.BlockSpec((tm, N), lambda i, k: (i, 0))
    vec = pl.BlockSpec((1, N), lambda i, k: (0, 0))
    dx, dg = pl.pallas_call(
        body, name=name, grid=(M // tm, nk),
        in_specs=[pl.BlockSpec((tm, tk), lambda i, k: (i, k)), pl.BlockSpec((N, tk), lambda i, k: (0, k)), row, vec, row],
        out_specs=(row, vec), out_shape=(jax.ShapeDtypeStruct((M, N), F32), jax.ShapeDtypeStruct((1, N), F32)),
        scratch_shapes=[pltpu.VMEM((tm, N), F32)] if nk > 1 else [],
        compiler_params=_cparams(("arbitrary", "arbitrary")),
    )(a, b, x, g.reshape(1, N), add)
    return dx, dg.reshape(N)


FFN_TILE = 1408
FFN_SUBTILES = ((0, 512), (512, 1024), (1024, 1408))


def _ffn_pair_columns(w_gate_up):
    F = w_gate_up.shape[-1] // 2
    parts = []
    for j in range(F // FFN_TILE):
        parts += [w_gate_up[..., j * FFN_TILE:(j + 1) * FFN_TILE], w_gate_up[..., F + j * FFN_TILE:F + (j + 1) * FFN_TILE]]
    return jnp.concatenate(parts, axis=-1)


def _ffn_unpair_columns(dw):
    F = dw.shape[-1] // 2
    n = F // FFN_TILE
    blk = [dw[..., j * FFN_TILE:(j + 1) * FFN_TILE] for j in range(2 * n)]
    return jnp.concatenate(blk[0::2] + blk[1::2], axis=-1)


def _ffn_up(hn, w_pair):
    S, D = hn.shape
    W2 = w_pair.shape[1]
    F, tf = W2 // 2, FFN_TILE
    tm = _rows(S)

    def body(h_ref, w_ref, hf_ref, act_ref):
        hv = h_ref[...]
        for lo, hi in FFN_SUBTILES:
            gt = jnp.dot(hv, w_ref[:, lo:hi], preferred_element_type=F32)
            up = jnp.dot(hv, w_ref[:, tf + lo:tf + hi], preferred_element_type=F32)
            hf_ref[:, lo:hi] = gt.astype(BF16)
            hf_ref[:, tf + lo:tf + hi] = up.astype(BF16)
            act_ref[:, lo:hi] = (gt * jax.nn.sigmoid(gt) * up).astype(BF16)

    return pl.pallas_call(
        body, name="ffn_up", grid=(S // tm, F // tf),
        in_specs=[pl.BlockSpec((tm, D), lambda i, j: (i, 0)), pl.BlockSpec((D, 2 * tf), lambda i, j: (0, j))],
        out_specs=(pl.BlockSpec((tm, 2 * tf), lambda i, j: (i, j)), pl.BlockSpec((tm, tf), lambda i, j: (i, j))),
        out_shape=(jax.ShapeDtypeStruct((S, W2), BF16), jax.ShapeDtypeStruct((S, F), BF16)),
        compiler_params=_cparams(("parallel", "parallel")),
    )(hn, w_pair)


def _ffn_down_bwd(dx, w_down, hf):
    S, D = dx.shape
    F, tf = w_down.shape[0], FFN_TILE
    tm = _rows(S)

    def body(d_ref, w_ref, h_ref, o_ref):
        dv = d_ref[...].astype(BF16)
        for lo, hi in FFN_SUBTILES:
            dact = lax.dot_general(dv, w_ref[lo:hi, :], _NT, preferred_element_type=F32)
            gt, up = h_ref[:, lo:hi].astype(F32), h_ref[:, tf + lo:tf + hi].astype(F32)
            sg = jax.nn.sigmoid(gt)
            o_ref[:, lo:hi] = (dact * up * sg * (1.0 + gt * (1.0 - sg))).astype(BF16)
            o_ref[:, tf + lo:tf + hi] = (dact * gt * sg).astype(BF16)

    pair = pl.BlockSpec((tm, 2 * tf), lambda i, j: (i, j))
    return pl.pallas_call(
        body, name="ffn_down_bwd", grid=(S // tm, F // tf),
        in_specs=[pl.BlockSpec((tm, D), lambda i, j: (i, 0)), pl.BlockSpec((tf, D), lambda i, j: (j, 0)), pair],
        out_specs=pair, out_shape=jax.ShapeDtypeStruct((S, 2 * F), BF16),
        compiler_params=_cparams(("parallel", "parallel")),
    )(dx, w_down, hf)


def _rmsnorm_fwd(x, g, name):
    S, W = x.shape
    tm = _rows(S)

    def body(x_ref, g_ref, o_ref):
        xf = x_ref[...]
        rstd = lax.rsqrt(jnp.mean(xf * xf, axis=1, keepdims=True) + EPS)
        o_ref[...] = (xf * rstd * g_ref[...]).astype(BF16)

    return pl.pallas_call(
        body, name=name, grid=(S // tm,),
        in_specs=[pl.BlockSpec((tm, W), lambda i: (i, 0)), pl.BlockSpec((1, W), lambda i: (0, 0))],
        out_specs=pl.BlockSpec((tm, W), lambda i: (i, 0)),
        out_shape=jax.ShapeDtypeStruct((S, W), BF16), compiler_params=_cparams(("parallel",)),
    )(x, g.reshape(1, W))


def _rmsnorm_bwd(x, g, dy, *, add=None, out_dtype=F32, name):
    S, W = x.shape
    tm = _rows(S)
    has_add = add is not None

    def body(*refs):
        x_ref, g_ref, dy_ref = refs[:3]
        add_ref = refs[3] if has_add else None
        dx_ref, dg_ref = refs[-2], refs[-1]
        xf = x_ref[...]
        rstd = lax.rsqrt(jnp.mean(xf * xf, axis=1, keepdims=True) + EPS)
        xhat = xf * rstd
        dyv = dy_ref[...]
        dxh = dyv * g_ref[...]
        dx = rstd * (dxh - xhat * jnp.mean(dxh * xhat, axis=1, keepdims=True))
        if has_add:
            dx = dx + add_ref[...]
        dx_ref[...] = dx.astype(out_dtype)

        @pl.when(pl.program_id(0) == 0)
        def _():
            dg_ref[...] = jnp.zeros_like(dg_ref)

        dg_ref[...] += jnp.sum(dyv * xhat, axis=0, keepdims=True)

    row = pl.BlockSpec((tm, W), lambda i: (i, 0))
    vec = pl.BlockSpec((1, W), lambda i: (0, 0))
    dx, dg = pl.pallas_call(
        body, name=name, grid=(S // tm,),
        in_specs=[row, vec, row] + ([row] if has_add else []),
        out_specs=(row, vec),
        out_shape=(jax.ShapeDtypeStruct((S, W), out_dtype), jax.ShapeDtypeStruct((1, W), F32)),
        compiler_params=_cparams(("arbitrary",)),
    )(x, g.reshape(1, W), dy, *((add,) if has_add else ()))
    return dx, dg.reshape(W)


def _loss_head(x, g, target):
    S, W = x.shape
    tm = _rows(S)

    def body(x_ref, g_ref, t_ref, loss_ref, dx_ref, dg_ref):
        xf = x_ref[...]
        gv = g_ref[...]
        rstd = lax.rsqrt(jnp.mean(xf * xf, axis=1, keepdims=True) + EPS)
        xhat = xf * rstd
        err = xhat * gv - t_ref[...]
        part = 0.5 * jnp.sum(jnp.mean(err * err, axis=1, keepdims=True), axis=0, keepdims=True)
        dyv = err * (1.0 / W)
        dxh = dyv * gv
        dx_ref[...] = rstd * (dxh - xhat * jnp.mean(dxh * xhat, axis=1, keepdims=True))

        @pl.when(pl.program_id(0) == 0)
        def _():
            dg_ref[...] = jnp.zeros_like(dg_ref)
            loss_ref[...] = jnp.zeros_like(loss_ref)

        dg_ref[...] += jnp.sum(dyv * xhat, axis=0, keepdims=True)
        loss_ref[...] += part

    row = pl.BlockSpec((tm, W), lambda i: (i, 0))
    vec = pl.BlockSpec((1, W), lambda i: (0, 0))
    loss, dx, dg = pl.pallas_call(
        body, name="loss_head", grid=(S // tm,), in_specs=[row, vec, row],
        out_specs=(pl.BlockSpec((1, 1), lambda i: (0, 0)), row, vec),
        out_shape=(jax.ShapeDtypeStruct((1, 1), F32), jax.ShapeDtypeStruct((S, W), F32), jax.ShapeDtypeStruct((1, W), F32)),
        compiler_params=_cparams(("arbitrary",)),
    )(x, g.reshape(1, W), target)
    return loss[0, 0], dx, dg.reshape(W)


def _scan_fwd(a, b, row):
    T = a.shape[0]
    d = 1
    while d < T:
        keep = row >= d
        b = jnp.where(keep, a * pltpu.roll(b, d, axis=0) + b, b)
        a = jnp.where(keep, a * pltpu.roll(a, d, axis=0), a)
        d *= 2
    return a, b


def _scan_bwd(a, b, row):
    T = a.shape[0]
    d = 1
    while d < T:
        keep = row < T - d
        b = jnp.where(keep, a * pltpu.roll(b, T - d, axis=0) + b, b)
        a = jnp.where(keep, a * pltpu.roll(a, T - d, axis=0), a)
        d *= 2
    return a, b


def _expm1(x):
    small = x * (1.0 + x * (0.5 + x * (1.0 / 6 + x * (1.0 / 24 + x * (1.0 / 120 + x * (1.0 / 720 + x * (1.0 / 5040)))))))
    return jnp.where(jnp.abs(x) < 0.25, small, jnp.exp(x) - 1.0)


_GELU_C = math.sqrt(2.0 / math.pi)


def _gelu_and_grad(x):
    inner = _GELU_C * (x + 0.044715 * x * x * x)
    th = jnp.tanh(inner)
    val = 0.5 * x * (1.0 + th)
    grad = 0.5 * (1.0 + th) + 0.5 * x * (1.0 - th * th) * _GELU_C * (1.0 + 3 * 0.044715 * x * x)
    return val, grad


def _lru_gates(xc, wa, wx, ba, bx, lam):
    xcb = xc.astype(BF16)
    r = jax.nn.sigmoid(jnp.dot(xcb, wa, preferred_element_type=F32) + ba)
    ig = jax.nn.sigmoid(jnp.dot(xcb, wx, preferred_element_type=F32) + bx)
    sp = jax.nn.softplus(-lam)
    log_a = -LRU_C * r * sp
    a = jnp.exp(log_a)
    mult = jnp.sqrt(-_expm1(2.0 * log_a))
    return xcb, r, ig, sp, a, mult


def _lru_fwd(u, ug, conv_w, conv_b, wa_bd, wx_bd, ba, bx, lam):
    S, W = u.shape
    T = min(LRU_CHUNK, S)
    nl, nc = W // LANES, S // T

    def body(u_ref, ug_ref, cw_ref, cb_ref, wa_ref, wx_ref, ba_ref, bx_ref, lam_ref, ya_ref, xc_ref, h_ref, prev_u, h_carry):
        c = pl.program_id(1)

        @pl.when(c == 0)
        def _():
            prev_u[...] = jnp.zeros_like(prev_u)
            h_carry[...] = jnp.zeros_like(h_carry)

        uv = u_ref[...]
        row = lax.broadcasted_iota(jnp.int32, (T, LANES), 0)
        row8 = lax.broadcasted_iota(jnp.int32, (8, LANES), 0)
        cw = cw_ref[...]
        xc = cb_ref[...] + uv * cw[3:4, :]
        pv = prev_u[...]
        for k in range(1, CONV_WIDTH):
            us = pltpu.roll(uv, k, axis=0)
            top = jnp.where(row8 < k, pltpu.roll(pv, k, axis=0), us[0:8])
            us = jnp.concatenate([top, us[8:]], axis=0)
            xc = xc + us * cw[3 - k:4 - k, :]
        prev_u[...] = uv[T - 8:T]
        _, r, ig, sp, a, mult = _lru_gates(xc, wa_ref[...], wx_ref[...], ba_ref[...], bx_ref[...], lam_ref[...])
        bb = mult * (ig * xc)
        aa, hh = _scan_fwd(a, bb, row)
        h = hh + aa * h_carry[7:8, :]
        h_carry[...] = h[T - 8:T]
        gl, _ = _gelu_and_grad(ug_ref[...])
        ya_ref[...] = (h * gl).astype(BF16)
        xc_ref[...] = xc
        h_ref[...] = h

    seq = pl.BlockSpec((T, LANES), lambda l, c: (c, l))
    vec = pl.BlockSpec((1, LANES), lambda l, c: (0, l))
    mat = pl.BlockSpec((None, LANES, LANES), lambda l, c: (l, 0, 0))
    return pl.pallas_call(
        body, name="lru_fwd", grid=(nl, nc),
        in_specs=[seq, seq, pl.BlockSpec((CONV_WIDTH, LANES), lambda l, c: (0, l)), vec, mat, mat, vec, vec, vec],
        out_specs=(seq, seq, seq),
        out_shape=(jax.ShapeDtypeStruct((S, W), BF16), jax.ShapeDtypeStruct((S, W), F32), jax.ShapeDtypeStruct((S, W), F32)),
        scratch_shapes=[pltpu.VMEM((8, LANES), F32), pltpu.VMEM((8, LANES), F32)],
        compiler_params=_cparams(("parallel", "arbitrary")),
    )(u, ug, conv_w, conv_b.reshape(1, W), wa_bd, wx_bd, ba.reshape(1, W), bx.reshape(1, W), lam.reshape(1, W))


def _lru_bwd(dya, u, ug, xc, h, conv_w, wa_bd, wx_bd, ba, bx, lam):
    S, W = u.shape
    T = min(LRU_CHUNK, S)
    nl, nc = W // LANES, S // T
    tb8 = T // 8

    def body(dya_ref, u_ref, ug_ref, xc_ref, h_ref, hp_ref, cw_ref, wa_ref, wx_ref, ba_ref, bx_ref, lam_ref,
             du_ref, dug_ref, dcw_ref, dcb_ref, dba_ref, dbx_ref, dlam_ref, dwa_ref, dwx_ref,
             g_next, a_next, dxc_next):
        c = pl.program_id(1)

        @pl.when(c == 0)
        def _():
            g_next[...] = jnp.zeros_like(g_next)
            a_next[...] = jnp.zeros_like(a_next)
            dxc_next[...] = jnp.zeros_like(dxc_next)
            for ref in (dcw_ref, dcb_ref, dba_ref, dbx_ref, dlam_ref, dwa_ref, dwx_ref):
                ref[...] = jnp.zeros_like(ref)

        row = lax.broadcasted_iota(jnp.int32, (T, LANES), 0)
        row8 = lax.broadcasted_iota(jnp.int32, (8, LANES), 0)
        xcv = xc_ref[...]
        wa, wx = wa_ref[...], wx_ref[...]
        xcb, r, ig, sp, a, mult = _lru_gates(xcv, wa, wx, ba_ref[...], bx_ref[...], lam_ref[...])
        gl, dgl = _gelu_and_grad(ug_ref[...])
        dyav = dya_ref[...]
        hv = h_ref[...]
        dug_ref[...] = (dyav * hv * dgl).astype(BF16)
        dh = dyav * gl
        a_up = pltpu.roll(a, T - 1, axis=0)
        a_up = jnp.where(row == T - 1, a_next[0:1, :], a_up)
        prod, gg = _scan_bwd(a_up, dh, row)
        g = gg + prod * g_next[0:1, :]
        h_prev = pltpu.roll(hv, 1, axis=0)
        first_chunk = c == nc - 1
        h_before = jnp.where(first_chunk, 0.0, hp_ref[7:8, :])
        h_prev = jnp.where(row == 0, h_before, h_prev)
        da = g * h_prev
        d_mult = g * (ig * xcv)
        d_ig = g * mult * xcv
        dxc = g * mult * ig
        d_log_a = da * a - d_mult * (a * a) / mult
        d_r = d_log_a * (-LRU_C * sp)
        d_pa = d_r * r * (1.0 - r)
        d_px = d_ig * ig * (1.0 - ig)
        d_pab, d_pxb = d_pa.astype(BF16), d_px.astype(BF16)
        nt = (((1,), (1,)), ((), ()))
        tn = (((0,), (0,)), ((), ()))
        dxc = dxc + lax.dot_general(d_pab, wa, nt, preferred_element_type=F32) + lax.dot_general(d_pxb, wx, nt, preferred_element_type=F32)
        dwa_ref[...] += lax.dot_general(xcb, d_pab, tn, preferred_element_type=F32)
        dwx_ref[...] += lax.dot_general(xcb, d_pxb, tn, preferred_element_type=F32)
        dlam_ref[...] += jnp.sum(d_log_a * r, axis=0, keepdims=True)
        dba_ref[...] += jnp.sum(d_pa, axis=0, keepdims=True)
        dbx_ref[...] += jnp.sum(d_px, axis=0, keepdims=True)
        dcb_ref[...] += jnp.sum(dxc, axis=0, keepdims=True)
        uv = u_ref[...]
        cw = cw_ref[...]
        nxt = dxc_next[...]
        du = dxc * cw[3:4, :]
        dcw_ref[3:4, :] += jnp.sum(uv * dxc, axis=0, keepdims=True)
        for k in range(1, CONV_WIDTH):
            ds = pltpu.roll(dxc, T - k, axis=0)
            bot = jnp.where(row8 >= 8 - k, pltpu.roll(nxt, 8 - k, axis=0), ds[T - 8:T])
            ds = jnp.concatenate([ds[:T - 8], bot], axis=0)
            du = du + ds * cw[3 - k:4 - k, :]
            dcw_ref[3 - k:4 - k, :] += jnp.sum(uv * ds, axis=0, keepdims=True)
        du_ref[...] = du.astype(BF16)
        g_next[...] = g[0:8]
        a_next[...] = a[0:8]
        dxc_next[...] = dxc[0:8]

    seq = pl.BlockSpec((T, LANES), lambda l, c: (nc - 1 - c, l))
    before = pl.BlockSpec((8, LANES), lambda l, c: (jnp.maximum((nc - 1 - c) * tb8 - 1, 0), l))
    vec = pl.BlockSpec((1, LANES), lambda l, c: (0, l))
    cwb = pl.BlockSpec((CONV_WIDTH, LANES), lambda l, c: (0, l))
    mat = pl.BlockSpec((None, LANES, LANES), lambda l, c: (l, 0, 0))
    vshape = jax.ShapeDtypeStruct((1, W), F32)
    mshape = jax.ShapeDtypeStruct((nl, LANES, LANES), F32)
    return pl.pallas_call(
        body, name="lru_bwd", grid=(nl, nc),
        in_specs=[seq, seq, seq, seq, seq, before, cwb, mat, mat, vec, vec, vec],
        out_specs=(seq, seq, cwb, vec, vec, vec, vec, mat, mat),
        out_shape=(jax.ShapeDtypeStruct((S, W), BF16), jax.ShapeDtypeStruct((S, W), BF16),
                   jax.ShapeDtypeStruct((CONV_WIDTH, W), F32), vshape, vshape, vshape, vshape, mshape, mshape),
        scratch_shapes=[pltpu.VMEM((8, LANES), F32)] * 3,
        compiler_params=_cparams(("parallel", "arbitrary")),
    )(dya, u, ug, xc, h, h, conv_w, wa_bd, wx_bd, ba.reshape(1, W), bx.reshape(1, W), lam.reshape(1, W))


def _decay_fwd(f_logit, bf):
    S = f_logit.shape[0]
    T = min(LRU_CHUNK, S)

    def body(f_ref, b_ref, o_ref, carry):
        @pl.when(pl.program_id(0) == 0)
        def _():
            carry[...] = jnp.zeros_like(carry)

        row = lax.broadcasted_iota(jnp.int32, (T, LANES), 0)
        v = jax.nn.log_sigmoid(f_ref[...] + b_ref[...])
        d = 1
        while d < T:
            v = jnp.where(row >= d, v + pltpu.roll(v, d, axis=0), v)
            d *= 2
        v = v + carry[7:8, :]
        carry[...] = v[T - 8:T]
        o_ref[...] = v

    return pl.pallas_call(
        body, name="decay_fwd", grid=(S // T,),
        in_specs=[pl.BlockSpec((T, LANES), lambda c: (c, 0)), pl.BlockSpec((1, LANES), lambda c: (0, 0))],
        out_specs=pl.BlockSpec((T, LANES), lambda c: (c, 0)),
        out_shape=jax.ShapeDtypeStruct((S, LANES), F32), scratch_shapes=[pltpu.VMEM((8, LANES), F32)],
        compiler_params=_cparams(("arbitrary",)),
    )(f_logit, bf)


def _decay_bwd(d_dec, f_logit, bf):
    S = f_logit.shape[0]
    T = min(LRU_CHUNK, S)
    nc = S // T

    def body(dd_ref, f_ref, b_ref, df_ref, db_ref, carry):
        @pl.when(pl.program_id(0) == 0)
        def _():
            carry[...] = jnp.zeros_like(carry)
            db_ref[...] = jnp.zeros_like(db_ref)

        row = lax.broadcasted_iota(jnp.int32, (T, LANES), 0)
        v = dd_ref[...]
        d = 1
        while d < T:
            v = jnp.where(row < T - d, v + pltpu.roll(v, T - d, axis=0), v)
            d *= 2
        v = v + carry[0:1, :]
        carry[...] = v[0:8]
        df = v * jax.nn.sigmoid(-(f_ref[...] + b_ref[...]))
        df_ref[...] = df.astype(BF16)
        db_ref[...] += jnp.sum(df, axis=0, keepdims=True)

    seq = pl.BlockSpec((T, LANES), lambda c: (nc - 1 - c, 0))
    vec = pl.BlockSpec((1, LANES), lambda c: (0, 0))
    return pl.pallas_call(
        body, name="decay_bwd", grid=(nc,), in_specs=[seq, seq, vec], out_specs=(seq, vec),
        out_shape=(jax.ShapeDtypeStruct((S, LANES), BF16), jax.ShapeDtypeStruct((1, LANES), F32)),
        scratch_shapes=[pltpu.VMEM((8, LANES), F32)], compiler_params=_cparams(("arbitrary",)),
    )(d_dec, f_logit, bf)


def _rope_tables(S):
    pos = jnp.arange(S, dtype=F32)
    inv_freq = ROPE_BASE ** (-jnp.arange(0, MLA_ROPE, 2, dtype=F32) / MLA_ROPE)
    ang = pos[:, None] * inv_freq[None, :]
    cos, sin = jnp.cos(ang), jnp.sin(ang)
    half = MLA_ROPE // 2
    z = lambda n: jnp.zeros((S, n), F32)
    c_q = jnp.concatenate([jnp.ones((S, MLA_NOPE), F32), cos, cos, z(HEAD_PAD - MLA_NOPE - MLA_ROPE)], axis=1)
    c_k = jnp.concatenate([z(MLA_NOPE), cos, cos, z(HEAD_PAD - MLA_NOPE - MLA_ROPE)], axis=1)
    s_lo = jnp.concatenate([z(MLA_NOPE), -sin, z(HEAD_PAD - MLA_NOPE - half)], axis=1)
    s_hi = jnp.concatenate([z(MLA_NOPE + half), sin, z(HEAD_PAD - MLA_NOPE - MLA_ROPE)], axis=1)
    return c_q, c_k, s_lo, s_hi


def _rot(v, c, s_lo, s_hi):
    half = MLA_ROPE // 2
    return v * c + pltpu.roll(v, LANES - half, axis=1) * s_lo + pltpu.roll(v, half, axis=1) * s_hi


def _rot_t(dv, c, s_lo, s_hi):
    half = MLA_ROPE // 2
    return dv * c + pltpu.roll(dv * s_lo, half, axis=1) + pltpu.roll(dv * s_hi, LANES - half, axis=1)


def _rope_q(q_pre, c_q, s_lo, s_hi, *, transpose, out_dtype, name):
    S, W = q_pre.shape
    tm = _rows(S)
    fn = _rot_t if transpose else _rot

    def body(q_ref, c_ref, lo_ref, hi_ref, o_ref):
        c, lo, hi = c_ref[...], lo_ref[...], hi_ref[...]
        for hd in range(W // LANES):
            cols = slice(hd * LANES, (hd + 1) * LANES)
            o_ref[:, cols] = fn(q_ref[:, cols] * MLA_SCALE, c, lo, hi).astype(out_dtype)

    blk = pl.BlockSpec((tm, W), lambda i: (i, 0))
    tab = pl.BlockSpec((tm, LANES), lambda i: (i, 0))
    return pl.pallas_call(
        body, name=name, grid=(S // tm,), in_specs=[blk, tab, tab, tab], out_specs=blk,
        out_shape=jax.ShapeDtypeStruct((S, W), out_dtype), compiler_params=_cparams(("parallel",)),
    )(q_pre, c_q, s_lo, s_hi)


def _rope_k(k_pre, k_rope, c_k, s_lo, s_hi):
    S, W = k_pre.shape
    tm = _rows(S)

    def body(k_ref, r_ref, c_ref, lo_ref, hi_ref, o_ref):
        rot = _rot(r_ref[...], c_ref[...], lo_ref[...], hi_ref[...])
        for hd in range(W // LANES):
            cols = slice(hd * LANES, (hd + 1) * LANES)
            o_ref[:, cols] = (k_ref[:, cols] + rot).astype(BF16)

    blk = pl.BlockSpec((tm, W), lambda i: (i, 0))
    tab = pl.BlockSpec((tm, LANES), lambda i: (i, 0))
    return pl.pallas_call(
        body, name="rope_k", grid=(S // tm,), in_specs=[blk, tab, tab, tab, tab], out_specs=blk,
        out_shape=jax.ShapeDtypeStruct((S, W), BF16), compiler_params=_cparams(("parallel",)),
    )(k_pre, k_rope, c_k, s_lo, s_hi)


def _rope_k_bwd(dk, c_k, s_lo, s_hi):
    S, W = dk.shape
    tm = _rows(S)

    def body(dk_ref, c_ref, lo_ref, hi_ref, o_ref):
        tot = dk_ref[:, 0:LANES]
        for hd in range(1, W // LANES):
            tot = tot + dk_ref[:, hd * LANES:(hd + 1) * LANES]
        o_ref[...] = _rot_t(tot, c_ref[...], lo_ref[...], hi_ref[...]).astype(BF16)

    tab = pl.BlockSpec((tm, LANES), lambda i: (i, 0))
    return pl.pallas_call(
        body, name="rope_k_bwd", grid=(S // tm,), in_specs=[pl.BlockSpec((tm, W), lambda i: (i, 0)), tab, tab, tab],
        out_specs=tab, out_shape=jax.ShapeDtypeStruct((S, LANES), BF16), compiler_params=_cparams(("parallel",)),
    )(dk, c_k, s_lo, s_hi)


def _pairs(n, by_key):
    if by_key:
        pr = [(i, j) for j in range(n) for i in range(j, n)]
    else:
        pr = [(i, j) for i in range(n) for j in range(i + 1)]
    return (jnp.asarray(np.array([p[0] for p in pr], np.int32)), jnp.asarray(np.array([p[1] for p in pr], np.int32)), len(pr))


def _unit_mask(shape, unit, key_axis, q_off=0):
    q = lax.broadcasted_iota(jnp.int32, shape, 1 - key_axis) + q_off
    k = lax.broadcasted_iota(jnp.int32, shape, key_axis)
    if unit > 1:
        q, k = q // unit, k // unit
    return q >= k


_NT = (((1,), (1,)), ((), ()))


ONES_LANE = 64
ROW_SPLIT = 1
ROW_SPLIT_BWD = 4


def _ones_lane_bias():
    one = np.zeros((HEADS, HEAD_PAD), np.float32)
    one[:, ONES_LANE] = 1.0
    return jnp.asarray(one.reshape(1, HEADS * HEAD_PAD))


def _lane_sum(t):
    tot = t[:, 0:LANES]
    for c in range(1, t.shape[1] // LANES):
        tot = tot + t[:, c * LANES:(c + 1) * LANES]
    return tot


def _fa_fwd(q, k, v, dec_row, *, unit, name):
    S, W = q.shape
    H = W // LANES
    T = min(ATTN_TILE, S)
    n, hT = S // T, T // ROW_SPLIT
    qi, kj, npairs = _pairs(n, by_key=False)
    has_dec = dec_row is not None

    def body(qi_ref, kj_ref, *refs):
        if has_dec:
            q_ref, k_ref, v_ref, dr_ref, o_ref, lse_ref, m_s, acc = refs
        else:
            q_ref, k_ref, v_ref, o_ref, lse_ref, m_s, acc = refs
        t = pl.program_id(1)
        i, j = qi_ref[t], kj_ref[t]

        @pl.when(j == 0)
        def _():
            m_s[...] = jnp.full_like(m_s, NEG_INF)
            acc[...] = jnp.zeros_like(acc)

        def step(diag):
            for r in range(ROW_SPLIT):
                rows = slice(r * hT, (r + 1) * hT)
                nk = (r + 1) * hT if diag else T
                s = lax.dot_general(q_ref[rows, :], k_ref[0:nk, :], _NT, preferred_element_type=F32)
                if has_dec:
                    s = s - dr_ref[:, 0:nk]
                if diag:
                    s = jnp.where(_unit_mask((hT, nk), unit, 1, r * hT), s, NEG_INF)
                m_prev = m_s[rows, :]
                m_new = jnp.maximum(m_prev, jnp.max(s, axis=1, keepdims=True))
                alpha = jnp.exp(m_prev - m_new)
                p = jnp.exp(s - jnp.tile(m_new, (1, nk // LANES)))
                acc[rows, :] = alpha * acc[rows, :] + jnp.dot(p.astype(BF16), v_ref[0:nk, :], preferred_element_type=F32)
                m_s[rows, :] = m_new

        @pl.when(j < i)
        def _():
            step(False)

        @pl.when(j == i)
        def _():
            step(True)
            av = acc[...]
            l = av[:, ONES_LANE:ONES_LANE + 1]
            lane = lax.broadcasted_iota(jnp.int32, (T, LANES), 1)
            o_ref[...] = jnp.where(lane < ONES_LANE, av / l, 0.0).astype(BF16)
            lse_ref[...] = m_s[...] + jnp.log(l)

    qb = pl.BlockSpec((T, LANES), lambda h, t, qi, kj: (qi[t], h))
    kb = pl.BlockSpec((T, LANES), lambda h, t, qi, kj: (kj[t], h))
    repq = pl.BlockSpec((None, T, LANES), lambda h, t, qi, kj: (h, qi[t], 0))
    rowk = pl.BlockSpec((None, 1, T), lambda h, t, qi, kj: (h, 0, kj[t]))
    in_specs = [qb, kb, kb] + ([rowk] if has_dec else [])
    args = (q, k, v) + ((dec_row,) if has_dec else ())
    return pl.pallas_call(
        body, name=name,
        grid_spec=pltpu.PrefetchScalarGridSpec(
            num_scalar_prefetch=2, grid=(H, npairs), in_specs=in_specs, out_specs=(qb, repq),
            scratch_shapes=[pltpu.VMEM((T, LANES), F32), pltpu.VMEM((T, LANES), F32)]),
        out_shape=(jax.ShapeDtypeStruct((S, W), BF16), jax.ShapeDtypeStruct((H, S, LANES), F32)),
        compiler_params=_cparams(("parallel", "arbitrary")),
    )(qi, kj, *args)


_TN = (((0,), (0,)), ((), ()))


def _fa_bwd_fused(q, k, v, do, o, lse, dec_row, *, unit, dq_dtype, dk_dtype, name):
    S, W = q.shape
    H = W // LANES
    T = min(ATTN_TILE, S)
    n, hT = S // T, T // ROW_SPLIT_BWD
    qi, kj, npairs = _pairs(n, by_key=False)
    has_dec = dec_row is not None

    def body(qi_ref, kj_ref, *refs):
        if has_dec:
            (q_ref, k_ref, v_ref, do_ref, o_ref, lse_ref, dr_ref, dq_ref, dk_ref, dv_ref, ddq_ref, ddk_ref,
             qacc, kacc, vacc, dl_ref, rsum, csum) = refs
        else:
            q_ref, k_ref, v_ref, do_ref, o_ref, lse_ref, dq_ref, dk_ref, dv_ref, qacc, kacc, vacc, dl_ref = refs
        t = pl.program_id(1)
        i, j = qi_ref[t], kj_ref[t]

        @pl.when(t == 0)
        def _():
            kacc[...] = jnp.zeros_like(kacc)
            vacc[...] = jnp.zeros_like(vacc)
            if has_dec:
                csum[...] = jnp.zeros_like(csum)

        @pl.when(j == 0)
        def _():
            qacc[...] = jnp.zeros_like(qacc)
            delta = jnp.sum(do_ref[...].astype(F32) * o_ref[...].astype(F32), axis=1, keepdims=True)
            dl_ref[...] = jnp.broadcast_to(delta, (T, LANES))
            if has_dec:
                rsum[...] = jnp.zeros_like(rsum)

        def step(diag):
            for r in range(ROW_SPLIT_BWD):
                rows = slice(r * hT, (r + 1) * hT)
                nk = (r + 1) * hT if diag else T
                qv, dov, kv = q_ref[rows, :], do_ref[rows, :], k_ref[0:nk, :]
                s = lax.dot_general(qv, kv, _NT, preferred_element_type=F32)
                if has_dec:
                    s = s - dr_ref[:, 0:nk]
                if diag:
                    s = jnp.where(_unit_mask((hT, nk), unit, 1, r * hT), s, NEG_INF)
                p = jnp.exp(s - jnp.tile(lse_ref[rows, :], (1, nk // LANES)))
                dp = lax.dot_general(dov, v_ref[0:nk, :], _NT, preferred_element_type=F32)
                ds = p * (dp - jnp.tile(dl_ref[rows, :], (1, nk // LANES)))
                pb, dsb = p.astype(BF16), ds.astype(BF16)
                qacc[rows, :] += jnp.dot(dsb, kv, preferred_element_type=F32)
                vacc[j, 0:nk, :] += lax.dot_general(pb, dov, _TN, preferred_element_type=F32)
                kacc[j, 0:nk, :] += lax.dot_general(dsb, qv, _TN, preferred_element_type=F32)
                if has_dec:
                    rsum[rows, :] += _lane_sum(ds)
                    csum[j, :, 0:nk] -= jnp.sum(ds, axis=0, keepdims=True)

        @pl.when(j < i)
        def _():
            step(False)

        @pl.when(j == i)
        def _():
            step(True)
            dq_ref[...] = qacc[...].astype(dq_dtype)
            if has_dec:
                ddq_ref[...] = jnp.broadcast_to(jnp.sum(rsum[...], axis=1, keepdims=True), (T, LANES))

        @pl.when(t == npairs - 1)
        def _():
            for jj in range(n):
                dk_ref[jj * T:(jj + 1) * T, :] = kacc[jj].astype(dk_dtype)
                dv_ref[jj * T:(jj + 1) * T, :] = vacc[jj].astype(BF16)
                if has_dec:
                    ddk_ref[:, jj * T:(jj + 1) * T] = csum[jj]

    qb = pl.BlockSpec((T, LANES), lambda h, t, qi, kj: (qi[t], h))
    kb = pl.BlockSpec((T, LANES), lambda h, t, qi, kj: (kj[t], h))
    head = pl.BlockSpec((S, LANES), lambda h, t, qi, kj: (0, h))
    repq = pl.BlockSpec((None, T, LANES), lambda h, t, qi, kj: (h, qi[t], 0))
    rowk = pl.BlockSpec((None, 1, T), lambda h, t, qi, kj: (h, 0, kj[t]))
    rowh = pl.BlockSpec((None, 1, S), lambda h, t, qi, kj: (h, 0, 0))
    in_specs = [qb, kb, kb, qb, qb, repq] + ([rowk] if has_dec else [])
    args = (q, k, v, do, o, lse) + ((dec_row,) if has_dec else ())
    out_specs = [qb, head, head] + ([repq, rowh] if has_dec else [])
    out_shape = [jax.ShapeDtypeStruct((S, W), dq_dtype), jax.ShapeDtypeStruct((S, W), dk_dtype), jax.ShapeDtypeStruct((S, W), BF16)]
    scratch = [pltpu.VMEM((T, LANES), F32), pltpu.VMEM((n, T, LANES), F32), pltpu.VMEM((n, T, LANES), F32),
               pltpu.VMEM((T, LANES), F32)]
    if has_dec:
        out_shape += [jax.ShapeDtypeStruct((H, S, LANES), F32), jax.ShapeDtypeStruct((H, 1, S), F32)]
        scratch += [pltpu.VMEM((T, LANES), F32), pltpu.VMEM((n, 1, T), F32)]
    res = pl.pallas_call(
        body, name=name,
        grid_spec=pltpu.PrefetchScalarGridSpec(num_scalar_prefetch=2, grid=(H, npairs), in_specs=in_specs,
                                               out_specs=tuple(out_specs), scratch_shapes=scratch),
        out_shape=tuple(out_shape),
        compiler_params=pltpu.CompilerParams(dimension_semantics=("parallel", "arbitrary"), vmem_limit_bytes=FUSED_BWD_VMEM_BYTES),
    )(qi, kj, *args)
    return res if has_dec else (res[0], res[1], res[2], None, None)


FUSED_BWD_VMEM_BYTES = 58 * 1024 * 1024


def _fa_bwd(q, k, v, o, lse, do, dec_row, *, unit, dq_dtype, dk_dtype, name):
    dq, dk, dv, dd_q, dd_k = _fa_bwd_fused(q, k, v, do, o, lse, dec_row, unit=unit, dq_dtype=dq_dtype, dk_dtype=dk_dtype, name=name)
    if dd_k is None:
        return dq, dk, dv, None
    return dq, dk, dv, jnp.max(dd_q, axis=2) + dd_k.reshape(dd_k.shape[0], dd_k.shape[2])


def _merge_fwd(ya, yb, yc, gate_logit, gate_b):
    S, D = ya.shape
    tm = min(256, S)

    def body(a_ref, b_ref, c_ref, gl_ref, gb_ref, o_ref):
        g = jax.nn.sigmoid(gl_ref[...] + gb_ref[...])
        o_ref[...] = (g[:, 0:D] * a_ref[...] + g[:, D:2 * D] * b_ref[...] + g[:, 2 * D:3 * D] * c_ref[...]).astype(BF16)

    row = pl.BlockSpec((tm, D), lambda i: (i, 0))
    return pl.pallas_call(
        body, name="merge_fwd", grid=(S // tm,),
        in_specs=[row, row, row, pl.BlockSpec((tm, 3 * D), lambda i: (i, 0)), pl.BlockSpec((1, 3 * D), lambda i: (0, 0))],
        out_specs=row, out_shape=jax.ShapeDtypeStruct((S, D), BF16), compiler_params=_cparams(("parallel",)),
    )(ya, yb, yc, gate_logit, gate_b.reshape(1, 3 * D))


def _merge_bwd(dm, ya, yb, yc, gate_logit, gate_b):
    S, D = ya.shape
    tm = min(256, S)

    def body(dm_ref, a_ref, b_ref, c_ref, gl_ref, gb_ref, da_ref, db_ref, dc_ref, dgl_ref, dgb_ref):
        g = jax.nn.sigmoid(gl_ref[...] + gb_ref[...])
        dmv = dm_ref[...]
        parts = []
        for n, (y_ref, dy_ref) in enumerate(((a_ref, da_ref), (b_ref, db_ref), (c_ref, dc_ref))):
            gn = g[:, n * D:(n + 1) * D]
            dy_ref[...] = (dmv * gn).astype(BF16)
            parts.append(dmv * y_ref[...] * gn * (1.0 - gn))
        dgl = jnp.concatenate(parts, axis=1)
        dgl_ref[...] = dgl.astype(BF16)

        @pl.when(pl.program_id(0) == 0)
        def _():
            dgb_ref[...] = jnp.zeros_like(dgb_ref)

        dgb_ref[...] += jnp.sum(dgl, axis=0, keepdims=True)

    row = pl.BlockSpec((tm, D), lambda i: (i, 0))
    wide = pl.BlockSpec((tm, 3 * D), lambda i: (i, 0))
    vec = pl.BlockSpec((1, 3 * D), lambda i: (0, 0))
    act = jax.ShapeDtypeStruct((S, D), BF16)
    da, db, dc, dgl, dgb = pl.pallas_call(
        body, name="merge_bwd", grid=(S // tm,), in_specs=[row, row, row, row, wide, vec],
        out_specs=(row, row, row, wide, vec),
        out_shape=(act, act, act, jax.ShapeDtypeStruct((S, 3 * D), BF16), jax.ShapeDtypeStruct((1, 3 * D), F32)),
        compiler_params=_cparams(("arbitrary",)),
    )(dm, ya, yb, yc, gate_logit, gate_b.reshape(1, 3 * D))
    return da, db, dc, dgl, dgb.reshape(3 * D)


def _ple_fwd(x, pre, e, g_next):
    S, D = x.shape
    tm = _rows(S)
    with_norm = g_next is not None

    def body(*refs):
        x_ref, p_ref, e_ref = refs[:3]
        xn = x_ref[...] + jax.nn.sigmoid(p_ref[...]) * e_ref[...]
        if with_norm:
            g_ref, o_ref, h_ref = refs[3:]
            rstd = lax.rsqrt(jnp.mean(xn * xn, axis=1, keepdims=True) + EPS)
            h_ref[...] = (xn * rstd * g_ref[...]).astype(BF16)
        else:
            o_ref = refs[3]
        o_ref[...] = xn

    row = pl.BlockSpec((tm, D), lambda i: (i, 0))
    xs = jax.ShapeDtypeStruct((S, D), F32)
    if not with_norm:
        return pl.pallas_call(body, name="ple_fwd_last", grid=(S // tm,), in_specs=[row, row, row], out_specs=row,
                              out_shape=xs, compiler_params=_cparams(("parallel",)))(x, pre, e), None
    return pl.pallas_call(body, name="ple_fwd", grid=(S // tm,), in_specs=[row, row, row, pl.BlockSpec((1, D), lambda i: (0, 0))],
                          out_specs=(row, row), out_shape=(xs, jax.ShapeDtypeStruct((S, D), BF16)),
                          compiler_params=_cparams(("parallel",)))(x, pre, e, g_next.reshape(1, D))


def _ple_bwd(dx, pre, e):
    S, D = dx.shape
    tm = _rows(S)

    def body(dx_ref, p_ref, e_ref, dp_ref, de_ref):
        pg = jax.nn.sigmoid(p_ref[...])
        dxv = dx_ref[...]
        dp_ref[...] = (dxv * e_ref[...] * pg * (1.0 - pg)).astype(BF16)
        de_ref[...] = (dxv * pg).astype(BF16)

    row = pl.BlockSpec((tm, D), lambda i: (i, 0))
    act = jax.ShapeDtypeStruct((S, D), BF16)
    return pl.pallas_call(body, name="ple_bwd", grid=(S // tm,), in_specs=[row, row, row], out_specs=(row, row),
                          out_shape=(act, act), compiler_params=_cparams(("parallel",)))(dx, pre, e)


def _pad_heads(w, real):
    K = w.shape[0]
    w = w.reshape(K, HEADS, real)
    return jnp.pad(w, ((0, 0), (0, 0), (0, HEAD_PAD - real))).reshape(K, HEADS * HEAD_PAD)


def _unpad_heads(w, real):
    K = w.shape[0]
    return w.reshape(K, HEADS, HEAD_PAD)[:, :, :real].reshape(K, HEADS * real)


def _pad_head_rows(w, real):
    N = w.shape[1]
    w = w.reshape(HEADS, real, N)
    return jnp.pad(w, ((0, 0), (0, HEAD_PAD - real), (0, 0))).reshape(HEADS * HEAD_PAD, N)


def _unpad_head_rows(w, real):
    N = w.shape[1]
    return w.reshape(HEADS, HEAD_PAD, N)[:, :real].reshape(HEADS * real, N)


def _block_diag(w):
    w = w.reshape(4, 2, 64, 64)
    z = jnp.zeros((4, 64, 64), w.dtype)
    top = jnp.concatenate([w[:, 0], z], axis=2)
    bot = jnp.concatenate([z, w[:, 1]], axis=2)
    return jnp.concatenate([top, bot], axis=1)


def _block_diag_t(w):
    return jnp.stack([w[:, :64, :64], w[:, 64:, 64:]], axis=1).reshape(8, 64, 64)


_IN_SPLITS = (512, 512, 384, 288, 512, 512, 512, 8, 3072)
_IN_OFF = np.concatenate([[0], np.cumsum(_IN_SPLITS)])
_KR_OFF = 64
_SEG_NAMES = ("u", "ug", "cq", "ckv", "kr", "fq", "fk", "fv", "fl", "gate")


def _in_segments(w_in):
    c = lambda n: w_in[:, int(_IN_OFF[n]):int(_IN_OFF[n + 1])]
    kv = c(3)
    kr = jnp.pad(kv[:, MLA_KV_LORA:], ((0, 0), (_KR_OFF, LANES - _KR_OFF - MLA_ROPE)))
    fl = jnp.pad(c(7), ((0, 0), (0, LANES - HEADS)))
    fq = _pad_heads(c(4), FOX_HEAD_DIM) * jnp.asarray(FOX_SCALE, w_in.dtype)
    return [c(0), c(1), c(2), kv[:, :MLA_KV_LORA], kr, fq, _pad_heads(c(5), FOX_HEAD_DIM), _pad_heads(c(6), FOX_HEAD_DIM), fl, c(8)]


def _in_unsegment(dw_p, widths):
    offs = np.concatenate([[0], np.cumsum(widths)])
    seg = [dw_p[:, int(offs[n]):int(offs[n + 1])] for n in range(len(widths))]
    u, ug, cq, ckv, kr, fq, fk, fv, fl, gate = seg
    return jnp.concatenate([
        u, ug, cq, ckv, kr[:, _KR_OFF:_KR_OFF + MLA_ROPE], _unpad_heads(fq, FOX_HEAD_DIM) * FOX_SCALE,
        _unpad_heads(fk, FOX_HEAD_DIM), _unpad_heads(fv, FOX_HEAD_DIM), fl[:, :HEADS], gate], axis=1)


def _split_wuq(wuq):
    return _pad_heads(wuq, MLA_NOPE + MLA_ROPE)


def _split_wukv(wukv):
    w = wukv.reshape(MLA_KV_LORA, HEADS, MLA_NOPE + MLA_V)
    pad = lambda t: jnp.pad(t, ((0, 0), (0, 0), (0, HEAD_PAD - t.shape[2]))).reshape(MLA_KV_LORA, HEADS * HEAD_PAD)
    return pad(w[:, :, :MLA_NOPE]), pad(w[:, :, MLA_NOPE:])


def _merge_wukv(dk_p, dv_p):
    k = dk_p.reshape(MLA_KV_LORA, HEADS, HEAD_PAD)[:, :, :MLA_NOPE]
    v = dv_p.reshape(MLA_KV_LORA, HEADS, HEAD_PAD)[:, :, :MLA_V]
    return jnp.concatenate([k, v], axis=2).reshape(MLA_KV_LORA, HEADS * (MLA_NOPE + MLA_V))


def _layer_fwd(x, h, p_i, w, g_next, tabs):
    c_q, c_k, s_lo, s_hi = tabs
    sv = {"x0": x}
    segs = _in_segments(w["w_in"])
    z = {}
    for nm, ws in zip(_SEG_NAMES, segs):
        z[nm] = _mm(h, ws, out_dtype=BF16 if nm in ("fq", "fk", "fv", "gate") else F32, bias=_ones_lane_bias() if nm == "fv" else None,
                    name="in_" + nm)
    sv.update(h=h, z=z)
    wa_bd, wx_bd = _block_diag(w["lru_wa"]).astype(BF16), _block_diag(w["lru_wx"]).astype(BF16)
    oa, xc, hs = _lru_fwd(z["u"], z["ug"], w["conv_w"], w["conv_b"], wa_bd, wx_bd, w["lru_ba"], w["lru_bx"], w["lru_lambda"])
    sv.update(oa=oa, xc=xc, hs=hs)
    qn = _rmsnorm_fwd(z["cq"], w["mla_q_norm"], "q_norm_fwd")
    kvn = _rmsnorm_fwd(z["ckv"], w["mla_kv_norm"], "kv_norm_fwd")
    wuq_p = _split_wuq(w["mla_wuq"])
    wk_p, wv_p = _split_wukv(w["mla_wukv"])
    qb = _rope_q(_mm(qn, wuq_p, name="mla_q"), c_q, s_lo, s_hi, transpose=False, out_dtype=BF16, name="rope_q")
    kb = _rope_k(_mm(kvn, wk_p, name="mla_k"), z["kr"], c_k, s_lo, s_hi)
    vb = _mm(kvn, wv_p, out_dtype=BF16, bias=_ones_lane_bias(), name="mla_v")
    ob, lse_b = _fa_fwd(qb, kb, vb, None, unit=64, name="mla_attn")
    sv.update(qn=qn, kvn=kvn, qb=qb, kb=kb, vb=vb, ob=ob, lse_b=lse_b)
    bf = jnp.pad(w["fox_bf"], (0, LANES - HEADS)).reshape(1, LANES)
    dec = _decay_fwd(z["fl"], bf)
    drow = dec[:, :HEADS].T.reshape(HEADS, 1, dec.shape[0])
    oc, lse_c = _fa_fwd(z["fq"], z["fk"], z["fv"], drow, unit=1, name="fox_attn")
    sv.update(drow=drow, oc=oc, lse_c=lse_c)
    ya = _mm(oa, w["w_br_a"], out_dtype=BF16, name="br_a")
    yb = _mm(ob, _pad_head_rows(w["w_br_b"], MLA_V), out_dtype=BF16, name="br_b")
    yc = _mm(oc, _pad_head_rows(w["w_br_c"], FOX_HEAD_DIM), out_dtype=BF16, name="br_c")
    merged = _merge_fwd(ya, yb, yc, z["gate"], w["gate_b"])
    x1, hn = _mm_res_norm(merged, w["w_o"], x, w["ffn_norm"], "w_o")
    sv.update(ya=ya, yb=yb, yc=yc, merged=merged, x1=x1)
    hf, act = _ffn_up(hn, _ffn_pair_columns(w["w_gate_up"]))
    x2, pn = _mm_res_norm(act, w["w_down"], x1, w["ple_norm"], "ffn_down")
    sv.update(hn=hn, hf=hf, act=act, x2=x2)
    pre = _mm(pn, w["w_ple_gate"], name="ple_gate")
    e = _mm(p_i, w["w_ple"], name="ple_embed")
    x3, h_next = _ple_fwd(x2, pre, e, g_next)
    sv.update(pn=pn, pre=pre, e=e, p_i=p_i)
    return x3, h_next, sv


def _layer_bwd(dx3, w, sv, tabs):
    c_q, c_k, s_lo, s_hi = tabs
    g = {}
    z = sv["z"]
    dpre, de = _ple_bwd(dx3, sv["pre"], sv["e"])
    g["w_ple"] = _mm(sv["p_i"], de, ta=True, name="d_w_ple")
    g["w_ple_gate"] = _mm(sv["pn"], dpre, ta=True, name="d_w_ple_gate")
    dx2, g["ple_norm"] = _mm_norm_bwd(dpre, w["w_ple_gate"], sv["x2"], w["ple_norm"], dx3, "ple_norm_bwd")
    g["w_down"] = _mm(sv["act"], dx2, ta=True, name="d_w_down")
    dhf = _ffn_down_bwd(dx2, w["w_down"], sv["hf"])
    g["w_gate_up"] = _ffn_unpair_columns(_mm(sv["hn"], dhf, ta=True, name="d_w_gate_up"))
    dx1, g["ffn_norm"] = _mm_norm_bwd(dhf, _ffn_pair_columns(w["w_gate_up"]), sv["x1"], w["ffn_norm"], dx2, "ffn_norm_bwd")
    g["w_o"] = _mm(sv["merged"], dx1, ta=True, name="d_w_o")
    dm = _mm(dx1, w["w_o"], tb=True, name="d_merged")
    dya, dyb, dyc, dgate, g["gate_b"] = _merge_bwd(dm, sv["ya"], sv["yb"], sv["yc"], z["gate"], w["gate_b"])
    wbb_p, wbc_p = _pad_head_rows(w["w_br_b"], MLA_V), _pad_head_rows(w["w_br_c"], FOX_HEAD_DIM)
    g["w_br_a"] = _mm(sv["oa"], dya, ta=True, name="d_w_br_a")
    g["w_br_b"] = _unpad_head_rows(_mm(sv["ob"], dyb, ta=True, name="d_w_br_b"), MLA_V)
    g["w_br_c"] = _unpad_head_rows(_mm(sv["oc"], dyc, ta=True, name="d_w_br_c"), FOX_HEAD_DIM)
    doa = _mm(dya, w["w_br_a"], tb=True, name="d_oa")
    dob = _mm(dyb, wbb_p, tb=True, out_dtype=BF16, name="d_ob")
    doc = _mm(dyc, wbc_p, tb=True, out_dtype=BF16, name="d_oc")
    dfq, dfk, dfv, d_dec = _fa_bwd(z["fq"], z["fk"], z["fv"], sv["oc"], sv["lse_c"], doc, sv["drow"],
                                   unit=1, dq_dtype=BF16, dk_dtype=BF16, name="fox_attn_bwd")
    d_dec = jnp.pad(d_dec.T, ((0, 0), (0, LANES - HEADS)))
    bf = jnp.pad(w["fox_bf"], (0, LANES - HEADS)).reshape(1, LANES)
    dfl, dbf = _decay_bwd(d_dec, z["fl"], bf)
    g["fox_bf"] = dbf[0, :HEADS]
    dqb, dkb, dvb, _ = _fa_bwd(sv["qb"], sv["kb"], sv["vb"], sv["ob"], sv["lse_b"], dob, None,
                               unit=64, dq_dtype=F32, dk_dtype=F32, name="mla_attn_bwd")
    wuq_p = _split_wuq(w["mla_wuq"])
    wk_p, wv_p = _split_wukv(w["mla_wukv"])
    dq_pre = _rope_q(dqb, c_q, s_lo, s_hi, transpose=True, out_dtype=BF16, name="rope_q_bwd")
    dkr = _rope_k_bwd(dkb, c_k, s_lo, s_hi)
    g["mla_wuq"] = _unpad_heads(_mm(sv["qn"], dq_pre, ta=True, name="d_wuq"), MLA_NOPE + MLA_ROPE)
    g["mla_wukv"] = _merge_wukv(_mm(sv["kvn"], dkb, ta=True, name="d_wuk"), _mm(sv["kvn"], dvb, ta=True, name="d_wuv"))
    dqn = _mm(dq_pre, wuq_p, tb=True, name="d_qn")
    dkvn = _mm(dvb, wv_p, tb=True, res=_mm(dkb, wk_p, tb=True, name="d_kvn_k"), name="d_kvn")
    dcq, g["mla_q_norm"] = _rmsnorm_bwd(z["cq"], w["mla_q_norm"], dqn, out_dtype=BF16, name="q_norm_bwd")
    dckv, g["mla_kv_norm"] = _rmsnorm_bwd(z["ckv"], w["mla_kv_norm"], dkvn, out_dtype=BF16, name="kv_norm_bwd")
    wa_bd, wx_bd = _block_diag(w["lru_wa"]).astype(BF16), _block_diag(w["lru_wx"]).astype(BF16)
    du, dug, dcw, dcb, dba, dbx, dlam, dwa, dwx = _lru_bwd(
        doa, z["u"], z["ug"], sv["xc"], sv["hs"], w["conv_w"], wa_bd, wx_bd, w["lru_ba"], w["lru_bx"], w["lru_lambda"])
    g["conv_w"], g["conv_b"], g["lru_ba"], g["lru_bx"] = dcw, dcb[0], dba[0], dbx[0]
    g["lru_lambda"] = dlam[0] * LRU_C * jax.nn.sigmoid(-w["lru_lambda"])
    g["lru_wa"], g["lru_wx"] = _block_diag_t(dwa), _block_diag_t(dwx)
    dsegs = [du, dug, dcq, dckv, dkr, dfq, dfk, dfv, dfl, dgate]
    dz = jnp.concatenate(dsegs, axis=1)
    w_in_p = jnp.concatenate(_in_segments(w["w_in"]), axis=1)
    g["w_in"] = _in_unsegment(_mm(sv["h"], dz, ta=True, name="d_w_in"), [d.shape[1] for d in dsegs])
    dx0, g["mix_norm"] = _mm_norm_bwd(dz, w_in_p, sv["x0"], w["mix_norm"], dx1, "mix_norm_bwd")
    return dx0, g


_LAYER_WEIGHTS = ("mix_norm", "w_in", "gate_b", "conv_w", "conv_b", "lru_wa", "lru_ba", "lru_wx", "lru_bx", "lru_lambda",
                  "mla_q_norm", "mla_wuq", "mla_kv_norm", "mla_wukv", "fox_bf", "w_br_a", "w_br_b", "w_br_c", "w_o",
                  "ffn_norm", "w_gate_up", "w_down", "ple_norm", "w_ple_gate", "w_ple")
_BIG = ("w_in", "mla_wuq", "mla_wukv", "w_br_a", "w_br_b", "w_br_c", "w_o", "w_gate_up", "w_down", "w_ple_gate", "w_ple")
_ROW_SHARDED = ("w_o", "w_down", "w_ple_gate")
_SMALL = ("mix_norm", "gate_b", "conv_b", "lru_wa", "lru_ba", "lru_wx", "lru_bx", "lru_lambda", "mla_q_norm", "mla_kv_norm",
          "fox_bf", "ffn_norm", "ple_norm")


def _local_step(x, p, layers, final_norm, target):
    tabs = _rope_tables(x.shape[0])
    saved = []
    h = _rmsnorm_fwd(x, layers[0]["mix_norm"], "mix_norm_fwd")
    for i in range(DEPTH):
        g_next = layers[i + 1]["mix_norm"] if i + 1 < DEPTH else None
        x, h, sv = _layer_fwd(x, h, p[i], layers[i], g_next, tabs)
        saved.append(sv)
    loss, dx, d_final = _loss_head(x, final_norm, target)
    grads = [None] * DEPTH
    for i in reversed(range(DEPTH)):
        dx, grads[i] = _layer_bwd(dx, layers[i], saved[i], tabs)
    return loss, dx, grads, d_final


def _hbm():
    return pl.BlockSpec(memory_space=pltpu.HBM)


def _peers(x, y):
    return [(1 - x, y), (x, 1 - y), (1 - x, 1 - y)]


def _gather_weights(arrs, name):
    n_arr = len(arrs)

    def body(*refs):
        srcs, outs = refs[:n_arr], refs[n_arr:2 * n_arr]
        send_sems, recv_sems = refs[2 * n_arr:]
        x, y, c = lax.axis_index("x"), lax.axis_index("y"), lax.axis_index("c")
        me = 2 * x + y
        peers = _peers(x, y)

        def copy(sem, src, dst, to):
            return pltpu.make_async_remote_copy(src_ref=src, dst_ref=dst, send_sem=send_sems.at[sem], recv_sem=recv_sems.at[sem],
                                                device_id=to, device_id_type=MESH)

        started = []
        for a in range(n_arr):
            for j, (px, py) in enumerate(peers):
                cp = copy(6 * a + j, srcs[a].at[c], outs[a].at[me, c], (px, py, c))
                cp.start()
                started.append(cp)
        for a in range(n_arr):
            for j, (px, py) in enumerate(peers):
                landed = outs[a].at[2 * px + py, c]
                copy(6 * a + j, srcs[a].at[c], landed, (px, py, c)).wait_recv()
                cp = copy(6 * a + 3 + j, landed, landed, (x, y, 1 - c))
                cp.start()
                started.append(cp)
        for a in range(n_arr):
            for j, (px, py) in enumerate(peers):
                copy(6 * a + 3 + j, srcs[a].at[1 - c], outs[a].at[2 * px + py, 1 - c], (x, y, 1 - c)).wait_recv()
        for cp in started:
            cp.wait_send()

    return pl.pallas_call(
        body, name=name, in_specs=[_hbm()] * n_arr, out_specs=tuple([_hbm()] * n_arr),
        out_shape=tuple(jax.ShapeDtypeStruct((4,) + t.shape, t.dtype) for t in arrs),
        scratch_shapes=[pltpu.SemaphoreType.DMA((6 * n_arr,)), pltpu.SemaphoreType.DMA((6 * n_arr,))],
    )(*arrs)


def _pair_swap_halves(g4):
    n, R, W = g4.shape
    Rh = R // 2

    def body(src_ref, out_ref, send_sem, recv_sem):
        x, y, c = lax.axis_index("x"), lax.axis_index("y"), lax.axis_index("c")
        cp = pltpu.make_async_remote_copy(src_ref=src_ref.at[:, pl.ds((1 - c) * Rh, Rh), :], dst_ref=out_ref, send_sem=send_sem,
                                          recv_sem=recv_sem, device_id=(x, y, 1 - c), device_id_type=MESH)
        cp.start()
        cp.wait()

    return pl.pallas_call(
        body, name="grad_pair_swap", in_specs=[_hbm()], out_specs=_hbm(), out_shape=jax.ShapeDtypeStruct((n, Rh, W), g4.dtype),
        scratch_shapes=[pltpu.SemaphoreType.DMA, pltpu.SemaphoreType.DMA],
    )(g4)


def _pair_add(g4, sib, c_arr):
    n, R, W = g4.shape
    Rh = R // 2
    tr = _tile_rows(Rh)
    nb = Rh // tr

    def body(c_ref, a_ref, b_ref, o_ref):
        o_ref[...] = (a_ref[...].astype(F32) + b_ref[...].astype(F32)).astype(o_ref.dtype)

    return pl.pallas_call(
        body, name="grad_pair_add",
        grid_spec=pltpu.PrefetchScalarGridSpec(
            num_scalar_prefetch=1, grid=(n, nb),
            in_specs=[pl.BlockSpec((None, tr, W), lambda s, i, c: (s, c[0] * nb + i, 0)), pl.BlockSpec((None, tr, W), lambda s, i, c: (s, i, 0))],
            out_specs=pl.BlockSpec((None, tr, W), lambda s, i, c: (s, i, 0))),
        out_shape=jax.ShapeDtypeStruct((n, Rh, W), g4.dtype), compiler_params=_cparams(("parallel", "parallel")),
    )(c_arr, g4, sib)


def _tile_rows(n):
    for t in (512, 480, 400, 320, 256, 240, 160, 128, 80, 64, 40, 32, 16, 8):
        if n % t == 0:
            return t
    return n


def _chips_exchange(part):
    n, Rh, W = part.shape

    def body(src_ref, out_ref, send_sems, recv_sems):
        x, y, c = lax.axis_index("x"), lax.axis_index("y"), lax.axis_index("c")

        def copy(j, to):
            return pltpu.make_async_remote_copy(src_ref=src_ref.at[2 * to[0] + to[1]], dst_ref=out_ref.at[j], send_sem=send_sems.at[j],
                                                recv_sem=recv_sems.at[j], device_id=(to[0], to[1], c), device_id_type=MESH)

        cps = [copy(j, peer) for j, peer in enumerate(_peers(x, y))]
        for cp in cps:
            cp.start()
        for cp in cps:
            cp.wait()

    return pl.pallas_call(
        body, name="grad_chips_exchange", in_specs=[_hbm()], out_specs=_hbm(), out_shape=jax.ShapeDtypeStruct((3, Rh, W), part.dtype),
        scratch_shapes=[pltpu.SemaphoreType.DMA((3,)), pltpu.SemaphoreType.DMA((3,))],
    )(part)


def _chips_add(part, got, k_arr, c_arr):
    n, Rh, W = part.shape
    tr = _tile_rows(Rh)
    nb = Rh // tr

    def body(k_ref, c_ref, a_ref, b_ref, o_ref):
        mine = pl.program_id(0) == c_ref[0]

        @pl.when(mine)
        def _():
            o_ref[...] = ((a_ref[...].astype(F32) + b_ref[0].astype(F32)) + b_ref[1].astype(F32)) + b_ref[2].astype(F32)

        @pl.when(jnp.logical_not(mine))
        def _():
            o_ref[...] = jnp.zeros_like(o_ref)

    return pl.pallas_call(
        body, name="grad_chips_add",
        grid_spec=pltpu.PrefetchScalarGridSpec(
            num_scalar_prefetch=2, grid=(2, nb),
            in_specs=[pl.BlockSpec((None, tr, W), lambda h, i, k, c: (k[0], i, 0)), pl.BlockSpec((3, tr, W), lambda h, i, k, c: (0, i, 0))],
            out_specs=pl.BlockSpec((tr, W), lambda h, i, k, c: (h * nb + i, 0))),
        out_shape=jax.ShapeDtypeStruct((2 * Rh, W), F32), compiler_params=_cparams(("parallel", "parallel")),
    )(k_arr, c_arr, part, got)


def _pair_gather(buf):
    R, W = buf.shape
    Rh = R // 2

    def body(src_ref, out_ref, send_sem, recv_sem):
        x, y, c = lax.axis_index("x"), lax.axis_index("y"), lax.axis_index("c")
        mine, other = pl.ds(c * Rh, Rh), pl.ds((1 - c) * Rh, Rh)
        pltpu.make_async_remote_copy(src_ref=src_ref.at[mine], dst_ref=out_ref.at[mine], send_sem=send_sem, recv_sem=recv_sem,
                                     device_id=(x, y, 1 - c), device_id_type=MESH).start()
        pltpu.make_async_remote_copy(src_ref=src_ref.at[mine], dst_ref=out_ref.at[other], send_sem=send_sem, recv_sem=recv_sem,
                                     device_id=(x, y, 1 - c), device_id_type=MESH).wait()

    return pl.pallas_call(
        body, name="grad_pair_gather", in_specs=[_hbm()], out_specs=_hbm(), out_shape=jax.ShapeDtypeStruct((R, W), buf.dtype),
        input_output_aliases={0: 0}, scratch_shapes=[pltpu.SemaphoreType.DMA, pltpu.SemaphoreType.DMA],
    )(buf)


def _gather_all(buf):
    R, W = buf.shape

    def body(src_ref, out_ref, send_sems, recv_sems, local_sem):
        x, y, c = lax.axis_index("x"), lax.axis_index("y"), lax.axis_index("c")
        me = 4 * x + 2 * y + c
        mine = pltpu.make_async_copy(src_ref, out_ref.at[me], local_sem)
        mine.start()
        rel = [((x + (r >> 2 & 1)) % 2, (y + (r >> 1 & 1)) % 2, (c + (r & 1)) % 2) for r in range(1, 8)]

        def copy(j, slot, to):
            return pltpu.make_async_remote_copy(src_ref=src_ref, dst_ref=out_ref.at[slot], send_sem=send_sems.at[j],
                                                recv_sem=recv_sems.at[j], device_id=to, device_id_type=MESH)

        sends = [copy(j, me, to) for j, to in enumerate(rel)]
        for cp in sends:
            cp.start()
        for j, to in enumerate(rel):
            copy(j, 4 * to[0] + 2 * to[1] + to[2], to).wait_recv()
        for cp in sends:
            cp.wait_send()
        mine.wait()

    return pl.pallas_call(
        body, name="small_gather", in_specs=[_hbm()], out_specs=_hbm(), out_shape=jax.ShapeDtypeStruct((8, R, W), buf.dtype),
        scratch_shapes=[pltpu.SemaphoreType.DMA((7,)), pltpu.SemaphoreType.DMA((7,)), pltpu.SemaphoreType.DMA],
    )(buf)


def _sum_slots(stack):
    n, R, W = stack.shape
    tr = _tile_rows(R)

    def body(s_ref, o_ref):
        tot = s_ref[0]
        for j in range(1, n):
            tot = tot + s_ref[j]
        o_ref[...] = tot

    return pl.pallas_call(
        body, name="small_sum", grid=(R // tr,), in_specs=[pl.BlockSpec((n, tr, W), lambda i: (0, i, 0))],
        out_specs=pl.BlockSpec((tr, W), lambda i: (i, 0)), out_shape=jax.ShapeDtypeStruct((R, W), F32),
        compiler_params=_cparams(("parallel",)),
    )(stack)


def _adamw(wp, gp, mp, vp, name):
    R, W = wp.shape
    tr = R
    for t in (1024, 512, 256, 128, 64, 32, 16, 8):
        if R % t == 0 and t * W <= 512 * 1024:
            tr = t
            break
    c1 = 1.0 - ADAM_B1 ** ADAM_STEP
    c2 = 1.0 - ADAM_B2 ** ADAM_STEP

    def body(w_ref, g_ref, m_ref, v_ref, d_ref, mo_ref, vo_ref):
        gv = g_ref[...]
        m = ADAM_B1 * m_ref[...] + (1.0 - ADAM_B1) * gv
        v = ADAM_B2 * v_ref[...] + (1.0 - ADAM_B2) * (gv * gv)
        m_hat = m / c1
        v_hat = v / c2
        d_ref[...] = -ADAM_LR * (m_hat / (jnp.sqrt(v_hat) + ADAM_EPS) + ADAM_WD * w_ref[...])
        mo_ref[...] = m
        vo_ref[...] = v

    blk = pl.BlockSpec((tr, W), lambda i: (i, 0))
    shp = jax.ShapeDtypeStruct((R, W), F32)
    return pl.pallas_call(body, name=name, grid=(R // tr,), in_specs=[blk] * 4, out_specs=(blk,) * 3, out_shape=(shp,) * 3,
                          compiler_params=_cparams(("parallel",)))(wp, gp, mp, vp)


def _pack(arrs, rows):
    flat = jnp.concatenate([a.reshape(-1) for a in arrs])
    return jnp.pad(flat, (0, rows * PACK_W - flat.shape[0])).reshape(rows, PACK_W)


def _unpack(buf, shapes):
    flat = buf.reshape(-1)
    out, off = [], 0
    for shp in shapes:
        n = int(np.prod(shp))
        out.append(flat[off:off + n].reshape(shp))
        off += n
    return out


def _rows_for(shapes, mult):
    n = sum(int(np.prod(s)) for s in shapes)
    rows = -(-n // PACK_W)
    return -(-rows // mult) * mult


def _shard_major(g, name):
    L, K, N = g.shape
    if name in _ROW_SHARDED:
        t = g.reshape(L, 4, K // 4, N).transpose(1, 0, 2, 3)
    else:
        t = g.reshape(L, K, 4, N // 4).transpose(2, 0, 1, 3)
    return t.reshape(4, -1, PACK_W)


def _join_shards(blocks, name):
    return jnp.concatenate(blocks, axis=1 if name in _ROW_SHARDED else 2)


def kernel(x, p, mix_norm, w_in, gate_b, conv_w, conv_b, lru_wa, lru_ba, lru_wx, lru_bx, lru_lambda, mla_q_norm, mla_wuq, mla_kv_norm, mla_wukv, fox_bf, w_br_a, w_br_b, w_br_c, w_o, ffn_norm, w_gate_up, w_down, ple_norm, w_ple_gate, w_ple, final_norm, loss_target, m_mix_norm, m_w_in, m_gate_b, m_conv_w, m_conv_b, m_lru_wa, m_lru_ba, m_lru_wx, m_lru_bx, m_lru_lambda, m_mla_q_norm, m_mla_wuq, m_mla_kv_norm, m_mla_wukv, m_fox_bf, m_w_br_a, m_w_br_b, m_w_br_c, m_w_o, m_ffn_norm, m_w_gate_up, m_w_down, m_ple_norm, m_w_ple_gate, m_w_ple, m_final_norm, v_mix_norm, v_w_in, v_gate_b, v_conv_w, v_conv_b, v_lru_wa, v_lru_ba, v_lru_wx, v_lru_bx, v_lru_lambda, v_mla_q_norm, v_mla_wuq, v_mla_kv_norm, v_mla_wukv, v_fox_bf, v_w_br_a, v_w_br_b, v_w_br_c, v_w_o, v_ffn_norm, v_w_gate_up, v_w_down, v_ple_norm, v_w_ple_gate, v_w_ple, v_final_norm):
    a = dict(locals())
    names = list(_LAYER_WEIGHTS) + ["final_norm"]
    W = {n: a[n] for n in names}
    M = {n: a["m_" + n] for n in names}
    V = {n: a["v_" + n] for n in names}
    ix, iy, ic = lax.axis_index("x"), lax.axis_index("y"), lax.axis_index("c")

    sharded = list(_BIG) + ["conv_w"]
    shard_shapes = [W[n].shape for n in sharded]
    R = _rows_for(shard_shapes, 64)
    mine = [W[n].astype(BF16) for n in _BIG] + [conv_w]
    gathered = _gather_weights(mine, "weight_gather")
    me = 2 * ix + iy
    gathered = [lax.dynamic_update_slice(g, t[None], (me,) + (0,) * t.ndim) for g, t in zip(gathered, mine)]
    full = {n: _join_shards([g[k] for k in range(4)], n) for n, g in zip(sharded, gathered)}
    conv_w_full = full["conv_w"]
    layers = []
    for i in range(DEPTH):
        lw = {n: W[n][i] for n in _SMALL}
        for n in _BIG:
            lw[n] = full[n][i]
        lw["conv_w"] = conv_w_full[i]
        layers.append(lw)

    loss_sum, dx, grads, d_final = _local_step(x[0], p[:, 0], layers, final_norm, loss_target[0])
    loss = lax.psum(loss_sum, ("x", "y", "c"))

    parts = [_shard_major(jnp.stack([grads[i][n] for i in range(DEPTH)]), n).astype(BF16) for n in sharded]
    g4, off = jnp.zeros((4, R, PACK_W), BF16), 0
    for t in parts:
        g4 = lax.dynamic_update_slice(g4, t, (0, off, 0))
        off += t.shape[1]
    c_arr = jnp.reshape(ic, (1,)).astype(jnp.int32)
    k_arr = jnp.reshape(2 * ix + iy, (1,)).astype(jnp.int32)
    pair = _pair_add(g4, _pair_swap_halves(g4), c_arr)
    g_pack = _pair_gather(_chips_add(pair, _chips_exchange(pair), k_arr, c_arr))
    big_out = {}
    for n, gsh in zip(sharded, _unpack(g_pack, shard_shapes)):
        view = lambda t: t.reshape(-1, t.shape[-1])
        d, nm, nv = _adamw(view(W[n]), view(gsh), view(M[n]), view(V[n]), "adamw_" + n)
        for key, arr in (("g", gsh), ("d", d), ("m", nm), ("v", nv)):
            big_out[(key, n)] = arr.reshape(W[n].shape)

    small = list(_SMALL) + ["final_norm"]
    small_shapes = [W[n].shape for n in small]
    Rs = _rows_for(small_shapes, 8)
    mine_small = [d_final if n == "final_norm" else jnp.stack([grads[i][n] for i in range(DEPTH)]) for n in small]
    sg = _sum_slots(_gather_all(_pack(mine_small, Rs)))
    small_out = {}
    for n, gsm in zip(small, _unpack(sg, small_shapes)):
        view = lambda t: t.reshape(-1, t.shape[-1])
        d, nm, nv = _adamw(view(W[n]), view(gsm), view(M[n]), view(V[n]), "adamw_" + n)
        for key, arr in (("g", gsm), ("d", d), ("m", nm), ("v", nv)):
            small_out[(key, n)] = arr.reshape(W[n].shape)

    def assemble(key, n):
        return big_out[(key, n)] if n in sharded else small_out[(key, n)]

    outs = [loss, dx[None]]
    for key in ("g", "d", "m", "v"):
        outs += [assemble(key, n) for n in names]
    return tuple(outs)
```

```python
import functools
import math

import numpy as np
import jax
import jax.numpy as jnp
from jax import lax
from jax.experimental import pallas as pl
from jax.experimental.pallas import tpu as pltpu

F32, BF16 = jnp.float32, jnp.bfloat16
MESH = pl.DeviceIdType.MESH

D_MODEL = 1024
DEPTH = 2
EPS = 1e-6
NEG_INF = -1e30
LRU_WIDTH = 512
LRU_HEADS = 8
LRU_C = 8.0
CONV_WIDTH = 4
HEADS = 8
MLA_Q_LORA = 384
MLA_KV_LORA = 256
MLA_NOPE = 64
MLA_ROPE = 32
MLA_V = 64
ROPE_BASE = 10000.0
FOX_HEAD_DIM = 64
D_FF = 2816
PLE_DIM = 256
HEAD_PAD = 128
MLA_SCALE = (MLA_NOPE + MLA_ROPE) ** -0.5
FOX_SCALE = FOX_HEAD_DIM ** -0.5

ADAM_LR, ADAM_B1, ADAM_B2, ADAM_EPS, ADAM_WD, ADAM_STEP = 0.001, 0.9, 0.999, 1e-08, 0.01, 10

VMEM_LIMIT_BYTES = 48 * 1024 * 1024
LANES = 128
PACK_W = 1024

ROW_TILE = 512
ATTN_TILE = 1024
LRU_CHUNK = 512


def _cparams(dims):
    return pltpu.CompilerParams(dimension_semantics=dims, vmem_limit_bytes=VMEM_LIMIT_BYTES)


def _tile(n, cap):
    if n <= cap:
        return n
    t = (cap // LANES) * LANES
    while t >= LANES:
        if n % t == 0:
            return t
        t -= LANES
    raise ValueError(f"no tile for {n} under {cap}")


def _rows(n):
    return min(ROW_TILE, n)


MM_VMEM_BUDGET = 36 * 1024 * 1024


def _mm_tiles(M, N, K, a_bytes, b_bytes, o_bytes, has_res):
    best, best_work = None, 0
    for tm in {_tile(M, c) for c in (1024, 512, 256)}:
        for tn in {_tile(N, c) for c in (1792, 1024, 512)}:
            for tk in {_tile(K, c) for c in (2048, 1408, 1024, 512)}:
                need = 2 * (tm * tk * a_bytes + tk * tn * b_bytes + tm * tn * o_bytes + (tm * tn * 4 if has_res else 0))
                need += tm * tn * 4 if tk < K else 0
                need += tm * tn * 4
                if need <= MM_VMEM_BUDGET and tm * tn * tk > best_work:
                    best, best_work = (tm, tn, tk), tm * tn * tk
    assert best is not None, (M, N, K)
    return best

def _mm(a, b, *, ta=False, tb=False, out_dtype=F32, res=None, bias=None, name):
    K, M = a.shape if ta else a.shape[::-1]
    N, K2 = b.shape if tb else b.shape[::-1]
    assert K == K2, (name, a.shape, b.shape)
    assert res is None or bias is None
    tm, tn, tk = _mm_tiles(M, N, K, a.dtype.itemsize, b.dtype.itemsize, jnp.dtype(out_dtype).itemsize, res is not None)
    nk = K // tk
    a_spec = pl.BlockSpec((tk, tm), lambda i, j, k: (k, i)) if ta else pl.BlockSpec((tm, tk), lambda i, j, k: (i, k))
    b_spec = pl.BlockSpec((tn, tk), lambda i, j, k: (j, k)) if tb else pl.BlockSpec((tk, tn), lambda i, j, k: (k, j))
    o_spec = pl.BlockSpec((tm, tn), lambda i, j, k: (i, j))
    dn = (((0,) if ta else (1,), (1,) if tb else (0,)), ((), ()))
    if bias is not None:
        res, r_spec = bias, pl.BlockSpec((1, tn), lambda i, j, k: (0, j))
    else:
        r_spec = o_spec
    has_res = res is not None

    def body(*refs):
        a_ref, b_ref = refs[0], refs[1]
        r_ref = refs[2] if has_res else None
        o_ref = refs[3] if has_res else refs[2]
        av, bv = a_ref[...], b_ref[...]
        if av.dtype != BF16:
            av = av.astype(BF16)
        if bv.dtype != BF16:
            bv = bv.astype(BF16)
        part = lax.dot_general(av, bv, dn, preferred_element_type=F32)

        def finish(total):
            if has_res:
                total = total + r_ref[...]
            o_ref[...] = total.astype(out_dtype)

        if nk == 1:
            finish(part)
        else:
            acc = refs[-1]
            k = pl.program_id(2)

            @pl.when(k == 0)
            def _():
                acc[...] = part

            @pl.when(k > 0)
            def _():
                acc[...] += part

            @pl.when(k == nk - 1)
            def _():
                finish(acc[...])

    in_specs = [a_spec, b_spec] + ([r_spec] if has_res else [])
    args = (a, b) + ((res,) if has_res else ())
    return pl.pallas_call(
        body, name=name, grid=(M // tm, N // tn, nk), in_specs=in_specs, out_specs=o_spec,
        out_shape=jax.ShapeDtypeStruct((M, N), out_dtype),
        scratch_shapes=[pltpu.VMEM((tm, tn), F32)] if nk > 1 else [],
        compiler_params=_cparams(("parallel", "parallel", "arbitrary")),
    )(*args)


def _mm_res_norm(a, b, res, g, name):
    M, K = a.shape
    N = b.shape[1]
    tm, tk = _tile(M, 512), _tile(K, 1408)
    nk = K // tk

    def body(a_ref, b_ref, r_ref, g_ref, o_ref, h_ref, *scratch):
        part = jnp.dot(a_ref[...], b_ref[...], preferred_element_type=F32)

        def finish(total):
            xn = total + r_ref[...]
            o_ref[...] = xn
            rstd = lax.rsqrt(jnp.mean(xn * xn, axis=1, keepdims=True) + EPS)
            h_ref[...] = (xn * rstd * g_ref[...]).astype(BF16)

        if nk == 1:
            finish(part)
        else:
            acc = scratch[0]
            k = pl.program_id(1)

            @pl.when(k == 0)
            def _():
                acc[...] = part

            @pl.when(k > 0)
            def _():
                acc[...] += part

            @pl.when(k == nk - 1)
            def _():
                finish(acc[...])

    row = pl.BlockSpec((tm, N), lambda i, k: (i, 0))
    return pl.pallas_call(
        body, name=name, grid=(M // tm, nk),
        in_specs=[pl.BlockSpec((tm, tk), lambda i, k: (i, k)), pl.BlockSpec((tk, N), lambda i, k: (k, 0)), row,
                  pl.BlockSpec((1, N), lambda i, k: (0, 0))],
        out_specs=(row, row), out_shape=(jax.ShapeDtypeStruct((M, N), F32), jax.ShapeDtypeStruct((M, N), BF16)),
        scratch_shapes=[pltpu.VMEM((tm, N), F32)] if nk > 1 else [],
        compiler_params=_cparams(("parallel", "arbitrary")),
    )(a, b, res, g.reshape(1, N))


def _mm_norm_bwd(a, b, x, g, add, name):
    M, K = a.shape
    N = b.shape[0]
    tm, tk = _tile(M, 1024), _tile(K, 1408)
    nk = K // tk

    def body(a_ref, b_ref, x_ref, g_ref, add_ref, dx_ref, dg_ref, *scratch):
        i, k = pl.program_id(0), pl.program_id(1)
        part = lax.dot_general(a_ref[...], b_ref[...], _NT, preferred_element_type=F32)

        @pl.when(jnp.logical_and(i == 0, k == 0))
        def _():
            dg_ref[...] = jnp.zeros_like(dg_ref)

        def finish(dyv):
            xf = x_ref[...]
            rstd = lax.rsqrt(jnp.mean(xf * xf, axis=1, keepdims=True) + EPS)
            xhat = xf * rstd
            dxh = dyv * g_ref[...]
            dx_ref[...] = rstd * (dxh - xhat * jnp.mean(dxh * xhat, axis=1, keepdims=True)) + add_ref[...]
            dg_ref[...] += jnp.sum(dyv * xhat, axis=0, keepdims=True)

        if nk == 1:
            finish(part)
        else:
            acc = scratch[0]

            @pl.when(k == 0)
            def _():
                acc[...] = part

            @pl.when(k > 0)
            def _():
                acc[...] += part

            @pl.when(k == nk - 1)
            def _():
                finish(acc[...])

    row = pl.BlockSpec((tm, N), lambda i, k: (i, 0))
    vec = pl.BlockSpec((1, N), lambda i, k: (0, 0))
    dx, dg = pl.pallas_call(
        body, name=name, grid=(M // tm, nk),
        in_specs=[pl.BlockSpec((tm, tk), lambda i, k: (i, k)), pl.BlockSpec((N, tk), lambda i, k: (0, k)), row, vec, row],
        out_specs=(row, vec), out_shape=(jax.ShapeDtypeStruct((M, N), F32), jax.ShapeDtypeStruct((1, N), F32)),
        scratch_shapes=[pltpu.VMEM((tm, N), F32)] if nk > 1 else [],
        compiler_params=_cparams(("arbitrary", "arbitrary")),
    )(a, b, x, g.reshape(1, N), add)
    return dx, dg.reshape(N)


FFN_TILE = 1408
FFN_SUBTILES = ((0, 512), (512, 1024), (1024, 1408))


def _ffn_pair_columns(w_gate_up):
    F = w_gate_up.shape[-1] // 2
    parts = []
    for j in range(F // FFN_TILE):
        parts += [w_gate_up[..., j * FFN_TILE:(j + 1) * FFN_TILE], w_gate_up[..., F + j * FFN_TILE:F + (j + 1) * FFN_TILE]]
    return jnp.concatenate(parts, axis=-1)


def _ffn_unpair_columns(dw):
    F = dw.shape[-1] // 2
    n = F // FFN_TILE
    blk = [dw[..., j * FFN_TILE:(j + 1) * FFN_TILE] for j in range(2 * n)]
    return jnp.concatenate(blk[0::2] + blk[1::2], axis=-1)


def _ffn_up(hn, w_pair):
    S, D = hn.shape
    W2 = w_pair.shape[1]
    F, tf = W2 // 2, FFN_TILE
    tm = _rows(S)

    def body(h_ref, w_ref, hf_ref, act_ref):
        hv = h_ref[...]
        for lo, hi in FFN_SUBTILES:
            gt = jnp.dot(hv, w_ref[:, lo:hi], preferred_element_type=F32)
            up = jnp.dot(hv, w_ref[:, tf + lo:tf + hi], preferred_element_type=F32)
            hf_ref[:, lo:hi] = gt.astype(BF16)
            hf_ref[:, tf + lo:tf + hi] = up.astype(BF16)
            act_ref[:, lo:hi] = (gt * jax.nn.sigmoid(gt) * up).astype(BF16)

    return pl.pallas_call(
        body, name="ffn_up", grid=(S // tm, F // tf),
        in_specs=[pl.BlockSpec((tm, D), lambda i, j: (i, 0)), pl.BlockSpec((D, 2 * tf), lambda i, j: (0, j))],
        out_specs=(pl.BlockSpec((tm, 2 * tf), lambda i, j: (i, j)), pl.BlockSpec((tm, tf), lambda i, j: (i, j))),
        out_shape=(jax.ShapeDtypeStruct((S, W2), BF16), jax.ShapeDtypeStruct((S, F), BF16)),
        compiler_params=_cparams(("parallel", "parallel")),
    )(hn, w_pair)


def _ffn_down_bwd(dx, w_down, hf):
    S, D = dx.shape
    F, tf = w_down.shape[0], FFN_TILE
    tm = _rows(S)

    def body(d_ref, w_ref, h_ref, o_ref):
        dv = d_ref[...].astype(BF16)
        for lo, hi in FFN_SUBTILES:
            dact = lax.dot_general(dv, w_ref[lo:hi, :], _NT, preferred_element_type=F32)
            gt, up = h_ref[:, lo:hi].astype(F32), h_ref[:, tf + lo:tf + hi].astype(F32)
            sg = jax.nn.sigmoid(gt)
            o_ref[:, lo:hi] = (dact * up * sg * (1.0 + gt * (1.0 - sg))).astype(BF16)
            o_ref[:, tf + lo:tf + hi] = (dact * gt * sg).astype(BF16)

    pair = pl.BlockSpec((tm, 2 * tf), lambda i, j: (i, j))
    return pl.pallas_call(
        body, name="ffn_down_bwd", grid=(S // tm, F // tf),
        in_specs=[pl.BlockSpec((tm, D), lambda i, j: (i, 0)), pl.BlockSpec((tf, D), lambda i, j: (j, 0)), pair],
        out_specs=pair, out_shape=jax.ShapeDtypeStruct((S, 2 * F), BF16),
        compiler_params=_cparams(("parallel", "parallel")),
    )(dx, w_down, hf)


def _rmsnorm_fwd(x, g, name):
    S, W = x.shape
    tm = _rows(S)

    def body(x_ref, g_ref, o_ref):
        xf = x_ref[...]
        rstd = lax.rsqrt(jnp.mean(xf * xf, axis=1, keepdims=True) + EPS)
        o_ref[...] = (xf * rstd * g_ref[...]).astype(BF16)

    return pl.pallas_call(
        body, name=name, grid=(S // tm,),
        in_specs=[pl.BlockSpec((tm, W), lambda i: (i, 0)), pl.BlockSpec((1, W), lambda i: (0, 0))],
        out_specs=pl.BlockSpec((tm, W), lambda i: (i, 0)),
        out_shape=jax.ShapeDtypeStruct((S, W), BF16), compiler_params=_cparams(("parallel",)),
    )(x, g.reshape(1, W))


def _rmsnorm_bwd(x, g, dy, *, add=None, out_dtype=F32, name):
    S, W = x.shape
    tm = _rows(S)
    has_add = add is not None

    def body(*refs):
        x_ref, g_ref, dy_ref = refs[:3]
        add_ref = refs[3] if has_add else None
        dx_ref, dg_ref = refs[-2], refs[-1]
        xf = x_ref[...]
        rstd = lax.rsqrt(jnp.mean(xf * xf, axis=1, keepdims=True) + EPS)
        xhat = xf * rstd
        dyv = dy_ref[...]
        dxh = dyv * g_ref[...]
        dx = rstd * (dxh - xhat * jnp.mean(dxh * xhat, axis=1, keepdims=True))
        if has_add:
            dx = dx + add_ref[...]
        dx_ref[...] = dx.astype(out_dtype)

        @pl.when(pl.program_id(0) == 0)
        def _():
            dg_ref[...] = jnp.zeros_like(dg_ref)

        dg_ref[...] += jnp.sum(dyv * xhat, axis=0, keepdims=True)

    row = pl.BlockSpec((tm, W), lambda i: (i, 0))
    vec = pl.BlockSpec((1, W), lambda i: (0, 0))
    dx, dg = pl.pallas_call(
        body, name=name, grid=(S // tm,),
        in_specs=[row, vec, row] + ([row] if has_add else []),
        out_specs=(row, vec),
        out_shape=(jax.ShapeDtypeStruct((S, W), out_dtype), jax.ShapeDtypeStruct((1, W), F32)),
        compiler_params=_cparams(("arbitrary",)),
    )(x, g.reshape(1, W), dy, *((add,) if has_add else ()))
    return dx, dg.reshape(W)


def _loss_head(x, g, target):
    S, W = x.shape
    tm = _rows(S)

    def body(x_ref, g_ref, t_ref, loss_ref, dx_ref, dg_ref):
        xf = x_ref[...]
        gv = g_ref[...]
        rstd = lax.rsqrt(jnp.mean(xf * xf, axis=1, keepdims=True) + EPS)
        xhat = xf * rstd
        err = xhat * gv - t_ref[...]
        part = 0.5 * jnp.sum(jnp.mean(err * err, axis=1, keepdims=True), axis=0, keepdims=True)
        dyv = err * (1.0 / W)
        dxh = dyv * gv
        dx_ref[...] = rstd * (dxh - xhat * jnp.mean(dxh * xhat, axis=1, keepdims=True))

        @pl.when(pl.program_id(0) == 0)
        def _():
            dg_ref[...] = jnp.zeros_like(dg_ref)
            loss_ref[...] = jnp.zeros_like(loss_ref)

        dg_ref[...] += jnp.sum(dyv * xhat, axis=0, keepdims=True)
        loss_ref[...] += part

    row = pl.BlockSpec((tm, W), lambda i: (i, 0))
    vec = pl.BlockSpec((1, W), lambda i: (0, 0))
    loss, dx, dg = pl.pallas_call(
        body, name="loss_head", grid=(S // tm,), in_specs=[row, vec, row],
        out_specs=(pl.BlockSpec((1, 1), lambda i: (0, 0)), row, vec),
        out_shape=(jax.ShapeDtypeStruct((1, 1), F32), jax.ShapeDtypeStruct((S, W), F32), jax.ShapeDtypeStruct((1, W), F32)),
        compiler_params=_cparams(("arbitrary",)),
    )(x, g.reshape(1, W), target)
    return loss[0, 0], dx, dg.reshape(W)


def _scan_fwd(a, b, row):
    T = a.shape[0]
    d = 1
    while d < T:
        keep = row >= d
        b = jnp.where(keep, a * pltpu.roll(b, d, axis=0) + b, b)
        a = jnp.where(keep, a * pltpu.roll(a, d, axis=0), a)
        d *= 2
    return a, b


def _scan_bwd(a, b, row):
    T = a.shape[0]
    d = 1
    while d < T:
        keep = row < T - d
        b = jnp.where(keep, a * pltpu.roll(b, T - d, axis=0) + b, b)
        a = jnp.where(keep, a * pltpu.roll(a, T - d, axis=0), a)
        d *= 2
    return a, b


def _expm1(x):
    small = x * (1.0 + x * (0.5 + x * (1.0 / 6 + x * (1.0 / 24 + x * (1.0 / 120 + x * (1.0 / 720 + x * (1.0 / 5040)))))))
    return jnp.where(jnp.abs(x) < 0.25, small, jnp.exp(x) - 1.0)


_GELU_C = math.sqrt(2.0 / math.pi)


def _gelu_and_grad(x):
    inner = _GELU_C * (x + 0.044715 * x * x * x)
    th = jnp.tanh(inner)
    val = 0.5 * x * (1.0 + th)
    grad = 0.5 * (1.0 + th) + 0.5 * x * (1.0 - th * th) * _GELU_C * (1.0 + 3 * 0.044715 * x * x)
    return val, grad


def _lru_gates(xc, wa, wx, ba, bx, lam):
    xcb = xc.astype(BF16)
    r = jax.nn.sigmoid(jnp.dot(xcb, wa, preferred_element_type=F32) + ba)
    ig = jax.nn.sigmoid(jnp.dot(xcb, wx, preferred_element_type=F32) + bx)
    sp = jax.nn.softplus(-lam)
    log_a = -LRU_C * r * sp
    a = jnp.exp(log_a)
    mult = jnp.sqrt(-_expm1(2.0 * log_a))
    return xcb, r, ig, sp, a, mult


def _lru_fwd(u, ug, conv_w, conv_b, wa_bd, wx_bd, ba, bx, lam):
    S, W = u.shape
    T = min(LRU_CHUNK, S)
    nl, nc = W // LANES, S // T

    def body(u_ref, ug_ref, cw_ref, cb_ref, wa_ref, wx_ref, ba_ref, bx_ref, lam_ref, ya_ref, xc_ref, h_ref, prev_u, h_carry):
        c = pl.program_id(1)

        @pl.when(c == 0)
        def _():
            prev_u[...] = jnp.zeros_like(prev_u)
            h_carry[...] = jnp.zeros_like(h_carry)

        uv = u_ref[...]
        row = lax.broadcasted_iota(jnp.int32, (T, LANES), 0)
        row8 = lax.broadcasted_iota(jnp.int32, (8, LANES), 0)
        cw = cw_ref[...]
        xc = cb_ref[...] + uv * cw[3:4, :]
        pv = prev_u[...]
        for k in range(1, CONV_WIDTH):
            us = pltpu.roll(uv, k, axis=0)
            top = jnp.where(row8 < k, pltpu.roll(pv, k, axis=0), us[0:8])
            us = jnp.concatenate([top, us[8:]], axis=0)
            xc = xc + us * cw[3 - k:4 - k, :]
        prev_u[...] = uv[T - 8:T]
        _, r, ig, sp, a, mult = _lru_gates(xc, wa_ref[...], wx_ref[...], ba_ref[...], bx_ref[...], lam_ref[...])
        bb = mult * (ig * xc)
        aa, hh = _scan_fwd(a, bb, row)
        h = hh + aa * h_carry[7:8, :]
        h_carry[...] = h[T - 8:T]
        gl, _ = _gelu_and_grad(ug_ref[...])
        ya_ref[...] = (h * gl).astype(BF16)
        xc_ref[...] = xc
        h_ref[...] = h

    seq = pl.BlockSpec((T, LANES), lambda l, c: (c, l))
    vec = pl.BlockSpec((1, LANES), lambda l, c: (0, l))
    mat = pl.BlockSpec((None, LANES, LANES), lambda l, c: (l, 0, 0))
    return pl.pallas_call(
        body, name="lru_fwd", grid=(nl, nc),
        in_specs=[seq, seq, pl.BlockSpec((CONV_WIDTH, LANES), lambda l, c: (0, l)), vec, mat, mat, vec, vec, vec],
        out_specs=(seq, seq, seq),
        out_shape=(jax.ShapeDtypeStruct((S, W), BF16), jax.ShapeDtypeStruct((S, W), F32), jax.ShapeDtypeStruct((S, W), F32)),
        scratch_shapes=[pltpu.VMEM((8, LANES), F32), pltpu.VMEM((8, LANES), F32)],
        compiler_params=_cparams(("parallel", "arbitrary")),
    )(u, ug, conv_w, conv_b.reshape(1, W), wa_bd, wx_bd, ba.reshape(1, W), bx.reshape(1, W), lam.reshape(1, W))


def _lru_bwd(dya, u, ug, xc, h, conv_w, wa_bd, wx_bd, ba, bx, lam):
    S, W = u.shape
    T = min(LRU_CHUNK, S)
    nl, nc = W // LANES, S // T
    tb8 = T // 8

    def body(dya_ref, u_ref, ug_ref, xc_ref, h_ref, hp_ref, cw_ref, wa_ref, wx_ref, ba_ref, bx_ref, lam_ref,
             du_ref, dug_ref, dcw_ref, dcb_ref, dba_ref, dbx_ref, dlam_ref, dwa_ref, dwx_ref,
             g_next, a_next, dxc_next):
        c = pl.program_id(1)

        @pl.when(c == 0)
        def _():
            g_next[...] = jnp.zeros_like(g_next)
            a_next[...] = jnp.zeros_like(a_next)
            dxc_next[...] = jnp.zeros_like(dxc_next)
            for ref in (dcw_ref, dcb_ref, dba_ref, dbx_ref, dlam_ref, dwa_ref, dwx_ref):
                ref[...] = jnp.zeros_like(ref)

        row = lax.broadcasted_iota(jnp.int32, (T, LANES), 0)
        row8 = lax.broadcasted_iota(jnp.int32, (8, LANES), 0)
        xcv = xc_ref[...]
        wa, wx = wa_ref[...], wx_ref[...]
        xcb, r, ig, sp, a, mult = _lru_gates(xcv, wa, wx, ba_ref[...], bx_ref[...], lam_ref[...])
        gl, dgl = _gelu_and_grad(ug_ref[...])
        dyav = dya_ref[...]
        hv = h_ref[...]
        dug_ref[...] = (dyav * hv * dgl).astype(BF16)
        dh = dyav * gl
        a_up = pltpu.roll(a, T - 1, axis=0)
        a_up = jnp.where(row == T - 1, a_next[0:1, :], a_up)
        prod, gg = _scan_bwd(a_up, dh, row)
        g = gg + prod * g_next[0:1, :]
        h_prev = pltpu.roll(hv, 1, axis=0)
        first_chunk = c == nc - 1
        h_before = jnp.where(first_chunk, 0.0, hp_ref[7:8, :])
        h_prev = jnp.where(row == 0, h_before, h_prev)
        da = g * h_prev
        d_mult = g * (ig * xcv)
        d_ig = g * mult * xcv
        dxc = g * mult * ig
        d_log_a = da * a - d_mult * (a * a) / mult
        d_r = d_log_a * (-LRU_C * sp)
        d_pa = d_r * r * (1.0 - r)
        d_px = d_ig * ig * (1.0 - ig)
        d_pab, d_pxb = d_pa.astype(BF16), d_px.astype(BF16)
        nt = (((1,), (1,)), ((), ()))
        tn = (((0,), (0,)), ((), ()))
        dxc = dxc + lax.dot_general(d_pab, wa, nt, preferred_element_type=F32) + lax.dot_general(d_pxb, wx, nt, preferred_element_type=F32)
        dwa_ref[...] += lax.dot_general(xcb, d_pab, tn, preferred_element_type=F32)
        dwx_ref[...] += lax.dot_general(xcb, d_pxb, tn, preferred_element_type=F32)
        dlam_ref[...] += jnp.sum(d_log_a * r, axis=0, keepdims=True)
        dba_ref[...] += jnp.sum(d_pa, axis=0, keepdims=True)
        dbx_ref[...] += jnp.sum(d_px, axis=0, keepdims=True)
        dcb_ref[...] += jnp.sum(dxc, axis=0, keepdims=True)
        uv = u_ref[...]
        cw = cw_ref[...]
        nxt = dxc_next[...]
        du = dxc * cw[3:4, :]
        dcw_ref[3:4, :] += jnp.sum(uv * dxc, axis=0, keepdims=True)
        for k in range(1, CONV_WIDTH):
            ds = pltpu.roll(dxc, T - k, axis=0)
            bot = jnp.where(row8 >= 8 - k, pltpu.roll(nxt, 8 - k, axis=0), ds[T - 8:T])
            ds = jnp.concatenate([ds[:T - 8], bot], axis=0)
            du = du + ds * cw[3 - k:4 - k, :]
            dcw_ref[3 - k:4 - k, :] += jnp.sum(uv * ds, axis=0, keepdims=True)
        du_ref[...] = du.astype(BF16)
        g_next[...] = g[0:8]
        a_next[...] = a[0:8]
        dxc_next[...] = dxc[0:8]

    seq = pl.BlockSpec((T, LANES), lambda l, c: (nc - 1 - c, l))
    before = pl.BlockSpec((8, LANES), lambda l, c: (jnp.maximum((nc - 1 - c) * tb8 - 1, 0), l))
    vec = pl.BlockSpec((1, LANES), lambda l, c: (0, l))
    cwb = pl.BlockSpec((CONV_WIDTH, LANES), lambda l, c: (0, l))
    mat = pl.BlockSpec((None, LANES, LANES), lambda l, c: (l, 0, 0))
    vshape = jax.ShapeDtypeStruct((1, W), F32)
    mshape = jax.ShapeDtypeStruct((nl, LANES, LANES), F32)
    return pl.pallas_call(
        body, name="lru_bwd", grid=(nl, nc),
        in_specs=[seq, seq, seq, seq, seq, before, cwb, mat, mat, vec, vec, vec],
        out_specs=(seq, seq, cwb, vec, vec, vec, vec, mat, mat),
        out_shape=(jax.ShapeDtypeStruct((S, W), BF16), jax.ShapeDtypeStruct((S, W), BF16),
                   jax.ShapeDtypeStruct((CONV_WIDTH, W), F32), vshape, vshape, vshape, vshape, mshape, mshape),
        scratch_shapes=[pltpu.VMEM((8, LANES), F32)] * 3,
        compiler_params=_cparams(("parallel", "arbitrary")),
    )(dya, u, ug, xc, h, h, conv_w, wa_bd, wx_bd, ba.reshape(1, W), bx.reshape(1, W), lam.reshape(1, W))


def _decay_fwd(f_logit, bf):
    S = f_logit.shape[0]
    T = min(LRU_CHUNK, S)

    def body(f_ref, b_ref, o_ref, carry):
        @pl.when(pl.program_id(0) == 0)
        def _():
            carry[...] = jnp.zeros_like(carry)

        row = lax.broadcasted_iota(jnp.int32, (T, LANES), 0)
        v = jax.nn.log_sigmoid(f_ref[...] + b_ref[...])
        d = 1
        while d < T:
            v = jnp.where(row >= d, v + pltpu.roll(v, d, axis=0), v)
            d *= 2
        v = v + carry[7:8, :]
        carry[...] = v[T - 8:T]
        o_ref[...] = v

    return pl.pallas_call(
        body, name="decay_fwd", grid=(S // T,),
        in_specs=[pl.BlockSpec((T, LANES), lambda c: (c, 0)), pl.BlockSpec((1, LANES), lambda c: (0, 0))],
        out_specs=pl.BlockSpec((T, LANES), lambda c: (c, 0)),
        out_shape=jax.ShapeDtypeStruct((S, LANES), F32), scratch_shapes=[pltpu.VMEM((8, LANES), F32)],
        compiler_params=_cparams(("arbitrary",)),
    )(f_logit, bf)


def _decay_bwd(d_dec, f_logit, bf):
    S = f_logit.shape[0]
    T = min(LRU_CHUNK, S)
    nc = S // T

    def body(dd_ref, f_ref, b_ref, df_ref, db_ref, carry):
        @pl.when(pl.program_id(0) == 0)
        def _():
            carry[...] = jnp.zeros_like(carry)
            db_ref[...] = jnp.zeros_like(db_ref)

        row = lax.broadcasted_iota(jnp.int32, (T, LANES), 0)
        v = dd_ref[...]
        d = 1
        while d < T:
            v = jnp.where(row < T - d, v + pltpu.roll(v, T - d, axis=0), v)
            d *= 2
        v = v + carry[0:1, :]
        carry[...] = v[0:8]
        df = v * jax.nn.sigmoid(-(f_ref[...] + b_ref[...]))
        df_ref[...] = df.astype(BF16)
        db_ref[...] += jnp.sum(df, axis=0, keepdims=True)

    seq = pl.BlockSpec((T, LANES), lambda c: (nc - 1 - c, 0))
    vec = pl.BlockSpec((1, LANES), lambda c: (0, 0))
    return pl.pallas_call(
        body, name="decay_bwd", grid=(nc,), in_specs=[seq, seq, vec], out_specs=(seq, vec),
        out_shape=(jax.ShapeDtypeStruct((S, LANES), BF16), jax.ShapeDtypeStruct((1, LANES), F32)),
        scratch_shapes=[pltpu.VMEM((8, LANES), F32)], compiler_params=_cparams(("arbitrary",)),
    )(d_dec, f_logit, bf)


def _rope_tables(S):
    pos = jnp.arange(S, dtype=F32)
    inv_freq = ROPE_BASE ** (-jnp.arange(0, MLA_ROPE, 2, dtype=F32) / MLA_ROPE)
    ang = pos[:, None] * inv_freq[None, :]
    cos, sin = jnp.cos(ang), jnp.sin(ang)
    half = MLA_ROPE // 2
    z = lambda n: jnp.zeros((S, n), F32)
    c_q = jnp.concatenate([jnp.ones((S, MLA_NOPE), F32), cos, cos, z(HEAD_PAD - MLA_NOPE - MLA_ROPE)], axis=1)
    c_k = jnp.concatenate([z(MLA_NOPE), cos, cos, z(HEAD_PAD - MLA_NOPE - MLA_ROPE)], axis=1)
    s_lo = jnp.concatenate([z(MLA_NOPE), -sin, z(HEAD_PAD - MLA_NOPE - half)], axis=1)
    s_hi = jnp.concatenate([z(MLA_NOPE + half), sin, z(HEAD_PAD - MLA_NOPE - MLA_ROPE)], axis=1)
    return c_q, c_k, s_lo, s_hi


def _rot(v, c, s_lo, s_hi):
    half = MLA_ROPE // 2
    return v * c + pltpu.roll(v, LANES - half, axis=1) * s_lo + pltpu.roll(v, half, axis=1) * s_hi


def _rot_t(dv, c, s_lo, s_hi):
    half = MLA_ROPE // 2
    return dv * c + pltpu.roll(dv * s_lo, half, axis=1) + pltpu.roll(dv * s_hi, LANES - half, axis=1)


def _rope_q(q_pre, c_q, s_lo, s_hi, *, transpose, out_dtype, name):
    S, W = q_pre.shape
    tm = _rows(S)
    fn = _rot_t if transpose else _rot

    def body(q_ref, c_ref, lo_ref, hi_ref, o_ref):
        c, lo, hi = c_ref[...], lo_ref[...], hi_ref[...]
        for hd in range(W // LANES):
            cols = slice(hd * LANES, (hd + 1) * LANES)
            o_ref[:, cols] = fn(q_ref[:, cols] * MLA_SCALE, c, lo, hi).astype(out_dtype)

    blk = pl.BlockSpec((tm, W), lambda i: (i, 0))
    tab = pl.BlockSpec((tm, LANES), lambda i: (i, 0))
    return pl.pallas_call(
        body, name=name, grid=(S // tm,), in_specs=[blk, tab, tab, tab], out_specs=blk,
        out_shape=jax.ShapeDtypeStruct((S, W), out_dtype), compiler_params=_cparams(("parallel",)),
    )(q_pre, c_q, s_lo, s_hi)


def _rope_k(k_pre, k_rope, c_k, s_lo, s_hi):
    S, W = k_pre.shape
    tm = _rows(S)

    def body(k_ref, r_ref, c_ref, lo_ref, hi_ref, o_ref):
        rot = _rot(r_ref[...], c_ref[...], lo_ref[...], hi_ref[...])
        for hd in range(W // LANES):
            cols = slice(hd * LANES, (hd + 1) * LANES)
            o_ref[:, cols] = (k_ref[:, cols] + rot).astype(BF16)

    blk = pl.BlockSpec((tm, W), lambda i: (i, 0))
    tab = pl.BlockSpec((tm, LANES), lambda i: (i, 0))
    return pl.pallas_call(
        body, name="rope_k", grid=(S // tm,), in_specs=[blk, tab, tab, tab, tab], out_specs=blk,
        out_shape=jax.ShapeDtypeStruct((S, W), BF16), compiler_params=_cparams(("parallel",)),
    )(k_pre, k_rope, c_k, s_lo, s_hi)


def _rope_k_bwd(dk, c_k, s_lo, s_hi):
    S, W = dk.shape
    tm = _rows(S)

    def body(dk_ref, c_ref, lo_ref, hi_ref, o_ref):
        tot = dk_ref[:, 0:LANES]
        for hd in range(1, W // LANES):
            tot = tot + dk_ref[:, hd * LANES:(hd + 1) * LANES]
        o_ref[...] = _rot_t(tot, c_ref[...], lo_ref[...], hi_ref[...]).astype(BF16)

    tab = pl.BlockSpec((tm, LANES), lambda i: (i, 0))
    return pl.pallas_call(
        body, name="rope_k_bwd", grid=(S // tm,), in_specs=[pl.BlockSpec((tm, W), lambda i: (i, 0)), tab, tab, tab],
        out_specs=tab, out_shape=jax.ShapeDtypeStruct((S, LANES), BF16), compiler_params=_cparams(("parallel",)),
    )(dk, c_k, s_lo, s_hi)


def _pairs(n, by_key):
    if by_key:
        pr = [(i, j) for j in range(n) for i in range(j, n)]
    else:
        pr = [(i, j) for i in range(n) for j in range(i + 1)]
    return (jnp.asarray(np.array([p[0] for p in pr], np.int32)), jnp.asarray(np.array([p[1] for p in pr], np.int32)), len(pr))


def _unit_mask(shape, unit, key_axis, q_off=0):
    q = lax.broadcasted_iota(jnp.int32, shape, 1 - key_axis) + q_off
    k = lax.broadcasted_iota(jnp.int32, shape, key_axis)
    if unit > 1:
        q, k = q // unit, k // unit
    return q >= k


_NT = (((1,), (1,)), ((), ()))


ONES_LANE = 64
ROW_SPLIT = 1
ROW_SPLIT_BWD = 4


def _ones_lane_bias():
    one = np.zeros((HEADS, HEAD_PAD), np.float32)
    one[:, ONES_LANE] = 1.0
    return jnp.asarray(one.reshape(1, HEADS * HEAD_PAD))


def _lane_sum(t):
    tot = t[:, 0:LANES]
    for c in range(1, t.shape[1] // LANES):
        tot = tot + t[:, c * LANES:(c + 1) * LANES]
    return tot


def _fa_fwd(q, k, v, dec_row, *, unit, name):
    S, W = q.shape
    H = W // LANES
    T = min(ATTN_TILE, S)
    n, hT = S // T, T // ROW_SPLIT
    qi, kj, npairs = _pairs(n, by_key=False)
    has_dec = dec_row is not None

    def body(qi_ref, kj_ref, *refs):
        if has_dec:
            q_ref, k_ref, v_ref, dr_ref, o_ref, lse_ref, m_s, acc = refs
        else:
            q_ref, k_ref, v_ref, o_ref, lse_ref, m_s, acc = refs
        t = pl.program_id(1)
        i, j = qi_ref[t], kj_ref[t]

        @pl.when(j == 0)
        def _():
            m_s[...] = jnp.full_like(m_s, NEG_INF)
            acc[...] = jnp.zeros_like(acc)

        def step(diag):
            for r in range(ROW_SPLIT):
                rows = slice(r * hT, (r + 1) * hT)
                nk = (r + 1) * hT if diag else T
                s = lax.dot_general(q_ref[rows, :], k_ref[0:nk, :], _NT, preferred_element_type=F32)
                if has_dec:
                    s = s - dr_ref[:, 0:nk]
                if diag:
                    s = jnp.where(_unit_mask((hT, nk), unit, 1, r * hT), s, NEG_INF)
                m_prev = m_s[rows, :]
                m_new = jnp.maximum(m_prev, jnp.max(s, axis=1, keepdims=True))
                alpha = jnp.exp(m_prev - m_new)
                p = jnp.exp(s - jnp.tile(m_new, (1, nk // LANES)))
                acc[rows, :] = alpha * acc[rows, :] + jnp.dot(p.astype(BF16), v_ref[0:nk, :], preferred_element_type=F32)
                m_s[rows, :] = m_new

        @pl.when(j < i)
        def _():
            step(False)

        @pl.when(j == i)
        def _():
            step(True)
            av = acc[...]
            l = av[:, ONES_LANE:ONES_LANE + 1]
            lane = lax.broadcasted_iota(jnp.int32, (T, LANES), 1)
            o_ref[...] = jnp.where(lane < ONES_LANE, av / l, 0.0).astype(BF16)
            lse_ref[...] = m_s[...] + jnp.log(l)

    qb = pl.BlockSpec((T, LANES), lambda h, t, qi, kj: (qi[t], h))
    kb = pl.BlockSpec((T, LANES), lambda h, t, qi, kj: (kj[t], h))
    repq = pl.BlockSpec((None, T, LANES), lambda h, t, qi, kj: (h, qi[t], 0))
    rowk = pl.BlockSpec((None, 1, T), lambda h, t, qi, kj: (h, 0, kj[t]))
    in_specs = [qb, kb, kb] + ([rowk] if has_dec else [])
    args = (q, k, v) + ((dec_row,) if has_dec else ())
    return pl.pallas_call(
        body, name=name,
        grid_spec=pltpu.PrefetchScalarGridSpec(
            num_scalar_prefetch=2, grid=(H, npairs), in_specs=in_specs, out_specs=(qb, repq),
            scratch_shapes=[pltpu.VMEM((T, LANES), F32), pltpu.VMEM((T, LANES), F32)]),
        out_shape=(jax.ShapeDtypeStruct((S, W), BF16), jax.ShapeDtypeStruct((H, S, LANES), F32)),
        compiler_params=_cparams(("parallel", "arbitrary")),
    )(qi, kj, *args)


_TN = (((0,), (0,)), ((), ()))


def _fa_bwd_fused(q, k, v, do, o, lse, dec_row, *, unit, dq_dtype, dk_dtype, name):
    S, W = q.shape
    H = W // LANES
    T = min(ATTN_TILE, S)
    n, hT = S // T, T // ROW_SPLIT_BWD
    qi, kj, npairs = _pairs(n, by_key=False)
    has_dec = dec_row is not None

    def body(qi_ref, kj_ref, *refs):
        if has_dec:
            (q_ref, k_ref, v_ref, do_ref, o_ref, lse_ref, dr_ref, dq_ref, dk_ref, dv_ref, ddq_ref, ddk_ref,
             qacc, kacc, vacc, dl_ref, rsum, csum) = refs
        else:
            q_ref, k_ref, v_ref, do_ref, o_ref, lse_ref, dq_ref, dk_ref, dv_ref, qacc, kacc, vacc, dl_ref = refs
        t = pl.program_id(1)
        i, j = qi_ref[t], kj_ref[t]

        @pl.when(t == 0)
        def _():
            kacc[...] = jnp.zeros_like(kacc)
            vacc[...] = jnp.zeros_like(vacc)
            if has_dec:
                csum[...] = jnp.zeros_like(csum)

        @pl.when(j == 0)
        def _():
            qacc[...] = jnp.zeros_like(qacc)
            delta = jnp.sum(do_ref[...].astype(F32) * o_ref[...].astype(F32), axis=1, keepdims=True)
            dl_ref[...] = jnp.broadcast_to(delta, (T, LANES))
            if has_dec:
                rsum[...] = jnp.zeros_like(rsum)

        def step(diag):
            for r in range(ROW_SPLIT_BWD):
                rows = slice(r * hT, (r + 1) * hT)
                nk = (r + 1) * hT if diag else T
                qv, dov, kv = q_ref[rows, :], do_ref[rows, :], k_ref[0:nk, :]
                s = lax.dot_general(qv, kv, _NT, preferred_element_type=F32)
                if has_dec:
                    s = s - dr_ref[:, 0:nk]
                if diag:
                    s = jnp.where(_unit_mask((hT, nk), unit, 1, r * hT), s, NEG_INF)
                p = jnp.exp(s - jnp.tile(lse_ref[rows, :], (1, nk // LANES)))
                dp = lax.dot_general(dov, v_ref[0:nk, :], _NT, preferred_element_type=F32)
                ds = p * (dp - jnp.tile(dl_ref[rows, :], (1, nk // LANES)))
                pb, dsb = p.astype(BF16), ds.astype(BF16)
                qacc[rows, :] += jnp.dot(dsb, kv, preferred_element_type=F32)
                vacc[j, 0:nk, :] += lax.dot_general(pb, dov, _TN, preferred_element_type=F32)
                kacc[j, 0:nk, :] += lax.dot_general(dsb, qv, _TN, preferred_element_type=F32)
                if has_dec:
                    rsum[rows, :] += _lane_sum(ds)
                    csum[j, :, 0:nk] -= jnp.sum(ds, axis=0, keepdims=True)

        @pl.when(j < i)
        def _():
            step(False)

        @pl.when(j == i)
        def _():
            step(True)
            dq_ref[...] = qacc[...].astype(dq_dtype)
            if has_dec:
                ddq_ref[...] = jnp.broadcast_to(jnp.sum(rsum[...], axis=1, keepdims=True), (T, LANES))

        @pl.when(t == npairs - 1)
        def _():
            for jj in range(n):
                dk_ref[jj * T:(jj + 1) * T, :] = kacc[jj].astype(dk_dtype)
                dv_ref[jj * T:(jj + 1) * T, :] = vacc[jj].astype(BF16)
                if has_dec:
                    ddk_ref[:, jj * T:(jj + 1) * T] = csum[jj]

    qb = pl.BlockSpec((T, LANES), lambda h, t, qi, kj: (qi[t], h))
    kb = pl.BlockSpec((T, LANES), lambda h, t, qi, kj: (kj[t], h))
    head = pl.BlockSpec((S, LANES), lambda h, t, qi, kj: (0, h))
    repq = pl.BlockSpec((None, T, LANES), lambda h, t, qi, kj: (h, qi[t], 0))
    rowk = pl.BlockSpec((None, 1, T), lambda h, t, qi, kj: (h, 0, kj[t]))
    rowh = pl.BlockSpec((None, 1, S), lambda h, t, qi, kj: (h, 0, 0))
    in_specs = [qb, kb, kb, qb, qb, repq] + ([rowk] if has_dec else [])
    args = (q, k, v, do, o, lse) + ((dec_row,) if has_dec else ())
    out_specs = [qb, head, head] + ([repq, rowh] if has_dec else [])
    out_shape = [jax.ShapeDtypeStruct((S, W), dq_dtype), jax.ShapeDtypeStruct((S, W), dk_dtype), jax.ShapeDtypeStruct((S, W), BF16)]
    scratch = [pltpu.VMEM((T, LANES), F32), pltpu.VMEM((n, T, LANES), F32), pltpu.VMEM((n, T, LANES), F32),
               pltpu.VMEM((T, LANES), F32)]
    if has_dec:
        out_shape += [jax.ShapeDtypeStruct((H, S, LANES), F32), jax.ShapeDtypeStruct((H, 1, S), F32)]
        scratch += [pltpu.VMEM((T, LANES), F32), pltpu.VMEM((n, 1, T), F32)]
    res = pl.pallas_call(
        body, name=name,
        grid_spec=pltpu.PrefetchScalarGridSpec(num_scalar_prefetch=2, grid=(H, npairs), in_specs=in_specs,
                                               out_specs=tuple(out_specs), scratch_shapes=scratch),
        out_shape=tuple(out_shape),
        compiler_params=pltpu.CompilerParams(dimension_semantics=("parallel", "arbitrary"), vmem_limit_bytes=FUSED_BWD_VMEM_BYTES),
    )(qi, kj, *args)
    return res if has_dec else (res[0], res[1], res[2], None, None)


FUSED_BWD_VMEM_BYTES = 58 * 1024 * 1024


def _fa_bwd(q, k, v, o, lse, do, dec_row, *, unit, dq_dtype, dk_dtype, name):
    dq, dk, dv, dd_q, dd_k = _fa_bwd_fused(q, k, v, do, o, lse, dec_row, unit=unit, dq_dtype=dq_dtype, dk_dtype=dk_dtype, name=name)
    if dd_k is None:
        return dq, dk, dv, None
    return dq, dk, dv, jnp.max(dd_q, axis=2) + dd_k.reshape(dd_k.shape[0], dd_k.shape[2])


def _merge_fwd(ya, yb, yc, gate_logit, gate_b):
    S, D = ya.shape
    tm = min(256, S)

    def body(a_ref, b_ref, c_ref, gl_ref, gb_ref, o_ref):
        g = jax.nn.sigmoid(gl_ref[...] + gb_ref[...])
        o_ref[...] = (g[:, 0:D] * a_ref[...] + g[:, D:2 * D] * b_ref[...] + g[:, 2 * D:3 * D] * c_ref[...]).astype(BF16)

    row = pl.BlockSpec((tm, D), lambda i: (i, 0))
    return pl.pallas_call(
        body, name="merge_fwd", grid=(S // tm,),
        in_specs=[row, row, row, pl.BlockSpec((tm, 3 * D), lambda i: (i, 0)), pl.BlockSpec((1, 3 * D), lambda i: (0, 0))],
        out_specs=row, out_shape=jax.ShapeDtypeStruct((S, D), BF16), compiler_params=_cparams(("parallel",)),
    )(ya, yb, yc, gate_logit, gate_b.reshape(1, 3 * D))


def _merge_bwd(dm, ya, yb, yc, gate_logit, gate_b):
    S, D = ya.shape
    tm = min(256, S)

    def body(dm_ref, a_ref, b_ref, c_ref, gl_ref, gb_ref, da_ref, db_ref, dc_ref, dgl_ref, dgb_ref):
        g = jax.nn.sigmoid(gl_ref[...] + gb_ref[...])
        dmv = dm_ref[...]
        parts = []
        for n, (y_ref, dy_ref) in enumerate(((a_ref, da_ref), (b_ref, db_ref), (c_ref, dc_ref))):
            gn = g[:, n * D:(n + 1) * D]
            dy_ref[...] = (dmv * gn).astype(BF16)
            parts.append(dmv * y_ref[...] * gn * (1.0 - gn))
        dgl = jnp.concatenate(parts, axis=1)
        dgl_ref[...] = dgl.astype(BF16)

        @pl.when(pl.program_id(0) == 0)
        def _():
            dgb_ref[...] = jnp.zeros_like(dgb_ref)

        dgb_ref[...] += jnp.sum(dgl, axis=0, keepdims=True)

    row = pl.BlockSpec((tm, D), lambda i: (i, 0))
    wide = pl.BlockSpec((tm, 3 * D), lambda i: (i, 0))
    vec = pl.BlockSpec((1, 3 * D), lambda i: (0, 0))
    act = jax.ShapeDtypeStruct((S, D), BF16)
    da, db, dc, dgl, dgb = pl.pallas_call(
        body, name="merge_bwd", grid=(S // tm,), in_specs=[row, row, row, row, wide, vec],
        out_specs=(row, row, row, wide, vec),
        out_shape=(act, act, act, jax.ShapeDtypeStruct((S, 3 * D), BF16), jax.ShapeDtypeStruct((1, 3 * D), F32)),
        compiler_params=_cparams(("arbitrary",)),
    )(dm, ya, yb, yc, gate_logit, gate_b.reshape(1, 3 * D))
    return da, db, dc, dgl, dgb.reshape(3 * D)


def _ple_fwd(x, pre, e, g_next):
    S, D = x.shape
    tm = _rows(S)
    with_norm = g_next is not None

    def body(*refs):
        x_ref, p_ref, e_ref = refs[:3]
        xn = x_ref[...] + jax.nn.sigmoid(p_ref[...]) * e_ref[...]
        if with_norm:
            g_ref, o_ref, h_ref = refs[3:]
            rstd = lax.rsqrt(jnp.mean(xn * xn, axis=1, keepdims=True) + EPS)
            h_ref[...] = (xn * rstd * g_ref[...]).astype(BF16)
        else:
            o_ref = refs[3]
        o_ref[...] = xn

    row = pl.BlockSpec((tm, D), lambda i: (i, 0))
    xs = jax.ShapeDtypeStruct((S, D), F32)
    if not with_norm:
        return pl.pallas_call(body, name="ple_fwd_last", grid=(S // tm,), in_specs=[row, row, row], out_specs=row,
                              out_shape=xs, compiler_params=_cparams(("parallel",)))(x, pre, e), None
    return pl.pallas_call(body, name="ple_fwd", grid=(S // tm,), in_specs=[row, row, row, pl.BlockSpec((1, D), lambda i: (0, 0))],
                          out_specs=(row, row), out_shape=(xs, jax.ShapeDtypeStruct((S, D), BF16)),
                          compiler_params=_cparams(("parallel",)))(x, pre, e, g_next.reshape(1, D))


def _ple_bwd(dx, pre, e):
    S, D = dx.shape
    tm = _rows(S)

    def body(dx_ref, p_ref, e_ref, dp_ref, de_ref):
        pg = jax.nn.sigmoid(p_ref[...])
        dxv = dx_ref[...]
        dp_ref[...] = (dxv * e_ref[...] * pg * (1.0 - pg)).astype(BF16)
        de_ref[...] = (dxv * pg).astype(BF16)

    row = pl.BlockSpec((tm, D), lambda i: (i, 0))
    act = jax.ShapeDtypeStruct((S, D), BF16)
    return pl.pallas_call(body, name="ple_bwd", grid=(S // tm,), in_specs=[row, row, row], out_specs=(row, row),
                          out_shape=(act, act), compiler_params=_cparams(("parallel",)))(dx, pre, e)


def _pad_heads(w, real):
    K = w.shape[0]
    w = w.reshape(K, HEADS, real)
    return jnp.pad(w, ((0, 0), (0, 0), (0, HEAD_PAD - real))).reshape(K, HEADS * HEAD_PAD)


def _unpad_heads(w, real):
    K = w.shape[0]
    return w.reshape(K, HEADS, HEAD_PAD)[:, :, :real].reshape(K, HEADS * real)


def _pad_head_rows(w, real):
    N = w.shape[1]
    w = w.reshape(HEADS, real, N)
    return jnp.pad(w, ((0, 0), (0, HEAD_PAD - real), (0, 0))).reshape(HEADS * HEAD_PAD, N)


def _unpad_head_rows(w, real):
    N = w.shape[1]
    return w.reshape(HEADS, HEAD_PAD, N)[:, :real].reshape(HEADS * real, N)


def _block_diag(w):
    w = w.reshape(4, 2, 64, 64)
    z = jnp.zeros((4, 64, 64), w.dtype)
    top = jnp.concatenate([w[:, 0], z], axis=2)
    bot = jnp.concatenate([z, w[:, 1]], axis=2)
    return jnp.concatenate([top, bot], axis=1)


def _block_diag_t(w):
    return jnp.stack([w[:, :64, :64], w[:, 64:, 64:]], axis=1).reshape(8, 64, 64)


_IN_SPLITS = (512, 512, 384, 288, 512, 512, 512, 8, 3072)
_IN_OFF = np.concatenate([[0], np.cumsum(_IN_SPLITS)])
_KR_OFF = 64
_SEG_NAMES = ("u", "ug", "cq", "ckv", "kr", "fq", "fk", "fv", "fl", "gate")


def _in_segments(w_in):
    c = lambda n: w_in[:, int(_IN_OFF[n]):int(_IN_OFF[n + 1])]
    kv = c(3)
    kr = jnp.pad(kv[:, MLA_KV_LORA:], ((0, 0), (_KR_OFF, LANES - _KR_OFF - MLA_ROPE)))
    fl = jnp.pad(c(7), ((0, 0), (0, LANES - HEADS)))
    fq = _pad_heads(c(4), FOX_HEAD_DIM) * jnp.asarray(FOX_SCALE, w_in.dtype)
    return [c(0), c(1), c(2), kv[:, :MLA_KV_LORA], kr, fq, _pad_heads(c(5), FOX_HEAD_DIM), _pad_heads(c(6), FOX_HEAD_DIM), fl, c(8)]


def _in_unsegment(dw_p, widths):
    offs = np.concatenate([[0], np.cumsum(widths)])
    seg = [dw_p[:, int(offs[n]):int(offs[n + 1])] for n in range(len(widths))]
    u, ug, cq, ckv, kr, fq, fk, fv, fl, gate = seg
    return jnp.concatenate([
        u, ug, cq, ckv, kr[:, _KR_OFF:_KR_OFF + MLA_ROPE], _unpad_heads(fq, FOX_HEAD_DIM) * FOX_SCALE,
        _unpad_heads(fk, FOX_HEAD_DIM), _unpad_heads(fv, FOX_HEAD_DIM), fl[:, :HEADS], gate], axis=1)


def _split_wuq(wuq):
    return _pad_heads(wuq, MLA_NOPE + MLA_ROPE)


def _split_wukv(wukv):
    w = wukv.reshape(MLA_KV_LORA, HEADS, MLA_NOPE + MLA_V)
    pad = lambda t: jnp.pad(t, ((0, 0), (0, 0), (0, HEAD_PAD - t.shape[2]))).reshape(MLA_KV_LORA, HEADS * HEAD_PAD)
    return pad(w[:, :, :MLA_NOPE]), pad(w[:, :, MLA_NOPE:])


def _merge_wukv(dk_p, dv_p):
    k = dk_p.reshape(MLA_KV_LORA, HEADS, HEAD_PAD)[:, :, :MLA_NOPE]
    v = dv_p.reshape(MLA_KV_LORA, HEADS, HEAD_PAD)[:, :, :MLA_V]
    return jnp.concatenate([k, v], axis=2).reshape(MLA_KV_LORA, HEADS * (MLA_NOPE + MLA_V))


def _layer_fwd(x, h, p_i, w, g_next, tabs):
    c_q, c_k, s_lo, s_hi = tabs
    sv = {"x0": x}
    segs = _in_segments(w["w_in"])
    z = {}
    for nm, ws in zip(_SEG_NAMES, segs):
        z[nm] = _mm(h, ws, out_dtype=BF16 if nm in ("fq", "fk", "fv", "gate") else F32, bias=_ones_lane_bias() if nm == "fv" else None,
                    name="in_" + nm)
    sv.update(h=h, z=z)
    wa_bd, wx_bd = _block_diag(w["lru_wa"]).astype(BF16), _block_diag(w["lru_wx"]).astype(BF16)
    oa, xc, hs = _lru_fwd(z["u"], z["ug"], w["conv_w"], w["conv_b"], wa_bd, wx_bd, w["lru_ba"], w["lru_bx"], w["lru_lambda"])
    sv.update(oa=oa, xc=xc, hs=hs)
    qn = _rmsnorm_fwd(z["cq"], w["mla_q_norm"], "q_norm_fwd")
    kvn = _rmsnorm_fwd(z["ckv"], w["mla_kv_norm"], "kv_norm_fwd")
    wuq_p = _split_wuq(w["mla_wuq"])
    wk_p, wv_p = _split_wukv(w["mla_wukv"])
    qb = _rope_q(_mm(qn, wuq_p, name="mla_q"), c_q, s_lo, s_hi, transpose=False, out_dtype=BF16, name="rope_q")
    kb = _rope_k(_mm(kvn, wk_p, name="mla_k"), z["kr"], c_k, s_lo, s_hi)
    vb = _mm(kvn, wv_p, out_dtype=BF16, bias=_ones_lane_bias(), name="mla_v")
    ob, lse_b = _fa_fwd(qb, kb, vb, None, unit=64, name="mla_attn")
    sv.update(qn=qn, kvn=kvn, qb=qb, kb=kb, vb=vb, ob=ob, lse_b=lse_b)
    bf = jnp.pad(w["fox_bf"], (0, LANES - HEADS)).reshape(1, LANES)
    dec = _decay_fwd(z["fl"], bf)
    drow = dec[:, :HEADS].T.reshape(HEADS, 1, dec.shape[0])
    oc, lse_c = _fa_fwd(z["fq"], z["fk"], z["fv"], drow, unit=1, name="fox_attn")
    sv.update(drow=drow, oc=oc, lse_c=lse_c)
    ya = _mm(oa, w["w_br_a"], out_dtype=BF16, name="br_a")
    yb = _mm(ob, _pad_head_rows(w["w_br_b"], MLA_V), out_dtype=BF16, name="br_b")
    yc = _mm(oc, _pad_head_rows(w["w_br_c"], FOX_HEAD_DIM), out_dtype=BF16, name="br_c")
    merged = _merge_fwd(ya, yb, yc, z["gate"], w["gate_b"])
    x1, hn = _mm_res_norm(merged, w["w_o"], x, w["ffn_norm"], "w_o")
    sv.update(ya=ya, yb=yb, yc=yc, merged=merged, x1=x1)
    hf, act = _ffn_up(hn, _ffn_pair_columns(w["w_gate_up"]))
    x2, pn = _mm_res_norm(act, w["w_down"], x1, w["ple_norm"], "ffn_down")
    sv.update(hn=hn, hf=hf, act=act, x2=x2)
    pre = _mm(pn, w["w_ple_gate"], name="ple_gate")
    e = _mm(p_i, w["w_ple"], name="ple_embed")
    x3, h_next = _ple_fwd(x2, pre, e, g_next)
    sv.update(pn=pn, pre=pre, e=e, p_i=p_i)
    return x3, h_next, sv


def _layer_bwd(dx3, w, sv, tabs):
    c_q, c_k, s_lo, s_hi = tabs
    g = {}
    z = sv["z"]
    dpre, de = _ple_bwd(dx3, sv["pre"], sv["e"])
    g["w_ple"] = _mm(sv["p_i"], de, ta=True, out_dtype=BF16, name="d_w_ple")
    g["w_ple_gate"] = _mm(sv["pn"], dpre, ta=True, out_dtype=BF16, name="d_w_ple_gate")
    dx2, g["ple_norm"] = _mm_norm_bwd(dpre, w["w_ple_gate"], sv["x2"], w["ple_norm"], dx3, "ple_norm_bwd")
    g["w_down"] = _mm(sv["act"], dx2, ta=True, out_dtype=BF16, name="d_w_down")
    dhf = _ffn_down_bwd(dx2, w["w_down"], sv["hf"])
    g["w_gate_up"] = _ffn_unpair_columns(_mm(sv["hn"], dhf, ta=True, out_dtype=BF16, name="d_w_gate_up"))
    dx1, g["ffn_norm"] = _mm_norm_bwd(dhf, _ffn_pair_columns(w["w_gate_up"]), sv["x1"], w["ffn_norm"], dx2, "ffn_norm_bwd")
    g["w_o"] = _mm(sv["merged"], dx1, ta=True, out_dtype=BF16, name="d_w_o")
    dm = _mm(dx1, w["w_o"], tb=True, name="d_merged")
    dya, dyb, dyc, dgate, g["gate_b"] = _merge_bwd(dm, sv["ya"], sv["yb"], sv["yc"], z["gate"], w["gate_b"])
    wbb_p, wbc_p = _pad_head_rows(w["w_br_b"], MLA_V), _pad_head_rows(w["w_br_c"], FOX_HEAD_DIM)
    g["w_br_a"] = _mm(sv["oa"], dya, ta=True, out_dtype=BF16, name="d_w_br_a")
    g["w_br_b"] = _unpad_head_rows(_mm(sv["ob"], dyb, ta=True, out_dtype=BF16, name="d_w_br_b"), MLA_V)
    g["w_br_c"] = _unpad_head_rows(_mm(sv["oc"], dyc, ta=True, out_dtype=BF16, name="d_w_br_c"), FOX_HEAD_DIM)
    doa = _mm(dya, w["w_br_a"], tb=True, name="d_oa")
    dob = _mm(dyb, wbb_p, tb=True, out_dtype=BF16, name="d_ob")
    doc = _mm(dyc, wbc_p, tb=True, out_dtype=BF16, name="d_oc")
    dfq, dfk, dfv, d_dec = _fa_bwd(z["fq"], z["fk"], z["fv"], sv["oc"], sv["lse_c"], doc, sv["drow"],
                                   unit=1, dq_dtype=BF16, dk_dtype=BF16, name="fox_attn_bwd")
    d_dec = jnp.pad(d_dec.T, ((0, 0), (0, LANES - HEADS)))
    bf = jnp.pad(w["fox_bf"], (0, LANES - HEADS)).reshape(1, LANES)
    dfl, dbf = _decay_bwd(d_dec, z["fl"], bf)
    g["fox_bf"] = dbf[0, :HEADS]
    dqb, dkb, dvb, _ = _fa_bwd(sv["qb"], sv["kb"], sv["vb"], sv["ob"], sv["lse_b"], dob, None,
                               unit=64, dq_dtype=F32, dk_dtype=F32, name="mla_attn_bwd")
    wuq_p = _split_wuq(w["mla_wuq"])
    wk_p, wv_p = _split_wukv(w["mla_wukv"])
    dq_pre = _rope_q(dqb, c_q, s_lo, s_hi, transpose=True, out_dtype=BF16, name="rope_q_bwd")
    dkr = _rope_k_bwd(dkb, c_k, s_lo, s_hi)
    g["mla_wuq"] = _unpad_heads(_mm(sv["qn"], dq_pre, ta=True, out_dtype=BF16, name="d_wuq"), MLA_NOPE + MLA_ROPE)
    g["mla_wukv"] = _merge_wukv(_mm(sv["kvn"], dkb, ta=True, out_dtype=BF16, name="d_wuk"), _mm(sv["kvn"], dvb, ta=True, out_dtype=BF16, name="d_wuv"))
    dqn = _mm(dq_pre, wuq_p, tb=True, name="d_qn")
    dkvn = _mm(dvb, wv_p, tb=True, res=_mm(dkb, wk_p, tb=True, name="d_kvn_k"), name="d_kvn")
    dcq, g["mla_q_norm"] = _rmsnorm_bwd(z["cq"], w["mla_q_norm"], dqn, out_dtype=BF16, name="q_norm_bwd")
    dckv, g["mla_kv_norm"] = _rmsnorm_bwd(z["ckv"], w["mla_kv_norm"], dkvn, out_dtype=BF16, name="kv_norm_bwd")
    wa_bd, wx_bd = _block_diag(w["lru_wa"]).astype(BF16), _block_diag(w["lru_wx"]).astype(BF16)
    du, dug, dcw, dcb, dba, dbx, dlam, dwa, dwx = _lru_bwd(
        doa, z["u"], z["ug"], sv["xc"], sv["hs"], w["conv_w"], wa_bd, wx_bd, w["lru_ba"], w["lru_bx"], w["lru_lambda"])
    g["conv_w"], g["conv_b"], g["lru_ba"], g["lru_bx"] = dcw, dcb[0], dba[0], dbx[0]
    g["lru_lambda"] = dlam[0] * LRU_C * jax.nn.sigmoid(-w["lru_lambda"])
    g["lru_wa"], g["lru_wx"] = _block_diag_t(dwa), _block_diag_t(dwx)
    dsegs = [du, dug, dcq, dckv, dkr, dfq, dfk, dfv, dfl, dgate]
    dz = jnp.concatenate(dsegs, axis=1)
    w_in_p = jnp.concatenate(_in_segments(w["w_in"]), axis=1)
    g["w_in"] = _in_unsegment(_mm(sv["h"], dz, ta=True, out_dtype=BF16, name="d_w_in"), [d.shape[1] for d in dsegs])
    dx0, g["mix_norm"] = _mm_norm_bwd(dz, w_in_p, sv["x0"], w["mix_norm"], dx1, "mix_norm_bwd")
    return dx0, g


_LAYER_WEIGHTS = ("mix_norm", "w_in", "gate_b", "conv_w", "conv_b", "lru_wa", "lru_ba", "lru_wx", "lru_bx", "lru_lambda",
                  "mla_q_norm", "mla_wuq", "mla_kv_norm", "mla_wukv", "fox_bf", "w_br_a", "w_br_b", "w_br_c", "w_o",
                  "ffn_norm", "w_gate_up", "w_down", "ple_norm", "w_ple_gate", "w_ple")
_BIG = ("w_in", "mla_wuq", "mla_wukv", "w_br_a", "w_br_b", "w_br_c", "w_o", "w_gate_up", "w_down", "w_ple_gate", "w_ple")
_ROW_SHARDED = ("w_o", "w_down", "w_ple_gate")
_SMALL = ("mix_norm", "gate_b", "conv_b", "lru_wa", "lru_ba", "lru_wx", "lru_bx", "lru_lambda", "mla_q_norm", "mla_kv_norm",
          "fox_bf", "ffn_norm", "ple_norm")


def _local_step(x, p, layers, final_norm, target):
    tabs = _rope_tables(x.shape[0])
    saved = []
    h = _rmsnorm_fwd(x, layers[0]["mix_norm"], "mix_norm_fwd")
    for i in range(DEPTH):
        g_next = layers[i + 1]["mix_norm"] if i + 1 < DEPTH else None
        x, h, sv = _layer_fwd(x, h, p[i], layers[i], g_next, tabs)
        saved.append(sv)
    loss, dx, d_final = _loss_head(x, final_norm, target)
    grads = [None] * DEPTH
    for i in reversed(range(DEPTH)):
        dx, grads[i] = _layer_bwd(dx, layers[i], saved[i], tabs)
    return loss, dx, grads, d_final


def _hbm():
    return pl.BlockSpec(memory_space=pltpu.HBM)


def _peers(x, y):
    return [(1 - x, y), (x, 1 - y), (1 - x, 1 - y)]


def _gather_weights(arrs, name):
    n_arr = len(arrs)

    def body(*refs):
        srcs, outs = refs[:n_arr], refs[n_arr:2 * n_arr]
        send_sems, recv_sems = refs[2 * n_arr:]
        x, y, c = lax.axis_index("x"), lax.axis_index("y"), lax.axis_index("c")
        me = 2 * x + y
        peers = _peers(x, y)

        def copy(sem, src, dst, to):
            return pltpu.make_async_remote_copy(src_ref=src, dst_ref=dst, send_sem=send_sems.at[sem], recv_sem=recv_sems.at[sem],
                                                device_id=to, device_id_type=MESH)

        started = []
        for a in range(n_arr):
            for j, (px, py) in enumerate(peers):
                cp = copy(6 * a + j, srcs[a].at[c], outs[a].at[me, c], (px, py, c))
                cp.start()
                started.append(cp)
        for a in range(n_arr):
            for j, (px, py) in enumerate(peers):
                landed = outs[a].at[2 * px + py, c]
                copy(6 * a + j, srcs[a].at[c], landed, (px, py, c)).wait_recv()
                cp = copy(6 * a + 3 + j, landed, landed, (x, y, 1 - c))
                cp.start()
                started.append(cp)
        for a in range(n_arr):
            for j, (px, py) in enumerate(peers):
                copy(6 * a + 3 + j, srcs[a].at[1 - c], outs[a].at[2 * px + py, 1 - c], (x, y, 1 - c)).wait_recv()
        for cp in started:
            cp.wait_send()

    return pl.pallas_call(
        body, name=name, in_specs=[_hbm()] * n_arr, out_specs=tuple([_hbm()] * n_arr),
        out_shape=tuple(jax.ShapeDtypeStruct((4,) + t.shape, t.dtype) for t in arrs),
        scratch_shapes=[pltpu.SemaphoreType.DMA((6 * n_arr,)), pltpu.SemaphoreType.DMA((6 * n_arr,))],
    )(*arrs)


def _pair_swap_halves(g4):
    n, R, W = g4.shape
    Rh = R // 2

    def body(src_ref, out_ref, send_sem, recv_sem):
        x, y, c = lax.axis_index("x"), lax.axis_index("y"), lax.axis_index("c")
        cp = pltpu.make_async_remote_copy(src_ref=src_ref.at[:, pl.ds((1 - c) * Rh, Rh), :], dst_ref=out_ref, send_sem=send_sem,
                                          recv_sem=recv_sem, device_id=(x, y, 1 - c), device_id_type=MESH)
        cp.start()
        cp.wait()

    return pl.pallas_call(
        body, name="grad_pair_swap", in_specs=[_hbm()], out_specs=_hbm(), out_shape=jax.ShapeDtypeStruct((n, Rh, W), g4.dtype),
        scratch_shapes=[pltpu.SemaphoreType.DMA, pltpu.SemaphoreType.DMA],
    )(g4)


def _pair_add(g4, sib, c_arr):
    n, R, W = g4.shape
    Rh = R // 2
    tr = _tile_rows(Rh)
    nb = Rh // tr

    def body(c_ref, a_ref, b_ref, o_ref):
        o_ref[...] = (a_ref[...].astype(F32) + b_ref[...].astype(F32)).astype(o_ref.dtype)

    return pl.pallas_call(
        body, name="grad_pair_add",
        grid_spec=pltpu.PrefetchScalarGridSpec(
            num_scalar_prefetch=1, grid=(n, nb),
            in_specs=[pl.BlockSpec((None, tr, W), lambda s, i, c: (s, c[0] * nb + i, 0)), pl.BlockSpec((None, tr, W), lambda s, i, c: (s, i, 0))],
            out_specs=pl.BlockSpec((None, tr, W), lambda s, i, c: (s, i, 0))),
        out_shape=jax.ShapeDtypeStruct((n, Rh, W), g4.dtype), compiler_params=_cparams(("parallel", "parallel")),
    )(c_arr, g4, sib)


def _tile_rows(n):
    for t in (512, 480, 400, 320, 256, 240, 160, 128, 80, 64, 40, 32, 16, 8):
        if n % t == 0:
            return t
    return n


def _chips_exchange(part):
    n, Rh, W = part.shape

    def body(src_ref, out_ref, send_sems, recv_sems):
        x, y, c = lax.axis_index("x"), lax.axis_index("y"), lax.axis_index("c")

        def copy(j, to):
            return pltpu.make_async_remote_copy(src_ref=src_ref.at[2 * to[0] + to[1]], dst_ref=out_ref.at[j], send_sem=send_sems.at[j],
                                                recv_sem=recv_sems.at[j], device_id=(to[0], to[1], c), device_id_type=MESH)

        cps = [copy(j, peer) for j, peer in enumerate(_peers(x, y))]
        for cp in cps:
            cp.start()
        for cp in cps:
            cp.wait()

    return pl.pallas_call(
        body, name="grad_chips_exchange", in_specs=[_hbm()], out_specs=_hbm(), out_shape=jax.ShapeDtypeStruct((3, Rh, W), part.dtype),
        scratch_shapes=[pltpu.SemaphoreType.DMA((3,)), pltpu.SemaphoreType.DMA((3,))],
    )(part)


def _chips_add(part, got, k_arr, c_arr):
    n, Rh, W = part.shape
    tr = _tile_rows(Rh)
    nb = Rh // tr

    def body(k_ref, c_ref, a_ref, b_ref, o_ref):
        mine = pl.program_id(0) == c_ref[0]

        @pl.when(mine)
        def _():
            o_ref[...] = ((a_ref[...].astype(F32) + b_ref[0].astype(F32)) + b_ref[1].astype(F32)) + b_ref[2].astype(F32)

        @pl.when(jnp.logical_not(mine))
        def _():
            o_ref[...] = jnp.zeros_like(o_ref)

    return pl.pallas_call(
        body, name="grad_chips_add",
        grid_spec=pltpu.PrefetchScalarGridSpec(
            num_scalar_prefetch=2, grid=(2, nb),
            in_specs=[pl.BlockSpec((None, tr, W), lambda h, i, k, c: (k[0], i, 0)), pl.BlockSpec((3, tr, W), lambda h, i, k, c: (0, i, 0))],
            out_specs=pl.BlockSpec((tr, W), lambda h, i, k, c: (h * nb + i, 0))),
        out_shape=jax.ShapeDtypeStruct((2 * Rh, W), F32), compiler_params=_cparams(("parallel", "parallel")),
    )(k_arr, c_arr, part, got)


def _pair_gather(buf):
    R, W = buf.shape
    Rh = R // 2

    def body(src_ref, out_ref, send_sem, recv_sem):
        x, y, c = lax.axis_index("x"), lax.axis_index("y"), lax.axis_index("c")
        mine, other = pl.ds(c * Rh, Rh), pl.ds((1 - c) * Rh, Rh)
        pltpu.make_async_remote_copy(src_ref=src_ref.at[mine], dst_ref=out_ref.at[mine], send_sem=send_sem, recv_sem=recv_sem,
                                     device_id=(x, y, 1 - c), device_id_type=MESH).start()
        pltpu.make_async_remote_copy(src_ref=src_ref.at[mine], dst_ref=out_ref.at[other], send_sem=send_sem, recv_sem=recv_sem,
                                     device_id=(x, y, 1 - c), device_id_type=MESH).wait()

    return pl.pallas_call(
        body, name="grad_pair_gather", in_specs=[_hbm()], out_specs=_hbm(), out_shape=jax.ShapeDtypeStruct((R, W), buf.dtype),
        input_output_aliases={0: 0}, scratch_shapes=[pltpu.SemaphoreType.DMA, pltpu.SemaphoreType.DMA],
    )(buf)


def _gather_all(buf):
    R, W = buf.shape

    def body(src_ref, out_ref, send_sems, recv_sems, local_sem):
        x, y, c = lax.axis_index("x"), lax.axis_index("y"), lax.axis_index("c")
        me = 4 * x + 2 * y + c
        mine = pltpu.make_async_copy(src_ref, out_ref.at[me], local_sem)
        mine.start()
        rel = [((x + (r >> 2 & 1)) % 2, (y + (r >> 1 & 1)) % 2, (c + (r & 1)) % 2) for r in range(1, 8)]

        def copy(j, slot, to):
            return pltpu.make_async_remote_copy(src_ref=src_ref, dst_ref=out_ref.at[slot], send_sem=send_sems.at[j],
                                                recv_sem=recv_sems.at[j], device_id=to, device_id_type=MESH)

        sends = [copy(j, me, to) for j, to in enumerate(rel)]
        for cp in sends:
            cp.start()
        for j, to in enumerate(rel):
            copy(j, 4 * to[0] + 2 * to[1] + to[2], to).wait_recv()
        for cp in sends:
            cp.wait_send()
        mine.wait()

    return pl.pallas_call(
        body, name="small_gather", in_specs=[_hbm()], out_specs=_hbm(), out_shape=jax.ShapeDtypeStruct((8, R, W), buf.dtype),
        scratch_shapes=[pltpu.SemaphoreType.DMA((7,)), pltpu.SemaphoreType.DMA((7,)), pltpu.SemaphoreType.DMA],
    )(buf)


def _sum_slots(stack):
    n, R, W = stack.shape
    tr = _tile_rows(R)

    def body(s_ref, o_ref):
        tot = s_ref[0]
        for j in range(1, n):
            tot = tot + s_ref[j]
        o_ref[...] = tot

    return pl.pallas_call(
        body, name="small_sum", grid=(R // tr,), in_specs=[pl.BlockSpec((n, tr, W), lambda i: (0, i, 0))],
        out_specs=pl.BlockSpec((tr, W), lambda i: (i, 0)), out_shape=jax.ShapeDtypeStruct((R, W), F32),
        compiler_params=_cparams(("parallel",)),
    )(stack)


def _adamw(wp, gp, mp, vp, name):
    R, W = wp.shape
    tr = R
    for t in (1024, 512, 256, 128, 64, 32, 16, 8):
        if R % t == 0 and t * W <= 512 * 1024:
            tr = t
            break
    c1 = 1.0 - ADAM_B1 ** ADAM_STEP
    c2 = 1.0 - ADAM_B2 ** ADAM_STEP

    def body(w_ref, g_ref, m_ref, v_ref, d_ref, mo_ref, vo_ref):
        gv = g_ref[...]
        m = ADAM_B1 * m_ref[...] + (1.0 - ADAM_B1) * gv
        v = ADAM_B2 * v_ref[...] + (1.0 - ADAM_B2) * (gv * gv)
        m_hat = m / c1
        v_hat = v / c2
        d_ref[...] = -ADAM_LR * (m_hat / (jnp.sqrt(v_hat) + ADAM_EPS) + ADAM_WD * w_ref[...])
        mo_ref[...] = m
        vo_ref[...] = v

    blk = pl.BlockSpec((tr, W), lambda i: (i, 0))
    shp = jax.ShapeDtypeStruct((R, W), F32)
    return pl.pallas_call(body, name=name, grid=(R // tr,), in_specs=[blk] * 4, out_specs=(blk,) * 3, out_shape=(shp,) * 3,
                          compiler_params=_cparams(("parallel",)))(wp, gp, mp, vp)


def _pack(arrs, rows):
    flat = jnp.concatenate([a.reshape(-1) for a in arrs])
    return jnp.pad(flat, (0, rows * PACK_W - flat.shape[0])).reshape(rows, PACK_W)


def _unpack(buf, shapes):
    flat = buf.reshape(-1)
    out, off = [], 0
    for shp in shapes:
        n = int(np.prod(shp))
        out.append(flat[off:off + n].reshape(shp))
        off += n
    return out


def _rows_for(shapes, mult):
    n = sum(int(np.prod(s)) for s in shapes)
    rows = -(-n // PACK_W)
    return -(-rows // mult) * mult


def _shard_major(g, name):
    L, K, N = g.shape
    if name in _ROW_SHARDED:
        t = g.reshape(L, 4, K // 4, N).transpose(1, 0, 2, 3)
    else:
        t = g.reshape(L, K, 4, N // 4).transpose(2, 0, 1, 3)
    return t.reshape(4, -1, PACK_W)


def _join_shards(blocks, name):
    return jnp.concatenate(blocks, axis=1 if name in _ROW_SHARDED else 2)


def kernel(x, p, mix_norm, w_in, gate_b, conv_w, conv_b, lru_wa, lru_ba, lru_wx, lru_bx, lru_lambda, mla_q_norm, mla_wuq, mla_kv_norm, mla_wukv, fox_bf, w_br_a, w_br_b, w_br_c, w_o, ffn_norm, w_gate_up, w_down, ple_norm, w_ple_gate, w_ple, final_norm, loss_target, m_mix_norm, m_w_in, m_gate_b, m_conv_w, m_conv_b, m_lru_wa, m_lru_ba, m_lru_wx, m_lru_bx, m_lru_lambda, m_mla_q_norm, m_mla_wuq, m_mla_kv_norm, m_mla_wukv, m_fox_bf, m_w_br_a, m_w_br_b, m_w_br_c, m_w_o, m_ffn_norm, m_w_gate_up, m_w_down, m_ple_norm, m_w_ple_gate, m_w_ple, m_final_norm, v_mix_norm, v_w_in, v_gate_b, v_conv_w, v_conv_b, v_lru_wa, v_lru_ba, v_lru_wx, v_lru_bx, v_lru_lambda, v_mla_q_norm, v_mla_wuq, v_mla_kv_norm, v_mla_wukv, v_fox_bf, v_w_br_a, v_w_br_b, v_w_br_c, v_w_o, v_ffn_norm, v_w_gate_up, v_w_down, v_ple_norm, v_w_ple_gate, v_w_ple, v_final_norm):
    a = dict(locals())
    names = list(_LAYER_WEIGHTS) + ["final_norm"]
    W = {n: a[n] for n in names}
    M = {n: a["m_" + n] for n in names}
    V = {n: a["v_" + n] for n in names}
    ix, iy, ic = lax.axis_index("x"), lax.axis_index("y"), lax.axis_index("c")

    sharded = list(_BIG) + ["conv_w"]
    shard_shapes = [W[n].shape for n in sharded]
    R = _rows_for(shard_shapes, 64)
    mine = [W[n].astype(BF16) for n in _BIG] + [conv_w]
    gathered = _gather_weights(mine, "weight_gather")
    me = 2 * ix + iy
    gathered = [lax.dynamic_update_slice(g, t[None], (me,) + (0,) * t.ndim) for g, t in zip(gathered, mine)]
    full = {n: _join_shards([g[k] for k in range(4)], n) for n, g in zip(sharded, gathered)}
    conv_w_full = full["conv_w"]
    layers = []
    for i in range(DEPTH):
        lw = {n: W[n][i] for n in _SMALL}
        for n in _BIG:
            lw[n] = full[n][i]
        lw["conv_w"] = conv_w_full[i]
        layers.append(lw)

    loss_sum, dx, grads, d_final = _local_step(x[0], p[:, 0], layers, final_norm, loss_target[0])
    loss = lax.psum(loss_sum, ("x", "y", "c"))

    parts = [_shard_major(jnp.stack([grads[i][n] for i in range(DEPTH)]), n).astype(BF16) for n in sharded]
    g4, off = jnp.zeros((4, R, PACK_W), BF16), 0
    for t in parts:
        g4 = lax.dynamic_update_slice(g4, t, (0, off, 0))
        off += t.shape[1]
    c_arr = jnp.reshape(ic, (1,)).astype(jnp.int32)
    k_arr = jnp.reshape(2 * ix + iy, (1,)).astype(jnp.int32)
    pair = _pair_add(g4, _pair_swap_halves(g4), c_arr)
    g_pack = _pair_gather(_chips_add(pair, _chips_exchange(pair), k_arr, c_arr))
    big_out = {}
    for n, gsh in zip(sharded, _unpack(g_pack, shard_shapes)):
        view = lambda t: t.reshape(-1, t.shape[-1])
        d, nm, nv = _adamw(view(W[n]), view(gsh), view(M[n]), view(V[n]), "adamw_" + n)
        for key, arr in (("g", gsh), ("d", d), ("m", nm), ("v", nv)):
            big_out[(key, n)] = arr.reshape(W[n].shape)

    small = list(_SMALL) + ["final_norm"]
    small_shapes = [W[n].shape for n in small]
    Rs = _rows_for(small_shapes, 8)
    mine_small = [d_final if n == "final_norm" else jnp.stack([grads[i][n] for i in range(DEPTH)]) for n in small]
    sg = _sum_slots(_gather_all(_pack(mine_small, Rs)))
    small_out = {}
    for n, gsm in zip(small, _unpack(sg, small_shapes)):
        view = lambda t: t.reshape(-1, t.shape[-1])
        d, nm, nv = _adamw(view(W[n]), view(gsm), view(M[n]), view(V[n]), "adamw_" + n)
        for key, arr in (("g", gsm), ("d", d), ("m", nm), ("v", nv)):
            small_out[(key, n)] = arr.reshape(W[n].shape)

    def assemble(key, n):
        return big_out[(key, n)] if n in sharded else small_out[(key, n)]

    outs = [loss, dx[None]]
    for key in ("g", "d", "m", "v"):
        outs += [assemble(key, n) for n in names]
    return tuple(outs)
```

```python
import math

import numpy as np
import jax
import jax.numpy as jnp
from jax import lax
from jax.experimental import pallas as pl
from jax.experimental.pallas import tpu as pltpu

F32, BF16 = jnp.float32, jnp.bfloat16
MESH = pl.DeviceIdType.MESH

D_MODEL = 1024
DEPTH = 2
EPS = 1e-6
NEG_INF = -1e30
LRU_C = 8.0
CONV_WIDTH = 4
HEADS = 8
MLA_Q_LORA = 384
MLA_KV_LORA = 256
MLA_NOPE = 64
MLA_ROPE = 32
MLA_V = 64
ROPE_BASE = 10000.0
FOX_HEAD_DIM = 64
D_FF = 2816
HEAD_PAD = 128
MLA_SCALE = (MLA_NOPE + MLA_ROPE) ** -0.5
FOX_SCALE = FOX_HEAD_DIM ** -0.5

ADAM_LR, ADAM_B1, ADAM_B2, ADAM_EPS, ADAM_WD, ADAM_STEP = 0.001, 0.9, 0.999, 1e-08, 0.01, 10

VMEM_LIMIT_BYTES = 48 * 1024 * 1024
LANES = 128
PACK_W = 1024

ROW_TILE = 512
ATTN_TILE = 1024
LRU_CHUNK = 512


def _cparams(dims):
    return pltpu.CompilerParams(dimension_semantics=dims, vmem_limit_bytes=VMEM_LIMIT_BYTES)


def _tile(n, cap):
    if n <= cap:
        return n
    t = (cap // LANES) * LANES
    while t >= LANES:
        if n % t == 0:
            return t
        t -= LANES
    raise ValueError(f"no tile for {n} under {cap}")


def _rows(n):
    return min(ROW_TILE, n)


MM_VMEM_BUDGET = 36 * 1024 * 1024


def _mm_tiles(M, N, K, a_bytes, b_bytes, o_bytes, has_res):
    best, best_work = None, 0
    for tm in {_tile(M, c) for c in (1024, 512, 256)}:
        for tn in {_tile(N, c) for c in (1792, 1024, 512)}:
            for tk in {_tile(K, c) for c in (2048, 1408, 1024, 512)}:
                need = 2 * (tm * tk * a_bytes + tk * tn * b_bytes + tm * tn * o_bytes + (tm * tn * 4 if has_res else 0))
                need += tm * tn * 4 if tk < K else 0
                need += tm * tn * 4
                if need <= MM_VMEM_BUDGET and tm * tn * tk > best_work:
                    best, best_work = (tm, tn, tk), tm * tn * tk
    assert best is not None, (M, N, K)
    return best

def _mm(a, b, *, ta=False, tb=False, out_dtype=F32, res=None, bias=None, name):
    K, M = a.shape if ta else a.shape[::-1]
    N, K2 = b.shape if tb else b.shape[::-1]
    assert K == K2, (name, a.shape, b.shape)
    assert res is None or bias is None
    tm, tn, tk = _mm_tiles(M, N, K, a.dtype.itemsize, b.dtype.itemsize, jnp.dtype(out_dtype).itemsize, res is not None)
    nk = K // tk
    a_spec = pl.BlockSpec((tk, tm), lambda i, j, k: (k, i)) if ta else pl.BlockSpec((tm, tk), lambda i, j, k: (i, k))
    b_spec = pl.BlockSpec((tn, tk), lambda i, j, k: (j, k)) if tb else pl.BlockSpec((tk, tn), lambda i, j, k: (k, j))
    o_spec = pl.BlockSpec((tm, tn), lambda i, j, k: (i, j))
    dn = (((0,) if ta else (1,), (1,) if tb else (0,)), ((), ()))
    if bias is not None:
        res, r_spec = bias, pl.BlockSpec((1, tn), lambda i, j, k: (0, j))
    else:
        r_spec = o_spec
    has_res = res is not None

    def body(*refs):
        a_ref, b_ref = refs[0], refs[1]
        r_ref = refs[2] if has_res else None
        o_ref = refs[3] if has_res else refs[2]
        av, bv = a_ref[...], b_ref[...]
        if av.dtype != BF16:
            av = av.astype(BF16)
        if bv.dtype != BF16:
            bv = bv.astype(BF16)
        part = lax.dot_general(av, bv, dn, preferred_element_type=F32)

        def finish(total):
            if has_res:
                total = total + r_ref[...]
            o_ref[...] = total.astype(out_dtype)

        if nk == 1:
            finish(part)
        else:
            acc = refs[-1]
            k = pl.program_id(2)

            @pl.when(k == 0)
            def _():
                acc[...] = part

            @pl.when(k > 0)
            def _():
                acc[...] += part

            @pl.when(k == nk - 1)
            def _():
                finish(acc[...])

    in_specs = [a_spec, b_spec] + ([r_spec] if has_res else [])
    args = (a, b) + ((res,) if has_res else ())
    return pl.pallas_call(
        body, name=name, grid=(M // tm, N // tn, nk), in_specs=in_specs, out_specs=o_spec,
        out_shape=jax.ShapeDtypeStruct((M, N), out_dtype),
        scratch_shapes=[pltpu.VMEM((tm, tn), F32)] if nk > 1 else [],
        compiler_params=_cparams(("parallel", "parallel", "arbitrary")),
    )(*args)


def _mm_res_norm(a, b, res, g, name):
    M, K = a.shape
    N = b.shape[1]
    tm, tk = _tile(M, 1024), _tile(K, 1408)
    nk = K // tk

    def body(a_ref, b_ref, r_ref, g_ref, o_ref, h_ref, *scratch):
        part = jnp.dot(a_ref[...], b_ref[...], preferred_element_type=F32)

        def finish(total):
            xn = total + r_ref[...]
            o_ref[...] = xn
            rstd = lax.rsqrt(jnp.mean(xn * xn, axis=1, keepdims=True) + EPS)
            h_ref[...] = (xn * rstd * g_ref[...]).astype(BF16)

        if nk == 1:
            finish(part)
        else:
            acc = scratch[0]
            k = pl.program_id(1)

            @pl.when(k == 0)
            def _():
                acc[...] = part

            @pl.when(k > 0)
            def _():
                acc[...] += part

            @pl.when(k == nk - 1)
            def _():
                finish(acc[...])

    row = pl.BlockSpec((tm, N), lambda i, k: (i, 0))
    return pl.pallas_call(
        body, name=name, grid=(M // tm, nk),
        in_specs=[pl.BlockSpec((tm, tk), lambda i, k: (i, k)), pl.BlockSpec((tk, N), lambda i, k: (k, 0)), row,
                  pl.BlockSpec((1, N), lambda i, k: (0, 0))],
        out_specs=(row, row), out_shape=(jax.ShapeDtypeStruct((M, N), F32), jax.ShapeDtypeStruct((M, N), BF16)),
        scratch_shapes=[pltpu.VMEM((tm, N), F32)] if nk > 1 else [],
        compiler_params=_cparams(("parallel", "arbitrary")),
    )(a, b, res, g.reshape(1, N))


def _mm_norm_bwd(a, b, x, g, add, name):
    M, K = a.shape
    N = b.shape[0]
    tm, tk = _tile(M, 1024), _tile(K, 1408)
    nk = K // tk

    def body(a_ref, b_ref, x_ref, g_ref, add_ref, dx_ref, dg_ref, *scratch):
        i, k = pl.program_id(0), pl.program_id(1)
        part = lax.dot_general(a_ref[...], b_ref[...], _NT, preferred_element_type=F32)

        @pl.when(jnp.logical_and(i == 0, k == 0))
        def _():
            dg_ref[...] = jnp.zeros_like(dg_ref)

        def finish(dyv):
            xf = x_ref[...]
            rstd = lax.rsqrt(jnp.mean(xf * xf, axis=1, keepdims=True) + EPS)
            xhat = xf * rstd
            dxh = dyv * g_ref[...]
            dx_ref[...] = rstd * (dxh - xhat * jnp.mean(dxh * xhat, axis=1, keepdims=True)) + add_ref[...]
            dg_ref[...] += jnp.sum(dyv * xhat, axis=0, keepdims=True)

        if nk == 1:
            finish(part)
        else:
            acc = scratch[0]

            @pl.when(k == 0)
            def _():
                acc[...] = part

            @pl.when(k > 0)
            def _():
                acc[...] += part

            @pl.when(k == nk - 1)
            def _():
                finish(acc[...])

    row = pl.BlockSpec((tm, N), lambda i, k: (i, 0))
    vec = pl.BlockSpec((1, N), lambda i, k: (0, 0))
    dx, dg = pl.pallas_call(
        body, name=name, grid=(M // tm, nk),
        in_specs=[pl.BlockSpec((tm, tk), lambda i, k: (i, k)), pl.BlockSpec((N, tk), lambda i, k: (0, k)), row, vec, row],
        out_specs=(row, vec), out_shape=(jax.ShapeDtypeStruct((M, N), F32), jax.ShapeDtypeStruct((1, N), F32)),
        scratch_shapes=[pltpu.VMEM((tm, N), F32)] if nk > 1 else [],
        compiler_params=_cparams(("arbitrary", "arbitrary")),
    )(a, b, x, g.reshape(1, N), add)
    return dx, dg.reshape(N)


FFN_TILE = 1408
FFN_SUBTILES = ((0, 512), (512, 1024), (1024, 1408))
assert D_FF == 2 * FFN_TILE and FFN_SUBTILES[-1][1] == FFN_TILE


def _ffn_pair_columns(w_gate_up):
    F = w_gate_up.shape[-1] // 2
    parts = []
    for j in range(F // FFN_TILE):
        parts += [w_gate_up[..., j * FFN_TILE:(j + 1) * FFN_TILE], w_gate_up[..., F + j * FFN_TILE:F + (j + 1) * FFN_TILE]]
    return jnp.concatenate(parts, axis=-1)


def _ffn_unpair_columns(dw):
    F = dw.shape[-1] // 2
    n = F // FFN_TILE
    blk = [dw[..., j * FFN_TILE:(j + 1) * FFN_TILE] for j in range(2 * n)]
    return jnp.concatenate(blk[0::2] + blk[1::2], axis=-1)


def _ffn_up(hn, w_pair):
    S, D = hn.shape
    W2 = w_pair.shape[1]
    F, tf = W2 // 2, FFN_TILE
    tm = _rows(S)

    def body(h_ref, w_ref, hf_ref, act_ref):
        hv = h_ref[...]
        for lo, hi in FFN_SUBTILES:
            gt = jnp.dot(hv, w_ref[:, lo:hi], preferred_element_type=F32)
            up = jnp.dot(hv, w_ref[:, tf + lo:tf + hi], preferred_element_type=F32)
            hf_ref[:, lo:hi] = gt.astype(BF16)
            hf_ref[:, tf + lo:tf + hi] = up.astype(BF16)
            act_ref[:, lo:hi] = (gt * jax.nn.sigmoid(gt) * up).astype(BF16)

    return pl.pallas_call(
        body, name="ffn_up", grid=(S // tm, F // tf),
        in_specs=[pl.BlockSpec((tm, D), lambda i, j: (i, 0)), pl.BlockSpec((D, 2 * tf), lambda i, j: (0, j))],
        out_specs=(pl.BlockSpec((tm, 2 * tf), lambda i, j: (i, j)), pl.BlockSpec((tm, tf), lambda i, j: (i, j))),
        out_shape=(jax.ShapeDtypeStruct((S, W2), BF16), jax.ShapeDtypeStruct((S, F), BF16)),
        compiler_params=_cparams(("parallel", "parallel")),
    )(hn, w_pair)


def _ffn_down_bwd(dx, w_down, hf):
    S, D = dx.shape
    F, tf = w_down.shape[0], FFN_TILE
    tm = _rows(S)

    def body(d_ref, w_ref, h_ref, o_ref):
        dv = d_ref[...].astype(BF16)
        for lo, hi in FFN_SUBTILES:
            dact = lax.dot_general(dv, w_ref[lo:hi, :], _NT, preferred_element_type=F32)
            gt, up = h_ref[:, lo:hi].astype(F32), h_ref[:, tf + lo:tf + hi].astype(F32)
            sg = jax.nn.sigmoid(gt)
            o_ref[:, lo:hi] = (dact * up * sg * (1.0 + gt * (1.0 - sg))).astype(BF16)
            o_ref[:, tf + lo:tf + hi] = (dact * gt * sg).astype(BF16)

    pair = pl.BlockSpec((tm, 2 * tf), lambda i, j: (i, j))
    return pl.pallas_call(
        body, name="ffn_down_bwd", grid=(S // tm, F // tf),
        in_specs=[pl.BlockSpec((tm, D), lambda i, j: (i, 0)), pl.BlockSpec((tf, D), lambda i, j: (j, 0)), pair],
        out_specs=pair, out_shape=jax.ShapeDtypeStruct((S, 2 * F), BF16),
        compiler_params=_cparams(("parallel", "parallel")),
    )(dx, w_down, hf)


def _rmsnorm_fwd(x, g, name):
    S, W = x.shape
    tm = _rows(S)

    def body(x_ref, g_ref, o_ref):
        xf = x_ref[...]
        rstd = lax.rsqrt(jnp.mean(xf * xf, axis=1, keepdims=True) + EPS)
        o_ref[...] = (xf * rstd * g_ref[...]).astype(BF16)

    return pl.pallas_call(
        body, name=name, grid=(S // tm,),
        in_specs=[pl.BlockSpec((tm, W), lambda i: (i, 0)), pl.BlockSpec((1, W), lambda i: (0, 0))],
        out_specs=pl.BlockSpec((tm, W), lambda i: (i, 0)),
        out_shape=jax.ShapeDtypeStruct((S, W), BF16), compiler_params=_cparams(("parallel",)),
    )(x, g.reshape(1, W))


def _rmsnorm_bwd(x, g, dy, *, add=None, out_dtype=F32, name):
    S, W = x.shape
    tm = _rows(S)
    has_add = add is not None

    def body(*refs):
        x_ref, g_ref, dy_ref = refs[:3]
        add_ref = refs[3] if has_add else None
        dx_ref, dg_ref = refs[-2], refs[-1]
        xf = x_ref[...]
        rstd = lax.rsqrt(jnp.mean(xf * xf, axis=1, keepdims=True) + EPS)
        xhat = xf * rstd
        dyv = dy_ref[...]
        dxh = dyv * g_ref[...]
        dx = rstd * (dxh - xhat * jnp.mean(dxh * xhat, axis=1, keepdims=True))
        if has_add:
            dx = dx + add_ref[...]
        dx_ref[...] = dx.astype(out_dtype)

        @pl.when(pl.program_id(0) == 0)
        def _():
            dg_ref[...] = jnp.zeros_like(dg_ref)

        dg_ref[...] += jnp.sum(dyv * xhat, axis=0, keepdims=True)

    row = pl.BlockSpec((tm, W), lambda i: (i, 0))
    vec = pl.BlockSpec((1, W), lambda i: (0, 0))
    dx, dg = pl.pallas_call(
        body, name=name, grid=(S // tm,),
        in_specs=[row, vec, row] + ([row] if has_add else []),
        out_specs=(row, vec),
        out_shape=(jax.ShapeDtypeStruct((S, W), out_dtype), jax.ShapeDtypeStruct((1, W), F32)),
        compiler_params=_cparams(("arbitrary",)),
    )(x, g.reshape(1, W), dy, *((add,) if has_add else ()))
    return dx, dg.reshape(W)


def _loss_head(x, g, target):
    S, W = x.shape
    tm = _rows(S)

    def body(x_ref, g_ref, t_ref, loss_ref, dx_ref, dg_ref):
        xf = x_ref[...]
        gv = g_ref[...]
        rstd = lax.rsqrt(jnp.mean(xf * xf, axis=1, keepdims=True) + EPS)
        xhat = xf * rstd
        err = xhat * gv - t_ref[...]
        part = 0.5 * jnp.sum(jnp.mean(err * err, axis=1, keepdims=True), axis=0, keepdims=True)
        dyv = err * (1.0 / W)
        dxh = dyv * gv
        dx_ref[...] = rstd * (dxh - xhat * jnp.mean(dxh * xhat, axis=1, keepdims=True))

        @pl.when(pl.program_id(0) == 0)
        def _():
            dg_ref[...] = jnp.zeros_like(dg_ref)
            loss_ref[...] = jnp.zeros_like(loss_ref)

        dg_ref[...] += jnp.sum(dyv * xhat, axis=0, keepdims=True)
        loss_ref[...] += part

    row = pl.BlockSpec((tm, W), lambda i: (i, 0))
    vec = pl.BlockSpec((1, W), lambda i: (0, 0))
    loss, dx, dg = pl.pallas_call(
        body, name="loss_head", grid=(S // tm,), in_specs=[row, vec, row],
        out_specs=(pl.BlockSpec((1, 1), lambda i: (0, 0)), row, vec),
        out_shape=(jax.ShapeDtypeStruct((1, 1), F32), jax.ShapeDtypeStruct((S, W), F32), jax.ShapeDtypeStruct((1, W), F32)),
        compiler_params=_cparams(("arbitrary",)),
    )(x, g.reshape(1, W), target)
    return loss[0, 0], dx, dg.reshape(W)


def _scan_fwd(a, b, row):
    T = a.shape[0]
    d = 1
    while d < T:
        keep = row >= d
        b = jnp.where(keep, a * pltpu.roll(b, d, axis=0) + b, b)
        a = jnp.where(keep, a * pltpu.roll(a, d, axis=0), a)
        d *= 2
    return a, b


def _scan_bwd(a, b, row):
    T = a.shape[0]
    d = 1
    while d < T:
        keep = row < T - d
        b = jnp.where(keep, a * pltpu.roll(b, T - d, axis=0) + b, b)
        a = jnp.where(keep, a * pltpu.roll(a, T - d, axis=0), a)
        d *= 2
    return a, b


def _expm1(x):
    small = x * (1.0 + x * (0.5 + x * (1.0 / 6 + x * (1.0 / 24 + x * (1.0 / 120 + x * (1.0 / 720 + x * (1.0 / 5040)))))))
    return jnp.where(jnp.abs(x) < 0.25, small, jnp.exp(x) - 1.0)


_GELU_C = math.sqrt(2.0 / math.pi)


def _gelu_and_grad(x):
    inner = _GELU_C * (x + 0.044715 * x * x * x)
    th = jnp.tanh(inner)
    val = 0.5 * x * (1.0 + th)
    grad = 0.5 * (1.0 + th) + 0.5 * x * (1.0 - th * th) * _GELU_C * (1.0 + 3 * 0.044715 * x * x)
    return val, grad


def _lru_gates(xc, wa, wx, ba, bx, lam):
    xcb = xc.astype(BF16)
    r = jax.nn.sigmoid(jnp.dot(xcb, wa, preferred_element_type=F32) + ba)
    ig = jax.nn.sigmoid(jnp.dot(xcb, wx, preferred_element_type=F32) + bx)
    sp = jax.nn.softplus(-lam)
    log_a = -LRU_C * r * sp
    a = jnp.exp(log_a)
    mult = jnp.sqrt(-_expm1(2.0 * log_a))
    return xcb, r, ig, sp, a, mult


def _lru_fwd(u, ug, conv_w, conv_b, wa_bd, wx_bd, ba, bx, lam):
    S, W = u.shape
    T = min(LRU_CHUNK, S)
    nl, nc = W // LANES, S // T

    def body(u_ref, ug_ref, cw_ref, cb_ref, wa_ref, wx_ref, ba_ref, bx_ref, lam_ref, ya_ref, xc_ref, h_ref, prev_u, h_carry):
        c = pl.program_id(1)

        @pl.when(c == 0)
        def _():
            prev_u[...] = jnp.zeros_like(prev_u)
            h_carry[...] = jnp.zeros_like(h_carry)

        uv = u_ref[...]
        row = lax.broadcasted_iota(jnp.int32, (T, LANES), 0)
        row8 = lax.broadcasted_iota(jnp.int32, (8, LANES), 0)
        cw = cw_ref[...]
        xc = cb_ref[...] + uv * cw[3:4, :]
        pv = prev_u[...]
        for k in range(1, CONV_WIDTH):
            us = pltpu.roll(uv, k, axis=0)
            top = jnp.where(row8 < k, pltpu.roll(pv, k, axis=0), us[0:8])
            us = jnp.concatenate([top, us[8:]], axis=0)
            xc = xc + us * cw[3 - k:4 - k, :]
        prev_u[...] = uv[T - 8:T]
        _, r, ig, sp, a, mult = _lru_gates(xc, wa_ref[...], wx_ref[...], ba_ref[...], bx_ref[...], lam_ref[...])
        bb = mult * (ig * xc)
        aa, hh = _scan_fwd(a, bb, row)
        h = hh + aa * h_carry[7:8, :]
        h_carry[...] = h[T - 8:T]
        gl, _ = _gelu_and_grad(ug_ref[...])
        ya_ref[...] = (h * gl).astype(BF16)
        xc_ref[...] = xc
        h_ref[...] = h

    seq = pl.BlockSpec((T, LANES), lambda l, c: (c, l))
    vec = pl.BlockSpec((1, LANES), lambda l, c: (0, l))
    mat = pl.BlockSpec((None, LANES, LANES), lambda l, c: (l, 0, 0))
    return pl.pallas_call(
        body, name="lru_fwd", grid=(nl, nc),
        in_specs=[seq, seq, pl.BlockSpec((CONV_WIDTH, LANES), lambda l, c: (0, l)), vec, mat, mat, vec, vec, vec],
        out_specs=(seq, seq, seq),
        out_shape=(jax.ShapeDtypeStruct((S, W), BF16), jax.ShapeDtypeStruct((S, W), F32), jax.ShapeDtypeStruct((S, W), F32)),
        scratch_shapes=[pltpu.VMEM((8, LANES), F32), pltpu.VMEM((8, LANES), F32)],
        compiler_params=_cparams(("parallel", "arbitrary")),
    )(u, ug, conv_w, conv_b.reshape(1, W), wa_bd, wx_bd, ba.reshape(1, W), bx.reshape(1, W), lam.reshape(1, W))


def _lru_bwd(dya, u, ug, xc, h, conv_w, wa_bd, wx_bd, ba, bx, lam):
    S, W = u.shape
    T = min(LRU_CHUNK, S)
    nl, nc = W // LANES, S // T
    tb8 = T // 8

    def body(dya_ref, u_ref, ug_ref, xc_ref, h_ref, hp_ref, cw_ref, wa_ref, wx_ref, ba_ref, bx_ref, lam_ref,
             du_ref, dug_ref, dcw_ref, dcb_ref, dba_ref, dbx_ref, dlam_ref, dwa_ref, dwx_ref,
             g_next, a_next, dxc_next):
        c = pl.program_id(1)

        @pl.when(c == 0)
        def _():
            g_next[...] = jnp.zeros_like(g_next)
            a_next[...] = jnp.zeros_like(a_next)
            dxc_next[...] = jnp.zeros_like(dxc_next)
            for ref in (dcw_ref, dcb_ref, dba_ref, dbx_ref, dlam_ref, dwa_ref, dwx_ref):
                ref[...] = jnp.zeros_like(ref)

        row = lax.broadcasted_iota(jnp.int32, (T, LANES), 0)
        row8 = lax.broadcasted_iota(jnp.int32, (8, LANES), 0)
        xcv = xc_ref[...]
        wa, wx = wa_ref[...], wx_ref[...]
        xcb, r, ig, sp, a, mult = _lru_gates(xcv, wa, wx, ba_ref[...], bx_ref[...], lam_ref[...])
        gl, dgl = _gelu_and_grad(ug_ref[...])
        dyav = dya_ref[...]
        hv = h_ref[...]
        dug_ref[...] = (dyav * hv * dgl).astype(BF16)
        dh = dyav * gl
        a_up = pltpu.roll(a, T - 1, axis=0)
        a_up = jnp.where(row == T - 1, a_next[0:1, :], a_up)
        prod, gg = _scan_bwd(a_up, dh, row)
        g = gg + prod * g_next[0:1, :]
        h_prev = pltpu.roll(hv, 1, axis=0)
        first_chunk = c == nc - 1
        h_before = jnp.where(first_chunk, 0.0, hp_ref[7:8, :])
        h_prev = jnp.where(row == 0, h_before, h_prev)
        da = g * h_prev
        d_mult = g * (ig * xcv)
        d_ig = g * mult * xcv
        dxc = g * mult * ig
        d_log_a = da * a - d_mult * (a * a) / mult
        d_r = d_log_a * (-LRU_C * sp)
        d_pa = d_r * r * (1.0 - r)
        d_px = d_ig * ig * (1.0 - ig)
        d_pab, d_pxb = d_pa.astype(BF16), d_px.astype(BF16)
        nt = (((1,), (1,)), ((), ()))
        tn = (((0,), (0,)), ((), ()))
        dxc = dxc + lax.dot_general(d_pab, wa, nt, preferred_element_type=F32) + lax.dot_general(d_pxb, wx, nt, preferred_element_type=F32)
        dwa_ref[...] += lax.dot_general(xcb, d_pab, tn, preferred_element_type=F32)
        dwx_ref[...] += lax.dot_general(xcb, d_pxb, tn, preferred_element_type=F32)
        dlam_ref[...] += jnp.sum(d_log_a * r, axis=0, keepdims=True)
        dba_ref[...] += jnp.sum(d_pa, axis=0, keepdims=True)
        dbx_ref[...] += jnp.sum(d_px, axis=0, keepdims=True)
        dcb_ref[...] += jnp.sum(dxc, axis=0, keepdims=True)
        uv = u_ref[...]
        cw = cw_ref[...]
        nxt = dxc_next[...]
        du = dxc * cw[3:4, :]
        dcw_ref[3:4, :] += jnp.sum(uv * dxc, axis=0, keepdims=True)
        for k in range(1, CONV_WIDTH):
            ds = pltpu.roll(dxc, T - k, axis=0)
            bot = jnp.where(row8 >= 8 - k, pltpu.roll(nxt, 8 - k, axis=0), ds[T - 8:T])
            ds = jnp.concatenate([ds[:T - 8], bot], axis=0)
            du = du + ds * cw[3 - k:4 - k, :]
            dcw_ref[3 - k:4 - k, :] += jnp.sum(uv * ds, axis=0, keepdims=True)
        du_ref[...] = du.astype(BF16)
        g_next[...] = g[0:8]
        a_next[...] = a[0:8]
        dxc_next[...] = dxc[0:8]

    seq = pl.BlockSpec((T, LANES), lambda l, c: (nc - 1 - c, l))
    before = pl.BlockSpec((8, LANES), lambda l, c: (jnp.maximum((nc - 1 - c) * tb8 - 1, 0), l))
    vec = pl.BlockSpec((1, LANES), lambda l, c: (0, l))
    cwb = pl.BlockSpec((CONV_WIDTH, LANES), lambda l, c: (0, l))
    mat = pl.BlockSpec((None, LANES, LANES), lambda l, c: (l, 0, 0))
    vshape = jax.ShapeDtypeStruct((1, W), F32)
    mshape = jax.ShapeDtypeStruct((nl, LANES, LANES), F32)
    return pl.pallas_call(
        body, name="lru_bwd", grid=(nl, nc),
        in_specs=[seq, seq, seq, seq, seq, before, cwb, mat, mat, vec, vec, vec],
        out_specs=(seq, seq, cwb, vec, vec, vec, vec, mat, mat),
        out_shape=(jax.ShapeDtypeStruct((S, W), BF16), jax.ShapeDtypeStruct((S, W), BF16),
                   jax.ShapeDtypeStruct((CONV_WIDTH, W), F32), vshape, vshape, vshape, vshape, mshape, mshape),
        scratch_shapes=[pltpu.VMEM((8, LANES), F32)] * 3,
        compiler_params=_cparams(("parallel", "arbitrary")),
    )(dya, u, ug, xc, h, h, conv_w, wa_bd, wx_bd, ba.reshape(1, W), bx.reshape(1, W), lam.reshape(1, W))


def _decay_fwd(f_logit, bf):
    S = f_logit.shape[0]
    T = min(LRU_CHUNK, S)

    def body(f_ref, b_ref, o_ref, carry):
        @pl.when(pl.program_id(0) == 0)
        def _():
            carry[...] = jnp.zeros_like(carry)

        row = lax.broadcasted_iota(jnp.int32, (T, LANES), 0)
        v = jax.nn.log_sigmoid(f_ref[...] + b_ref[...])
        d = 1
        while d < T:
            v = jnp.where(row >= d, v + pltpu.roll(v, d, axis=0), v)
            d *= 2
        v = v + carry[7:8, :]
        carry[...] = v[T - 8:T]
        o_ref[...] = v

    return pl.pallas_call(
        body, name="decay_fwd", grid=(S // T,),
        in_specs=[pl.BlockSpec((T, LANES), lambda c: (c, 0)), pl.BlockSpec((1, LANES), lambda c: (0, 0))],
        out_specs=pl.BlockSpec((T, LANES), lambda c: (c, 0)),
        out_shape=jax.ShapeDtypeStruct((S, LANES), F32), scratch_shapes=[pltpu.VMEM((8, LANES), F32)],
        compiler_params=_cparams(("arbitrary",)),
    )(f_logit, bf)


def _decay_bwd(d_dec, f_logit, bf):
    S = f_logit.shape[0]
    T = min(LRU_CHUNK, S)
    nc = S // T

    def body(dd_ref, f_ref, b_ref, df_ref, db_ref, carry):
        @pl.when(pl.program_id(0) == 0)
        def _():
            carry[...] = jnp.zeros_like(carry)
            db_ref[...] = jnp.zeros_like(db_ref)

        row = lax.broadcasted_iota(jnp.int32, (T, LANES), 0)
        v = dd_ref[...]
        d = 1
        while d < T:
            v = jnp.where(row < T - d, v + pltpu.roll(v, T - d, axis=0), v)
            d *= 2
        v = v + carry[0:1, :]
        carry[...] = v[0:8]
        df = v * jax.nn.sigmoid(-(f_ref[...] + b_ref[...]))
        df_ref[...] = df.astype(BF16)
        db_ref[...] += jnp.sum(df, axis=0, keepdims=True)

    seq = pl.BlockSpec((T, LANES), lambda c: (nc - 1 - c, 0))
    vec = pl.BlockSpec((1, LANES), lambda c: (0, 0))
    return pl.pallas_call(
        body, name="decay_bwd", grid=(nc,), in_specs=[seq, seq, vec], out_specs=(seq, vec),
        out_shape=(jax.ShapeDtypeStruct((S, LANES), BF16), jax.ShapeDtypeStruct((1, LANES), F32)),
        scratch_shapes=[pltpu.VMEM((8, LANES), F32)], compiler_params=_cparams(("arbitrary",)),
    )(d_dec, f_logit, bf)


def _rope_tables(S):
    pos = jnp.arange(S, dtype=F32)
    inv_freq = ROPE_BASE ** (-jnp.arange(0, MLA_ROPE, 2, dtype=F32) / MLA_ROPE)
    ang = pos[:, None] * inv_freq[None, :]
    cos, sin = jnp.cos(ang), jnp.sin(ang)
    half = MLA_ROPE // 2
    z = lambda n: jnp.zeros((S, n), F32)
    c_q = jnp.concatenate([jnp.ones((S, MLA_NOPE), F32), cos, cos, z(HEAD_PAD - MLA_NOPE - MLA_ROPE)], axis=1)
    c_k = jnp.concatenate([z(MLA_NOPE), cos, cos, z(HEAD_PAD - MLA_NOPE - MLA_ROPE)], axis=1)
    s_lo = jnp.concatenate([z(MLA_NOPE), -sin, z(HEAD_PAD - MLA_NOPE - half)], axis=1)
    s_hi = jnp.concatenate([z(MLA_NOPE + half), sin, z(HEAD_PAD - MLA_NOPE - MLA_ROPE)], axis=1)
    return c_q, c_k, s_lo, s_hi


def _rot(v, c, s_lo, s_hi):
    half = MLA_ROPE // 2
    return v * c + pltpu.roll(v, LANES - half, axis=1) * s_lo + pltpu.roll(v, half, axis=1) * s_hi


def _rot_t(dv, c, s_lo, s_hi):
    half = MLA_ROPE // 2
    return dv * c + pltpu.roll(dv * s_lo, half, axis=1) + pltpu.roll(dv * s_hi, LANES - half, axis=1)


def _rope_q(q_pre, c_q, s_lo, s_hi, *, transpose, out_dtype, name):
    S, W = q_pre.shape
    tm = _rows(S)
    fn = _rot_t if transpose else _rot

    def body(q_ref, c_ref, lo_ref, hi_ref, o_ref):
        c, lo, hi = c_ref[...], lo_ref[...], hi_ref[...]
        for hd in range(W // LANES):
            cols = slice(hd * LANES, (hd + 1) * LANES)
            o_ref[:, cols] = fn(q_ref[:, cols] * MLA_SCALE, c, lo, hi).astype(out_dtype)

    blk = pl.BlockSpec((tm, W), lambda i: (i, 0))
    tab = pl.BlockSpec((tm, LANES), lambda i: (i, 0))
    return pl.pallas_call(
        body, name=name, grid=(S // tm,), in_specs=[blk, tab, tab, tab], out_specs=blk,
        out_shape=jax.ShapeDtypeStruct((S, W), out_dtype), compiler_params=_cparams(("parallel",)),
    )(q_pre, c_q, s_lo, s_hi)


def _rope_k(k_pre, k_rope, c_k, s_lo, s_hi):
    S, W = k_pre.shape
    tm = _rows(S)

    def body(k_ref, r_ref, c_ref, lo_ref, hi_ref, o_ref):
        rot = _rot(r_ref[...], c_ref[...], lo_ref[...], hi_ref[...])
        for hd in range(W // LANES):
            cols = slice(hd * LANES, (hd + 1) * LANES)
            o_ref[:, cols] = (k_ref[:, cols] + rot).astype(BF16)

    blk = pl.BlockSpec((tm, W), lambda i: (i, 0))
    tab = pl.BlockSpec((tm, LANES), lambda i: (i, 0))
    return pl.pallas_call(
        body, name="rope_k", grid=(S // tm,), in_specs=[blk, tab, tab, tab, tab], out_specs=blk,
        out_shape=jax.ShapeDtypeStruct((S, W), BF16), compiler_params=_cparams(("parallel",)),
    )(k_pre, k_rope, c_k, s_lo, s_hi)


def _rope_k_bwd(dk, c_k, s_lo, s_hi):
    S, W = dk.shape
    tm = _rows(S)

    def body(dk_ref, c_ref, lo_ref, hi_ref, o_ref):
        tot = dk_ref[:, 0:LANES]
        for hd in range(1, W // LANES):
            tot = tot + dk_ref[:, hd * LANES:(hd + 1) * LANES]
        o_ref[...] = _rot_t(tot, c_ref[...], lo_ref[...], hi_ref[...]).astype(BF16)

    tab = pl.BlockSpec((tm, LANES), lambda i: (i, 0))
    return pl.pallas_call(
        body, name="rope_k_bwd", grid=(S // tm,), in_specs=[pl.BlockSpec((tm, W), lambda i: (i, 0)), tab, tab, tab],
        out_specs=tab, out_shape=jax.ShapeDtypeStruct((S, LANES), BF16), compiler_params=_cparams(("parallel",)),
    )(dk, c_k, s_lo, s_hi)


def _pairs(n):
    pr = [(i, j) for i in range(n) for j in range(i + 1)]
    return (jnp.asarray(np.array([p[0] for p in pr], np.int32)), jnp.asarray(np.array([p[1] for p in pr], np.int32)), len(pr))


def _unit_mask(shape, unit, q_off=0):
    q = lax.broadcasted_iota(jnp.int32, shape, 0) + q_off
    k = lax.broadcasted_iota(jnp.int32, shape, 1)
    if unit > 1:
        q, k = q // unit, k // unit
    return q >= k


_NT = (((1,), (1,)), ((), ()))


ONES_LANE = 64
ROW_SPLIT = 1
ROW_SPLIT_BWD = 4


def _ones_lane_bias():
    one = np.zeros((HEADS, HEAD_PAD), np.float32)
    one[:, ONES_LANE] = 1.0
    return jnp.asarray(one.reshape(1, HEADS * HEAD_PAD))


def _lane_sum(t):
    tot = t[:, 0:LANES]
    for c in range(1, t.shape[1] // LANES):
        tot = tot + t[:, c * LANES:(c + 1) * LANES]
    return tot


def _fa_fwd(q, k, v, dec_row, *, unit, name):
    S, W = q.shape
    H = W // LANES
    T = min(ATTN_TILE, S)
    n, hT = S // T, T // ROW_SPLIT
    qi, kj, npairs = _pairs(n)
    has_dec = dec_row is not None

    def body(qi_ref, kj_ref, *refs):
        if has_dec:
            q_ref, k_ref, v_ref, dr_ref, o_ref, lse_ref, m_s, acc = refs
        else:
            q_ref, k_ref, v_ref, o_ref, lse_ref, m_s, acc = refs
        t = pl.program_id(1)
        i, j = qi_ref[t], kj_ref[t]

        @pl.when(j == 0)
        def _():
            m_s[...] = jnp.full_like(m_s, NEG_INF)
            acc[...] = jnp.zeros_like(acc)

        def step(diag):
            for r in range(ROW_SPLIT):
                rows = slice(r * hT, (r + 1) * hT)
                nk = (r + 1) * hT if diag else T
                s = lax.dot_general(q_ref[rows, :], k_ref[0:nk, :], _NT, preferred_element_type=F32)
                if has_dec:
                    s = s - dr_ref[:, 0:nk]
                if diag:
                    s = jnp.where(_unit_mask((hT, nk), unit, r * hT), s, NEG_INF)
                m_prev = m_s[rows, :]
                m_new = jnp.maximum(m_prev, jnp.max(s, axis=1, keepdims=True))
                alpha = jnp.exp(m_prev - m_new)
                p = jnp.exp(s - jnp.tile(m_new, (1, nk // LANES)))
                acc[rows, :] = alpha * acc[rows, :] + jnp.dot(p.astype(BF16), v_ref[0:nk, :], preferred_element_type=F32)
                m_s[rows, :] = m_new

        @pl.when(j < i)
        def _():
            step(False)

        @pl.when(j == i)
        def _():
            step(True)
            av = acc[...]
            l = av[:, ONES_LANE:ONES_LANE + 1]
            lane = lax.broadcasted_iota(jnp.int32, (T, LANES), 1)
            o_ref[...] = jnp.where(lane < ONES_LANE, av / l, 0.0).astype(BF16)
            lse_ref[...] = m_s[...] + jnp.log(l)

    qb = pl.BlockSpec((T, LANES), lambda h, t, qi, kj: (qi[t], h))
    kb = pl.BlockSpec((T, LANES), lambda h, t, qi, kj: (kj[t], h))
    repq = pl.BlockSpec((None, T, LANES), lambda h, t, qi, kj: (h, qi[t], 0))
    rowk = pl.BlockSpec((None, 1, T), lambda h, t, qi, kj: (h, 0, kj[t]))
    in_specs = [qb, kb, kb] + ([rowk] if has_dec else [])
    args = (q, k, v) + ((dec_row,) if has_dec else ())
    return pl.pallas_call(
        body, name=name,
        grid_spec=pltpu.PrefetchScalarGridSpec(
            num_scalar_prefetch=2, grid=(H, npairs), in_specs=in_specs, out_specs=(qb, repq),
            scratch_shapes=[pltpu.VMEM((T, LANES), F32), pltpu.VMEM((T, LANES), F32)]),
        out_shape=(jax.ShapeDtypeStruct((S, W), BF16), jax.ShapeDtypeStruct((H, S, LANES), F32)),
        compiler_params=_cparams(("parallel", "arbitrary")),
    )(qi, kj, *args)


_TN = (((0,), (0,)), ((), ()))


def _fa_bwd_fused(q, k, v, do, o, lse, dec_row, *, unit, dq_dtype, dk_dtype, name):
    S, W = q.shape
    H = W // LANES
    T = min(ATTN_TILE, S)
    n, hT = S // T, T // ROW_SPLIT_BWD
    qi, kj, npairs = _pairs(n)
    has_dec = dec_row is not None

    def body(qi_ref, kj_ref, *refs):
        if has_dec:
            (q_ref, k_ref, v_ref, do_ref, o_ref, lse_ref, dr_ref, dq_ref, dk_ref, dv_ref, ddq_ref, ddk_ref,
             qacc, kacc, vacc, dl_ref, rsum, csum) = refs
        else:
            q_ref, k_ref, v_ref, do_ref, o_ref, lse_ref, dq_ref, dk_ref, dv_ref, qacc, kacc, vacc, dl_ref = refs
        t = pl.program_id(1)
        i, j = qi_ref[t], kj_ref[t]

        @pl.when(t == 0)
        def _():
            kacc[...] = jnp.zeros_like(kacc)
            vacc[...] = jnp.zeros_like(vacc)
            if has_dec:
                csum[...] = jnp.zeros_like(csum)

        @pl.when(j == 0)
        def _():
            qacc[...] = jnp.zeros_like(qacc)
            delta = jnp.sum(do_ref[...].astype(F32) * o_ref[...].astype(F32), axis=1, keepdims=True)
            dl_ref[...] = jnp.broadcast_to(delta, (T, LANES))
            if has_dec:
                rsum[...] = jnp.zeros_like(rsum)

        def step(diag):
            for r in range(ROW_SPLIT_BWD):
                rows = slice(r * hT, (r + 1) * hT)
                nk = (r + 1) * hT if diag else T
                qv, dov, kv = q_ref[rows, :], do_ref[rows, :], k_ref[0:nk, :]
                s = lax.dot_general(qv, kv, _NT, preferred_element_type=F32)
                if has_dec:
                    s = s - dr_ref[:, 0:nk]
                if diag:
                    s = jnp.where(_unit_mask((hT, nk), unit, r * hT), s, NEG_INF)
                p = jnp.exp(s - jnp.tile(lse_ref[rows, :], (1, nk // LANES)))
                dp = lax.dot_general(dov, v_ref[0:nk, :], _NT, preferred_element_type=F32)
                ds = p * (dp - jnp.tile(dl_ref[rows, :], (1, nk // LANES)))
                pb, dsb = p.astype(BF16), ds.astype(BF16)
                qacc[rows, :] += jnp.dot(dsb, kv, preferred_element_type=F32)
                vacc[j, 0:nk, :] += lax.dot_general(pb, dov, _TN, preferred_element_type=F32)
                kacc[j, 0:nk, :] += lax.dot_general(dsb, qv, _TN, preferred_element_type=F32)
                if has_dec:
                    rsum[rows, :] += _lane_sum(ds)
                    csum[j, :, 0:nk] -= jnp.sum(ds, axis=0, keepdims=True)

        @pl.when(j < i)
        def _():
            step(False)

        @pl.when(j == i)
        def _():
            step(True)
            dq_ref[...] = qacc[...].astype(dq_dtype)
            if has_dec:
                ddq_ref[...] = jnp.broadcast_to(jnp.sum(rsum[...], axis=1, keepdims=True), (T, LANES))

        @pl.when(t == npairs - 1)
        def _():
            for jj in range(n):
                dk_ref[jj * T:(jj + 1) * T, :] = kacc[jj].astype(dk_dtype)
                dv_ref[jj * T:(jj + 1) * T, :] = vacc[jj].astype(BF16)
                if has_dec:
                    ddk_ref[:, jj * T:(jj + 1) * T] = csum[jj]

    qb = pl.BlockSpec((T, LANES), lambda h, t, qi, kj: (qi[t], h))
    kb = pl.BlockSpec((T, LANES), lambda h, t, qi, kj: (kj[t], h))
    head = pl.BlockSpec((S, LANES), lambda h, t, qi, kj: (0, h))
    repq = pl.BlockSpec((None, T, LANES), lambda h, t, qi, kj: (h, qi[t], 0))
    rowk = pl.BlockSpec((None, 1, T), lambda h, t, qi, kj: (h, 0, kj[t]))
    rowh = pl.BlockSpec((None, 1, S), lambda h, t, qi, kj: (h, 0, 0))
    in_specs = [qb, kb, kb, qb, qb, repq] + ([rowk] if has_dec else [])
    args = (q, k, v, do, o, lse) + ((dec_row,) if has_dec else ())
    out_specs = [qb, head, head] + ([repq, rowh] if has_dec else [])
    out_shape = [jax.ShapeDtypeStruct((S, W), dq_dtype), jax.ShapeDtypeStruct((S, W), dk_dtype), jax.ShapeDtypeStruct((S, W), BF16)]
    scratch = [pltpu.VMEM((T, LANES), F32), pltpu.VMEM((n, T, LANES), F32), pltpu.VMEM((n, T, LANES), F32),
               pltpu.VMEM((T, LANES), F32)]
    if has_dec:
        out_shape += [jax.ShapeDtypeStruct((H, S, LANES), F32), jax.ShapeDtypeStruct((H, 1, S), F32)]
        scratch += [pltpu.VMEM((T, LANES), F32), pltpu.VMEM((n, 1, T), F32)]
    res = pl.pallas_call(
        body, name=name,
        grid_spec=pltpu.PrefetchScalarGridSpec(num_scalar_prefetch=2, grid=(H, npairs), in_specs=in_specs,
                                               out_specs=tuple(out_specs), scratch_shapes=scratch),
        out_shape=tuple(out_shape),
        compiler_params=pltpu.CompilerParams(dimension_semantics=("parallel", "arbitrary"), vmem_limit_bytes=FUSED_BWD_VMEM_BYTES),
    )(qi, kj, *args)
    return res if has_dec else (res[0], res[1], res[2], None, None)


FUSED_BWD_VMEM_BYTES = 58 * 1024 * 1024


def _fa_bwd(q, k, v, o, lse, do, dec_row, *, unit, dq_dtype, dk_dtype, name):
    dq, dk, dv, dd_q, dd_k = _fa_bwd_fused(q, k, v, do, o, lse, dec_row, unit=unit, dq_dtype=dq_dtype, dk_dtype=dk_dtype, name=name)
    if dd_k is None:
        return dq, dk, dv, None
    return dq, dk, dv, jnp.max(dd_q, axis=2) + dd_k.reshape(dd_k.shape[0], dd_k.shape[2])


def _merge_fwd(ya, yb, yc, gate_logit, gate_b):
    S, D = ya.shape
    tm = min(256, S)

    def body(a_ref, b_ref, c_ref, gl_ref, gb_ref, o_ref):
        g = jax.nn.sigmoid(gl_ref[...] + gb_ref[...])
        o_ref[...] = (g[:, 0:D] * a_ref[...] + g[:, D:2 * D] * b_ref[...] + g[:, 2 * D:3 * D] * c_ref[...]).astype(BF16)

    row = pl.BlockSpec((tm, D), lambda i: (i, 0))
    return pl.pallas_call(
        body, name="merge_fwd", grid=(S // tm,),
        in_specs=[row, row, row, pl.BlockSpec((tm, 3 * D), lambda i: (i, 0)), pl.BlockSpec((1, 3 * D), lambda i: (0, 0))],
        out_specs=row, out_shape=jax.ShapeDtypeStruct((S, D), BF16), compiler_params=_cparams(("parallel",)),
    )(ya, yb, yc, gate_logit, gate_b.reshape(1, 3 * D))


def _merge_bwd(dm, ya, yb, yc, gate_logit, gate_b):
    S, D = ya.shape
    tm = min(256, S)

    def body(dm_ref, a_ref, b_ref, c_ref, gl_ref, gb_ref, da_ref, db_ref, dc_ref, dgl_ref, dgb_ref):
        g = jax.nn.sigmoid(gl_ref[...] + gb_ref[...])
        dmv = dm_ref[...]
        parts = []
        for n, (y_ref, dy_ref) in enumerate(((a_ref, da_ref), (b_ref, db_ref), (c_ref, dc_ref))):
            gn = g[:, n * D:(n + 1) * D]
            dy_ref[...] = (dmv * gn).astype(BF16)
            parts.append(dmv * y_ref[...] * gn * (1.0 - gn))
        dgl = jnp.concatenate(parts, axis=1)
        dgl_ref[...] = dgl.astype(BF16)

        @pl.when(pl.program_id(0) == 0)
        def _():
            dgb_ref[...] = jnp.zeros_like(dgb_ref)

        dgb_ref[...] += jnp.sum(dgl, axis=0, keepdims=True)

    row = pl.BlockSpec((tm, D), lambda i: (i, 0))
    wide = pl.BlockSpec((tm, 3 * D), lambda i: (i, 0))
    vec = pl.BlockSpec((1, 3 * D), lambda i: (0, 0))
    act = jax.ShapeDtypeStruct((S, D), BF16)
    da, db, dc, dgl, dgb = pl.pallas_call(
        body, name="merge_bwd", grid=(S // tm,), in_specs=[row, row, row, row, wide, vec],
        out_specs=(row, row, row, wide, vec),
        out_shape=(act, act, act, jax.ShapeDtypeStruct((S, 3 * D), BF16), jax.ShapeDtypeStruct((1, 3 * D), F32)),
        compiler_params=_cparams(("arbitrary",)),
    )(dm, ya, yb, yc, gate_logit, gate_b.reshape(1, 3 * D))
    return da, db, dc, dgl, dgb.reshape(3 * D)


def _ple_fwd(x, pre, e, g_next):
    S, D = x.shape
    tm = _rows(S)
    with_norm = g_next is not None

    def body(*refs):
        x_ref, p_ref, e_ref = refs[:3]
        xn = x_ref[...] + jax.nn.sigmoid(p_ref[...]) * e_ref[...]
        if with_norm:
            g_ref, o_ref, h_ref = refs[3:]
            rstd = lax.rsqrt(jnp.mean(xn * xn, axis=1, keepdims=True) + EPS)
            h_ref[...] = (xn * rstd * g_ref[...]).astype(BF16)
        else:
            o_ref = refs[3]
        o_ref[...] = xn

    row = pl.BlockSpec((tm, D), lambda i: (i, 0))
    xs = jax.ShapeDtypeStruct((S, D), F32)
    if not with_norm:
        return pl.pallas_call(body, name="ple_fwd_last", grid=(S // tm,), in_specs=[row, row, row], out_specs=row,
                              out_shape=xs, compiler_params=_cparams(("parallel",)))(x, pre, e), None
    return pl.pallas_call(body, name="ple_fwd", grid=(S // tm,), in_specs=[row, row, row, pl.BlockSpec((1, D), lambda i: (0, 0))],
                          out_specs=(row, row), out_shape=(xs, jax.ShapeDtypeStruct((S, D), BF16)),
                          compiler_params=_cparams(("parallel",)))(x, pre, e, g_next.reshape(1, D))


def _ple_bwd(dx, pre, e):
    S, D = dx.shape
    tm = _rows(S)

    def body(dx_ref, p_ref, e_ref, dp_ref, de_ref):
        pg = jax.nn.sigmoid(p_ref[...])
        dxv = dx_ref[...]
        dp_ref[...] = (dxv * e_ref[...] * pg * (1.0 - pg)).astype(BF16)
        de_ref[...] = (dxv * pg).astype(BF16)

    row = pl.BlockSpec((tm, D), lambda i: (i, 0))
    act = jax.ShapeDtypeStruct((S, D), BF16)
    return pl.pallas_call(body, name="ple_bwd", grid=(S // tm,), in_specs=[row, row, row], out_specs=(row, row),
                          out_shape=(act, act), compiler_params=_cparams(("parallel",)))(dx, pre, e)


def _pad_heads(w, real):
    K = w.shape[0]
    w = w.reshape(K, HEADS, real)
    return jnp.pad(w, ((0, 0), (0, 0), (0, HEAD_PAD - real))).reshape(K, HEADS * HEAD_PAD)


def _unpad_heads(w, real):
    K = w.shape[0]
    return w.reshape(K, HEADS, HEAD_PAD)[:, :, :real].reshape(K, HEADS * real)


def _pad_head_rows(w, real):
    N = w.shape[1]
    w = w.reshape(HEADS, real, N)
    return jnp.pad(w, ((0, 0), (0, HEAD_PAD - real), (0, 0))).reshape(HEADS * HEAD_PAD, N)


def _unpad_head_rows(w, real):
    N = w.shape[1]
    return w.reshape(HEADS, HEAD_PAD, N)[:, :real].reshape(HEADS * real, N)


def _block_diag(w):
    w = w.reshape(4, 2, 64, 64)
    z = jnp.zeros((4, 64, 64), w.dtype)
    top = jnp.concatenate([w[:, 0], z], axis=2)
    bot = jnp.concatenate([z, w[:, 1]], axis=2)
    return jnp.concatenate([top, bot], axis=1)


def _block_diag_t(w):
    return jnp.stack([w[:, :64, :64], w[:, 64:, 64:]], axis=1).reshape(8, 64, 64)


_IN_SPLITS = (512, 512, 384, 288, 512, 512, 512, 8, 3072)
_IN_OFF = np.concatenate([[0], np.cumsum(_IN_SPLITS)])
_KR_OFF = 64
_SEG_NAMES = ("u", "ug", "cq", "ckv", "kr", "fq", "fk", "fv", "fl", "gate")


def _in_segments(w_in):
    c = lambda n: w_in[:, int(_IN_OFF[n]):int(_IN_OFF[n + 1])]
    kv = c(3)
    kr = jnp.pad(kv[:, MLA_KV_LORA:], ((0, 0), (_KR_OFF, LANES - _KR_OFF - MLA_ROPE)))
    fl = jnp.pad(c(7), ((0, 0), (0, LANES - HEADS)))
    fq = _pad_heads(c(4), FOX_HEAD_DIM) * jnp.asarray(FOX_SCALE, w_in.dtype)
    return [c(0), c(1), c(2), kv[:, :MLA_KV_LORA], kr, fq, _pad_heads(c(5), FOX_HEAD_DIM), _pad_heads(c(6), FOX_HEAD_DIM), fl, c(8)]


def _in_unsegment(dw_p, widths):
    offs = np.concatenate([[0], np.cumsum(widths)])
    seg = [dw_p[:, int(offs[n]):int(offs[n + 1])] for n in range(len(widths))]
    u, ug, cq, ckv, kr, fq, fk, fv, fl, gate = seg
    return jnp.concatenate([
        u, ug, cq, ckv, kr[:, _KR_OFF:_KR_OFF + MLA_ROPE], _unpad_heads(fq, FOX_HEAD_DIM) * FOX_SCALE,
        _unpad_heads(fk, FOX_HEAD_DIM), _unpad_heads(fv, FOX_HEAD_DIM), fl[:, :HEADS], gate], axis=1)


def _split_wuq(wuq):
    return _pad_heads(wuq, MLA_NOPE + MLA_ROPE)


def _split_wukv(wukv):
    w = wukv.reshape(MLA_KV_LORA, HEADS, MLA_NOPE + MLA_V)
    pad = lambda t: jnp.pad(t, ((0, 0), (0, 0), (0, HEAD_PAD - t.shape[2]))).reshape(MLA_KV_LORA, HEADS * HEAD_PAD)
    return pad(w[:, :, :MLA_NOPE]), pad(w[:, :, MLA_NOPE:])


def _merge_wukv(dk_p, dv_p):
    k = dk_p.reshape(MLA_KV_LORA, HEADS, HEAD_PAD)[:, :, :MLA_NOPE]
    v = dv_p.reshape(MLA_KV_LORA, HEADS, HEAD_PAD)[:, :, :MLA_V]
    return jnp.concatenate([k, v], axis=2).reshape(MLA_KV_LORA, HEADS * (MLA_NOPE + MLA_V))


def _layer_fwd(x, h, p_i, w, g_next, tabs):
    c_q, c_k, s_lo, s_hi = tabs
    sv = {"x0": x}
    segs = _in_segments(w["w_in"])
    z = {}
    for nm, ws in zip(_SEG_NAMES, segs):
        z[nm] = _mm(h, ws, out_dtype=BF16 if nm in ("fq", "fk", "fv", "gate") else F32, bias=_ones_lane_bias() if nm == "fv" else None,
                    name="in_" + nm)
    sv.update(h=h, z=z)
    wa_bd, wx_bd = _block_diag(w["lru_wa"]).astype(BF16), _block_diag(w["lru_wx"]).astype(BF16)
    oa, xc, hs = _lru_fwd(z["u"], z["ug"], w["conv_w"], w["conv_b"], wa_bd, wx_bd, w["lru_ba"], w["lru_bx"], w["lru_lambda"])
    sv.update(oa=oa, xc=xc, hs=hs)
    qn = _rmsnorm_fwd(z["cq"], w["mla_q_norm"], "q_norm_fwd")
    kvn = _rmsnorm_fwd(z["ckv"], w["mla_kv_norm"], "kv_norm_fwd")
    wuq_p = _split_wuq(w["mla_wuq"])
    wk_p, wv_p = _split_wukv(w["mla_wukv"])
    qb = _rope_q(_mm(qn, wuq_p, name="mla_q"), c_q, s_lo, s_hi, transpose=False, out_dtype=BF16, name="rope_q")
    kb = _rope_k(_mm(kvn, wk_p, name="mla_k"), z["kr"], c_k, s_lo, s_hi)
    vb = _mm(kvn, wv_p, out_dtype=BF16, bias=_ones_lane_bias(), name="mla_v")
    ob, lse_b = _fa_fwd(qb, kb, vb, None, unit=64, name="mla_attn")
    sv.update(qn=qn, kvn=kvn, qb=qb, kb=kb, vb=vb, ob=ob, lse_b=lse_b)
    bf = jnp.pad(w["fox_bf"], (0, LANES - HEADS)).reshape(1, LANES)
    dec = _decay_fwd(z["fl"], bf)
    drow = dec[:, :HEADS].T.reshape(HEADS, 1, dec.shape[0])
    oc, lse_c = _fa_fwd(z["fq"], z["fk"], z["fv"], drow, unit=1, name="fox_attn")
    sv.update(drow=drow, oc=oc, lse_c=lse_c)
    ya = _mm(oa, w["w_br_a"], out_dtype=BF16, name="br_a")
    yb = _mm(ob, _pad_head_rows(w["w_br_b"], MLA_V), out_dtype=BF16, name="br_b")
    yc = _mm(oc, _pad_head_rows(w["w_br_c"], FOX_HEAD_DIM), out_dtype=BF16, name="br_c")
    merged = _merge_fwd(ya, yb, yc, z["gate"], w["gate_b"])
    x1, hn = _mm_res_norm(merged, w["w_o"], x, w["ffn_norm"], "w_o")
    sv.update(ya=ya, yb=yb, yc=yc, merged=merged, x1=x1)
    hf, act = _ffn_up(hn, _ffn_pair_columns(w["w_gate_up"]))
    x2, pn = _mm_res_norm(act, w["w_down"], x1, w["ple_norm"], "ffn_down")
    sv.update(hn=hn, hf=hf, act=act, x2=x2)
    pre = _mm(pn, w["w_ple_gate"], name="ple_gate")
    e = _mm(p_i, w["w_ple"], name="ple_embed")
    x3, h_next = _ple_fwd(x2, pre, e, g_next)
    sv.update(pn=pn, pre=pre, e=e, p_i=p_i)
    return x3, h_next, sv


def _layer_bwd(dx3, w, sv, tabs):
    c_q, c_k, s_lo, s_hi = tabs
    g = {}
    z = sv["z"]
    dpre, de = _ple_bwd(dx3, sv["pre"], sv["e"])
    g["w_ple"] = _mm(sv["p_i"], de, ta=True, out_dtype=BF16, name="d_w_ple")
    g["w_ple_gate"] = _mm(sv["pn"], dpre, ta=True, out_dtype=BF16, name="d_w_ple_gate")
    dx2, g["ple_norm"] = _mm_norm_bwd(dpre, w["w_ple_gate"], sv["x2"], w["ple_norm"], dx3, "ple_norm_bwd")
    g["w_down"] = _mm(sv["act"], dx2, ta=True, out_dtype=BF16, name="d_w_down")
    dhf = _ffn_down_bwd(dx2, w["w_down"], sv["hf"])
    g["w_gate_up"] = _ffn_unpair_columns(_mm(sv["hn"], dhf, ta=True, out_dtype=BF16, name="d_w_gate_up"))
    dx1, g["ffn_norm"] = _mm_norm_bwd(dhf, _ffn_pair_columns(w["w_gate_up"]), sv["x1"], w["ffn_norm"], dx2, "ffn_norm_bwd")
    g["w_o"] = _mm(sv["merged"], dx1, ta=True, out_dtype=BF16, name="d_w_o")
    dm = _mm(dx1, w["w_o"], tb=True, name="d_merged")
    dya, dyb, dyc, dgate, g["gate_b"] = _merge_bwd(dm, sv["ya"], sv["yb"], sv["yc"], z["gate"], w["gate_b"])
    wbb_p, wbc_p = _pad_head_rows(w["w_br_b"], MLA_V), _pad_head_rows(w["w_br_c"], FOX_HEAD_DIM)
    g["w_br_a"] = _mm(sv["oa"], dya, ta=True, out_dtype=BF16, name="d_w_br_a")
    g["w_br_b"] = _unpad_head_rows(_mm(sv["ob"], dyb, ta=True, out_dtype=BF16, name="d_w_br_b"), MLA_V)
    g["w_br_c"] = _unpad_head_rows(_mm(sv["oc"], dyc, ta=True, out_dtype=BF16, name="d_w_br_c"), FOX_HEAD_DIM)
    doa = _mm(dya, w["w_br_a"], tb=True, name="d_oa")
    dob = _mm(dyb, wbb_p, tb=True, out_dtype=BF16, name="d_ob")
    doc = _mm(dyc, wbc_p, tb=True, out_dtype=BF16, name="d_oc")
    dfq, dfk, dfv, d_dec = _fa_bwd(z["fq"], z["fk"], z["fv"], sv["oc"], sv["lse_c"], doc, sv["drow"],
                                   unit=1, dq_dtype=BF16, dk_dtype=BF16, name="fox_attn_bwd")
    d_dec = jnp.pad(d_dec.T, ((0, 0), (0, LANES - HEADS)))
    bf = jnp.pad(w["fox_bf"], (0, LANES - HEADS)).reshape(1, LANES)
    dfl, dbf = _decay_bwd(d_dec, z["fl"], bf)
    g["fox_bf"] = dbf[0, :HEADS]
    dqb, dkb, dvb, _ = _fa_bwd(sv["qb"], sv["kb"], sv["vb"], sv["ob"], sv["lse_b"], dob, None,
                               unit=64, dq_dtype=F32, dk_dtype=F32, name="mla_attn_bwd")
    wuq_p = _split_wuq(w["mla_wuq"])
    wk_p, wv_p = _split_wukv(w["mla_wukv"])
    dq_pre = _rope_q(dqb, c_q, s_lo, s_hi, transpose=True, out_dtype=BF16, name="rope_q_bwd")
    dkr = _rope_k_bwd(dkb, c_k, s_lo, s_hi)
    g["mla_wuq"] = _unpad_heads(_mm(sv["qn"], dq_pre, ta=True, out_dtype=BF16, name="d_wuq"), MLA_NOPE + MLA_ROPE)
    g["mla_wukv"] = _merge_wukv(_mm(sv["kvn"], dkb, ta=True, out_dtype=BF16, name="d_wuk"), _mm(sv["kvn"], dvb, ta=True, out_dtype=BF16, name="d_wuv"))
    dqn = _mm(dq_pre, wuq_p, tb=True, name="d_qn")
    dkvn = _mm(dvb, wv_p, tb=True, res=_mm(dkb, wk_p, tb=True, name="d_kvn_k"), name="d_kvn")
    dcq, g["mla_q_norm"] = _rmsnorm_bwd(z["cq"], w["mla_q_norm"], dqn, out_dtype=BF16, name="q_norm_bwd")
    dckv, g["mla_kv_norm"] = _rmsnorm_bwd(z["ckv"], w["mla_kv_norm"], dkvn, out_dtype=BF16, name="kv_norm_bwd")
    wa_bd, wx_bd = _block_diag(w["lru_wa"]).astype(BF16), _block_diag(w["lru_wx"]).astype(BF16)
    du, dug, dcw, dcb, dba, dbx, dlam, dwa, dwx = _lru_bwd(
        doa, z["u"], z["ug"], sv["xc"], sv["hs"], w["conv_w"], wa_bd, wx_bd, w["lru_ba"], w["lru_bx"], w["lru_lambda"])
    g["conv_w"], g["conv_b"], g["lru_ba"], g["lru_bx"] = dcw, dcb[0], dba[0], dbx[0]
    g["lru_lambda"] = dlam[0] * LRU_C * jax.nn.sigmoid(-w["lru_lambda"])
    g["lru_wa"], g["lru_wx"] = _block_diag_t(dwa), _block_diag_t(dwx)
    dsegs = [du, dug, dcq, dckv, dkr, dfq, dfk, dfv, dfl, dgate]
    dz = jnp.concatenate(dsegs, axis=1)
    w_in_p = jnp.concatenate(_in_segments(w["w_in"]), axis=1)
    g["w_in"] = _in_unsegment(_mm(sv["h"], dz, ta=True, out_dtype=BF16, name="d_w_in"), [d.shape[1] for d in dsegs])
    dx0, g["mix_norm"] = _mm_norm_bwd(dz, w_in_p, sv["x0"], w["mix_norm"], dx1, "mix_norm_bwd")
    return dx0, g


_LAYER_WEIGHTS = ("mix_norm", "w_in", "gate_b", "conv_w", "conv_b", "lru_wa", "lru_ba", "lru_wx", "lru_bx", "lru_lambda",
                  "mla_q_norm", "mla_wuq", "mla_kv_norm", "mla_wukv", "fox_bf", "w_br_a", "w_br_b", "w_br_c", "w_o",
                  "ffn_norm", "w_gate_up", "w_down", "ple_norm", "w_ple_gate", "w_ple")
_BIG = ("w_in", "mla_wuq", "mla_wukv", "w_br_a", "w_br_b", "w_br_c", "w_o", "w_gate_up", "w_down", "w_ple_gate", "w_ple")
_ROW_SHARDED = ("w_o", "w_down", "w_ple_gate")
_SMALL = ("mix_norm", "gate_b", "conv_b", "lru_wa", "lru_ba", "lru_wx", "lru_bx", "lru_lambda", "mla_q_norm", "mla_kv_norm",
          "fox_bf", "ffn_norm", "ple_norm")


def _local_step(x, p, layers, final_norm, target):
    tabs = _rope_tables(x.shape[0])
    saved = []
    h = _rmsnorm_fwd(x, layers[0]["mix_norm"], "mix_norm_fwd")
    for i in range(DEPTH):
        g_next = layers[i + 1]["mix_norm"] if i + 1 < DEPTH else None
        x, h, sv = _layer_fwd(x, h, p[i], layers[i], g_next, tabs)
        saved.append(sv)
    loss, dx, d_final = _loss_head(x, final_norm, target)
    grads = [None] * DEPTH
    for i in reversed(range(DEPTH)):
        dx, grads[i] = _layer_bwd(dx, layers[i], saved[i], tabs)
    return loss, dx, grads, d_final


def _hbm():
    return pl.BlockSpec(memory_space=pltpu.HBM)


def _peers(x, y):
    return [(1 - x, y), (x, 1 - y), (1 - x, 1 - y)]


def _gather_weights(arrs, name):
    n_arr = len(arrs)

    def body(*refs):
        srcs, outs = refs[:n_arr], refs[n_arr:2 * n_arr]
        send_sems, recv_sems = refs[2 * n_arr:]
        x, y, c = lax.axis_index("x"), lax.axis_index("y"), lax.axis_index("c")
        me = 2 * x + y
        peers = _peers(x, y)

        def copy(sem, src, dst, to):
            return pltpu.make_async_remote_copy(src_ref=src, dst_ref=dst, send_sem=send_sems.at[sem], recv_sem=recv_sems.at[sem],
                                                device_id=to, device_id_type=MESH)

        started = []
        for a in range(n_arr):
            for j, (px, py) in enumerate(peers):
                cp = copy(6 * a + j, srcs[a].at[c], outs[a].at[me, c], (px, py, c))
                cp.start()
                started.append(cp)
        for a in range(n_arr):
            for j, (px, py) in enumerate(peers):
                landed = outs[a].at[2 * px + py, c]
                copy(6 * a + j, srcs[a].at[c], landed, (px, py, c)).wait_recv()
                cp = copy(6 * a + 3 + j, landed, landed, (x, y, 1 - c))
                cp.start()
                started.append(cp)
        for a in range(n_arr):
            for j, (px, py) in enumerate(peers):
                copy(6 * a + 3 + j, srcs[a].at[1 - c], outs[a].at[2 * px + py, 1 - c], (x, y, 1 - c)).wait_recv()
        for cp in started:
            cp.wait_send()

    return pl.pallas_call(
        body, name=name, in_specs=[_hbm()] * n_arr, out_specs=tuple([_hbm()] * n_arr),
        out_shape=tuple(jax.ShapeDtypeStruct((4,) + t.shape, t.dtype) for t in arrs),
        scratch_shapes=[pltpu.SemaphoreType.DMA((6 * n_arr,)), pltpu.SemaphoreType.DMA((6 * n_arr,))],
    )(*arrs)


def _pair_swap_halves(g4):
    n, R, W = g4.shape
    Rh = R // 2

    def body(src_ref, out_ref, send_sem, recv_sem):
        x, y, c = lax.axis_index("x"), lax.axis_index("y"), lax.axis_index("c")
        cp = pltpu.make_async_remote_copy(src_ref=src_ref.at[:, pl.ds((1 - c) * Rh, Rh), :], dst_ref=out_ref, send_sem=send_sem,
                                          recv_sem=recv_sem, device_id=(x, y, 1 - c), device_id_type=MESH)
        cp.start()
        cp.wait()

    return pl.pallas_call(
        body, name="grad_pair_swap", in_specs=[_hbm()], out_specs=_hbm(), out_shape=jax.ShapeDtypeStruct((n, Rh, W), g4.dtype),
        scratch_shapes=[pltpu.SemaphoreType.DMA, pltpu.SemaphoreType.DMA],
    )(g4)


def _pair_add(g4, sib, c_arr):
    n, R, W = g4.shape
    Rh = R // 2
    tr = _tile_rows(Rh)
    nb = Rh // tr

    def body(c_ref, a_ref, b_ref, o_ref):
        o_ref[...] = (a_ref[...].astype(F32) + b_ref[...].astype(F32)).astype(o_ref.dtype)

    return pl.pallas_call(
        body, name="grad_pair_add",
        grid_spec=pltpu.PrefetchScalarGridSpec(
            num_scalar_prefetch=1, grid=(n, nb),
            in_specs=[pl.BlockSpec((None, tr, W), lambda s, i, c: (s, c[0] * nb + i, 0)), pl.BlockSpec((None, tr, W), lambda s, i, c: (s, i, 0))],
            out_specs=pl.BlockSpec((None, tr, W), lambda s, i, c: (s, i, 0))),
        out_shape=jax.ShapeDtypeStruct((n, Rh, W), g4.dtype), compiler_params=_cparams(("parallel", "parallel")),
    )(c_arr, g4, sib)


def _tile_rows(n):
    for t in (512, 480, 400, 320, 256, 240, 160, 128, 80, 64, 40, 32, 16, 8):
        if n % t == 0:
            return t
    return n


def _chips_exchange(part):
    n, Rh, W = part.shape

    def body(src_ref, out_ref, send_sems, recv_sems):
        x, y, c = lax.axis_index("x"), lax.axis_index("y"), lax.axis_index("c")

        def copy(j, to):
            return pltpu.make_async_remote_copy(src_ref=src_ref.at[2 * to[0] + to[1]], dst_ref=out_ref.at[j], send_sem=send_sems.at[j],
                                                recv_sem=recv_sems.at[j], device_id=(to[0], to[1], c), device_id_type=MESH)

        cps = [copy(j, peer) for j, peer in enumerate(_peers(x, y))]
        for cp in cps:
            cp.start()
        for cp in cps:
            cp.wait()

    return pl.pallas_call(
        body, name="grad_chips_exchange", in_specs=[_hbm()], out_specs=_hbm(), out_shape=jax.ShapeDtypeStruct((3, Rh, W), part.dtype),
        scratch_shapes=[pltpu.SemaphoreType.DMA((3,)), pltpu.SemaphoreType.DMA((3,))],
    )(part)


def _chips_add(part, got, k_arr, c_arr):
    n, Rh, W = part.shape
    tr = _tile_rows(Rh)
    nb = Rh // tr

    def body(k_ref, c_ref, a_ref, b_ref, o_ref):
        mine = pl.program_id(0) == c_ref[0]

        @pl.when(mine)
        def _():
            o_ref[...] = ((a_ref[...].astype(F32) + b_ref[0].astype(F32)) + b_ref[1].astype(F32)) + b_ref[2].astype(F32)

        @pl.when(jnp.logical_not(mine))
        def _():
            o_ref[...] = jnp.zeros_like(o_ref)

    return pl.pallas_call(
        body, name="grad_chips_add",
        grid_spec=pltpu.PrefetchScalarGridSpec(
            num_scalar_prefetch=2, grid=(2, nb),
            in_specs=[pl.BlockSpec((None, tr, W), lambda h, i, k, c: (k[0], i, 0)), pl.BlockSpec((3, tr, W), lambda h, i, k, c: (0, i, 0))],
            out_specs=pl.BlockSpec((tr, W), lambda h, i, k, c: (h * nb + i, 0))),
        out_shape=jax.ShapeDtypeStruct((2 * Rh, W), F32), compiler_params=_cparams(("parallel", "parallel")),
    )(k_arr, c_arr, part, got)


def _pair_gather(buf):
    R, W = buf.shape
    Rh = R // 2

    def body(src_ref, out_ref, send_sem, recv_sem):
        x, y, c = lax.axis_index("x"), lax.axis_index("y"), lax.axis_index("c")
        mine, other = pl.ds(c * Rh, Rh), pl.ds((1 - c) * Rh, Rh)
        pltpu.make_async_remote_copy(src_ref=src_ref.at[mine], dst_ref=out_ref.at[mine], send_sem=send_sem, recv_sem=recv_sem,
                                     device_id=(x, y, 1 - c), device_id_type=MESH).start()
        pltpu.make_async_remote_copy(src_ref=src_ref.at[mine], dst_ref=out_ref.at[other], send_sem=send_sem, recv_sem=recv_sem,
                                     device_id=(x, y, 1 - c), device_id_type=MESH).wait()

    return pl.pallas_call(
        body, name="grad_pair_gather", in_specs=[_hbm()], out_specs=_hbm(), out_shape=jax.ShapeDtypeStruct((R, W), buf.dtype),
        input_output_aliases={0: 0}, scratch_shapes=[pltpu.SemaphoreType.DMA, pltpu.SemaphoreType.DMA],
    )(buf)


def _gather_all(buf):
    R, W = buf.shape

    def body(src_ref, out_ref, send_sems, recv_sems, local_sem):
        x, y, c = lax.axis_index("x"), lax.axis_index("y"), lax.axis_index("c")
        me = 4 * x + 2 * y + c
        mine = pltpu.make_async_copy(src_ref, out_ref.at[me], local_sem)
        mine.start()
        rel = [((x + (r >> 2 & 1)) % 2, (y + (r >> 1 & 1)) % 2, (c + (r & 1)) % 2) for r in range(1, 8)]

        def copy(j, slot, to):
            return pltpu.make_async_remote_copy(src_ref=src_ref, dst_ref=out_ref.at[slot], send_sem=send_sems.at[j],
                                                recv_sem=recv_sems.at[j], device_id=to, device_id_type=MESH)

        sends = [copy(j, me, to) for j, to in enumerate(rel)]
        for cp in sends:
            cp.start()
        for j, to in enumerate(rel):
            copy(j, 4 * to[0] + 2 * to[1] + to[2], to).wait_recv()
        for cp in sends:
            cp.wait_send()
        mine.wait()

    return pl.pallas_call(
        body, name="small_gather", in_specs=[_hbm()], out_specs=_hbm(), out_shape=jax.ShapeDtypeStruct((8, R, W), buf.dtype),
        scratch_shapes=[pltpu.SemaphoreType.DMA((7,)), pltpu.SemaphoreType.DMA((7,)), pltpu.SemaphoreType.DMA],
    )(buf)


def _sum_slots(stack):
    n, R, W = stack.shape
    tr = _tile_rows(R)

    def body(s_ref, o_ref):
        tot = s_ref[0]
        for j in range(1, n):
            tot = tot + s_ref[j]
        o_ref[...] = tot

    return pl.pallas_call(
        body, name="small_sum", grid=(R // tr,), in_specs=[pl.BlockSpec((n, tr, W), lambda i: (0, i, 0))],
        out_specs=pl.BlockSpec((tr, W), lambda i: (i, 0)), out_shape=jax.ShapeDtypeStruct((R, W), F32),
        compiler_params=_cparams(("parallel",)),
    )(stack)


def _adamw(wp, gp, mp, vp, name):
    R, W = wp.shape
    tr = R
    for t in (1024, 512, 256, 128, 64, 32, 16, 8):
        if R % t == 0 and t * W <= 512 * 1024:
            tr = t
            break
    c1 = 1.0 - ADAM_B1 ** ADAM_STEP
    c2 = 1.0 - ADAM_B2 ** ADAM_STEP

    def body(w_ref, g_ref, m_ref, v_ref, d_ref, mo_ref, vo_ref):
        gv = g_ref[...]
        m = ADAM_B1 * m_ref[...] + (1.0 - ADAM_B1) * gv
        v = ADAM_B2 * v_ref[...] + (1.0 - ADAM_B2) * (gv * gv)
        m_hat = m / c1
        v_hat = v / c2
        d_ref[...] = -ADAM_LR * (m_hat / (jnp.sqrt(v_hat) + ADAM_EPS) + ADAM_WD * w_ref[...])
        mo_ref[...] = m
        vo_ref[...] = v

    blk = pl.BlockSpec((tr, W), lambda i: (i, 0))
    shp = jax.ShapeDtypeStruct((R, W), F32)
    return pl.pallas_call(body, name=name, grid=(R // tr,), in_specs=[blk] * 4, out_specs=(blk,) * 3, out_shape=(shp,) * 3,
                          compiler_params=_cparams(("parallel",)))(wp, gp, mp, vp)


def _pack(arrs, rows):
    flat = jnp.concatenate([a.reshape(-1) for a in arrs])
    return jnp.pad(flat, (0, rows * PACK_W - flat.shape[0])).reshape(rows, PACK_W)


def _unpack(buf, shapes):
    flat = buf.reshape(-1)
    out, off = [], 0
    for shp in shapes:
        n = int(np.prod(shp))
        out.append(flat[off:off + n].reshape(shp))
        off += n
    return out


def _rows_for(shapes, mult):
    n = sum(int(np.prod(s)) for s in shapes)
    rows = -(-n // PACK_W)
    return -(-rows // mult) * mult


def _shard_major(g, name):
    L, K, N = g.shape
    if name in _ROW_SHARDED:
        t = g.reshape(L, 4, K // 4, N).transpose(1, 0, 2, 3)
    else:
        t = g.reshape(L, K, 4, N // 4).transpose(2, 0, 1, 3)
    return t.reshape(4, -1, PACK_W)


def _join_shards(blocks, name):
    return jnp.concatenate(blocks, axis=1 if name in _ROW_SHARDED else 2)


def kernel(x, p, mix_norm, w_in, gate_b, conv_w, conv_b, lru_wa, lru_ba, lru_wx, lru_bx, lru_lambda, mla_q_norm, mla_wuq, mla_kv_norm, mla_wukv, fox_bf, w_br_a, w_br_b, w_br_c, w_o, ffn_norm, w_gate_up, w_down, ple_norm, w_ple_gate, w_ple, final_norm, loss_target, m_mix_norm, m_w_in, m_gate_b, m_conv_w, m_conv_b, m_lru_wa, m_lru_ba, m_lru_wx, m_lru_bx, m_lru_lambda, m_mla_q_norm, m_mla_wuq, m_mla_kv_norm, m_mla_wukv, m_fox_bf, m_w_br_a, m_w_br_b, m_w_br_c, m_w_o, m_ffn_norm, m_w_gate_up, m_w_down, m_ple_norm, m_w_ple_gate, m_w_ple, m_final_norm, v_mix_norm, v_w_in, v_gate_b, v_conv_w, v_conv_b, v_lru_wa, v_lru_ba, v_lru_wx, v_lru_bx, v_lru_lambda, v_mla_q_norm, v_mla_wuq, v_mla_kv_norm, v_mla_wukv, v_fox_bf, v_w_br_a, v_w_br_b, v_w_br_c, v_w_o, v_ffn_norm, v_w_gate_up, v_w_down, v_ple_norm, v_w_ple_gate, v_w_ple, v_final_norm):
    a = dict(locals())
    names = list(_LAYER_WEIGHTS) + ["final_norm"]
    W = {n: a[n] for n in names}
    M = {n: a["m_" + n] for n in names}
    V = {n: a["v_" + n] for n in names}
    ix, iy, ic = lax.axis_index("x"), lax.axis_index("y"), lax.axis_index("c")

    sharded = list(_BIG) + ["conv_w"]
    shard_shapes = [W[n].shape for n in sharded]
    R = _rows_for(shard_shapes, 64)
    mine = [W[n].astype(BF16) for n in _BIG] + [conv_w]
    gathered = _gather_weights(mine, "weight_gather")
    me = 2 * ix + iy
    gathered = [lax.dynamic_update_slice(g, t[None], (me,) + (0,) * t.ndim) for g, t in zip(gathered, mine)]
    full = {n: _join_shards([g[k] for k in range(4)], n) for n, g in zip(sharded, gathered)}
    conv_w_full = full["conv_w"]
    layers = []
    for i in range(DEPTH):
        lw = {n: W[n][i] for n in _SMALL}
        for n in _BIG:
            lw[n] = full[n][i]
        lw["conv_w"] = conv_w_full[i]
        layers.append(lw)

    loss_sum, dx, grads, d_final = _local_step(x[0], p[:, 0], layers, final_norm, loss_target[0])
    loss = lax.psum(loss_sum, ("x", "y", "c"))

    parts = [_shard_major(jnp.stack([grads[i][n] for i in range(DEPTH)]), n).astype(BF16) for n in sharded]
    g4, off = jnp.zeros((4, R, PACK_W), BF16), 0
    for t in parts:
        g4 = lax.dynamic_update_slice(g4, t, (0, off, 0))
        off += t.shape[1]
    c_arr = jnp.reshape(ic, (1,)).astype(jnp.int32)
    k_arr = jnp.reshape(2 * ix + iy, (1,)).astype(jnp.int32)
    pair = _pair_add(g4, _pair_swap_halves(g4), c_arr)
    g_pack = _pair_gather(_chips_add(pair, _chips_exchange(pair), k_arr, c_arr))
    big_out = {}
    for n, gsh in zip(sharded, _unpack(g_pack, shard_shapes)):
        view = lambda t: t.reshape(-1, t.shape[-1])
        d, nm, nv = _adamw(view(W[n]), view(gsh), view(M[n]), view(V[n]), "adamw_" + n)
        for key, arr in (("g", gsh), ("d", d), ("m", nm), ("v", nv)):
            big_out[(key, n)] = arr.reshape(W[n].shape)

    small = list(_SMALL) + ["final_norm"]
    small_shapes = [W[n].shape for n in small]
    Rs = _rows_for(small_shapes, 8)
    mine_small = [d_final if n == "final_norm" else jnp.stack([grads[i][n] for i in range(DEPTH)]) for n in small]
    sg = _sum_slots(_gather_all(_pack(mine_small, Rs)))
    small_out = {}
    for n, gsm in zip(small, _unpack(sg, small_shapes)):
        view = lambda t: t.reshape(-1, t.shape[-1])
        d, nm, nv = _adamw(view(W[n]), view(gsm), view(M[n]), view(V[n]), "adamw_" + n)
        for key, arr in (("g", gsm), ("d", d), ("m", nm), ("v", nv)):
            small_out[(key, n)] = arr.reshape(W[n].shape)

    def assemble(key, n):
        return big_out[(key, n)] if n in sharded else small_out[(key, n)]

    outs = [loss, dx[None]]
    for key in ("g", "d", "m", "v"):
        outs += [assemble(key, n) for n in names]
    return tuple(outs)
```

```python
import math

import numpy as np
import jax
import jax.numpy as jnp
from jax import lax
from jax.experimental import pallas as pl
from jax.experimental.pallas import tpu as pltpu

F32, BF16 = jnp.float32, jnp.bfloat16
MESH = pl.DeviceIdType.MESH

D_MODEL = 1024
DEPTH = 2
EPS = 1e-6
NEG_INF = -1e30
LRU_C = 8.0
CONV_WIDTH = 4
HEADS = 8
MLA_Q_LORA = 384
MLA_KV_LORA = 256
MLA_NOPE = 64
MLA_ROPE = 32
MLA_V = 64
ROPE_BASE = 10000.0
FOX_HEAD_DIM = 64
D_FF = 2816
HEAD_PAD = 128
MLA_SCALE = (MLA_NOPE + MLA_ROPE) ** -0.5
FOX_SCALE = FOX_HEAD_DIM ** -0.5

ADAM_LR, ADAM_B1, ADAM_B2, ADAM_EPS, ADAM_WD, ADAM_STEP = 0.001, 0.9, 0.999, 1e-08, 0.01, 10

VMEM_LIMIT_BYTES = 48 * 1024 * 1024
LANES = 128
PACK_W = 1024

ROW_TILE = 512
ATTN_TILE = 1024
LRU_CHUNK = 512


def _cparams(dims):
    return pltpu.CompilerParams(dimension_semantics=dims, vmem_limit_bytes=VMEM_LIMIT_BYTES)


def _tile(n, cap):
    if n <= cap:
        return n
    t = (cap // LANES) * LANES
    while t >= LANES:
        if n % t == 0:
            return t
        t -= LANES
    raise ValueError(f"no tile for {n} under {cap}")


def _rows(n):
    return min(ROW_TILE, n)


MM_VMEM_BUDGET = 36 * 1024 * 1024


def _mm_tiles(M, N, K, a_bytes, b_bytes, o_bytes, has_res):
    best, best_work = None, 0
    for tm in {_tile(M, c) for c in (1024, 512, 256)}:
        for tn in {_tile(N, c) for c in (1792, 1024, 512)}:
            for tk in {_tile(K, c) for c in (2048, 1408, 1024, 512)}:
                need = 2 * (tm * tk * a_bytes + tk * tn * b_bytes + tm * tn * o_bytes + (tm * tn * 4 if has_res else 0))
                need += tm * tn * 4 if tk < K else 0
                need += tm * tn * 4
                if need <= MM_VMEM_BUDGET and tm * tn * tk > best_work:
                    best, best_work = (tm, tn, tk), tm * tn * tk
    assert best is not None, (M, N, K)
    return best

def _mm(a, b, *, ta=False, tb=False, out_dtype=F32, res=None, bias=None, name):
    K, M = a.shape if ta else a.shape[::-1]
    N, K2 = b.shape if tb else b.shape[::-1]
    assert K == K2, (name, a.shape, b.shape)
    assert res is None or bias is None
    tm, tn, tk = _mm_tiles(M, N, K, a.dtype.itemsize, b.dtype.itemsize, jnp.dtype(out_dtype).itemsize, res is not None)
    nk = K // tk
    a_spec = pl.BlockSpec((tk, tm), lambda i, j, k: (k, i)) if ta else pl.BlockSpec((tm, tk), lambda i, j, k: (i, k))
    b_spec = pl.BlockSpec((tn, tk), lambda i, j, k: (j, k)) if tb else pl.BlockSpec((tk, tn), lambda i, j, k: (k, j))
    o_spec = pl.BlockSpec((tm, tn), lambda i, j, k: (i, j))
    dn = (((0,) if ta else (1,), (1,) if tb else (0,)), ((), ()))
    if bias is not None:
        res, r_spec = bias, pl.BlockSpec((1, tn), lambda i, j, k: (0, j))
    else:
        r_spec = o_spec
    has_res = res is not None

    def body(*refs):
        a_ref, b_ref = refs[0], refs[1]
        r_ref = refs[2] if has_res else None
        o_ref = refs[3] if has_res else refs[2]
        av, bv = a_ref[...], b_ref[...]
        if av.dtype != BF16:
            av = av.astype(BF16)
        if bv.dtype != BF16:
            bv = bv.astype(BF16)
        part = lax.dot_general(av, bv, dn, preferred_element_type=F32)

        def finish(total):
            if has_res:
                total = total + r_ref[...]
            o_ref[...] = total.astype(out_dtype)

        if nk == 1:
            finish(part)
        else:
            acc = refs[-1]
            k = pl.program_id(2)

            @pl.when(k == 0)
            def _():
                acc[...] = part

            @pl.when(k > 0)
            def _():
                acc[...] += part

            @pl.when(k == nk - 1)
            def _():
                finish(acc[...])

    in_specs = [a_spec, b_spec] + ([r_spec] if has_res else [])
    args = (a, b) + ((res,) if has_res else ())
    return pl.pallas_call(
        body, name=name, grid=(M // tm, N // tn, nk), in_specs=in_specs, out_specs=o_spec,
        out_shape=jax.ShapeDtypeStruct((M, N), out_dtype),
        scratch_shapes=[pltpu.VMEM((tm, tn), F32)] if nk > 1 else [],
        compiler_params=_cparams(("parallel", "parallel", "arbitrary")),
    )(*args)


def _mm_res_norm(a, b, res, g, name):
    M, K = a.shape
    N = b.shape[1]
    tm, tk = _tile(M, 1024), _tile(K, 1408)
    nk = K // tk

    def body(a_ref, b_ref, r_ref, g_ref, o_ref, h_ref, *scratch):
        part = jnp.dot(a_ref[...], b_ref[...], preferred_element_type=F32)

        def finish(total):
            xn = total + r_ref[...]
            o_ref[...] = xn
            rstd = lax.rsqrt(jnp.mean(xn * xn, axis=1, keepdims=True) + EPS)
            h_ref[...] = (xn * rstd * g_ref[...]).astype(BF16)

        if nk == 1:
            finish(part)
        else:
            acc = scratch[0]
            k = pl.program_id(1)

            @pl.when(k == 0)
            def _():
                acc[...] = part

            @pl.when(k > 0)
            def _():
                acc[...] += part

            @pl.when(k == nk - 1)
            def _():
                finish(acc[...])

    row = pl.BlockSpec((tm, N), lambda i, k: (i, 0))
    return pl.pallas_call(
        body, name=name, grid=(M // tm, nk),
        in_specs=[pl.BlockSpec((tm, tk), lambda i, k: (i, k)), pl.BlockSpec((tk, N), lambda i, k: (k, 0)), row,
                  pl.BlockSpec((1, N), lambda i, k: (0, 0))],
        out_specs=(row, row), out_shape=(jax.ShapeDtypeStruct((M, N), F32), jax.ShapeDtypeStruct((M, N), BF16)),
        scratch_shapes=[pltpu.VMEM((tm, N), F32)] if nk > 1 else [],
        compiler_params=_cparams(("parallel", "arbitrary")),
    )(a, b, res, g.reshape(1, N))


def _mm_norm_bwd(a, b, x, g, add, name):
    M, K = a.shape
    N = b.shape[0]
    tm, tk = _tile(M, 1024), _tile(K, 1408)
    nk = K // tk

    def body(a_ref, b_ref, x_ref, g_ref, add_ref, dx_ref, dg_ref, *scratch):
        i, k = pl.program_id(0), pl.program_id(1)
        part = lax.dot_general(a_ref[...], b_ref[...], _NT, preferred_element_type=F32)

        @pl.when(jnp.logical_and(i == 0, k == 0))
        def _():
            dg_ref[...] = jnp.zeros_like(dg_ref)

        def finish(dyv):
            xf = x_ref[...]
            rstd = lax.rsqrt(jnp.mean(xf * xf, axis=1, keepdims=True) + EPS)
            xhat = xf * rstd
            dxh = dyv * g_ref[...]
            dx_ref[...] = rstd * (dxh - xhat * jnp.mean(dxh * xhat, axis=1, keepdims=True)) + add_ref[...]
            dg_ref[...] += jnp.sum(dyv * xhat, axis=0, keepdims=True)

        if nk == 1:
            finish(part)
        else:
            acc = scratch[0]

            @pl.when(k == 0)
            def _():
                acc[...] = part

            @pl.when(k > 0)
            def _():
                acc[...] += part

            @pl.when(k == nk - 1)
            def _():
                finish(acc[...])

    row = pl.BlockSpec((tm, N), lambda i, k: (i, 0))
    vec = pl.BlockSpec((1, N), lambda i, k: (0, 0))
    dx, dg = pl.pallas_call(
        body, name=name, grid=(M // tm, nk),
        in_specs=[pl.BlockSpec((tm, tk), lambda i, k: (i, k)), pl.BlockSpec((N, tk), lambda i, k: (0, k)), row, vec, row],
        out_specs=(row, vec), out_shape=(jax.ShapeDtypeStruct((M, N), F32), jax.ShapeDtypeStruct((1, N), F32)),
        scratch_shapes=[pltpu.VMEM((tm, N), F32)] if nk > 1 else [],
        compiler_params=_cparams(("arbitrary", "arbitrary")),
    )(a, b, x, g.reshape(1, N), add)
    return dx, dg.reshape(N)


FFN_TILE = 1408
FFN_SUBTILES = ((0, 512), (512, 1024), (1024, 1408))
assert D_FF == 2 * FFN_TILE and FFN_SUBTILES[-1][1] == FFN_TILE
FFN_ROWS = 1024


def _ffn_pair_columns(w_gate_up):
    F = w_gate_up.shape[-1] // 2
    parts = []
    for j in range(F // FFN_TILE):
        parts += [w_gate_up[..., j * FFN_TILE:(j + 1) * FFN_TILE], w_gate_up[..., F + j * FFN_TILE:F + (j + 1) * FFN_TILE]]
    return jnp.concatenate(parts, axis=-1)


def _ffn_unpair_columns(dw):
    F = dw.shape[-1] // 2
    n = F // FFN_TILE
    blk = [dw[..., j * FFN_TILE:(j + 1) * FFN_TILE] for j in range(2 * n)]
    return jnp.concatenate(blk[0::2] + blk[1::2], axis=-1)


def _ffn_up(hn, w_pair):
    S, D = hn.shape
    W2 = w_pair.shape[1]
    F, tf = W2 // 2, FFN_TILE
    tm = min(FFN_ROWS, S)

    def body(h_ref, w_ref, hf_ref, act_ref):
        hv = h_ref[...]
        for lo, hi in FFN_SUBTILES:
            gt = jnp.dot(hv, w_ref[:, lo:hi], preferred_element_type=F32)
            up = jnp.dot(hv, w_ref[:, tf + lo:tf + hi], preferred_element_type=F32)
            hf_ref[:, lo:hi] = gt.astype(BF16)
            hf_ref[:, tf + lo:tf + hi] = up.astype(BF16)
            act_ref[:, lo:hi] = (gt * jax.nn.sigmoid(gt) * up).astype(BF16)

    return pl.pallas_call(
        body, name="ffn_up", grid=(S // tm, F // tf),
        in_specs=[pl.BlockSpec((tm, D), lambda i, j: (i, 0)), pl.BlockSpec((D, 2 * tf), lambda i, j: (0, j))],
        out_specs=(pl.BlockSpec((tm, 2 * tf), lambda i, j: (i, j)), pl.BlockSpec((tm, tf), lambda i, j: (i, j))),
        out_shape=(jax.ShapeDtypeStruct((S, W2), BF16), jax.ShapeDtypeStruct((S, F), BF16)),
        compiler_params=_cparams(("parallel", "parallel")),
    )(hn, w_pair)


def _ffn_down_bwd(dx, w_down, hf):
    S, D = dx.shape
    F, tf = w_down.shape[0], FFN_TILE
    tm = min(FFN_ROWS, S)

    def body(d_ref, w_ref, h_ref, o_ref):
        dv = d_ref[...].astype(BF16)
        for lo, hi in FFN_SUBTILES:
            dact = lax.dot_general(dv, w_ref[lo:hi, :], _NT, preferred_element_type=F32)
            gt, up = h_ref[:, lo:hi].astype(F32), h_ref[:, tf + lo:tf + hi].astype(F32)
            sg = jax.nn.sigmoid(gt)
            o_ref[:, lo:hi] = (dact * up * sg * (1.0 + gt * (1.0 - sg))).astype(BF16)
            o_ref[:, tf + lo:tf + hi] = (dact * gt * sg).astype(BF16)

    pair = pl.BlockSpec((tm, 2 * tf), lambda i, j: (i, j))
    return pl.pallas_call(
        body, name="ffn_down_bwd", grid=(S // tm, F // tf),
        in_specs=[pl.BlockSpec((tm, D), lambda i, j: (i, 0)), pl.BlockSpec((tf, D), lambda i, j: (j, 0)), pair],
        out_specs=pair, out_shape=jax.ShapeDtypeStruct((S, 2 * F), BF16),
        compiler_params=_cparams(("parallel", "parallel")),
    )(dx, w_down, hf)


def _rmsnorm_fwd(x, g, name):
    S, W = x.shape
    tm = _rows(S)

    def body(x_ref, g_ref, o_ref):
        xf = x_ref[...]
        rstd = lax.rsqrt(jnp.mean(xf * xf, axis=1, keepdims=True) + EPS)
        o_ref[...] = (xf * rstd * g_ref[...]).astype(BF16)

    return pl.pallas_call(
        body, name=name, grid=(S // tm,),
        in_specs=[pl.BlockSpec((tm, W), lambda i: (i, 0)), pl.BlockSpec((1, W), lambda i: (0, 0))],
        out_specs=pl.BlockSpec((tm, W), lambda i: (i, 0)),
        out_shape=jax.ShapeDtypeStruct((S, W), BF16), compiler_params=_cparams(("parallel",)),
    )(x, g.reshape(1, W))


def _rmsnorm_bwd(x, g, dy, *, add=None, out_dtype=F32, name):
    S, W = x.shape
    tm = _rows(S)
    has_add = add is not None

    def body(*refs):
        x_ref, g_ref, dy_ref = refs[:3]
        add_ref = refs[3] if has_add else None
        dx_ref, dg_ref = refs[-2], refs[-1]
        xf = x_ref[...]
        rstd = lax.rsqrt(jnp.mean(xf * xf, axis=1, keepdims=True) + EPS)
        xhat = xf * rstd
        dyv = dy_ref[...]
        dxh = dyv * g_ref[...]
        dx = rstd * (dxh - xhat * jnp.mean(dxh * xhat, axis=1, keepdims=True))
        if has_add:
            dx = dx + add_ref[...]
        dx_ref[...] = dx.astype(out_dtype)

        @pl.when(pl.program_id(0) == 0)
        def _():
            dg_ref[...] = jnp.zeros_like(dg_ref)

        dg_ref[...] += jnp.sum(dyv * xhat, axis=0, keepdims=True)

    row = pl.BlockSpec((tm, W), lambda i: (i, 0))
    vec = pl.BlockSpec((1, W), lambda i: (0, 0))
    dx, dg = pl.pallas_call(
        body, name=name, grid=(S // tm,),
        in_specs=[row, vec, row] + ([row] if has_add else []),
        out_specs=(row, vec),
        out_shape=(jax.ShapeDtypeStruct((S, W), out_dtype), jax.ShapeDtypeStruct((1, W), F32)),
        compiler_params=_cparams(("arbitrary",)),
    )(x, g.reshape(1, W), dy, *((add,) if has_add else ()))
    return dx, dg.reshape(W)


def _loss_head(x, g, target):
    S, W = x.shape
    tm = _rows(S)

    def body(x_ref, g_ref, t_ref, loss_ref, dx_ref, dg_ref):
        xf = x_ref[...]
        gv = g_ref[...]
        rstd = lax.rsqrt(jnp.mean(xf * xf, axis=1, keepdims=True) + EPS)
        xhat = xf * rstd
        err = xhat * gv - t_ref[...]
        part = 0.5 * jnp.sum(jnp.mean(err * err, axis=1, keepdims=True), axis=0, keepdims=True)
        dyv = err * (1.0 / W)
        dxh = dyv * gv
        dx_ref[...] = rstd * (dxh - xhat * jnp.mean(dxh * xhat, axis=1, keepdims=True))

        @pl.when(pl.program_id(0) == 0)
        def _():
            dg_ref[...] = jnp.zeros_like(dg_ref)
            loss_ref[...] = jnp.zeros_like(loss_ref)

        dg_ref[...] += jnp.sum(dyv * xhat, axis=0, keepdims=True)
        loss_ref[...] += part

    row = pl.BlockSpec((tm, W), lambda i: (i, 0))
    vec = pl.BlockSpec((1, W), lambda i: (0, 0))
    loss, dx, dg = pl.pallas_call(
        body, name="loss_head", grid=(S // tm,), in_specs=[row, vec, row],
        out_specs=(pl.BlockSpec((1, 1), lambda i: (0, 0)), row, vec),
        out_shape=(jax.ShapeDtypeStruct((1, 1), F32), jax.ShapeDtypeStruct((S, W), F32), jax.ShapeDtypeStruct((1, W), F32)),
        compiler_params=_cparams(("arbitrary",)),
    )(x, g.reshape(1, W), target)
    return loss[0, 0], dx, dg.reshape(W)


def _scan_fwd(a, b, row):
    T = a.shape[0]
    d = 1
    while d < T:
        keep = row >= d
        b = jnp.where(keep, a * pltpu.roll(b, d, axis=0) + b, b)
        a = jnp.where(keep, a * pltpu.roll(a, d, axis=0), a)
        d *= 2
    return a, b


def _scan_bwd(a, b, row):
    T = a.shape[0]
    d = 1
    while d < T:
        keep = row < T - d
        b = jnp.where(keep, a * pltpu.roll(b, T - d, axis=0) + b, b)
        a = jnp.where(keep, a * pltpu.roll(a, T - d, axis=0), a)
        d *= 2
    return a, b


def _expm1(x):
    small = x * (1.0 + x * (0.5 + x * (1.0 / 6 + x * (1.0 / 24 + x * (1.0 / 120 + x * (1.0 / 720 + x * (1.0 / 5040)))))))
    return jnp.where(jnp.abs(x) < 0.25, small, jnp.exp(x) - 1.0)


_GELU_C = math.sqrt(2.0 / math.pi)


def _gelu_and_grad(x):
    inner = _GELU_C * (x + 0.044715 * x * x * x)
    th = jnp.tanh(inner)
    val = 0.5 * x * (1.0 + th)
    grad = 0.5 * (1.0 + th) + 0.5 * x * (1.0 - th * th) * _GELU_C * (1.0 + 3 * 0.044715 * x * x)
    return val, grad


def _lru_gates(xc, wa, wx, ba, bx, lam):
    xcb = xc.astype(BF16)
    r = jax.nn.sigmoid(jnp.dot(xcb, wa, preferred_element_type=F32) + ba)
    ig = jax.nn.sigmoid(jnp.dot(xcb, wx, preferred_element_type=F32) + bx)
    sp = jax.nn.softplus(-lam)
    log_a = -LRU_C * r * sp
    a = jnp.exp(log_a)
    mult = jnp.sqrt(-_expm1(2.0 * log_a))
    return xcb, r, ig, sp, a, mult


def _lru_fwd(u, ug, conv_w, conv_b, wa_bd, wx_bd, ba, bx, lam):
    S, W = u.shape
    T = min(LRU_CHUNK, S)
    nl, nc = W // LANES, S // T

    def body(u_ref, ug_ref, cw_ref, cb_ref, wa_ref, wx_ref, ba_ref, bx_ref, lam_ref, ya_ref, xc_ref, h_ref, prev_u, h_carry):
        c = pl.program_id(1)

        @pl.when(c == 0)
        def _():
            prev_u[...] = jnp.zeros_like(prev_u)
            h_carry[...] = jnp.zeros_like(h_carry)

        uv = u_ref[...]
        row = lax.broadcasted_iota(jnp.int32, (T, LANES), 0)
        row8 = lax.broadcasted_iota(jnp.int32, (8, LANES), 0)
        cw = cw_ref[...]
        xc = cb_ref[...] + uv * cw[3:4, :]
        pv = prev_u[...]
        for k in range(1, CONV_WIDTH):
            us = pltpu.roll(uv, k, axis=0)
            top = jnp.where(row8 < k, pltpu.roll(pv, k, axis=0), us[0:8])
            us = jnp.concatenate([top, us[8:]], axis=0)
            xc = xc + us * cw[3 - k:4 - k, :]
        prev_u[...] = uv[T - 8:T]
        _, r, ig, sp, a, mult = _lru_gates(xc, wa_ref[...], wx_ref[...], ba_ref[...], bx_ref[...], lam_ref[...])
        bb = mult * (ig * xc)
        aa, hh = _scan_fwd(a, bb, row)
        h = hh + aa * h_carry[7:8, :]
        h_carry[...] = h[T - 8:T]
        gl, _ = _gelu_and_grad(ug_ref[...])
        ya_ref[...] = (h * gl).astype(BF16)
        xc_ref[...] = xc
        h_ref[...] = h

    seq = pl.BlockSpec((T, LANES), lambda l, c: (c, l))
    vec = pl.BlockSpec((1, LANES), lambda l, c: (0, l))
    mat = pl.BlockSpec((None, LANES, LANES), lambda l, c: (l, 0, 0))
    return pl.pallas_call(
        body, name="lru_fwd", grid=(nl, nc),
        in_specs=[seq, seq, pl.BlockSpec((CONV_WIDTH, LANES), lambda l, c: (0, l)), vec, mat, mat, vec, vec, vec],
        out_specs=(seq, seq, seq),
        out_shape=(jax.ShapeDtypeStruct((S, W), BF16), jax.ShapeDtypeStruct((S, W), F32), jax.ShapeDtypeStruct((S, W), F32)),
        scratch_shapes=[pltpu.VMEM((8, LANES), F32), pltpu.VMEM((8, LANES), F32)],
        compiler_params=_cparams(("parallel", "arbitrary")),
    )(u, ug, conv_w, conv_b.reshape(1, W), wa_bd, wx_bd, ba.reshape(1, W), bx.reshape(1, W), lam.reshape(1, W))


def _lru_bwd(dya, u, ug, xc, h, conv_w, wa_bd, wx_bd, ba, bx, lam):
    S, W = u.shape
    T = min(LRU_CHUNK, S)
    nl, nc = W // LANES, S // T
    tb8 = T // 8

    def body(dya_ref, u_ref, ug_ref, xc_ref, h_ref, hp_ref, cw_ref, wa_ref, wx_ref, ba_ref, bx_ref, lam_ref,
             du_ref, dug_ref, dcw_ref, dcb_ref, dba_ref, dbx_ref, dlam_ref, dwa_ref, dwx_ref,
             g_next, a_next, dxc_next):
        c = pl.program_id(1)

        @pl.when(c == 0)
        def _():
            g_next[...] = jnp.zeros_like(g_next)
            a_next[...] = jnp.zeros_like(a_next)
            dxc_next[...] = jnp.zeros_like(dxc_next)
            for ref in (dcw_ref, dcb_ref, dba_ref, dbx_ref, dlam_ref, dwa_ref, dwx_ref):
                ref[...] = jnp.zeros_like(ref)

        row = lax.broadcasted_iota(jnp.int32, (T, LANES), 0)
        row8 = lax.broadcasted_iota(jnp.int32, (8, LANES), 0)
        xcv = xc_ref[...]
        wa, wx = wa_ref[...], wx_ref[...]
        xcb, r, ig, sp, a, mult = _lru_gates(xcv, wa, wx, ba_ref[...], bx_ref[...], lam_ref[...])
        gl, dgl = _gelu_and_grad(ug_ref[...])
        dyav = dya_ref[...]
        hv = h_ref[...]
        dug_ref[...] = (dyav * hv * dgl).astype(BF16)
        dh = dyav * gl
        a_up = pltpu.roll(a, T - 1, axis=0)
        a_up = jnp.where(row == T - 1, a_next[0:1, :], a_up)
        prod, gg = _scan_bwd(a_up, dh, row)
        g = gg + prod * g_next[0:1, :]
        h_prev = pltpu.roll(hv, 1, axis=0)
        first_chunk = c == nc - 1
        h_before = jnp.where(first_chunk, 0.0, hp_ref[7:8, :])
        h_prev = jnp.where(row == 0, h_before, h_prev)
        da = g * h_prev
        d_mult = g * (ig * xcv)
        d_ig = g * mult * xcv
        dxc = g * mult * ig
        d_log_a = da * a - d_mult * (a * a) / mult
        d_r = d_log_a * (-LRU_C * sp)
        d_pa = d_r * r * (1.0 - r)
        d_px = d_ig * ig * (1.0 - ig)
        d_pab, d_pxb = d_pa.astype(BF16), d_px.astype(BF16)
        nt = (((1,), (1,)), ((), ()))
        tn = (((0,), (0,)), ((), ()))
        dxc = dxc + lax.dot_general(d_pab, wa, nt, preferred_element_type=F32) + lax.dot_general(d_pxb, wx, nt, preferred_element_type=F32)
        dwa_ref[...] += lax.dot_general(xcb, d_pab, tn, preferred_element_type=F32)
        dwx_ref[...] += lax.dot_general(xcb, d_pxb, tn, preferred_element_type=F32)
        dlam_ref[...] += jnp.sum(d_log_a * r, axis=0, keepdims=True)
        dba_ref[...] += jnp.sum(d_pa, axis=0, keepdims=True)
        dbx_ref[...] += jnp.sum(d_px, axis=0, keepdims=True)
        dcb_ref[...] += jnp.sum(dxc, axis=0, keepdims=True)
        uv = u_ref[...]
        cw = cw_ref[...]
        nxt = dxc_next[...]
        du = dxc * cw[3:4, :]
        dcw_ref[3:4, :] += jnp.sum(uv * dxc, axis=0, keepdims=True)
        for k in range(1, CONV_WIDTH):
            ds = pltpu.roll(dxc, T - k, axis=0)
            bot = jnp.where(row8 >= 8 - k, pltpu.roll(nxt, 8 - k, axis=0), ds[T - 8:T])
            ds = jnp.concatenate([ds[:T - 8], bot], axis=0)
            du = du + ds * cw[3 - k:4 - k, :]
            dcw_ref[3 - k:4 - k, :] += jnp.sum(uv * ds, axis=0, keepdims=True)
        du_ref[...] = du.astype(BF16)
        g_next[...] = g[0:8]
        a_next[...] = a[0:8]
        dxc_next[...] = dxc[0:8]

    seq = pl.BlockSpec((T, LANES), lambda l, c: (nc - 1 - c, l))
    before = pl.BlockSpec((8, LANES), lambda l, c: (jnp.maximum((nc - 1 - c) * tb8 - 1, 0), l))
    vec = pl.BlockSpec((1, LANES), lambda l, c: (0, l))
    cwb = pl.BlockSpec((CONV_WIDTH, LANES), lambda l, c: (0, l))
    mat = pl.BlockSpec((None, LANES, LANES), lambda l, c: (l, 0, 0))
    vshape = jax.ShapeDtypeStruct((1, W), F32)
    mshape = jax.ShapeDtypeStruct((nl, LANES, LANES), F32)
    return pl.pallas_call(
        body, name="lru_bwd", grid=(nl, nc),
        in_specs=[seq, seq, seq, seq, seq, before, cwb, mat, mat, vec, vec, vec],
        out_specs=(seq, seq, cwb, vec, vec, vec, vec, mat, mat),
        out_shape=(jax.ShapeDtypeStruct((S, W), BF16), jax.ShapeDtypeStruct((S, W), BF16),
                   jax.ShapeDtypeStruct((CONV_WIDTH, W), F32), vshape, vshape, vshape, vshape, mshape, mshape),
        scratch_shapes=[pltpu.VMEM((8, LANES), F32)] * 3,
        compiler_params=_cparams(("parallel", "arbitrary")),
    )(dya, u, ug, xc, h, h, conv_w, wa_bd, wx_bd, ba.reshape(1, W), bx.reshape(1, W), lam.reshape(1, W))


def _decay_fwd(f_logit, bf):
    S = f_logit.shape[0]
    T = min(LRU_CHUNK, S)

    def body(f_ref, b_ref, o_ref, carry):
        @pl.when(pl.program_id(0) == 0)
        def _():
            carry[...] = jnp.zeros_like(carry)

        row = lax.broadcasted_iota(jnp.int32, (T, LANES), 0)
        v = jax.nn.log_sigmoid(f_ref[...] + b_ref[...])
        d = 1
        while d < T:
            v = jnp.where(row >= d, v + pltpu.roll(v, d, axis=0), v)
            d *= 2
        v = v + carry[7:8, :]
        carry[...] = v[T - 8:T]
        o_ref[...] = v

    return pl.pallas_call(
        body, name="decay_fwd", grid=(S // T,),
        in_specs=[pl.BlockSpec((T, LANES), lambda c: (c, 0)), pl.BlockSpec((1, LANES), lambda c: (0, 0))],
        out_specs=pl.BlockSpec((T, LANES), lambda c: (c, 0)),
        out_shape=jax.ShapeDtypeStruct((S, LANES), F32), scratch_shapes=[pltpu.VMEM((8, LANES), F32)],
        compiler_params=_cparams(("arbitrary",)),
    )(f_logit, bf)


def _decay_bwd(d_dec, f_logit, bf):
    S = f_logit.shape[0]
    T = min(LRU_CHUNK, S)
    nc = S // T

    def body(dd_ref, f_ref, b_ref, df_ref, db_ref, carry):
        @pl.when(pl.program_id(0) == 0)
        def _():
            carry[...] = jnp.zeros_like(carry)
            db_ref[...] = jnp.zeros_like(db_ref)

        row = lax.broadcasted_iota(jnp.int32, (T, LANES), 0)
        v = dd_ref[...]
        d = 1
        while d < T:
            v = jnp.where(row < T - d, v + pltpu.roll(v, T - d, axis=0), v)
            d *= 2
        v = v + carry[0:1, :]
        carry[...] = v[0:8]
        df = v * jax.nn.sigmoid(-(f_ref[...] + b_ref[...]))
        df_ref[...] = df.astype(BF16)
        db_ref[...] += jnp.sum(df, axis=0, keepdims=True)

    seq = pl.BlockSpec((T, LANES), lambda c: (nc - 1 - c, 0))
    vec = pl.BlockSpec((1, LANES), lambda c: (0, 0))
    return pl.pallas_call(
        body, name="decay_bwd", grid=(nc,), in_specs=[seq, seq, vec], out_specs=(seq, vec),
        out_shape=(jax.ShapeDtypeStruct((S, LANES), BF16), jax.ShapeDtypeStruct((1, LANES), F32)),
        scratch_shapes=[pltpu.VMEM((8, LANES), F32)], compiler_params=_cparams(("arbitrary",)),
    )(d_dec, f_logit, bf)


def _rope_tables(S):
    pos = jnp.arange(S, dtype=F32)
    inv_freq = ROPE_BASE ** (-jnp.arange(0, MLA_ROPE, 2, dtype=F32) / MLA_ROPE)
    ang = pos[:, None] * inv_freq[None, :]
    cos, sin = jnp.cos(ang), jnp.sin(ang)
    half = MLA_ROPE // 2
    z = lambda n: jnp.zeros((S, n), F32)
    c_q = jnp.concatenate([jnp.ones((S, MLA_NOPE), F32), cos, cos, z(HEAD_PAD - MLA_NOPE - MLA_ROPE)], axis=1)
    c_k = jnp.concatenate([z(MLA_NOPE), cos, cos, z(HEAD_PAD - MLA_NOPE - MLA_ROPE)], axis=1)
    s_lo = jnp.concatenate([z(MLA_NOPE), -sin, z(HEAD_PAD - MLA_NOPE - half)], axis=1)
    s_hi = jnp.concatenate([z(MLA_NOPE + half), sin, z(HEAD_PAD - MLA_NOPE - MLA_ROPE)], axis=1)
    return c_q, c_k, s_lo, s_hi


def _rot(v, c, s_lo, s_hi):
    half = MLA_ROPE // 2
    return v * c + pltpu.roll(v, LANES - half, axis=1) * s_lo + pltpu.roll(v, half, axis=1) * s_hi


def _rot_t(dv, c, s_lo, s_hi):
    half = MLA_ROPE // 2
    return dv * c + pltpu.roll(dv * s_lo, half, axis=1) + pltpu.roll(dv * s_hi, LANES - half, axis=1)


def _rope_q(q_pre, c_q, s_lo, s_hi, *, transpose, out_dtype, name):
    S, W = q_pre.shape
    tm = _rows(S)
    fn = _rot_t if transpose else _rot

    def body(q_ref, c_ref, lo_ref, hi_ref, o_ref):
        c, lo, hi = c_ref[...], lo_ref[...], hi_ref[...]
        for hd in range(W // LANES):
            cols = slice(hd * LANES, (hd + 1) * LANES)
            o_ref[:, cols] = fn(q_ref[:, cols] * MLA_SCALE, c, lo, hi).astype(out_dtype)

    blk = pl.BlockSpec((tm, W), lambda i: (i, 0))
    tab = pl.BlockSpec((tm, LANES), lambda i: (i, 0))
    return pl.pallas_call(
        body, name=name, grid=(S // tm,), in_specs=[blk, tab, tab, tab], out_specs=blk,
        out_shape=jax.ShapeDtypeStruct((S, W), out_dtype), compiler_params=_cparams(("parallel",)),
    )(q_pre, c_q, s_lo, s_hi)


def _rope_k(k_pre, k_rope, c_k, s_lo, s_hi):
    S, W = k_pre.shape
    tm = _rows(S)

    def body(k_ref, r_ref, c_ref, lo_ref, hi_ref, o_ref):
        rot = _rot(r_ref[...], c_ref[...], lo_ref[...], hi_ref[...])
        for hd in range(W // LANES):
            cols = slice(hd * LANES, (hd + 1) * LANES)
            o_ref[:, cols] = (k_ref[:, cols] + rot).astype(BF16)

    blk = pl.BlockSpec((tm, W), lambda i: (i, 0))
    tab = pl.BlockSpec((tm, LANES), lambda i: (i, 0))
    return pl.pallas_call(
        body, name="rope_k", grid=(S // tm,), in_specs=[blk, tab, tab, tab, tab], out_specs=blk,
        out_shape=jax.ShapeDtypeStruct((S, W), BF16), compiler_params=_cparams(("parallel",)),
    )(k_pre, k_rope, c_k, s_lo, s_hi)


def _rope_k_bwd(dk, c_k, s_lo, s_hi):
    S, W = dk.shape
    tm = _rows(S)

    def body(dk_ref, c_ref, lo_ref, hi_ref, o_ref):
        tot = dk_ref[:, 0:LANES]
        for hd in range(1, W // LANES):
            tot = tot + dk_ref[:, hd * LANES:(hd + 1) * LANES]
        o_ref[...] = _rot_t(tot, c_ref[...], lo_ref[...], hi_ref[...]).astype(BF16)

    tab = pl.BlockSpec((tm, LANES), lambda i: (i, 0))
    return pl.pallas_call(
        body, name="rope_k_bwd", grid=(S // tm,), in_specs=[pl.BlockSpec((tm, W), lambda i: (i, 0)), tab, tab, tab],
        out_specs=tab, out_shape=jax.ShapeDtypeStruct((S, LANES), BF16), compiler_params=_cparams(("parallel",)),
    )(dk, c_k, s_lo, s_hi)


def _pairs(n):
    pr = [(i, j) for i in range(n) for j in range(i + 1)]
    return (jnp.asarray(np.array([p[0] for p in pr], np.int32)), jnp.asarray(np.array([p[1] for p in pr], np.int32)), len(pr))


def _unit_mask(shape, unit, q_off=0):
    q = lax.broadcasted_iota(jnp.int32, shape, 0) + q_off
    k = lax.broadcasted_iota(jnp.int32, shape, 1)
    if unit > 1:
        q, k = q // unit, k // unit
    return q >= k


_NT = (((1,), (1,)), ((), ()))


ONES_LANE = 64
ROW_SPLIT = 1
ROW_SPLIT_BWD = 4


def _ones_lane_bias():
    one = np.zeros((HEADS, HEAD_PAD), np.float32)
    one[:, ONES_LANE] = 1.0
    return jnp.asarray(one.reshape(1, HEADS * HEAD_PAD))


def _lane_sum(t):
    tot = t[:, 0:LANES]
    for c in range(1, t.shape[1] // LANES):
        tot = tot + t[:, c * LANES:(c + 1) * LANES]
    return tot


def _fa_fwd(q, k, v, dec_row, *, unit, name):
    S, W = q.shape
    H = W // LANES
    T = min(ATTN_TILE, S)
    n, hT = S // T, T // ROW_SPLIT
    qi, kj, npairs = _pairs(n)
    has_dec = dec_row is not None

    def body(qi_ref, kj_ref, *refs):
        if has_dec:
            q_ref, k_ref, v_ref, dr_ref, o_ref, lse_ref, m_s, acc = refs
        else:
            q_ref, k_ref, v_ref, o_ref, lse_ref, m_s, acc = refs
        t = pl.program_id(1)
        i, j = qi_ref[t], kj_ref[t]

        @pl.when(j == 0)
        def _():
            m_s[...] = jnp.full_like(m_s, NEG_INF)
            acc[...] = jnp.zeros_like(acc)

        def step(diag):
            for r in range(ROW_SPLIT):
                rows = slice(r * hT, (r + 1) * hT)
                nk = (r + 1) * hT if diag else T
                s = lax.dot_general(q_ref[rows, :], k_ref[0:nk, :], _NT, preferred_element_type=F32)
                if has_dec:
                    s = s - dr_ref[:, 0:nk]
                if diag:
                    s = jnp.where(_unit_mask((hT, nk), unit, r * hT), s, NEG_INF)
                m_prev = m_s[rows, :]
                m_new = jnp.maximum(m_prev, jnp.max(s, axis=1, keepdims=True))
                alpha = jnp.exp(m_prev - m_new)
                p = jnp.exp(s - jnp.tile(m_new, (1, nk // LANES)))
                acc[rows, :] = alpha * acc[rows, :] + jnp.dot(p.astype(BF16), v_ref[0:nk, :], preferred_element_type=F32)
                m_s[rows, :] = m_new

        @pl.when(j < i)
        def _():
            step(False)

        @pl.when(j == i)
        def _():
            step(True)
            av = acc[...]
            l = av[:, ONES_LANE:ONES_LANE + 1]
            lane = lax.broadcasted_iota(jnp.int32, (T, LANES), 1)
            o_ref[...] = jnp.where(lane < ONES_LANE, av / l, 0.0).astype(BF16)
            lse_ref[...] = m_s[...] + jnp.log(l)

    qb = pl.BlockSpec((T, LANES), lambda h, t, qi, kj: (qi[t], h))
    kb = pl.BlockSpec((T, LANES), lambda h, t, qi, kj: (kj[t], h))
    repq = pl.BlockSpec((None, T, LANES), lambda h, t, qi, kj: (h, qi[t], 0))
    rowk = pl.BlockSpec((None, 1, T), lambda h, t, qi, kj: (h, 0, kj[t]))
    in_specs = [qb, kb, kb] + ([rowk] if has_dec else [])
    args = (q, k, v) + ((dec_row,) if has_dec else ())
    return pl.pallas_call(
        body, name=name,
        grid_spec=pltpu.PrefetchScalarGridSpec(
            num_scalar_prefetch=2, grid=(H, npairs), in_specs=in_specs, out_specs=(qb, repq),
            scratch_shapes=[pltpu.VMEM((T, LANES), F32), pltpu.VMEM((T, LANES), F32)]),
        out_shape=(jax.ShapeDtypeStruct((S, W), BF16), jax.ShapeDtypeStruct((H, S, LANES), F32)),
        compiler_params=_cparams(("parallel", "arbitrary")),
    )(qi, kj, *args)


_TN = (((0,), (0,)), ((), ()))


def _fa_bwd_fused(q, k, v, do, o, lse, dec_row, *, unit, dq_dtype, dk_dtype, name):
    S, W = q.shape
    H = W // LANES
    T = min(ATTN_TILE, S)
    n, hT = S // T, T // ROW_SPLIT_BWD
    qi, kj, npairs = _pairs(n)
    has_dec = dec_row is not None

    def body(qi_ref, kj_ref, *refs):
        if has_dec:
            (q_ref, k_ref, v_ref, do_ref, o_ref, lse_ref, dr_ref, dq_ref, dk_ref, dv_ref, ddq_ref, ddk_ref,
             qacc, kacc, vacc, dl_ref, rsum, csum) = refs
        else:
            q_ref, k_ref, v_ref, do_ref, o_ref, lse_ref, dq_ref, dk_ref, dv_ref, qacc, kacc, vacc, dl_ref = refs
        t = pl.program_id(1)
        i, j = qi_ref[t], kj_ref[t]

        @pl.when(t == 0)
        def _():
            kacc[...] = jnp.zeros_like(kacc)
            vacc[...] = jnp.zeros_like(vacc)
            if has_dec:
                csum[...] = jnp.zeros_like(csum)

        @pl.when(j == 0)
        def _():
            qacc[...] = jnp.zeros_like(qacc)
            delta = jnp.sum(do_ref[...].astype(F32) * o_ref[...].astype(F32), axis=1, keepdims=True)
            dl_ref[...] = jnp.broadcast_to(delta, (T, LANES))
            if has_dec:
                rsum[...] = jnp.zeros_like(rsum)

        def step(diag):
            for r in range(ROW_SPLIT_BWD):
                rows = slice(r * hT, (r + 1) * hT)
                nk = (r + 1) * hT if diag else T
                qv, dov, kv = q_ref[rows, :], do_ref[rows, :], k_ref[0:nk, :]
                s = lax.dot_general(qv, kv, _NT, preferred_element_type=F32)
                if has_dec:
                    s = s - dr_ref[:, 0:nk]
                if diag:
                    s = jnp.where(_unit_mask((hT, nk), unit, r * hT), s, NEG_INF)
                p = jnp.exp(s - jnp.tile(lse_ref[rows, :], (1, nk // LANES)))
                dp = lax.dot_general(dov, v_ref[0:nk, :], _NT, preferred_element_type=F32)
                ds = p * (dp - jnp.tile(dl_ref[rows, :], (1, nk // LANES)))
                pb, dsb = p.astype(BF16), ds.astype(BF16)
                qacc[rows, :] += jnp.dot(dsb, kv, preferred_element_type=F32)
                vacc[j, 0:nk, :] += lax.dot_general(pb, dov, _TN, preferred_element_type=F32)
                kacc[j, 0:nk, :] += lax.dot_general(dsb, qv, _TN, preferred_element_type=F32)
                if has_dec:
                    rsum[rows, :] += _lane_sum(ds)
                    csum[j, :, 0:nk] -= jnp.sum(ds, axis=0, keepdims=True)

        @pl.when(j < i)
        def _():
            step(False)

        @pl.when(j == i)
        def _():
            step(True)
            dq_ref[...] = qacc[...].astype(dq_dtype)
            if has_dec:
                ddq_ref[...] = jnp.broadcast_to(jnp.sum(rsum[...], axis=1, keepdims=True), (T, LANES))

        @pl.when(t == npairs - 1)
        def _():
            for jj in range(n):
                dk_ref[jj * T:(jj + 1) * T, :] = kacc[jj].astype(dk_dtype)
                dv_ref[jj * T:(jj + 1) * T, :] = vacc[jj].astype(BF16)
                if has_dec:
                    ddk_ref[:, jj * T:(jj + 1) * T] = csum[jj]

    qb = pl.BlockSpec((T, LANES), lambda h, t, qi, kj: (qi[t], h))
    kb = pl.BlockSpec((T, LANES), lambda h, t, qi, kj: (kj[t], h))
    head = pl.BlockSpec((S, LANES), lambda h, t, qi, kj: (0, h))
    repq = pl.BlockSpec((None, T, LANES), lambda h, t, qi, kj: (h, qi[t], 0))
    rowk = pl.BlockSpec((None, 1, T), lambda h, t, qi, kj: (h, 0, kj[t]))
    rowh = pl.BlockSpec((None, 1, S), lambda h, t, qi, kj: (h, 0, 0))
    in_specs = [qb, kb, kb, qb, qb, repq] + ([rowk] if has_dec else [])
    args = (q, k, v, do, o, lse) + ((dec_row,) if has_dec else ())
    out_specs = [qb, head, head] + ([repq, rowh] if has_dec else [])
    out_shape = [jax.ShapeDtypeStruct((S, W), dq_dtype), jax.ShapeDtypeStruct((S, W), dk_dtype), jax.ShapeDtypeStruct((S, W), BF16)]
    scratch = [pltpu.VMEM((T, LANES), F32), pltpu.VMEM((n, T, LANES), F32), pltpu.VMEM((n, T, LANES), F32),
               pltpu.VMEM((T, LANES), F32)]
    if has_dec:
        out_shape += [jax.ShapeDtypeStruct((H, S, LANES), F32), jax.ShapeDtypeStruct((H, 1, S), F32)]
        scratch += [pltpu.VMEM((T, LANES), F32), pltpu.VMEM((n, 1, T), F32)]
    res = pl.pallas_call(
        body, name=name,
        grid_spec=pltpu.PrefetchScalarGridSpec(num_scalar_prefetch=2, grid=(H, npairs), in_specs=in_specs,
                                               out_specs=tuple(out_specs), scratch_shapes=scratch),
        out_shape=tuple(out_shape),
        compiler_params=pltpu.CompilerParams(dimension_semantics=("parallel", "arbitrary"), vmem_limit_bytes=FUSED_BWD_VMEM_BYTES),
    )(qi, kj, *args)
    return res if has_dec else (res[0], res[1], res[2], None, None)


FUSED_BWD_VMEM_BYTES = 58 * 1024 * 1024


def _fa_bwd(q, k, v, o, lse, do, dec_row, *, unit, dq_dtype, dk_dtype, name):
    dq, dk, dv, dd_q, dd_k = _fa_bwd_fused(q, k, v, do, o, lse, dec_row, unit=unit, dq_dtype=dq_dtype, dk_dtype=dk_dtype, name=name)
    if dd_k is None:
        return dq, dk, dv, None
    return dq, dk, dv, jnp.max(dd_q, axis=2) + dd_k.reshape(dd_k.shape[0], dd_k.shape[2])


def _merge_fwd(ya, yb, yc, gate_logit, gate_b):
    S, D = ya.shape
    tm = min(256, S)

    def body(a_ref, b_ref, c_ref, gl_ref, gb_ref, o_ref):
        g = jax.nn.sigmoid(gl_ref[...] + gb_ref[...])
        o_ref[...] = (g[:, 0:D] * a_ref[...] + g[:, D:2 * D] * b_ref[...] + g[:, 2 * D:3 * D] * c_ref[...]).astype(BF16)

    row = pl.BlockSpec((tm, D), lambda i: (i, 0))
    return pl.pallas_call(
        body, name="merge_fwd", grid=(S // tm,),
        in_specs=[row, row, row, pl.BlockSpec((tm, 3 * D), lambda i: (i, 0)), pl.BlockSpec((1, 3 * D), lambda i: (0, 0))],
        out_specs=row, out_shape=jax.ShapeDtypeStruct((S, D), BF16), compiler_params=_cparams(("parallel",)),
    )(ya, yb, yc, gate_logit, gate_b.reshape(1, 3 * D))


def _merge_bwd(dm, ya, yb, yc, gate_logit, gate_b):
    S, D = ya.shape
    tm = min(256, S)

    def body(dm_ref, a_ref, b_ref, c_ref, gl_ref, gb_ref, da_ref, db_ref, dc_ref, dgl_ref, dgb_ref):
        g = jax.nn.sigmoid(gl_ref[...] + gb_ref[...])
        dmv = dm_ref[...]
        parts = []
        for n, (y_ref, dy_ref) in enumerate(((a_ref, da_ref), (b_ref, db_ref), (c_ref, dc_ref))):
            gn = g[:, n * D:(n + 1) * D]
            dy_ref[...] = (dmv * gn).astype(BF16)
            parts.append(dmv * y_ref[...] * gn * (1.0 - gn))
        dgl = jnp.concatenate(parts, axis=1)
        dgl_ref[...] = dgl.astype(BF16)

        @pl.when(pl.program_id(0) == 0)
        def _():
            dgb_ref[...] = jnp.zeros_like(dgb_ref)

        dgb_ref[...] += jnp.sum(dgl, axis=0, keepdims=True)

    row = pl.BlockSpec((tm, D), lambda i: (i, 0))
    wide = pl.BlockSpec((tm, 3 * D), lambda i: (i, 0))
    vec = pl.BlockSpec((1, 3 * D), lambda i: (0, 0))
    act = jax.ShapeDtypeStruct((S, D), BF16)
    da, db, dc, dgl, dgb = pl.pallas_call(
        body, name="merge_bwd", grid=(S // tm,), in_specs=[row, row, row, row, wide, vec],
        out_specs=(row, row, row, wide, vec),
        out_shape=(act, act, act, jax.ShapeDtypeStruct((S, 3 * D), BF16), jax.ShapeDtypeStruct((1, 3 * D), F32)),
        compiler_params=_cparams(("arbitrary",)),
    )(dm, ya, yb, yc, gate_logit, gate_b.reshape(1, 3 * D))
    return da, db, dc, dgl, dgb.reshape(3 * D)


def _ple_fwd(x, pre, e, g_next):
    S, D = x.shape
    tm = _rows(S)
    with_norm = g_next is not None

    def body(*refs):
        x_ref, p_ref, e_ref = refs[:3]
        xn = x_ref[...] + jax.nn.sigmoid(p_ref[...]) * e_ref[...]
        if with_norm:
            g_ref, o_ref, h_ref = refs[3:]
            rstd = lax.rsqrt(jnp.mean(xn * xn, axis=1, keepdims=True) + EPS)
            h_ref[...] = (xn * rstd * g_ref[...]).astype(BF16)
        else:
            o_ref = refs[3]
        o_ref[...] = xn

    row = pl.BlockSpec((tm, D), lambda i: (i, 0))
    xs = jax.ShapeDtypeStruct((S, D), F32)
    if not with_norm:
        return pl.pallas_call(body, name="ple_fwd_last", grid=(S // tm,), in_specs=[row, row, row], out_specs=row,
                              out_shape=xs, compiler_params=_cparams(("parallel",)))(x, pre, e), None
    return pl.pallas_call(body, name="ple_fwd", grid=(S // tm,), in_specs=[row, row, row, pl.BlockSpec((1, D), lambda i: (0, 0))],
                          out_specs=(row, row), out_shape=(xs, jax.ShapeDtypeStruct((S, D), BF16)),
                          compiler_params=_cparams(("parallel",)))(x, pre, e, g_next.reshape(1, D))


def _ple_bwd(dx, pre, e):
    S, D = dx.shape
    tm = _rows(S)

    def body(dx_ref, p_ref, e_ref, dp_ref, de_ref):
        pg = jax.nn.sigmoid(p_ref[...])
        dxv = dx_ref[...]
        dp_ref[...] = (dxv * e_ref[...] * pg * (1.0 - pg)).astype(BF16)
        de_ref[...] = (dxv * pg).astype(BF16)

    row = pl.BlockSpec((tm, D), lambda i: (i, 0))
    act = jax.ShapeDtypeStruct((S, D), BF16)
    return pl.pallas_call(body, name="ple_bwd", grid=(S // tm,), in_specs=[row, row, row], out_specs=(row, row),
                          out_shape=(act, act), compiler_params=_cparams(("parallel",)))(dx, pre, e)


def _pad_heads(w, real):
    K = w.shape[0]
    w = w.reshape(K, HEADS, real)
    return jnp.pad(w, ((0, 0), (0, 0), (0, HEAD_PAD - real))).reshape(K, HEADS * HEAD_PAD)


def _unpad_heads(w, real):
    K = w.shape[0]
    return w.reshape(K, HEADS, HEAD_PAD)[:, :, :real].reshape(K, HEADS * real)


def _pad_head_rows(w, real):
    N = w.shape[1]
    w = w.reshape(HEADS, real, N)
    return jnp.pad(w, ((0, 0), (0, HEAD_PAD - real), (0, 0))).reshape(HEADS * HEAD_PAD, N)


def _unpad_head_rows(w, real):
    N = w.shape[1]
    return w.reshape(HEADS, HEAD_PAD, N)[:, :real].reshape(HEADS * real, N)


def _block_diag(w):
    w = w.reshape(4, 2, 64, 64)
    z = jnp.zeros((4, 64, 64), w.dtype)
    top = jnp.concatenate([w[:, 0], z], axis=2)
    bot = jnp.concatenate([z, w[:, 1]], axis=2)
    return jnp.concatenate([top, bot], axis=1)


def _block_diag_t(w):
    return jnp.stack([w[:, :64, :64], w[:, 64:, 64:]], axis=1).reshape(8, 64, 64)


_IN_SPLITS = (512, 512, 384, 288, 512, 512, 512, 8, 3072)
_IN_OFF = np.concatenate([[0], np.cumsum(_IN_SPLITS)])
_KR_OFF = 64
_SEG_NAMES = ("u", "ug", "cq", "ckv", "kr", "fq", "fk", "fv", "fl", "gate")


def _in_segments(w_in):
    c = lambda n: w_in[:, int(_IN_OFF[n]):int(_IN_OFF[n + 1])]
    kv = c(3)
    kr = jnp.pad(kv[:, MLA_KV_LORA:], ((0, 0), (_KR_OFF, LANES - _KR_OFF - MLA_ROPE)))
    fl = jnp.pad(c(7), ((0, 0), (0, LANES - HEADS)))
    fq = _pad_heads(c(4), FOX_HEAD_DIM) * jnp.asarray(FOX_SCALE, w_in.dtype)
    return [c(0), c(1), c(2), kv[:, :MLA_KV_LORA], kr, fq, _pad_heads(c(5), FOX_HEAD_DIM), _pad_heads(c(6), FOX_HEAD_DIM), fl, c(8)]


def _in_unsegment(dw_p, widths):
    offs = np.concatenate([[0], np.cumsum(widths)])
    seg = [dw_p[:, int(offs[n]):int(offs[n + 1])] for n in range(len(widths))]
    u, ug, cq, ckv, kr, fq, fk, fv, fl, gate = seg
    return jnp.concatenate([
        u, ug, cq, ckv, kr[:, _KR_OFF:_KR_OFF + MLA_ROPE], _unpad_heads(fq, FOX_HEAD_DIM) * FOX_SCALE,
        _unpad_heads(fk, FOX_HEAD_DIM), _unpad_heads(fv, FOX_HEAD_DIM), fl[:, :HEADS], gate], axis=1)


def _split_wuq(wuq):
    return _pad_heads(wuq, MLA_NOPE + MLA_ROPE)


def _split_wukv(wukv):
    w = wukv.reshape(MLA_KV_LORA, HEADS, MLA_NOPE + MLA_V)
    pad = lambda t: jnp.pad(t, ((0, 0), (0, 0), (0, HEAD_PAD - t.shape[2]))).reshape(MLA_KV_LORA, HEADS * HEAD_PAD)
    return pad(w[:, :, :MLA_NOPE]), pad(w[:, :, MLA_NOPE:])


def _merge_wukv(dk_p, dv_p):
    k = dk_p.reshape(MLA_KV_LORA, HEADS, HEAD_PAD)[:, :, :MLA_NOPE]
    v = dv_p.reshape(MLA_KV_LORA, HEADS, HEAD_PAD)[:, :, :MLA_V]
    return jnp.concatenate([k, v], axis=2).reshape(MLA_KV_LORA, HEADS * (MLA_NOPE + MLA_V))


def _layer_fwd(x, h, p_i, w, g_next, tabs):
    c_q, c_k, s_lo, s_hi = tabs
    sv = {"x0": x}
    segs = _in_segments(w["w_in"])
    z = {}
    for nm, ws in zip(_SEG_NAMES, segs):
        z[nm] = _mm(h, ws, out_dtype=BF16 if nm in ("fq", "fk", "fv", "gate") else F32, bias=_ones_lane_bias() if nm == "fv" else None,
                    name="in_" + nm)
    sv.update(h=h, z=z)
    wa_bd, wx_bd = _block_diag(w["lru_wa"]).astype(BF16), _block_diag(w["lru_wx"]).astype(BF16)
    oa, xc, hs = _lru_fwd(z["u"], z["ug"], w["conv_w"], w["conv_b"], wa_bd, wx_bd, w["lru_ba"], w["lru_bx"], w["lru_lambda"])
    sv.update(oa=oa, xc=xc, hs=hs)
    qn = _rmsnorm_fwd(z["cq"], w["mla_q_norm"], "q_norm_fwd")
    kvn = _rmsnorm_fwd(z["ckv"], w["mla_kv_norm"], "kv_norm_fwd")
    wuq_p = _split_wuq(w["mla_wuq"])
    wk_p, wv_p = _split_wukv(w["mla_wukv"])
    qb = _rope_q(_mm(qn, wuq_p, name="mla_q"), c_q, s_lo, s_hi, transpose=False, out_dtype=BF16, name="rope_q")
    kb = _rope_k(_mm(kvn, wk_p, name="mla_k"), z["kr"], c_k, s_lo, s_hi)
    vb = _mm(kvn, wv_p, out_dtype=BF16, bias=_ones_lane_bias(), name="mla_v")
    ob, lse_b = _fa_fwd(qb, kb, vb, None, unit=64, name="mla_attn")
    sv.update(qn=qn, kvn=kvn, qb=qb, kb=kb, vb=vb, ob=ob, lse_b=lse_b)
    bf = jnp.pad(w["fox_bf"], (0, LANES - HEADS)).reshape(1, LANES)
    dec = _decay_fwd(z["fl"], bf)
    drow = dec[:, :HEADS].T.reshape(HEADS, 1, dec.shape[0])
    oc, lse_c = _fa_fwd(z["fq"], z["fk"], z["fv"], drow, unit=1, name="fox_attn")
    sv.update(drow=drow, oc=oc, lse_c=lse_c)
    ya = _mm(oa, w["w_br_a"], out_dtype=BF16, name="br_a")
    yb = _mm(ob, _pad_head_rows(w["w_br_b"], MLA_V), out_dtype=BF16, name="br_b")
    yc = _mm(oc, _pad_head_rows(w["w_br_c"], FOX_HEAD_DIM), out_dtype=BF16, name="br_c")
    merged = _merge_fwd(ya, yb, yc, z["gate"], w["gate_b"])
    x1, hn = _mm_res_norm(merged, w["w_o"], x, w["ffn_norm"], "w_o")
    sv.update(ya=ya, yb=yb, yc=yc, merged=merged, x1=x1)
    hf, act = _ffn_up(hn, _ffn_pair_columns(w["w_gate_up"]))
    x2, pn = _mm_res_norm(act, w["w_down"], x1, w["ple_norm"], "ffn_down")
    sv.update(hn=hn, hf=hf, act=act, x2=x2)
    pre = _mm(pn, w["w_ple_gate"], name="ple_gate")
    e = _mm(p_i, w["w_ple"], name="ple_embed")
    x3, h_next = _ple_fwd(x2, pre, e, g_next)
    sv.update(pn=pn, pre=pre, e=e, p_i=p_i)
    return x3, h_next, sv


def _layer_bwd(dx3, w, sv, tabs):
    c_q, c_k, s_lo, s_hi = tabs
    g = {}
    z = sv["z"]
    dpre, de = _ple_bwd(dx3, sv["pre"], sv["e"])
    g["w_ple"] = _mm(sv["p_i"], de, ta=True, out_dtype=BF16, name="d_w_ple")
    g["w_ple_gate"] = _mm(sv["pn"], dpre, ta=True, out_dtype=BF16, name="d_w_ple_gate")
    dx2, g["ple_norm"] = _mm_norm_bwd(dpre, w["w_ple_gate"], sv["x2"], w["ple_norm"], dx3, "ple_norm_bwd")
    g["w_down"] = _mm(sv["act"], dx2, ta=True, out_dtype=BF16, name="d_w_down")
    dhf = _ffn_down_bwd(dx2, w["w_down"], sv["hf"])
    g["w_gate_up"] = _ffn_unpair_columns(_mm(sv["hn"], dhf, ta=True, out_dtype=BF16, name="d_w_gate_up"))
    dx1, g["ffn_norm"] = _mm_norm_bwd(dhf, _ffn_pair_columns(w["w_gate_up"]), sv["x1"], w["ffn_norm"], dx2, "ffn_norm_bwd")
    g["w_o"] = _mm(sv["merged"], dx1, ta=True, out_dtype=BF16, name="d_w_o")
    dm = _mm(dx1, w["w_o"], tb=True, name="d_merged")
    dya, dyb, dyc, dgate, g["gate_b"] = _merge_bwd(dm, sv["ya"], sv["yb"], sv["yc"], z["gate"], w["gate_b"])
    wbb_p, wbc_p = _pad_head_rows(w["w_br_b"], MLA_V), _pad_head_rows(w["w_br_c"], FOX_HEAD_DIM)
    g["w_br_a"] = _mm(sv["oa"], dya, ta=True, out_dtype=BF16, name="d_w_br_a")
    g["w_br_b"] = _unpad_head_rows(_mm(sv["ob"], dyb, ta=True, out_dtype=BF16, name="d_w_br_b"), MLA_V)
    g["w_br_c"] = _unpad_head_rows(_mm(sv["oc"], dyc, ta=True, out_dtype=BF16, name="d_w_br_c"), FOX_HEAD_DIM)
    doa = _mm(dya, w["w_br_a"], tb=True, name="d_oa")
    dob = _mm(dyb, wbb_p, tb=True, out_dtype=BF16, name="d_ob")
    doc = _mm(dyc, wbc_p, tb=True, out_dtype=BF16, name="d_oc")
    dfq, dfk, dfv, d_dec = _fa_bwd(z["fq"], z["fk"], z["fv"], sv["oc"], sv["lse_c"], doc, sv["drow"],
                                   unit=1, dq_dtype=BF16, dk_dtype=BF16, name="fox_attn_bwd")
    d_dec = jnp.pad(d_dec.T, ((0, 0), (0, LANES - HEADS)))
    bf = jnp.pad(w["fox_bf"], (0, LANES - HEADS)).reshape(1, LANES)
    dfl, dbf = _decay_bwd(d_dec, z["fl"], bf)
    g["fox_bf"] = dbf[0, :HEADS]
    dqb, dkb, dvb, _ = _fa_bwd(sv["qb"], sv["kb"], sv["vb"], sv["ob"], sv["lse_b"], dob, None,
                               unit=64, dq_dtype=F32, dk_dtype=F32, name="mla_attn_bwd")
    wuq_p = _split_wuq(w["mla_wuq"])
    wk_p, wv_p = _split_wukv(w["mla_wukv"])
    dq_pre = _rope_q(dqb, c_q, s_lo, s_hi, transpose=True, out_dtype=BF16, name="rope_q_bwd")
    dkr = _rope_k_bwd(dkb, c_k, s_lo, s_hi)
    g["mla_wuq"] = _unpad_heads(_mm(sv["qn"], dq_pre, ta=True, out_dtype=BF16, name="d_wuq"), MLA_NOPE + MLA_ROPE)
    g["mla_wukv"] = _merge_wukv(_mm(sv["kvn"], dkb, ta=True, out_dtype=BF16, name="d_wuk"), _mm(sv["kvn"], dvb, ta=True, out_dtype=BF16, name="d_wuv"))
    dqn = _mm(dq_pre, wuq_p, tb=True, name="d_qn")
    dkvn = _mm(dvb, wv_p, tb=True, res=_mm(dkb, wk_p, tb=True, name="d_kvn_k"), name="d_kvn")
    dcq, g["mla_q_norm"] = _rmsnorm_bwd(z["cq"], w["mla_q_norm"], dqn, out_dtype=BF16, name="q_norm_bwd")
    dckv, g["mla_kv_norm"] = _rmsnorm_bwd(z["ckv"], w["mla_kv_norm"], dkvn, out_dtype=BF16, name="kv_norm_bwd")
    wa_bd, wx_bd = _block_diag(w["lru_wa"]).astype(BF16), _block_diag(w["lru_wx"]).astype(BF16)
    du, dug, dcw, dcb, dba, dbx, dlam, dwa, dwx = _lru_bwd(
        doa, z["u"], z["ug"], sv["xc"], sv["hs"], w["conv_w"], wa_bd, wx_bd, w["lru_ba"], w["lru_bx"], w["lru_lambda"])
    g["conv_w"], g["conv_b"], g["lru_ba"], g["lru_bx"] = dcw, dcb[0], dba[0], dbx[0]
    g["lru_lambda"] = dlam[0] * LRU_C * jax.nn.sigmoid(-w["lru_lambda"])
    g["lru_wa"], g["lru_wx"] = _block_diag_t(dwa), _block_diag_t(dwx)
    dsegs = [du, dug, dcq, dckv, dkr, dfq, dfk, dfv, dfl, dgate]
    dz = jnp.concatenate(dsegs, axis=1)
    w_in_p = jnp.concatenate(_in_segments(w["w_in"]), axis=1)
    g["w_in"] = _in_unsegment(_mm(sv["h"], dz, ta=True, out_dtype=BF16, name="d_w_in"), [d.shape[1] for d in dsegs])
    dx0, g["mix_norm"] = _mm_norm_bwd(dz, w_in_p, sv["x0"], w["mix_norm"], dx1, "mix_norm_bwd")
    return dx0, g


_LAYER_WEIGHTS = ("mix_norm", "w_in", "gate_b", "conv_w", "conv_b", "lru_wa", "lru_ba", "lru_wx", "lru_bx", "lru_lambda",
                  "mla_q_norm", "mla_wuq", "mla_kv_norm", "mla_wukv", "fox_bf", "w_br_a", "w_br_b", "w_br_c", "w_o",
                  "ffn_norm", "w_gate_up", "w_down", "ple_norm", "w_ple_gate", "w_ple")
_BIG = ("w_in", "mla_wuq", "mla_wukv", "w_br_a", "w_br_b", "w_br_c", "w_o", "w_gate_up", "w_down", "w_ple_gate", "w_ple")
_ROW_SHARDED = ("w_o", "w_down", "w_ple_gate")
_SMALL = ("mix_norm", "gate_b", "conv_b", "lru_wa", "lru_ba", "lru_wx", "lru_bx", "lru_lambda", "mla_q_norm", "mla_kv_norm",
          "fox_bf", "ffn_norm", "ple_norm")


def _local_step(x, p, layers, final_norm, target):
    tabs = _rope_tables(x.shape[0])
    saved = []
    h = _rmsnorm_fwd(x, layers[0]["mix_norm"], "mix_norm_fwd")
    for i in range(DEPTH):
        g_next = layers[i + 1]["mix_norm"] if i + 1 < DEPTH else None
        x, h, sv = _layer_fwd(x, h, p[i], layers[i], g_next, tabs)
        saved.append(sv)
    loss, dx, d_final = _loss_head(x, final_norm, target)
    grads = [None] * DEPTH
    for i in reversed(range(DEPTH)):
        dx, grads[i] = _layer_bwd(dx, layers[i], saved[i], tabs)
    return loss, dx, grads, d_final


def _hbm():
    return pl.BlockSpec(memory_space=pltpu.HBM)


def _peers(x, y):
    return [(1 - x, y), (x, 1 - y), (1 - x, 1 - y)]


def _gather_weights(arrs, name):
    n_arr = len(arrs)

    def body(*refs):
        srcs, outs = refs[:n_arr], refs[n_arr:2 * n_arr]
        send_sems, recv_sems = refs[2 * n_arr:]
        x, y, c = lax.axis_index("x"), lax.axis_index("y"), lax.axis_index("c")
        me = 2 * x + y
        peers = _peers(x, y)

        def copy(sem, src, dst, to):
            return pltpu.make_async_remote_copy(src_ref=src, dst_ref=dst, send_sem=send_sems.at[sem], recv_sem=recv_sems.at[sem],
                                                device_id=to, device_id_type=MESH)

        started = []
        for a in range(n_arr):
            for j, (px, py) in enumerate(peers):
                cp = copy(6 * a + j, srcs[a].at[c], outs[a].at[me, c], (px, py, c))
                cp.start()
                started.append(cp)
        for a in range(n_arr):
            for j, (px, py) in enumerate(peers):
                landed = outs[a].at[2 * px + py, c]
                copy(6 * a + j, srcs[a].at[c], landed, (px, py, c)).wait_recv()
                cp = copy(6 * a + 3 + j, landed, landed, (x, y, 1 - c))
                cp.start()
                started.append(cp)
        for a in range(n_arr):
            for j, (px, py) in enumerate(peers):
                copy(6 * a + 3 + j, srcs[a].at[1 - c], outs[a].at[2 * px + py, 1 - c], (x, y, 1 - c)).wait_recv()
        for cp in started:
            cp.wait_send()

    return pl.pallas_call(
        body, name=name, in_specs=[_hbm()] * n_arr, out_specs=tuple([_hbm()] * n_arr),
        out_shape=tuple(jax.ShapeDtypeStruct((4,) + t.shape, t.dtype) for t in arrs),
        scratch_shapes=[pltpu.SemaphoreType.DMA((6 * n_arr,)), pltpu.SemaphoreType.DMA((6 * n_arr,))],
    )(*arrs)


def _pair_swap_halves(g4):
    n, R, W = g4.shape
    Rh = R // 2

    def body(src_ref, out_ref, send_sem, recv_sem):
        x, y, c = lax.axis_index("x"), lax.axis_index("y"), lax.axis_index("c")
        cp = pltpu.make_async_remote_copy(src_ref=src_ref.at[:, pl.ds((1 - c) * Rh, Rh), :], dst_ref=out_ref, send_sem=send_sem,
                                          recv_sem=recv_sem, device_id=(x, y, 1 - c), device_id_type=MESH)
        cp.start()
        cp.wait()

    return pl.pallas_call(
        body, name="grad_pair_swap", in_specs=[_hbm()], out_specs=_hbm(), out_shape=jax.ShapeDtypeStruct((n, Rh, W), g4.dtype),
        scratch_shapes=[pltpu.SemaphoreType.DMA, pltpu.SemaphoreType.DMA],
    )(g4)


def _pair_add(g4, sib, c_arr):
    n, R, W = g4.shape
    Rh = R // 2
    tr = _tile_rows(Rh)
    nb = Rh // tr

    def body(c_ref, a_ref, b_ref, o_ref):
        o_ref[...] = (a_ref[...].astype(F32) + b_ref[...].astype(F32)).astype(o_ref.dtype)

    return pl.pallas_call(
        body, name="grad_pair_add",
        grid_spec=pltpu.PrefetchScalarGridSpec(
            num_scalar_prefetch=1, grid=(n, nb),
            in_specs=[pl.BlockSpec((None, tr, W), lambda s, i, c: (s, c[0] * nb + i, 0)), pl.BlockSpec((None, tr, W), lambda s, i, c: (s, i, 0))],
            out_specs=pl.BlockSpec((None, tr, W), lambda s, i, c: (s, i, 0))),
        out_shape=jax.ShapeDtypeStruct((n, Rh, W), g4.dtype), compiler_params=_cparams(("parallel", "parallel")),
    )(c_arr, g4, sib)


def _tile_rows(n):
    for t in (512, 480, 400, 320, 256, 240, 160, 128, 80, 64, 40, 32, 16, 8):
        if n % t == 0:
            return t
    return n


def _chips_exchange(part):
    n, Rh, W = part.shape

    def body(src_ref, out_ref, send_sems, recv_sems):
        x, y, c = lax.axis_index("x"), lax.axis_index("y"), lax.axis_index("c")

        def copy(j, to):
            return pltpu.make_async_remote_copy(src_ref=src_ref.at[2 * to[0] + to[1]], dst_ref=out_ref.at[j], send_sem=send_sems.at[j],
                                                recv_sem=recv_sems.at[j], device_id=(to[0], to[1], c), device_id_type=MESH)

        cps = [copy(j, peer) for j, peer in enumerate(_peers(x, y))]
        for cp in cps:
            cp.start()
        for cp in cps:
            cp.wait()

    return pl.pallas_call(
        body, name="grad_chips_exchange", in_specs=[_hbm()], out_specs=_hbm(), out_shape=jax.ShapeDtypeStruct((3, Rh, W), part.dtype),
        scratch_shapes=[pltpu.SemaphoreType.DMA((3,)), pltpu.SemaphoreType.DMA((3,))],
    )(part)


def _chips_add(part, got, k_arr, c_arr):
    n, Rh, W = part.shape
    tr = _tile_rows(Rh)
    nb = Rh // tr

    def body(k_ref, c_ref, a_ref, b_ref, o_ref):
        mine = pl.program_id(0) == c_ref[0]

        @pl.when(mine)
        def _():
            o_ref[...] = ((a_ref[...].astype(F32) + b_ref[0].astype(F32)) + b_ref[1].astype(F32)) + b_ref[2].astype(F32)

        @pl.when(jnp.logical_not(mine))
        def _():
            o_ref[...] = jnp.zeros_like(o_ref)

    return pl.pallas_call(
        body, name="grad_chips_add",
        grid_spec=pltpu.PrefetchScalarGridSpec(
            num_scalar_prefetch=2, grid=(2, nb),
            in_specs=[pl.BlockSpec((None, tr, W), lambda h, i, k, c: (k[0], i, 0)), pl.BlockSpec((3, tr, W), lambda h, i, k, c: (0, i, 0))],
            out_specs=pl.BlockSpec((tr, W), lambda h, i, k, c: (h * nb + i, 0))),
        out_shape=jax.ShapeDtypeStruct((2 * Rh, W), F32), compiler_params=_cparams(("parallel", "parallel")),
    )(k_arr, c_arr, part, got)


def _pair_gather(buf):
    R, W = buf.shape
    Rh = R // 2

    def body(src_ref, out_ref, send_sem, recv_sem):
        x, y, c = lax.axis_index("x"), lax.axis_index("y"), lax.axis_index("c")
        mine, other = pl.ds(c * Rh, Rh), pl.ds((1 - c) * Rh, Rh)
        pltpu.make_async_remote_copy(src_ref=src_ref.at[mine], dst_ref=out_ref.at[mine], send_sem=send_sem, recv_sem=recv_sem,
                                     device_id=(x, y, 1 - c), device_id_type=MESH).start()
        pltpu.make_async_remote_copy(src_ref=src_ref.at[mine], dst_ref=out_ref.at[other], send_sem=send_sem, recv_sem=recv_sem,
                                     device_id=(x, y, 1 - c), device_id_type=MESH).wait()

    return pl.pallas_call(
        body, name="grad_pair_gather", in_specs=[_hbm()], out_specs=_hbm(), out_shape=jax.ShapeDtypeStruct((R, W), buf.dtype),
        input_output_aliases={0: 0}, scratch_shapes=[pltpu.SemaphoreType.DMA, pltpu.SemaphoreType.DMA],
    )(buf)


def _gather_all(buf):
    R, W = buf.shape

    def body(src_ref, out_ref, send_sems, recv_sems, local_sem):
        x, y, c = lax.axis_index("x"), lax.axis_index("y"), lax.axis_index("c")
        me = 4 * x + 2 * y + c
        mine = pltpu.make_async_copy(src_ref, out_ref.at[me], local_sem)
        mine.start()
        rel = [((x + (r >> 2 & 1)) % 2, (y + (r >> 1 & 1)) % 2, (c + (r & 1)) % 2) for r in range(1, 8)]

        def copy(j, slot, to):
            return pltpu.make_async_remote_copy(src_ref=src_ref, dst_ref=out_ref.at[slot], send_sem=send_sems.at[j],
                                                recv_sem=recv_sems.at[j], device_id=to, device_id_type=MESH)

        sends = [copy(j, me, to) for j, to in enumerate(rel)]
        for cp in sends:
            cp.start()
        for j, to in enumerate(rel):
            copy(j, 4 * to[0] + 2 * to[1] + to[2], to).wait_recv()
        for cp in sends:
            cp.wait_send()
        mine.wait()

    return pl.pallas_call(
        body, name="small_gather", in_specs=[_hbm()], out_specs=_hbm(), out_shape=jax.ShapeDtypeStruct((8, R, W), buf.dtype),
        scratch_shapes=[pltpu.SemaphoreType.DMA((7,)), pltpu.SemaphoreType.DMA((7,)), pltpu.SemaphoreType.DMA],
    )(buf)


def _sum_slots(stack):
    n, R, W = stack.shape
    tr = _tile_rows(R)

    def body(s_ref, o_ref):
        tot = s_ref[0]
        for j in range(1, n):
            tot = tot + s_ref[j]
        o_ref[...] = tot

    return pl.pallas_call(
        body, name="small_sum", grid=(R // tr,), in_specs=[pl.BlockSpec((n, tr, W), lambda i: (0, i, 0))],
        out_specs=pl.BlockSpec((tr, W), lambda i: (i, 0)), out_shape=jax.ShapeDtypeStruct((R, W), F32),
        compiler_params=_cparams(("parallel",)),
    )(stack)


def _adamw(wp, gp, mp, vp, name):
    R, W = wp.shape
    tr = R
    for t in (1024, 512, 256, 128, 64, 32, 16, 8):
        if R % t == 0 and t * W <= 512 * 1024:
            tr = t
            break
    c1 = 1.0 - ADAM_B1 ** ADAM_STEP
    c2 = 1.0 - ADAM_B2 ** ADAM_STEP

    def body(w_ref, g_ref, m_ref, v_ref, d_ref, mo_ref, vo_ref):
        gv = g_ref[...]
        m = ADAM_B1 * m_ref[...] + (1.0 - ADAM_B1) * gv
        v = ADAM_B2 * v_ref[...] + (1.0 - ADAM_B2) * (gv * gv)
        m_hat = m / c1
        v_hat = v / c2
        d_ref[...] = -ADAM_LR * (m_hat / (jnp.sqrt(v_hat) + ADAM_EPS) + ADAM_WD * w_ref[...])
        mo_ref[...] = m
        vo_ref[...] = v

    blk = pl.BlockSpec((tr, W), lambda i: (i, 0))
    shp = jax.ShapeDtypeStruct((R, W), F32)
    return pl.pallas_call(body, name=name, grid=(R // tr,), in_specs=[blk] * 4, out_specs=(blk,) * 3, out_shape=(shp,) * 3,
                          compiler_params=_cparams(("parallel",)))(wp, gp, mp, vp)


def _pack(arrs, rows):
    flat = jnp.concatenate([a.reshape(-1) for a in arrs])
    return jnp.pad(flat, (0, rows * PACK_W - flat.shape[0])).reshape(rows, PACK_W)


def _unpack(buf, shapes):
    flat = buf.reshape(-1)
    out, off = [], 0
    for shp in shapes:
        n = int(np.prod(shp))
        out.append(flat[off:off + n].reshape(shp))
        off += n
    return out


def _rows_for(shapes, mult):
    n = sum(int(np.prod(s)) for s in shapes)
    rows = -(-n // PACK_W)
    return -(-rows // mult) * mult


def _shard_major(g, name):
    L, K, N = g.shape
    if name in _ROW_SHARDED:
        t = g.reshape(L, 4, K // 4, N).transpose(1, 0, 2, 3)
    else:
        t = g.reshape(L, K, 4, N // 4).transpose(2, 0, 1, 3)
    return t.reshape(4, -1, PACK_W)


def _join_shards(blocks, name):
    return jnp.concatenate(blocks, axis=1 if name in _ROW_SHARDED else 2)


def kernel(x, p, mix_norm, w_in, gate_b, conv_w, conv_b, lru_wa, lru_ba, lru_wx, lru_bx, lru_lambda, mla_q_norm, mla_wuq, mla_kv_norm, mla_wukv, fox_bf, w_br_a, w_br_b, w_br_c, w_o, ffn_norm, w_gate_up, w_down, ple_norm, w_ple_gate, w_ple, final_norm, loss_target, m_mix_norm, m_w_in, m_gate_b, m_conv_w, m_conv_b, m_lru_wa, m_lru_ba, m_lru_wx, m_lru_bx, m_lru_lambda, m_mla_q_norm, m_mla_wuq, m_mla_kv_norm, m_mla_wukv, m_fox_bf, m_w_br_a, m_w_br_b, m_w_br_c, m_w_o, m_ffn_norm, m_w_gate_up, m_w_down, m_ple_norm, m_w_ple_gate, m_w_ple, m_final_norm, v_mix_norm, v_w_in, v_gate_b, v_conv_w, v_conv_b, v_lru_wa, v_lru_ba, v_lru_wx, v_lru_bx, v_lru_lambda, v_mla_q_norm, v_mla_wuq, v_mla_kv_norm, v_mla_wukv, v_fox_bf, v_w_br_a, v_w_br_b, v_w_br_c, v_w_o, v_ffn_norm, v_w_gate_up, v_w_down, v_ple_norm, v_w_ple_gate, v_w_ple, v_final_norm):
    a = dict(locals())
    names = list(_LAYER_WEIGHTS) + ["final_norm"]
    W = {n: a[n] for n in names}
    M = {n: a["m_" + n] for n in names}
    V = {n: a["v_" + n] for n in names}
    ix, iy, ic = lax.axis_index("x"), lax.axis_index("y"), lax.axis_index("c")

    sharded = list(_BIG) + ["conv_w"]
    shard_shapes = [W[n].shape for n in sharded]
    R = _rows_for(shard_shapes, 64)
    mine = [W[n].astype(BF16) for n in _BIG] + [conv_w]
    gathered = _gather_weights(mine, "weight_gather")
    me = 2 * ix + iy
    gathered = [lax.dynamic_update_slice(g, t[None], (me,) + (0,) * t.ndim) for g, t in zip(gathered, mine)]
    full = {n: _join_shards([g[k] for k in range(4)], n) for n, g in zip(sharded, gathered)}
    conv_w_full = full["conv_w"]
    layers = []
    for i in range(DEPTH):
        lw = {n: W[n][i] for n in _SMALL}
        for n in _BIG:
            lw[n] = full[n][i]
        lw["conv_w"] = conv_w_full[i]
        layers.append(lw)

    loss_sum, dx, grads, d_final = _local_step(x[0], p[:, 0], layers, final_norm, loss_target[0])
    loss = lax.psum(loss_sum, ("x", "y", "c"))

    parts = [_shard_major(jnp.stack([grads[i][n] for i in range(DEPTH)]), n).astype(BF16) for n in sharded]
    g4, off = jnp.zeros((4, R, PACK_W), BF16), 0
    for t in parts:
        g4 = lax.dynamic_update_slice(g4, t, (0, off, 0))
        off += t.shape[1]
    c_arr = jnp.reshape(ic, (1,)).astype(jnp.int32)
    k_arr = jnp.reshape(2 * ix + iy, (1,)).astype(jnp.int32)
    pair = _pair_add(g4, _pair_swap_halves(g4), c_arr)
    g_pack = _pair_gather(_chips_add(pair, _chips_exchange(pair), k_arr, c_arr))
    big_out = {}
    for n, gsh in zip(sharded, _unpack(g_pack, shard_shapes)):
        view = lambda t: t.reshape(-1, t.shape[-1])
        d, nm, nv = _adamw(view(W[n]), view(gsh), view(M[n]), view(V[n]), "adamw_" + n)
        for key, arr in (("g", gsh), ("d", d), ("m", nm), ("v", nv)):
            big_out[(key, n)] = arr.reshape(W[n].shape)

    small = list(_SMALL) + ["final_norm"]
    small_shapes = [W[n].shape for n in small]
    Rs = _rows_for(small_shapes, 8)
    mine_small = [d_final if n == "final_norm" else jnp.stack([grads[i][n] for i in range(DEPTH)]) for n in small]
    sg = _sum_slots(_gather_all(_pack(mine_small, Rs)))
    small_out = {}
    for n, gsm in zip(small, _unpack(sg, small_shapes)):
        view = lambda t: t.reshape(-1, t.shape[-1])
        d, nm, nv = _adamw(view(W[n]), view(gsm), view(M[n]), view(V[n]), "adamw_" + n)
        for key, arr in (("g", gsm), ("d", d), ("m", nm), ("v", nv)):
            small_out[(key, n)] = arr.reshape(W[n].shape)

    def assemble(key, n):
        return big_out[(key, n)] if n in sharded else small_out[(key, n)]

    outs = [loss, dx[None]]
    for key in ("g", "d", "m", "v"):
        outs += [assemble(key, n) for n in names]
    return tuple(outs)
```

```python
import math

import numpy as np
import jax
import jax.numpy as jnp
from jax import lax
from jax.experimental import pallas as pl
from jax.experimental.pallas import tpu as pltpu

F32, BF16 = jnp.float32, jnp.bfloat16
MESH = pl.DeviceIdType.MESH

D_MODEL = 1024
DEPTH = 2
EPS = 1e-6
NEG_INF = -1e30
LRU_C = 8.0
CONV_WIDTH = 4
HEADS = 8
MLA_Q_LORA = 384
MLA_KV_LORA = 256
MLA_NOPE = 64
MLA_ROPE = 32
MLA_V = 64
ROPE_BASE = 10000.0
FOX_HEAD_DIM = 64
D_FF = 2816
HEAD_PAD = 128
MLA_SCALE = (MLA_NOPE + MLA_ROPE) ** -0.5
FOX_SCALE = FOX_HEAD_DIM ** -0.5

ADAM_LR, ADAM_B1, ADAM_B2, ADAM_EPS, ADAM_WD, ADAM_STEP = 0.001, 0.9, 0.999, 1e-08, 0.01, 10

VMEM_LIMIT_BYTES = 48 * 1024 * 1024
LANES = 128
PACK_W = 1024

ROW_TILE = 512
MERGE_ROWS = 512
ATTN_TILE = 1024
LRU_CHUNK = 512


def _cparams(dims):
    return pltpu.CompilerParams(dimension_semantics=dims, vmem_limit_bytes=VMEM_LIMIT_BYTES)


def _tile(n, cap):
    if n <= cap:
        return n
    t = (cap // LANES) * LANES
    while t >= LANES:
        if n % t == 0:
            return t
        t -= LANES
    raise ValueError(f"no tile for {n} under {cap}")


def _rows(n):
    return min(ROW_TILE, n)


MM_VMEM_BUDGET = 36 * 1024 * 1024


def _mm_tiles(M, N, K, a_bytes, b_bytes, o_bytes, has_res):
    best, best_work = None, 0
    for tm in {_tile(M, c) for c in (1024, 512, 256)}:
        for tn in {_tile(N, c) for c in (1792, 1024, 512)}:
            for tk in {_tile(K, c) for c in (2048, 1408, 1024, 512)}:
                need = 2 * (tm * tk * a_bytes + tk * tn * b_bytes + tm * tn * o_bytes + (tm * tn * 4 if has_res else 0))
                need += tm * tn * 4 if tk < K else 0
                need += tm * tn * 4
                if need <= MM_VMEM_BUDGET and tm * tn * tk > best_work:
                    best, best_work = (tm, tn, tk), tm * tn * tk
    assert best is not None, (M, N, K)
    return best

def _mm(a, b, *, ta=False, tb=False, out_dtype=F32, res=None, bias=None, name):
    K, M = a.shape if ta else a.shape[::-1]
    N, K2 = b.shape if tb else b.shape[::-1]
    assert K == K2, (name, a.shape, b.shape)
    assert res is None or bias is None
    tm, tn, tk = _mm_tiles(M, N, K, a.dtype.itemsize, b.dtype.itemsize, jnp.dtype(out_dtype).itemsize, res is not None)
    nk = K // tk
    a_spec = pl.BlockSpec((tk, tm), lambda i, j, k: (k, i)) if ta else pl.BlockSpec((tm, tk), lambda i, j, k: (i, k))
    b_spec = pl.BlockSpec((tn, tk), lambda i, j, k: (j, k)) if tb else pl.BlockSpec((tk, tn), lambda i, j, k: (k, j))
    o_spec = pl.BlockSpec((tm, tn), lambda i, j, k: (i, j))
    dn = (((0,) if ta else (1,), (1,) if tb else (0,)), ((), ()))
    if bias is not None:
        res, r_spec = bias, pl.BlockSpec((1, tn), lambda i, j, k: (0, j))
    else:
        r_spec = o_spec
    has_res = res is not None

    def body(*refs):
        a_ref, b_ref = refs[0], refs[1]
        r_ref = refs[2] if has_res else None
        o_ref = refs[3] if has_res else refs[2]
        av, bv = a_ref[...], b_ref[...]
        if av.dtype != BF16:
            av = av.astype(BF16)
        if bv.dtype != BF16:
            bv = bv.astype(BF16)
        part = lax.dot_general(av, bv, dn, preferred_element_type=F32)

        def finish(total):
            if has_res:
                total = total + r_ref[...]
            o_ref[...] = total.astype(out_dtype)

        if nk == 1:
            finish(part)
        else:
            acc = refs[-1]
            k = pl.program_id(2)

            @pl.when(k == 0)
            def _():
                acc[...] = part

            @pl.when(k > 0)
            def _():
                acc[...] += part

            @pl.when(k == nk - 1)
            def _():
                finish(acc[...])

    in_specs = [a_spec, b_spec] + ([r_spec] if has_res else [])
    args = (a, b) + ((res,) if has_res else ())
    return pl.pallas_call(
        body, name=name, grid=(M // tm, N // tn, nk), in_specs=in_specs, out_specs=o_spec,
        out_shape=jax.ShapeDtypeStruct((M, N), out_dtype),
        scratch_shapes=[pltpu.VMEM((tm, tn), F32)] if nk > 1 else [],
        compiler_params=_cparams(("parallel", "parallel", "arbitrary")),
    )(*args)


def _mm_res_norm(a, b, res, g, name):
    M, K = a.shape
    N = b.shape[1]
    tm, tk = _tile(M, 1024), _tile(K, 1408)
    nk = K // tk

    def body(a_ref, b_ref, r_ref, g_ref, o_ref, h_ref, *scratch):
        part = jnp.dot(a_ref[...], b_ref[...], preferred_element_type=F32)

        def finish(total):
            xn = total + r_ref[...]
            o_ref[...] = xn
            rstd = lax.rsqrt(jnp.mean(xn * xn, axis=1, keepdims=True) + EPS)
            h_ref[...] = (xn * rstd * g_ref[...]).astype(BF16)

        if nk == 1:
            finish(part)
        else:
            acc = scratch[0]
            k = pl.program_id(1)

            @pl.when(k == 0)
            def _():
                acc[...] = part

            @pl.when(k > 0)
            def _():
                acc[...] += part

            @pl.when(k == nk - 1)
            def _():
                finish(acc[...])

    row = pl.BlockSpec((tm, N), lambda i, k: (i, 0))
    return pl.pallas_call(
        body, name=name, grid=(M // tm, nk),
        in_specs=[pl.BlockSpec((tm, tk), lambda i, k: (i, k)), pl.BlockSpec((tk, N), lambda i, k: (k, 0)), row,
                  pl.BlockSpec((1, N), lambda i, k: (0, 0))],
        out_specs=(row, row), out_shape=(jax.ShapeDtypeStruct((M, N), F32), jax.ShapeDtypeStruct((M, N), BF16)),
        scratch_shapes=[pltpu.VMEM((tm, N), F32)] if nk > 1 else [],
        compiler_params=_cparams(("parallel", "arbitrary")),
    )(a, b, res, g.reshape(1, N))


def _mm_norm_bwd(a, b, x, g, add, name):
    M, K = a.shape
    N = b.shape[0]
    tm, tk = _tile(M, 1024), _tile(K, 1408)
    nk = K // tk

    def body(a_ref, b_ref, x_ref, g_ref, add_ref, dx_ref, dg_ref, *scratch):
        i, k = pl.program_id(0), pl.program_id(1)
        part = lax.dot_general(a_ref[...], b_ref[...], _NT, preferred_element_type=F32)

        @pl.when(jnp.logical_and(i == 0, k == 0))
        def _():
            dg_ref[...] = jnp.zeros_like(dg_ref)

        def finish(dyv):
            xf = x_ref[...]
            rstd = lax.rsqrt(jnp.mean(xf * xf, axis=1, keepdims=True) + EPS)
            xhat = xf * rstd
            dxh = dyv * g_ref[...]
            dx_ref[...] = rstd * (dxh - xhat * jnp.mean(dxh * xhat, axis=1, keepdims=True)) + add_ref[...]
            dg_ref[...] += jnp.sum(dyv * xhat, axis=0, keepdims=True)

        if nk == 1:
            finish(part)
        else:
            acc = scratch[0]

            @pl.when(k == 0)
            def _():
                acc[...] = part

            @pl.when(k > 0)
            def _():
                acc[...] += part

            @pl.when(k == nk - 1)
            def _():
                finish(acc[...])

    row = pl.BlockSpec((tm, N), lambda i, k: (i, 0))
    vec = pl.BlockSpec((1, N), lambda i, k: (0, 0))
    dx, dg = pl.pallas_call(
        body, name=name, grid=(M // tm, nk),
        in_specs=[pl.BlockSpec((tm, tk), lambda i, k: (i, k)), pl.BlockSpec((N, tk), lambda i, k: (0, k)), row, vec, row],
        out_specs=(row, vec), out_shape=(jax.ShapeDtypeStruct((M, N), F32), jax.ShapeDtypeStruct((1, N), F32)),
        scratch_shapes=[pltpu.VMEM((tm, N), F32)] if nk > 1 else [],
        compiler_params=_cparams(("arbitrary", "arbitrary")),
    )(a, b, x, g.reshape(1, N), add)
    return dx, dg.reshape(N)


FFN_TILE = 1408
FFN_SUBTILES = ((0, 512), (512, 1024), (1024, 1408))
assert D_FF == 2 * FFN_TILE and FFN_SUBTILES[-1][1] == FFN_TILE
FFN_ROWS = 1024


def _ffn_pair_columns(w_gate_up):
    F = w_gate_up.shape[-1] // 2
    parts = []
    for j in range(F // FFN_TILE):
        parts += [w_gate_up[..., j * FFN_TILE:(j + 1) * FFN_TILE], w_gate_up[..., F + j * FFN_TILE:F + (j + 1) * FFN_TILE]]
    return jnp.concatenate(parts, axis=-1)


def _ffn_unpair_columns(dw):
    F = dw.shape[-1] // 2
    n = F // FFN_TILE
    blk = [dw[..., j * FFN_TILE:(j + 1) * FFN_TILE] for j in range(2 * n)]
    return jnp.concatenate(blk[0::2] + blk[1::2], axis=-1)


def _ffn_up(hn, w_pair):
    S, D = hn.shape
    W2 = w_pair.shape[1]
    F, tf = W2 // 2, FFN_TILE
    tm = min(FFN_ROWS, S)

    def body(h_ref, w_ref, hf_ref, act_ref):
        hv = h_ref[...]
        for lo, hi in FFN_SUBTILES:
            gt = jnp.dot(hv, w_ref[:, lo:hi], preferred_element_type=F32)
            up = jnp.dot(hv, w_ref[:, tf + lo:tf + hi], preferred_element_type=F32)
            hf_ref[:, lo:hi] = gt.astype(BF16)
            hf_ref[:, tf + lo:tf + hi] = up.astype(BF16)
            act_ref[:, lo:hi] = (gt * jax.nn.sigmoid(gt) * up).astype(BF16)

    return pl.pallas_call(
        body, name="ffn_up", grid=(S // tm, F // tf),
        in_specs=[pl.BlockSpec((tm, D), lambda i, j: (i, 0)), pl.BlockSpec((D, 2 * tf), lambda i, j: (0, j))],
        out_specs=(pl.BlockSpec((tm, 2 * tf), lambda i, j: (i, j)), pl.BlockSpec((tm, tf), lambda i, j: (i, j))),
        out_shape=(jax.ShapeDtypeStruct((S, W2), BF16), jax.ShapeDtypeStruct((S, F), BF16)),
        compiler_params=_cparams(("parallel", "parallel")),
    )(hn, w_pair)


def _ffn_down_bwd(dx, w_down, hf):
    S, D = dx.shape
    F, tf = w_down.shape[0], FFN_TILE
    tm = min(FFN_ROWS, S)

    def body(d_ref, w_ref, h_ref, o_ref):
        dv = d_ref[...].astype(BF16)
        for lo, hi in FFN_SUBTILES:
            dact = lax.dot_general(dv, w_ref[lo:hi, :], _NT, preferred_element_type=F32)
            gt, up = h_ref[:, lo:hi].astype(F32), h_ref[:, tf + lo:tf + hi].astype(F32)
            sg = jax.nn.sigmoid(gt)
            o_ref[:, lo:hi] = (dact * up * sg * (1.0 + gt * (1.0 - sg))).astype(BF16)
            o_ref[:, tf + lo:tf + hi] = (dact * gt * sg).astype(BF16)

    pair = pl.BlockSpec((tm, 2 * tf), lambda i, j: (i, j))
    return pl.pallas_call(
        body, name="ffn_down_bwd", grid=(S // tm, F // tf),
        in_specs=[pl.BlockSpec((tm, D), lambda i, j: (i, 0)), pl.BlockSpec((tf, D), lambda i, j: (j, 0)), pair],
        out_specs=pair, out_shape=jax.ShapeDtypeStruct((S, 2 * F), BF16),
        compiler_params=_cparams(("parallel", "parallel")),
    )(dx, w_down, hf)


def _rmsnorm_fwd(x, g, name):
    S, W = x.shape
    tm = _rows(S)

    def body(x_ref, g_ref, o_ref):
        xf = x_ref[...]
        rstd = lax.rsqrt(jnp.mean(xf * xf, axis=1, keepdims=True) + EPS)
        o_ref[...] = (xf * rstd * g_ref[...]).astype(BF16)

    return pl.pallas_call(
        body, name=name, grid=(S // tm,),
        in_specs=[pl.BlockSpec((tm, W), lambda i: (i, 0)), pl.BlockSpec((1, W), lambda i: (0, 0))],
        out_specs=pl.BlockSpec((tm, W), lambda i: (i, 0)),
        out_shape=jax.ShapeDtypeStruct((S, W), BF16), compiler_params=_cparams(("parallel",)),
    )(x, g.reshape(1, W))


def _rmsnorm_bwd(x, g, dy, *, add=None, out_dtype=F32, name):
    S, W = x.shape
    tm = _rows(S)
    has_add = add is not None

    def body(*refs):
        x_ref, g_ref, dy_ref = refs[:3]
        add_ref = refs[3] if has_add else None
        dx_ref, dg_ref = refs[-2], refs[-1]
        xf = x_ref[...]
        rstd = lax.rsqrt(jnp.mean(xf * xf, axis=1, keepdims=True) + EPS)
        xhat = xf * rstd
        dyv = dy_ref[...]
        dxh = dyv * g_ref[...]
        dx = rstd * (dxh - xhat * jnp.mean(dxh * xhat, axis=1, keepdims=True))
        if has_add:
            dx = dx + add_ref[...]
        dx_ref[...] = dx.astype(out_dtype)

        @pl.when(pl.program_id(0) == 0)
        def _():
            dg_ref[...] = jnp.zeros_like(dg_ref)

        dg_ref[...] += jnp.sum(dyv * xhat, axis=0, keepdims=True)

    row = pl.BlockSpec((tm, W), lambda i: (i, 0))
    vec = pl.BlockSpec((1, W), lambda i: (0, 0))
    dx, dg = pl.pallas_call(
        body, name=name, grid=(S // tm,),
        in_specs=[row, vec, row] + ([row] if has_add else []),
        out_specs=(row, vec),
        out_shape=(jax.ShapeDtypeStruct((S, W), out_dtype), jax.ShapeDtypeStruct((1, W), F32)),
        compiler_params=_cparams(("arbitrary",)),
    )(x, g.reshape(1, W), dy, *((add,) if has_add else ()))
    return dx, dg.reshape(W)


def _loss_head(x, g, target):
    S, W = x.shape
    tm = _rows(S)

    def body(x_ref, g_ref, t_ref, loss_ref, dx_ref, dg_ref):
        xf = x_ref[...]
        gv = g_ref[...]
        rstd = lax.rsqrt(jnp.mean(xf * xf, axis=1, keepdims=True) + EPS)
        xhat = xf * rstd
        err = xhat * gv - t_ref[...]
        part = 0.5 * jnp.sum(jnp.mean(err * err, axis=1, keepdims=True), axis=0, keepdims=True)
        dyv = err * (1.0 / W)
        dxh = dyv * gv
        dx_ref[...] = rstd * (dxh - xhat * jnp.mean(dxh * xhat, axis=1, keepdims=True))

        @pl.when(pl.program_id(0) == 0)
        def _():
            dg_ref[...] = jnp.zeros_like(dg_ref)
            loss_ref[...] = jnp.zeros_like(loss_ref)

        dg_ref[...] += jnp.sum(dyv * xhat, axis=0, keepdims=True)
        loss_ref[...] += part

    row = pl.BlockSpec((tm, W), lambda i: (i, 0))
    vec = pl.BlockSpec((1, W), lambda i: (0, 0))
    loss, dx, dg = pl.pallas_call(
        body, name="loss_head", grid=(S // tm,), in_specs=[row, vec, row],
        out_specs=(pl.BlockSpec((1, 1), lambda i: (0, 0)), row, vec),
        out_shape=(jax.ShapeDtypeStruct((1, 1), F32), jax.ShapeDtypeStruct((S, W), F32), jax.ShapeDtypeStruct((1, W), F32)),
        compiler_params=_cparams(("arbitrary",)),
    )(x, g.reshape(1, W), target)
    return loss[0, 0], dx, dg.reshape(W)


def _scan_fwd(a, b, row):
    T = a.shape[0]
    d = 1
    while d < T:
        keep = row >= d
        b = jnp.where(keep, a * pltpu.roll(b, d, axis=0) + b, b)
        a = jnp.where(keep, a * pltpu.roll(a, d, axis=0), a)
        d *= 2
    return a, b


def _scan_bwd(a, b, row):
    T = a.shape[0]
    d = 1
    while d < T:
        keep = row < T - d
        b = jnp.where(keep, a * pltpu.roll(b, T - d, axis=0) + b, b)
        a = jnp.where(keep, a * pltpu.roll(a, T - d, axis=0), a)
        d *= 2
    return a, b


def _expm1(x):
    small = x * (1.0 + x * (0.5 + x * (1.0 / 6 + x * (1.0 / 24 + x * (1.0 / 120 + x * (1.0 / 720 + x * (1.0 / 5040)))))))
    return jnp.where(jnp.abs(x) < 0.25, small, jnp.exp(x) - 1.0)


_GELU_C = math.sqrt(2.0 / math.pi)


def _gelu_and_grad(x):
    inner = _GELU_C * (x + 0.044715 * x * x * x)
    th = jnp.tanh(inner)
    val = 0.5 * x * (1.0 + th)
    grad = 0.5 * (1.0 + th) + 0.5 * x * (1.0 - th * th) * _GELU_C * (1.0 + 3 * 0.044715 * x * x)
    return val, grad


def _lru_gates(xc, wa, wx, ba, bx, lam):
    xcb = xc.astype(BF16)
    r = jax.nn.sigmoid(jnp.dot(xcb, wa, preferred_element_type=F32) + ba)
    ig = jax.nn.sigmoid(jnp.dot(xcb, wx, preferred_element_type=F32) + bx)
    sp = jax.nn.softplus(-lam)
    log_a = -LRU_C * r * sp
    a = jnp.exp(log_a)
    mult = jnp.sqrt(-_expm1(2.0 * log_a))
    return xcb, r, ig, sp, a, mult


def _lru_fwd(u, ug, conv_w, conv_b, wa_bd, wx_bd, ba, bx, lam):
    S, W = u.shape
    T = min(LRU_CHUNK, S)
    nl, nc = W // LANES, S // T

    def body(u_ref, ug_ref, cw_ref, cb_ref, wa_ref, wx_ref, ba_ref, bx_ref, lam_ref, ya_ref, xc_ref, h_ref, prev_u, h_carry):
        c = pl.program_id(1)

        @pl.when(c == 0)
        def _():
            prev_u[...] = jnp.zeros_like(prev_u)
            h_carry[...] = jnp.zeros_like(h_carry)

        uv = u_ref[...]
        row = lax.broadcasted_iota(jnp.int32, (T, LANES), 0)
        row8 = lax.broadcasted_iota(jnp.int32, (8, LANES), 0)
        cw = cw_ref[...]
        xc = cb_ref[...] + uv * cw[3:4, :]
        pv = prev_u[...]
        for k in range(1, CONV_WIDTH):
            us = pltpu.roll(uv, k, axis=0)
            top = jnp.where(row8 < k, pltpu.roll(pv, k, axis=0), us[0:8])
            us = jnp.concatenate([top, us[8:]], axis=0)
            xc = xc + us * cw[3 - k:4 - k, :]
        prev_u[...] = uv[T - 8:T]
        _, r, ig, sp, a, mult = _lru_gates(xc, wa_ref[...], wx_ref[...], ba_ref[...], bx_ref[...], lam_ref[...])
        bb = mult * (ig * xc)
        aa, hh = _scan_fwd(a, bb, row)
        h = hh + aa * h_carry[7:8, :]
        h_carry[...] = h[T - 8:T]
        gl, _ = _gelu_and_grad(ug_ref[...])
        ya_ref[...] = (h * gl).astype(BF16)
        xc_ref[...] = xc
        h_ref[...] = h

    seq = pl.BlockSpec((T, LANES), lambda l, c: (c, l))
    vec = pl.BlockSpec((1, LANES), lambda l, c: (0, l))
    mat = pl.BlockSpec((None, LANES, LANES), lambda l, c: (l, 0, 0))
    return pl.pallas_call(
        body, name="lru_fwd", grid=(nl, nc),
        in_specs=[seq, seq, pl.BlockSpec((CONV_WIDTH, LANES), lambda l, c: (0, l)), vec, mat, mat, vec, vec, vec],
        out_specs=(seq, seq, seq),
        out_shape=(jax.ShapeDtypeStruct((S, W), BF16), jax.ShapeDtypeStruct((S, W), F32), jax.ShapeDtypeStruct((S, W), F32)),
        scratch_shapes=[pltpu.VMEM((8, LANES), F32), pltpu.VMEM((8, LANES), F32)],
        compiler_params=_cparams(("parallel", "arbitrary")),
    )(u, ug, conv_w, conv_b.reshape(1, W), wa_bd, wx_bd, ba.reshape(1, W), bx.reshape(1, W), lam.reshape(1, W))


def _lru_bwd(dya, u, ug, xc, h, conv_w, wa_bd, wx_bd, ba, bx, lam):
    S, W = u.shape
    T = min(LRU_CHUNK, S)
    nl, nc = W // LANES, S // T
    tb8 = T // 8

    def body(dya_ref, u_ref, ug_ref, xc_ref, h_ref, hp_ref, cw_ref, wa_ref, wx_ref, ba_ref, bx_ref, lam_ref,
             du_ref, dug_ref, dcw_ref, dcb_ref, dba_ref, dbx_ref, dlam_ref, dwa_ref, dwx_ref,
             g_next, a_next, dxc_next):
        c = pl.program_id(1)

        @pl.when(c == 0)
        def _():
            g_next[...] = jnp.zeros_like(g_next)
            a_next[...] = jnp.zeros_like(a_next)
            dxc_next[...] = jnp.zeros_like(dxc_next)
            for ref in (dcw_ref, dcb_ref, dba_ref, dbx_ref, dlam_ref, dwa_ref, dwx_ref):
                ref[...] = jnp.zeros_like(ref)

        row = lax.broadcasted_iota(jnp.int32, (T, LANES), 0)
        row8 = lax.broadcasted_iota(jnp.int32, (8, LANES), 0)
        xcv = xc_ref[...]
        wa, wx = wa_ref[...], wx_ref[...]
        xcb, r, ig, sp, a, mult = _lru_gates(xcv, wa, wx, ba_ref[...], bx_ref[...], lam_ref[...])
        gl, dgl = _gelu_and_grad(ug_ref[...])
        dyav = dya_ref[...]
        hv = h_ref[...]
        dug_ref[...] = (dyav * hv * dgl).astype(BF16)
        dh = dyav * gl
        a_up = pltpu.roll(a, T - 1, axis=0)
        a_up = jnp.where(row == T - 1, a_next[0:1, :], a_up)
        prod, gg = _scan_bwd(a_up, dh, row)
        g = gg + prod * g_next[0:1, :]
        h_prev = pltpu.roll(hv, 1, axis=0)
        first_chunk = c == nc - 1
        h_before = jnp.where(first_chunk, 0.0, hp_ref[7:8, :])
        h_prev = jnp.where(row == 0, h_before, h_prev)
        da = g * h_prev
        d_mult = g * (ig * xcv)
        d_ig = g * mult * xcv
        dxc = g * mult * ig
        d_log_a = da * a - d_mult * (a * a) / mult
        d_r = d_log_a * (-LRU_C * sp)
        d_pa = d_r * r * (1.0 - r)
        d_px = d_ig * ig * (1.0 - ig)
        d_pab, d_pxb = d_pa.astype(BF16), d_px.astype(BF16)
        nt = (((1,), (1,)), ((), ()))
        tn = (((0,), (0,)), ((), ()))
        dxc = dxc + lax.dot_general(d_pab, wa, nt, preferred_element_type=F32) + lax.dot_general(d_pxb, wx, nt, preferred_element_type=F32)
        dwa_ref[...] += lax.dot_general(xcb, d_pab, tn, preferred_element_type=F32)
        dwx_ref[...] += lax.dot_general(xcb, d_pxb, tn, preferred_element_type=F32)
        dlam_ref[...] += jnp.sum(d_log_a * r, axis=0, keepdims=True)
        dba_ref[...] += jnp.sum(d_pa, axis=0, keepdims=True)
        dbx_ref[...] += jnp.sum(d_px, axis=0, keepdims=True)
        dcb_ref[...] += jnp.sum(dxc, axis=0, keepdims=True)
        uv = u_ref[...]
        cw = cw_ref[...]
        nxt = dxc_next[...]
        du = dxc * cw[3:4, :]
        dcw_ref[3:4, :] += jnp.sum(uv * dxc, axis=0, keepdims=True)
        for k in range(1, CONV_WIDTH):
            ds = pltpu.roll(dxc, T - k, axis=0)
            bot = jnp.where(row8 >= 8 - k, pltpu.roll(nxt, 8 - k, axis=0), ds[T - 8:T])
            ds = jnp.concatenate([ds[:T - 8], bot], axis=0)
            du = du + ds * cw[3 - k:4 - k, :]
            dcw_ref[3 - k:4 - k, :] += jnp.sum(uv * ds, axis=0, keepdims=True)
        du_ref[...] = du.astype(BF16)
        g_next[...] = g[0:8]
        a_next[...] = a[0:8]
        dxc_next[...] = dxc[0:8]

    seq = pl.BlockSpec((T, LANES), lambda l, c: (nc - 1 - c, l))
    before = pl.BlockSpec((8, LANES), lambda l, c: (jnp.maximum((nc - 1 - c) * tb8 - 1, 0), l))
    vec = pl.BlockSpec((1, LANES), lambda l, c: (0, l))
    cwb = pl.BlockSpec((CONV_WIDTH, LANES), lambda l, c: (0, l))
    mat = pl.BlockSpec((None, LANES, LANES), lambda l, c: (l, 0, 0))
    vshape = jax.ShapeDtypeStruct((1, W), F32)
    mshape = jax.ShapeDtypeStruct((nl, LANES, LANES), F32)
    return pl.pallas_call(
        body, name="lru_bwd", grid=(nl, nc),
        in_specs=[seq, seq, seq, seq, seq, before, cwb, mat, mat, vec, vec, vec],
        out_specs=(seq, seq, cwb, vec, vec, vec, vec, mat, mat),
        out_shape=(jax.ShapeDtypeStruct((S, W), BF16), jax.ShapeDtypeStruct((S, W), BF16),
                   jax.ShapeDtypeStruct((CONV_WIDTH, W), F32), vshape, vshape, vshape, vshape, mshape, mshape),
        scratch_shapes=[pltpu.VMEM((8, LANES), F32)] * 3,
        compiler_params=_cparams(("parallel", "arbitrary")),
    )(dya, u, ug, xc, h, h, conv_w, wa_bd, wx_bd, ba.reshape(1, W), bx.reshape(1, W), lam.reshape(1, W))


def _decay_fwd(f_logit, bf):
    S = f_logit.shape[0]
    T = min(LRU_CHUNK, S)

    def body(f_ref, b_ref, o_ref, carry):
        @pl.when(pl.program_id(0) == 0)
        def _():
            carry[...] = jnp.zeros_like(carry)

        row = lax.broadcasted_iota(jnp.int32, (T, LANES), 0)
        v = jax.nn.log_sigmoid(f_ref[...] + b_ref[...])
        d = 1
        while d < T:
            v = jnp.where(row >= d, v + pltpu.roll(v, d, axis=0), v)
            d *= 2
        v = v + carry[7:8, :]
        carry[...] = v[T - 8:T]
        o_ref[...] = v

    return pl.pallas_call(
        body, name="decay_fwd", grid=(S // T,),
        in_specs=[pl.BlockSpec((T, LANES), lambda c: (c, 0)), pl.BlockSpec((1, LANES), lambda c: (0, 0))],
        out_specs=pl.BlockSpec((T, LANES), lambda c: (c, 0)),
        out_shape=jax.ShapeDtypeStruct((S, LANES), F32), scratch_shapes=[pltpu.VMEM((8, LANES), F32)],
        compiler_params=_cparams(("arbitrary",)),
    )(f_logit, bf)


def _decay_bwd(d_dec, f_logit, bf):
    S = f_logit.shape[0]
    T = min(LRU_CHUNK, S)
    nc = S // T

    def body(dd_ref, f_ref, b_ref, df_ref, db_ref, carry):
        @pl.when(pl.program_id(0) == 0)
        def _():
            carry[...] = jnp.zeros_like(carry)
            db_ref[...] = jnp.zeros_like(db_ref)

        row = lax.broadcasted_iota(jnp.int32, (T, LANES), 0)
        v = dd_ref[...]
        d = 1
        while d < T:
            v = jnp.where(row < T - d, v + pltpu.roll(v, T - d, axis=0), v)
            d *= 2
        v = v + carry[0:1, :]
        carry[...] = v[0:8]
        df = v * jax.nn.sigmoid(-(f_ref[...] + b_ref[...]))
        df_ref[...] = df.astype(BF16)
        db_ref[...] += jnp.sum(df, axis=0, keepdims=True)

    seq = pl.BlockSpec((T, LANES), lambda c: (nc - 1 - c, 0))
    vec = pl.BlockSpec((1, LANES), lambda c: (0, 0))
    return pl.pallas_call(
        body, name="decay_bwd", grid=(nc,), in_specs=[seq, seq, vec], out_specs=(seq, vec),
        out_shape=(jax.ShapeDtypeStruct((S, LANES), BF16), jax.ShapeDtypeStruct((1, LANES), F32)),
        scratch_shapes=[pltpu.VMEM((8, LANES), F32)], compiler_params=_cparams(("arbitrary",)),
    )(d_dec, f_logit, bf)


def _rope_tables(S):
    pos = jnp.arange(S, dtype=F32)
    inv_freq = ROPE_BASE ** (-jnp.arange(0, MLA_ROPE, 2, dtype=F32) / MLA_ROPE)
    ang = pos[:, None] * inv_freq[None, :]
    cos, sin = jnp.cos(ang), jnp.sin(ang)
    half = MLA_ROPE // 2
    z = lambda n: jnp.zeros((S, n), F32)
    c_q = jnp.concatenate([jnp.ones((S, MLA_NOPE), F32), cos, cos, z(HEAD_PAD - MLA_NOPE - MLA_ROPE)], axis=1)
    c_k = jnp.concatenate([z(MLA_NOPE), cos, cos, z(HEAD_PAD - MLA_NOPE - MLA_ROPE)], axis=1)
    s_lo = jnp.concatenate([z(MLA_NOPE), -sin, z(HEAD_PAD - MLA_NOPE - half)], axis=1)
    s_hi = jnp.concatenate([z(MLA_NOPE + half), sin, z(HEAD_PAD - MLA_NOPE - MLA_ROPE)], axis=1)
    return c_q, c_k, s_lo, s_hi


def _rot(v, c, s_lo, s_hi):
    half = MLA_ROPE // 2
    return v * c + pltpu.roll(v, LANES - half, axis=1) * s_lo + pltpu.roll(v, half, axis=1) * s_hi


def _rot_t(dv, c, s_lo, s_hi):
    half = MLA_ROPE // 2
    return dv * c + pltpu.roll(dv * s_lo, half, axis=1) + pltpu.roll(dv * s_hi, LANES - half, axis=1)


def _rope_q(q_pre, c_q, s_lo, s_hi, *, transpose, out_dtype, name):
    S, W = q_pre.shape
    tm = _rows(S)
    fn = _rot_t if transpose else _rot

    def body(q_ref, c_ref, lo_ref, hi_ref, o_ref):
        c, lo, hi = c_ref[...], lo_ref[...], hi_ref[...]
        for hd in range(W // LANES):
            cols = slice(hd * LANES, (hd + 1) * LANES)
            o_ref[:, cols] = fn(q_ref[:, cols] * MLA_SCALE, c, lo, hi).astype(out_dtype)

    blk = pl.BlockSpec((tm, W), lambda i: (i, 0))
    tab = pl.BlockSpec((tm, LANES), lambda i: (i, 0))
    return pl.pallas_call(
        body, name=name, grid=(S // tm,), in_specs=[blk, tab, tab, tab], out_specs=blk,
        out_shape=jax.ShapeDtypeStruct((S, W), out_dtype), compiler_params=_cparams(("parallel",)),
    )(q_pre, c_q, s_lo, s_hi)


def _rope_k(k_pre, k_rope, c_k, s_lo, s_hi):
    S, W = k_pre.shape
    tm = _rows(S)

    def body(k_ref, r_ref, c_ref, lo_ref, hi_ref, o_ref):
        rot = _rot(r_ref[...], c_ref[...], lo_ref[...], hi_ref[...])
        for hd in range(W // LANES):
            cols = slice(hd * LANES, (hd + 1) * LANES)
            o_ref[:, cols] = (k_ref[:, cols] + rot).astype(BF16)

    blk = pl.BlockSpec((tm, W), lambda i: (i, 0))
    tab = pl.BlockSpec((tm, LANES), lambda i: (i, 0))
    return pl.pallas_call(
        body, name="rope_k", grid=(S // tm,), in_specs=[blk, tab, tab, tab, tab], out_specs=blk,
        out_shape=jax.ShapeDtypeStruct((S, W), BF16), compiler_params=_cparams(("parallel",)),
    )(k_pre, k_rope, c_k, s_lo, s_hi)


def _rope_k_bwd(dk, c_k, s_lo, s_hi):
    S, W = dk.shape
    tm = _rows(S)

    def body(dk_ref, c_ref, lo_ref, hi_ref, o_ref):
        tot = dk_ref[:, 0:LANES]
        for hd in range(1, W // LANES):
            tot = tot + dk_ref[:, hd * LANES:(hd + 1) * LANES]
        o_ref[...] = _rot_t(tot, c_ref[...], lo_ref[...], hi_ref[...]).astype(BF16)

    tab = pl.BlockSpec((tm, LANES), lambda i: (i, 0))
    return pl.pallas_call(
        body, name="rope_k_bwd", grid=(S // tm,), in_specs=[pl.BlockSpec((tm, W), lambda i: (i, 0)), tab, tab, tab],
        out_specs=tab, out_shape=jax.ShapeDtypeStruct((S, LANES), BF16), compiler_params=_cparams(("parallel",)),
    )(dk, c_k, s_lo, s_hi)


def _pairs(n):
    pr = [(i, j) for i in range(n) for j in range(i + 1)]
    return (jnp.asarray(np.array([p[0] for p in pr], np.int32)), jnp.asarray(np.array([p[1] for p in pr], np.int32)), len(pr))


def _unit_mask(shape, unit, q_off=0):
    q = lax.broadcasted_iota(jnp.int32, shape, 0) + q_off
    k = lax.broadcasted_iota(jnp.int32, shape, 1)
    if unit > 1:
        q, k = q // unit, k // unit
    return q >= k


_NT = (((1,), (1,)), ((), ()))


ONES_LANE = 64
ROW_SPLIT = 1
ROW_SPLIT_BWD = 4


def _ones_lane_bias():
    one = np.zeros((HEADS, HEAD_PAD), np.float32)
    one[:, ONES_LANE] = 1.0
    return jnp.asarray(one.reshape(1, HEADS * HEAD_PAD))


def _lane_sum(t):
    tot = t[:, 0:LANES]
    for c in range(1, t.shape[1] // LANES):
        tot = tot + t[:, c * LANES:(c + 1) * LANES]
    return tot


def _fa_fwd(q, k, v, dec_row, *, unit, name):
    S, W = q.shape
    H = W // LANES
    T = min(ATTN_TILE, S)
    n, hT = S // T, T // ROW_SPLIT
    qi, kj, npairs = _pairs(n)
    has_dec = dec_row is not None

    def body(qi_ref, kj_ref, *refs):
        if has_dec:
            q_ref, k_ref, v_ref, dr_ref, o_ref, lse_ref, m_s, acc = refs
        else:
            q_ref, k_ref, v_ref, o_ref, lse_ref, m_s, acc = refs
        t = pl.program_id(1)
        i, j = qi_ref[t], kj_ref[t]

        @pl.when(j == 0)
        def _():
            m_s[...] = jnp.full_like(m_s, NEG_INF)
            acc[...] = jnp.zeros_like(acc)

        def step(diag):
            for r in range(ROW_SPLIT):
                rows = slice(r * hT, (r + 1) * hT)
                nk = (r + 1) * hT if diag else T
                s = lax.dot_general(q_ref[rows, :], k_ref[0:nk, :], _NT, preferred_element_type=F32)
                if has_dec:
                    s = s - dr_ref[:, 0:nk]
                if diag:
                    s = jnp.where(_unit_mask((hT, nk), unit, r * hT), s, NEG_INF)
                m_prev = m_s[rows, :]
                m_new = jnp.maximum(m_prev, jnp.max(s, axis=1, keepdims=True))
                alpha = jnp.exp(m_prev - m_new)
                p = jnp.exp(s - jnp.tile(m_new, (1, nk // LANES)))
                acc[rows, :] = alpha * acc[rows, :] + jnp.dot(p.astype(BF16), v_ref[0:nk, :], preferred_element_type=F32)
                m_s[rows, :] = m_new

        @pl.when(j < i)
        def _():
            step(False)

        @pl.when(j == i)
        def _():
            step(True)
            av = acc[...]
            l = av[:, ONES_LANE:ONES_LANE + 1]
            lane = lax.broadcasted_iota(jnp.int32, (T, LANES), 1)
            o_ref[...] = jnp.where(lane < ONES_LANE, av / l, 0.0).astype(BF16)
            lse_ref[...] = m_s[...] + jnp.log(l)

    qb = pl.BlockSpec((T, LANES), lambda h, t, qi, kj: (qi[t], h))
    kb = pl.BlockSpec((T, LANES), lambda h, t, qi, kj: (kj[t], h))
    repq = pl.BlockSpec((None, T, LANES), lambda h, t, qi, kj: (h, qi[t], 0))
    rowk = pl.BlockSpec((None, 1, T), lambda h, t, qi, kj: (h, 0, kj[t]))
    in_specs = [qb, kb, kb] + ([rowk] if has_dec else [])
    args = (q, k, v) + ((dec_row,) if has_dec else ())
    return pl.pallas_call(
        body, name=name,
        grid_spec=pltpu.PrefetchScalarGridSpec(
            num_scalar_prefetch=2, grid=(H, npairs), in_specs=in_specs, out_specs=(qb, repq),
            scratch_shapes=[pltpu.VMEM((T, LANES), F32), pltpu.VMEM((T, LANES), F32)]),
        out_shape=(jax.ShapeDtypeStruct((S, W), BF16), jax.ShapeDtypeStruct((H, S, LANES), F32)),
        compiler_params=_cparams(("parallel", "arbitrary")),
    )(qi, kj, *args)


_TN = (((0,), (0,)), ((), ()))


def _fa_bwd_fused(q, k, v, do, o, lse, dec_row, *, unit, dq_dtype, dk_dtype, name):
    S, W = q.shape
    H = W // LANES
    T = min(ATTN_TILE, S)
    n, hT = S // T, T // ROW_SPLIT_BWD
    qi, kj, npairs = _pairs(n)
    has_dec = dec_row is not None

    def body(qi_ref, kj_ref, *refs):
        if has_dec:
            (q_ref, k_ref, v_ref, do_ref, o_ref, lse_ref, dr_ref, dq_ref, dk_ref, dv_ref, ddq_ref, ddk_ref,
             qacc, kacc, vacc, dl_ref, rsum, csum) = refs
        else:
            q_ref, k_ref, v_ref, do_ref, o_ref, lse_ref, dq_ref, dk_ref, dv_ref, qacc, kacc, vacc, dl_ref = refs
        t = pl.program_id(1)
        i, j = qi_ref[t], kj_ref[t]

        @pl.when(t == 0)
        def _():
            kacc[...] = jnp.zeros_like(kacc)
            vacc[...] = jnp.zeros_like(vacc)
            if has_dec:
                csum[...] = jnp.zeros_like(csum)

        @pl.when(j == 0)
        def _():
            qacc[...] = jnp.zeros_like(qacc)
            delta = jnp.sum(do_ref[...].astype(F32) * o_ref[...].astype(F32), axis=1, keepdims=True)
            dl_ref[...] = jnp.broadcast_to(delta, (T, LANES))
            if has_dec:
                rsum[...] = jnp.zeros_like(rsum)

        def step(diag):
            for r in range(ROW_SPLIT_BWD):
                rows = slice(r * hT, (r + 1) * hT)
                nk = (r + 1) * hT if diag else T
                qv, dov, kv = q_ref[rows, :], do_ref[rows, :], k_ref[0:nk, :]
                s = lax.dot_general(qv, kv, _NT, preferred_element_type=F32)
                if has_dec:
                    s = s - dr_ref[:, 0:nk]
                if diag:
                    s = jnp.where(_unit_mask((hT, nk), unit, r * hT), s, NEG_INF)
                p = jnp.exp(s - jnp.tile(lse_ref[rows, :], (1, nk // LANES)))
                dp = lax.dot_general(dov, v_ref[0:nk, :], _NT, preferred_element_type=F32)
                ds = p * (dp - jnp.tile(dl_ref[rows, :], (1, nk // LANES)))
                pb, dsb = p.astype(BF16), ds.astype(BF16)
                qacc[rows, :] += jnp.dot(dsb, kv, preferred_element_type=F32)
                vacc[j, 0:nk, :] += lax.dot_general(pb, dov, _TN, preferred_element_type=F32)
                kacc[j, 0:nk, :] += lax.dot_general(dsb, qv, _TN, preferred_element_type=F32)
                if has_dec:
                    rsum[rows, :] += _lane_sum(ds)
                    csum[j, :, 0:nk] -= jnp.sum(ds, axis=0, keepdims=True)

        @pl.when(j < i)
        def _():
            step(False)

        @pl.when(j == i)
        def _():
            step(True)
            dq_ref[...] = qacc[...].astype(dq_dtype)
            if has_dec:
                ddq_ref[...] = jnp.broadcast_to(jnp.sum(rsum[...], axis=1, keepdims=True), (T, LANES))

        @pl.when(t == npairs - 1)
        def _():
            for jj in range(n):
                dk_ref[jj * T:(jj + 1) * T, :] = kacc[jj].astype(dk_dtype)
                dv_ref[jj * T:(jj + 1) * T, :] = vacc[jj].astype(BF16)
                if has_dec:
                    ddk_ref[:, jj * T:(jj + 1) * T] = csum[jj]

    qb = pl.BlockSpec((T, LANES), lambda h, t, qi, kj: (qi[t], h))
    kb = pl.BlockSpec((T, LANES), lambda h, t, qi, kj: (kj[t], h))
    head = pl.BlockSpec((S, LANES), lambda h, t, qi, kj: (0, h))
    repq = pl.BlockSpec((None, T, LANES), lambda h, t, qi, kj: (h, qi[t], 0))
    rowk = pl.BlockSpec((None, 1, T), lambda h, t, qi, kj: (h, 0, kj[t]))
    rowh = pl.BlockSpec((None, 1, S), lambda h, t, qi, kj: (h, 0, 0))
    in_specs = [qb, kb, kb, qb, qb, repq] + ([rowk] if has_dec else [])
    args = (q, k, v, do, o, lse) + ((dec_row,) if has_dec else ())
    out_specs = [qb, head, head] + ([repq, rowh] if has_dec else [])
    out_shape = [jax.ShapeDtypeStruct((S, W), dq_dtype), jax.ShapeDtypeStruct((S, W), dk_dtype), jax.ShapeDtypeStruct((S, W), BF16)]
    scratch = [pltpu.VMEM((T, LANES), F32), pltpu.VMEM((n, T, LANES), F32), pltpu.VMEM((n, T, LANES), F32),
               pltpu.VMEM((T, LANES), F32)]
    if has_dec:
        out_shape += [jax.ShapeDtypeStruct((H, S, LANES), F32), jax.ShapeDtypeStruct((H, 1, S), F32)]
        scratch += [pltpu.VMEM((T, LANES), F32), pltpu.VMEM((n, 1, T), F32)]
    res = pl.pallas_call(
        body, name=name,
        grid_spec=pltpu.PrefetchScalarGridSpec(num_scalar_prefetch=2, grid=(H, npairs), in_specs=in_specs,
                                               out_specs=tuple(out_specs), scratch_shapes=scratch),
        out_shape=tuple(out_shape),
        compiler_params=pltpu.CompilerParams(dimension_semantics=("parallel", "arbitrary"), vmem_limit_bytes=FUSED_BWD_VMEM_BYTES),
    )(qi, kj, *args)
    return res if has_dec else (res[0], res[1], res[2], None, None)


FUSED_BWD_VMEM_BYTES = 58 * 1024 * 1024


def _fa_bwd(q, k, v, o, lse, do, dec_row, *, unit, dq_dtype, dk_dtype, name):
    dq, dk, dv, dd_q, dd_k = _fa_bwd_fused(q, k, v, do, o, lse, dec_row, unit=unit, dq_dtype=dq_dtype, dk_dtype=dk_dtype, name=name)
    if dd_k is None:
        return dq, dk, dv, None
    return dq, dk, dv, jnp.max(dd_q, axis=2) + dd_k.reshape(dd_k.shape[0], dd_k.shape[2])


def _merge_fwd(ya, yb, yc, gate_logit, gate_b):
    S, D = ya.shape
    tm = min(MERGE_ROWS, S)

    def body(a_ref, b_ref, c_ref, gl_ref, gb_ref, o_ref):
        g = jax.nn.sigmoid(gl_ref[...] + gb_ref[...])
        o_ref[...] = (g[:, 0:D] * a_ref[...] + g[:, D:2 * D] * b_ref[...] + g[:, 2 * D:3 * D] * c_ref[...]).astype(BF16)

    row = pl.BlockSpec((tm, D), lambda i: (i, 0))
    return pl.pallas_call(
        body, name="merge_fwd", grid=(S // tm,),
        in_specs=[row, row, row, pl.BlockSpec((tm, 3 * D), lambda i: (i, 0)), pl.BlockSpec((1, 3 * D), lambda i: (0, 0))],
        out_specs=row, out_shape=jax.ShapeDtypeStruct((S, D), BF16), compiler_params=_cparams(("parallel",)),
    )(ya, yb, yc, gate_logit, gate_b.reshape(1, 3 * D))


def _merge_bwd(dm, ya, yb, yc, gate_logit, gate_b):
    S, D = ya.shape
    tm = min(MERGE_ROWS, S)

    def body(dm_ref, a_ref, b_ref, c_ref, gl_ref, gb_ref, da_ref, db_ref, dc_ref, dgl_ref, dgb_ref):
        g = jax.nn.sigmoid(gl_ref[...] + gb_ref[...])
        dmv = dm_ref[...]
        parts = []
        for n, (y_ref, dy_ref) in enumerate(((a_ref, da_ref), (b_ref, db_ref), (c_ref, dc_ref))):
            gn = g[:, n * D:(n + 1) * D]
            dy_ref[...] = (dmv * gn).astype(BF16)
            parts.append(dmv * y_ref[...] * gn * (1.0 - gn))
        dgl = jnp.concatenate(parts, axis=1)
        dgl_ref[...] = dgl.astype(BF16)

        @pl.when(pl.program_id(0) == 0)
        def _():
            dgb_ref[...] = jnp.zeros_like(dgb_ref)

        dgb_ref[...] += jnp.sum(dgl, axis=0, keepdims=True)

    row = pl.BlockSpec((tm, D), lambda i: (i, 0))
    wide = pl.BlockSpec((tm, 3 * D), lambda i: (i, 0))
    vec = pl.BlockSpec((1, 3 * D), lambda i: (0, 0))
    act = jax.ShapeDtypeStruct((S, D), BF16)
    da, db, dc, dgl, dgb = pl.pallas_call(
        body, name="merge_bwd", grid=(S // tm,), in_specs=[row, row, row, row, wide, vec],
        out_specs=(row, row, row, wide, vec),
        out_shape=(act, act, act, jax.ShapeDtypeStruct((S, 3 * D), BF16), jax.ShapeDtypeStruct((1, 3 * D), F32)),
        compiler_params=_cparams(("arbitrary",)),
    )(dm, ya, yb, yc, gate_logit, gate_b.reshape(1, 3 * D))
    return da, db, dc, dgl, dgb.reshape(3 * D)


def _ple_fwd(x, pre, e, g_next):
    S, D = x.shape
    tm = _rows(S)
    with_norm = g_next is not None

    def body(*refs):
        x_ref, p_ref, e_ref = refs[:3]
        xn = x_ref[...] + jax.nn.sigmoid(p_ref[...]) * e_ref[...]
        if with_norm:
            g_ref, o_ref, h_ref = refs[3:]
            rstd = lax.rsqrt(jnp.mean(xn * xn, axis=1, keepdims=True) + EPS)
            h_ref[...] = (xn * rstd * g_ref[...]).astype(BF16)
        else:
            o_ref = refs[3]
        o_ref[...] = xn

    row = pl.BlockSpec((tm, D), lambda i: (i, 0))
    xs = jax.ShapeDtypeStruct((S, D), F32)
    if not with_norm:
        return pl.pallas_call(body, name="ple_fwd_last", grid=(S // tm,), in_specs=[row, row, row], out_specs=row,
                              out_shape=xs, compiler_params=_cparams(("parallel",)))(x, pre, e), None
    return pl.pallas_call(body, name="ple_fwd", grid=(S // tm,), in_specs=[row, row, row, pl.BlockSpec((1, D), lambda i: (0, 0))],
                          out_specs=(row, row), out_shape=(xs, jax.ShapeDtypeStruct((S, D), BF16)),
                          compiler_params=_cparams(("parallel",)))(x, pre, e, g_next.reshape(1, D))


def _ple_bwd(dx, pre, e):
    S, D = dx.shape
    tm = _rows(S)

    def body(dx_ref, p_ref, e_ref, dp_ref, de_ref):
        pg = jax.nn.sigmoid(p_ref[...])
        dxv = dx_ref[...]
        dp_ref[...] = (dxv * e_ref[...] * pg * (1.0 - pg)).astype(BF16)
        de_ref[...] = (dxv * pg).astype(BF16)

    row = pl.BlockSpec((tm, D), lambda i: (i, 0))
    act = jax.ShapeDtypeStruct((S, D), BF16)
    return pl.pallas_call(body, name="ple_bwd", grid=(S // tm,), in_specs=[row, row, row], out_specs=(row, row),
                          out_shape=(act, act), compiler_params=_cparams(("parallel",)))(dx, pre, e)


def _pad_heads(w, real):
    K = w.shape[0]
    w = w.reshape(K, HEADS, real)
    return jnp.pad(w, ((0, 0), (0, 0), (0, HEAD_PAD - real))).reshape(K, HEADS * HEAD_PAD)


def _unpad_heads(w, real):
    K = w.shape[0]
    return w.reshape(K, HEADS, HEAD_PAD)[:, :, :real].reshape(K, HEADS * real)


def _pad_head_rows(w, real):
    N = w.shape[1]
    w = w.reshape(HEADS, real, N)
    return jnp.pad(w, ((0, 0), (0, HEAD_PAD - real), (0, 0))).reshape(HEADS * HEAD_PAD, N)


def _unpad_head_rows(w, real):
    N = w.shape[1]
    return w.reshape(HEADS, HEAD_PAD, N)[:, :real].reshape(HEADS * real, N)


def _block_diag(w):
    w = w.reshape(4, 2, 64, 64)
    z = jnp.zeros((4, 64, 64), w.dtype)
    top = jnp.concatenate([w[:, 0], z], axis=2)
    bot = jnp.concatenate([z, w[:, 1]], axis=2)
    return jnp.concatenate([top, bot], axis=1)


def _block_diag_t(w):
    return jnp.stack([w[:, :64, :64], w[:, 64:, 64:]], axis=1).reshape(8, 64, 64)


_IN_SPLITS = (512, 512, 384, 288, 512, 512, 512, 8, 3072)
_IN_OFF = np.concatenate([[0], np.cumsum(_IN_SPLITS)])
_KR_OFF = 64
_SEG_NAMES = ("u", "ug", "cq", "ckv", "kr", "fq", "fk", "fv", "fl", "gate")


def _in_segments(w_in):
    c = lambda n: w_in[:, int(_IN_OFF[n]):int(_IN_OFF[n + 1])]
    kv = c(3)
    kr = jnp.pad(kv[:, MLA_KV_LORA:], ((0, 0), (_KR_OFF, LANES - _KR_OFF - MLA_ROPE)))
    fl = jnp.pad(c(7), ((0, 0), (0, LANES - HEADS)))
    fq = _pad_heads(c(4), FOX_HEAD_DIM) * jnp.asarray(FOX_SCALE, w_in.dtype)
    return [c(0), c(1), c(2), kv[:, :MLA_KV_LORA], kr, fq, _pad_heads(c(5), FOX_HEAD_DIM), _pad_heads(c(6), FOX_HEAD_DIM), fl, c(8)]


def _in_unsegment(dw_p, widths):
    offs = np.concatenate([[0], np.cumsum(widths)])
    seg = [dw_p[:, int(offs[n]):int(offs[n + 1])] for n in range(len(widths))]
    u, ug, cq, ckv, kr, fq, fk, fv, fl, gate = seg
    return jnp.concatenate([
        u, ug, cq, ckv, kr[:, _KR_OFF:_KR_OFF + MLA_ROPE], _unpad_heads(fq, FOX_HEAD_DIM) * FOX_SCALE,
        _unpad_heads(fk, FOX_HEAD_DIM), _unpad_heads(fv, FOX_HEAD_DIM), fl[:, :HEADS], gate], axis=1)


def _split_wuq(wuq):
    return _pad_heads(wuq, MLA_NOPE + MLA_ROPE)


def _split_wukv(wukv):
    w = wukv.reshape(MLA_KV_LORA, HEADS, MLA_NOPE + MLA_V)
    pad = lambda t: jnp.pad(t, ((0, 0), (0, 0), (0, HEAD_PAD - t.shape[2]))).reshape(MLA_KV_LORA, HEADS * HEAD_PAD)
    return pad(w[:, :, :MLA_NOPE]), pad(w[:, :, MLA_NOPE:])


def _merge_wukv(dk_p, dv_p):
    k = dk_p.reshape(MLA_KV_LORA, HEADS, HEAD_PAD)[:, :, :MLA_NOPE]
    v = dv_p.reshape(MLA_KV_LORA, HEADS, HEAD_PAD)[:, :, :MLA_V]
    return jnp.concatenate([k, v], axis=2).reshape(MLA_KV_LORA, HEADS * (MLA_NOPE + MLA_V))


def _layer_fwd(x, h, p_i, w, g_next, tabs):
    c_q, c_k, s_lo, s_hi = tabs
    sv = {"x0": x}
    segs = _in_segments(w["w_in"])
    z = {}
    for nm, ws in zip(_SEG_NAMES, segs):
        z[nm] = _mm(h, ws, out_dtype=BF16 if nm in ("fq", "fk", "fv", "gate") else F32, bias=_ones_lane_bias() if nm == "fv" else None,
                    name="in_" + nm)
    sv.update(h=h, z=z)
    wa_bd, wx_bd = _block_diag(w["lru_wa"]).astype(BF16), _block_diag(w["lru_wx"]).astype(BF16)
    oa, xc, hs = _lru_fwd(z["u"], z["ug"], w["conv_w"], w["conv_b"], wa_bd, wx_bd, w["lru_ba"], w["lru_bx"], w["lru_lambda"])
    sv.update(oa=oa, xc=xc, hs=hs)
    qn = _rmsnorm_fwd(z["cq"], w["mla_q_norm"], "q_norm_fwd")
    kvn = _rmsnorm_fwd(z["ckv"], w["mla_kv_norm"], "kv_norm_fwd")
    wuq_p = _split_wuq(w["mla_wuq"])
    wk_p, wv_p = _split_wukv(w["mla_wukv"])
    qb = _rope_q(_mm(qn, wuq_p, name="mla_q"), c_q, s_lo, s_hi, transpose=False, out_dtype=BF16, name="rope_q")
    kb = _rope_k(_mm(kvn, wk_p, name="mla_k"), z["kr"], c_k, s_lo, s_hi)
    vb = _mm(kvn, wv_p, out_dtype=BF16, bias=_ones_lane_bias(), name="mla_v")
    ob, lse_b = _fa_fwd(qb, kb, vb, None, unit=64, name="mla_attn")
    sv.update(qn=qn, kvn=kvn, qb=qb, kb=kb, vb=vb, ob=ob, lse_b=lse_b)
    bf = jnp.pad(w["fox_bf"], (0, LANES - HEADS)).reshape(1, LANES)
    dec = _decay_fwd(z["fl"], bf)
    drow = dec[:, :HEADS].T.reshape(HEADS, 1, dec.shape[0])
    oc, lse_c = _fa_fwd(z["fq"], z["fk"], z["fv"], drow, unit=1, name="fox_attn")
    sv.update(drow=drow, oc=oc, lse_c=lse_c)
    ya = _mm(oa, w["w_br_a"], out_dtype=BF16, name="br_a")
    yb = _mm(ob, _pad_head_rows(w["w_br_b"], MLA_V), out_dtype=BF16, name="br_b")
    yc = _mm(oc, _pad_head_rows(w["w_br_c"], FOX_HEAD_DIM), out_dtype=BF16, name="br_c")
    merged = _merge_fwd(ya, yb, yc, z["gate"], w["gate_b"])
    x1, hn = _mm_res_norm(merged, w["w_o"], x, w["ffn_norm"], "w_o")
    sv.update(ya=ya, yb=yb, yc=yc, merged=merged, x1=x1)
    hf, act = _ffn_up(hn, _ffn_pair_columns(w["w_gate_up"]))
    x2, pn = _mm_res_norm(act, w["w_down"], x1, w["ple_norm"], "ffn_down")
    sv.update(hn=hn, hf=hf, act=act, x2=x2)
    pre = _mm(pn, w["w_ple_gate"], name="ple_gate")
    e = _mm(p_i, w["w_ple"], name="ple_embed")
    x3, h_next = _ple_fwd(x2, pre, e, g_next)
    sv.update(pn=pn, pre=pre, e=e, p_i=p_i)
    return x3, h_next, sv


def _layer_bwd(dx3, w, sv, tabs):
    c_q, c_k, s_lo, s_hi = tabs
    g = {}
    z = sv["z"]
    dpre, de = _ple_bwd(dx3, sv["pre"], sv["e"])
    g["w_ple"] = _mm(sv["p_i"], de, ta=True, out_dtype=BF16, name="d_w_ple")
    g["w_ple_gate"] = _mm(sv["pn"], dpre, ta=True, out_dtype=BF16, name="d_w_ple_gate")
    dx2, g["ple_norm"] = _mm_norm_bwd(dpre, w["w_ple_gate"], sv["x2"], w["ple_norm"], dx3, "ple_norm_bwd")
    g["w_down"] = _mm(sv["act"], dx2, ta=True, out_dtype=BF16, name="d_w_down")
    dhf = _ffn_down_bwd(dx2, w["w_down"], sv["hf"])
    g["w_gate_up"] = _ffn_unpair_columns(_mm(sv["hn"], dhf, ta=True, out_dtype=BF16, name="d_w_gate_up"))
    dx1, g["ffn_norm"] = _mm_norm_bwd(dhf, _ffn_pair_columns(w["w_gate_up"]), sv["x1"], w["ffn_norm"], dx2, "ffn_norm_bwd")
    g["w_o"] = _mm(sv["merged"], dx1, ta=True, out_dtype=BF16, name="d_w_o")
    dm = _mm(dx1, w["w_o"], tb=True, name="d_merged")
    dya, dyb, dyc, dgate, g["gate_b"] = _merge_bwd(dm, sv["ya"], sv["yb"], sv["yc"], z["gate"], w["gate_b"])
    wbb_p, wbc_p = _pad_head_rows(w["w_br_b"], MLA_V), _pad_head_rows(w["w_br_c"], FOX_HEAD_DIM)
    g["w_br_a"] = _mm(sv["oa"], dya, ta=True, out_dtype=BF16, name="d_w_br_a")
    g["w_br_b"] = _unpad_head_rows(_mm(sv["ob"], dyb, ta=True, out_dtype=BF16, name="d_w_br_b"), MLA_V)
    g["w_br_c"] = _unpad_head_rows(_mm(sv["oc"], dyc, ta=True, out_dtype=BF16, name="d_w_br_c"), FOX_HEAD_DIM)
    doa = _mm(dya, w["w_br_a"], tb=True, name="d_oa")
    dob = _mm(dyb, wbb_p, tb=True, out_dtype=BF16, name="d_ob")
    doc = _mm(dyc, wbc_p, tb=True, out_dtype=BF16, name="d_oc")
    dfq, dfk, dfv, d_dec = _fa_bwd(z["fq"], z["fk"], z["fv"], sv["oc"], sv["lse_c"], doc, sv["drow"],
                                   unit=1, dq_dtype=BF16, dk_dtype=BF16, name="fox_attn_bwd")
    d_dec = jnp.pad(d_dec.T, ((0, 0), (0, LANES - HEADS)))
    bf = jnp.pad(w["fox_bf"], (0, LANES - HEADS)).reshape(1, LANES)
    dfl, dbf = _decay_bwd(d_dec, z["fl"], bf)
    g["fox_bf"] = dbf[0, :HEADS]
    dqb, dkb, dvb, _ = _fa_bwd(sv["qb"], sv["kb"], sv["vb"], sv["ob"], sv["lse_b"], dob, None,
                               unit=64, dq_dtype=F32, dk_dtype=F32, name="mla_attn_bwd")
    wuq_p = _split_wuq(w["mla_wuq"])
    wk_p, wv_p = _split_wukv(w["mla_wukv"])
    dq_pre = _rope_q(dqb, c_q, s_lo, s_hi, transpose=True, out_dtype=BF16, name="rope_q_bwd")
    dkr = _rope_k_bwd(dkb, c_k, s_lo, s_hi)
    g["mla_wuq"] = _unpad_heads(_mm(sv["qn"], dq_pre, ta=True, out_dtype=BF16, name="d_wuq"), MLA_NOPE + MLA_ROPE)
    g["mla_wukv"] = _merge_wukv(_mm(sv["kvn"], dkb, ta=True, out_dtype=BF16, name="d_wuk"), _mm(sv["kvn"], dvb, ta=True, out_dtype=BF16, name="d_wuv"))
    dqn = _mm(dq_pre, wuq_p, tb=True, name="d_qn")
    dkvn = _mm(dvb, wv_p, tb=True, res=_mm(dkb, wk_p, tb=True, name="d_kvn_k"), name="d_kvn")
    dcq, g["mla_q_norm"] = _rmsnorm_bwd(z["cq"], w["mla_q_norm"], dqn, out_dtype=BF16, name="q_norm_bwd")
    dckv, g["mla_kv_norm"] = _rmsnorm_bwd(z["ckv"], w["mla_kv_norm"], dkvn, out_dtype=BF16, name="kv_norm_bwd")
    wa_bd, wx_bd = _block_diag(w["lru_wa"]).astype(BF16), _block_diag(w["lru_wx"]).astype(BF16)
    du, dug, dcw, dcb, dba, dbx, dlam, dwa, dwx = _lru_bwd(
        doa, z["u"], z["ug"], sv["xc"], sv["hs"], w["conv_w"], wa_bd, wx_bd, w["lru_ba"], w["lru_bx"], w["lru_lambda"])
    g["conv_w"], g["conv_b"], g["lru_ba"], g["lru_bx"] = dcw, dcb[0], dba[0], dbx[0]
    g["lru_lambda"] = dlam[0] * LRU_C * jax.nn.sigmoid(-w["lru_lambda"])
    g["lru_wa"], g["lru_wx"] = _block_diag_t(dwa), _block_diag_t(dwx)
    dsegs = [du, dug, dcq, dckv, dkr, dfq, dfk, dfv, dfl, dgate]
    dz = jnp.concatenate(dsegs, axis=1)
    w_in_p = jnp.concatenate(_in_segments(w["w_in"]), axis=1)
    g["w_in"] = _in_unsegment(_mm(sv["h"], dz, ta=True, out_dtype=BF16, name="d_w_in"), [d.shape[1] for d in dsegs])
    dx0, g["mix_norm"] = _mm_norm_bwd(dz, w_in_p, sv["x0"], w["mix_norm"], dx1, "mix_norm_bwd")
    return dx0, g


_LAYER_WEIGHTS = ("mix_norm", "w_in", "gate_b", "conv_w", "conv_b", "lru_wa", "lru_ba", "lru_wx", "lru_bx", "lru_lambda",
                  "mla_q_norm", "mla_wuq", "mla_kv_norm", "mla_wukv", "fox_bf", "w_br_a", "w_br_b", "w_br_c", "w_o",
                  "ffn_norm", "w_gate_up", "w_down", "ple_norm", "w_ple_gate", "w_ple")
_BIG = ("w_in", "mla_wuq", "mla_wukv", "w_br_a", "w_br_b", "w_br_c", "w_o", "w_gate_up", "w_down", "w_ple_gate", "w_ple")
_ROW_SHARDED = ("w_o", "w_down", "w_ple_gate")
_SMALL = ("mix_norm", "gate_b", "conv_b", "lru_wa", "lru_ba", "lru_wx", "lru_bx", "lru_lambda", "mla_q_norm", "mla_kv_norm",
          "fox_bf", "ffn_norm", "ple_norm")


def _local_step(x, p, layers, final_norm, target):
    tabs = _rope_tables(x.shape[0])
    saved = []
    h = _rmsnorm_fwd(x, layers[0]["mix_norm"], "mix_norm_fwd")
    for i in range(DEPTH):
        g_next = layers[i + 1]["mix_norm"] if i + 1 < DEPTH else None
        x, h, sv = _layer_fwd(x, h, p[i], layers[i], g_next, tabs)
        saved.append(sv)
    loss, dx, d_final = _loss_head(x, final_norm, target)
    grads = [None] * DEPTH
    for i in reversed(range(DEPTH)):
        dx, grads[i] = _layer_bwd(dx, layers[i], saved[i], tabs)
    return loss, dx, grads, d_final


def _hbm():
    return pl.BlockSpec(memory_space=pltpu.HBM)


def _peers(x, y):
    return [(1 - x, y), (x, 1 - y), (1 - x, 1 - y)]


def _gather_weights(arrs, name):
    n_arr = len(arrs)

    def body(*refs):
        srcs, outs = refs[:n_arr], refs[n_arr:2 * n_arr]
        send_sems, recv_sems = refs[2 * n_arr:]
        x, y, c = lax.axis_index("x"), lax.axis_index("y"), lax.axis_index("c")
        me = 2 * x + y
        peers = _peers(x, y)

        def copy(sem, src, dst, to):
            return pltpu.make_async_remote_copy(src_ref=src, dst_ref=dst, send_sem=send_sems.at[sem], recv_sem=recv_sems.at[sem],
                                                device_id=to, device_id_type=MESH)

        started = []
        for a in range(n_arr):
            for j, (px, py) in enumerate(peers):
                cp = copy(6 * a + j, srcs[a].at[c], outs[a].at[me, c], (px, py, c))
                cp.start()
                started.append(cp)
        for a in range(n_arr):
            for j, (px, py) in enumerate(peers):
                landed = outs[a].at[2 * px + py, c]
                copy(6 * a + j, srcs[a].at[c], landed, (px, py, c)).wait_recv()
                cp = copy(6 * a + 3 + j, landed, landed, (x, y, 1 - c))
                cp.start()
                started.append(cp)
        for a in range(n_arr):
            for j, (px, py) in enumerate(peers):
                copy(6 * a + 3 + j, srcs[a].at[1 - c], outs[a].at[2 * px + py, 1 - c], (x, y, 1 - c)).wait_recv()
        for cp in started:
            cp.wait_send()

    return pl.pallas_call(
        body, name=name, in_specs=[_hbm()] * n_arr, out_specs=tuple([_hbm()] * n_arr),
        out_shape=tuple(jax.ShapeDtypeStruct((4,) + t.shape, t.dtype) for t in arrs),
        scratch_shapes=[pltpu.SemaphoreType.DMA((6 * n_arr,)), pltpu.SemaphoreType.DMA((6 * n_arr,))],
    )(*arrs)


def _pair_swap_halves(g4):
    n, R, W = g4.shape
    Rh = R // 2

    def body(src_ref, out_ref, send_sem, recv_sem):
        x, y, c = lax.axis_index("x"), lax.axis_index("y"), lax.axis_index("c")
        cp = pltpu.make_async_remote_copy(src_ref=src_ref.at[:, pl.ds((1 - c) * Rh, Rh), :], dst_ref=out_ref, send_sem=send_sem,
                                          recv_sem=recv_sem, device_id=(x, y, 1 - c), device_id_type=MESH)
        cp.start()
        cp.wait()

    return pl.pallas_call(
        body, name="grad_pair_swap", in_specs=[_hbm()], out_specs=_hbm(), out_shape=jax.ShapeDtypeStruct((n, Rh, W), g4.dtype),
        scratch_shapes=[pltpu.SemaphoreType.DMA, pltpu.SemaphoreType.DMA],
    )(g4)


def _pair_add(g4, sib, c_arr):
    n, R, W = g4.shape
    Rh = R // 2
    tr = _tile_rows(Rh)
    nb = Rh // tr

    def body(c_ref, a_ref, b_ref, o_ref):
        o_ref[...] = (a_ref[...].astype(F32) + b_ref[...].astype(F32)).astype(o_ref.dtype)

    return pl.pallas_call(
        body, name="grad_pair_add",
        grid_spec=pltpu.PrefetchScalarGridSpec(
            num_scalar_prefetch=1, grid=(n, nb),
            in_specs=[pl.BlockSpec((None, tr, W), lambda s, i, c: (s, c[0] * nb + i, 0)), pl.BlockSpec((None, tr, W), lambda s, i, c: (s, i, 0))],
            out_specs=pl.BlockSpec((None, tr, W), lambda s, i, c: (s, i, 0))),
        out_shape=jax.ShapeDtypeStruct((n, Rh, W), g4.dtype), compiler_params=_cparams(("parallel", "parallel")),
    )(c_arr, g4, sib)


def _tile_rows(n):
    for t in (512, 480, 400, 320, 256, 240, 160, 128, 80, 64, 40, 32, 16, 8):
        if n % t == 0:
            return t
    return n


def _chips_exchange(part):
    n, Rh, W = part.shape

    def body(src_ref, out_ref, send_sems, recv_sems):
        x, y, c = lax.axis_index("x"), lax.axis_index("y"), lax.axis_index("c")

        def copy(j, to):
            return pltpu.make_async_remote_copy(src_ref=src_ref.at[2 * to[0] + to[1]], dst_ref=out_ref.at[j], send_sem=send_sems.at[j],
                                                recv_sem=recv_sems.at[j], device_id=(to[0], to[1], c), device_id_type=MESH)

        cps = [copy(j, peer) for j, peer in enumerate(_peers(x, y))]
        for cp in cps:
            cp.start()
        for cp in cps:
            cp.wait()

    return pl.pallas_call(
        body, name="grad_chips_exchange", in_specs=[_hbm()], out_specs=_hbm(), out_shape=jax.ShapeDtypeStruct((3, Rh, W), part.dtype),
        scratch_shapes=[pltpu.SemaphoreType.DMA((3,)), pltpu.SemaphoreType.DMA((3,))],
    )(part)


def _chips_add(part, got, k_arr, c_arr):
    n, Rh, W = part.shape
    tr = _tile_rows(Rh)
    nb = Rh // tr

    def body(k_ref, c_ref, a_ref, b_ref, o_ref):
        mine = pl.program_id(0) == c_ref[0]

        @pl.when(mine)
        def _():
            o_ref[...] = ((a_ref[...].astype(F32) + b_ref[0].astype(F32)) + b_ref[1].astype(F32)) + b_ref[2].astype(F32)

        @pl.when(jnp.logical_not(mine))
        def _():
            o_ref[...] = jnp.zeros_like(o_ref)

    return pl.pallas_call(
        body, name="grad_chips_add",
        grid_spec=pltpu.PrefetchScalarGridSpec(
            num_scalar_prefetch=2, grid=(2, nb),
            in_specs=[pl.BlockSpec((None, tr, W), lambda h, i, k, c: (k[0], i, 0)), pl.BlockSpec((3, tr, W), lambda h, i, k, c: (0, i, 0))],
            out_specs=pl.BlockSpec((tr, W), lambda h, i, k, c: (h * nb + i, 0))),
        out_shape=jax.ShapeDtypeStruct((2 * Rh, W), F32), compiler_params=_cparams(("parallel", "parallel")),
    )(k_arr, c_arr, part, got)


def _pair_gather(buf):
    R, W = buf.shape
    Rh = R // 2

    def body(src_ref, out_ref, send_sem, recv_sem):
        x, y, c = lax.axis_index("x"), lax.axis_index("y"), lax.axis_index("c")
        mine, other = pl.ds(c * Rh, Rh), pl.ds((1 - c) * Rh, Rh)
        pltpu.make_async_remote_copy(src_ref=src_ref.at[mine], dst_ref=out_ref.at[mine], send_sem=send_sem, recv_sem=recv_sem,
                                     device_id=(x, y, 1 - c), device_id_type=MESH).start()
        pltpu.make_async_remote_copy(src_ref=src_ref.at[mine], dst_ref=out_ref.at[other], send_sem=send_sem, recv_sem=recv_sem,
                                     device_id=(x, y, 1 - c), device_id_type=MESH).wait()

    return pl.pallas_call(
        body, name="grad_pair_gather", in_specs=[_hbm()], out_specs=_hbm(), out_shape=jax.ShapeDtypeStruct((R, W), buf.dtype),
        input_output_aliases={0: 0}, scratch_shapes=[pltpu.SemaphoreType.DMA, pltpu.SemaphoreType.DMA],
    )(buf)


def _gather_all(buf):
    R, W = buf.shape

    def body(src_ref, out_ref, send_sems, recv_sems, local_sem):
        x, y, c = lax.axis_index("x"), lax.axis_index("y"), lax.axis_index("c")
        me = 4 * x + 2 * y + c
        mine = pltpu.make_async_copy(src_ref, out_ref.at[me], local_sem)
        mine.start()
        rel = [((x + (r >> 2 & 1)) % 2, (y + (r >> 1 & 1)) % 2, (c + (r & 1)) % 2) for r in range(1, 8)]

        def copy(j, slot, to):
            return pltpu.make_async_remote_copy(src_ref=src_ref, dst_ref=out_ref.at[slot], send_sem=send_sems.at[j],
                                                recv_sem=recv_sems.at[j], device_id=to, device_id_type=MESH)

        sends = [copy(j, me, to) for j, to in enumerate(rel)]
        for cp in sends:
            cp.start()
        for j, to in enumerate(rel):
            copy(j, 4 * to[0] + 2 * to[1] + to[2], to).wait_recv()
        for cp in sends:
            cp.wait_send()
        mine.wait()

    return pl.pallas_call(
        body, name="small_gather", in_specs=[_hbm()], out_specs=_hbm(), out_shape=jax.ShapeDtypeStruct((8, R, W), buf.dtype),
        scratch_shapes=[pltpu.SemaphoreType.DMA((7,)), pltpu.SemaphoreType.DMA((7,)), pltpu.SemaphoreType.DMA],
    )(buf)


def _sum_slots(stack):
    n, R, W = stack.shape
    tr = _tile_rows(R)

    def body(s_ref, o_ref):
        tot = s_ref[0]
        for j in range(1, n):
            tot = tot + s_ref[j]
        o_ref[...] = tot

    return pl.pallas_call(
        body, name="small_sum", grid=(R // tr,), in_specs=[pl.BlockSpec((n, tr, W), lambda i: (0, i, 0))],
        out_specs=pl.BlockSpec((tr, W), lambda i: (i, 0)), out_shape=jax.ShapeDtypeStruct((R, W), F32),
        compiler_params=_cparams(("parallel",)),
    )(stack)


def _adamw(wp, gp, mp, vp, name):
    R, W = wp.shape
    tr = R
    for t in (1024, 512, 256, 128, 64, 32, 16, 8):
        if R % t == 0 and t * W <= 512 * 1024:
            tr = t
            break
    c1 = 1.0 - ADAM_B1 ** ADAM_STEP
    c2 = 1.0 - ADAM_B2 ** ADAM_STEP

    def body(w_ref, g_ref, m_ref, v_ref, d_ref, mo_ref, vo_ref):
        gv = g_ref[...]
        m = ADAM_B1 * m_ref[...] + (1.0 - ADAM_B1) * gv
        v = ADAM_B2 * v_ref[...] + (1.0 - ADAM_B2) * (gv * gv)
        m_hat = m / c1
        v_hat = v / c2
        d_ref[...] = -ADAM_LR * (m_hat / (jnp.sqrt(v_hat) + ADAM_EPS) + ADAM_WD * w_ref[...])
        mo_ref[...] = m
        vo_ref[...] = v

    blk = pl.BlockSpec((tr, W), lambda i: (i, 0))
    shp = jax.ShapeDtypeStruct((R, W), F32)
    return pl.pallas_call(body, name=name, grid=(R // tr,), in_specs=[blk] * 4, out_specs=(blk,) * 3, out_shape=(shp,) * 3,
                          compiler_params=_cparams(("parallel",)))(wp, gp, mp, vp)


def _pack(arrs, rows):
    flat = jnp.concatenate([a.reshape(-1) for a in arrs])
    return jnp.pad(flat, (0, rows * PACK_W - flat.shape[0])).reshape(rows, PACK_W)


def _unpack(buf, shapes):
    flat = buf.reshape(-1)
    out, off = [], 0
    for shp in shapes:
        n = int(np.prod(shp))
        out.append(flat[off:off + n].reshape(shp))
        off += n
    return out


def _rows_for(shapes, mult):
    n = sum(int(np.prod(s)) for s in shapes)
    rows = -(-n // PACK_W)
    return -(-rows // mult) * mult


def _shard_major(g, name):
    L, K, N = g.shape
    if name in _ROW_SHARDED:
        t = g.reshape(L, 4, K // 4, N).transpose(1, 0, 2, 3)
    else:
        t = g.reshape(L, K, 4, N // 4).transpose(2, 0, 1, 3)
    return t.reshape(4, -1, PACK_W)


def _join_shards(blocks, name):
    return jnp.concatenate(blocks, axis=1 if name in _ROW_SHARDED else 2)


def kernel(x, p, mix_norm, w_in, gate_b, conv_w, conv_b, lru_wa, lru_ba, lru_wx, lru_bx, lru_lambda, mla_q_norm, mla_wuq, mla_kv_norm, mla_wukv, fox_bf, w_br_a, w_br_b, w_br_c, w_o, ffn_norm, w_gate_up, w_down, ple_norm, w_ple_gate, w_ple, final_norm, loss_target, m_mix_norm, m_w_in, m_gate_b, m_conv_w, m_conv_b, m_lru_wa, m_lru_ba, m_lru_wx, m_lru_bx, m_lru_lambda, m_mla_q_norm, m_mla_wuq, m_mla_kv_norm, m_mla_wukv, m_fox_bf, m_w_br_a, m_w_br_b, m_w_br_c, m_w_o, m_ffn_norm, m_w_gate_up, m_w_down, m_ple_norm, m_w_ple_gate, m_w_ple, m_final_norm, v_mix_norm, v_w_in, v_gate_b, v_conv_w, v_conv_b, v_lru_wa, v_lru_ba, v_lru_wx, v_lru_bx, v_lru_lambda, v_mla_q_norm, v_mla_wuq, v_mla_kv_norm, v_mla_wukv, v_fox_bf, v_w_br_a, v_w_br_b, v_w_br_c, v_w_o, v_ffn_norm, v_w_gate_up, v_w_down, v_ple_norm, v_w_ple_gate, v_w_ple, v_final_norm):
    a = dict(locals())
    names = list(_LAYER_WEIGHTS) + ["final_norm"]
    W = {n: a[n] for n in names}
    M = {n: a["m_" + n] for n in names}
    V = {n: a["v_" + n] for n in names}
    ix, iy, ic = lax.axis_index("x"), lax.axis_index("y"), lax.axis_index("c")

    sharded = list(_BIG) + ["conv_w"]
    shard_shapes = [W[n].shape for n in sharded]
    R = _rows_for(shard_shapes, 64)
    mine = [W[n].astype(BF16) for n in _BIG] + [conv_w]
    gathered = _gather_weights(mine, "weight_gather")
    me = 2 * ix + iy
    gathered = [lax.dynamic_update_slice(g, t[None], (me,) + (0,) * t.ndim) for g, t in zip(gathered, mine)]
    full = {n: _join_shards([g[k] for k in range(4)], n) for n, g in zip(sharded, gathered)}
    conv_w_full = full["conv_w"]
    layers = []
    for i in range(DEPTH):
        lw = {n: W[n][i] for n in _SMALL}
        for n in _BIG:
            lw[n] = full[n][i]
        lw["conv_w"] = conv_w_full[i]
        layers.append(lw)

    loss_sum, dx, grads, d_final = _local_step(x[0], p[:, 0], layers, final_norm, loss_target[0])
    loss = lax.psum(loss_sum, ("x", "y", "c"))

    parts = [_shard_major(jnp.stack([grads[i][n] for i in range(DEPTH)]), n).astype(BF16) for n in sharded]
    g4, off = jnp.zeros((4, R, PACK_W), BF16), 0
    for t in parts:
        g4 = lax.dynamic_update_slice(g4, t, (0, off, 0))
        off += t.shape[1]
    c_arr = jnp.reshape(ic, (1,)).astype(jnp.int32)
    k_arr = jnp.reshape(2 * ix + iy, (1,)).astype(jnp.int32)
    pair = _pair_add(g4, _pair_swap_halves(g4), c_arr)
    g_pack = _pair_gather(_chips_add(pair, _chips_exchange(pair), k_arr, c_arr))
    big_out = {}
    for n, gsh in zip(sharded, _unpack(g_pack, shard_shapes)):
        view = lambda t: t.reshape(-1, t.shape[-1])
        d, nm, nv = _adamw(view(W[n]), view(gsh), view(M[n]), view(V[n]), "adamw_" + n)
        for key, arr in (("g", gsh), ("d", d), ("m", nm), ("v", nv)):
            big_out[(key, n)] = arr.reshape(W[n].shape)

    small = list(_SMALL) + ["final_norm"]
    small_shapes = [W[n].shape for n in small]
    Rs = _rows_for(small_shapes, 8)
    mine_small = [d_final if n == "final_norm" else jnp.stack([grads[i][n] for i in range(DEPTH)]) for n in small]
    sg = _sum_slots(_gather_all(_pack(mine_small, Rs)))
    small_out = {}
    for n, gsm in zip(small, _unpack(sg, small_shapes)):
        view = lambda t: t.reshape(-1, t.shape[-1])
        d, nm, nv = _adamw(view(W[n]), view(gsm), view(M[n]), view(V[n]), "adamw_" + n)
        for key, arr in (("g", gsm), ("d", d), ("m", nm), ("v", nv)):
            small_out[(key, n)] = arr.reshape(W[n].shape)

    def assemble(key, n):
        return big_out[(key, n)] if n in sharded else small_out[(key, n)]

    outs = [loss, dx[None]]
    for key in ("g", "d", "m", "v"):
        outs += [assemble(key, n) for n in names]
    return tuple(outs)
```

```python
import math

import numpy as np
import jax
import jax.numpy as jnp
from jax import lax
from jax.experimental import pallas as pl
from jax.experimental.pallas import tpu as pltpu

F32, BF16 = jnp.float32, jnp.bfloat16
MESH = pl.DeviceIdType.MESH

D_MODEL = 1024
DEPTH = 2
EPS = 1e-6
NEG_INF = -1e30
LRU_C = 8.0
CONV_WIDTH = 4
HEADS = 8
MLA_Q_LORA = 384
MLA_KV_LORA = 256
MLA_NOPE = 64
MLA_ROPE = 32
MLA_V = 64
ROPE_BASE = 10000.0
FOX_HEAD_DIM = 64
D_FF = 2816
HEAD_PAD = 128
MLA_SCALE = (MLA_NOPE + MLA_ROPE) ** -0.5
FOX_SCALE = FOX_HEAD_DIM ** -0.5

ADAM_LR, ADAM_B1, ADAM_B2, ADAM_EPS, ADAM_WD, ADAM_STEP = 0.001, 0.9, 0.999, 1e-08, 0.01, 10

VMEM_LIMIT_BYTES = 48 * 1024 * 1024
LANES = 128
PACK_W = 1024

ROW_TILE = 512
MERGE_ROWS = 512
ATTN_TILE = 1024
LRU_CHUNK = 512


def _cparams(dims):
    return pltpu.CompilerParams(dimension_semantics=dims, vmem_limit_bytes=VMEM_LIMIT_BYTES)


def _tile(n, cap):
    if n <= cap:
        return n
    t = (cap // LANES) * LANES
    while t >= LANES:
        if n % t == 0:
            return t
        t -= LANES
    raise ValueError(f"no tile for {n} under {cap}")


def _rows(n):
    return min(ROW_TILE, n)


MM_VMEM_BUDGET = 36 * 1024 * 1024


def _mm_tiles(M, N, K, a_bytes, b_bytes, o_bytes, has_res):
    best, best_work = None, 0
    for tm in {_tile(M, c) for c in (1024, 512, 256)}:
        for tn in {_tile(N, c) for c in (1792, 1024, 512)}:
            for tk in {_tile(K, c) for c in (2048, 1408, 1024, 512)}:
                need = 2 * (tm * tk * a_bytes + tk * tn * b_bytes + tm * tn * o_bytes + (tm * tn * 4 if has_res else 0))
                need += tm * tn * 4 if tk < K else 0
                need += tm * tn * 4
                if need <= MM_VMEM_BUDGET and tm * tn * tk > best_work:
                    best, best_work = (tm, tn, tk), tm * tn * tk
    assert best is not None, (M, N, K)
    return best

def _mm(a, b, *, ta=False, tb=False, out_dtype=F32, res=None, bias=None, name):
    K, M = a.shape if ta else a.shape[::-1]
    N, K2 = b.shape if tb else b.shape[::-1]
    assert K == K2, (name, a.shape, b.shape)
    assert res is None or bias is None
    tm, tn, tk = _mm_tiles(M, N, K, a.dtype.itemsize, b.dtype.itemsize, jnp.dtype(out_dtype).itemsize, res is not None)
    nk = K // tk
    a_spec = pl.BlockSpec((tk, tm), lambda i, j, k: (k, i)) if ta else pl.BlockSpec((tm, tk), lambda i, j, k: (i, k))
    b_spec = pl.BlockSpec((tn, tk), lambda i, j, k: (j, k)) if tb else pl.BlockSpec((tk, tn), lambda i, j, k: (k, j))
    o_spec = pl.BlockSpec((tm, tn), lambda i, j, k: (i, j))
    dn = (((0,) if ta else (1,), (1,) if tb else (0,)), ((), ()))
    if bias is not None:
        res, r_spec = bias, pl.BlockSpec((1, tn), lambda i, j, k: (0, j))
    else:
        r_spec = o_spec
    has_res = res is not None

    def body(*refs):
        a_ref, b_ref = refs[0], refs[1]
        r_ref = refs[2] if has_res else None
        o_ref = refs[3] if has_res else refs[2]
        av, bv = a_ref[...], b_ref[...]
        if av.dtype != BF16:
            av = av.astype(BF16)
        if bv.dtype != BF16:
            bv = bv.astype(BF16)
        part = lax.dot_general(av, bv, dn, preferred_element_type=F32)

        def finish(total):
            if has_res:
                total = total + r_ref[...]
            o_ref[...] = total.astype(out_dtype)

        if nk == 1:
            finish(part)
        else:
            acc = refs[-1]
            k = pl.program_id(2)

            @pl.when(k == 0)
            def _():
                acc[...] = part

            @pl.when(k > 0)
            def _():
                acc[...] += part

            @pl.when(k == nk - 1)
            def _():
                finish(acc[...])

    in_specs = [a_spec, b_spec] + ([r_spec] if has_res else [])
    args = (a, b) + ((res,) if has_res else ())
    return pl.pallas_call(
        body, name=name, grid=(M // tm, N // tn, nk), in_specs=in_specs, out_specs=o_spec,
        out_shape=jax.ShapeDtypeStruct((M, N), out_dtype),
        scratch_shapes=[pltpu.VMEM((tm, tn), F32)] if nk > 1 else [],
        compiler_params=_cparams(("parallel", "parallel", "arbitrary")),
    )(*args)


def _mm_res_norm(a, b, res, g, name):
    M, K = a.shape
    N = b.shape[1]
    tm, tk = _tile(M, 1024), _tile(K, 1408)
    nk = K // tk

    def body(a_ref, b_ref, r_ref, g_ref, o_ref, h_ref, *scratch):
        part = jnp.dot(a_ref[...], b_ref[...], preferred_element_type=F32)

        def finish(total):
            xn = total + r_ref[...]
            o_ref[...] = xn
            rstd = lax.rsqrt(jnp.mean(xn * xn, axis=1, keepdims=True) + EPS)
            h_ref[...] = (xn * rstd * g_ref[...]).astype(BF16)

        if nk == 1:
            finish(part)
        else:
            acc = scratch[0]
            k = pl.program_id(1)

            @pl.when(k == 0)
            def _():
                acc[...] = part

            @pl.when(k > 0)
            def _():
                acc[...] += part

            @pl.when(k == nk - 1)
            def _():
                finish(acc[...])

    row = pl.BlockSpec((tm, N), lambda i, k: (i, 0))
    return pl.pallas_call(
        body, name=name, grid=(M // tm, nk),
        in_specs=[pl.BlockSpec((tm, tk), lambda i, k: (i, k)), pl.BlockSpec((tk, N), lambda i, k: (k, 0)), row,
                  pl.BlockSpec((1, N), lambda i, k: (0, 0))],
        out_specs=(row, row), out_shape=(jax.ShapeDtypeStruct((M, N), F32), jax.ShapeDtypeStruct((M, N), BF16)),
        scratch_shapes=[pltpu.VMEM((tm, N), F32)] if nk > 1 else [],
        compiler_params=_cparams(("parallel", "arbitrary")),
    )(a, b, res, g.reshape(1, N))


def _mm_norm_bwd(a, b, x, g, add, name):
    M, K = a.shape
    N = b.shape[0]
    tm, tk = _tile(M, 1024), _tile(K, 1408)
    nk = K // tk

    def body(a_ref, b_ref, x_ref, g_ref, add_ref, dx_ref, dg_ref, *scratch):
        i, k = pl.program_id(0), pl.program_id(1)
        part = lax.dot_general(a_ref[...], b_ref[...], _NT, preferred_element_type=F32)

        @pl.when(jnp.logical_and(i == 0, k == 0))
        def _():
            dg_ref[...] = jnp.zeros_like(dg_ref)

        def finish(dyv):
            xf = x_ref[...]
            rstd = lax.rsqrt(jnp.mean(xf * xf, axis=1, keepdims=True) + EPS)
            xhat = xf * rstd
            dxh = dyv * g_ref[...]
            dx_ref[...] = rstd * (dxh - xhat * jnp.mean(dxh * xhat, axis=1, keepdims=True)) + add_ref[...]
            dg_ref[...] += jnp.sum(dyv * xhat, axis=0, keepdims=True)

        if nk == 1:
            finish(part)
        else:
            acc = scratch[0]

            @pl.when(k == 0)
            def _():
                acc[...] = part

            @pl.when(k > 0)
            def _():
                acc[...] += part

            @pl.when(k == nk - 1)
            def _():
                finish(acc[...])

    row = pl.BlockSpec((tm, N), lambda i, k: (i, 0))
    vec = pl.BlockSpec((1, N), lambda i, k: (0, 0))
    dx, dg = pl.pallas_call(
        body, name=name, grid=(M // tm, nk),
        in_specs=[pl.BlockSpec((tm, tk), lambda i, k: (i, k)), pl.BlockSpec((N, tk), lambda i, k: (0, k)), row, vec, row],
        out_specs=(row, vec), out_shape=(jax.ShapeDtypeStruct((M, N), F32), jax.ShapeDtypeStruct((1, N), F32)),
        scratch_shapes=[pltpu.VMEM((tm, N), F32)] if nk > 1 else [],
        compiler_params=_cparams(("arbitrary", "arbitrary")),
    )(a, b, x, g.reshape(1, N), add)
    return dx, dg.reshape(N)


FFN_TILE = 1408
FFN_SUBTILES = ((0, 512), (512, 1024), (1024, 1408))
assert D_FF == 2 * FFN_TILE and FFN_SUBTILES[-1][1] == FFN_TILE
FFN_ROWS = 1024


def _ffn_pair_columns(w_gate_up):
    F = w_gate_up.shape[-1] // 2
    parts = []
    for j in range(F // FFN_TILE):
        parts += [w_gate_up[..., j * FFN_TILE:(j + 1) * FFN_TILE], w_gate_up[..., F + j * FFN_TILE:F + (j + 1) * FFN_TILE]]
    return jnp.concatenate(parts, axis=-1)


def _ffn_unpair_columns(dw):
    F = dw.shape[-1] // 2
    n = F // FFN_TILE
    blk = [dw[..., j * FFN_TILE:(j + 1) * FFN_TILE] for j in range(2 * n)]
    return jnp.concatenate(blk[0::2] + blk[1::2], axis=-1)


def _ffn_up(hn, w_pair):
    S, D = hn.shape
    W2 = w_pair.shape[1]
    F, tf = W2 // 2, FFN_TILE
    tm = min(FFN_ROWS, S)

    def body(h_ref, w_ref, hf_ref, act_ref):
        hv = h_ref[...]
        for lo, hi in FFN_SUBTILES:
            gt = jnp.dot(hv, w_ref[:, lo:hi], preferred_element_type=F32)
            up = jnp.dot(hv, w_ref[:, tf + lo:tf + hi], preferred_element_type=F32)
            hf_ref[:, lo:hi] = gt.astype(BF16)
            hf_ref[:, tf + lo:tf + hi] = up.astype(BF16)
            act_ref[:, lo:hi] = (gt * jax.nn.sigmoid(gt) * up).astype(BF16)

    return pl.pallas_call(
        body, name="ffn_up", grid=(S // tm, F // tf),
        in_specs=[pl.BlockSpec((tm, D), lambda i, j: (i, 0)), pl.BlockSpec((D, 2 * tf), lambda i, j: (0, j))],
        out_specs=(pl.BlockSpec((tm, 2 * tf), lambda i, j: (i, j)), pl.BlockSpec((tm, tf), lambda i, j: (i, j))),
        out_shape=(jax.ShapeDtypeStruct((S, W2), BF16), jax.ShapeDtypeStruct((S, F), BF16)),
        compiler_params=_cparams(("parallel", "parallel")),
    )(hn, w_pair)


def _ffn_down_bwd(dx, w_down, hf):
    S, D = dx.shape
    F, tf = w_down.shape[0], FFN_TILE
    tm = min(FFN_ROWS, S)

    def body(d_ref, w_ref, h_ref, o_ref):
        dv = d_ref[...].astype(BF16)
        for lo, hi in FFN_SUBTILES:
            dact = lax.dot_general(dv, w_ref[lo:hi, :], _NT, preferred_element_type=F32)
            gt, up = h_ref[:, lo:hi].astype(F32), h_ref[:, tf + lo:tf + hi].astype(F32)
            sg = jax.nn.sigmoid(gt)
            o_ref[:, lo:hi] = (dact * up * sg * (1.0 + gt * (1.0 - sg))).astype(BF16)
            o_ref[:, tf + lo:tf + hi] = (dact * gt * sg).astype(BF16)

    pair = pl.BlockSpec((tm, 2 * tf), lambda i, j: (i, j))
    return pl.pallas_call(
        body, name="ffn_down_bwd", grid=(S // tm, F // tf),
        in_specs=[pl.BlockSpec((tm, D), lambda i, j: (i, 0)), pl.BlockSpec((tf, D), lambda i, j: (j, 0)), pair],
        out_specs=pair, out_shape=jax.ShapeDtypeStruct((S, 2 * F), BF16),
        compiler_params=_cparams(("parallel", "parallel")),
    )(dx, w_down, hf)


def _rmsnorm_fwd(x, g, name):
    S, W = x.shape
    tm = _rows(S)

    def body(x_ref, g_ref, o_ref):
        xf = x_ref[...]
        rstd = lax.rsqrt(jnp.mean(xf * xf, axis=1, keepdims=True) + EPS)
        o_ref[...] = (xf * rstd * g_ref[...]).astype(BF16)

    return pl.pallas_call(
        body, name=name, grid=(S // tm,),
        in_specs=[pl.BlockSpec((tm, W), lambda i: (i, 0)), pl.BlockSpec((1, W), lambda i: (0, 0))],
        out_specs=pl.BlockSpec((tm, W), lambda i: (i, 0)),
        out_shape=jax.ShapeDtypeStruct((S, W), BF16), compiler_params=_cparams(("parallel",)),
    )(x, g.reshape(1, W))


def _rmsnorm_bwd(x, g, dy, *, add=None, out_dtype=F32, name):
    S, W = x.shape
    tm = _rows(S)
    has_add = add is not None

    def body(*refs):
        x_ref, g_ref, dy_ref = refs[:3]
        add_ref = refs[3] if has_add else None
        dx_ref, dg_ref = refs[-2], refs[-1]
        xf = x_ref[...]
        rstd = lax.rsqrt(jnp.mean(xf * xf, axis=1, keepdims=True) + EPS)
        xhat = xf * rstd
        dyv = dy_ref[...]
        dxh = dyv * g_ref[...]
        dx = rstd * (dxh - xhat * jnp.mean(dxh * xhat, axis=1, keepdims=True))
        if has_add:
            dx = dx + add_ref[...]
        dx_ref[...] = dx.astype(out_dtype)

        @pl.when(pl.program_id(0) == 0)
        def _():
            dg_ref[...] = jnp.zeros_like(dg_ref)

        dg_ref[...] += jnp.sum(dyv * xhat, axis=0, keepdims=True)

    row = pl.BlockSpec((tm, W), lambda i: (i, 0))
    vec = pl.BlockSpec((1, W), lambda i: (0, 0))
    dx, dg = pl.pallas_call(
        body, name=name, grid=(S // tm,),
        in_specs=[row, vec, row] + ([row] if has_add else []),
        out_specs=(row, vec),
        out_shape=(jax.ShapeDtypeStruct((S, W), out_dtype), jax.ShapeDtypeStruct((1, W), F32)),
        compiler_params=_cparams(("arbitrary",)),
    )(x, g.reshape(1, W), dy, *((add,) if has_add else ()))
    return dx, dg.reshape(W)


def _loss_head(x, g, target):
    S, W = x.shape
    tm = _rows(S)

    def body(x_ref, g_ref, t_ref, loss_ref, dx_ref, dg_ref):
        xf = x_ref[...]
        gv = g_ref[...]
        rstd = lax.rsqrt(jnp.mean(xf * xf, axis=1, keepdims=True) + EPS)
        xhat = xf * rstd
        err = xhat * gv - t_ref[...]
        part = 0.5 * jnp.sum(jnp.mean(err * err, axis=1, keepdims=True), axis=0, keepdims=True)
        dyv = err * (1.0 / W)
        dxh = dyv * gv
        dx_ref[...] = rstd * (dxh - xhat * jnp.mean(dxh * xhat, axis=1, keepdims=True))

        @pl.when(pl.program_id(0) == 0)
        def _():
            dg_ref[...] = jnp.zeros_like(dg_ref)
            loss_ref[...] = jnp.zeros_like(loss_ref)

        dg_ref[...] += jnp.sum(dyv * xhat, axis=0, keepdims=True)
        loss_ref[...] += part

    row = pl.BlockSpec((tm, W), lambda i: (i, 0))
    vec = pl.BlockSpec((1, W), lambda i: (0, 0))
    loss, dx, dg = pl.pallas_call(
        body, name="loss_head", grid=(S // tm,), in_specs=[row, vec, row],
        out_specs=(pl.BlockSpec((1, 1), lambda i: (0, 0)), row, vec),
        out_shape=(jax.ShapeDtypeStruct((1, 1), F32), jax.ShapeDtypeStruct((S, W), F32), jax.ShapeDtypeStruct((1, W), F32)),
        compiler_params=_cparams(("arbitrary",)),
    )(x, g.reshape(1, W), target)
    return loss[0, 0], dx, dg.reshape(W)


def _scan_fwd(a, b, row):
    T = a.shape[0]
    d = 1
    while d < T:
        keep = row >= d
        b = jnp.where(keep, a * pltpu.roll(b, d, axis=0) + b, b)
        a = jnp.where(keep, a * pltpu.roll(a, d, axis=0), a)
        d *= 2
    return a, b


def _scan_bwd(a, b, row):
    T = a.shape[0]
    d = 1
    while d < T:
        keep = row < T - d
        b = jnp.where(keep, a * pltpu.roll(b, T - d, axis=0) + b, b)
        a = jnp.where(keep, a * pltpu.roll(a, T - d, axis=0), a)
        d *= 2
    return a, b


def _expm1(x):
    small = x * (1.0 + x * (0.5 + x * (1.0 / 6 + x * (1.0 / 24 + x * (1.0 / 120 + x * (1.0 / 720 + x * (1.0 / 5040)))))))
    return jnp.where(jnp.abs(x) < 0.25, small, jnp.exp(x) - 1.0)


_GELU_C = math.sqrt(2.0 / math.pi)


def _gelu_and_grad(x):
    inner = _GELU_C * (x + 0.044715 * x * x * x)
    th = jnp.tanh(inner)
    val = 0.5 * x * (1.0 + th)
    grad = 0.5 * (1.0 + th) + 0.5 * x * (1.0 - th * th) * _GELU_C * (1.0 + 3 * 0.044715 * x * x)
    return val, grad


def _lru_gates(xc, wa, wx, ba, bx, lam):
    xcb = xc.astype(BF16)
    r = jax.nn.sigmoid(jnp.dot(xcb, wa, preferred_element_type=F32) + ba)
    ig = jax.nn.sigmoid(jnp.dot(xcb, wx, preferred_element_type=F32) + bx)
    sp = jax.nn.softplus(-lam)
    log_a = -LRU_C * r * sp
    a = jnp.exp(log_a)
    mult = jnp.sqrt(-_expm1(2.0 * log_a))
    return xcb, r, ig, sp, a, mult


def _lru_fwd(u, ug, conv_w, conv_b, wa_bd, wx_bd, ba, bx, lam):
    S, W = u.shape
    T = min(LRU_CHUNK, S)
    nl, nc = W // LANES, S // T

    def body(u_ref, ug_ref, cw_ref, cb_ref, wa_ref, wx_ref, ba_ref, bx_ref, lam_ref, ya_ref, xc_ref, h_ref, prev_u, h_carry):
        c = pl.program_id(1)

        @pl.when(c == 0)
        def _():
            prev_u[...] = jnp.zeros_like(prev_u)
            h_carry[...] = jnp.zeros_like(h_carry)

        uv = u_ref[...]
        row = lax.broadcasted_iota(jnp.int32, (T, LANES), 0)
        row8 = lax.broadcasted_iota(jnp.int32, (8, LANES), 0)
        cw = cw_ref[...]
        xc = cb_ref[...] + uv * cw[3:4, :]
        pv = prev_u[...]
        for k in range(1, CONV_WIDTH):
            us = pltpu.roll(uv, k, axis=0)
            top = jnp.where(row8 < k, pltpu.roll(pv, k, axis=0), us[0:8])
            us = jnp.concatenate([top, us[8:]], axis=0)
            xc = xc + us * cw[3 - k:4 - k, :]
        prev_u[...] = uv[T - 8:T]
        _, r, ig, sp, a, mult = _lru_gates(xc, wa_ref[...], wx_ref[...], ba_ref[...], bx_ref[...], lam_ref[...])
        bb = mult * (ig * xc)
        aa, hh = _scan_fwd(a, bb, row)
        h = hh + aa * h_carry[7:8, :]
        h_carry[...] = h[T - 8:T]
        gl, _ = _gelu_and_grad(ug_ref[...])
        ya_ref[...] = (h * gl).astype(BF16)
        xc_ref[...] = xc
        h_ref[...] = h

    seq = pl.BlockSpec((T, LANES), lambda l, c: (c, l))
    vec = pl.BlockSpec((1, LANES), lambda l, c: (0, l))
    mat = pl.BlockSpec((None, LANES, LANES), lambda l, c: (l, 0, 0))
    return pl.pallas_call(
        body, name="lru_fwd", grid=(nl, nc),
        in_specs=[seq, seq, pl.BlockSpec((CONV_WIDTH, LANES), lambda l, c: (0, l)), vec, mat, mat, vec, vec, vec],
        out_specs=(seq, seq, seq),
        out_shape=(jax.ShapeDtypeStruct((S, W), BF16), jax.ShapeDtypeStruct((S, W), F32), jax.ShapeDtypeStruct((S, W), F32)),
        scratch_shapes=[pltpu.VMEM((8, LANES), F32), pltpu.VMEM((8, LANES), F32)],
        compiler_params=_cparams(("parallel", "arbitrary")),
    )(u, ug, conv_w, conv_b.reshape(1, W), wa_bd, wx_bd, ba.reshape(1, W), bx.reshape(1, W), lam.reshape(1, W))


def _lru_bwd(dya, u, ug, xc, h, conv_w, wa_bd, wx_bd, ba, bx, lam):
    S, W = u.shape
    T = min(LRU_CHUNK, S)
    nl, nc = W // LANES, S // T
    tb8 = T // 8

    def body(dya_ref, u_ref, ug_ref, xc_ref, h_ref, hp_ref, cw_ref, wa_ref, wx_ref, ba_ref, bx_ref, lam_ref,
             du_ref, dug_ref, dcw_ref, dcb_ref, dba_ref, dbx_ref, dlam_ref, dwa_ref, dwx_ref,
             g_next, a_next, dxc_next):
        c = pl.program_id(1)

        @pl.when(c == 0)
        def _():
            g_next[...] = jnp.zeros_like(g_next)
            a_next[...] = jnp.zeros_like(a_next)
            dxc_next[...] = jnp.zeros_like(dxc_next)
            for ref in (dcw_ref, dcb_ref, dba_ref, dbx_ref, dlam_ref, dwa_ref, dwx_ref):
                ref[...] = jnp.zeros_like(ref)

        row = lax.broadcasted_iota(jnp.int32, (T, LANES), 0)
        row8 = lax.broadcasted_iota(jnp.int32, (8, LANES), 0)
        xcv = xc_ref[...]
        wa, wx = wa_ref[...], wx_ref[...]
        xcb, r, ig, sp, a, mult = _lru_gates(xcv, wa, wx, ba_ref[...], bx_ref[...], lam_ref[...])
        gl, dgl = _gelu_and_grad(ug_ref[...])
        dyav = dya_ref[...]
        hv = h_ref[...]
        dug_ref[...] = (dyav * hv * dgl).astype(BF16)
        dh = dyav * gl
        a_up = pltpu.roll(a, T - 1, axis=0)
        a_up = jnp.where(row == T - 1, a_next[0:1, :], a_up)
        prod, gg = _scan_bwd(a_up, dh, row)
        g = gg + prod * g_next[0:1, :]
        h_prev = pltpu.roll(hv, 1, axis=0)
        first_chunk = c == nc - 1
        h_before = jnp.where(first_chunk, 0.0, hp_ref[7:8, :])
        h_prev = jnp.where(row == 0, h_before, h_prev)
        da = g * h_prev
        d_mult = g * (ig * xcv)
        d_ig = g * mult * xcv
        dxc = g * mult * ig
        d_log_a = da * a - d_mult * (a * a) / mult
        d_r = d_log_a * (-LRU_C * sp)
        d_pa = d_r * r * (1.0 - r)
        d_px = d_ig * ig * (1.0 - ig)
        d_pab, d_pxb = d_pa.astype(BF16), d_px.astype(BF16)
        nt = (((1,), (1,)), ((), ()))
        tn = (((0,), (0,)), ((), ()))
        dxc = dxc + lax.dot_general(d_pab, wa, nt, preferred_element_type=F32) + lax.dot_general(d_pxb, wx, nt, preferred_element_type=F32)
        dwa_ref[...] += lax.dot_general(xcb, d_pab, tn, preferred_element_type=F32)
        dwx_ref[...] += lax.dot_general(xcb, d_pxb, tn, preferred_element_type=F32)
        dlam_ref[...] += jnp.sum(d_log_a * r, axis=0, keepdims=True)
        dba_ref[...] += jnp.sum(d_pa, axis=0, keepdims=True)
        dbx_ref[...] += jnp.sum(d_px, axis=0, keepdims=True)
        dcb_ref[...] += jnp.sum(dxc, axis=0, keepdims=True)
        uv = u_ref[...]
        cw = cw_ref[...]
        nxt = dxc_next[...]
        du = dxc * cw[3:4, :]
        dcw_ref[3:4, :] += jnp.sum(uv * dxc, axis=0, keepdims=True)
        for k in range(1, CONV_WIDTH):
            ds = pltpu.roll(dxc, T - k, axis=0)
            bot = jnp.where(row8 >= 8 - k, pltpu.roll(nxt, 8 - k, axis=0), ds[T - 8:T])
            ds = jnp.concatenate([ds[:T - 8], bot], axis=0)
            du = du + ds * cw[3 - k:4 - k, :]
            dcw_ref[3 - k:4 - k, :] += jnp.sum(uv * ds, axis=0, keepdims=True)
        du_ref[...] = du.astype(BF16)
        g_next[...] = g[0:8]
        a_next[...] = a[0:8]
        dxc_next[...] = dxc[0:8]

    seq = pl.BlockSpec((T, LANES), lambda l, c: (nc - 1 - c, l))
    before = pl.BlockSpec((8, LANES), lambda l, c: (jnp.maximum((nc - 1 - c) * tb8 - 1, 0), l))
    vec = pl.BlockSpec((1, LANES), lambda l, c: (0, l))
    cwb = pl.BlockSpec((CONV_WIDTH, LANES), lambda l, c: (0, l))
    mat = pl.BlockSpec((None, LANES, LANES), lambda l, c: (l, 0, 0))
    vshape = jax.ShapeDtypeStruct((1, W), F32)
    mshape = jax.ShapeDtypeStruct((nl, LANES, LANES), F32)
    return pl.pallas_call(
        body, name="lru_bwd", grid=(nl, nc),
        in_specs=[seq, seq, seq, seq, seq, before, cwb, mat, mat, vec, vec, vec],
        out_specs=(seq, seq, cwb, vec, vec, vec, vec, mat, mat),
        out_shape=(jax.ShapeDtypeStruct((S, W), BF16), jax.ShapeDtypeStruct((S, W), BF16),
                   jax.ShapeDtypeStruct((CONV_WIDTH, W), F32), vshape, vshape, vshape, vshape, mshape, mshape),
        scratch_shapes=[pltpu.VMEM((8, LANES), F32)] * 3,
        compiler_params=_cparams(("parallel", "arbitrary")),
    )(dya, u, ug, xc, h, h, conv_w, wa_bd, wx_bd, ba.reshape(1, W), bx.reshape(1, W), lam.reshape(1, W))


def _decay_fwd(f_logit, bf):
    S = f_logit.shape[0]
    T = min(LRU_CHUNK, S)

    def body(f_ref, b_ref, o_ref, carry):
        @pl.when(pl.program_id(0) == 0)
        def _():
            carry[...] = jnp.zeros_like(carry)

        row = lax.broadcasted_iota(jnp.int32, (T, LANES), 0)
        v = jax.nn.log_sigmoid(f_ref[...] + b_ref[...])
        d = 1
        while d < T:
            v = jnp.where(row >= d, v + pltpu.roll(v, d, axis=0), v)
            d *= 2
        v = v + carry[7:8, :]
        carry[...] = v[T - 8:T]
        o_ref[...] = v

    return pl.pallas_call(
        body, name="decay_fwd", grid=(S // T,),
        in_specs=[pl.BlockSpec((T, LANES), lambda c: (c, 0)), pl.BlockSpec((1, LANES), lambda c: (0, 0))],
        out_specs=pl.BlockSpec((T, LANES), lambda c: (c, 0)),
        out_shape=jax.ShapeDtypeStruct((S, LANES), F32), scratch_shapes=[pltpu.VMEM((8, LANES), F32)],
        compiler_params=_cparams(("arbitrary",)),
    )(f_logit, bf)


def _decay_bwd(d_dec, f_logit, bf):
    S = f_logit.shape[0]
    T = min(LRU_CHUNK, S)
    nc = S // T

    def body(dd_ref, f_ref, b_ref, df_ref, db_ref, carry):
        @pl.when(pl.program_id(0) == 0)
        def _():
            carry[...] = jnp.zeros_like(carry)
            db_ref[...] = jnp.zeros_like(db_ref)

        row = lax.broadcasted_iota(jnp.int32, (T, LANES), 0)
        v = dd_ref[...]
        d = 1
        while d < T:
            v = jnp.where(row < T - d, v + pltpu.roll(v, T - d, axis=0), v)
            d *= 2
        v = v + carry[0:1, :]
        carry[...] = v[0:8]
        df = v * jax.nn.sigmoid(-(f_ref[...] + b_ref[...]))
        df_ref[...] = df.astype(BF16)
        db_ref[...] += jnp.sum(df, axis=0, keepdims=True)

    seq = pl.BlockSpec((T, LANES), lambda c: (nc - 1 - c, 0))
    vec = pl.BlockSpec((1, LANES), lambda c: (0, 0))
    return pl.pallas_call(
        body, name="decay_bwd", grid=(nc,), in_specs=[seq, seq, vec], out_specs=(seq, vec),
        out_shape=(jax.ShapeDtypeStruct((S, LANES), BF16), jax.ShapeDtypeStruct((1, LANES), F32)),
        scratch_shapes=[pltpu.VMEM((8, LANES), F32)], compiler_params=_cparams(("arbitrary",)),
    )(d_dec, f_logit, bf)


def _rope_tables(S):
    pos = jnp.arange(S, dtype=F32)
    inv_freq = ROPE_BASE ** (-jnp.arange(0, MLA_ROPE, 2, dtype=F32) / MLA_ROPE)
    ang = pos[:, None] * inv_freq[None, :]
    cos, sin = jnp.cos(ang), jnp.sin(ang)
    half = MLA_ROPE // 2
    z = lambda n: jnp.zeros((S, n), F32)
    c_q = jnp.concatenate([jnp.ones((S, MLA_NOPE), F32), cos, cos, z(HEAD_PAD - MLA_NOPE - MLA_ROPE)], axis=1)
    c_k = jnp.concatenate([z(MLA_NOPE), cos, cos, z(HEAD_PAD - MLA_NOPE - MLA_ROPE)], axis=1)
    s_lo = jnp.concatenate([z(MLA_NOPE), -sin, z(HEAD_PAD - MLA_NOPE - half)], axis=1)
    s_hi = jnp.concatenate([z(MLA_NOPE + half), sin, z(HEAD_PAD - MLA_NOPE - MLA_ROPE)], axis=1)
    return c_q, c_k, s_lo, s_hi


def _rot(v, c, s_lo, s_hi):
    half = MLA_ROPE // 2
    return v * c + pltpu.roll(v, LANES - half, axis=1) * s_lo + pltpu.roll(v, half, axis=1) * s_hi


def _rot_t(dv, c, s_lo, s_hi):
    half = MLA_ROPE // 2
    return dv * c + pltpu.roll(dv * s_lo, half, axis=1) + pltpu.roll(dv * s_hi, LANES - half, axis=1)


def _rope_q(q_pre, c_q, s_lo, s_hi, *, transpose, out_dtype, name):
    S, W = q_pre.shape
    tm = _rows(S)
    fn = _rot_t if transpose else _rot

    def body(q_ref, c_ref, lo_ref, hi_ref, o_ref):
        c, lo, hi = c_ref[...], lo_ref[...], hi_ref[...]
        for hd in range(W // LANES):
            cols = slice(hd * LANES, (hd + 1) * LANES)
            o_ref[:, cols] = fn(q_ref[:, cols] * MLA_SCALE, c, lo, hi).astype(out_dtype)

    blk = pl.BlockSpec((tm, W), lambda i: (i, 0))
    tab = pl.BlockSpec((tm, LANES), lambda i: (i, 0))
    return pl.pallas_call(
        body, name=name, grid=(S // tm,), in_specs=[blk, tab, tab, tab], out_specs=blk,
        out_shape=jax.ShapeDtypeStruct((S, W), out_dtype), compiler_params=_cparams(("parallel",)),
    )(q_pre, c_q, s_lo, s_hi)


def _rope_k(k_pre, k_rope, c_k, s_lo, s_hi):
    S, W = k_pre.shape
    tm = _rows(S)

    def body(k_ref, r_ref, c_ref, lo_ref, hi_ref, o_ref):
        rot = _rot(r_ref[...], c_ref[...], lo_ref[...], hi_ref[...])
        for hd in range(W // LANES):
            cols = slice(hd * LANES, (hd + 1) * LANES)
            o_ref[:, cols] = (k_ref[:, cols] + rot).astype(BF16)

    blk = pl.BlockSpec((tm, W), lambda i: (i, 0))
    tab = pl.BlockSpec((tm, LANES), lambda i: (i, 0))
    return pl.pallas_call(
        body, name="rope_k", grid=(S // tm,), in_specs=[blk, tab, tab, tab, tab], out_specs=blk,
        out_shape=jax.ShapeDtypeStruct((S, W), BF16), compiler_params=_cparams(("parallel",)),
    )(k_pre, k_rope, c_k, s_lo, s_hi)


def _rope_k_bwd(dk, c_k, s_lo, s_hi):
    S, W = dk.shape
    tm = _rows(S)

    def body(dk_ref, c_ref, lo_ref, hi_ref, o_ref):
        tot = dk_ref[:, 0:LANES].astype(F32)
        for hd in range(1, W // LANES):
            tot = tot + dk_ref[:, hd * LANES:(hd + 1) * LANES].astype(F32)
        o_ref[...] = _rot_t(tot, c_ref[...], lo_ref[...], hi_ref[...]).astype(BF16)

    tab = pl.BlockSpec((tm, LANES), lambda i: (i, 0))
    return pl.pallas_call(
        body, name="rope_k_bwd", grid=(S // tm,), in_specs=[pl.BlockSpec((tm, W), lambda i: (i, 0)), tab, tab, tab],
        out_specs=tab, out_shape=jax.ShapeDtypeStruct((S, LANES), BF16), compiler_params=_cparams(("parallel",)),
    )(dk, c_k, s_lo, s_hi)


def _pairs(n):
    pr = [(i, j) for i in range(n) for j in range(i + 1)]
    return (jnp.asarray(np.array([p[0] for p in pr], np.int32)), jnp.asarray(np.array([p[1] for p in pr], np.int32)), len(pr))


def _unit_mask(shape, unit, q_off=0):
    q = lax.broadcasted_iota(jnp.int32, shape, 0) + q_off
    k = lax.broadcasted_iota(jnp.int32, shape, 1)
    if unit > 1:
        q, k = q // unit, k // unit
    return q >= k


_NT = (((1,), (1,)), ((), ()))


ONES_LANE = 64
ROW_SPLIT = 1
ROW_SPLIT_BWD = 4


def _ones_lane_bias():
    one = np.zeros((HEADS, HEAD_PAD), np.float32)
    one[:, ONES_LANE] = 1.0
    return jnp.asarray(one.reshape(1, HEADS * HEAD_PAD))


def _lane_sum(t):
    tot = t[:, 0:LANES]
    for c in range(1, t.shape[1] // LANES):
        tot = tot + t[:, c * LANES:(c + 1) * LANES]
    return tot


def _fa_fwd(q, k, v, dec_row, *, unit, name):
    S, W = q.shape
    H = W // LANES
    T = min(ATTN_TILE, S)
    n, hT = S // T, T // ROW_SPLIT
    qi, kj, npairs = _pairs(n)
    has_dec = dec_row is not None

    def body(qi_ref, kj_ref, *refs):
        if has_dec:
            q_ref, k_ref, v_ref, dr_ref, o_ref, lse_ref, m_s, acc = refs
        else:
            q_ref, k_ref, v_ref, o_ref, lse_ref, m_s, acc = refs
        t = pl.program_id(1)
        i, j = qi_ref[t], kj_ref[t]

        @pl.when(j == 0)
        def _():
            m_s[...] = jnp.full_like(m_s, NEG_INF)
            acc[...] = jnp.zeros_like(acc)

        def step(diag):
            for r in range(ROW_SPLIT):
                rows = slice(r * hT, (r + 1) * hT)
                nk = (r + 1) * hT if diag else T
                s = lax.dot_general(q_ref[rows, :], k_ref[0:nk, :], _NT, preferred_element_type=F32)
                if has_dec:
                    s = s - dr_ref[:, 0:nk]
                if diag:
                    s = jnp.where(_unit_mask((hT, nk), unit, r * hT), s, NEG_INF)
                m_prev = m_s[rows, :]
                m_new = jnp.maximum(m_prev, jnp.max(s, axis=1, keepdims=True))
                alpha = jnp.exp(m_prev - m_new)
                p = jnp.exp(s - jnp.tile(m_new, (1, nk // LANES)))
                acc[rows, :] = alpha * acc[rows, :] + jnp.dot(p.astype(BF16), v_ref[0:nk, :], preferred_element_type=F32)
                m_s[rows, :] = m_new

        @pl.when(j < i)
        def _():
            step(False)

        @pl.when(j == i)
        def _():
            step(True)
            av = acc[...]
            l = av[:, ONES_LANE:ONES_LANE + 1]
            lane = lax.broadcasted_iota(jnp.int32, (T, LANES), 1)
            o_ref[...] = jnp.where(lane < ONES_LANE, av / l, 0.0).astype(BF16)
            lse_ref[...] = m_s[...] + jnp.log(l)

    qb = pl.BlockSpec((T, LANES), lambda h, t, qi, kj: (qi[t], h))
    kb = pl.BlockSpec((T, LANES), lambda h, t, qi, kj: (kj[t], h))
    repq = pl.BlockSpec((None, T, LANES), lambda h, t, qi, kj: (h, qi[t], 0))
    rowk = pl.BlockSpec((None, 1, T), lambda h, t, qi, kj: (h, 0, kj[t]))
    in_specs = [qb, kb, kb] + ([rowk] if has_dec else [])
    args = (q, k, v) + ((dec_row,) if has_dec else ())
    return pl.pallas_call(
        body, name=name,
        grid_spec=pltpu.PrefetchScalarGridSpec(
            num_scalar_prefetch=2, grid=(H, npairs), in_specs=in_specs, out_specs=(qb, repq),
            scratch_shapes=[pltpu.VMEM((T, LANES), F32), pltpu.VMEM((T, LANES), F32)]),
        out_shape=(jax.ShapeDtypeStruct((S, W), BF16), jax.ShapeDtypeStruct((H, S, LANES), F32)),
        compiler_params=_cparams(("parallel", "arbitrary")),
    )(qi, kj, *args)


_TN = (((0,), (0,)), ((), ()))


def _fa_bwd_fused(q, k, v, do, o, lse, dec_row, *, unit, dq_dtype, dk_dtype, name):
    S, W = q.shape
    H = W // LANES
    T = min(ATTN_TILE, S)
    n, hT = S // T, T // ROW_SPLIT_BWD
    qi, kj, npairs = _pairs(n)
    has_dec = dec_row is not None

    def body(qi_ref, kj_ref, *refs):
        if has_dec:
            (q_ref, k_ref, v_ref, do_ref, o_ref, lse_ref, dr_ref, dq_ref, dk_ref, dv_ref, ddq_ref, ddk_ref,
             qacc, kacc, vacc, dl_ref, rsum, csum) = refs
        else:
            q_ref, k_ref, v_ref, do_ref, o_ref, lse_ref, dq_ref, dk_ref, dv_ref, qacc, kacc, vacc, dl_ref = refs
        t = pl.program_id(1)
        i, j = qi_ref[t], kj_ref[t]

        @pl.when(t == 0)
        def _():
            kacc[...] = jnp.zeros_like(kacc)
            vacc[...] = jnp.zeros_like(vacc)
            if has_dec:
                csum[...] = jnp.zeros_like(csum)

        @pl.when(j == 0)
        def _():
            qacc[...] = jnp.zeros_like(qacc)
            delta = jnp.sum(do_ref[...].astype(F32) * o_ref[...].astype(F32), axis=1, keepdims=True)
            dl_ref[...] = jnp.broadcast_to(delta, (T, LANES))
            if has_dec:
                rsum[...] = jnp.zeros_like(rsum)

        def step(diag):
            for r in range(ROW_SPLIT_BWD):
                rows = slice(r * hT, (r + 1) * hT)
                nk = (r + 1) * hT if diag else T
                qv, dov, kv = q_ref[rows, :], do_ref[rows, :], k_ref[0:nk, :]
                s = lax.dot_general(qv, kv, _NT, preferred_element_type=F32)
                if has_dec:
                    s = s - dr_ref[:, 0:nk]
                if diag:
                    s = jnp.where(_unit_mask((hT, nk), unit, r * hT), s, NEG_INF)
                p = jnp.exp(s - jnp.tile(lse_ref[rows, :], (1, nk // LANES)))
                dp = lax.dot_general(dov, v_ref[0:nk, :], _NT, preferred_element_type=F32)
                ds = p * (dp - jnp.tile(dl_ref[rows, :], (1, nk // LANES)))
                pb, dsb = p.astype(BF16), ds.astype(BF16)
                qacc[rows, :] += jnp.dot(dsb, kv, preferred_element_type=F32)
                vacc[j, 0:nk, :] += lax.dot_general(pb, dov, _TN, preferred_element_type=F32)
                kacc[j, 0:nk, :] += lax.dot_general(dsb, qv, _TN, preferred_element_type=F32)
                if has_dec:
                    rsum[rows, :] += _lane_sum(ds)
                    csum[j, :, 0:nk] -= jnp.sum(ds, axis=0, keepdims=True)

        @pl.when(j < i)
        def _():
            step(False)

        @pl.when(j == i)
        def _():
            step(True)
            dq_ref[...] = qacc[...].astype(dq_dtype)
            if has_dec:
                ddq_ref[...] = jnp.broadcast_to(jnp.sum(rsum[...], axis=1, keepdims=True), (T, LANES))

        @pl.when(t == npairs - 1)
        def _():
            for jj in range(n):
                dk_ref[jj * T:(jj + 1) * T, :] = kacc[jj].astype(dk_dtype)
                dv_ref[jj * T:(jj + 1) * T, :] = vacc[jj].astype(BF16)
                if has_dec:
                    ddk_ref[:, jj * T:(jj + 1) * T] = csum[jj]

    qb = pl.BlockSpec((T, LANES), lambda h, t, qi, kj: (qi[t], h))
    kb = pl.BlockSpec((T, LANES), lambda h, t, qi, kj: (kj[t], h))
    head = pl.BlockSpec((S, LANES), lambda h, t, qi, kj: (0, h))
    repq = pl.BlockSpec((None, T, LANES), lambda h, t, qi, kj: (h, qi[t], 0))
    rowk = pl.BlockSpec((None, 1, T), lambda h, t, qi, kj: (h, 0, kj[t]))
    rowh = pl.BlockSpec((None, 1, S), lambda h, t, qi, kj: (h, 0, 0))
    in_specs = [qb, kb, kb, qb, qb, repq] + ([rowk] if has_dec else [])
    args = (q, k, v, do, o, lse) + ((dec_row,) if has_dec else ())
    out_specs = [qb, head, head] + ([repq, rowh] if has_dec else [])
    out_shape = [jax.ShapeDtypeStruct((S, W), dq_dtype), jax.ShapeDtypeStruct((S, W), dk_dtype), jax.ShapeDtypeStruct((S, W), BF16)]
    scratch = [pltpu.VMEM((T, LANES), F32), pltpu.VMEM((n, T, LANES), F32), pltpu.VMEM((n, T, LANES), F32),
               pltpu.VMEM((T, LANES), F32)]
    if has_dec:
        out_shape += [jax.ShapeDtypeStruct((H, S, LANES), F32), jax.ShapeDtypeStruct((H, 1, S), F32)]
        scratch += [pltpu.VMEM((T, LANES), F32), pltpu.VMEM((n, 1, T), F32)]
    res = pl.pallas_call(
        body, name=name,
        grid_spec=pltpu.PrefetchScalarGridSpec(num_scalar_prefetch=2, grid=(H, npairs), in_specs=in_specs,
                                               out_specs=tuple(out_specs), scratch_shapes=scratch),
        out_shape=tuple(out_shape),
        compiler_params=pltpu.CompilerParams(dimension_semantics=("parallel", "arbitrary"), vmem_limit_bytes=FUSED_BWD_VMEM_BYTES),
    )(qi, kj, *args)
    return res if has_dec else (res[0], res[1], res[2], None, None)


FUSED_BWD_VMEM_BYTES = 58 * 1024 * 1024


def _fa_bwd(q, k, v, o, lse, do, dec_row, *, unit, dq_dtype, dk_dtype, name):
    dq, dk, dv, dd_q, dd_k = _fa_bwd_fused(q, k, v, do, o, lse, dec_row, unit=unit, dq_dtype=dq_dtype, dk_dtype=dk_dtype, name=name)
    if dd_k is None:
        return dq, dk, dv, None
    return dq, dk, dv, jnp.max(dd_q, axis=2) + dd_k.reshape(dd_k.shape[0], dd_k.shape[2])


def _merge_fwd(ya, yb, yc, gate_logit, gate_b):
    S, D = ya.shape
    tm = min(MERGE_ROWS, S)

    def body(a_ref, b_ref, c_ref, gl_ref, gb_ref, o_ref):
        g = jax.nn.sigmoid(gl_ref[...] + gb_ref[...])
        o_ref[...] = (g[:, 0:D] * a_ref[...] + g[:, D:2 * D] * b_ref[...] + g[:, 2 * D:3 * D] * c_ref[...]).astype(BF16)

    row = pl.BlockSpec((tm, D), lambda i: (i, 0))
    return pl.pallas_call(
        body, name="merge_fwd", grid=(S // tm,),
        in_specs=[row, row, row, pl.BlockSpec((tm, 3 * D), lambda i: (i, 0)), pl.BlockSpec((1, 3 * D), lambda i: (0, 0))],
        out_specs=row, out_shape=jax.ShapeDtypeStruct((S, D), BF16), compiler_params=_cparams(("parallel",)),
    )(ya, yb, yc, gate_logit, gate_b.reshape(1, 3 * D))


def _merge_bwd(dm, ya, yb, yc, gate_logit, gate_b):
    S, D = ya.shape
    tm = min(MERGE_ROWS, S)

    def body(dm_ref, a_ref, b_ref, c_ref, gl_ref, gb_ref, da_ref, db_ref, dc_ref, dgl_ref, dgb_ref):
        g = jax.nn.sigmoid(gl_ref[...] + gb_ref[...])
        dmv = dm_ref[...]
        parts = []
        for n, (y_ref, dy_ref) in enumerate(((a_ref, da_ref), (b_ref, db_ref), (c_ref, dc_ref))):
            gn = g[:, n * D:(n + 1) * D]
            dy_ref[...] = (dmv * gn).astype(BF16)
            parts.append(dmv * y_ref[...] * gn * (1.0 - gn))
        dgl = jnp.concatenate(parts, axis=1)
        dgl_ref[...] = dgl.astype(BF16)

        @pl.when(pl.program_id(0) == 0)
        def _():
            dgb_ref[...] = jnp.zeros_like(dgb_ref)

        dgb_ref[...] += jnp.sum(dgl, axis=0, keepdims=True)

    row = pl.BlockSpec((tm, D), lambda i: (i, 0))
    wide = pl.BlockSpec((tm, 3 * D), lambda i: (i, 0))
    vec = pl.BlockSpec((1, 3 * D), lambda i: (0, 0))
    act = jax.ShapeDtypeStruct((S, D), BF16)
    da, db, dc, dgl, dgb = pl.pallas_call(
        body, name="merge_bwd", grid=(S // tm,), in_specs=[row, row, row, row, wide, vec],
        out_specs=(row, row, row, wide, vec),
        out_shape=(act, act, act, jax.ShapeDtypeStruct((S, 3 * D), BF16), jax.ShapeDtypeStruct((1, 3 * D), F32)),
        compiler_params=_cparams(("arbitrary",)),
    )(dm, ya, yb, yc, gate_logit, gate_b.reshape(1, 3 * D))
    return da, db, dc, dgl, dgb.reshape(3 * D)


def _ple_fwd(x, pre, e, g_next):
    S, D = x.shape
    tm = _rows(S)
    with_norm = g_next is not None

    def body(*refs):
        x_ref, p_ref, e_ref = refs[:3]
        xn = x_ref[...] + jax.nn.sigmoid(p_ref[...]) * e_ref[...]
        if with_norm:
            g_ref, o_ref, h_ref = refs[3:]
            rstd = lax.rsqrt(jnp.mean(xn * xn, axis=1, keepdims=True) + EPS)
            h_ref[...] = (xn * rstd * g_ref[...]).astype(BF16)
        else:
            o_ref = refs[3]
        o_ref[...] = xn

    row = pl.BlockSpec((tm, D), lambda i: (i, 0))
    xs = jax.ShapeDtypeStruct((S, D), F32)
    if not with_norm:
        return pl.pallas_call(body, name="ple_fwd_last", grid=(S // tm,), in_specs=[row, row, row], out_specs=row,
                              out_shape=xs, compiler_params=_cparams(("parallel",)))(x, pre, e), None
    return pl.pallas_call(body, name="ple_fwd", grid=(S // tm,), in_specs=[row, row, row, pl.BlockSpec((1, D), lambda i: (0, 0))],
                          out_specs=(row, row), out_shape=(xs, jax.ShapeDtypeStruct((S, D), BF16)),
                          compiler_params=_cparams(("parallel",)))(x, pre, e, g_next.reshape(1, D))


def _ple_bwd(dx, pre, e):
    S, D = dx.shape
    tm = _rows(S)

    def body(dx_ref, p_ref, e_ref, dp_ref, de_ref):
        pg = jax.nn.sigmoid(p_ref[...])
        dxv = dx_ref[...]
        dp_ref[...] = (dxv * e_ref[...] * pg * (1.0 - pg)).astype(BF16)
        de_ref[...] = (dxv * pg).astype(BF16)

    row = pl.BlockSpec((tm, D), lambda i: (i, 0))
    act = jax.ShapeDtypeStruct((S, D), BF16)
    return pl.pallas_call(body, name="ple_bwd", grid=(S // tm,), in_specs=[row, row, row], out_specs=(row, row),
                          out_shape=(act, act), compiler_params=_cparams(("parallel",)))(dx, pre, e)


def _pad_heads(w, real):
    K = w.shape[0]
    w = w.reshape(K, HEADS, real)
    return jnp.pad(w, ((0, 0), (0, 0), (0, HEAD_PAD - real))).reshape(K, HEADS * HEAD_PAD)


def _unpad_heads(w, real):
    K = w.shape[0]
    return w.reshape(K, HEADS, HEAD_PAD)[:, :, :real].reshape(K, HEADS * real)


def _pad_head_rows(w, real):
    N = w.shape[1]
    w = w.reshape(HEADS, real, N)
    return jnp.pad(w, ((0, 0), (0, HEAD_PAD - real), (0, 0))).reshape(HEADS * HEAD_PAD, N)


def _unpad_head_rows(w, real):
    N = w.shape[1]
    return w.reshape(HEADS, HEAD_PAD, N)[:, :real].reshape(HEADS * real, N)


def _block_diag(w):
    w = w.reshape(4, 2, 64, 64)
    z = jnp.zeros((4, 64, 64), w.dtype)
    top = jnp.concatenate([w[:, 0], z], axis=2)
    bot = jnp.concatenate([z, w[:, 1]], axis=2)
    return jnp.concatenate([top, bot], axis=1)


def _block_diag_t(w):
    return jnp.stack([w[:, :64, :64], w[:, 64:, 64:]], axis=1).reshape(8, 64, 64)


_IN_SPLITS = (512, 512, 384, 288, 512, 512, 512, 8, 3072)
_IN_OFF = np.concatenate([[0], np.cumsum(_IN_SPLITS)])
_KR_OFF = 64
_SEG_NAMES = ("u", "ug", "cq", "ckv", "kr", "fq", "fk", "fv", "fl", "gate")


def _in_segments(w_in):
    c = lambda n: w_in[:, int(_IN_OFF[n]):int(_IN_OFF[n + 1])]
    kv = c(3)
    kr = jnp.pad(kv[:, MLA_KV_LORA:], ((0, 0), (_KR_OFF, LANES - _KR_OFF - MLA_ROPE)))
    fl = jnp.pad(c(7), ((0, 0), (0, LANES - HEADS)))
    fq = _pad_heads(c(4), FOX_HEAD_DIM) * jnp.asarray(FOX_SCALE, w_in.dtype)
    return [c(0), c(1), c(2), kv[:, :MLA_KV_LORA], kr, fq, _pad_heads(c(5), FOX_HEAD_DIM), _pad_heads(c(6), FOX_HEAD_DIM), fl, c(8)]


def _in_unsegment(dw_p, widths):
    offs = np.concatenate([[0], np.cumsum(widths)])
    seg = [dw_p[:, int(offs[n]):int(offs[n + 1])] for n in range(len(widths))]
    u, ug, cq, ckv, kr, fq, fk, fv, fl, gate = seg
    return jnp.concatenate([
        u, ug, cq, ckv, kr[:, _KR_OFF:_KR_OFF + MLA_ROPE], _unpad_heads(fq, FOX_HEAD_DIM) * FOX_SCALE,
        _unpad_heads(fk, FOX_HEAD_DIM), _unpad_heads(fv, FOX_HEAD_DIM), fl[:, :HEADS], gate], axis=1)


def _split_wuq(wuq):
    return _pad_heads(wuq, MLA_NOPE + MLA_ROPE)


def _split_wukv(wukv):
    w = wukv.reshape(MLA_KV_LORA, HEADS, MLA_NOPE + MLA_V)
    pad = lambda t: jnp.pad(t, ((0, 0), (0, 0), (0, HEAD_PAD - t.shape[2]))).reshape(MLA_KV_LORA, HEADS * HEAD_PAD)
    return pad(w[:, :, :MLA_NOPE]), pad(w[:, :, MLA_NOPE:])


def _merge_wukv(dk_p, dv_p):
    k = dk_p.reshape(MLA_KV_LORA, HEADS, HEAD_PAD)[:, :, :MLA_NOPE]
    v = dv_p.reshape(MLA_KV_LORA, HEADS, HEAD_PAD)[:, :, :MLA_V]
    return jnp.concatenate([k, v], axis=2).reshape(MLA_KV_LORA, HEADS * (MLA_NOPE + MLA_V))


def _layer_fwd(x, h, p_i, w, g_next, tabs):
    c_q, c_k, s_lo, s_hi = tabs
    sv = {"x0": x}
    segs = _in_segments(w["w_in"])
    z = {}
    for nm, ws in zip(_SEG_NAMES, segs):
        z[nm] = _mm(h, ws, out_dtype=BF16 if nm in ("fq", "fk", "fv", "gate") else F32, bias=_ones_lane_bias() if nm == "fv" else None,
                    name="in_" + nm)
    sv.update(h=h, z=z)
    wa_bd, wx_bd = _block_diag(w["lru_wa"]).astype(BF16), _block_diag(w["lru_wx"]).astype(BF16)
    oa, xc, hs = _lru_fwd(z["u"], z["ug"], w["conv_w"], w["conv_b"], wa_bd, wx_bd, w["lru_ba"], w["lru_bx"], w["lru_lambda"])
    sv.update(oa=oa, xc=xc, hs=hs)
    qn = _rmsnorm_fwd(z["cq"], w["mla_q_norm"], "q_norm_fwd")
    kvn = _rmsnorm_fwd(z["ckv"], w["mla_kv_norm"], "kv_norm_fwd")
    wuq_p = _split_wuq(w["mla_wuq"])
    wk_p, wv_p = _split_wukv(w["mla_wukv"])
    qb = _rope_q(_mm(qn, wuq_p, name="mla_q"), c_q, s_lo, s_hi, transpose=False, out_dtype=BF16, name="rope_q")
    kb = _rope_k(_mm(kvn, wk_p, name="mla_k"), z["kr"], c_k, s_lo, s_hi)
    vb = _mm(kvn, wv_p, out_dtype=BF16, bias=_ones_lane_bias(), name="mla_v")
    ob, lse_b = _fa_fwd(qb, kb, vb, None, unit=64, name="mla_attn")
    sv.update(qn=qn, kvn=kvn, qb=qb, kb=kb, vb=vb, ob=ob, lse_b=lse_b)
    bf = jnp.pad(w["fox_bf"], (0, LANES - HEADS)).reshape(1, LANES)
    dec = _decay_fwd(z["fl"], bf)
    drow = dec[:, :HEADS].T.reshape(HEADS, 1, dec.shape[0])
    oc, lse_c = _fa_fwd(z["fq"], z["fk"], z["fv"], drow, unit=1, name="fox_attn")
    sv.update(drow=drow, oc=oc, lse_c=lse_c)
    ya = _mm(oa, w["w_br_a"], out_dtype=BF16, name="br_a")
    yb = _mm(ob, _pad_head_rows(w["w_br_b"], MLA_V), out_dtype=BF16, name="br_b")
    yc = _mm(oc, _pad_head_rows(w["w_br_c"], FOX_HEAD_DIM), out_dtype=BF16, name="br_c")
    merged = _merge_fwd(ya, yb, yc, z["gate"], w["gate_b"])
    x1, hn = _mm_res_norm(merged, w["w_o"], x, w["ffn_norm"], "w_o")
    sv.update(ya=ya, yb=yb, yc=yc, merged=merged, x1=x1)
    hf, act = _ffn_up(hn, _ffn_pair_columns(w["w_gate_up"]))
    x2, pn = _mm_res_norm(act, w["w_down"], x1, w["ple_norm"], "ffn_down")
    sv.update(hn=hn, hf=hf, act=act, x2=x2)
    pre = _mm(pn, w["w_ple_gate"], name="ple_gate")
    e = _mm(p_i, w["w_ple"], name="ple_embed")
    x3, h_next = _ple_fwd(x2, pre, e, g_next)
    sv.update(pn=pn, pre=pre, e=e, p_i=p_i)
    return x3, h_next, sv


def _layer_bwd(dx3, w, sv, tabs):
    c_q, c_k, s_lo, s_hi = tabs
    g = {}
    z = sv["z"]
    dpre, de = _ple_bwd(dx3, sv["pre"], sv["e"])
    g["w_ple"] = _mm(sv["p_i"], de, ta=True, out_dtype=BF16, name="d_w_ple")
    g["w_ple_gate"] = _mm(sv["pn"], dpre, ta=True, out_dtype=BF16, name="d_w_ple_gate")
    dx2, g["ple_norm"] = _mm_norm_bwd(dpre, w["w_ple_gate"], sv["x2"], w["ple_norm"], dx3, "ple_norm_bwd")
    g["w_down"] = _mm(sv["act"], dx2, ta=True, out_dtype=BF16, name="d_w_down")
    dhf = _ffn_down_bwd(dx2, w["w_down"], sv["hf"])
    g["w_gate_up"] = _ffn_unpair_columns(_mm(sv["hn"], dhf, ta=True, out_dtype=BF16, name="d_w_gate_up"))
    dx1, g["ffn_norm"] = _mm_norm_bwd(dhf, _ffn_pair_columns(w["w_gate_up"]), sv["x1"], w["ffn_norm"], dx2, "ffn_norm_bwd")
    g["w_o"] = _mm(sv["merged"], dx1, ta=True, out_dtype=BF16, name="d_w_o")
    dm = _mm(dx1, w["w_o"], tb=True, name="d_merged")
    dya, dyb, dyc, dgate, g["gate_b"] = _merge_bwd(dm, sv["ya"], sv["yb"], sv["yc"], z["gate"], w["gate_b"])
    wbb_p, wbc_p = _pad_head_rows(w["w_br_b"], MLA_V), _pad_head_rows(w["w_br_c"], FOX_HEAD_DIM)
    g["w_br_a"] = _mm(sv["oa"], dya, ta=True, out_dtype=BF16, name="d_w_br_a")
    g["w_br_b"] = _unpad_head_rows(_mm(sv["ob"], dyb, ta=True, out_dtype=BF16, name="d_w_br_b"), MLA_V)
    g["w_br_c"] = _unpad_head_rows(_mm(sv["oc"], dyc, ta=True, out_dtype=BF16, name="d_w_br_c"), FOX_HEAD_DIM)
    doa = _mm(dya, w["w_br_a"], tb=True, name="d_oa")
    dob = _mm(dyb, wbb_p, tb=True, out_dtype=BF16, name="d_ob")
    doc = _mm(dyc, wbc_p, tb=True, out_dtype=BF16, name="d_oc")
    dfq, dfk, dfv, d_dec = _fa_bwd(z["fq"], z["fk"], z["fv"], sv["oc"], sv["lse_c"], doc, sv["drow"],
                                   unit=1, dq_dtype=BF16, dk_dtype=BF16, name="fox_attn_bwd")
    d_dec = jnp.pad(d_dec.T, ((0, 0), (0, LANES - HEADS)))
    bf = jnp.pad(w["fox_bf"], (0, LANES - HEADS)).reshape(1, LANES)
    dfl, dbf = _decay_bwd(d_dec, z["fl"], bf)
    g["fox_bf"] = dbf[0, :HEADS]
    dqb, dkb, dvb, _ = _fa_bwd(sv["qb"], sv["kb"], sv["vb"], sv["ob"], sv["lse_b"], dob, None,
                               unit=64, dq_dtype=F32, dk_dtype=BF16, name="mla_attn_bwd")
    wuq_p = _split_wuq(w["mla_wuq"])
    wk_p, wv_p = _split_wukv(w["mla_wukv"])
    dq_pre = _rope_q(dqb, c_q, s_lo, s_hi, transpose=True, out_dtype=BF16, name="rope_q_bwd")
    dkr = _rope_k_bwd(dkb, c_k, s_lo, s_hi)
    g["mla_wuq"] = _unpad_heads(_mm(sv["qn"], dq_pre, ta=True, out_dtype=BF16, name="d_wuq"), MLA_NOPE + MLA_ROPE)
    g["mla_wukv"] = _merge_wukv(_mm(sv["kvn"], dkb, ta=True, out_dtype=BF16, name="d_wuk"), _mm(sv["kvn"], dvb, ta=True, out_dtype=BF16, name="d_wuv"))
    dqn = _mm(dq_pre, wuq_p, tb=True, name="d_qn")
    dkvn = _mm(dvb, wv_p, tb=True, res=_mm(dkb, wk_p, tb=True, name="d_kvn_k"), name="d_kvn")
    dcq, g["mla_q_norm"] = _rmsnorm_bwd(z["cq"], w["mla_q_norm"], dqn, out_dtype=BF16, name="q_norm_bwd")
    dckv, g["mla_kv_norm"] = _rmsnorm_bwd(z["ckv"], w["mla_kv_norm"], dkvn, out_dtype=BF16, name="kv_norm_bwd")
    wa_bd, wx_bd = _block_diag(w["lru_wa"]).astype(BF16), _block_diag(w["lru_wx"]).astype(BF16)
    du, dug, dcw, dcb, dba, dbx, dlam, dwa, dwx = _lru_bwd(
        doa, z["u"], z["ug"], sv["xc"], sv["hs"], w["conv_w"], wa_bd, wx_bd, w["lru_ba"], w["lru_bx"], w["lru_lambda"])
    g["conv_w"], g["conv_b"], g["lru_ba"], g["lru_bx"] = dcw, dcb[0], dba[0], dbx[0]
    g["lru_lambda"] = dlam[0] * LRU_C * jax.nn.sigmoid(-w["lru_lambda"])
    g["lru_wa"], g["lru_wx"] = _block_diag_t(dwa), _block_diag_t(dwx)
    dsegs = [du, dug, dcq, dckv, dkr, dfq, dfk, dfv, dfl, dgate]
    dz = jnp.concatenate(dsegs, axis=1)
    w_in_p = jnp.concatenate(_in_segments(w["w_in"]), axis=1)
    g["w_in"] = _in_unsegment(_mm(sv["h"], dz, ta=True, out_dtype=BF16, name="d_w_in"), [d.shape[1] for d in dsegs])
    dx0, g["mix_norm"] = _mm_norm_bwd(dz, w_in_p, sv["x0"], w["mix_norm"], dx1, "mix_norm_bwd")
    return dx0, g


_LAYER_WEIGHTS = ("mix_norm", "w_in", "gate_b", "conv_w", "conv_b", "lru_wa", "lru_ba", "lru_wx", "lru_bx", "lru_lambda",
                  "mla_q_norm", "mla_wuq", "mla_kv_norm", "mla_wukv", "fox_bf", "w_br_a", "w_br_b", "w_br_c", "w_o",
                  "ffn_norm", "w_gate_up", "w_down", "ple_norm", "w_ple_gate", "w_ple")
_BIG = ("w_in", "mla_wuq", "mla_wukv", "w_br_a", "w_br_b", "w_br_c", "w_o", "w_gate_up", "w_down", "w_ple_gate", "w_ple")
_ROW_SHARDED = ("w_o", "w_down", "w_ple_gate")
_SMALL = ("mix_norm", "gate_b", "conv_b", "lru_wa", "lru_ba", "lru_wx", "lru_bx", "lru_lambda", "mla_q_norm", "mla_kv_norm",
          "fox_bf", "ffn_norm", "ple_norm")


def _local_step(x, p, layers, final_norm, target):
    tabs = _rope_tables(x.shape[0])
    saved = []
    h = _rmsnorm_fwd(x, layers[0]["mix_norm"], "mix_norm_fwd")
    for i in range(DEPTH):
        g_next = layers[i + 1]["mix_norm"] if i + 1 < DEPTH else None
        x, h, sv = _layer_fwd(x, h, p[i], layers[i], g_next, tabs)
        saved.append(sv)
    loss, dx, d_final = _loss_head(x, final_norm, target)
    grads = [None] * DEPTH
    for i in reversed(range(DEPTH)):
        dx, grads[i] = _layer_bwd(dx, layers[i], saved[i], tabs)
    return loss, dx, grads, d_final


def _hbm():
    return pl.BlockSpec(memory_space=pltpu.HBM)


def _peers(x, y):
    return [(1 - x, y), (x, 1 - y), (1 - x, 1 - y)]


def _gather_weights(arrs, name):
    n_arr = len(arrs)

    def body(*refs):
        srcs, outs = refs[:n_arr], refs[n_arr:2 * n_arr]
        send_sems, recv_sems = refs[2 * n_arr:]
        x, y, c = lax.axis_index("x"), lax.axis_index("y"), lax.axis_index("c")
        me = 2 * x + y
        peers = _peers(x, y)

        def copy(sem, src, dst, to):
            return pltpu.make_async_remote_copy(src_ref=src, dst_ref=dst, send_sem=send_sems.at[sem], recv_sem=recv_sems.at[sem],
                                                device_id=to, device_id_type=MESH)

        started = []
        for a in range(n_arr):
            for j, (px, py) in enumerate(peers):
                cp = copy(6 * a + j, srcs[a].at[c], outs[a].at[me, c], (px, py, c))
                cp.start()
                started.append(cp)
        for a in range(n_arr):
            for j, (px, py) in enumerate(peers):
                landed = outs[a].at[2 * px + py, c]
                copy(6 * a + j, srcs[a].at[c], landed, (px, py, c)).wait_recv()
                cp = copy(6 * a + 3 + j, landed, landed, (x, y, 1 - c))
                cp.start()
                started.append(cp)
        for a in range(n_arr):
            for j, (px, py) in enumerate(peers):
                copy(6 * a + 3 + j, srcs[a].at[1 - c], outs[a].at[2 * px + py, 1 - c], (x, y, 1 - c)).wait_recv()
        for cp in started:
            cp.wait_send()

    return pl.pallas_call(
        body, name=name, in_specs=[_hbm()] * n_arr, out_specs=tuple([_hbm()] * n_arr),
        out_shape=tuple(jax.ShapeDtypeStruct((4,) + t.shape, t.dtype) for t in arrs),
        scratch_shapes=[pltpu.SemaphoreType.DMA((6 * n_arr,)), pltpu.SemaphoreType.DMA((6 * n_arr,))],
    )(*arrs)


def _pair_swap_halves(g4):
    n, R, W = g4.shape
    Rh = R // 2

    def body(src_ref, out_ref, send_sem, recv_sem):
        x, y, c = lax.axis_index("x"), lax.axis_index("y"), lax.axis_index("c")
        cp = pltpu.make_async_remote_copy(src_ref=src_ref.at[:, pl.ds((1 - c) * Rh, Rh), :], dst_ref=out_ref, send_sem=send_sem,
                                          recv_sem=recv_sem, device_id=(x, y, 1 - c), device_id_type=MESH)
        cp.start()
        cp.wait()

    return pl.pallas_call(
        body, name="grad_pair_swap", in_specs=[_hbm()], out_specs=_hbm(), out_shape=jax.ShapeDtypeStruct((n, Rh, W), g4.dtype),
        scratch_shapes=[pltpu.SemaphoreType.DMA, pltpu.SemaphoreType.DMA],
    )(g4)


def _pair_add(g4, sib, c_arr):
    n, R, W = g4.shape
    Rh = R // 2
    tr = _tile_rows(Rh)
    nb = Rh // tr

    def body(c_ref, a_ref, b_ref, o_ref):
        o_ref[...] = (a_ref[...].astype(F32) + b_ref[...].astype(F32)).astype(o_ref.dtype)

    return pl.pallas_call(
        body, name="grad_pair_add",
        grid_spec=pltpu.PrefetchScalarGridSpec(
            num_scalar_prefetch=1, grid=(n, nb),
            in_specs=[pl.BlockSpec((None, tr, W), lambda s, i, c: (s, c[0] * nb + i, 0)), pl.BlockSpec((None, tr, W), lambda s, i, c: (s, i, 0))],
            out_specs=pl.BlockSpec((None, tr, W), lambda s, i, c: (s, i, 0))),
        out_shape=jax.ShapeDtypeStruct((n, Rh, W), g4.dtype), compiler_params=_cparams(("parallel", "parallel")),
    )(c_arr, g4, sib)


def _tile_rows(n):
    for t in (512, 480, 400, 320, 256, 240, 160, 128, 80, 64, 40, 32, 16, 8):
        if n % t == 0:
            return t
    return n


def _chips_exchange(part):
    n, Rh, W = part.shape

    def body(src_ref, out_ref, send_sems, recv_sems):
        x, y, c = lax.axis_index("x"), lax.axis_index("y"), lax.axis_index("c")

        def copy(j, to):
            return pltpu.make_async_remote_copy(src_ref=src_ref.at[2 * to[0] + to[1]], dst_ref=out_ref.at[j], send_sem=send_sems.at[j],
                                                recv_sem=recv_sems.at[j], device_id=(to[0], to[1], c), device_id_type=MESH)

        cps = [copy(j, peer) for j, peer in enumerate(_peers(x, y))]
        for cp in cps:
            cp.start()
        for cp in cps:
            cp.wait()

    return pl.pallas_call(
        body, name="grad_chips_exchange", in_specs=[_hbm()], out_specs=_hbm(), out_shape=jax.ShapeDtypeStruct((3, Rh, W), part.dtype),
        scratch_shapes=[pltpu.SemaphoreType.DMA((3,)), pltpu.SemaphoreType.DMA((3,))],
    )(part)


def _chips_add(part, got, k_arr, c_arr):
    n, Rh, W = part.shape
    tr = _tile_rows(Rh)
    nb = Rh // tr

    def body(k_ref, c_ref, a_ref, b_ref, o_ref):
        mine = pl.program_id(0) == c_ref[0]

        @pl.when(mine)
        def _():
            o_ref[...] = ((a_ref[...].astype(F32) + b_ref[0].astype(F32)) + b_ref[1].astype(F32)) + b_ref[2].astype(F32)

        @pl.when(jnp.logical_not(mine))
        def _():
            o_ref[...] = jnp.zeros_like(o_ref)

    return pl.pallas_call(
        body, name="grad_chips_add",
        grid_spec=pltpu.PrefetchScalarGridSpec(
            num_scalar_prefetch=2, grid=(2, nb),
            in_specs=[pl.BlockSpec((None, tr, W), lambda h, i, k, c: (k[0], i, 0)), pl.BlockSpec((3, tr, W), lambda h, i, k, c: (0, i, 0))],
            out_specs=pl.BlockSpec((tr, W), lambda h, i, k, c: (h * nb + i, 0))),
        out_shape=jax.ShapeDtypeStruct((2 * Rh, W), F32), compiler_params=_cparams(("parallel", "parallel")),
    )(k_arr, c_arr, part, got)


def _pair_gather(buf):
    R, W = buf.shape
    Rh = R // 2

    def body(src_ref, out_ref, send_sem, recv_sem):
        x, y, c = lax.axis_index("x"), lax.axis_index("y"), lax.axis_index("c")
        mine, other = pl.ds(c * Rh, Rh), pl.ds((1 - c) * Rh, Rh)
        pltpu.make_async_remote_copy(src_ref=src_ref.at[mine], dst_ref=out_ref.at[mine], send_sem=send_sem, recv_sem=recv_sem,
                                     device_id=(x, y, 1 - c), device_id_type=MESH).start()
        pltpu.make_async_remote_copy(src_ref=src_ref.at[mine], dst_ref=out_ref.at[other], send_sem=send_sem, recv_sem=recv_sem,
                                     device_id=(x, y, 1 - c), device_id_type=MESH).wait()

    return pl.pallas_call(
        body, name="grad_pair_gather", in_specs=[_hbm()], out_specs=_hbm(), out_shape=jax.ShapeDtypeStruct((R, W), buf.dtype),
        input_output_aliases={0: 0}, scratch_shapes=[pltpu.SemaphoreType.DMA, pltpu.SemaphoreType.DMA],
    )(buf)


def _gather_all(buf):
    R, W = buf.shape

    def body(src_ref, out_ref, send_sems, recv_sems, local_sem):
        x, y, c = lax.axis_index("x"), lax.axis_index("y"), lax.axis_index("c")
        me = 4 * x + 2 * y + c
        mine = pltpu.make_async_copy(src_ref, out_ref.at[me], local_sem)
        mine.start()
        rel = [((x + (r >> 2 & 1)) % 2, (y + (r >> 1 & 1)) % 2, (c + (r & 1)) % 2) for r in range(1, 8)]

        def copy(j, slot, to):
            return pltpu.make_async_remote_copy(src_ref=src_ref, dst_ref=out_ref.at[slot], send_sem=send_sems.at[j],
                                                recv_sem=recv_sems.at[j], device_id=to, device_id_type=MESH)

        sends = [copy(j, me, to) for j, to in enumerate(rel)]
        for cp in sends:
            cp.start()
        for j, to in enumerate(rel):
            copy(j, 4 * to[0] + 2 * to[1] + to[2], to).wait_recv()
        for cp in sends:
            cp.wait_send()
        mine.wait()

    return pl.pallas_call(
        body, name="small_gather", in_specs=[_hbm()], out_specs=_hbm(), out_shape=jax.ShapeDtypeStruct((8, R, W), buf.dtype),
        scratch_shapes=[pltpu.SemaphoreType.DMA((7,)), pltpu.SemaphoreType.DMA((7,)), pltpu.SemaphoreType.DMA],
    )(buf)


def _sum_slots(stack):
    n, R, W = stack.shape
    tr = _tile_rows(R)

    def body(s_ref, o_ref):
        tot = s_ref[0]
        for j in range(1, n):
            tot = tot + s_ref[j]
        o_ref[...] = tot

    return pl.pallas_call(
        body, name="small_sum", grid=(R // tr,), in_specs=[pl.BlockSpec((n, tr, W), lambda i: (0, i, 0))],
        out_specs=pl.BlockSpec((tr, W), lambda i: (i, 0)), out_shape=jax.ShapeDtypeStruct((R, W), F32),
        compiler_params=_cparams(("parallel",)),
    )(stack)


def _adamw(wp, gp, mp, vp, name):
    R, W = wp.shape
    tr = R
    for t in (1024, 512, 256, 128, 64, 32, 16, 8):
        if R % t == 0 and t * W <= 512 * 1024:
            tr = t
            break
    c1 = 1.0 - ADAM_B1 ** ADAM_STEP
    c2 = 1.0 - ADAM_B2 ** ADAM_STEP

    def body(w_ref, g_ref, m_ref, v_ref, d_ref, mo_ref, vo_ref):
        gv = g_ref[...]
        m = ADAM_B1 * m_ref[...] + (1.0 - ADAM_B1) * gv
        v = ADAM_B2 * v_ref[...] + (1.0 - ADAM_B2) * (gv * gv)
        m_hat = m / c1
        v_hat = v / c2
        d_ref[...] = -ADAM_LR * (m_hat / (jnp.sqrt(v_hat) + ADAM_EPS) + ADAM_WD * w_ref[...])
        mo_ref[...] = m
        vo_ref[...] = v

    blk = pl.BlockSpec((tr, W), lambda i: (i, 0))
    shp = jax.ShapeDtypeStruct((R, W), F32)
    return pl.pallas_call(body, name=name, grid=(R // tr,), in_specs=[blk] * 4, out_specs=(blk,) * 3, out_shape=(shp,) * 3,
                          compiler_params=_cparams(("parallel",)))(wp, gp, mp, vp)


def _pack(arrs, rows):
    flat = jnp.concatenate([a.reshape(-1) for a in arrs])
    return jnp.pad(flat, (0, rows * PACK_W - flat.shape[0])).reshape(rows, PACK_W)


def _unpack(buf, shapes):
    flat = buf.reshape(-1)
    out, off = [], 0
    for shp in shapes:
        n = int(np.prod(shp))
        out.append(flat[off:off + n].reshape(shp))
        off += n
    return out


def _rows_for(shapes, mult):
    n = sum(int(np.prod(s)) for s in shapes)
    rows = -(-n // PACK_W)
    return -(-rows // mult) * mult


def _shard_major(g, name):
    L, K, N = g.shape
    if name in _ROW_SHARDED:
        t = g.reshape(L, 4, K // 4, N).transpose(1, 0, 2, 3)
    else:
        t = g.reshape(L, K, 4, N // 4).transpose(2, 0, 1, 3)
    return t.reshape(4, -1, PACK_W)


def _join_shards(blocks, name):
    return jnp.concatenate(blocks, axis=1 if name in _ROW_SHARDED else 2)


def kernel(x, p, mix_norm, w_in, gate_b, conv_w, conv_b, lru_wa, lru_ba, lru_wx, lru_bx, lru_lambda, mla_q_norm, mla_wuq, mla_kv_norm, mla_wukv, fox_bf, w_br_a, w_br_b, w_br_c, w_o, ffn_norm, w_gate_up, w_down, ple_norm, w_ple_gate, w_ple, final_norm, loss_target, m_mix_norm, m_w_in, m_gate_b, m_conv_w, m_conv_b, m_lru_wa, m_lru_ba, m_lru_wx, m_lru_bx, m_lru_lambda, m_mla_q_norm, m_mla_wuq, m_mla_kv_norm, m_mla_wukv, m_fox_bf, m_w_br_a, m_w_br_b, m_w_br_c, m_w_o, m_ffn_norm, m_w_gate_up, m_w_down, m_ple_norm, m_w_ple_gate, m_w_ple, m_final_norm, v_mix_norm, v_w_in, v_gate_b, v_conv_w, v_conv_b, v_lru_wa, v_lru_ba, v_lru_wx, v_lru_bx, v_lru_lambda, v_mla_q_norm, v_mla_wuq, v_mla_kv_norm, v_mla_wukv, v_fox_bf, v_w_br_a, v_w_br_b, v_w_br_c, v_w_o, v_ffn_norm, v_w_gate_up, v_w_down, v_ple_norm, v_w_ple_gate, v_w_ple, v_final_norm):
    a = dict(locals())
    names = list(_LAYER_WEIGHTS) + ["final_norm"]
    W = {n: a[n] for n in names}
    M = {n: a["m_" + n] for n in names}
    V = {n: a["v_" + n] for n in names}
    ix, iy, ic = lax.axis_index("x"), lax.axis_index("y"), lax.axis_index("c")

    sharded = list(_BIG) + ["conv_w"]
    shard_shapes = [W[n].shape for n in sharded]
    R = _rows_for(shard_shapes, 64)
    mine = [W[n].astype(BF16) for n in _BIG] + [conv_w]
    gathered = _gather_weights(mine, "weight_gather")
    me = 2 * ix + iy
    gathered = [lax.dynamic_update_slice(g, t[None], (me,) + (0,) * t.ndim) for g, t in zip(gathered, mine)]
    full = {n: _join_shards([g[k] for k in range(4)], n) for n, g in zip(sharded, gathered)}
    conv_w_full = full["conv_w"]
    layers = []
    for i in range(DEPTH):
        lw = {n: W[n][i] for n in _SMALL}
        for n in _BIG:
            lw[n] = full[n][i]
        lw["conv_w"] = conv_w_full[i]
        layers.append(lw)

    loss_sum, dx, grads, d_final = _local_step(x[0], p[:, 0], layers, final_norm, loss_target[0])
    loss = lax.psum(loss_sum, ("x", "y", "c"))

    parts = [_shard_major(jnp.stack([grads[i][n] for i in range(DEPTH)]), n).astype(BF16) for n in sharded]
    g4, off = jnp.zeros((4, R, PACK_W), BF16), 0
    for t in parts:
        g4 = lax.dynamic_update_slice(g4, t, (0, off, 0))
        off += t.shape[1]
    c_arr = jnp.reshape(ic, (1,)).astype(jnp.int32)
    k_arr = jnp.reshape(2 * ix + iy, (1,)).astype(jnp.int32)
    pair = _pair_add(g4, _pair_swap_halves(g4), c_arr)
    g_pack = _pair_gather(_chips_add(pair, _chips_exchange(pair), k_arr, c_arr))
    big_out = {}
    for n, gsh in zip(sharded, _unpack(g_pack, shard_shapes)):
        view = lambda t: t.reshape(-1, t.shape[-1])
        d, nm, nv = _adamw(view(W[n]), view(gsh), view(M[n]), view(V[n]), "adamw_" + n)
        for key, arr in (("g", gsh), ("d", d), ("m", nm), ("v", nv)):
            big_out[(key, n)] = arr.reshape(W[n].shape)

    small = list(_SMALL) + ["final_norm"]
    small_shapes = [W[n].shape for n in small]
    Rs = _rows_for(small_shapes, 8)
    mine_small = [d_final if n == "final_norm" else jnp.stack([grads[i][n] for i in range(DEPTH)]) for n in small]
    sg = _sum_slots(_gather_all(_pack(mine_small, Rs)))
    small_out = {}
    for n, gsm in zip(small, _unpack(sg, small_shapes)):
        view = lambda t: t.reshape(-1, t.shape[-1])
        d, nm, nv = _adamw(view(W[n]), view(gsm), view(M[n]), view(V[n]), "adamw_" + n)
        for key, arr in (("g", gsm), ("d", d), ("m", nm), ("v", nv)):
            small_out[(key, n)] = arr.reshape(W[n].shape)

    def assemble(key, n):
        return big_out[(key, n)] if n in sharded else small_out[(key, n)]

    outs = [loss, dx[None]]
    for key in ("g", "d", "m", "v"):
        outs += [assemble(key, n) for n in names]
    return tuple(outs)
```
